```python
import math
import jax, jax.numpy as jnp
from jax import lax
import numpy as np

D_MODEL = 2048
BATCH = 8
SEQ = 8192
DEPTH = 2

CHUNK = 64
N_MIXERS = 2
N_RET_LAYERS = (DEPTH + 1) // 2
N_SSD_LAYERS = DEPTH // 2
RMS_EPS = 1e-6

RET_HEAD_DK = 256
RET_HEADS = D_MODEL // RET_HEAD_DK
RET_QK = RET_HEADS * RET_HEAD_DK
RET_HEAD_DV = 2 * RET_HEAD_DK
RET_V = RET_HEADS * RET_HEAD_DV
RET_IN = 2 * RET_QK + 2 * RET_V
ROPE_BASE = 10000.0
GN_EPS = 1e-5

SSD_EXPAND = 2
SSD_D_INNER = SSD_EXPAND * D_MODEL
SSD_HEADDIM = 64
SSD_HEADS = SSD_D_INNER // SSD_HEADDIM
SSD_GROUPS = 8
SSD_HEADS_PER_GROUP = SSD_HEADS // SSD_GROUPS
SSD_STATE = 128
SSD_CONV_W = 4
SSD_CONV_DIM = SSD_D_INNER + 2 * SSD_GROUPS * SSD_STATE
SSD_IN = SSD_D_INNER + SSD_CONV_DIM + SSD_HEADS
SSD_NORM_GROUPS = SSD_GROUPS

D_FF = 4 * D_MODEL

kernel_name = "hybrid_retention_ssd_sandwich_trunk"


def rms_norm(x, w):
    xf = x.astype(jnp.float32)
    y = xf * lax.rsqrt(jnp.mean(xf * xf, axis=-1, keepdims=True) + RMS_EPS)
    return (y * w.astype(jnp.float32)).astype(x.dtype)


def to_chunks(t):
    b, s = t.shape[:2]
    return jnp.moveaxis(t.reshape(b, s // CHUNK, CHUNK, *t.shape[2:]), 1, 0)


def from_chunks(t):
    t = jnp.moveaxis(t, 0, 1)
    return t.reshape(t.shape[0], t.shape[1] * t.shape[2], *t.shape[3:])


def rotary(t, pos):
    half = t.shape[-1] // 2
    inv_freq = ROPE_BASE ** (-jnp.arange(half, dtype=jnp.float32) / half)
    ang = pos.astype(jnp.float32)[:, None] * inv_freq[None, :]
    cos = jnp.cos(ang)[None, :, None, :]
    sin = jnp.sin(ang)[None, :, None, :]
    t1, t2 = t[..., :half], t[..., half:]
    return jnp.concatenate([t1 * cos - t2 * sin, t1 * sin + t2 * cos], axis=-1)


def retention_mixer(u, w_in, gn_w, w_out):
    b, s, _ = u.shape
    f32 = jnp.float32
    proj = u @ w_in
    q, k, v, g = jnp.split(proj, [RET_QK, 2 * RET_QK, 2 * RET_QK + RET_V], axis=-1)
    pos = jnp.arange(s)
    q = rotary(q.astype(f32).reshape(b, s, RET_HEADS, RET_HEAD_DK), pos)
    k = rotary(k.astype(f32).reshape(b, s, RET_HEADS, RET_HEAD_DK), pos) * (RET_HEAD_DK ** -0.5)
    v = v.astype(f32).reshape(b, s, RET_HEADS, RET_HEAD_DV)

    log_gamma = jnp.log1p(-jnp.exp2(-5.0 - jnp.arange(RET_HEADS, dtype=f32)))
    idx = jnp.arange(CHUNK, dtype=f32)
    dist = jnp.abs(idx[:, None] - idx[None, :])
    dmat = jnp.exp(dist[None] * log_gamma[:, None, None])
    xi = jnp.exp((idx[:, None] + 1.0) * log_gamma[None, :])
    zeta = jnp.exp((CHUNK - 1.0 - idx)[:, None] * log_gamma[None, :])
    chunk_decay = jnp.exp(CHUNK * log_gamma)

    def step(state, inp):
        qc, kc, vc = inp
        scores = jnp.einsum('blhd,bmhd->bhlm', qc, kc) * dmat[None]
        inner = jnp.einsum('bhlm,bmhv->blhv', scores, vc)
        cross = jnp.einsum('blhd,bhdv->blhv', qc, state) * xi[None, :, :, None]
        state = state * chunk_decay[None, :, None, None] + jnp.einsum(
            'bmhd,bmhv->bhdv', kc * zeta[None, :, :, None], vc)
        return state, inner + cross

    state0 = jnp.zeros((b, RET_HEADS, RET_HEAD_DK, RET_HEAD_DV), f32)
    _, o = lax.scan(step, state0, (to_chunks(q), to_chunks(k), to_chunks(v)))
    o = from_chunks(o)
    mu = jnp.mean(o, axis=-1, keepdims=True)
    var = jnp.mean(jnp.square(o - mu), axis=-1, keepdims=True)
    o = ((o - mu) * lax.rsqrt(var + GN_EPS)).reshape(b, s, RET_V) * gn_w.astype(f32)
    y = jax.nn.silu(g.astype(f32)) * o
    return y.astype(u.dtype) @ w_out


def causal_depthwise_conv(t, w, bias):
    s = t.shape[1]
    width = w.shape[0]
    tp = jnp.pad(t, ((0, 0), (width - 1, 0), (0, 0)))
    out = bias[None, None, :]
    for tap in range(width):
        out = out + tp[:, tap:tap + s, :] * w[tap][None, None, :]
    return out


def ssd_mixer(u, w_in, conv_w, conv_b, dt_bias, a_log, d_skip, norm_w, w_out):
    b, s, _ = u.shape
    f32 = jnp.float32
    G, R, P, N = SSD_GROUPS, SSD_HEADS_PER_GROUP, SSD_HEADDIM, SSD_STATE
    proj = u @ w_in
    z, xbc, dt = jnp.split(proj, [SSD_D_INNER, SSD_D_INNER + SSD_CONV_DIM], axis=-1)
    xbc = jax.nn.silu(causal_depthwise_conv(xbc, conv_w, conv_b))
    xs, bm, cm = jnp.split(xbc, [SSD_D_INNER, SSD_D_INNER + G * N], axis=-1)
    xs = xs.astype(f32).reshape(b, s, G, R, P)
    bm = bm.astype(f32).reshape(b, s, G, N)
    cm = cm.astype(f32).reshape(b, s, G, N)
    dt = jax.nn.softplus(dt.astype(f32) + dt_bias.astype(f32)).reshape(b, s, G, R)
    a = -jnp.exp(a_log.astype(f32)).reshape(G, R)
    da = dt * a[None, None]
    xdt = xs * dt[..., None]
    tril = jnp.tril(jnp.ones((CHUNK, CHUNK), dtype=bool))

    def step(state, inp):
        xc, ac, bc, cc = inp
        acum = jnp.cumsum(ac, axis=1)
        seg = acum[:, :, None] - acum[:, None, :]
        lmat = jnp.exp(jnp.where(tril[None, :, :, None, None], seg, -jnp.inf))
        cb = jnp.einsum('blgn,bsgn->blsg', cc, bc)
        y_diag = jnp.einsum('blsgr,bsgrp->blgrp', cb[..., None] * lmat, xc)
        y_off = jnp.einsum('blgn,bgrpn->blgrp', cc, state) * jnp.exp(acum)[..., None]
        decay_to_end = jnp.exp(acum[:, -1:] - acum)
        state = state * jnp.exp(acum[:, -1])[..., None, None] + jnp.einsum(
            'bsgn,bsgrp->bgrpn', bc, xc * decay_to_end[..., None])
        return state, y_diag + y_off

    state0 = jnp.zeros((b, G, R, P, N), f32)
    _, ys = lax.scan(step, state0, (to_chunks(xdt), to_chunks(da), to_chunks(bm), to_chunks(cm)))
    y = from_chunks(ys) + d_skip.astype(f32).reshape(G, R)[None, None, :, :, None] * xs
    y = y.reshape(b, s, SSD_D_INNER) * jax.nn.silu(z.astype(f32))
    yg = y.reshape(b, s, SSD_NORM_GROUPS, SSD_D_INNER // SSD_NORM_GROUPS)
    yg = yg * lax.rsqrt(jnp.mean(yg * yg, axis=-1, keepdims=True) + RMS_EPS)
    y = yg.reshape(b, s, SSD_D_INNER) * norm_w.astype(f32)
    return y.astype(u.dtype) @ w_out


def sq_relu_mlp(u, w_up, w_down):
    h = jax.nn.relu(u @ w_up)
    return (h * h) @ w_down


def _fwd_setup_inputs(seed: int = 0) -> dict:
    key = jax.random.key(seed)
    ks = jax.random.split(key, 20)
    f32 = jnp.float32

    def nrm(k, shape, scale):
        return jax.random.normal(k, shape, f32) * scale

    def gain(k, shape):
        return 1.0 + 0.05 * jax.random.normal(k, shape, f32)

    x = jax.random.normal(ks[0], (BATCH, SEQ, D_MODEL), f32)
    norm_mix_pre = gain(ks[1], (DEPTH, D_MODEL))
    norm_mix_post = gain(ks[2], (DEPTH, D_MODEL))
    norm_ffn_pre = gain(ks[3], (DEPTH, D_MODEL))
    norm_ffn_post = gain(ks[4], (DEPTH, D_MODEL))

    ret_w_in = nrm(ks[5], (N_RET_LAYERS, D_MODEL, RET_IN), D_MODEL ** -0.5)
    ret_gn_w = gain(ks[6], (N_RET_LAYERS, RET_V))
    ret_w_out = nrm(ks[7], (N_RET_LAYERS, RET_V, D_MODEL), RET_V ** -0.5)

    ssd_w_in = nrm(ks[8], (N_SSD_LAYERS, D_MODEL, SSD_IN), D_MODEL ** -0.5)
    ssd_conv_w = nrm(ks[9], (N_SSD_LAYERS, SSD_CONV_W, SSD_CONV_DIM), SSD_CONV_W ** -0.5)
    ssd_conv_b = nrm(ks[10], (N_SSD_LAYERS, SSD_CONV_DIM), 0.02)
    dt0 = jnp.exp(jax.random.uniform(ks[11], (N_SSD_LAYERS, SSD_HEADS), f32,
                                     math.log(1e-3), math.log(1e-1)))
    ssd_dt_bias = dt0 + jnp.log(-jnp.expm1(-dt0))
    ssd_a_log = jnp.log(jax.random.uniform(ks[12], (N_SSD_LAYERS, SSD_HEADS), f32, 1.0, 16.0))
    ssd_d = gain(ks[13], (N_SSD_LAYERS, SSD_HEADS))
    ssd_norm_w = gain(ks[14], (N_SSD_LAYERS, SSD_D_INNER))
    ssd_w_out = nrm(ks[15], (N_SSD_LAYERS, SSD_D_INNER, D_MODEL), SSD_D_INNER ** -0.5)

    mlp_w_up = nrm(ks[16], (DEPTH, D_MODEL, D_FF), D_MODEL ** -0.5)
    mlp_w_down = nrm(ks[17], (DEPTH, D_FF, D_MODEL), D_FF ** -0.5)

    return {"x": x,
            "norm_mix_pre": norm_mix_pre, "norm_mix_post": norm_mix_post,
            "norm_ffn_pre": norm_ffn_pre, "norm_ffn_post": norm_ffn_post,
            "ret_w_in": ret_w_in, "ret_gn_w": ret_gn_w, "ret_w_out": ret_w_out,
            "ssd_w_in": ssd_w_in, "ssd_conv_w": ssd_conv_w, "ssd_conv_b": ssd_conv_b,
            "ssd_dt_bias": ssd_dt_bias, "ssd_a_log": ssd_a_log, "ssd_d": ssd_d,
            "ssd_norm_w": ssd_norm_w, "ssd_w_out": ssd_w_out,
            "mlp_w_up": mlp_w_up, "mlp_w_down": mlp_w_down}


def _fwd_reference(x, norm_mix_pre, norm_mix_post, norm_ffn_pre, norm_ffn_post,
              ret_w_in, ret_gn_w, ret_w_out,
              ssd_w_in, ssd_conv_w, ssd_conv_b, ssd_dt_bias, ssd_a_log, ssd_d,
              ssd_norm_w, ssd_w_out, mlp_w_up, mlp_w_down):
    h = x
    for i in range(DEPTH):
        j = i // N_MIXERS
        u = rms_norm(h, norm_mix_pre[i])
        if i % N_MIXERS == 0:
            m = retention_mixer(u, ret_w_in[j], ret_gn_w[j], ret_w_out[j])
        else:
            m = ssd_mixer(u, ssd_w_in[j], ssd_conv_w[j], ssd_conv_b[j], ssd_dt_bias[j],
                          ssd_a_log[j], ssd_d[j], ssd_norm_w[j], ssd_w_out[j])
        h = h + rms_norm(m, norm_mix_post[i])
        u = rms_norm(h, norm_ffn_pre[i])
        h = h + rms_norm(sq_relu_mlp(u, mlp_w_up[i], mlp_w_down[i]), norm_ffn_post[i])
    return h


import jax as _jax
import jax.numpy as _jnp

TWIN_FORMAT = 'train_step'
FWD_PARAMS = ['x', 'norm_mix_pre', 'norm_mix_post', 'norm_ffn_pre', 'norm_ffn_post', 'ret_w_in', 'ret_gn_w', 'ret_w_out', 'ssd_w_in', 'ssd_conv_w', 'ssd_conv_b', 'ssd_dt_bias', 'ssd_a_log', 'ssd_d', 'ssd_norm_w', 'ssd_w_out', 'mlp_w_up', 'mlp_w_down']
TWIN_WEIGHTS = ['norm_mix_pre', 'norm_mix_post', 'norm_ffn_pre', 'norm_ffn_post', 'ret_w_in', 'ret_gn_w', 'ret_w_out', 'ssd_w_in', 'ssd_conv_w', 'ssd_conv_b', 'ssd_dt_bias', 'ssd_a_log', 'ssd_d', 'ssd_norm_w', 'ssd_w_out', 'mlp_w_up', 'mlp_w_down']
TWIN_DIFF_INPUT = 'x'
TWIN_INPUTS = ['x', 'norm_mix_pre', 'norm_mix_post', 'norm_ffn_pre', 'norm_ffn_post', 'ret_w_in', 'ret_gn_w', 'ret_w_out', 'ssd_w_in', 'ssd_conv_w', 'ssd_conv_b', 'ssd_dt_bias', 'ssd_a_log', 'ssd_d', 'ssd_norm_w', 'ssd_w_out', 'mlp_w_up', 'mlp_w_down', 'loss_target', 'm_norm_mix_pre', 'm_norm_mix_post', 'm_norm_ffn_pre', 'm_norm_ffn_post', 'm_ret_w_in', 'm_ret_gn_w', 'm_ret_w_out', 'm_ssd_w_in', 'm_ssd_conv_w', 'm_ssd_conv_b', 'm_ssd_dt_bias', 'm_ssd_a_log', 'm_ssd_d', 'm_ssd_norm_w', 'm_ssd_w_out', 'm_mlp_w_up', 'm_mlp_w_down', 'v_norm_mix_pre', 'v_norm_mix_post', 'v_norm_ffn_pre', 'v_norm_ffn_post', 'v_ret_w_in', 'v_ret_gn_w', 'v_ret_w_out', 'v_ssd_w_in', 'v_ssd_conv_w', 'v_ssd_conv_b', 'v_ssd_dt_bias', 'v_ssd_a_log', 'v_ssd_d', 'v_ssd_norm_w', 'v_ssd_w_out', 'v_mlp_w_up', 'v_mlp_w_down']
TWIN_OUTPUTS = ['loss', 'grad_x', 'grad_norm_mix_pre', 'grad_norm_mix_post', 'grad_norm_ffn_pre', 'grad_norm_ffn_post', 'grad_ret_w_in', 'grad_ret_gn_w', 'grad_ret_w_out', 'grad_ssd_w_in', 'grad_ssd_conv_w', 'grad_ssd_conv_b', 'grad_ssd_dt_bias', 'grad_ssd_a_log', 'grad_ssd_d', 'grad_ssd_norm_w', 'grad_ssd_w_out', 'grad_mlp_w_up', 'grad_mlp_w_down', 'delta_norm_mix_pre', 'delta_norm_mix_post', 'delta_norm_ffn_pre', 'delta_norm_ffn_post', 'delta_ret_w_in', 'delta_ret_gn_w', 'delta_ret_w_out', 'delta_ssd_w_in', 'delta_ssd_conv_w', 'delta_ssd_conv_b', 'delta_ssd_dt_bias', 'delta_ssd_a_log', 'delta_ssd_d', 'delta_ssd_norm_w', 'delta_ssd_w_out', 'delta_mlp_w_up', 'delta_mlp_w_down', 'new_m_norm_mix_pre', 'new_m_norm_mix_post', 'new_m_norm_ffn_pre', 'new_m_norm_ffn_post', 'new_m_ret_w_in', 'new_m_ret_gn_w', 'new_m_ret_w_out', 'new_m_ssd_w_in', 'new_m_ssd_conv_w', 'new_m_ssd_conv_b', 'new_m_ssd_dt_bias', 'new_m_ssd_a_log', 'new_m_ssd_d', 'new_m_ssd_norm_w', 'new_m_ssd_w_out', 'new_m_mlp_w_up', 'new_m_mlp_w_down', 'new_v_norm_mix_pre', 'new_v_norm_mix_post', 'new_v_norm_ffn_pre', 'new_v_norm_ffn_post', 'new_v_ret_w_in', 'new_v_ret_gn_w', 'new_v_ret_w_out', 'new_v_ssd_w_in', 'new_v_ssd_conv_w', 'new_v_ssd_conv_b', 'new_v_ssd_dt_bias', 'new_v_ssd_a_log', 'new_v_ssd_d', 'new_v_ssd_norm_w', 'new_v_ssd_w_out', 'new_v_mlp_w_up', 'new_v_mlp_w_down']
TWIN_LEAF_KINDS = {'loss': 'loss', 'grad_x': 'grad_x', 'grad_norm_mix_pre': 'grad_w', 'grad_norm_mix_post': 'grad_w', 'grad_norm_ffn_pre': 'grad_w', 'grad_norm_ffn_post': 'grad_w', 'grad_ret_w_in': 'grad_w', 'grad_ret_gn_w': 'grad_w', 'grad_ret_w_out': 'grad_w', 'grad_ssd_w_in': 'grad_w', 'grad_ssd_conv_w': 'grad_w', 'grad_ssd_conv_b': 'grad_w', 'grad_ssd_dt_bias': 'grad_w', 'grad_ssd_a_log': 'grad_w', 'grad_ssd_d': 'grad_w', 'grad_ssd_norm_w': 'grad_w', 'grad_ssd_w_out': 'grad_w', 'grad_mlp_w_up': 'grad_w', 'grad_mlp_w_down': 'grad_w', 'delta_norm_mix_pre': 'delta_w', 'delta_norm_mix_post': 'delta_w', 'delta_norm_ffn_pre': 'delta_w', 'delta_norm_ffn_post': 'delta_w', 'delta_ret_w_in': 'delta_w', 'delta_ret_gn_w': 'delta_w', 'delta_ret_w_out': 'delta_w', 'delta_ssd_w_in': 'delta_w', 'delta_ssd_conv_w': 'delta_w', 'delta_ssd_conv_b': 'delta_w', 'delta_ssd_dt_bias': 'delta_w', 'delta_ssd_a_log': 'delta_w', 'delta_ssd_d': 'delta_w', 'delta_ssd_norm_w': 'delta_w', 'delta_ssd_w_out': 'delta_w', 'delta_mlp_w_up': 'delta_w', 'delta_mlp_w_down': 'delta_w', 'new_m_norm_mix_pre': 'new_m', 'new_m_norm_mix_post': 'new_m', 'new_m_norm_ffn_pre': 'new_m', 'new_m_norm_ffn_post': 'new_m', 'new_m_ret_w_in': 'new_m', 'new_m_ret_gn_w': 'new_m', 'new_m_ret_w_out': 'new_m', 'new_m_ssd_w_in': 'new_m', 'new_m_ssd_conv_w': 'new_m', 'new_m_ssd_conv_b': 'new_m', 'new_m_ssd_dt_bias': 'new_m', 'new_m_ssd_a_log': 'new_m', 'new_m_ssd_d': 'new_m', 'new_m_ssd_norm_w': 'new_m', 'new_m_ssd_w_out': 'new_m', 'new_m_mlp_w_up': 'new_m', 'new_m_mlp_w_down': 'new_m', 'new_v_norm_mix_pre': 'new_v', 'new_v_norm_mix_post': 'new_v', 'new_v_norm_ffn_pre': 'new_v', 'new_v_norm_ffn_post': 'new_v', 'new_v_ret_w_in': 'new_v', 'new_v_ret_gn_w': 'new_v', 'new_v_ret_w_out': 'new_v', 'new_v_ssd_w_in': 'new_v', 'new_v_ssd_conv_w': 'new_v', 'new_v_ssd_conv_b': 'new_v', 'new_v_ssd_dt_bias': 'new_v', 'new_v_ssd_a_log': 'new_v', 'new_v_ssd_d': 'new_v', 'new_v_ssd_norm_w': 'new_v', 'new_v_ssd_w_out': 'new_v', 'new_v_mlp_w_up': 'new_v', 'new_v_mlp_w_down': 'new_v'}


def _forward(args):
    return _fwd_reference(*[args[k] for k in FWD_PARAMS])


def _output_shape():
    def fwd():
        inp = _fwd_setup_inputs(0)
        return _fwd_reference(*[inp[k] for k in FWD_PARAMS])
    out = _jax.eval_shape(fwd)
    return out.shape, out.dtype

N_MICROBATCH = 1
ADAM_LR = 0.001
ADAM_B1 = 0.9
ADAM_B2 = 0.999
ADAM_EPS = 1e-08
ADAM_WD = 0.01
ADAM_STEP = 10
PER_EXAMPLE_BATCH_AXIS = {'x': 0, 'loss_target': 0}
SHARED_INPUTS = []
_WEIGHT_DTYPES = {'norm_mix_pre': _jnp.float32, 'norm_mix_post': _jnp.float32, 'norm_ffn_pre': _jnp.float32, 'norm_ffn_post': _jnp.float32, 'ret_w_in': _jnp.float32, 'ret_gn_w': _jnp.float32, 'ret_w_out': _jnp.float32, 'ssd_w_in': _jnp.float32, 'ssd_conv_w': _jnp.float32, 'ssd_conv_b': _jnp.float32, 'ssd_dt_bias': _jnp.float32, 'ssd_a_log': _jnp.float32, 'ssd_d': _jnp.float32, 'ssd_norm_w': _jnp.float32, 'ssd_w_out': _jnp.float32, 'mlp_w_up': _jnp.float32, 'mlp_w_down': _jnp.float32}
MOMENT_SCALE = {'norm_mix_pre': 1.847504e+00, 'norm_mix_post': 3.216691e+01, 'norm_ffn_pre': 1.560305e+00, 'norm_ffn_post': 3.408270e+01, 'ret_w_in': 4.593508e-01, 'ret_gn_w': 4.990358e-01, 'ret_w_out': 6.312967e-01, 'ssd_w_in': 1.346247e+00, 'ssd_conv_w': 2.210550e+00, 'ssd_conv_b': 5.884572e+00, 'ssd_dt_bias': 8.190701e-01, 'ssd_a_log': 1.179114e+01, 'ssd_d': 1.215334e+01, 'ssd_norm_w': 3.491491e+00, 'ssd_w_out': 4.856161e+00, 'mlp_w_up': 7.808098e-01, 'mlp_w_down': 7.113133e+00}


def _to_microbatches(a, axis):
    t = _jnp.moveaxis(a, axis, 0)
    t = t.reshape((N_MICROBATCH, t.shape[0] // N_MICROBATCH) + t.shape[1:])
    return _jnp.moveaxis(t, 1, axis + 1)


def setup_inputs(seed: int = 0) -> dict:
    inp = _fwd_setup_inputs(seed)
    key = _jax.random.fold_in(_jax.random.key(seed), 7919)
    shape, _ = _output_shape()
    out = dict(inp)
    out["loss_target"] = _jax.random.normal(_jax.random.fold_in(key, 0), shape, _jnp.float32)
    for i, name in enumerate(TWIN_WEIGHTS):
        w = inp[name].astype(_jnp.float32)
        if MOMENT_SCALE is None:
            s = _jnp.sqrt(_jnp.mean(_jnp.square(w)) + 1e-30)
        else:
            s = MOMENT_SCALE[name]
        km, kv = _jax.random.split(_jax.random.fold_in(key, i + 1))
        out[name] = w
        out["m_" + name] = s * _jax.random.normal(km, w.shape, _jnp.float32)
        out["v_" + name] = (s * s) * _jax.random.uniform(kv, w.shape, _jnp.float32, 0.5, 1.5)
    if N_MICROBATCH > 1:
        for name, axis in PER_EXAMPLE_BATCH_AXIS.items():
            out[name] = _to_microbatches(out[name], axis)
    return {'x': out['x'], 'norm_mix_pre': out['norm_mix_pre'], 'norm_mix_post': out['norm_mix_post'], 'norm_ffn_pre': out['norm_ffn_pre'], 'norm_ffn_post': out['norm_ffn_post'], 'ret_w_in': out['ret_w_in'], 'ret_gn_w': out['ret_gn_w'], 'ret_w_out': out['ret_w_out'], 'ssd_w_in': out['ssd_w_in'], 'ssd_conv_w': out['ssd_conv_w'], 'ssd_conv_b': out['ssd_conv_b'], 'ssd_dt_bias': out['ssd_dt_bias'], 'ssd_a_log': out['ssd_a_log'], 'ssd_d': out['ssd_d'], 'ssd_norm_w': out['ssd_norm_w'], 'ssd_w_out': out['ssd_w_out'], 'mlp_w_up': out['mlp_w_up'], 'mlp_w_down': out['mlp_w_down'], 'loss_target': out['loss_target'], 'm_norm_mix_pre': out['m_norm_mix_pre'], 'm_norm_mix_post': out['m_norm_mix_post'], 'm_norm_ffn_pre': out['m_norm_ffn_pre'], 'm_norm_ffn_post': out['m_norm_ffn_post'], 'm_ret_w_in': out['m_ret_w_in'], 'm_ret_gn_w': out['m_ret_gn_w'], 'm_ret_w_out': out['m_ret_w_out'], 'm_ssd_w_in': out['m_ssd_w_in'], 'm_ssd_conv_w': out['m_ssd_conv_w'], 'm_ssd_conv_b': out['m_ssd_conv_b'], 'm_ssd_dt_bias': out['m_ssd_dt_bias'], 'm_ssd_a_log': out['m_ssd_a_log'], 'm_ssd_d': out['m_ssd_d'], 'm_ssd_norm_w': out['m_ssd_norm_w'], 'm_ssd_w_out': out['m_ssd_w_out'], 'm_mlp_w_up': out['m_mlp_w_up'], 'm_mlp_w_down': out['m_mlp_w_down'], 'v_norm_mix_pre': out['v_norm_mix_pre'], 'v_norm_mix_post': out['v_norm_mix_post'], 'v_norm_ffn_pre': out['v_norm_ffn_pre'], 'v_norm_ffn_post': out['v_norm_ffn_post'], 'v_ret_w_in': out['v_ret_w_in'], 'v_ret_gn_w': out['v_ret_gn_w'], 'v_ret_w_out': out['v_ret_w_out'], 'v_ssd_w_in': out['v_ssd_w_in'], 'v_ssd_conv_w': out['v_ssd_conv_w'], 'v_ssd_conv_b': out['v_ssd_conv_b'], 'v_ssd_dt_bias': out['v_ssd_dt_bias'], 'v_ssd_a_log': out['v_ssd_a_log'], 'v_ssd_d': out['v_ssd_d'], 'v_ssd_norm_w': out['v_ssd_norm_w'], 'v_ssd_w_out': out['v_ssd_w_out'], 'v_mlp_w_up': out['v_mlp_w_up'], 'v_mlp_w_down': out['v_mlp_w_down']}


def _loss(weights, diff, rest, loss_target):
    with _jax.named_scope("forward"):
        args = {**rest, TWIN_DIFF_INPUT: diff, **{k: w.astype(_WEIGHT_DTYPES[k]) for k, w in weights.items()}}
        y = _forward(args)
    with _jax.named_scope("loss_head"):
        err = _jnp.square(y.astype(_jnp.float32) - loss_target)
        return 0.5 * _jnp.sum(_jnp.mean(err, axis=-1)) if err.ndim else 0.5 * err


def _adamw(w, g, m, v):
    m = ADAM_B1 * m + (1.0 - ADAM_B1) * g
    v = ADAM_B2 * v + (1.0 - ADAM_B2) * _jnp.square(g)
    m_hat = m / (1.0 - ADAM_B1 ** ADAM_STEP)
    v_hat = v / (1.0 - ADAM_B2 ** ADAM_STEP)
    delta = -ADAM_LR * (m_hat / (_jnp.sqrt(v_hat) + ADAM_EPS) + ADAM_WD * w)
    return delta, m, v


def reference(x, norm_mix_pre, norm_mix_post, norm_ffn_pre, norm_ffn_post, ret_w_in, ret_gn_w, ret_w_out, ssd_w_in, ssd_conv_w, ssd_conv_b, ssd_dt_bias, ssd_a_log, ssd_d, ssd_norm_w, ssd_w_out, mlp_w_up, mlp_w_down, loss_target, m_norm_mix_pre, m_norm_mix_post, m_norm_ffn_pre, m_norm_ffn_post, m_ret_w_in, m_ret_gn_w, m_ret_w_out, m_ssd_w_in, m_ssd_conv_w, m_ssd_conv_b, m_ssd_dt_bias, m_ssd_a_log, m_ssd_d, m_ssd_norm_w, m_ssd_w_out, m_mlp_w_up, m_mlp_w_down, v_norm_mix_pre, v_norm_mix_post, v_norm_ffn_pre, v_norm_ffn_post, v_ret_w_in, v_ret_gn_w, v_ret_w_out, v_ssd_w_in, v_ssd_conv_w, v_ssd_conv_b, v_ssd_dt_bias, v_ssd_a_log, v_ssd_d, v_ssd_norm_w, v_ssd_w_out, v_mlp_w_up, v_mlp_w_down):
    given = dict(x=x, norm_mix_pre=norm_mix_pre, norm_mix_post=norm_mix_post, norm_ffn_pre=norm_ffn_pre, norm_ffn_post=norm_ffn_post, ret_w_in=ret_w_in, ret_gn_w=ret_gn_w, ret_w_out=ret_w_out, ssd_w_in=ssd_w_in, ssd_conv_w=ssd_conv_w, ssd_conv_b=ssd_conv_b, ssd_dt_bias=ssd_dt_bias, ssd_a_log=ssd_a_log, ssd_d=ssd_d, ssd_norm_w=ssd_norm_w, ssd_w_out=ssd_w_out, mlp_w_up=mlp_w_up, mlp_w_down=mlp_w_down, loss_target=loss_target, m_norm_mix_pre=m_norm_mix_pre, m_norm_mix_post=m_norm_mix_post, m_norm_ffn_pre=m_norm_ffn_pre, m_norm_ffn_post=m_norm_ffn_post, m_ret_w_in=m_ret_w_in, m_ret_gn_w=m_ret_gn_w, m_ret_w_out=m_ret_w_out, m_ssd_w_in=m_ssd_w_in, m_ssd_conv_w=m_ssd_conv_w, m_ssd_conv_b=m_ssd_conv_b, m_ssd_dt_bias=m_ssd_dt_bias, m_ssd_a_log=m_ssd_a_log, m_ssd_d=m_ssd_d, m_ssd_norm_w=m_ssd_norm_w, m_ssd_w_out=m_ssd_w_out, m_mlp_w_up=m_mlp_w_up, m_mlp_w_down=m_mlp_w_down, v_norm_mix_pre=v_norm_mix_pre, v_norm_mix_post=v_norm_mix_post, v_norm_ffn_pre=v_norm_ffn_pre, v_norm_ffn_post=v_norm_ffn_post, v_ret_w_in=v_ret_w_in, v_ret_gn_w=v_ret_gn_w, v_ret_w_out=v_ret_w_out, v_ssd_w_in=v_ssd_w_in, v_ssd_conv_w=v_ssd_conv_w, v_ssd_conv_b=v_ssd_conv_b, v_ssd_dt_bias=v_ssd_dt_bias, v_ssd_a_log=v_ssd_a_log, v_ssd_d=v_ssd_d, v_ssd_norm_w=v_ssd_norm_w, v_ssd_w_out=v_ssd_w_out, v_mlp_w_up=v_mlp_w_up, v_mlp_w_down=v_mlp_w_down)
    weights = {n: given[n] for n in TWIN_WEIGHTS}
    shared = {n: given[n] for n in SHARED_INPUTS}
    per_example = {n: given[n] for n in ['x']}
    grad_fn = _jax.value_and_grad(_loss, argnums=(0, 1))

    def one_microbatch(ex, loss_target):
        ex = dict(ex)
        diff = ex.pop(TWIN_DIFF_INPUT)
        return grad_fn(weights, diff, {**shared, **ex}, loss_target)

    if N_MICROBATCH == 1:
        loss, (grad_w, grad_x) = one_microbatch(per_example, given["loss_target"])
    else:
        def body(carry, xs):
            loss_sum, grad_sum = carry
            l_k, (gw_k, gx_k) = one_microbatch(xs[0], xs[1])
            with _jax.named_scope("update"):
                return (loss_sum + l_k, _jax.tree.map(_jnp.add, grad_sum, gw_k)), gx_k

        init = (_jnp.zeros((), _jnp.float32), _jax.tree.map(_jnp.zeros_like, weights))
        (loss, grad_w), grad_x = _jax.lax.scan(body, init, (per_example, given["loss_target"]))
    with _jax.named_scope("update"):
        delta_w, new_m, new_v = {}, {}, {}
        for n in TWIN_WEIGHTS:
            delta_w[n], new_m[n], new_v[n] = _adamw(weights[n], grad_w[n], given["m_" + n], given["v_" + n])
    return (loss, grad_x, *[grad_w[n] for n in TWIN_WEIGHTS], *[delta_w[n] for n in TWIN_WEIGHTS],
            *[new_m[n] for n in TWIN_WEIGHTS], *[new_v[n] for n in TWIN_WEIGHTS])
```

```python
import functools
import math
import types

import jax
import jax.numpy as jnp
from jax import lax
from jax.experimental import pallas as pl
from jax.experimental.pallas import tpu as pltpu

F32 = jnp.float32
BF16 = jnp.bfloat16
HI = lax.Precision.HIGHEST
NN = (((1,), (0,)), ((), ()))
NT = (((1,), (1,)), ((), ()))
TN = (((0,), (0,)), ((), ()))
MESH = pl.DeviceIdType.MESH

V7X_VMEM_BYTES = 64 * 2**20
VMEM_LIMIT = V7X_VMEM_BYTES - 8 * 2**20
LANES = 128
SUBLANES = 8
N_DEV = 8

D_MODEL = 2048
SEQ = 8192
DEPTH = 2
CHUNK = 64
RMS_EPS = 1e-6
RET_HEAD_DK = 256
ROPE_BASE = 10000.0
GN_EPS = 1e-5
SSD_HEADDIM = 64
SSD_HEADS_PER_GROUP = 8
SSD_STATE = 128
SSD_CONV_W = 4
ADAM_LR = 0.001
ADAM_B1 = 0.9
ADAM_B2 = 0.999
ADAM_EPS = 1e-08
ADAM_WD = 0.01
ADAM_STEP = 10


def _cfg():
    c = types.SimpleNamespace()
    c.D, c.T, c.L = D_MODEL, SEQ, CHUNK
    c.DK = RET_HEAD_DK
    c.H = c.D // c.DK
    c.QK = c.H * c.DK
    c.DV = 2 * c.DK
    c.V = c.H * c.DV
    c.RIN = 2 * c.QK + 2 * c.V
    c.DI = 2 * c.D
    c.P = SSD_HEADDIM
    c.SH = c.DI // c.P
    c.RG = SSD_HEADS_PER_GROUP
    c.G = c.SH // c.RG
    c.GW = c.RG * c.P
    c.N = SSD_STATE
    c.CD = c.DI + 2 * c.G * c.N
    c.SIN = c.DI + c.CD + c.SH
    c.SINP = -(-c.SIN // LANES) * LANES
    c.FF = 4 * c.D
    c.NC = c.T // c.L
    return c


def _pcall(body, **kw):
    return pl.pallas_call(body, **kw)


def _cp(sem=None):
    return pltpu.CompilerParams(dimension_semantics=sem, vmem_limit_bytes=VMEM_LIMIT)


def _tile(n, pref, mult):
    if n <= pref:
        return n
    t = (pref // mult) * mult
    while t >= mult:
        if n % t == 0:
            return t
        t -= mult
    return n


def _dot(a, b, dn=NN, prec=None):
    return lax.dot_general(a, b, dn, precision=prec, preferred_element_type=F32)


def _sigmoid(x):
    return 1.0 / (1.0 + jnp.exp(-x))


def _mm(a, b, *, kind, name, out_dtype=F32, ti=1024, tj=512, tr=2048, epi=None, epi_in=None, out_nblk=1):
    b_blk = b.ndim == 3
    if kind == "tn":
        R, I = a.shape
    else:
        I, R = a.shape
    if kind == "nn":
        J = b.shape[1] if not b_blk else b.shape[0] * b.shape[2]
        nb_inner = b.shape[2] if b_blk else J
        r_inner = R
    elif kind == "nt":
        J = b.shape[0] if not b_blk else b.shape[1]
        nb_inner = J
        r_inner = b.shape[2] if b_blk else R
    else:
        J = b.shape[1]
        nb_inner = J
        r_inner = R
    out_inner = J // out_nblk
    ti = _tile(I, ti, LANES if kind == "tn" else SUBLANES)
    tj = _tile(min(nb_inner, out_inner), tj, LANES)
    assert nb_inner % tj == 0 and out_inner % tj == 0 and J % tj == 0
    tr = _tile(r_inner, tr, LANES)
    assert R % tr == 0
    ni, nj, nr = I // ti, J // tj, R // tr
    dn = {"nn": NN, "nt": NT, "tn": TN}[kind]

    if kind == "tn":
        a_spec = pl.BlockSpec((tr, ti), lambda i, j, r: (r, i))
    else:
        a_spec = pl.BlockSpec((ti, tr), lambda i, j, r: (i, r))
    if kind == "nn":
        if b_blk:
            per = nb_inner // tj
            b_spec = pl.BlockSpec((None, tr, tj), lambda i, j, r: (j // per, r, j % per))
        else:
            b_spec = pl.BlockSpec((tr, tj), lambda i, j, r: (r, j))
    elif kind == "nt":
        if b_blk:
            per = r_inner // tr
            b_spec = pl.BlockSpec((None, tj, tr), lambda i, j, r: (r // per, j, r % per))
        else:
            b_spec = pl.BlockSpec((tj, tr), lambda i, j, r: (j, r))
    else:
        b_spec = pl.BlockSpec((tr, tj), lambda i, j, r: (r, j))
    if out_nblk > 1:
        pero = out_inner // tj
        o_spec = pl.BlockSpec((None, ti, tj), lambda i, j, r: (j // pero, i, j % pero))
        o_shape = (out_nblk, I, out_inner)
    else:
        o_spec = pl.BlockSpec((ti, tj), lambda i, j, r: (i, j))
        o_shape = (I, J)
    in_specs = [a_spec, b_spec]
    args = [a, b]
    if epi == "mul2act":
        in_specs.append(pl.BlockSpec((ti, tj), lambda i, j, r: (i, j)))
        args.append(epi_in)
    if epi == "relu2":
        out_shape = (jax.ShapeDtypeStruct(o_shape, BF16), jax.ShapeDtypeStruct(o_shape, BF16))
        out_specs = (o_spec, o_spec)
    else:
        out_shape = jax.ShapeDtypeStruct(o_shape, out_dtype)
        out_specs = o_spec
    n_in = len(args)
    n_out = 2 if epi == "relu2" else 1

    def body(*refs):
        a_ref, b_ref = refs[0], refs[1]
        outs = refs[n_in:n_in + n_out]
        acc_ref = refs[n_in + n_out] if nr > 1 else None

        def finish(acc):
            if epi == "relu2":
                act = jnp.maximum(acc, 0.0)
                outs[0][...] = (act * act).astype(BF16)
                outs[1][...] = act.astype(BF16)
            elif epi == "mul2act":
                outs[0][...] = (acc * (2.0 * refs[2][...].astype(F32))).astype(out_dtype)
            else:
                outs[0][...] = acc.astype(out_dtype)

        part = _dot(a_ref[...], b_ref[...], dn)
        if nr == 1:
            finish(part)
        else:
            r = pl.program_id(2)

            @pl.when(r == 0)
            def _():
                acc_ref[...] = part

            @pl.when(r > 0)
            def _():
                acc_ref[...] += part

            @pl.when(r == nr - 1)
            def _():
                finish(acc_ref[...])

    return _pcall(
        body, name=name, grid=(ni, nj, nr), in_specs=in_specs, out_specs=out_specs, out_shape=out_shape,
        scratch_shapes=[pltpu.VMEM((ti, tj), F32)] if nr > 1 else [],
        compiler_params=_cp(("parallel", "parallel", "arbitrary")),
    )(*args)


def _rstd(x):
    return lax.rsqrt(jnp.mean(x * x, axis=-1, keepdims=True) + RMS_EPS)


def _rms_bwd_rows(x, w, dy):
    r = _rstd(x)
    xh = x * r
    dxh = dy * w
    dx = r * (dxh - xh * jnp.mean(dxh * xh, axis=-1, keepdims=True))
    return dx, jnp.sum(dy * xh, axis=0, keepdims=True)


def _row_spec(tb, d):
    return pl.BlockSpec((tb, d), lambda i: (i, 0))


def _vec_spec(d):
    return pl.BlockSpec((1, d), lambda i: (0, 0))


def _rms_fwd(h, w, name):
    T, D = h.shape
    tb = _tile(T, 512, SUBLANES)

    def body(h_ref, w_ref, u_ref):
        x = h_ref[...]
        u_ref[...] = (x * _rstd(x) * w_ref[...]).astype(BF16)

    return _pcall(body, name=name, grid=(T // tb,), in_specs=[_row_spec(tb, D), _vec_spec(D)],
                  out_specs=_row_spec(tb, D), out_shape=jax.ShapeDtypeStruct((T, D), BF16),
                  compiler_params=_cp(("parallel",)))(h, w)


def _resid_fwd(h, m, w_post, w_next, name):
    T, D = h.shape
    tb = _tile(T, 256, SUBLANES)

    def body(h_ref, m_ref, wp_ref, wn_ref, ho_ref, u_ref):
        x = m_ref[...]
        hn = h_ref[...] + x * _rstd(x) * wp_ref[...]
        ho_ref[...] = hn
        u_ref[...] = (hn * _rstd(hn) * wn_ref[...]).astype(BF16)

    return _pcall(body, name=name, grid=(T // tb,),
                  in_specs=[_row_spec(tb, D), _row_spec(tb, D), _vec_spec(D), _vec_spec(D)],
                  out_specs=(_row_spec(tb, D), _row_spec(tb, D)),
                  out_shape=(jax.ShapeDtypeStruct((T, D), F32), jax.ShapeDtypeStruct((T, D), BF16)),
                  compiler_params=_cp(("parallel",)))(h, m, w_post, w_next)


def _final_fwd(h, m, w_post, tgt, name):
    T, D = h.shape
    tb = _tile(T, 256, SUBLANES)

    def body(h_ref, m_ref, wp_ref, t_ref, g_ref, l_ref):
        x = m_ref[...]
        e = h_ref[...] + x * _rstd(x) * wp_ref[...] - t_ref[...]
        g_ref[...] = e * (1.0 / D)
        s = jnp.sum(e * e, axis=0, keepdims=True)

        @pl.when(pl.program_id(0) == 0)
        def _():
            l_ref[...] = s

        @pl.when(pl.program_id(0) > 0)
        def _():
            l_ref[...] += s

    return _pcall(body, name=name, grid=(T // tb,),
                  in_specs=[_row_spec(tb, D), _row_spec(tb, D), _vec_spec(D), _row_spec(tb, D)],
                  out_specs=(_row_spec(tb, D), _vec_spec(D)),
                  out_shape=(jax.ShapeDtypeStruct((T, D), F32), jax.ShapeDtypeStruct((1, D), F32)),
                  compiler_params=_cp(("arbitrary",)))(h, m, w_post, tgt)


def _norm_bwd(g_out, name, du=None, h=None, w_pre=None, m=None, w_post=None):
    T, D = g_out.shape
    tb = _tile(T, 256, SUBLANES)
    has_pre, has_post = du is not None, m is not None
    args, in_specs = [g_out], [_row_spec(tb, D)]
    if has_pre:
        args += [du, h, w_pre]
        in_specs += [_row_spec(tb, D), _row_spec(tb, D), _vec_spec(D)]
    if has_post:
        args += [m, w_post]
        in_specs += [_row_spec(tb, D), _vec_spec(D)]
    out_shape, out_specs = [], []
    if has_pre:
        out_shape.append(jax.ShapeDtypeStruct((T, D), F32))
        out_specs.append(_row_spec(tb, D))
    if has_post:
        out_shape.append(jax.ShapeDtypeStruct((T, D), BF16))
        out_specs.append(_row_spec(tb, D))
    n_w = int(has_pre) + int(has_post)
    out_shape += [jax.ShapeDtypeStruct((1, D), F32)] * n_w
    out_specs += [_vec_spec(D)] * n_w
    n_in = len(args)

    def body(*refs):
        ins, outs = list(refs[:n_in]), list(refs[n_in:])
        g = ins.pop(0)[...]
        sums = []
        if has_pre:
            du_ref, h_ref, w_ref = ins.pop(0), ins.pop(0), ins.pop(0)
            dx, s = _rms_bwd_rows(h_ref[...], w_ref[...], du_ref[...])
            g = g + dx
            outs.pop(0)[...] = g
            sums.append(s)
        if has_post:
            m_ref, w_ref = ins.pop(0), ins.pop(0)
            dx, s = _rms_bwd_rows(m_ref[...], w_ref[...], g)
            outs.pop(0)[...] = dx.astype(BF16)
            sums.append(s)
        first = pl.program_id(0) == 0
        for o_ref, s in zip(outs, sums):
            @pl.when(first)
            def _(o_ref=o_ref, s=s):
                o_ref[...] = s

            @pl.when(jnp.logical_not(first))
            def _(o_ref=o_ref, s=s):
                o_ref[...] += s

    return _pcall(body, name=name, grid=(T // tb,), in_specs=in_specs, out_specs=tuple(out_specs),
                  out_shape=tuple(out_shape), compiler_params=_cp(("arbitrary",)))(*args)


def _ret_consts(lg, L):
    ii = lax.broadcasted_iota(jnp.int32, (L, L), 0).astype(F32)
    jj = lax.broadcasted_iota(jnp.int32, (L, L), 1).astype(F32)
    dmat = jnp.exp(jnp.abs(ii - jj) * lg)
    idx = lax.broadcasted_iota(jnp.int32, (L, 1), 0).astype(F32)
    xi = jnp.exp((idx + 1.0) * lg)
    zeta = jnp.exp((L - 1.0 - idx) * lg)
    cd = jnp.exp(jnp.full((1, 1), L, F32) * lg)
    return dmat, xi, zeta, cd


def _rot(t, cs, sn):
    half = t.shape[-1] // 2
    t1, t2 = t[:, :half], t[:, half:]
    return jnp.concatenate([t1 * cs - t2 * sn, t1 * sn + t2 * cs], axis=-1)


def _rot_bwd(d, cs, sn):
    half = d.shape[-1] // 2
    d1, d2 = d[:, :half], d[:, half:]
    return jnp.concatenate([d1 * cs + d2 * sn, d2 * cs - d1 * sn], axis=-1)


def _ret_specs(cf, tb, rev):
    H, DK, DV = cf.H, cf.DK, cf.DV
    ni = cf.T // tb
    ri = (lambda i: ni - 1 - i) if rev else (lambda i: i)
    q = pl.BlockSpec((tb, DK), lambda h, i: (ri(i), h))
    k = pl.BlockSpec((tb, DK), lambda h, i: (ri(i), H + h))
    v = pl.BlockSpec((tb, DV), lambda h, i: (ri(i), cf.QK * 2 // DV + h))
    g = pl.BlockSpec((tb, DV), lambda h, i: (ri(i), cf.QK * 2 // DV + H + h))
    cs = pl.BlockSpec((tb, DK // 2), lambda h, i: (ri(i), 0))
    gw = pl.BlockSpec((1, DV), lambda h, i: (0, h))
    row_v = pl.BlockSpec((tb, DV), lambda h, i: (ri(i), h))
    row_k = pl.BlockSpec((tb, DK), lambda h, i: (ri(i), h))
    st = pl.BlockSpec((tb // cf.L, None, DK, DV), lambda h, i: (ri(i), h, 0, 0))
    lgs = pl.BlockSpec(memory_space=pltpu.SMEM)
    return q, k, v, g, cs, gw, row_v, row_k, st, lgs


def _ret_fwd(cf, proj, cos, sin, gn_w, lgam):
    T, L, H, DK, DV = cf.T, cf.L, cf.H, cf.DK, cf.DV
    tb = _tile(T, 512, L)
    nck = tb // L
    q_s, k_s, v_s, g_s, cs_s, gw_s, row_v, _, st_s, lg_s = _ret_specs(cf, tb, False)
    kscale = DK ** -0.5

    def body(lg_ref, q_ref, k_ref, v_ref, g_ref, cos_ref, sin_ref, gw_ref, y_ref, o_ref, st_ref, state):
        h = pl.program_id(0)

        @pl.when(pl.program_id(1) == 0)
        def _():
            state[...] = jnp.zeros_like(state)

        dmat, xi, zeta, cd = _ret_consts(lg_ref[h], L)
        gw = gw_ref[...]

        def chunk(c, carry):
            rows = pl.ds(pl.multiple_of(c * L, L), L)
            cs, sn = cos_ref[rows, :], sin_ref[rows, :]
            qr = _rot(q_ref[rows, :], cs, sn)
            kr = _rot(k_ref[rows, :], cs, sn) * kscale
            qb, kb = qr.astype(BF16), kr.astype(BF16)
            vb = v_ref[rows, :].astype(BF16)
            st = state[...]
            stb = st.astype(BF16)
            st_ref[c] = stb
            s = _dot(qb, kb, NT) * dmat
            o = _dot(s.astype(BF16), vb) + _dot(qb, stb) * xi
            state[...] = st * cd + _dot((kr * zeta).astype(BF16), vb, TN)
            o_ref[rows, :] = o
            mu = jnp.mean(o, axis=-1, keepdims=True)
            oc = o - mu
            var = jnp.mean(oc * oc, axis=-1, keepdims=True)
            n = oc * lax.rsqrt(var + GN_EPS) * gw
            gt = g_ref[rows, :]
            y_ref[rows, :] = (gt * _sigmoid(gt) * n).astype(BF16)
            return carry

        lax.fori_loop(0, nck, chunk, 0)

    return _pcall(
        body, name="ret_fwd", grid=(H, T // tb),
        in_specs=[lg_s, q_s, k_s, v_s, g_s, cs_s, cs_s, gw_s],
        out_specs=(row_v, row_v, st_s),
        out_shape=(jax.ShapeDtypeStruct((T, cf.V), BF16), jax.ShapeDtypeStruct((T, cf.V), F32),
                   jax.ShapeDtypeStruct((cf.NC, H, DK, DV), BF16)),
        scratch_shapes=[pltpu.VMEM((DK, DV), F32)],
        compiler_params=_cp(("parallel", "arbitrary")),
    )(lgam, proj, proj, proj, proj, cos, sin, gn_w)


def _ret_bwd(cf, proj, cos, sin, gn_w, lgam, o, states, dy):
    T, L, H, DK, DV = cf.T, cf.L, cf.H, cf.DK, cf.DV
    tb = _tile(T, 512, L)
    nck = tb // L
    q_s, k_s, v_s, g_s, cs_s, gw_s, row_v, row_k, st_s, lg_s = _ret_specs(cf, tb, True)
    kscale = DK ** -0.5

    def body(lg_ref, q_ref, k_ref, v_ref, g_ref, cos_ref, sin_ref, gw_ref, o_ref, st_ref, dy_ref,
             dq_ref, dk_ref, dv_ref, dg_ref, dgw_ref, dstate):
        h = pl.program_id(0)

        @pl.when(pl.program_id(1) == 0)
        def _():
            dstate[...] = jnp.zeros_like(dstate)
            dgw_ref[...] = jnp.zeros_like(dgw_ref)

        dmat, xi, zeta, cd = _ret_consts(lg_ref[h], L)
        gw = gw_ref[...]

        def chunk(t, carry):
            c = nck - 1 - t
            rows = pl.ds(pl.multiple_of(c * L, L), L)
            cs, sn = cos_ref[rows, :], sin_ref[rows, :]
            qr = _rot(q_ref[rows, :], cs, sn)
            kr = _rot(k_ref[rows, :], cs, sn) * kscale
            qb, kb = qr.astype(BF16), kr.astype(BF16)
            kzb = (kr * zeta).astype(BF16)
            vb = v_ref[rows, :].astype(BF16)
            s = (_dot(qb, kb, NT) * dmat).astype(BF16)
            oo = o_ref[rows, :]
            mu = jnp.mean(oo, axis=-1, keepdims=True)
            oc = oo - mu
            rstd = lax.rsqrt(jnp.mean(oc * oc, axis=-1, keepdims=True) + GN_EPS)
            oh = oc * rstd
            gt = g_ref[rows, :]
            sg = _sigmoid(gt)
            dyv = dy_ref[rows, :]
            dn = dyv * (gt * sg)
            dg_ref[rows, :] = (dyv * (oh * gw) * (sg * (1.0 + gt * (1.0 - sg)))).astype(BF16)
            dgw_ref[...] += jnp.sum(dn * oh, axis=0, keepdims=True)
            doh = dn * gw
            do = rstd * (doh - jnp.mean(doh, axis=-1, keepdims=True) - oh * jnp.mean(doh * oh, axis=-1, keepdims=True))
            dob = do.astype(BF16)
            doxb = (do * xi).astype(BF16)
            dst = dstate[...]
            dstb = dst.astype(BF16)
            stb = st_ref[c]
            dv_ref[rows, :] = (_dot(s, dob, TN) + _dot(kzb, dstb)).astype(BF16)
            ds = (_dot(dob, vb, NT) * dmat).astype(BF16)
            dqr = _dot(ds, kb) + _dot(doxb, stb, NT)
            dkr = _dot(ds, qb, TN) + _dot(vb, dstb, NT) * zeta
            dstate[...] = dst * cd + _dot(qb, doxb, TN)
            dq_ref[rows, :] = _rot_bwd(dqr, cs, sn).astype(BF16)
            dk_ref[rows, :] = _rot_bwd(dkr * kscale, cs, sn).astype(BF16)
            return carry

        lax.fori_loop(0, nck, chunk, 0)

    return _pcall(
        body, name="ret_bwd", grid=(H, T // tb),
        in_specs=[lg_s, q_s, k_s, v_s, g_s, cs_s, cs_s, gw_s, row_v, st_s, row_v],
        out_specs=(row_k, row_k, row_v, row_v, gw_s),
        out_shape=(jax.ShapeDtypeStruct((T, cf.QK), BF16), jax.ShapeDtypeStruct((T, cf.QK), BF16),
                   jax.ShapeDtypeStruct((T, cf.V), BF16), jax.ShapeDtypeStruct((T, cf.V), BF16),
                   jax.ShapeDtypeStruct((1, cf.V), F32)),
        scratch_shapes=[pltpu.VMEM((DK, DV), F32)],
        compiler_params=_cp(("parallel", "arbitrary")),
    )(lgam, proj, proj, proj, proj, cos, sin, gn_w, o, states, dy)


def _conv_pre(x, halo, w, b, first, W):
    tb = x.shape[0]
    ext = jnp.concatenate([jnp.where(first, 0.0, halo), x], axis=0)
    out = b + w[W - 1:W, :] * x
    for tap in range(W - 1):
        out = out + w[tap:tap + 1, :] * pltpu.roll(ext, W - 1 - tap, 0)[SUBLANES:SUBLANES + tb, :]
    return out, ext


def _conv_fwd(cf, proj, conv_w, conv_b):
    T, CD, W = cf.T, cf.CD, SSD_CONV_W
    tb = _tile(T, 512, SUBLANES)
    tc = _tile(CD, 512, LANES)
    off = cf.DI // tc
    nh = tb // SUBLANES

    def body(x_ref, halo_ref, w_ref, b_ref, o_ref):
        pre, _ = _conv_pre(x_ref[...], halo_ref[...], w_ref[...], b_ref[...], pl.program_id(1) == 0, W)
        o_ref[...] = pre * _sigmoid(pre)

    return _pcall(
        body, name="conv_fwd", grid=(CD // tc, T // tb),
        in_specs=[pl.BlockSpec((tb, tc), lambda j, i: (i, off + j)),
                  pl.BlockSpec((SUBLANES, tc), lambda j, i: (jnp.maximum(i * nh - 1, 0), off + j)),
                  pl.BlockSpec((W, tc), lambda j, i: (0, j)), pl.BlockSpec((1, tc), lambda j, i: (0, j))],
        out_specs=pl.BlockSpec((tb, tc), lambda j, i: (i, j)),
        out_shape=jax.ShapeDtypeStruct((T, CD), F32),
        compiler_params=_cp(("parallel", "arbitrary")),
    )(proj, proj, conv_w, conv_b)


def _conv_bwd_pre(cf, proj, conv_w, conv_b, dact):
    T, CD, W = cf.T, cf.CD, SSD_CONV_W
    tb = _tile(T, 512, SUBLANES)
    tc = _tile(CD, 512, LANES)
    off = cf.DI // tc
    nh = tb // SUBLANES

    def body(x_ref, halo_ref, w_ref, b_ref, da_ref, dp_ref, dw_ref, db_ref):
        x = x_ref[...]
        pre, ext = _conv_pre(x, halo_ref[...], w_ref[...], b_ref[...], pl.program_id(1) == 0, W)
        sg = _sigmoid(pre)
        dp = da_ref[...] * (sg * (1.0 + pre * (1.0 - sg)))
        dp_ref[...] = dp
        rows = [jnp.sum(dp * pltpu.roll(ext, W - 1 - tap, 0)[SUBLANES:SUBLANES + tb, :], axis=0, keepdims=True)
                for tap in range(W - 1)]
        rows.append(jnp.sum(dp * x, axis=0, keepdims=True))
        dw = jnp.concatenate(rows, axis=0)
        db = jnp.sum(dp, axis=0, keepdims=True)

        @pl.when(pl.program_id(1) == 0)
        def _():
            dw_ref[...] = dw
            db_ref[...] = db

        @pl.when(pl.program_id(1) > 0)
        def _():
            dw_ref[...] += dw
            db_ref[...] += db

    return _pcall(
        body, name="conv_bwd_pre", grid=(CD // tc, T // tb),
        in_specs=[pl.BlockSpec((tb, tc), lambda j, i: (i, off + j)),
                  pl.BlockSpec((SUBLANES, tc), lambda j, i: (jnp.maximum(i * nh - 1, 0), off + j)),
                  pl.BlockSpec((W, tc), lambda j, i: (0, j)), pl.BlockSpec((1, tc), lambda j, i: (0, j)),
                  pl.BlockSpec((tb, tc), lambda j, i: (i, j))],
        out_specs=(pl.BlockSpec((tb, tc), lambda j, i: (i, j)), pl.BlockSpec((W, tc), lambda j, i: (0, j)),
                   pl.BlockSpec((1, tc), lambda j, i: (0, j))),
        out_shape=(jax.ShapeDtypeStruct((T, CD), F32), jax.ShapeDtypeStruct((W, CD), F32),
                   jax.ShapeDtypeStruct((1, CD), F32)),
        compiler_params=_cp(("parallel", "arbitrary")),
    )(proj, proj, conv_w, conv_b, dact)


def _conv_bwd_x(cf, dpre, conv_w):
    T, CD, W = cf.T, cf.CD, SSD_CONV_W
    tb = _tile(T, 512, SUBLANES)
    tc = _tile(CD, 512, LANES)
    nh = tb // SUBLANES
    last_blk = T // SUBLANES - 1
    ni = T // tb

    def body(d_ref, halo_ref, w_ref, o_ref):
        d = d_ref[...]
        w = w_ref[...]
        nxt = jnp.where(pl.program_id(1) == ni - 1, 0.0, halo_ref[...])
        ext = jnp.concatenate([d, nxt], axis=0)
        n = tb + SUBLANES
        out = w[W - 1:W, :] * d
        for tap in range(W - 1):
            out = out + w[tap:tap + 1, :] * pltpu.roll(ext, n - (W - 1 - tap), 0)[:tb, :]
        o_ref[...] = out.astype(BF16)

    return _pcall(
        body, name="conv_bwd_x", grid=(CD // tc, ni),
        in_specs=[pl.BlockSpec((tb, tc), lambda j, i: (i, j)),
                  pl.BlockSpec((SUBLANES, tc), lambda j, i: (jnp.minimum((i + 1) * nh, last_blk), j)),
                  pl.BlockSpec((W, tc), lambda j, i: (0, j))],
        out_specs=pl.BlockSpec((tb, tc), lambda j, i: (i, j)),
        out_shape=jax.ShapeDtypeStruct((T, CD), BF16),
        compiler_params=_cp(("parallel", "arbitrary")),
    )(dpre, dpre, conv_w)


def _ssd_masks(cf, g):
    L, GW, P, RG = cf.L, cf.GW, cf.P, cf.RG
    i32 = jnp.int32
    hrow = lax.broadcasted_iota(i32, (LANES, GW), 0)
    hcol = lax.broadcasted_iota(i32, (LANES, GW), 1) // P
    expand = (hrow == g * RG + hcol).astype(F32)
    li = lax.broadcasted_iota(i32, (L, L), 0)
    lj = lax.broadcasted_iota(i32, (L, L), 1)
    tril = (li >= lj).astype(F32)
    r0 = lax.broadcasted_iota(i32, (L, RG * L), 0)
    c0 = lax.broadcasted_iota(i32, (L, RG * L), 1) % L
    tile_eye = (r0 == c0).astype(F32)
    lower = r0 >= c0
    b0 = lax.broadcasted_iota(i32, (RG * L, GW), 0) // L
    b1 = lax.broadcasted_iota(i32, (RG * L, GW), 1) // P
    bdiag = (b0 == b1).astype(F32)
    ones = jnp.ones((L, L), F32)
    return expand, tril, tile_eye, lower, bdiag, ones


def _softplus(x):
    return jnp.maximum(x, 0.0) + jnp.log1p(jnp.exp(-jnp.abs(x)))


def _ssd_chunk_fwd(cf, mk, dtr, xs, bm, cm, bias, a_e):
    expand, tril, tile_eye, lower, bdiag, ones = mk
    raw = _dot(dtr, expand, NN, HI) + bias
    dt = _softplus(raw)
    da = dt * a_e
    acum = _dot(tril, da, NN, HI)
    rowv = _dot(ones, acum * tile_eye, NN, HI)
    lf = jnp.exp(jnp.where(lower, acum - rowv, -1e30))
    xdt = xs * dt
    bb, cb_ = bm.astype(BF16), cm.astype(BF16)
    cbt = _dot(_dot(cb_, bb, NT), tile_eye, NN, HI)
    bd = (jnp.tile(xdt, (cf.RG, 1)) * bdiag).astype(BF16)
    return raw, dt, acum, lf, xdt, bb, cb_, cbt, bd


def _ssd_specs(cf, tb, rev):
    G, GW, N = cf.G, cf.GW, cf.N
    ni = cf.T // tb
    ri = (lambda i: ni - 1 - i) if rev else (lambda i: i)
    z = pl.BlockSpec((tb, GW), lambda g, i: (ri(i), g))
    dt = pl.BlockSpec((tb, LANES), lambda g, i: (ri(i), (cf.DI + cf.CD) // LANES))
    xs = pl.BlockSpec((tb, GW), lambda g, i: (ri(i), g))
    bm = pl.BlockSpec((tb, N), lambda g, i: (ri(i), cf.DI // N + g))
    cm = pl.BlockSpec((tb, N), lambda g, i: (ri(i), cf.DI // N + G + g))
    vec = pl.BlockSpec((1, GW), lambda g, i: (0, g))
    st = pl.BlockSpec((tb // cf.L, None, N, GW), lambda g, i: (ri(i), g, 0, 0))
    return z, dt, xs, bm, cm, vec, st


def _ssd_fwd(cf, proj, xact, bias_e, alog_e, dskip_e, norm_w):
    T, L, G, GW, N = cf.T, cf.L, cf.G, cf.GW, cf.N
    tb = _tile(T, 512, L)
    nck = tb // L
    z_s, dt_s, xs_s, b_s, c_s, vec_s, st_s = _ssd_specs(cf, tb, False)

    def body(z_ref, dt_ref, xs_ref, b_ref, c_ref, bias_ref, alog_ref, dsk_ref, nw_ref, y_ref, yp_ref, st_ref, state):
        @pl.when(pl.program_id(1) == 0)
        def _():
            state[...] = jnp.zeros_like(state)

        mk = _ssd_masks(cf, pl.program_id(0))
        a_e = -jnp.exp(alog_ref[...])
        bias, dsk, nw = bias_ref[...], dsk_ref[...], nw_ref[...]

        def chunk(c, carry):
            rows = pl.ds(pl.multiple_of(c * L, L), L)
            xs = xs_ref[rows, :]
            _, dt, acum, lf, xdt, bb, cb_, cbt, bd = _ssd_chunk_fwd(
                cf, mk, dt_ref[rows, :], xs, b_ref[rows, :], c_ref[rows, :], bias, a_e)
            st = state[...]
            stb = st.astype(BF16)
            st_ref[c] = stb
            y = _dot((cbt * lf).astype(BF16), bd) + _dot(cb_, stb) * jnp.exp(acum)
            al = acum[L - 1:L, :]
            state[...] = st * jnp.exp(al) + _dot(bb, (xdt * jnp.exp(al - acum)).astype(BF16), TN)
            yp_ref[rows, :] = y
            z = z_ref[rows, :]
            yg = (y + dsk * xs) * (z * _sigmoid(z))
            y_ref[rows, :] = (yg * _rstd(yg) * nw).astype(BF16)
            return carry

        lax.fori_loop(0, nck, chunk, 0)

    return _pcall(
        body, name="ssd_fwd", grid=(G, T // tb),
        in_specs=[z_s, dt_s, xs_s, b_s, c_s, vec_s, vec_s, vec_s, vec_s],
        out_specs=(z_s, z_s, st_s),
        out_shape=(jax.ShapeDtypeStruct((T, cf.DI), BF16), jax.ShapeDtypeStruct((T, cf.DI), F32),
                   jax.ShapeDtypeStruct((cf.NC, G, N, GW), BF16)),
        scratch_shapes=[pltpu.VMEM((N, GW), F32)],
        compiler_params=_cp(("parallel", "arbitrary")),
    )(proj, proj, xact, xact, xact, bias_e, alog_e, dskip_e, norm_w)


def _ssd_bwd(cf, proj, xact, bias_e, alog_e, dskip_e, norm_w, ypre, states, dy):
    T, L, G, GW, N, RG = cf.T, cf.L, cf.G, cf.GW, cf.N, cf.RG
    tb = _tile(T, 256, L)
    nck = tb // L
    ni = T // tb
    z_s, dt_s, xs_s, b_s, c_s, vec_s, st_s = _ssd_specs(cf, tb, True)
    bc_out = pl.BlockSpec((tb, N), lambda g, i: (ni - 1 - i, g))
    ddt_out = pl.BlockSpec((None, tb, LANES), lambda g, i: (g, ni - 1 - i, 0))

    def body(z_ref, dt_ref, xs_ref, b_ref, c_ref, bias_ref, alog_ref, dsk_ref, nw_ref, yp_ref, st_ref, dy_ref,
             dz_ref, dxs_ref, db_ref, dc_ref, ddt_ref, dnw_ref, ddsk_ref, dalog_ref, dbias_ref, dstate):
        @pl.when(pl.program_id(1) == 0)
        def _():
            dstate[...] = jnp.zeros_like(dstate)
            for r in (dnw_ref, ddsk_ref, dalog_ref, dbias_ref):
                r[...] = jnp.zeros_like(r)

        mk = _ssd_masks(cf, pl.program_id(0))
        expand, tril, tile_eye, lower, bdiag, ones = mk
        a_e = -jnp.exp(alog_ref[...])
        bias, dsk, nw = bias_ref[...], dsk_ref[...], nw_ref[...]
        last_row = (lax.broadcasted_iota(jnp.int32, (L, 1), 0) == L - 1).astype(F32)

        def chunk(t, carry):
            c = nck - 1 - t
            rows = pl.ds(pl.multiple_of(c * L, L), L)
            xs = xs_ref[rows, :]
            raw, dt, acum, lf, xdt, bb, cb_, cbt, bd = _ssd_chunk_fwd(
                cf, mk, dt_ref[rows, :], xs, b_ref[rows, :], c_ref[rows, :], bias, a_e)
            stb = st_ref[c]
            eac = jnp.exp(acum)
            al = acum[L - 1:L, :]
            eal = jnp.exp(al)
            dte = jnp.exp(al - acum)
            y = yp_ref[rows, :]
            z = z_ref[rows, :]
            sz = _sigmoid(z)
            silu = z * sz
            yd = y + dsk * xs
            yg = yd * silu
            rr = _rstd(yg)
            xh = yg * rr
            dout = dy_ref[rows, :]
            dnw_ref[...] += jnp.sum(dout * xh, axis=0, keepdims=True)
            dxh = dout * nw
            dyg = rr * (dxh - xh * jnp.mean(dxh * xh, axis=-1, keepdims=True))
            dz_ref[rows, :] = (dyg * yd * (sz * (1.0 + z * (1.0 - sz)))).astype(BF16)
            dys = dyg * silu
            ddsk_ref[...] += jnp.sum(dys * xs, axis=0, keepdims=True)
            dxs = dys * dsk
            dyb = dys.astype(BF16)
            mb = (cbt * lf).astype(BF16)
            dm = _dot(dyb, bd, NT)
            dbd = _dot(mb, dyb, TN) * bdiag
            dxdt = dbd[0:L, :]
            for r in range(1, RG):
                dxdt = dxdt + dbd[r * L:(r + 1) * L, :]
            dcb = _dot(dm * lf, tile_eye, NT, HI).astype(BF16)
            dcm = _dot(dcb, bb)
            dbm = _dot(dcb, cb_, TN)
            dseg = dm * cbt * lf
            dacum = dseg - tile_eye * jnp.sum(dseg, axis=0, keepdims=True)
            dyo = (dys * eac).astype(BF16)
            dcm = dcm + _dot(dyo, stb, NT)
            dacum = dacum + dys * _dot(cb_, stb) * eac
            dst = dstate[...]
            dstb = dst.astype(BF16)
            xd = xdt * dte
            dbm = dbm + _dot(xd.astype(BF16), dstb, NT)
            dxd = _dot(bb, dstb)
            dal = jnp.sum(dst * stb.astype(F32), axis=0, keepdims=True) * eal
            dxdt = dxdt + dxd * dte
            tt = dxd * xd
            dacum = dacum - tt + last_row * (dal + jnp.sum(tt, axis=0, keepdims=True))
            dstate[...] = dst * eal + _dot(cb_, dyo, TN)
            dxs_ref[rows, :] = dxs + dxdt * dt
            dda = _dot(tril, dacum, TN, HI)
            ddt = dxdt * xs + dda * a_e
            dalog_ref[...] += jnp.sum(dda * dt, axis=0, keepdims=True) * a_e
            draw = ddt * _sigmoid(raw)
            dbias_ref[...] += jnp.sum(draw, axis=0, keepdims=True)
            ddt_ref[rows, :] = _dot(draw, expand, NT, HI)
            db_ref[rows, :] = dbm
            dc_ref[rows, :] = dcm
            return carry

        lax.fori_loop(0, nck, chunk, 0)

    GN = G * N
    return _pcall(
        body, name="ssd_bwd", grid=(G, ni),
        in_specs=[z_s, dt_s, xs_s, b_s, c_s, vec_s, vec_s, vec_s, vec_s, z_s, st_s, z_s],
        out_specs=(z_s, z_s, bc_out, bc_out, ddt_out, vec_s, vec_s, vec_s, vec_s),
        out_shape=(jax.ShapeDtypeStruct((T, cf.DI), BF16), jax.ShapeDtypeStruct((T, cf.DI), F32),
                   jax.ShapeDtypeStruct((T, GN), F32), jax.ShapeDtypeStruct((T, GN), F32),
                   jax.ShapeDtypeStruct((G, T, LANES), F32)) + (jax.ShapeDtypeStruct((1, cf.DI), F32),) * 4,
        scratch_shapes=[pltpu.VMEM((N, GW), F32)],
        compiler_params=_cp(("parallel", "arbitrary")),
    )(proj, proj, xact, xact, xact, bias_e, alog_e, dskip_e, norm_w, ypre, states, dy)


def _sum_groups(parts, name):
    G, T, W = parts.shape
    tb = _tile(T, 512, SUBLANES)

    def body(p_ref, o_ref):
        acc = p_ref[0]
        for g in range(1, G):
            acc = acc + p_ref[g]
        o_ref[...] = acc.astype(BF16)

    return _pcall(body, name=name, grid=(T // tb,), in_specs=[pl.BlockSpec((G, tb, W), lambda i: (0, i, 0))],
                  out_specs=pl.BlockSpec((tb, W), lambda i: (i, 0)), out_shape=jax.ShapeDtypeStruct((T, W), BF16),
                  compiler_params=_cp(("parallel",)))(parts)


ANY = pl.BlockSpec(memory_space=pl.ANY)


def _all_gather(x, name):
    def body(x_ref, out_ref, send_sems, recv_sems, local_sem):
        x, y, c = lax.axis_index("x"), lax.axis_index("y"), lax.axis_index("c")
        me, sibling = (x, y, c), (x, y, 1 - c)
        chips = [(1 - x, y), (x, 1 - y), (1 - x, 1 - y)]

        def blk(px, py, pc):
            return out_ref.at[4 * px + 2 * py + pc]

        def copy(k, block, to, src=None):
            return pltpu.make_async_remote_copy(
                src_ref=blk(*block) if src is None else src, dst_ref=blk(*block),
                send_sem=send_sems.at[k], recv_sem=recv_sems.at[k], device_id=to, device_id_type=MESH)

        mine = pltpu.make_async_copy(x_ref, blk(*me), local_sem)
        mine.start()
        first = [copy(0, me, sibling, src=x_ref)]
        first += [copy(1 + j, me, (*chip, c), src=x_ref) for j, chip in enumerate(chips)]
        for cp in first:
            cp.start()
        passed = [copy(4 + j, (*chip, c), sibling) for j, chip in enumerate(chips)]
        for j, chip in enumerate(chips):
            copy(1 + j, (*chip, c), me).wait_recv()
            passed[j].start()
        copy(0, sibling, me).wait_recv()
        for j, chip in enumerate(chips):
            copy(4 + j, (*chip, 1 - c), me).wait_recv()
        for cp in first + passed:
            cp.wait_send()
        mine.wait()

    return _pcall(
        body, name=name, in_specs=[ANY], out_specs=ANY,
        out_shape=jax.ShapeDtypeStruct((N_DEV,) + x.shape, x.dtype),
        scratch_shapes=[pltpu.SemaphoreType.DMA((7,)), pltpu.SemaphoreType.DMA((7,)), pltpu.SemaphoreType.DMA],
    )(x)


def _swap_cores(g, name):
    def body(g_ref, out_ref, send_sems, recv_sems):
        x, y, c = lax.axis_index("x"), lax.axis_index("y"), lax.axis_index("c")
        copies = [pltpu.make_async_remote_copy(
            src_ref=g_ref.at[q, 1 - c], dst_ref=out_ref.at[q], send_sem=send_sems.at[q], recv_sem=recv_sems.at[q],
            device_id=(x, y, 1 - c), device_id_type=MESH) for q in range(4)]
        for cp in copies:
            cp.start()
        for cp in copies:
            cp.wait()

    return _pcall(
        body, name=name, in_specs=[ANY], out_specs=ANY,
        out_shape=jax.ShapeDtypeStruct((4,) + g.shape[2:], g.dtype),
        scratch_shapes=[pltpu.SemaphoreType.DMA((4,)), pltpu.SemaphoreType.DMA((4,))],
    )(g)


def _swap_chips(p, name):
    def body(p_ref, out_ref, send_sems, recv_sems):
        x, y, c = lax.axis_index("x"), lax.axis_index("y"), lax.axis_index("c")
        chips = [(1 - x, y), (x, 1 - y), (1 - x, 1 - y)]
        copies = [pltpu.make_async_remote_copy(
            src_ref=p_ref.at[2 * cx + cy], dst_ref=out_ref.at[k], send_sem=send_sems.at[k], recv_sem=recv_sems.at[k],
            device_id=(cx, cy, c), device_id_type=MESH) for k, (cx, cy) in enumerate(chips)]
        for cp in copies:
            cp.start()
        for cp in copies:
            cp.wait()

    return _pcall(
        body, name=name, in_specs=[ANY], out_specs=ANY,
        out_shape=jax.ShapeDtypeStruct((3,) + p.shape[1:], p.dtype),
        scratch_shapes=[pltpu.SemaphoreType.DMA((3,)), pltpu.SemaphoreType.DMA((3,))],
    )(p)


def _core_sum(g, recv, idx, name):
    _, _, a, b = g.shape
    tr = _tile(a, max(SUBLANES, (2 * 2**20) // (4 * b) // SUBLANES * SUBLANES), SUBLANES)

    def body(idx_ref, g_ref, r_ref, p_ref, own_ref):
        s = g_ref[...] + r_ref[...]
        p_ref[...] = s.astype(BF16)

        @pl.when(pl.program_id(1) == idx_ref[1])
        def _():
            own_ref[...] = s

    return _pcall(
        body, name=name,
        grid_spec=pltpu.PrefetchScalarGridSpec(
            num_scalar_prefetch=1, grid=(a // tr, 4),
            in_specs=[pl.BlockSpec((None, None, tr, b), lambda i, q, idx: (q, idx[0], i, 0)),
                      pl.BlockSpec((None, tr, b), lambda i, q, idx: (q, i, 0))],
            out_specs=(pl.BlockSpec((None, tr, b), lambda i, q, idx: (q, i, 0)),
                       pl.BlockSpec((tr, b), lambda i, q, idx: (i, 0)))),
        out_shape=(jax.ShapeDtypeStruct((4, a, b), BF16), jax.ShapeDtypeStruct((a, b), F32)),
        compiler_params=_cp(("parallel", "arbitrary")),
    )(idx, g, recv)


def _adam_math(w, g, m, v):
    m = ADAM_B1 * m + (1.0 - ADAM_B1) * g
    v = ADAM_B2 * v + (1.0 - ADAM_B2) * (g * g)
    m_hat = m / (1.0 - ADAM_B1 ** ADAM_STEP)
    v_hat = v / (1.0 - ADAM_B2 ** ADAM_STEP)
    delta = -ADAM_LR * (m_hat / (jnp.sqrt(v_hat) + ADAM_EPS) + ADAM_WD * w)
    return delta, m, v


def _chip_sum_adam(own, recv, w, m, v, layer, name):
    a, b = own.shape
    tr = _tile(a, max(SUBLANES, (2**20) // (4 * b) // SUBLANES * SUBLANES), SUBLANES)
    if layer is None:
        wspec = pl.BlockSpec((tr, b), lambda i: (i, 0))
    else:
        wspec = pl.BlockSpec((None, tr, b), lambda i: (layer, i, 0))
    ospec = pl.BlockSpec((tr, b), lambda i: (i, 0))

    def body(own_ref, r_ref, w_ref, m_ref, v_ref, g_ref, d_ref, mo_ref, vo_ref):
        g = own_ref[...]
        for k in range(3):
            g = g + r_ref[k].astype(F32)
        g_ref[...] = g
        d_ref[...], mo_ref[...], vo_ref[...] = _adam_math(w_ref[...], g, m_ref[...], v_ref[...])

    return _pcall(
        body, name=name, grid=(a // tr,),
        in_specs=[ospec, pl.BlockSpec((3, tr, b), lambda i: (0, i, 0)), wspec, wspec, wspec],
        out_specs=(ospec,) * 4, out_shape=(jax.ShapeDtypeStruct((a, b), F32),) * 4,
        compiler_params=_cp(("parallel",)),
    )(own, recv, w, m, v)


def _reduce_scatter_adam(gfull, w, m, v, layer, name, idx):
    _, a, b = gfull.shape
    g4 = gfull.reshape(4, 2, a, b)
    recv1 = _swap_cores(g4, name + "_swap_cores")
    part, own = _core_sum(g4, recv1, idx, name + "_core_sum")
    recv2 = _swap_chips(part, name + "_swap_chips")
    return _chip_sum_adam(own, recv2, w, m, v, layer, name + "_adam")


def _all_reduce_small(x, n_fold, fold_w, name):
    R, W = x.shape

    def body(x_ref, out_ref, buf, send_sems, recv_sems):
        xx, y, c = lax.axis_index("x"), lax.axis_index("y"), lax.axis_index("c")
        me = 4 * xx + 2 * y + c
        buf[me] = x_ref[...]
        copies = []
        for k in range(1, N_DEV):
            px, py, pc = xx ^ (k >> 2), y ^ ((k >> 1) & 1), c ^ (k & 1)
            copies.append(pltpu.make_async_remote_copy(
                src_ref=x_ref, dst_ref=buf.at[me], send_sem=send_sems.at[k - 1], recv_sem=recv_sems.at[k - 1],
                device_id=(px, py, pc), device_id_type=MESH))
        for cp in copies:
            cp.start()
        for cp in copies:
            cp.wait()
        acc = buf[0]
        for j in range(1, N_DEV):
            acc = acc + buf[j]
        out_ref[...] = acc
        if n_fold:
            l0 = lax.broadcasted_iota(jnp.int32, (W, W), 0) // fold_w
            l1 = lax.broadcasted_iota(jnp.int32, (W, W), 1)
            fold = (l0 == l1).astype(F32)
            out_ref[R - n_fold:, :] = _dot(acc[R - n_fold:, :], fold, NN, HI)

    return _pcall(
        body, name=name, in_specs=[pl.BlockSpec(memory_space=pltpu.VMEM)],
        out_specs=pl.BlockSpec(memory_space=pltpu.VMEM), out_shape=jax.ShapeDtypeStruct((R, W), F32),
        scratch_shapes=[pltpu.VMEM((N_DEV, R, W), F32), pltpu.SemaphoreType.DMA((N_DEV - 1,)),
                        pltpu.SemaphoreType.DMA((N_DEV - 1,))],
        compiler_params=pltpu.CompilerParams(vmem_limit_bytes=VMEM_LIMIT),
    )(x)


def _adam_small(w, g, m, v, name):
    def body(w_ref, g_ref, m_ref, v_ref, d_ref, mo_ref, vo_ref):
        d_ref[...], mo_ref[...], vo_ref[...] = _adam_math(w_ref[...], g_ref[...], m_ref[...], v_ref[...])

    return _pcall(body, name=name, out_shape=(jax.ShapeDtypeStruct(w.shape, F32),) * 3)(w, g, m, v)


def kernel(x, norm_mix_pre, norm_mix_post, norm_ffn_pre, norm_ffn_post, ret_w_in, ret_gn_w, ret_w_out, ssd_w_in, ssd_conv_w, ssd_conv_b, ssd_dt_bias, ssd_a_log, ssd_d, ssd_norm_w, ssd_w_out, mlp_w_up, mlp_w_down, loss_target, m_norm_mix_pre, m_norm_mix_post, m_norm_ffn_pre, m_norm_ffn_post, m_ret_w_in, m_ret_gn_w, m_ret_w_out, m_ssd_w_in, m_ssd_conv_w, m_ssd_conv_b, m_ssd_dt_bias, m_ssd_a_log, m_ssd_d, m_ssd_norm_w, m_ssd_w_out, m_mlp_w_up, m_mlp_w_down, v_norm_mix_pre, v_norm_mix_post, v_norm_ffn_pre, v_norm_ffn_post, v_ret_w_in, v_ret_gn_w, v_ret_w_out, v_ssd_w_in, v_ssd_conv_w, v_ssd_conv_b, v_ssd_dt_bias, v_ssd_a_log, v_ssd_d, v_ssd_norm_w, v_ssd_w_out, v_mlp_w_up, v_mlp_w_down):
    cf = _cfg()
    T, D = cf.T, cf.D
    ax, ay, ac = lax.axis_index("x"), lax.axis_index("y"), lax.axis_index("c")
    my_dev = 4 * ax + 2 * ay + ac
    idx = jnp.stack([ac, 2 * ax + ay]).astype(jnp.int32)

    def gather(w, name):
        return _all_gather(w.astype(BF16), name)

    w_ri = gather(ret_w_in[0], "ag_ret_w_in")
    w_ro = gather(ret_w_out[0], "ag_ret_w_out").reshape(cf.V, D)
    w_up = [gather(mlp_w_up[l], f"ag_mlp_up{l}") for l in range(DEPTH)]
    w_dn = [gather(mlp_w_down[l], f"ag_mlp_down{l}").reshape(cf.FF, D) for l in range(DEPTH)]
    w_si = gather(ssd_w_in[0], "ag_ssd_w_in")
    w_si = jnp.pad(jnp.transpose(w_si, (1, 0, 2)).reshape(D, cf.SIN), ((0, 0), (0, cf.SINP - cf.SIN)))
    w_so = gather(ssd_w_out[0], "ag_ssd_w_out").reshape(cf.DI, D)
    cw, cbw, nww = cf.CD // N_DEV, cf.CD // N_DEV, cf.DI // N_DEV
    small = jnp.concatenate([ssd_conv_w[0], ssd_conv_b, jnp.pad(ssd_norm_w, ((0, 0), (0, cw - nww))),
                             jnp.zeros((2, cw), F32)], axis=0)
    small = _all_gather(small, "ag_ssd_small")
    conv_w = jnp.transpose(small[:, :SSD_CONV_W, :], (1, 0, 2)).reshape(SSD_CONV_W, cf.CD)
    conv_b = small[:, SSD_CONV_W, :].reshape(1, cf.CD)
    ssd_nw = small[:, SSD_CONV_W + 1, :nww].reshape(1, cf.DI)

    half = cf.DK // 2
    inv_freq = ROPE_BASE ** (-jnp.arange(half, dtype=F32) / half)
    ang = jnp.arange(T).astype(F32)[:, None] * inv_freq[None, :]
    cos, sin = jnp.cos(ang), jnp.sin(ang)
    lgam = jnp.log1p(-jnp.exp2(-5.0 - jnp.arange(cf.H, dtype=F32)))
    rep = lambda p: jnp.repeat(p.reshape(1, cf.SH), cf.P, axis=1)
    bias_e, alog_e, dskip_e = rep(ssd_dt_bias), rep(ssd_a_log), rep(ssd_d)

    h0 = x.reshape(T, D)
    tgt = loss_target.reshape(T, D)
    nrm = lambda p, i: p[i:i + 1]

    u0 = _rms_fwd(h0, nrm(norm_mix_pre, 0), "rms_fwd0")
    proj0 = _mm(u0, w_ri, kind="nn", name="mm_ret_in")
    y0, o0, st0 = _ret_fwd(cf, proj0, cos, sin, ret_gn_w, lgam)
    m0 = _mm(y0, w_ro, kind="nn", name="mm_ret_out", tr=cf.V)
    h1, u1 = _resid_fwd(h0, m0, nrm(norm_mix_post, 0), nrm(norm_ffn_pre, 0), "resid_fwd0")
    sq0, act0 = _mm(u1, w_up[0], kind="nn", name="mm_up0", epi="relu2")
    f0 = _mm(sq0, w_dn[0], kind="nn", name="mm_down0", tj=1024)
    h2, u2 = _resid_fwd(h1, f0, nrm(norm_ffn_post, 0), nrm(norm_mix_pre, 1), "resid_fwd1")
    proj1 = _mm(u2, w_si, kind="nn", name="mm_ssd_in", tj=1152)
    xact = _conv_fwd(cf, proj1, conv_w, conv_b)
    y1, yp1, st1 = _ssd_fwd(cf, proj1, xact, bias_e, alog_e, dskip_e, ssd_nw)
    m1 = _mm(y1, w_so, kind="nn", name="mm_ssd_out", tr=cf.DI)
    h3, u3 = _resid_fwd(h2, m1, nrm(norm_mix_post, 1), nrm(norm_ffn_pre, 1), "resid_fwd2")
    sq1, act1 = _mm(u3, w_up[1], kind="nn", name="mm_up1", epi="relu2")
    f1 = _mm(sq1, w_dn[1], kind="nn", name="mm_down1", tj=1024)
    g4, lsum = _final_fwd(h3, f1, nrm(norm_ffn_post, 1), tgt, "final_fwd")
    loss = lax.psum(0.5 * jnp.sum(lsum) / D, ("x", "y", "c"))

    def mlp_bwd(l, df, u, sq, act):
        dpre = _mm(df, w_dn[l], kind="nt", name=f"mm_dpre{l}", out_dtype=BF16, epi="mul2act", epi_in=act)
        g_dn = _mm(sq, df, kind="tn", name=f"mm_gdown{l}", tj=1024, tr=1024, out_nblk=1)
        du = _mm(dpre, w_up[l], kind="nt", name=f"mm_du_mlp{l}", tj=1024, tr=1024)
        g_up = _mm(u, dpre, kind="tn", name=f"mm_gup{l}", tj=1024, tr=1024, out_nblk=N_DEV)
        return du, g_up, g_dn.reshape(N_DEV, cf.FF // N_DEV, D)

    df1, g_nfpost1 = _norm_bwd(g4, "norm_bwd4", m=f1, w_post=nrm(norm_ffn_post, 1))
    du3, g_up1, g_dn1 = mlp_bwd(1, df1, u3, sq1, act1)
    gh3, dm1, g_nfpre1, g_nmpost1 = _norm_bwd(g4, "norm_bwd3", du=du3, h=h3, w_pre=nrm(norm_ffn_pre, 1),
                                              m=m1, w_post=nrm(norm_mix_post, 1))
    dy1 = _mm(dm1, w_so, kind="nt", name="mm_dy_ssd")
    g_so = _mm(y1, dm1, kind="tn", name="mm_g_ssd_out", tj=1024, tr=1024).reshape(N_DEV, cf.DI // N_DEV, D)
    dz, dxs, dbm, dcm, ddt_parts, g_ssd_nw, g_dskip_e, g_alog_e, g_bias_e = _ssd_bwd(
        cf, proj1, xact, bias_e, alog_e, dskip_e, ssd_nw, yp1, st1, dy1)
    dact = jnp.concatenate([dxs, dbm, dcm], axis=1)
    dpre1, g_conv_w, g_conv_b = _conv_bwd_pre(cf, proj1, conv_w, conv_b, dact)
    dxbc = _conv_bwd_x(cf, dpre1, conv_w)
    ddt = _sum_groups(ddt_parts, "ssd_ddt_sum")
    dproj1 = jnp.concatenate([dz, dxbc, ddt], axis=1)
    if cf.SINP > cf.DI + cf.CD + LANES:
        dproj1 = jnp.pad(dproj1, ((0, 0), (0, cf.SINP - cf.DI - cf.CD - LANES)))
    du2 = _mm(dproj1, w_si, kind="nt", name="mm_du_ssd", tj=1024, tr=1152)
    g_si = _mm(u2, dproj1, kind="tn", name="mm_g_ssd_in", tj=1152, tr=1024)
    g_si = jnp.transpose(g_si[:, :cf.SIN].reshape(D, N_DEV, cf.SIN // N_DEV), (1, 0, 2))
    gh2, df0, g_nmpre1, g_nfpost0 = _norm_bwd(gh3, "norm_bwd2", du=du2, h=h2, w_pre=nrm(norm_mix_pre, 1),
                                              m=f0, w_post=nrm(norm_ffn_post, 0))
    du1, g_up0, g_dn0 = mlp_bwd(0, df0, u1, sq0, act0)
    gh1, dm0, g_nfpre0, g_nmpost0 = _norm_bwd(gh2, "norm_bwd1", du=du1, h=h1, w_pre=nrm(norm_ffn_pre, 0),
                                              m=m0, w_post=nrm(norm_mix_post, 0))
    dy0 = _mm(dm0, w_ro, kind="nt", name="mm_dy_ret")
    g_ro = _mm(y0, dm0, kind="tn", name="mm_g_ret_out", tj=1024, tr=1024).reshape(N_DEV, cf.V // N_DEV, D)
    dq, dk, dv, dg, g_gn = _ret_bwd(cf, proj0, cos, sin, ret_gn_w, lgam, o0, st0, dy0)
    dproj0 = jnp.concatenate([dq, dk, dv, dg], axis=1)
    rin8 = cf.RIN // N_DEV
    du0 = _mm(dproj0, w_ri, kind="nt", name="mm_du_ret", tj=1024, tr=_tile(rin8, 1024, LANES))
    g_ri = _mm(u0, dproj0, kind="tn", name="mm_g_ret_in", tj=_tile(rin8, 1024, LANES), tr=1024, out_nblk=N_DEV)
    grad_x, g_nmpre0 = _norm_bwd(gh1, "norm_bwd0", du=du0, h=h0, w_pre=nrm(norm_mix_pre, 0))

    g_nmpre = jnp.concatenate([g_nmpre0, g_nmpre1], axis=0)
    g_nmpost = jnp.concatenate([g_nmpost0, g_nmpost1], axis=0)
    g_nfpre = jnp.concatenate([g_nfpre0, g_nfpre1], axis=0)
    g_nfpost = jnp.concatenate([g_nfpost0, g_nfpost1], axis=0)
    segs = [g_nmpre, g_nmpost, g_nfpre, g_nfpost, g_gn, g_conv_w, g_conv_b, g_ssd_nw, g_bias_e, g_alog_e, g_dskip_e]
    flat = jnp.concatenate([s.reshape(-1, LANES) for s in segs], axis=0)
    n_fold = 3 * cf.DI // LANES
    red = _all_reduce_small(flat, n_fold, cf.P, "all_reduce_small")
    outs, r0 = [], 0
    for s in segs:
        nr = s.size // LANES
        outs.append(red[r0:r0 + nr])
        r0 += nr
    (g_nmpre, g_nmpost, g_nfpre, g_nfpost) = [o.reshape(DEPTH, D) for o in outs[:4]]
    g_gn = outs[4].reshape(1, cf.V)
    g_conv_w = lax.dynamic_slice_in_dim(outs[5].reshape(SSD_CONV_W, cf.CD), my_dev * cw, cw, axis=1)[None]
    g_conv_b = lax.dynamic_slice_in_dim(outs[6].reshape(1, cf.CD), my_dev * cbw, cbw, axis=1)
    g_ssd_nw = lax.dynamic_slice_in_dim(outs[7].reshape(1, cf.DI), my_dev * nww, nww, axis=1)
    per_row = LANES // cf.P
    g_bias, g_alog, g_dskip = [o[:, :per_row].reshape(1, cf.SH) for o in outs[8:]]

    def rs(g, w, m, v, layer, name):
        return _reduce_scatter_adam(g, w, m, v, layer, name, idx)

    r_ri = rs(g_ri, ret_w_in, m_ret_w_in, v_ret_w_in, 0, "rs_ret_w_in")
    r_ro = rs(g_ro, ret_w_out, m_ret_w_out, v_ret_w_out, 0, "rs_ret_w_out")
    r_si = rs(g_si, ssd_w_in, m_ssd_w_in, v_ssd_w_in, 0, "rs_ssd_w_in")
    r_so = rs(g_so, ssd_w_out, m_ssd_w_out, v_ssd_w_out, 0, "rs_ssd_w_out")
    r_up = [rs(g, mlp_w_up, m_mlp_w_up, v_mlp_w_up, l, f"rs_mlp_up{l}") for l, g in enumerate([g_up0, g_up1])]
    r_dn = [rs(g, mlp_w_down, m_mlp_w_down, v_mlp_w_down, l, f"rs_mlp_down{l}") for l, g in enumerate([g_dn0, g_dn1])]
    r_up = [jnp.stack([r_up[0][k], r_up[1][k]]) for k in range(4)]
    r_dn = [jnp.stack([r_dn[0][k], r_dn[1][k]]) for k in range(4)]
    lead = lambda r: [a[None] for a in r]

    def small_adam(w, g, m, v, name):
        return [g] + list(_adam_small(w, g, m, v, name))

    results = {
        "norm_mix_pre": small_adam(norm_mix_pre, g_nmpre, m_norm_mix_pre, v_norm_mix_pre, "adam_nmpre"),
        "norm_mix_post": small_adam(norm_mix_post, g_nmpost, m_norm_mix_post, v_norm_mix_post, "adam_nmpost"),
        "norm_ffn_pre": small_adam(norm_ffn_pre, g_nfpre, m_norm_ffn_pre, v_norm_ffn_pre, "adam_nfpre"),
        "norm_ffn_post": small_adam(norm_ffn_post, g_nfpost, m_norm_ffn_post, v_norm_ffn_post, "adam_nfpost"),
        "ret_w_in": lead(r_ri),
        "ret_gn_w": small_adam(ret_gn_w, g_gn, m_ret_gn_w, v_ret_gn_w, "adam_gn"),
        "ret_w_out": lead(r_ro),
        "ssd_w_in": lead(r_si),
        "ssd_conv_w": small_adam(ssd_conv_w, g_conv_w, m_ssd_conv_w, v_ssd_conv_w, "adam_conv_w"),
        "ssd_conv_b": small_adam(ssd_conv_b, g_conv_b, m_ssd_conv_b, v_ssd_conv_b, "adam_conv_b"),
        "ssd_dt_bias": small_adam(ssd_dt_bias, g_bias, m_ssd_dt_bias, v_ssd_dt_bias, "adam_dt_bias"),
        "ssd_a_log": small_adam(ssd_a_log, g_alog, m_ssd_a_log, v_ssd_a_log, "adam_a_log"),
        "ssd_d": small_adam(ssd_d, g_dskip, m_ssd_d, v_ssd_d, "adam_d"),
        "ssd_norm_w": small_adam(ssd_norm_w, g_ssd_nw, m_ssd_norm_w, v_ssd_norm_w, "adam_ssd_nw"),
        "ssd_w_out": lead(r_so),
        "mlp_w_up": r_up,
        "mlp_w_down": r_dn,
    }
    names = list(results)
    out = [loss, grad_x.reshape(1, T, D)]
    for k in range(4):
        out += [results[n][k] for n in names]
    return tuple(out)
```

```python
import functools
import math
import types

import jax
import jax.numpy as jnp
from jax import lax
from jax.experimental import pallas as pl
from jax.experimental.pallas import tpu as pltpu

F32 = jnp.float32
BF16 = jnp.bfloat16
HI = lax.Precision.HIGHEST
NN = (((1,), (0,)), ((), ()))
NT = (((1,), (1,)), ((), ()))
TN = (((0,), (0,)), ((), ()))
MESH = pl.DeviceIdType.MESH

V7X_VMEM_BYTES = 64 * 2**20
VMEM_LIMIT = V7X_VMEM_BYTES - 8 * 2**20
LANES = 128
SUBLANES = 8
N_DEV = 8

D_MODEL = 2048
SEQ = 8192
DEPTH = 2
CHUNK = 64
RMS_EPS = 1e-6
RET_HEAD_DK = 256
ROPE_BASE = 10000.0
GN_EPS = 1e-5
SSD_HEADDIM = 64
SSD_HEADS_PER_GROUP = 8
SSD_STATE = 128
SSD_CONV_W = 4
ADAM_LR = 0.001
ADAM_B1 = 0.9
ADAM_B2 = 0.999
ADAM_EPS = 1e-08
ADAM_WD = 0.01
ADAM_STEP = 10


def _cfg():
    c = types.SimpleNamespace()
    c.D, c.T, c.L = D_MODEL, SEQ, CHUNK
    c.DK = RET_HEAD_DK
    c.H = c.D // c.DK
    c.QK = c.H * c.DK
    c.DV = 2 * c.DK
    c.V = c.H * c.DV
    c.RIN = 2 * c.QK + 2 * c.V
    c.DI = 2 * c.D
    c.P = SSD_HEADDIM
    c.SH = c.DI // c.P
    c.RG = SSD_HEADS_PER_GROUP
    c.G = c.SH // c.RG
    c.GW = c.RG * c.P
    c.N = SSD_STATE
    c.CD = c.DI + 2 * c.G * c.N
    c.SIN = c.DI + c.CD + c.SH
    c.SINP = -(-c.SIN // LANES) * LANES
    c.FF = 4 * c.D
    c.NC = c.T // c.L
    return c


def _pcall(body, **kw):
    return pl.pallas_call(body, **kw)


def _cp(sem=None):
    return pltpu.CompilerParams(dimension_semantics=sem, vmem_limit_bytes=VMEM_LIMIT)


def _tile(n, pref, mult):
    if n <= pref:
        return n
    t = (pref // mult) * mult
    while t >= mult:
        if n % t == 0:
            return t
        t -= mult
    return n


def _dot(a, b, dn=NN, prec=None):
    return lax.dot_general(a, b, dn, precision=prec, preferred_element_type=F32)


ANY = pl.BlockSpec(memory_space=pl.ANY)


def _mesh_pos():
    return lax.axis_index("x"), lax.axis_index("y"), lax.axis_index("c")


def _rider_join(riders):
    j = types.SimpleNamespace(args=[], out_shape=[], aliases={}, sems=[])
    parts = []
    for r in riders:
        a0, o0, s0 = len(j.args), len(j.out_shape), len(j.sems)
        parts.append((r, a0, o0, s0))
        j.aliases.update({a0 + k: o0 + v for k, v in r.aliases.items()})
        j.args += r.args
        j.out_shape += r.out_shape
        j.sems += r.sems

    def make(rins, routs, sems):
        sends, recvs, locs = [], [], []
        for r, a0, o0, s0 in parts:
            s, rc, lc = r.make(rins[a0:a0 + len(r.args)], routs[o0:o0 + len(r.out_shape)], sems[s0:s0 + len(r.sems)])
            sends += s
            recvs += rc
            locs += lc
        return sends, recvs, locs

    j.make = make
    return j


def _rider_start(rider, rins, routs, sems):
    sends, _, locs = rider.make(rins, routs, sems)
    for cp in locs + sends:
        cp.start()


def _rider_wait(rider, rins, routs, sems):
    sends, recvs, locs = rider.make(rins, routs, sems)
    for cp in recvs:
        cp.wait_recv()
    for cp in sends:
        cp.wait_send()
    for cp in locs:
        cp.wait()


def _host_call(body, *, name, grid, in_specs, out_specs, out_shape, scratch, args, sem, rider=None):
    in_specs, out_specs, out_shape, scratch, args = map(list, (in_specs, out_specs, out_shape, scratch, args))
    if rider is None:
        res = _pcall(body, name=name, grid=grid, in_specs=in_specs, out_specs=out_specs, out_shape=out_shape,
                     scratch_shapes=scratch, compiler_params=_cp(sem))(*args)
        return list(res), []
    n_in, n_out, n_scr = len(args), len(out_shape), len(scratch)
    n_ra, n_ro = len(rider.args), len(rider.out_shape)

    def full(*refs):
        ins, rins = refs[:n_in], refs[n_in:n_in + n_ra]
        p = n_in + n_ra
        outs, routs = refs[p:p + n_out], refs[p + n_out:p + n_out + n_ro]
        p += n_out + n_ro
        scr, sems = refs[p:p + n_scr], refs[p + n_scr:]
        first = functools.reduce(jnp.logical_and, [pl.program_id(k) == 0 for k in range(len(grid))])
        last = functools.reduce(jnp.logical_and, [pl.program_id(k) == grid[k] - 1 for k in range(len(grid))])

        @pl.when(first)
        def _():
            _rider_start(rider, rins, routs, sems)

        body(*ins, *outs, *scr)

        @pl.when(last)
        def _():
            _rider_wait(rider, rins, routs, sems)

    res = _pcall(full, name=name, grid=grid, in_specs=in_specs + [ANY] * n_ra, out_specs=out_specs + [ANY] * n_ro,
                 out_shape=out_shape + rider.out_shape, scratch_shapes=scratch + rider.sems,
                 input_output_aliases={n_in + k: n_out + v for k, v in rider.aliases.items()},
                 compiler_params=_cp(("arbitrary",) * len(grid)))(*args, *rider.args)
    return list(res[:n_out]), list(res[n_out:])


def _comm(rider, name):
    n_ra, n_ro = len(rider.args), len(rider.out_shape)

    def body(*refs):
        rins, routs, sems = refs[:n_ra], refs[n_ra:n_ra + n_ro], refs[n_ra + n_ro:]
        _rider_start(rider, rins, routs, sems)
        _rider_wait(rider, rins, routs, sems)

    return list(_pcall(body, name=name, in_specs=[ANY] * n_ra, out_specs=[ANY] * n_ro, out_shape=rider.out_shape,
                       scratch_shapes=rider.sems, input_output_aliases=dict(rider.aliases))(*rider.args))


def _remote(src, dst, send_sems, recv_sems, k, to):
    return pltpu.make_async_remote_copy(src_ref=src, dst_ref=dst, send_sem=send_sems.at[k], recv_sem=recv_sems.at[k],
                                        device_id=to, device_id_type=MESH)


def _r_gather_chips(x):
    def make(rins, routs, sems):
        (x_ref,), (out_ref,), (send_sems, recv_sems, local_sem) = rins, routs, sems
        x_, y_, c_ = _mesh_pos()
        me = 4 * x_ + 2 * y_ + c_
        peers = [(x_, y_, 1 - c_), (1 - x_, y_, c_), (x_, 1 - y_, c_), (1 - x_, 1 - y_, c_)]
        sends = [_remote(x_ref, out_ref.at[me], send_sems, recv_sems, k, to) for k, to in enumerate(peers)]
        recvs = [_remote(x_ref, out_ref.at[4 * px + 2 * py + pc], send_sems, recv_sems, k, (px, py, pc))
                 for k, (px, py, pc) in enumerate(peers)]
        return sends, recvs, [pltpu.make_async_copy(x_ref, out_ref.at[me], local_sem)]

    return types.SimpleNamespace(
        args=[x], out_shape=[jax.ShapeDtypeStruct((N_DEV,) + x.shape, x.dtype)], aliases={},
        sems=[pltpu.SemaphoreType.DMA((4,)), pltpu.SemaphoreType.DMA((4,)), pltpu.SemaphoreType.DMA], make=make)


def _r_gather_cores(buf):
    def make(rins, routs, sems):
        (out_ref,), (send_sems, recv_sems) = routs, sems
        x_, y_, c_ = _mesh_pos()
        chips = [(1 - x_, y_), (x_, 1 - y_), (1 - x_, 1 - y_)]
        sends = [_remote(out_ref.at[4 * cx + 2 * cy + c_], out_ref.at[4 * cx + 2 * cy + c_], send_sems, recv_sems, k,
                         (x_, y_, 1 - c_)) for k, (cx, cy) in enumerate(chips)]
        recvs = [_remote(out_ref.at[4 * cx + 2 * cy + 1 - c_], out_ref.at[4 * cx + 2 * cy + 1 - c_], send_sems,
                         recv_sems, k, (x_, y_, 1 - c_)) for k, (cx, cy) in enumerate(chips)]
        return sends, recvs, []

    return types.SimpleNamespace(
        args=[buf], out_shape=[jax.ShapeDtypeStruct(buf.shape, buf.dtype)], aliases={0: 0},
        sems=[pltpu.SemaphoreType.DMA((3,)), pltpu.SemaphoreType.DMA((3,))], make=make)


def _r_swap_cores(g):
    def make(rins, routs, sems):
        (g_ref,), (out_ref,), (send_sems, recv_sems) = rins, routs, sems
        x_, y_, c_ = _mesh_pos()
        cps = [_remote(g_ref.at[q, 1 - c_], out_ref.at[q], send_sems, recv_sems, q, (x_, y_, 1 - c_)) for q in range(4)]
        return cps, cps, []

    return types.SimpleNamespace(
        args=[g], out_shape=[jax.ShapeDtypeStruct((4,) + g.shape[2:], g.dtype)], aliases={},
        sems=[pltpu.SemaphoreType.DMA((4,)), pltpu.SemaphoreType.DMA((4,))], make=make)


def _r_swap_chips(p):
    def make(rins, routs, sems):
        (p_ref,), (out_ref,), (send_sems, recv_sems) = rins, routs, sems
        x_, y_, c_ = _mesh_pos()
        chips = [(1 - x_, y_), (x_, 1 - y_), (1 - x_, 1 - y_)]
        cps = [_remote(p_ref.at[2 * cx + cy], out_ref.at[k], send_sems, recv_sems, k, (cx, cy, c_))
               for k, (cx, cy) in enumerate(chips)]
        return cps, cps, []

    return types.SimpleNamespace(
        args=[p], out_shape=[jax.ShapeDtypeStruct((3,) + p.shape[1:], p.dtype)], aliases={},
        sems=[pltpu.SemaphoreType.DMA((3,)), pltpu.SemaphoreType.DMA((3,))], make=make)


def _sigmoid(x):
    return 1.0 / (1.0 + jnp.exp(-x))


def _mm(a, b, *, kind, name, out_dtype=F32, ti=1024, tj=512, tr=2048, epi=None, epi_in=None, out_nblk=1, rider=None):
    b_blk = b.ndim == 3
    if kind == "tn":
        R, I = a.shape
    else:
        I, R = a.shape
    if kind == "nn":
        J = b.shape[1] if not b_blk else b.shape[0] * b.shape[2]
        nb_inner = b.shape[2] if b_blk else J
        r_inner = R
    elif kind == "nt":
        J = b.shape[0] if not b_blk else b.shape[1]
        nb_inner = J
        r_inner = b.shape[2] if b_blk else R
    else:
        J = b.shape[1]
        nb_inner = J
        r_inner = R
    out_inner = J // out_nblk
    ti = _tile(I, ti, LANES if kind == "tn" else SUBLANES)
    tj = _tile(min(nb_inner, out_inner), tj, LANES)
    assert nb_inner % tj == 0 and out_inner % tj == 0 and J % tj == 0
    tr = _tile(r_inner, tr, LANES)
    assert R % tr == 0
    ni, nj, nr = I // ti, J // tj, R // tr
    dn = {"nn": NN, "nt": NT, "tn": TN}[kind]

    if kind == "tn":
        a_spec = pl.BlockSpec((tr, ti), lambda i, j, r: (r, i))
    else:
        a_spec = pl.BlockSpec((ti, tr), lambda i, j, r: (i, r))
    if kind == "nn":
        if b_blk:
            per = nb_inner // tj
            b_spec = pl.BlockSpec((None, tr, tj), lambda i, j, r: (j // per, r, j % per))
        else:
            b_spec = pl.BlockSpec((tr, tj), lambda i, j, r: (r, j))
    elif kind == "nt":
        if b_blk:
            per = r_inner // tr
            b_spec = pl.BlockSpec((None, tj, tr), lambda i, j, r: (r // per, j, r % per))
        else:
            b_spec = pl.BlockSpec((tj, tr), lambda i, j, r: (j, r))
    else:
        b_spec = pl.BlockSpec((tr, tj), lambda i, j, r: (r, j))
    if out_nblk > 1:
        pero = out_inner // tj
        o_spec = pl.BlockSpec((None, ti, tj), lambda i, j, r: (j // pero, i, j % pero))
        o_shape = (out_nblk, I, out_inner)
    else:
        o_spec = pl.BlockSpec((ti, tj), lambda i, j, r: (i, j))
        o_shape = (I, J)
    in_specs = [a_spec, b_spec]
    args = [a, b]
    if epi == "mul2act":
        in_specs.append(pl.BlockSpec((ti, tj), lambda i, j, r: (i, j)))
        args.append(epi_in)
    if epi == "relu2":
        out_shape = (jax.ShapeDtypeStruct(o_shape, BF16), jax.ShapeDtypeStruct(o_shape, BF16))
        out_specs = (o_spec, o_spec)
    else:
        out_shape = jax.ShapeDtypeStruct(o_shape, out_dtype)
        out_specs = o_spec
    n_in = len(args)
    n_out = 2 if epi == "relu2" else 1

    def body(*refs):
        a_ref, b_ref = refs[0], refs[1]
        outs = refs[n_in:n_in + n_out]
        acc_ref = refs[n_in + n_out] if nr > 1 else None

        def finish(acc):
            if epi == "relu2":
                act = jnp.maximum(acc, 0.0)
                outs[0][...] = (act * act).astype(BF16)
                outs[1][...] = act.astype(BF16)
            elif epi == "mul2act":
                outs[0][...] = (acc * (2.0 * refs[2][...].astype(F32))).astype(out_dtype)
            else:
                outs[0][...] = acc.astype(out_dtype)

        part = _dot(a_ref[...], b_ref[...], dn)
        if nr == 1:
            finish(part)
        else:
            r = pl.program_id(2)

            @pl.when(r == 0)
            def _():
                acc_ref[...] = part

            @pl.when(r > 0)
            def _():
                acc_ref[...] += part

            @pl.when(r == nr - 1)
            def _():
                finish(acc_ref[...])

    res, rider_res = _host_call(
        body, name=name, grid=(ni, nj, nr), in_specs=in_specs,
        out_specs=out_specs if n_out > 1 else [out_specs], out_shape=out_shape if n_out > 1 else [out_shape],
        scratch=[pltpu.VMEM((ti, tj), F32)] if nr > 1 else [], args=args,
        sem=("parallel", "parallel", "arbitrary"), rider=rider)
    res = tuple(res) if n_out > 1 else res[0]
    return res if rider is None else (res, rider_res)


def _rstd(x):
    return lax.rsqrt(jnp.mean(x * x, axis=-1, keepdims=True) + RMS_EPS)


def _rms_bwd_rows(x, w, dy):
    r = _rstd(x)
    xh = x * r
    dxh = dy * w
    dx = r * (dxh - xh * jnp.mean(dxh * xh, axis=-1, keepdims=True))
    return dx, jnp.sum(dy * xh, axis=0, keepdims=True)


def _row_spec(tb, d):
    return pl.BlockSpec((tb, d), lambda i: (i, 0))


def _vec_spec(d):
    return pl.BlockSpec((1, d), lambda i: (0, 0))


def _rms_fwd(h, w, name):
    T, D = h.shape
    tb = _tile(T, 512, SUBLANES)

    def body(h_ref, w_ref, u_ref):
        x = h_ref[...]
        u_ref[...] = (x * _rstd(x) * w_ref[...]).astype(BF16)

    return _pcall(body, name=name, grid=(T // tb,), in_specs=[_row_spec(tb, D), _vec_spec(D)],
                  out_specs=_row_spec(tb, D), out_shape=jax.ShapeDtypeStruct((T, D), BF16),
                  compiler_params=_cp(("parallel",)))(h, w)


def _resid_fwd(h, m, w_post, w_next, name):
    T, D = h.shape
    tb = _tile(T, 256, SUBLANES)

    def body(h_ref, m_ref, wp_ref, wn_ref, ho_ref, u_ref):
        x = m_ref[...]
        hn = h_ref[...] + x * _rstd(x) * wp_ref[...]
        ho_ref[...] = hn
        u_ref[...] = (hn * _rstd(hn) * wn_ref[...]).astype(BF16)

    return _pcall(body, name=name, grid=(T // tb,),
                  in_specs=[_row_spec(tb, D), _row_spec(tb, D), _vec_spec(D), _vec_spec(D)],
                  out_specs=(_row_spec(tb, D), _row_spec(tb, D)),
                  out_shape=(jax.ShapeDtypeStruct((T, D), F32), jax.ShapeDtypeStruct((T, D), BF16)),
                  compiler_params=_cp(("parallel",)))(h, m, w_post, w_next)


def _final_fwd(h, m, w_post, tgt, name):
    T, D = h.shape
    tb = _tile(T, 256, SUBLANES)

    def body(h_ref, m_ref, wp_ref, t_ref, g_ref, l_ref):
        x = m_ref[...]
        e = h_ref[...] + x * _rstd(x) * wp_ref[...] - t_ref[...]
        g_ref[...] = e * (1.0 / D)
        s = jnp.sum(e * e, axis=0, keepdims=True)

        @pl.when(pl.program_id(0) == 0)
        def _():
            l_ref[...] = s

        @pl.when(pl.program_id(0) > 0)
        def _():
            l_ref[...] += s

    return _pcall(body, name=name, grid=(T // tb,),
                  in_specs=[_row_spec(tb, D), _row_spec(tb, D), _vec_spec(D), _row_spec(tb, D)],
                  out_specs=(_row_spec(tb, D), _vec_spec(D)),
                  out_shape=(jax.ShapeDtypeStruct((T, D), F32), jax.ShapeDtypeStruct((1, D), F32)),
                  compiler_params=_cp(("arbitrary",)))(h, m, w_post, tgt)


def _norm_bwd(g_out, name, du=None, h=None, w_pre=None, m=None, w_post=None):
    T, D = g_out.shape
    tb = _tile(T, 256, SUBLANES)
    has_pre, has_post = du is not None, m is not None
    args, in_specs = [g_out], [_row_spec(tb, D)]
    if has_pre:
        args += [du, h, w_pre]
        in_specs += [_row_spec(tb, D), _row_spec(tb, D), _vec_spec(D)]
    if has_post:
        args += [m, w_post]
        in_specs += [_row_spec(tb, D), _vec_spec(D)]
    out_shape, out_specs = [], []
    if has_pre:
        out_shape.append(jax.ShapeDtypeStruct((T, D), F32))
        out_specs.append(_row_spec(tb, D))
    if has_post:
        out_shape.append(jax.ShapeDtypeStruct((T, D), BF16))
        out_specs.append(_row_spec(tb, D))
    n_w = int(has_pre) + int(has_post)
    out_shape += [jax.ShapeDtypeStruct((1, D), F32)] * n_w
    out_specs += [_vec_spec(D)] * n_w
    n_in = len(args)

    def body(*refs):
        ins, outs = list(refs[:n_in]), list(refs[n_in:])
        g = ins.pop(0)[...]
        sums = []
        if has_pre:
            du_ref, h_ref, w_ref = ins.pop(0), ins.pop(0), ins.pop(0)
            dx, s = _rms_bwd_rows(h_ref[...], w_ref[...], du_ref[...])
            g = g + dx
            outs.pop(0)[...] = g
            sums.append(s)
        if has_post:
            m_ref, w_ref = ins.pop(0), ins.pop(0)
            dx, s = _rms_bwd_rows(m_ref[...], w_ref[...], g)
            outs.pop(0)[...] = dx.astype(BF16)
            sums.append(s)
        first = pl.program_id(0) == 0
        for o_ref, s in zip(outs, sums):
            @pl.when(first)
            def _(o_ref=o_ref, s=s):
                o_ref[...] = s

            @pl.when(jnp.logical_not(first))
            def _(o_ref=o_ref, s=s):
                o_ref[...] += s

    return _pcall(body, name=name, grid=(T // tb,), in_specs=in_specs, out_specs=tuple(out_specs),
                  out_shape=tuple(out_shape), compiler_params=_cp(("arbitrary",)))(*args)


def _ret_consts(lg, L):
    ii = lax.broadcasted_iota(jnp.int32, (L, L), 0).astype(F32)
    jj = lax.broadcasted_iota(jnp.int32, (L, L), 1).astype(F32)
    dmat = jnp.exp(jnp.abs(ii - jj) * lg)
    idx = lax.broadcasted_iota(jnp.int32, (L, 1), 0).astype(F32)
    xi = jnp.exp((idx + 1.0) * lg)
    zeta = jnp.exp((L - 1.0 - idx) * lg)
    cd = jnp.exp(jnp.full((1, 1), L, F32) * lg)
    return dmat, xi, zeta, cd


def _rot(t, cs, sn):
    half = t.shape[-1] // 2
    t1, t2 = t[:, :half], t[:, half:]
    return jnp.concatenate([t1 * cs - t2 * sn, t1 * sn + t2 * cs], axis=-1)


def _rot_bwd(d, cs, sn):
    half = d.shape[-1] // 2
    d1, d2 = d[:, :half], d[:, half:]
    return jnp.concatenate([d1 * cs + d2 * sn, d2 * cs - d1 * sn], axis=-1)


def _ret_specs(cf, tb, rev):
    H, DK, DV = cf.H, cf.DK, cf.DV
    ni = cf.T // tb
    ri = (lambda i: ni - 1 - i) if rev else (lambda i: i)
    q = pl.BlockSpec((tb, DK), lambda h, i: (ri(i), h))
    k = pl.BlockSpec((tb, DK), lambda h, i: (ri(i), H + h))
    v = pl.BlockSpec((tb, DV), lambda h, i: (ri(i), cf.QK * 2 // DV + h))
    g = pl.BlockSpec((tb, DV), lambda h, i: (ri(i), cf.QK * 2 // DV + H + h))
    cs = pl.BlockSpec((tb, DK // 2), lambda h, i: (ri(i), 0))
    gw = pl.BlockSpec((1, DV), lambda h, i: (0, h))
    row_v = pl.BlockSpec((tb, DV), lambda h, i: (ri(i), h))
    row_k = pl.BlockSpec((tb, DK), lambda h, i: (ri(i), h))
    st = pl.BlockSpec((tb // cf.L, None, DK, DV), lambda h, i: (ri(i), h, 0, 0))
    lgs = pl.BlockSpec(memory_space=pltpu.SMEM)
    return q, k, v, g, cs, gw, row_v, row_k, st, lgs


def _ret_fwd(cf, proj, cos, sin, gn_w, lgam, rider=None):
    T, L, H, DK, DV = cf.T, cf.L, cf.H, cf.DK, cf.DV
    tb = _tile(T, 512, L)
    nck = tb // L
    q_s, k_s, v_s, g_s, cs_s, gw_s, row_v, _, st_s, lg_s = _ret_specs(cf, tb, False)
    kscale = DK ** -0.5

    def body(lg_ref, q_ref, k_ref, v_ref, g_ref, cos_ref, sin_ref, gw_ref, y_ref, o_ref, st_ref, state):
        h = pl.program_id(0)

        @pl.when(pl.program_id(1) == 0)
        def _():
            state[...] = jnp.zeros_like(state)

        dmat, xi, zeta, cd = _ret_consts(lg_ref[h], L)
        gw = gw_ref[...]

        def chunk(c, carry):
            rows = pl.ds(pl.multiple_of(c * L, L), L)
            cs, sn = cos_ref[rows, :], sin_ref[rows, :]
            qr = _rot(q_ref[rows, :], cs, sn)
            kr = _rot(k_ref[rows, :], cs, sn) * kscale
            qb, kb = qr.astype(BF16), kr.astype(BF16)
            vb = v_ref[rows, :].astype(BF16)
            st = state[...]
            stb = st.astype(BF16)
            st_ref[c] = stb
            s = _dot(qb, kb, NT) * dmat
            o = _dot(s.astype(BF16), vb) + _dot(qb, stb) * xi
            state[...] = st * cd + _dot((kr * zeta).astype(BF16), vb, TN)
            o_ref[rows, :] = o
            mu = jnp.mean(o, axis=-1, keepdims=True)
            oc = o - mu
            var = jnp.mean(oc * oc, axis=-1, keepdims=True)
            n = oc * lax.rsqrt(var + GN_EPS) * gw
            gt = g_ref[rows, :]
            y_ref[rows, :] = (gt * _sigmoid(gt) * n).astype(BF16)
            return carry

        lax.fori_loop(0, nck, chunk, 0)

    return _host_call(
        body, name="ret_fwd", grid=(H, T // tb),
        in_specs=[lg_s, q_s, k_s, v_s, g_s, cs_s, cs_s, gw_s],
        out_specs=(row_v, row_v, st_s),
        out_shape=(jax.ShapeDtypeStruct((T, cf.V), BF16), jax.ShapeDtypeStruct((T, cf.V), F32),
                   jax.ShapeDtypeStruct((cf.NC, H, DK, DV), BF16)),
        scratch=[pltpu.VMEM((DK, DV), F32)], args=(lgam, proj, proj, proj, proj, cos, sin, gn_w),
        sem=("parallel", "arbitrary"), rider=rider)


def _ret_bwd(cf, proj, cos, sin, gn_w, lgam, o, states, dy):
    T, L, H, DK, DV = cf.T, cf.L, cf.H, cf.DK, cf.DV
    tb = _tile(T, 512, L)
    nck = tb // L
    q_s, k_s, v_s, g_s, cs_s, gw_s, row_v, row_k, st_s, lg_s = _ret_specs(cf, tb, True)
    kscale = DK ** -0.5

    def body(lg_ref, q_ref, k_ref, v_ref, g_ref, cos_ref, sin_ref, gw_ref, o_ref, st_ref, dy_ref,
             dq_ref, dk_ref, dv_ref, dg_ref, dgw_ref, dstate):
        h = pl.program_id(0)

        @pl.when(pl.program_id(1) == 0)
        def _():
            dstate[...] = jnp.zeros_like(dstate)
            dgw_ref[...] = jnp.zeros_like(dgw_ref)

        dmat, xi, zeta, cd = _ret_consts(lg_ref[h], L)
        gw = gw_ref[...]

        def chunk(t, carry):
            c = nck - 1 - t
            rows = pl.ds(pl.multiple_of(c * L, L), L)
            cs, sn = cos_ref[rows, :], sin_ref[rows, :]
            qr = _rot(q_ref[rows, :], cs, sn)
            kr = _rot(k_ref[rows, :], cs, sn) * kscale
            qb, kb = qr.astype(BF16), kr.astype(BF16)
            kzb = (kr * zeta).astype(BF16)
            vb = v_ref[rows, :].astype(BF16)
            s = (_dot(qb, kb, NT) * dmat).astype(BF16)
            oo = o_ref[rows, :]
            mu = jnp.mean(oo, axis=-1, keepdims=True)
            oc = oo - mu
            rstd = lax.rsqrt(jnp.mean(oc * oc, axis=-1, keepdims=True) + GN_EPS)
            oh = oc * rstd
            gt = g_ref[rows, :]
            sg = _sigmoid(gt)
            dyv = dy_ref[rows, :]
            dn = dyv * (gt * sg)
            dg_ref[rows, :] = (dyv * (oh * gw) * (sg * (1.0 + gt * (1.0 - sg)))).astype(BF16)
            dgw_ref[...] += jnp.sum(dn * oh, axis=0, keepdims=True)
            doh = dn * gw
            do = rstd * (doh - jnp.mean(doh, axis=-1, keepdims=True) - oh * jnp.mean(doh * oh, axis=-1, keepdims=True))
            dob = do.astype(BF16)
            doxb = (do * xi).astype(BF16)
            dst = dstate[...]
            dstb = dst.astype(BF16)
            stb = st_ref[c]
            dv_ref[rows, :] = (_dot(s, dob, TN) + _dot(kzb, dstb)).astype(BF16)
            ds = (_dot(dob, vb, NT) * dmat).astype(BF16)
            dqr = _dot(ds, kb) + _dot(doxb, stb, NT)
            dkr = _dot(ds, qb, TN) + _dot(vb, dstb, NT) * zeta
            dstate[...] = dst * cd + _dot(qb, doxb, TN)
            dq_ref[rows, :] = _rot_bwd(dqr, cs, sn).astype(BF16)
            dk_ref[rows, :] = _rot_bwd(dkr * kscale, cs, sn).astype(BF16)
            return carry

        lax.fori_loop(0, nck, chunk, 0)

    return _pcall(
        body, name="ret_bwd", grid=(H, T // tb),
        in_specs=[lg_s, q_s, k_s, v_s, g_s, cs_s, cs_s, gw_s, row_v, st_s, row_v],
        out_specs=(row_k, row_k, row_v, row_v, gw_s),
        out_shape=(jax.ShapeDtypeStruct((T, cf.QK), BF16), jax.ShapeDtypeStruct((T, cf.QK), BF16),
                   jax.ShapeDtypeStruct((T, cf.V), BF16), jax.ShapeDtypeStruct((T, cf.V), BF16),
                   jax.ShapeDtypeStruct((1, cf.V), F32)),
        scratch_shapes=[pltpu.VMEM((DK, DV), F32)],
        compiler_params=_cp(("parallel", "arbitrary")),
    )(lgam, proj, proj, proj, proj, cos, sin, gn_w, o, states, dy)


def _conv_pre(x, halo, w, b, first, W):
    tb = x.shape[0]
    ext = jnp.concatenate([jnp.where(first, 0.0, halo), x], axis=0)
    out = b + w[W - 1:W, :] * x
    for tap in range(W - 1):
        out = out + w[tap:tap + 1, :] * pltpu.roll(ext, W - 1 - tap, 0)[SUBLANES:SUBLANES + tb, :]
    return out, ext


def _conv_fwd(cf, proj, conv_w, conv_b):
    T, CD, W = cf.T, cf.CD, SSD_CONV_W
    tb = _tile(T, 512, SUBLANES)
    tc = _tile(CD, 512, LANES)
    off = cf.DI // tc
    nh = tb // SUBLANES

    def body(x_ref, halo_ref, w_ref, b_ref, o_ref):
        pre, _ = _conv_pre(x_ref[...], halo_ref[...], w_ref[...], b_ref[...], pl.program_id(1) == 0, W)
        o_ref[...] = pre * _sigmoid(pre)

    return _pcall(
        body, name="conv_fwd", grid=(CD // tc, T // tb),
        in_specs=[pl.BlockSpec((tb, tc), lambda j, i: (i, off + j)),
                  pl.BlockSpec((SUBLANES, tc), lambda j, i: (jnp.maximum(i * nh - 1, 0), off + j)),
                  pl.BlockSpec((W, tc), lambda j, i: (0, j)), pl.BlockSpec((1, tc), lambda j, i: (0, j))],
        out_specs=pl.BlockSpec((tb, tc), lambda j, i: (i, j)),
        out_shape=jax.ShapeDtypeStruct((T, CD), F32),
        compiler_params=_cp(("parallel", "arbitrary")),
    )(proj, proj, conv_w, conv_b)


def _conv_bwd_pre(cf, proj, conv_w, conv_b, dact):
    T, CD, W = cf.T, cf.CD, SSD_CONV_W
    tb = _tile(T, 512, SUBLANES)
    tc = _tile(CD, 512, LANES)
    off = cf.DI // tc
    nh = tb // SUBLANES

    def body(x_ref, halo_ref, w_ref, b_ref, da_ref, dp_ref, dw_ref, db_ref):
        x = x_ref[...]
        pre, ext = _conv_pre(x, halo_ref[...], w_ref[...], b_ref[...], pl.program_id(1) == 0, W)
        sg = _sigmoid(pre)
        dp = da_ref[...] * (sg * (1.0 + pre * (1.0 - sg)))
        dp_ref[...] = dp
        rows = [jnp.sum(dp * pltpu.roll(ext, W - 1 - tap, 0)[SUBLANES:SUBLANES + tb, :], axis=0, keepdims=True)
                for tap in range(W - 1)]
        rows.append(jnp.sum(dp * x, axis=0, keepdims=True))
        dw = jnp.concatenate(rows, axis=0)
        db = jnp.sum(dp, axis=0, keepdims=True)

        @pl.when(pl.program_id(1) == 0)
        def _():
            dw_ref[...] = dw
            db_ref[...] = db

        @pl.when(pl.program_id(1) > 0)
        def _():
            dw_ref[...] += dw
            db_ref[...] += db

    return _pcall(
        body, name="conv_bwd_pre", grid=(CD // tc, T // tb),
        in_specs=[pl.BlockSpec((tb, tc), lambda j, i: (i, off + j)),
                  pl.BlockSpec((SUBLANES, tc), lambda j, i: (jnp.maximum(i * nh - 1, 0), off + j)),
                  pl.BlockSpec((W, tc), lambda j, i: (0, j)), pl.BlockSpec((1, tc), lambda j, i: (0, j)),
                  pl.BlockSpec((tb, tc), lambda j, i: (i, j))],
        out_specs=(pl.BlockSpec((tb, tc), lambda j, i: (i, j)), pl.BlockSpec((W, tc), lambda j, i: (0, j)),
                   pl.BlockSpec((1, tc), lambda j, i: (0, j))),
        out_shape=(jax.ShapeDtypeStruct((T, CD), F32), jax.ShapeDtypeStruct((W, CD), F32),
                   jax.ShapeDtypeStruct((1, CD), F32)),
        compiler_params=_cp(("parallel", "arbitrary")),
    )(proj, proj, conv_w, conv_b, dact)


def _conv_bwd_x(cf, dpre, conv_w):
    T, CD, W = cf.T, cf.CD, SSD_CONV_W
    tb = _tile(T, 512, SUBLANES)
    tc = _tile(CD, 512, LANES)
    nh = tb // SUBLANES
    last_blk = T // SUBLANES - 1
    ni = T // tb

    def body(d_ref, halo_ref, w_ref, o_ref):
        d = d_ref[...]
        w = w_ref[...]
        nxt = jnp.where(pl.program_id(1) == ni - 1, 0.0, halo_ref[...])
        ext = jnp.concatenate([d, nxt], axis=0)
        n = tb + SUBLANES
        out = w[W - 1:W, :] * d
        for tap in range(W - 1):
            out = out + w[tap:tap + 1, :] * pltpu.roll(ext, n - (W - 1 - tap), 0)[:tb, :]
        o_ref[...] = out.astype(BF16)

    return _pcall(
        body, name="conv_bwd_x", grid=(CD // tc, ni),
        in_specs=[pl.BlockSpec((tb, tc), lambda j, i: (i, j)),
                  pl.BlockSpec((SUBLANES, tc), lambda j, i: (jnp.minimum((i + 1) * nh, last_blk), j)),
                  pl.BlockSpec((W, tc), lambda j, i: (0, j))],
        out_specs=pl.BlockSpec((tb, tc), lambda j, i: (i, j)),
        out_shape=jax.ShapeDtypeStruct((T, CD), BF16),
        compiler_params=_cp(("parallel", "arbitrary")),
    )(dpre, dpre, conv_w)


def _ssd_masks(cf, g, tb):
    L, GW, P, RG = cf.L, cf.GW, cf.P, cf.RG
    assert L == P and 2 * L == LANES and RG % 2 == 0
    i32 = jnp.int32
    hrow = lax.broadcasted_iota(i32, (LANES, GW), 0)
    hcol = lax.broadcasted_iota(i32, (LANES, GW), 1) // P
    expand = (hrow == g * RG + hcol).astype(F32)
    ti = lax.broadcasted_iota(i32, (tb, tb), 0)
    tj = lax.broadcasted_iota(i32, (tb, tb), 1)
    btril = jnp.logical_and(ti // L == tj // L, ti >= tj).astype(BF16)
    r0 = lax.broadcasted_iota(i32, (L, GW), 0)
    c0 = lax.broadcasted_iota(i32, (L, GW), 1) % L
    tile_eye = (r0 == c0).astype(F32)
    lower = r0 >= c0
    p0 = lax.broadcasted_iota(i32, (2 * L, LANES), 0) // L
    p1 = lax.broadcasted_iota(i32, (2 * L, LANES), 1) // P
    pair = (p0 == p1).astype(F32)
    return expand, btril, tile_eye, lower, pair


def _softplus(x):
    return jnp.maximum(x, 0.0) + jnp.log1p(jnp.exp(-jnp.abs(x)))


def _dot_exact(mask, x, dn=NN):
    hi = x.astype(BF16)
    r1 = x - hi.astype(F32)
    mid = r1.astype(BF16)
    lo = (r1 - mid.astype(F32)).astype(BF16)
    return (_dot(mask, lo, dn) + _dot(mask, mid, dn)) + _dot(mask, hi, dn)


def _ssd_chunk(cf, mk, acum, dt, xs, bm, cm):
    _, _, tile_eye, lower, pair = mk
    rowv = jnp.sum(acum * tile_eye, axis=0, keepdims=True)
    lf = jnp.exp(jnp.where(lower, acum - rowv, -1e30))
    xdt = xs * dt
    bb, cb_ = bm.astype(BF16), cm.astype(BF16)
    bb2 = jnp.concatenate([bb, bb], axis=0)
    cb2 = _dot(cb_, bb2, NT)
    ms, bds = [], []
    for j in range(cf.RG // 2):
        ln = slice(j * LANES, (j + 1) * LANES)
        ms.append((cb2 * lf[:, ln]).astype(BF16))
        xp = xdt[:, ln]
        bds.append((jnp.concatenate([xp, xp], axis=0) * pair).astype(BF16))
    return lf, xdt, bb, cb_, bb2, cb2, ms, bds


def _ssd_specs(cf, tb, rev):
    G, GW, N = cf.G, cf.GW, cf.N
    ni = cf.T // tb
    ri = (lambda i: ni - 1 - i) if rev else (lambda i: i)
    z = pl.BlockSpec((tb, GW), lambda g, i: (ri(i), g))
    dt = pl.BlockSpec((tb, LANES), lambda g, i: (ri(i), (cf.DI + cf.CD) // LANES))
    xs = pl.BlockSpec((tb, GW), lambda g, i: (ri(i), g))
    bm = pl.BlockSpec((tb, N), lambda g, i: (ri(i), cf.DI // N + g))
    cm = pl.BlockSpec((tb, N), lambda g, i: (ri(i), cf.DI // N + G + g))
    vec = pl.BlockSpec((1, GW), lambda g, i: (0, g))
    st = pl.BlockSpec((tb // cf.L, None, N, GW), lambda g, i: (ri(i), g, 0, 0))
    return z, dt, xs, bm, cm, vec, st


def _ssd_fwd(cf, proj, xact, bias_e, alog_e, dskip_e, norm_w):
    T, L, G, GW, N = cf.T, cf.L, cf.G, cf.GW, cf.N
    tb = _tile(T, 512, L)
    nck = tb // L
    z_s, dt_s, xs_s, b_s, c_s, vec_s, st_s = _ssd_specs(cf, tb, False)

    def body(z_ref, dt_ref, xs_ref, b_ref, c_ref, bias_ref, alog_ref, dsk_ref, nw_ref, y_ref, yp_ref, st_ref,
             state, dt_s, ac_s):
        @pl.when(pl.program_id(1) == 0)
        def _():
            state[...] = jnp.zeros_like(state)

        mk = _ssd_masks(cf, pl.program_id(0), tb)
        a_e = -jnp.exp(alog_ref[...])
        dt_all = _softplus(_dot(dt_ref[...], mk[0], NN, HI) + bias_ref[...])
        dt_s[...] = dt_all
        ac_s[...] = _dot_exact(mk[1], dt_all * a_e)

        def chunk(c, carry):
            rows = pl.ds(pl.multiple_of(c * L, L), L)
            acum = ac_s[rows, :]
            lf, xdt, bb, cb_, _, _, ms, bds = _ssd_chunk(cf, mk, acum, dt_s[rows, :], xs_ref[rows, :],
                                                         b_ref[rows, :], c_ref[rows, :])
            st = state[...]
            stb = st.astype(BF16)
            st_ref[c] = stb
            ydiag = jnp.concatenate([_dot(m, bd) for m, bd in zip(ms, bds)], axis=1)
            al = acum[L - 1:L, :]
            state[...] = st * jnp.exp(al) + _dot(bb, (xdt * jnp.exp(al - acum)).astype(BF16), TN)
            yp_ref[rows, :] = ydiag + _dot(cb_, stb) * jnp.exp(acum)
            return carry

        lax.fori_loop(0, nck, chunk, 0, unroll=2 if nck % 2 == 0 else 1)
        z = z_ref[...]
        yg = (yp_ref[...] + dsk_ref[...] * xs_ref[...]) * (z * _sigmoid(z))
        y_ref[...] = (yg * _rstd(yg) * nw_ref[...]).astype(BF16)

    return _pcall(
        body, name="ssd_fwd", grid=(G, T // tb),
        in_specs=[z_s, dt_s, xs_s, b_s, c_s, vec_s, vec_s, vec_s, vec_s],
        out_specs=(z_s, z_s, st_s),
        out_shape=(jax.ShapeDtypeStruct((T, cf.DI), BF16), jax.ShapeDtypeStruct((T, cf.DI), F32),
                   jax.ShapeDtypeStruct((cf.NC, G, N, GW), BF16)),
        scratch_shapes=[pltpu.VMEM((N, GW), F32), pltpu.VMEM((tb, GW), F32), pltpu.VMEM((tb, GW), F32)],
        compiler_params=_cp(("parallel", "arbitrary")),
    )(proj, proj, xact, xact, xact, bias_e, alog_e, dskip_e, norm_w)


def _ssd_bwd(cf, proj, xact, bias_e, alog_e, dskip_e, norm_w, ypre, states, dy):
    T, L, G, GW, N, RG = cf.T, cf.L, cf.G, cf.GW, cf.N, cf.RG
    tb = _tile(T, 512, L)
    nck = tb // L
    ni = T // tb
    z_s, dt_s, xs_s, b_s, c_s, vec_s, st_s = _ssd_specs(cf, tb, True)
    bc_out = pl.BlockSpec((tb, N), lambda g, i: (ni - 1 - i, g))
    ddt_out = pl.BlockSpec((None, tb, LANES), lambda g, i: (g, ni - 1 - i, 0))

    def body(z_ref, dt_ref, xs_ref, b_ref, c_ref, bias_ref, alog_ref, dsk_ref, nw_ref, yp_ref, st_ref, dy_ref,
             dz_ref, dxs_ref, db_ref, dc_ref, ddt_ref, dnw_ref, ddsk_ref, dalog_ref, dbias_ref,
             dstate, dt_s, ac_s, sg_s, dys_s, dxdt_s, dac_s):
        @pl.when(pl.program_id(1) == 0)
        def _():
            dstate[...] = jnp.zeros_like(dstate)
            for r in (dnw_ref, ddsk_ref, dalog_ref, dbias_ref):
                r[...] = jnp.zeros_like(r)

        mk = _ssd_masks(cf, pl.program_id(0), tb)
        expand, btril, tile_eye, lower, pair = mk
        a_e = -jnp.exp(alog_ref[...])
        dsk, nw = dsk_ref[...], nw_ref[...]
        last_row = (lax.broadcasted_iota(jnp.int32, (L, 1), 0) == L - 1).astype(F32)
        raw = _dot(dt_ref[...], expand, NN, HI) + bias_ref[...]
        dt_all = _softplus(raw)
        dt_s[...] = dt_all
        sg_s[...] = _sigmoid(raw)
        ac_s[...] = _dot_exact(btril, dt_all * a_e)
        z = z_ref[...]
        sz = _sigmoid(z)
        silu = z * sz
        xs_all = xs_ref[...]
        yd = yp_ref[...] + dsk * xs_all
        yg = yd * silu
        rr = _rstd(yg)
        xh = yg * rr
        dout = dy_ref[...]
        dnw_ref[...] += jnp.sum(dout * xh, axis=0, keepdims=True)
        dxh = dout * nw
        dyg = rr * (dxh - xh * jnp.mean(dxh * xh, axis=-1, keepdims=True))
        dz_ref[...] = (dyg * yd * (sz * (1.0 + z * (1.0 - sz)))).astype(BF16)
        dys_all = dyg * silu
        dys_s[...] = dys_all
        ddsk_ref[...] += jnp.sum(dys_all * xs_all, axis=0, keepdims=True)

        def chunk(t, carry):
            c = nck - 1 - t
            rows = pl.ds(pl.multiple_of(c * L, L), L)
            acum = ac_s[rows, :]
            lf, xdt, bb, cb_, bb2, cb2, ms, bds = _ssd_chunk(cf, mk, acum, dt_s[rows, :], xs_ref[rows, :],
                                                             b_ref[rows, :], c_ref[rows, :])
            stb = st_ref[c]
            eac = jnp.exp(acum)
            al = acum[L - 1:L, :]
            eal = jnp.exp(al)
            dte = jnp.exp(al - acum)
            dys = dys_s[rows, :]
            dyb = dys.astype(BF16)
            dms, dxs_, dsegs = [], [], []
            dcb2 = None
            for j in range(RG // 2):
                ln = slice(j * LANES, (j + 1) * LANES)
                dyj = dyb[:, ln]
                dbd = _dot(ms[j], dyj, TN) * pair
                dxs_.append(dbd[:L, :] + dbd[L:, :])
                tj = _dot(dyj, bds[j], NT) * lf[:, ln]
                dcb2 = tj if dcb2 is None else dcb2 + tj
                dsegs.append(tj * cb2)
            dxdt = jnp.concatenate(dxs_, axis=1)
            dseg = jnp.concatenate(dsegs, axis=1)
            dcb2 = dcb2.astype(BF16)
            dcm = _dot(dcb2, bb2)
            dbm2 = _dot(dcb2, cb_, TN)
            dbm = dbm2[:L, :] + dbm2[L:, :]
            dacum = dseg - tile_eye * jnp.sum(dseg, axis=0, keepdims=True)
            dyo = (dys * eac).astype(BF16)
            dcm = dcm + _dot(dyo, stb, NT)
            dacum = dacum + dys * _dot(cb_, stb) * eac
            dst = dstate[...]
            dstb = dst.astype(BF16)
            xd = xdt * dte
            dbm = dbm + _dot(xd.astype(BF16), dstb, NT)
            dxd = _dot(bb, dstb)
            dal = jnp.sum(dst * stb.astype(F32), axis=0, keepdims=True) * eal
            dxdt = dxdt + dxd * dte
            tt = dxd * xd
            dacum = dacum - tt + last_row * (dal + jnp.sum(tt, axis=0, keepdims=True))
            dstate[...] = dst * eal + _dot(cb_, dyo, TN)
            dxdt_s[rows, :] = dxdt
            dac_s[rows, :] = dacum
            db_ref[rows, :] = dbm
            dc_ref[rows, :] = dcm
            return carry

        lax.fori_loop(0, nck, chunk, 0, unroll=2 if nck % 2 == 0 else 1)
        dda = _dot_exact(btril, dac_s[...], TN)
        dxdt_all = dxdt_s[...]
        dt_all = dt_s[...]
        dxs_ref[...] = dys_s[...] * dsk + dxdt_all * dt_all
        ddt = dxdt_all * xs_ref[...] + dda * a_e
        dalog_ref[...] += jnp.sum(dda * dt_all, axis=0, keepdims=True) * a_e
        draw = ddt * sg_s[...]
        dbias_ref[...] += jnp.sum(draw, axis=0, keepdims=True)
        ddt_ref[...] = _dot(draw, expand, NT, HI)

    GN = G * N
    return _pcall(
        body, name="ssd_bwd", grid=(G, ni),
        in_specs=[z_s, dt_s, xs_s, b_s, c_s, vec_s, vec_s, vec_s, vec_s, z_s, st_s, z_s],
        out_specs=(z_s, z_s, bc_out, bc_out, ddt_out, vec_s, vec_s, vec_s, vec_s),
        out_shape=(jax.ShapeDtypeStruct((T, cf.DI), BF16), jax.ShapeDtypeStruct((T, cf.DI), F32),
                   jax.ShapeDtypeStruct((T, GN), F32), jax.ShapeDtypeStruct((T, GN), F32),
                   jax.ShapeDtypeStruct((G, T, LANES), F32)) + (jax.ShapeDtypeStruct((1, cf.DI), F32),) * 4,
        scratch_shapes=[pltpu.VMEM((N, GW), F32)] + [pltpu.VMEM((tb, GW), F32)] * 6,
        compiler_params=_cp(("parallel", "arbitrary")),
    )(proj, proj, xact, xact, xact, bias_e, alog_e, dskip_e, norm_w, ypre, states, dy)


def _sum_groups(parts, name):
    G, T, W = parts.shape
    tb = _tile(T, 512, SUBLANES)

    def body(p_ref, o_ref):
        acc = p_ref[0]
        for g in range(1, G):
            acc = acc + p_ref[g]
        o_ref[...] = acc.astype(BF16)

    return _pcall(body, name=name, grid=(T // tb,), in_specs=[pl.BlockSpec((G, tb, W), lambda i: (0, i, 0))],
                  out_specs=pl.BlockSpec((tb, W), lambda i: (i, 0)), out_shape=jax.ShapeDtypeStruct((T, W), BF16),
                  compiler_params=_cp(("parallel",)))(parts)


def _all_gather(x, name):
    def body(x_ref, out_ref, send_sems, recv_sems, local_sem):
        x, y, c = lax.axis_index("x"), lax.axis_index("y"), lax.axis_index("c")
        me, sibling = (x, y, c), (x, y, 1 - c)
        chips = [(1 - x, y), (x, 1 - y), (1 - x, 1 - y)]

        def blk(px, py, pc):
            return out_ref.at[4 * px + 2 * py + pc]

        def copy(k, block, to, src=None):
            return pltpu.make_async_remote_copy(
                src_ref=blk(*block) if src is None else src, dst_ref=blk(*block),
                send_sem=send_sems.at[k], recv_sem=recv_sems.at[k], device_id=to, device_id_type=MESH)

        mine = pltpu.make_async_copy(x_ref, blk(*me), local_sem)
        mine.start()
        first = [copy(0, me, sibling, src=x_ref)]
        first += [copy(1 + j, me, (*chip, c), src=x_ref) for j, chip in enumerate(chips)]
        for cp in first:
            cp.start()
        passed = [copy(4 + j, (*chip, c), sibling) for j, chip in enumerate(chips)]
        for j, chip in enumerate(chips):
            copy(1 + j, (*chip, c), me).wait_recv()
            passed[j].start()
        copy(0, sibling, me).wait_recv()
        for j, chip in enumerate(chips):
            copy(4 + j, (*chip, 1 - c), me).wait_recv()
        for cp in first + passed:
            cp.wait_send()
        mine.wait()

    return _pcall(
        body, name=name, in_specs=[ANY], out_specs=ANY,
        out_shape=jax.ShapeDtypeStruct((N_DEV,) + x.shape, x.dtype),
        scratch_shapes=[pltpu.SemaphoreType.DMA((7,)), pltpu.SemaphoreType.DMA((7,)), pltpu.SemaphoreType.DMA],
    )(x)


def _core_sum(g, recv, idx, name):
    _, _, a, b = g.shape
    tr = _tile(a, max(SUBLANES, (2 * 2**20) // (4 * b) // SUBLANES * SUBLANES), SUBLANES)

    def body(idx_ref, g_ref, r_ref, p_ref, own_ref):
        s = g_ref[...] + r_ref[...]
        p_ref[...] = s.astype(BF16)

        @pl.when(pl.program_id(1) == idx_ref[1])
        def _():
            own_ref[...] = s

    return _pcall(
        body, name=name,
        grid_spec=pltpu.PrefetchScalarGridSpec(
            num_scalar_prefetch=1, grid=(a // tr, 4),
            in_specs=[pl.BlockSpec((None, None, tr, b), lambda i, q, idx: (q, idx[0], i, 0)),
                      pl.BlockSpec((None, tr, b), lambda i, q, idx: (q, i, 0))],
            out_specs=(pl.BlockSpec((None, tr, b), lambda i, q, idx: (q, i, 0)),
                       pl.BlockSpec((tr, b), lambda i, q, idx: (i, 0)))),
        out_shape=(jax.ShapeDtypeStruct((4, a, b), BF16), jax.ShapeDtypeStruct((a, b), F32)),
        compiler_params=_cp(("parallel", "arbitrary")),
    )(idx, g, recv)


def _adam_math(w, g, m, v):
    m = ADAM_B1 * m + (1.0 - ADAM_B1) * g
    v = ADAM_B2 * v + (1.0 - ADAM_B2) * (g * g)
    m_hat = m / (1.0 - ADAM_B1 ** ADAM_STEP)
    v_hat = v / (1.0 - ADAM_B2 ** ADAM_STEP)
    delta = -ADAM_LR * (m_hat / (jnp.sqrt(v_hat) + ADAM_EPS) + ADAM_WD * w)
    return delta, m, v


def _chip_sum_adam(own, recv, w, m, v, layer, name):
    a, b = own.shape
    tr = _tile(a, max(SUBLANES, (2**20) // (4 * b) // SUBLANES * SUBLANES), SUBLANES)
    if layer is None:
        wspec = pl.BlockSpec((tr, b), lambda i: (i, 0))
    else:
        wspec = pl.BlockSpec((None, tr, b), lambda i: (layer, i, 0))
    ospec = pl.BlockSpec((tr, b), lambda i: (i, 0))

    def body(own_ref, r_ref, w_ref, m_ref, v_ref, g_ref, d_ref, mo_ref, vo_ref):
        g = own_ref[...]
        for k in range(3):
            g = g + r_ref[k].astype(F32)
        g_ref[...] = g
        d_ref[...], mo_ref[...], vo_ref[...] = _adam_math(w_ref[...], g, m_ref[...], v_ref[...])

    return _pcall(
        body, name=name, grid=(a // tr,),
        in_specs=[ospec, pl.BlockSpec((3, tr, b), lambda i: (0, i, 0)), wspec, wspec, wspec],
        out_specs=(ospec,) * 4, out_shape=(jax.ShapeDtypeStruct((a, b), F32),) * 4,
        compiler_params=_cp(("parallel",)),
    )(own, recv, w, m, v)


def _all_reduce_small(x, n_fold, fold_w, name):
    R, W = x.shape

    def body(x_ref, out_ref, buf, send_sems, recv_sems):
        xx, y, c = lax.axis_index("x"), lax.axis_index("y"), lax.axis_index("c")
        me = 4 * xx + 2 * y + c
        buf[me] = x_ref[...]
        copies = []
        for k in range(1, N_DEV):
            px, py, pc = xx ^ (k >> 2), y ^ ((k >> 1) & 1), c ^ (k & 1)
            copies.append(pltpu.make_async_remote_copy(
                src_ref=x_ref, dst_ref=buf.at[me], send_sem=send_sems.at[k - 1], recv_sem=recv_sems.at[k - 1],
                device_id=(px, py, pc), device_id_type=MESH))
        for cp in copies:
            cp.start()
        for cp in copies:
            cp.wait()
        acc = buf[0]
        for j in range(1, N_DEV):
            acc = acc + buf[j]
        out_ref[...] = acc
        if n_fold:
            l0 = lax.broadcasted_iota(jnp.int32, (W, W), 0) // fold_w
            l1 = lax.broadcasted_iota(jnp.int32, (W, W), 1)
            fold = (l0 == l1).astype(F32)
            out_ref[R - n_fold:, :] = _dot(acc[R - n_fold:, :], fold, NN, HI)

    return _pcall(
        body, name=name, in_specs=[pl.BlockSpec(memory_space=pltpu.VMEM)],
        out_specs=pl.BlockSpec(memory_space=pltpu.VMEM), out_shape=jax.ShapeDtypeStruct((R, W), F32),
        scratch_shapes=[pltpu.VMEM((N_DEV, R, W), F32), pltpu.SemaphoreType.DMA((N_DEV - 1,)),
                        pltpu.SemaphoreType.DMA((N_DEV - 1,))],
        compiler_params=pltpu.CompilerParams(vmem_limit_bytes=VMEM_LIMIT),
    )(x)


def _adam_small(w, g, m, v, name):
    def body(w_ref, g_ref, m_ref, v_ref, d_ref, mo_ref, vo_ref):
        d_ref[...], mo_ref[...], vo_ref[...] = _adam_math(w_ref[...], g_ref[...], m_ref[...], v_ref[...])

    return _pcall(body, name=name, out_shape=(jax.ShapeDtypeStruct(w.shape, F32),) * 3)(w, g, m, v)


def kernel(x, norm_mix_pre, norm_mix_post, norm_ffn_pre, norm_ffn_post, ret_w_in, ret_gn_w, ret_w_out, ssd_w_in, ssd_conv_w, ssd_conv_b, ssd_dt_bias, ssd_a_log, ssd_d, ssd_norm_w, ssd_w_out, mlp_w_up, mlp_w_down, loss_target, m_norm_mix_pre, m_norm_mix_post, m_norm_ffn_pre, m_norm_ffn_post, m_ret_w_in, m_ret_gn_w, m_ret_w_out, m_ssd_w_in, m_ssd_conv_w, m_ssd_conv_b, m_ssd_dt_bias, m_ssd_a_log, m_ssd_d, m_ssd_norm_w, m_ssd_w_out, m_mlp_w_up, m_mlp_w_down, v_norm_mix_pre, v_norm_mix_post, v_norm_ffn_pre, v_norm_ffn_post, v_ret_w_in, v_ret_gn_w, v_ret_w_out, v_ssd_w_in, v_ssd_conv_w, v_ssd_conv_b, v_ssd_dt_bias, v_ssd_a_log, v_ssd_d, v_ssd_norm_w, v_ssd_w_out, v_mlp_w_up, v_mlp_w_down):
    cf = _cfg()
    T, D = cf.T, cf.D
    ax, ay, ac = lax.axis_index("x"), lax.axis_index("y"), lax.axis_index("c")
    my_dev = 4 * ax + 2 * ay + ac
    idx = jnp.stack([ac, 2 * ax + ay]).astype(jnp.int32)

    w_ri = _all_gather(ret_w_in[0].astype(BF16), "ag_ret_w_in")
    chips_of = lambda ws: _rider_join([_r_gather_chips(w.astype(BF16)) for w in ws])
    cores_of = lambda bufs: _rider_join([_r_gather_cores(b) for b in bufs])
    cw, cbw, nww = cf.CD // N_DEV, cf.CD // N_DEV, cf.DI // N_DEV
    small = jnp.concatenate([ssd_conv_w[0], ssd_conv_b, jnp.pad(ssd_norm_w, ((0, 0), (0, cw - nww))),
                             jnp.zeros((2, cw), F32)], axis=0)
    small = _all_gather(small, "ag_ssd_small")
    conv_w = jnp.transpose(small[:, :SSD_CONV_W, :], (1, 0, 2)).reshape(SSD_CONV_W, cf.CD)
    conv_b = small[:, SSD_CONV_W, :].reshape(1, cf.CD)
    ssd_nw = small[:, SSD_CONV_W + 1, :nww].reshape(1, cf.DI)

    half = cf.DK // 2
    inv_freq = ROPE_BASE ** (-jnp.arange(half, dtype=F32) / half)
    ang = jnp.arange(T).astype(F32)[:, None] * inv_freq[None, :]
    cos, sin = jnp.cos(ang), jnp.sin(ang)
    lgam = jnp.log1p(-jnp.exp2(-5.0 - jnp.arange(cf.H, dtype=F32)))
    rep = lambda p: jnp.repeat(p.reshape(1, cf.SH), cf.P, axis=1)
    bias_e, alog_e, dskip_e = rep(ssd_dt_bias), rep(ssd_a_log), rep(ssd_d)

    h0 = x.reshape(T, D)
    tgt = loss_target.reshape(T, D)
    nrm = lambda p, i: p[i:i + 1]

    u0 = _rms_fwd(h0, nrm(norm_mix_pre, 0), "rms_fwd0")
    proj0, part_a = _mm(u0, w_ri, kind="nn", name="mm_ret_in",
                        rider=chips_of([ret_w_out[0], mlp_w_up[0], mlp_w_down[0]]))
    (y0, o0, st0), got = _ret_fwd(cf, proj0, cos, sin, ret_gn_w, lgam, rider=_rider_join(
        [cores_of(part_a), chips_of([ssd_w_in[0], ssd_w_out[0], mlp_w_up[1]])]))
    w_ro, w_up0, w_dn0 = got[0].reshape(cf.V, D), got[1], got[2].reshape(cf.FF, D)
    part_b = got[3:]
    m0 = _mm(y0, w_ro, kind="nn", name="mm_ret_out", tr=cf.V)
    h1, u1 = _resid_fwd(h0, m0, nrm(norm_mix_post, 0), nrm(norm_ffn_pre, 0), "resid_fwd0")
    (sq0, act0), part_c = _mm(u1, w_up0, kind="nn", name="mm_up0", epi="relu2", rider=chips_of([mlp_w_down[1]]))
    f0, got = _mm(sq0, w_dn0, kind="nn", name="mm_down0", tj=1024, rider=cores_of(part_b + part_c))
    w_si = jnp.pad(jnp.transpose(got[0], (1, 0, 2)).reshape(D, cf.SIN), ((0, 0), (0, cf.SINP - cf.SIN)))
    w_so, w_up1, w_dn1 = got[1].reshape(cf.DI, D), got[2], got[3].reshape(cf.FF, D)
    w_up, w_dn = [w_up0, w_up1], [w_dn0, w_dn1]
    h2, u2 = _resid_fwd(h1, f0, nrm(norm_ffn_post, 0), nrm(norm_mix_pre, 1), "resid_fwd1")
    proj1 = _mm(u2, w_si, kind="nn", name="mm_ssd_in", tj=1152)
    xact = _conv_fwd(cf, proj1, conv_w, conv_b)
    y1, yp1, st1 = _ssd_fwd(cf, proj1, xact, bias_e, alog_e, dskip_e, ssd_nw)
    m1 = _mm(y1, w_so, kind="nn", name="mm_ssd_out", tr=cf.DI)
    h3, u3 = _resid_fwd(h2, m1, nrm(norm_mix_post, 1), nrm(norm_ffn_pre, 1), "resid_fwd2")
    sq1, act1 = _mm(u3, w_up[1], kind="nn", name="mm_up1", epi="relu2")
    f1 = _mm(sq1, w_dn[1], kind="nn", name="mm_down1", tj=1024)
    g4, lsum = _final_fwd(h3, f1, nrm(norm_ffn_post, 1), tgt, "final_fwd")
    loss = lax.psum(0.5 * jnp.sum(lsum) / D, ("x", "y", "c"))

    as4 = lambda g: g.reshape(4, 2, g.shape[1], g.shape[2])
    swap_cores = lambda g: _r_swap_cores(as4(g))
    core_sum = lambda g, recv, name: _core_sum(as4(g), recv, idx, name + "_core_sum")

    def mlp_bwd(l, df, u, sq, act, rider=None, then=None):
        dpre = _mm(df, w_dn[l], kind="nt", name=f"mm_dpre{l}", out_dtype=BF16, epi="mul2act", epi_in=act, rider=rider)
        dpre, got = dpre if rider is not None else (dpre, [])
        g_dn = _mm(sq, df, kind="tn", name=f"mm_gdown{l}", tj=1024, rider=then(got) if then else None)
        g_dn, got = g_dn if then else (g_dn, [])
        g_dn = g_dn.reshape(N_DEV, cf.FF // N_DEV, D)
        du, (rc,) = _mm(dpre, w_up[l], kind="nt", name=f"mm_du_mlp{l}", tj=1024, tr=1024, rider=swap_cores(g_dn))
        part, own = core_sum(g_dn, rc, f"rs_mlp_down{l}")
        g_up, (r2,) = _mm(u, dpre, kind="tn", name=f"mm_gup{l}", tj=1024, out_nblk=N_DEV, rider=_r_swap_chips(part))
        return du, g_up, own, r2, got

    df1, g_nfpost1 = _norm_bwd(g4, "norm_bwd4", m=f1, w_post=nrm(norm_ffn_post, 1))
    du3, g_up1, own_dn1, r2_dn1, _ = mlp_bwd(1, df1, u3, sq1, act1, None)
    gh3, dm1, g_nfpre1, g_nmpost1 = _norm_bwd(g4, "norm_bwd3", du=du3, h=h3, w_pre=nrm(norm_ffn_pre, 1),
                                              m=m1, w_post=nrm(norm_mix_post, 1))
    dy1, (rc,) = _mm(dm1, w_so, kind="nt", name="mm_dy_ssd", rider=swap_cores(g_up1))
    part, own_up1 = core_sum(g_up1, rc, "rs_mlp_up1")
    g_so, (r2_up1,) = _mm(y1, dm1, kind="tn", name="mm_g_ssd_out", tj=1024, rider=_r_swap_chips(part))
    g_so = g_so.reshape(N_DEV, cf.DI // N_DEV, D)
    dz, dxs, dbm, dcm, ddt_parts, g_ssd_nw, g_dskip_e, g_alog_e, g_bias_e = _ssd_bwd(
        cf, proj1, xact, bias_e, alog_e, dskip_e, ssd_nw, yp1, st1, dy1)
    dact = jnp.concatenate([dxs, dbm, dcm], axis=1)
    dpre1, g_conv_w, g_conv_b = _conv_bwd_pre(cf, proj1, conv_w, conv_b, dact)
    dxbc = _conv_bwd_x(cf, dpre1, conv_w)
    ddt = _sum_groups(ddt_parts, "ssd_ddt_sum")
    dproj1 = jnp.concatenate([dz, dxbc, ddt], axis=1)
    if cf.SINP > cf.DI + cf.CD + LANES:
        dproj1 = jnp.pad(dproj1, ((0, 0), (0, cf.SINP - cf.DI - cf.CD - LANES)))
    du2, (rc,) = _mm(dproj1, w_si, kind="nt", name="mm_du_ssd", tj=1024, tr=3456, rider=swap_cores(g_so))
    part, own_so = core_sum(g_so, rc, "rs_ssd_w_out")
    g_si, (r2_so,) = _mm(u2, dproj1, kind="tn", name="mm_g_ssd_in", tj=1152, rider=_r_swap_chips(part))
    g_si = jnp.transpose(g_si[:, :cf.SIN].reshape(D, N_DEV, cf.SIN // N_DEV), (1, 0, 2))
    gh2, df0, g_nmpre1, g_nfpost0 = _norm_bwd(gh3, "norm_bwd2", du=du2, h=h2, w_pre=nrm(norm_mix_pre, 1),
                                              m=f0, w_post=nrm(norm_ffn_post, 0))
    own_si = []

    def si_chips(got):
        part, own = core_sum(g_si, got[0], "rs_ssd_w_in")
        own_si.append(own)
        return _r_swap_chips(part)

    du1, g_up0, own_dn0, r2_dn0, (r2_si,) = mlp_bwd(0, df0, u1, sq0, act0, swap_cores(g_si), si_chips)
    own_si = own_si[0]
    gh1, dm0, g_nfpre0, g_nmpost0 = _norm_bwd(gh2, "norm_bwd1", du=du1, h=h1, w_pre=nrm(norm_ffn_pre, 0),
                                              m=m0, w_post=nrm(norm_mix_post, 0))
    dy0, (rc,) = _mm(dm0, w_ro, kind="nt", name="mm_dy_ret", rider=swap_cores(g_up0))
    part, own_up0 = core_sum(g_up0, rc, "rs_mlp_up0")
    g_ro, (r2_up0,) = _mm(y0, dm0, kind="tn", name="mm_g_ret_out", tj=1024, rider=_r_swap_chips(part))
    g_ro = g_ro.reshape(N_DEV, cf.V // N_DEV, D)
    dq, dk, dv, dg, g_gn = _ret_bwd(cf, proj0, cos, sin, ret_gn_w, lgam, o0, st0, dy0)
    dproj0 = jnp.concatenate([dq, dk, dv, dg], axis=1)
    rin8 = cf.RIN // N_DEV
    g_ri, (rc,) = _mm(u0, dproj0, kind="tn", name="mm_g_ret_in", tj=rin8, out_nblk=N_DEV, rider=swap_cores(g_ro))
    part, own_ro = core_sum(g_ro, rc, "rs_ret_w_out")
    du0, (r2_ro, rc) = _mm(dproj0, w_ri, kind="nt", name="mm_du_ret", tj=1024, tr=rin8,
                           rider=_rider_join([_r_swap_chips(part), swap_cores(g_ri)]))
    part, own_ri = core_sum(g_ri, rc, "rs_ret_w_in")
    grad_x, g_nmpre0 = _norm_bwd(gh1, "norm_bwd0", du=du0, h=h0, w_pre=nrm(norm_mix_pre, 0))
    (r2_ri,) = _comm(_r_swap_chips(part), "rs_ret_w_in_swap_chips")

    g_nmpre = jnp.concatenate([g_nmpre0, g_nmpre1], axis=0)
    g_nmpost = jnp.concatenate([g_nmpost0, g_nmpost1], axis=0)
    g_nfpre = jnp.concatenate([g_nfpre0, g_nfpre1], axis=0)
    g_nfpost = jnp.concatenate([g_nfpost0, g_nfpost1], axis=0)
    segs = [g_nmpre, g_nmpost, g_nfpre, g_nfpost, g_gn, g_conv_w, g_conv_b, g_ssd_nw, g_bias_e, g_alog_e, g_dskip_e]
    flat = jnp.concatenate([s.reshape(-1, LANES) for s in segs], axis=0)
    n_fold = 3 * cf.DI // LANES
    red = _all_reduce_small(flat, n_fold, cf.P, "all_reduce_small")
    outs, r0 = [], 0
    for s in segs:
        nr = s.size // LANES
        outs.append(red[r0:r0 + nr])
        r0 += nr
    (g_nmpre, g_nmpost, g_nfpre, g_nfpost) = [o.reshape(DEPTH, D) for o in outs[:4]]
    g_gn = outs[4].reshape(1, cf.V)
    g_conv_w = lax.dynamic_slice_in_dim(outs[5].reshape(SSD_CONV_W, cf.CD), my_dev * cw, cw, axis=1)[None]
    g_conv_b = lax.dynamic_slice_in_dim(outs[6].reshape(1, cf.CD), my_dev * cbw, cbw, axis=1)
    g_ssd_nw = lax.dynamic_slice_in_dim(outs[7].reshape(1, cf.DI), my_dev * nww, nww, axis=1)
    per_row = LANES // cf.P
    g_bias, g_alog, g_dskip = [o[:, :per_row].reshape(1, cf.SH) for o in outs[8:]]

    def rs(own, recv, w, m, v, layer, name):
        return _chip_sum_adam(own, recv, w, m, v, layer, name + "_adam")

    r_ri = rs(own_ri, r2_ri, ret_w_in, m_ret_w_in, v_ret_w_in, 0, "rs_ret_w_in")
    r_ro = rs(own_ro, r2_ro, ret_w_out, m_ret_w_out, v_ret_w_out, 0, "rs_ret_w_out")
    r_si = rs(own_si, r2_si, ssd_w_in, m_ssd_w_in, v_ssd_w_in, 0, "rs_ssd_w_in")
    r_so = rs(own_so, r2_so, ssd_w_out, m_ssd_w_out, v_ssd_w_out, 0, "rs_ssd_w_out")
    r_up = [rs(o, r, mlp_w_up, m_mlp_w_up, v_mlp_w_up, l, f"rs_mlp_up{l}")
            for l, (o, r) in enumerate([(own_up0, r2_up0), (own_up1, r2_up1)])]
    r_dn = [rs(o, r, mlp_w_down, m_mlp_w_down, v_mlp_w_down, l, f"rs_mlp_down{l}")
            for l, (o, r) in enumerate([(own_dn0, r2_dn0), (own_dn1, r2_dn1)])]
    r_up = [jnp.stack([r_up[0][k], r_up[1][k]]) for k in range(4)]
    r_dn = [jnp.stack([r_dn[0][k], r_dn[1][k]]) for k in range(4)]
    lead = lambda r: [a[None] for a in r]

    def small_adam(w, g, m, v, name):
        return [g] + list(_adam_small(w, g, m, v, name))

    results = {
        "norm_mix_pre": small_adam(norm_mix_pre, g_nmpre, m_norm_mix_pre, v_norm_mix_pre, "adam_nmpre"),
        "norm_mix_post": small_adam(norm_mix_post, g_nmpost, m_norm_mix_post, v_norm_mix_post, "adam_nmpost"),
        "norm_ffn_pre": small_adam(norm_ffn_pre, g_nfpre, m_norm_ffn_pre, v_norm_ffn_pre, "adam_nfpre"),
        "norm_ffn_post": small_adam(norm_ffn_post, g_nfpost, m_norm_ffn_post, v_norm_ffn_post, "adam_nfpost"),
        "ret_w_in": lead(r_ri),
        "ret_gn_w": small_adam(ret_gn_w, g_gn, m_ret_gn_w, v_ret_gn_w, "adam_gn"),
        "ret_w_out": lead(r_ro),
        "ssd_w_in": lead(r_si),
        "ssd_conv_w": small_adam(ssd_conv_w, g_conv_w, m_ssd_conv_w, v_ssd_conv_w, "adam_conv_w"),
        "ssd_conv_b": small_adam(ssd_conv_b, g_conv_b, m_ssd_conv_b, v_ssd_conv_b, "adam_conv_b"),
        "ssd_dt_bias": small_adam(ssd_dt_bias, g_bias, m_ssd_dt_bias, v_ssd_dt_bias, "adam_dt_bias"),
        "ssd_a_log": small_adam(ssd_a_log, g_alog, m_ssd_a_log, v_ssd_a_log, "adam_a_log"),
        "ssd_d": small_adam(ssd_d, g_dskip, m_ssd_d, v_ssd_d, "adam_d"),
        "ssd_norm_w": small_adam(ssd_norm_w, g_ssd_nw, m_ssd_norm_w, v_ssd_norm_w, "adam_ssd_nw"),
        "ssd_w_out": lead(r_so),
        "mlp_w_up": r_up,
        "mlp_w_down": r_dn,
    }
    names = list(results)
    out = [loss, grad_x.reshape(1, T, D)]
    for k in range(4):
        out += [results[n][k] for n in names]
    return tuple(out)
```

```python
import functools
import math
import types

import jax
import jax.numpy as jnp
from jax import lax
from jax.experimental import pallas as pl
from jax.experimental.pallas import tpu as pltpu

F32 = jnp.float32
BF16 = jnp.bfloat16
HI = lax.Precision.HIGHEST
NN = (((1,), (0,)), ((), ()))
NT = (((1,), (1,)), ((), ()))
TN = (((0,), (0,)), ((), ()))
MESH = pl.DeviceIdType.MESH

V7X_VMEM_BYTES = 64 * 2**20
VMEM_LIMIT = V7X_VMEM_BYTES - 8 * 2**20
LANES = 128
SUBLANES = 8
N_DEV = 8

D_MODEL = 2048
SEQ = 8192
DEPTH = 2
CHUNK = 64
RMS_EPS = 1e-6
RET_HEAD_DK = 256
ROPE_BASE = 10000.0
GN_EPS = 1e-5
SSD_HEADDIM = 64
SSD_HEADS_PER_GROUP = 8
SSD_STATE = 128
SSD_CONV_W = 4
ADAM_LR = 0.001
ADAM_B1 = 0.9
ADAM_B2 = 0.999
ADAM_EPS = 1e-08
ADAM_WD = 0.01
ADAM_STEP = 10


def _cfg():
    c = types.SimpleNamespace()
    c.D, c.T, c.L = D_MODEL, SEQ, CHUNK
    c.DK = RET_HEAD_DK
    c.H = c.D // c.DK
    c.QK = c.H * c.DK
    c.DV = 2 * c.DK
    c.V = c.H * c.DV
    c.RIN = 2 * c.QK + 2 * c.V
    c.DI = 2 * c.D
    c.P = SSD_HEADDIM
    c.SH = c.DI // c.P
    c.RG = SSD_HEADS_PER_GROUP
    c.G = c.SH // c.RG
    c.GW = c.RG * c.P
    c.N = SSD_STATE
    c.CD = c.DI + 2 * c.G * c.N
    c.SIN = c.DI + c.CD + c.SH
    c.SINP = -(-c.SIN // LANES) * LANES
    c.FF = 4 * c.D
    c.NC = c.T // c.L
    return c


def _pcall(body, **kw):
    return pl.pallas_call(body, **kw)


def _cp(sem=None):
    return pltpu.CompilerParams(dimension_semantics=sem, vmem_limit_bytes=VMEM_LIMIT)


def _tile(n, pref, mult):
    if n <= pref:
        return n
    t = (pref // mult) * mult
    while t >= mult:
        if n % t == 0:
            return t
        t -= mult
    return n


def _dot(a, b, dn=NN, prec=None):
    return lax.dot_general(a, b, dn, precision=prec, preferred_element_type=F32)


ANY = pl.BlockSpec(memory_space=pl.ANY)


def _mesh_pos():
    return lax.axis_index("x"), lax.axis_index("y"), lax.axis_index("c")


def _rider_join(riders):
    j = types.SimpleNamespace(args=[], out_shape=[], aliases={}, sems=[])
    parts = []
    for r in riders:
        a0, o0, s0 = len(j.args), len(j.out_shape), len(j.sems)
        parts.append((r, a0, o0, s0))
        j.aliases.update({a0 + k: o0 + v for k, v in r.aliases.items()})
        j.args += r.args
        j.out_shape += r.out_shape
        j.sems += r.sems

    def make(rins, routs, sems):
        sends, recvs, locs = [], [], []
        for r, a0, o0, s0 in parts:
            s, rc, lc = r.make(rins[a0:a0 + len(r.args)], routs[o0:o0 + len(r.out_shape)], sems[s0:s0 + len(r.sems)])
            sends += s
            recvs += rc
            locs += lc
        return sends, recvs, locs

    j.make = make
    return j


def _rider_start(rider, rins, routs, sems):
    sends, _, locs = rider.make(rins, routs, sems)
    for cp in locs + sends:
        cp.start()


def _rider_wait(rider, rins, routs, sems):
    sends, recvs, locs = rider.make(rins, routs, sems)
    for cp in recvs:
        cp.wait_recv()
    for cp in sends:
        cp.wait_send()
    for cp in locs:
        cp.wait()


def _host_call(body, *, name, grid, in_specs, out_specs, out_shape, scratch, args, sem, rider=None):
    in_specs, out_specs, out_shape, scratch, args = map(list, (in_specs, out_specs, out_shape, scratch, args))
    if rider is None:
        res = _pcall(body, name=name, grid=grid, in_specs=in_specs, out_specs=out_specs, out_shape=out_shape,
                     scratch_shapes=scratch, compiler_params=_cp(sem))(*args)
        return list(res), []
    n_in, n_out, n_scr = len(args), len(out_shape), len(scratch)
    n_ra, n_ro = len(rider.args), len(rider.out_shape)

    def full(*refs):
        ins, rins = refs[:n_in], refs[n_in:n_in + n_ra]
        p = n_in + n_ra
        outs, routs = refs[p:p + n_out], refs[p + n_out:p + n_out + n_ro]
        p += n_out + n_ro
        scr, sems = refs[p:p + n_scr], refs[p + n_scr:]
        first = functools.reduce(jnp.logical_and, [pl.program_id(k) == 0 for k in range(len(grid))])
        last = functools.reduce(jnp.logical_and, [pl.program_id(k) == grid[k] - 1 for k in range(len(grid))])

        @pl.when(first)
        def _():
            _rider_start(rider, rins, routs, sems)

        body(*ins, *outs, *scr)

        @pl.when(last)
        def _():
            _rider_wait(rider, rins, routs, sems)

    res = _pcall(full, name=name, grid=grid, in_specs=in_specs + [ANY] * n_ra, out_specs=out_specs + [ANY] * n_ro,
                 out_shape=out_shape + rider.out_shape, scratch_shapes=scratch + rider.sems,
                 input_output_aliases={n_in + k: n_out + v for k, v in rider.aliases.items()},
                 compiler_params=_cp(("arbitrary",) * len(grid)))(*args, *rider.args)
    return list(res[:n_out]), list(res[n_out:])


def _comm(rider, name):
    n_ra, n_ro = len(rider.args), len(rider.out_shape)

    def body(*refs):
        rins, routs, sems = refs[:n_ra], refs[n_ra:n_ra + n_ro], refs[n_ra + n_ro:]
        _rider_start(rider, rins, routs, sems)
        _rider_wait(rider, rins, routs, sems)

    return list(_pcall(body, name=name, in_specs=[ANY] * n_ra, out_specs=[ANY] * n_ro, out_shape=rider.out_shape,
                       scratch_shapes=rider.sems, input_output_aliases=dict(rider.aliases))(*rider.args))


def _remote(src, dst, send_sems, recv_sems, k, to):
    return pltpu.make_async_remote_copy(src_ref=src, dst_ref=dst, send_sem=send_sems.at[k], recv_sem=recv_sems.at[k],
                                        device_id=to, device_id_type=MESH)


def _r_gather_chips(x):
    def make(rins, routs, sems):
        (x_ref,), (out_ref,), (send_sems, recv_sems, local_sem) = rins, routs, sems
        x_, y_, c_ = _mesh_pos()
        me = 4 * x_ + 2 * y_ + c_
        peers = [(x_, y_, 1 - c_), (1 - x_, y_, c_), (x_, 1 - y_, c_), (1 - x_, 1 - y_, c_)]
        sends = [_remote(x_ref, out_ref.at[me], send_sems, recv_sems, k, to) for k, to in enumerate(peers)]
        recvs = [_remote(x_ref, out_ref.at[4 * px + 2 * py + pc], send_sems, recv_sems, k, (px, py, pc))
                 for k, (px, py, pc) in enumerate(peers)]
        return sends, recvs, [pltpu.make_async_copy(x_ref, out_ref.at[me], local_sem)]

    return types.SimpleNamespace(
        args=[x], out_shape=[jax.ShapeDtypeStruct((N_DEV,) + x.shape, x.dtype)], aliases={},
        sems=[pltpu.SemaphoreType.DMA((4,)), pltpu.SemaphoreType.DMA((4,)), pltpu.SemaphoreType.DMA], make=make)


def _r_gather_cores(buf):
    def make(rins, routs, sems):
        (out_ref,), (send_sems, recv_sems) = routs, sems
        x_, y_, c_ = _mesh_pos()
        chips = [(1 - x_, y_), (x_, 1 - y_), (1 - x_, 1 - y_)]
        sends = [_remote(out_ref.at[4 * cx + 2 * cy + c_], out_ref.at[4 * cx + 2 * cy + c_], send_sems, recv_sems, k,
                         (x_, y_, 1 - c_)) for k, (cx, cy) in enumerate(chips)]
        recvs = [_remote(out_ref.at[4 * cx + 2 * cy + 1 - c_], out_ref.at[4 * cx + 2 * cy + 1 - c_], send_sems,
                         recv_sems, k, (x_, y_, 1 - c_)) for k, (cx, cy) in enumerate(chips)]
        return sends, recvs, []

    return types.SimpleNamespace(
        args=[buf], out_shape=[jax.ShapeDtypeStruct(buf.shape, buf.dtype)], aliases={0: 0},
        sems=[pltpu.SemaphoreType.DMA((3,)), pltpu.SemaphoreType.DMA((3,))], make=make)


def _r_swap_cores(g):
    def make(rins, routs, sems):
        (g_ref,), (out_ref,), (send_sems, recv_sems) = rins, routs, sems
        x_, y_, c_ = _mesh_pos()
        cps = [_remote(g_ref.at[q, 1 - c_], out_ref.at[q], send_sems, recv_sems, q, (x_, y_, 1 - c_)) for q in range(4)]
        return cps, cps, []

    return types.SimpleNamespace(
        args=[g], out_shape=[jax.ShapeDtypeStruct((4,) + g.shape[2:], g.dtype)], aliases={},
        sems=[pltpu.SemaphoreType.DMA((4,)), pltpu.SemaphoreType.DMA((4,))], make=make)


def _r_swap_chips(p):
    def make(rins, routs, sems):
        (p_ref,), (out_ref,), (send_sems, recv_sems) = rins, routs, sems
        x_, y_, c_ = _mesh_pos()
        chips = [(1 - x_, y_), (x_, 1 - y_), (1 - x_, 1 - y_)]
        cps = [_remote(p_ref.at[2 * cx + cy], out_ref.at[k], send_sems, recv_sems, k, (cx, cy, c_))
               for k, (cx, cy) in enumerate(chips)]
        return cps, cps, []

    return types.SimpleNamespace(
        args=[p], out_shape=[jax.ShapeDtypeStruct((3,) + p.shape[1:], p.dtype)], aliases={},
        sems=[pltpu.SemaphoreType.DMA((3,)), pltpu.SemaphoreType.DMA((3,))], make=make)


def _sigmoid(x):
    return 0.5 * (jnp.tanh(0.5 * x) + 1.0)


def _mm(a, b, *, kind, name, out_dtype=F32, ti=1024, tj=512, tr=2048, epi=None, epi_in=None, out_nblk=1, rider=None):
    b_blk = b.ndim == 3
    if kind == "tn":
        R, I = a.shape
    else:
        I, R = a.shape
    if kind == "nn":
        J = b.shape[1] if not b_blk else b.shape[0] * b.shape[2]
        nb_inner = b.shape[2] if b_blk else J
        r_inner = R
    elif kind == "nt":
        J = b.shape[0] if not b_blk else b.shape[1]
        nb_inner = J
        r_inner = b.shape[2] if b_blk else R
    else:
        J = b.shape[1]
        nb_inner = J
        r_inner = R
    out_inner = J // out_nblk
    ti = _tile(I, ti, LANES if kind == "tn" else SUBLANES)
    tj = _tile(min(nb_inner, out_inner), tj, LANES)
    assert nb_inner % tj == 0 and out_inner % tj == 0 and J % tj == 0
    tr = _tile(r_inner, tr, LANES)
    assert R % tr == 0
    ni, nj, nr = I // ti, J // tj, R // tr
    dn = {"nn": NN, "nt": NT, "tn": TN}[kind]

    if kind == "tn":
        a_spec = pl.BlockSpec((tr, ti), lambda i, j, r: (r, i))
    else:
        a_spec = pl.BlockSpec((ti, tr), lambda i, j, r: (i, r))
    if kind == "nn":
        if b_blk:
            per = nb_inner // tj
            b_spec = pl.BlockSpec((None, tr, tj), lambda i, j, r: (j // per, r, j % per))
        else:
            b_spec = pl.BlockSpec((tr, tj), lambda i, j, r: (r, j))
    elif kind == "nt":
        if b_blk:
            per = r_inner // tr
            b_spec = pl.BlockSpec((None, tj, tr), lambda i, j, r: (r // per, j, r % per))
        else:
            b_spec = pl.BlockSpec((tj, tr), lambda i, j, r: (j, r))
    else:
        b_spec = pl.BlockSpec((tr, tj), lambda i, j, r: (r, j))
    if out_nblk > 1:
        pero = out_inner // tj
        o_spec = pl.BlockSpec((None, ti, tj), lambda i, j, r: (j // pero, i, j % pero))
        o_shape = (out_nblk, I, out_inner)
    else:
        o_spec = pl.BlockSpec((ti, tj), lambda i, j, r: (i, j))
        o_shape = (I, J)
    in_specs = [a_spec, b_spec]
    args = [a, b]
    if epi == "mul2act":
        in_specs.append(pl.BlockSpec((ti, tj), lambda i, j, r: (i, j)))
        args.append(epi_in)
    if epi == "relu2":
        out_shape = (jax.ShapeDtypeStruct(o_shape, BF16), jax.ShapeDtypeStruct(o_shape, BF16))
        out_specs = (o_spec, o_spec)
    else:
        out_shape = jax.ShapeDtypeStruct(o_shape, out_dtype)
        out_specs = o_spec
    n_in = len(args)
    n_out = 2 if epi == "relu2" else 1

    def body(*refs):
        a_ref, b_ref = refs[0], refs[1]
        outs = refs[n_in:n_in + n_out]
        acc_ref = refs[n_in + n_out] if nr > 1 else None

        def finish(acc):
            if epi == "relu2":
                act = jnp.maximum(acc, 0.0)
                outs[0][...] = (act * act).astype(BF16)
                outs[1][...] = act.astype(BF16)
            elif epi == "mul2act":
                outs[0][...] = (acc * (2.0 * refs[2][...].astype(F32))).astype(out_dtype)
            else:
                outs[0][...] = acc.astype(out_dtype)

        part = _dot(a_ref[...], b_ref[...], dn)
        if nr == 1:
            finish(part)
        else:
            r = pl.program_id(2)

            @pl.when(r == 0)
            def _():
                acc_ref[...] = part

            @pl.when(r > 0)
            def _():
                acc_ref[...] += part

            @pl.when(r == nr - 1)
            def _():
                finish(acc_ref[...])

    res, rider_res = _host_call(
        body, name=name, grid=(ni, nj, nr), in_specs=in_specs,
        out_specs=out_specs if n_out > 1 else [out_specs], out_shape=out_shape if n_out > 1 else [out_shape],
        scratch=[pltpu.VMEM((ti, tj), F32)] if nr > 1 else [], args=args,
        sem=("parallel", "parallel", "arbitrary"), rider=rider)
    res = tuple(res) if n_out > 1 else res[0]
    return res if rider is None else (res, rider_res)


def _rstd(x):
    return lax.rsqrt(jnp.mean(x * x, axis=-1, keepdims=True) + RMS_EPS)


def _rms_bwd_rows(x, w, dy):
    r = _rstd(x)
    xh = x * r
    dxh = dy * w
    dx = r * (dxh - xh * jnp.mean(dxh * xh, axis=-1, keepdims=True))
    return dx, jnp.sum(dy * xh, axis=0, keepdims=True)


def _row_spec(tb, d):
    return pl.BlockSpec((tb, d), lambda i: (i, 0))


def _vec_spec(d):
    return pl.BlockSpec((1, d), lambda i: (0, 0))


def _rms_fwd(h, w, name):
    T, D = h.shape
    tb = _tile(T, 512, SUBLANES)

    def body(h_ref, w_ref, u_ref):
        x = h_ref[...]
        u_ref[...] = (x * _rstd(x) * w_ref[...]).astype(BF16)

    return _pcall(body, name=name, grid=(T // tb,), in_specs=[_row_spec(tb, D), _vec_spec(D)],
                  out_specs=_row_spec(tb, D), out_shape=jax.ShapeDtypeStruct((T, D), BF16),
                  compiler_params=_cp(("parallel",)))(h, w)


def _resid_fwd(h, m, w_post, w_next, name):
    T, D = h.shape
    tb = _tile(T, 256, SUBLANES)

    def body(h_ref, m_ref, wp_ref, wn_ref, ho_ref, u_ref):
        x = m_ref[...]
        hn = h_ref[...] + x * _rstd(x) * wp_ref[...]
        ho_ref[...] = hn
        u_ref[...] = (hn * _rstd(hn) * wn_ref[...]).astype(BF16)

    return _pcall(body, name=name, grid=(T // tb,),
                  in_specs=[_row_spec(tb, D), _row_spec(tb, D), _vec_spec(D), _vec_spec(D)],
                  out_specs=(_row_spec(tb, D), _row_spec(tb, D)),
                  out_shape=(jax.ShapeDtypeStruct((T, D), F32), jax.ShapeDtypeStruct((T, D), BF16)),
                  compiler_params=_cp(("parallel",)))(h, m, w_post, w_next)


def _final_fwd(h, m, w_post, tgt, name):
    T, D = h.shape
    tb = _tile(T, 256, SUBLANES)

    def body(h_ref, m_ref, wp_ref, t_ref, g_ref, l_ref):
        x = m_ref[...]
        e = h_ref[...] + x * _rstd(x) * wp_ref[...] - t_ref[...]
        g_ref[...] = e * (1.0 / D)
        s = jnp.sum(e * e, axis=0, keepdims=True)

        @pl.when(pl.program_id(0) == 0)
        def _():
            l_ref[...] = s

        @pl.when(pl.program_id(0) > 0)
        def _():
            l_ref[...] += s

    return _pcall(body, name=name, grid=(T // tb,),
                  in_specs=[_row_spec(tb, D), _row_spec(tb, D), _vec_spec(D), _row_spec(tb, D)],
                  out_specs=(_row_spec(tb, D), _vec_spec(D)),
                  out_shape=(jax.ShapeDtypeStruct((T, D), F32), jax.ShapeDtypeStruct((1, D), F32)),
                  compiler_params=_cp(("arbitrary",)))(h, m, w_post, tgt)


def _norm_bwd(g_out, name, du=None, h=None, w_pre=None, m=None, w_post=None):
    T, D = g_out.shape
    tb = _tile(T, 256, SUBLANES)
    has_pre, has_post = du is not None, m is not None
    args, in_specs = [g_out], [_row_spec(tb, D)]
    if has_pre:
        args += [du, h, w_pre]
        in_specs += [_row_spec(tb, D), _row_spec(tb, D), _vec_spec(D)]
    if has_post:
        args += [m, w_post]
        in_specs += [_row_spec(tb, D), _vec_spec(D)]
    out_shape, out_specs = [], []
    if has_pre:
        out_shape.append(jax.ShapeDtypeStruct((T, D), F32))
        out_specs.append(_row_spec(tb, D))
    if has_post:
        out_shape.append(jax.ShapeDtypeStruct((T, D), BF16))
        out_specs.append(_row_spec(tb, D))
    n_w = int(has_pre) + int(has_post)
    out_shape += [jax.ShapeDtypeStruct((1, D), F32)] * n_w
    out_specs += [_vec_spec(D)] * n_w
    n_in = len(args)

    def body(*refs):
        ins, outs = list(refs[:n_in]), list(refs[n_in:])
        g = ins.pop(0)[...]
        sums = []
        if has_pre:
            du_ref, h_ref, w_ref = ins.pop(0), ins.pop(0), ins.pop(0)
            dx, s = _rms_bwd_rows(h_ref[...], w_ref[...], du_ref[...])
            g = g + dx
            outs.pop(0)[...] = g
            sums.append(s)
        if has_post:
            m_ref, w_ref = ins.pop(0), ins.pop(0)
            dx, s = _rms_bwd_rows(m_ref[...], w_ref[...], g)
            outs.pop(0)[...] = dx.astype(BF16)
            sums.append(s)
        first = pl.program_id(0) == 0
        for o_ref, s in zip(outs, sums):
            @pl.when(first)
            def _(o_ref=o_ref, s=s):
                o_ref[...] = s

            @pl.when(jnp.logical_not(first))
            def _(o_ref=o_ref, s=s):
                o_ref[...] += s

    return _pcall(body, name=name, grid=(T // tb,), in_specs=in_specs, out_specs=tuple(out_specs),
                  out_shape=tuple(out_shape), compiler_params=_cp(("arbitrary",)))(*args)


def _ret_consts(lg, L):
    ii = lax.broadcasted_iota(jnp.int32, (L, L), 0).astype(F32)
    jj = lax.broadcasted_iota(jnp.int32, (L, L), 1).astype(F32)
    dmat = jnp.exp(jnp.abs(ii - jj) * lg)
    idx = lax.broadcasted_iota(jnp.int32, (L, 1), 0).astype(F32)
    xi = jnp.exp((idx + 1.0) * lg)
    zeta = jnp.exp((L - 1.0 - idx) * lg)
    cd = jnp.exp(jnp.full((1, 1), L, F32) * lg)
    return dmat, xi, zeta, cd


def _rot(t, cs, sn):
    half = t.shape[-1] // 2
    t1, t2 = t[:, :half], t[:, half:]
    return jnp.concatenate([t1 * cs - t2 * sn, t1 * sn + t2 * cs], axis=-1)


def _rot_bwd(d, cs, sn):
    half = d.shape[-1] // 2
    d1, d2 = d[:, :half], d[:, half:]
    return jnp.concatenate([d1 * cs + d2 * sn, d2 * cs - d1 * sn], axis=-1)


def _ret_specs(cf, tb, rev):
    H, DK, DV = cf.H, cf.DK, cf.DV
    ni = cf.T // tb
    ri = (lambda i: ni - 1 - i) if rev else (lambda i: i)
    q = pl.BlockSpec((tb, DK), lambda h, i: (ri(i), h))
    k = pl.BlockSpec((tb, DK), lambda h, i: (ri(i), H + h))
    v = pl.BlockSpec((tb, DV), lambda h, i: (ri(i), cf.QK * 2 // DV + h))
    g = pl.BlockSpec((tb, DV), lambda h, i: (ri(i), cf.QK * 2 // DV + H + h))
    cs = pl.BlockSpec((tb, DK // 2), lambda h, i: (ri(i), 0))
    gw = pl.BlockSpec((1, DV), lambda h, i: (0, h))
    row_v = pl.BlockSpec((tb, DV), lambda h, i: (ri(i), h))
    row_k = pl.BlockSpec((tb, DK), lambda h, i: (ri(i), h))
    st = pl.BlockSpec((tb // cf.L, None, DK, DV), lambda h, i: (ri(i), h, 0, 0))
    lgs = pl.BlockSpec(memory_space=pltpu.SMEM)
    return q, k, v, g, cs, gw, row_v, row_k, st, lgs


def _ret_fwd(cf, proj, cos, sin, gn_w, lgam, rider=None):
    T, L, H, DK, DV = cf.T, cf.L, cf.H, cf.DK, cf.DV
    tb = _tile(T, 512, L)
    nck = tb // L
    q_s, k_s, v_s, g_s, cs_s, gw_s, row_v, _, st_s, lg_s = _ret_specs(cf, tb, False)
    kscale = DK ** -0.5

    def body(lg_ref, q_ref, k_ref, v_ref, g_ref, cos_ref, sin_ref, gw_ref, y_ref, o_ref, st_ref, state):
        h = pl.program_id(0)

        @pl.when(pl.program_id(1) == 0)
        def _():
            state[...] = jnp.zeros_like(state)

        dmat, xi, zeta, cd = _ret_consts(lg_ref[h], L)
        gw = gw_ref[...]

        def chunk(c, carry):
            rows = pl.ds(pl.multiple_of(c * L, L), L)
            cs, sn = cos_ref[rows, :], sin_ref[rows, :]
            qr = _rot(q_ref[rows, :], cs, sn)
            kr = _rot(k_ref[rows, :], cs, sn) * kscale
            qb, kb = qr.astype(BF16), kr.astype(BF16)
            vb = v_ref[rows, :].astype(BF16)
            st = state[...]
            stb = st.astype(BF16)
            st_ref[c] = stb
            s = _dot(qb, kb, NT) * dmat
            o = _dot(s.astype(BF16), vb) + _dot(qb, stb) * xi
            state[...] = st * cd + _dot((kr * zeta).astype(BF16), vb, TN)
            o_ref[rows, :] = o
            mu = jnp.mean(o, axis=-1, keepdims=True)
            oc = o - mu
            var = jnp.mean(oc * oc, axis=-1, keepdims=True)
            n = oc * lax.rsqrt(var + GN_EPS) * gw
            gt = g_ref[rows, :]
            y_ref[rows, :] = (gt * _sigmoid(gt) * n).astype(BF16)
            return carry

        lax.fori_loop(0, nck, chunk, 0, unroll=2 if nck % 2 == 0 else 1)

    return _host_call(
        body, name="ret_fwd", grid=(H, T // tb),
        in_specs=[lg_s, q_s, k_s, v_s, g_s, cs_s, cs_s, gw_s],
        out_specs=(row_v, row_v, st_s),
        out_shape=(jax.ShapeDtypeStruct((T, cf.V), BF16), jax.ShapeDtypeStruct((T, cf.V), F32),
                   jax.ShapeDtypeStruct((cf.NC, H, DK, DV), BF16)),
        scratch=[pltpu.VMEM((DK, DV), F32)], args=(lgam, proj, proj, proj, proj, cos, sin, gn_w),
        sem=("parallel", "arbitrary"), rider=rider)


def _ret_bwd(cf, proj, cos, sin, gn_w, lgam, o, states, dy):
    T, L, H, DK, DV = cf.T, cf.L, cf.H, cf.DK, cf.DV
    tb = _tile(T, 512, L)
    nck = tb // L
    q_s, k_s, v_s, g_s, cs_s, gw_s, row_v, row_k, st_s, lg_s = _ret_specs(cf, tb, True)
    kscale = DK ** -0.5

    def body(lg_ref, q_ref, k_ref, v_ref, g_ref, cos_ref, sin_ref, gw_ref, o_ref, st_ref, dy_ref,
             dq_ref, dk_ref, dv_ref, dg_ref, dgw_ref, dstate):
        h = pl.program_id(0)

        @pl.when(pl.program_id(1) == 0)
        def _():
            dstate[...] = jnp.zeros_like(dstate)
            dgw_ref[...] = jnp.zeros_like(dgw_ref)

        dmat, xi, zeta, cd = _ret_consts(lg_ref[h], L)
        gw = gw_ref[...]

        def chunk(t, carry):
            c = nck - 1 - t
            rows = pl.ds(pl.multiple_of(c * L, L), L)
            cs, sn = cos_ref[rows, :], sin_ref[rows, :]
            qr = _rot(q_ref[rows, :], cs, sn)
            kr = _rot(k_ref[rows, :], cs, sn) * kscale
            qb, kb = qr.astype(BF16), kr.astype(BF16)
            kzb = (kr * zeta).astype(BF16)
            vb = v_ref[rows, :].astype(BF16)
            s = (_dot(qb, kb, NT) * dmat).astype(BF16)
            oo = o_ref[rows, :]
            mu = jnp.mean(oo, axis=-1, keepdims=True)
            oc = oo - mu
            rstd = lax.rsqrt(jnp.mean(oc * oc, axis=-1, keepdims=True) + GN_EPS)
            oh = oc * rstd
            gt = g_ref[rows, :]
            sg = _sigmoid(gt)
            dyv = dy_ref[rows, :]
            dn = dyv * (gt * sg)
            dg_ref[rows, :] = (dyv * (oh * gw) * (sg * (1.0 + gt * (1.0 - sg)))).astype(BF16)
            dgw_ref[...] += jnp.sum(dn * oh, axis=0, keepdims=True)
            doh = dn * gw
            do = rstd * (doh - jnp.mean(doh, axis=-1, keepdims=True) - oh * jnp.mean(doh * oh, axis=-1, keepdims=True))
            dob = do.astype(BF16)
            doxb = (do * xi).astype(BF16)
            dst = dstate[...]
            dstb = dst.astype(BF16)
            stb = st_ref[c]
            dv_ref[rows, :] = (_dot(s, dob, TN) + _dot(kzb, dstb)).astype(BF16)
            ds = (_dot(dob, vb, NT) * dmat).astype(BF16)
            dqr = _dot(ds, kb) + _dot(doxb, stb, NT)
            dkr = _dot(ds, qb, TN) + _dot(vb, dstb, NT) * zeta
            dstate[...] = dst * cd + _dot(qb, doxb, TN)
            dq_ref[rows, :] = _rot_bwd(dqr, cs, sn).astype(BF16)
            dk_ref[rows, :] = _rot_bwd(dkr * kscale, cs, sn).astype(BF16)
            return carry

        lax.fori_loop(0, nck, chunk, 0, unroll=2 if nck % 2 == 0 else 1)

    return _pcall(
        body, name="ret_bwd", grid=(H, T // tb),
        in_specs=[lg_s, q_s, k_s, v_s, g_s, cs_s, cs_s, gw_s, row_v, st_s, row_v],
        out_specs=(row_k, row_k, row_v, row_v, gw_s),
        out_shape=(jax.ShapeDtypeStruct((T, cf.QK), BF16), jax.ShapeDtypeStruct((T, cf.QK), BF16),
                   jax.ShapeDtypeStruct((T, cf.V), BF16), jax.ShapeDtypeStruct((T, cf.V), BF16),
                   jax.ShapeDtypeStruct((1, cf.V), F32)),
        scratch_shapes=[pltpu.VMEM((DK, DV), F32)],
        compiler_params=_cp(("parallel", "arbitrary")),
    )(lgam, proj, proj, proj, proj, cos, sin, gn_w, o, states, dy)


def _conv_pre(x, halo, w, b, first, W):
    tb = x.shape[0]
    ext = jnp.concatenate([jnp.where(first, 0.0, halo), x], axis=0)
    out = b + w[W - 1:W, :] * x
    for tap in range(W - 1):
        out = out + w[tap:tap + 1, :] * pltpu.roll(ext, W - 1 - tap, 0)[SUBLANES:SUBLANES + tb, :]
    return out, ext


def _conv_fwd(cf, proj, conv_w, conv_b):
    T, CD, W = cf.T, cf.CD, SSD_CONV_W
    tb = _tile(T, 512, SUBLANES)
    tc = _tile(CD, 512, LANES)
    off = cf.DI // tc
    nh = tb // SUBLANES

    def body(x_ref, halo_ref, w_ref, b_ref, o_ref):
        pre, _ = _conv_pre(x_ref[...], halo_ref[...], w_ref[...], b_ref[...], pl.program_id(1) == 0, W)
        o_ref[...] = pre * _sigmoid(pre)

    return _pcall(
        body, name="conv_fwd", grid=(CD // tc, T // tb),
        in_specs=[pl.BlockSpec((tb, tc), lambda j, i: (i, off + j)),
                  pl.BlockSpec((SUBLANES, tc), lambda j, i: (jnp.maximum(i * nh - 1, 0), off + j)),
                  pl.BlockSpec((W, tc), lambda j, i: (0, j)), pl.BlockSpec((1, tc), lambda j, i: (0, j))],
        out_specs=pl.BlockSpec((tb, tc), lambda j, i: (i, j)),
        out_shape=jax.ShapeDtypeStruct((T, CD), F32),
        compiler_params=_cp(("parallel", "arbitrary")),
    )(proj, proj, conv_w, conv_b)


def _conv_bwd_pre(cf, proj, conv_w, conv_b, dact):
    T, CD, W = cf.T, cf.CD, SSD_CONV_W
    tb = _tile(T, 512, SUBLANES)
    tc = _tile(CD, 512, LANES)
    off = cf.DI // tc
    nh = tb // SUBLANES

    def body(x_ref, halo_ref, w_ref, b_ref, da_ref, dp_ref, dw_ref, db_ref):
        x = x_ref[...]
        pre, ext = _conv_pre(x, halo_ref[...], w_ref[...], b_ref[...], pl.program_id(1) == 0, W)
        sg = _sigmoid(pre)
        dp = da_ref[...] * (sg * (1.0 + pre * (1.0 - sg)))
        dp_ref[...] = dp
        rows = [jnp.sum(dp * pltpu.roll(ext, W - 1 - tap, 0)[SUBLANES:SUBLANES + tb, :], axis=0, keepdims=True)
                for tap in range(W - 1)]
        rows.append(jnp.sum(dp * x, axis=0, keepdims=True))
        dw = jnp.concatenate(rows, axis=0)
        db = jnp.sum(dp, axis=0, keepdims=True)

        @pl.when(pl.program_id(1) == 0)
        def _():
            dw_ref[...] = dw
            db_ref[...] = db

        @pl.when(pl.program_id(1) > 0)
        def _():
            dw_ref[...] += dw
            db_ref[...] += db

    return _pcall(
        body, name="conv_bwd_pre", grid=(CD // tc, T // tb),
        in_specs=[pl.BlockSpec((tb, tc), lambda j, i: (i, off + j)),
                  pl.BlockSpec((SUBLANES, tc), lambda j, i: (jnp.maximum(i * nh - 1, 0), off + j)),
                  pl.BlockSpec((W, tc), lambda j, i: (0, j)), pl.BlockSpec((1, tc), lambda j, i: (0, j)),
                  pl.BlockSpec((tb, tc), lambda j, i: (i, j))],
        out_specs=(pl.BlockSpec((tb, tc), lambda j, i: (i, j)), pl.BlockSpec((W, tc), lambda j, i: (0, j)),
                   pl.BlockSpec((1, tc), lambda j, i: (0, j))),
        out_shape=(jax.ShapeDtypeStruct((T, CD), F32), jax.ShapeDtypeStruct((W, CD), F32),
                   jax.ShapeDtypeStruct((1, CD), F32)),
        compiler_params=_cp(("parallel", "arbitrary")),
    )(proj, proj, conv_w, conv_b, dact)


def _conv_bwd_x(cf, dpre, conv_w, into):
    T, CD, W = cf.T, cf.CD, SSD_CONV_W
    tb = _tile(T, 512, SUBLANES)
    tc = _tile(CD, 512, LANES)
    off = cf.DI // tc
    nh = tb // SUBLANES
    last_blk = T // SUBLANES - 1
    ni = T // tb

    def body(d_ref, halo_ref, w_ref, into_ref, o_ref):
        d = d_ref[...]
        w = w_ref[...]
        nxt = jnp.where(pl.program_id(1) == ni - 1, 0.0, halo_ref[...])
        ext = jnp.concatenate([d, nxt], axis=0)
        n = tb + SUBLANES
        out = w[W - 1:W, :] * d
        for tap in range(W - 1):
            out = out + w[tap:tap + 1, :] * pltpu.roll(ext, n - (W - 1 - tap), 0)[:tb, :]
        o_ref[...] = out.astype(BF16)

    return _pcall(
        body, name="conv_bwd_x", grid=(CD // tc, ni),
        in_specs=[pl.BlockSpec((tb, tc), lambda j, i: (i, j)),
                  pl.BlockSpec((SUBLANES, tc), lambda j, i: (jnp.minimum((i + 1) * nh, last_blk), j)),
                  pl.BlockSpec((W, tc), lambda j, i: (0, j)), ANY],
        out_specs=pl.BlockSpec((tb, tc), lambda j, i: (i, off + j)),
        out_shape=jax.ShapeDtypeStruct(into.shape, BF16), input_output_aliases={3: 0},
        compiler_params=_cp(("parallel", "arbitrary")),
    )(dpre, dpre, conv_w, into)


def _ssd_masks(cf, g, tb):
    L, GW, P, RG = cf.L, cf.GW, cf.P, cf.RG
    assert L == P and 2 * L == LANES and RG % 2 == 0
    i32 = jnp.int32
    hrow = lax.broadcasted_iota(i32, (LANES, GW), 0)
    hcol = lax.broadcasted_iota(i32, (LANES, GW), 1) // P
    expand = (hrow == g * RG + hcol).astype(BF16)
    ti = lax.broadcasted_iota(i32, (LANES, LANES), 0)
    tj = lax.broadcasted_iota(i32, (LANES, LANES), 1)
    btril = jnp.logical_and(ti // L == tj // L, ti >= tj).astype(BF16)
    r0 = lax.broadcasted_iota(i32, (L, GW), 0)
    c0 = lax.broadcasted_iota(i32, (L, GW), 1) % L
    tile_eye = (r0 == c0).astype(F32)
    lower = r0 >= c0
    p0 = lax.broadcasted_iota(i32, (2 * L, LANES), 0) // L
    p1 = lax.broadcasted_iota(i32, (2 * L, LANES), 1) // P
    pair = (p0 == p1).astype(F32)
    return expand, btril, tile_eye, lower, pair


def _softplus(x):
    return jnp.maximum(x, 0.0) + jnp.log1p(jnp.exp(-jnp.abs(x)))


def _split3(x):
    hi = x.astype(BF16)
    r1 = x - hi.astype(F32)
    mid = r1.astype(BF16)
    return hi, mid, (r1 - mid.astype(F32)).astype(BF16)


def _chunk_sums(btril, x, dn):
    hi, mid, lo = _split3(x)
    outs = []
    for k in range(x.shape[0] // LANES):
        sl = slice(k * LANES, (k + 1) * LANES)
        outs.append((_dot(btril, lo[sl], dn) + _dot(btril, mid[sl], dn)) + _dot(btril, hi[sl], dn))
    return jnp.concatenate(outs, axis=0)


def _expand_heads(x, expand, dn):
    hi, mid, lo = _split3(x)
    return (_dot(lo, expand, dn) + _dot(mid, expand, dn)) + _dot(hi, expand, dn)


def _ssd_chunk(cf, mk, acum, dt, xs, bm, cm):
    _, _, tile_eye, lower, pair = mk
    rowv = jnp.sum(acum * tile_eye, axis=0, keepdims=True)
    lf = jnp.exp(jnp.where(lower, acum - rowv, -1e30))
    xdt = xs * dt
    bb, cb_ = bm.astype(BF16), cm.astype(BF16)
    bb2 = jnp.concatenate([bb, bb], axis=0)
    cb2 = _dot(cb_, bb2, NT)
    ms, bds = [], []
    for j in range(cf.RG // 2):
        ln = slice(j * LANES, (j + 1) * LANES)
        ms.append((cb2 * lf[:, ln]).astype(BF16))
        xp = xdt[:, ln]
        bds.append((jnp.concatenate([xp, xp], axis=0) * pair).astype(BF16))
    return lf, xdt, bb, cb_, bb2, cb2, ms, bds


def _ssd_specs(cf, tb, rev):
    G, GW, N = cf.G, cf.GW, cf.N
    ni = cf.T // tb
    ri = (lambda i: ni - 1 - i) if rev else (lambda i: i)
    z = pl.BlockSpec((tb, GW), lambda g, i: (ri(i), g))
    dt = pl.BlockSpec((tb, LANES), lambda g, i: (ri(i), (cf.DI + cf.CD) // LANES))
    xs = pl.BlockSpec((tb, GW), lambda g, i: (ri(i), g))
    bm = pl.BlockSpec((tb, N), lambda g, i: (ri(i), cf.DI // N + g))
    cm = pl.BlockSpec((tb, N), lambda g, i: (ri(i), cf.DI // N + G + g))
    vec = pl.BlockSpec((1, GW), lambda g, i: (0, g))
    st = pl.BlockSpec((tb // cf.L, None, N, GW), lambda g, i: (ri(i), g, 0, 0))
    return z, dt, xs, bm, cm, vec, st


def _ssd_fwd(cf, proj, xact, bias_e, alog_e, dskip_e, norm_w):
    T, L, G, GW, N = cf.T, cf.L, cf.G, cf.GW, cf.N
    tb = _tile(T, 512, L)
    nck = tb // L
    z_s, dt_s, xs_s, b_s, c_s, vec_s, st_s = _ssd_specs(cf, tb, False)

    def body(z_ref, dt_ref, xs_ref, b_ref, c_ref, bias_ref, alog_ref, dsk_ref, nw_ref, y_ref, yp_ref, st_ref,
             state, dt_s, ac_s):
        @pl.when(pl.program_id(1) == 0)
        def _():
            state[...] = jnp.zeros_like(state)

        mk = _ssd_masks(cf, pl.program_id(0), tb)
        a_e = -jnp.exp(alog_ref[...])
        dt_all = _softplus(_expand_heads(dt_ref[...], mk[0], NN) + bias_ref[...])
        dt_s[...] = dt_all
        ac_s[...] = _chunk_sums(mk[1], dt_all * a_e, NN)

        def chunk(c, carry):
            rows = pl.ds(pl.multiple_of(c * L, L), L)
            acum = ac_s[rows, :]
            lf, xdt, bb, cb_, _, _, ms, bds = _ssd_chunk(cf, mk, acum, dt_s[rows, :], xs_ref[rows, :],
                                                         b_ref[rows, :], c_ref[rows, :])
            st = state[...]
            stb = st.astype(BF16)
            st_ref[c] = stb
            ydiag = jnp.concatenate([_dot(m, bd) for m, bd in zip(ms, bds)], axis=1)
            al = acum[L - 1:L, :]
            state[...] = st * jnp.exp(al) + _dot(bb, (xdt * jnp.exp(al - acum)).astype(BF16), TN)
            yp_ref[rows, :] = ydiag + _dot(cb_, stb) * jnp.exp(acum)
            return carry

        lax.fori_loop(0, nck, chunk, 0, unroll=2 if nck % 2 == 0 else 1)
        z = z_ref[...]
        yg = (yp_ref[...] + dsk_ref[...] * xs_ref[...]) * (z * _sigmoid(z))
        y_ref[...] = (yg * _rstd(yg) * nw_ref[...]).astype(BF16)

    return _pcall(
        body, name="ssd_fwd", grid=(G, T // tb),
        in_specs=[z_s, dt_s, xs_s, b_s, c_s, vec_s, vec_s, vec_s, vec_s],
        out_specs=(z_s, z_s, st_s),
        out_shape=(jax.ShapeDtypeStruct((T, cf.DI), BF16), jax.ShapeDtypeStruct((T, cf.DI), F32),
                   jax.ShapeDtypeStruct((cf.NC, G, N, GW), BF16)),
        scratch_shapes=[pltpu.VMEM((N, GW), F32), pltpu.VMEM((tb, GW), F32), pltpu.VMEM((tb, GW), F32)],
        compiler_params=_cp(("parallel", "arbitrary")),
    )(proj, proj, xact, xact, xact, bias_e, alog_e, dskip_e, norm_w)


def _ssd_bwd(cf, proj, xact, bias_e, alog_e, dskip_e, norm_w, ypre, states, dy):
    T, L, G, GW, N, RG = cf.T, cf.L, cf.G, cf.GW, cf.N, cf.RG
    tb = _tile(T, 512, L)
    nck = tb // L
    ni = T // tb
    z_s, dt_s, xs_s, b_s, c_s, vec_s, st_s = _ssd_specs(cf, tb, True)
    bc_out = pl.BlockSpec((tb, N), lambda g, i: (ni - 1 - i, g))
    ddt_out = pl.BlockSpec((None, tb, LANES), lambda g, i: (g, ni - 1 - i, 0))

    def body(z_ref, dt_ref, xs_ref, b_ref, c_ref, bias_ref, alog_ref, dsk_ref, nw_ref, yp_ref, st_ref, dy_ref,
             dz_ref, dxs_ref, db_ref, dc_ref, ddt_ref, dnw_ref, ddsk_ref, dalog_ref, dbias_ref,
             dstate, dt_s, ac_s, sg_s, dys_s, dxdt_s, dac_s):
        @pl.when(pl.program_id(1) == 0)
        def _():
            dstate[...] = jnp.zeros_like(dstate)
            for r in (dnw_ref, ddsk_ref, dalog_ref, dbias_ref):
                r[...] = jnp.zeros_like(r)

        mk = _ssd_masks(cf, pl.program_id(0), tb)
        expand, btril, tile_eye, lower, pair = mk
        a_e = -jnp.exp(alog_ref[...])
        dsk, nw = dsk_ref[...], nw_ref[...]
        last_row = (lax.broadcasted_iota(jnp.int32, (L, 1), 0) == L - 1).astype(F32)
        raw = _expand_heads(dt_ref[...], expand, NN) + bias_ref[...]
        dt_all = _softplus(raw)
        dt_s[...] = dt_all
        sg_s[...] = _sigmoid(raw)
        ac_s[...] = _chunk_sums(btril, dt_all * a_e, NN)
        z = z_ref[...]
        sz = _sigmoid(z)
        silu = z * sz
        xs_all = xs_ref[...]
        yd = yp_ref[...] + dsk * xs_all
        yg = yd * silu
        rr = _rstd(yg)
        xh = yg * rr
        dout = dy_ref[...]
        dnw_ref[...] += jnp.sum(dout * xh, axis=0, keepdims=True)
        dxh = dout * nw
        dyg = rr * (dxh - xh * jnp.mean(dxh * xh, axis=-1, keepdims=True))
        dz_ref[...] = (dyg * yd * (sz * (1.0 + z * (1.0 - sz)))).astype(BF16)
        dys_all = dyg * silu
        dys_s[...] = dys_all
        ddsk_ref[...] += jnp.sum(dys_all * xs_all, axis=0, keepdims=True)

        def chunk(t, carry):
            c = nck - 1 - t
            rows = pl.ds(pl.multiple_of(c * L, L), L)
            acum = ac_s[rows, :]
            lf, xdt, bb, cb_, bb2, cb2, ms, bds = _ssd_chunk(cf, mk, acum, dt_s[rows, :], xs_ref[rows, :],
                                                             b_ref[rows, :], c_ref[rows, :])
            stb = st_ref[c]
            eac = jnp.exp(acum)
            al = acum[L - 1:L, :]
            eal = jnp.exp(al)
            dte = jnp.exp(al - acum)
            dys = dys_s[rows, :]
            dyb = dys.astype(BF16)
            dms, dxs_, dsegs = [], [], []
            dcb2 = None
            for j in range(RG // 2):
                ln = slice(j * LANES, (j + 1) * LANES)
                dyj = dyb[:, ln]
                dbd = _dot(ms[j], dyj, TN) * pair
                dxs_.append(dbd[:L, :] + dbd[L:, :])
                tj = _dot(dyj, bds[j], NT) * lf[:, ln]
                dcb2 = tj if dcb2 is None else dcb2 + tj
                dsegs.append(tj * cb2)
            dxdt = jnp.concatenate(dxs_, axis=1)
            dseg = jnp.concatenate(dsegs, axis=1)
            dcb2 = dcb2.astype(BF16)
            dcm = _dot(dcb2, bb2)
            dbm2 = _dot(dcb2, cb_, TN)
            dbm = dbm2[:L, :] + dbm2[L:, :]
            dacum = dseg - tile_eye * jnp.sum(dseg, axis=0, keepdims=True)
            dyo = (dys * eac).astype(BF16)
            dcm = dcm + _dot(dyo, stb, NT)
            dacum = dacum + dys * _dot(cb_, stb) * eac
            dst = dstate[...]
            dstb = dst.astype(BF16)
            xd = xdt * dte
            dbm = dbm + _dot(xd.astype(BF16), dstb, NT)
            dxd = _dot(bb, dstb)
            dal = jnp.sum(dst * stb.astype(F32), axis=0, keepdims=True) * eal
            dxdt = dxdt + dxd * dte
            tt = dxd * xd
            dacum = dacum - tt + last_row * (dal + jnp.sum(tt, axis=0, keepdims=True))
            dstate[...] = dst * eal + _dot(cb_, dyo, TN)
            dxdt_s[rows, :] = dxdt
            dac_s[rows, :] = dacum
            db_ref[rows, :] = dbm
            dc_ref[rows, :] = dcm
            return carry

        lax.fori_loop(0, nck, chunk, 0, unroll=2 if nck % 2 == 0 else 1)
        dda = _chunk_sums(btril, dac_s[...], TN)
        dxdt_all = dxdt_s[...]
        dt_all = dt_s[...]
        dxs_ref[...] = dys_s[...] * dsk + dxdt_all * dt_all
        ddt = dxdt_all * xs_ref[...] + dda * a_e
        dalog_ref[...] += jnp.sum(dda * dt_all, axis=0, keepdims=True) * a_e
        draw = ddt * sg_s[...]
        dbias_ref[...] += jnp.sum(draw, axis=0, keepdims=True)
        ddt_ref[...] = _expand_heads(draw, expand, NT)

    GN = G * N
    return _pcall(
        body, name="ssd_bwd", grid=(G, ni),
        in_specs=[z_s, dt_s, xs_s, b_s, c_s, vec_s, vec_s, vec_s, vec_s, z_s, st_s, z_s],
        out_specs=(z_s, z_s, bc_out, bc_out, ddt_out, vec_s, vec_s, vec_s, vec_s),
        out_shape=(jax.ShapeDtypeStruct((T, cf.SINP), BF16), jax.ShapeDtypeStruct((T, cf.CD), F32),
                   jax.ShapeDtypeStruct((T, GN), F32), jax.ShapeDtypeStruct((T, GN), F32),
                   jax.ShapeDtypeStruct((G, T, LANES), F32)) + (jax.ShapeDtypeStruct((1, cf.DI), F32),) * 4,
        scratch_shapes=[pltpu.VMEM((N, GW), F32)] + [pltpu.VMEM((tb, GW), F32)] * 6,
        compiler_params=_cp(("parallel", "arbitrary")),
    )(proj, proj, xact, xact, xact, bias_e, alog_e, dskip_e, norm_w, ypre, states, dy)


def _sum_groups(parts, name, into, col_blk):
    G, T, W = parts.shape
    tb = _tile(T, 512, SUBLANES)

    def body(p_ref, into_ref, o_ref):
        acc = p_ref[0]
        for g in range(1, G):
            acc = acc + p_ref[g]
        o_ref[...] = acc.astype(BF16)

    return _pcall(body, name=name, grid=(T // tb,), in_specs=[pl.BlockSpec((G, tb, W), lambda i: (0, i, 0)), ANY],
                  out_specs=pl.BlockSpec((tb, W), lambda i: (i, col_blk)),
                  out_shape=jax.ShapeDtypeStruct(into.shape, BF16), input_output_aliases={1: 0},
                  compiler_params=_cp(("parallel",)))(parts, into)


def _fill_bc(dact, dbm, dcm, name):
    T, GN = dbm.shape
    tb = _tile(T, 512, SUBLANES)
    blk = (dact.shape[1] - 2 * GN) // (2 * GN)
    assert blk * 2 * GN == dact.shape[1] - 2 * GN

    def body(b_ref, c_ref, into_ref, o_ref):
        o_ref[:, :GN] = b_ref[...]
        o_ref[:, GN:] = c_ref[...]

    row = pl.BlockSpec((tb, GN), lambda i: (i, 0))
    return _pcall(body, name=name, grid=(T // tb,), in_specs=[row, row, ANY],
                  out_specs=pl.BlockSpec((tb, 2 * GN), lambda i: (i, blk)),
                  out_shape=jax.ShapeDtypeStruct(dact.shape, F32), input_output_aliases={2: 0},
                  compiler_params=_cp(("parallel",)))(dbm, dcm, dact)


def _all_gather(x, name):
    def body(x_ref, out_ref, send_sems, recv_sems, local_sem):
        x, y, c = lax.axis_index("x"), lax.axis_index("y"), lax.axis_index("c")
        me, sibling = (x, y, c), (x, y, 1 - c)
        chips = [(1 - x, y), (x, 1 - y), (1 - x, 1 - y)]

        def blk(px, py, pc):
            return out_ref.at[4 * px + 2 * py + pc]

        def copy(k, block, to, src=None):
            return pltpu.make_async_remote_copy(
                src_ref=blk(*block) if src is None else src, dst_ref=blk(*block),
                send_sem=send_sems.at[k], recv_sem=recv_sems.at[k], device_id=to, device_id_type=MESH)

        mine = pltpu.make_async_copy(x_ref, blk(*me), local_sem)
        mine.start()
        first = [copy(0, me, sibling, src=x_ref)]
        first += [copy(1 + j, me, (*chip, c), src=x_ref) for j, chip in enumerate(chips)]
        for cp in first:
            cp.start()
        passed = [copy(4 + j, (*chip, c), sibling) for j, chip in enumerate(chips)]
        for j, chip in enumerate(chips):
            copy(1 + j, (*chip, c), me).wait_recv()
            passed[j].start()
        copy(0, sibling, me).wait_recv()
        for j, chip in enumerate(chips):
            copy(4 + j, (*chip, 1 - c), me).wait_recv()
        for cp in first + passed:
            cp.wait_send()
        mine.wait()

    return _pcall(
        body, name=name, in_specs=[ANY], out_specs=ANY,
        out_shape=jax.ShapeDtypeStruct((N_DEV,) + x.shape, x.dtype),
        scratch_shapes=[pltpu.SemaphoreType.DMA((7,)), pltpu.SemaphoreType.DMA((7,)), pltpu.SemaphoreType.DMA],
    )(x)


def _core_sum(g, recv, idx, name):
    _, _, a, b = g.shape
    tr = _tile(a, max(SUBLANES, (2 * 2**20) // (4 * b) // SUBLANES * SUBLANES), SUBLANES)

    def body(idx_ref, g_ref, r_ref, p_ref, own_ref):
        s = g_ref[...] + r_ref[...]
        p_ref[...] = s.astype(BF16)

        @pl.when(pl.program_id(1) == idx_ref[1])
        def _():
            own_ref[...] = s

    return _pcall(
        body, name=name,
        grid_spec=pltpu.PrefetchScalarGridSpec(
            num_scalar_prefetch=1, grid=(a // tr, 4),
            in_specs=[pl.BlockSpec((None, None, tr, b), lambda i, q, idx: (q, idx[0], i, 0)),
                      pl.BlockSpec((None, tr, b), lambda i, q, idx: (q, i, 0))],
            out_specs=(pl.BlockSpec((None, tr, b), lambda i, q, idx: (q, i, 0)),
                       pl.BlockSpec((tr, b), lambda i, q, idx: (i, 0)))),
        out_shape=(jax.ShapeDtypeStruct((4, a, b), BF16), jax.ShapeDtypeStruct((a, b), F32)),
        compiler_params=_cp(("parallel", "arbitrary")),
    )(idx, g, recv)


def _adam_math(w, g, m, v):
    m = ADAM_B1 * m + (1.0 - ADAM_B1) * g
    v = ADAM_B2 * v + (1.0 - ADAM_B2) * (g * g)
    m_hat = m / (1.0 - ADAM_B1 ** ADAM_STEP)
    v_hat = v / (1.0 - ADAM_B2 ** ADAM_STEP)
    delta = -ADAM_LR * (m_hat / (jnp.sqrt(v_hat) + ADAM_EPS) + ADAM_WD * w)
    return delta, m, v


def _chip_sum_adam(own, recv, w, m, v, layer, name, into=None):
    a, b = own.shape
    n = w.shape[0]
    tr = _tile(a, max(SUBLANES, (2**20) // (4 * b) // SUBLANES * SUBLANES), SUBLANES)
    wspec = pl.BlockSpec((None, tr, b), lambda i: (layer, i, 0))
    ospec = pl.BlockSpec((tr, b), lambda i: (i, 0))
    n_into = 0 if into is None else 4

    def body(own_ref, r_ref, w_ref, m_ref, v_ref, *rest):
        g_ref, d_ref, mo_ref, vo_ref = rest[n_into:]
        g = own_ref[...]
        for k in range(3):
            g = g + r_ref[k].astype(F32)
        g_ref[...] = g
        d_ref[...], mo_ref[...], vo_ref[...] = _adam_math(w_ref[...], g, m_ref[...], v_ref[...])

    return _pcall(
        body, name=name, grid=(a // tr,),
        in_specs=[ospec, pl.BlockSpec((3, tr, b), lambda i: (0, i, 0)), wspec, wspec, wspec] + [ANY] * n_into,
        out_specs=(wspec,) * 4, out_shape=(jax.ShapeDtypeStruct((n, a, b), F32),) * 4,
        input_output_aliases={5 + k: k for k in range(n_into)},
        compiler_params=_cp(("parallel",)),
    )(own, recv, w, m, v, *(into or ()))


def _all_reduce_small(x, n_fold, fold_w, name):
    R, W = x.shape

    def body(x_ref, out_ref, buf, send_sems, recv_sems):
        xx, y, c = lax.axis_index("x"), lax.axis_index("y"), lax.axis_index("c")
        me = 4 * xx + 2 * y + c
        buf[me] = x_ref[...]
        copies = []
        for k in range(1, N_DEV):
            px, py, pc = xx ^ (k >> 2), y ^ ((k >> 1) & 1), c ^ (k & 1)
            copies.append(pltpu.make_async_remote_copy(
                src_ref=x_ref, dst_ref=buf.at[me], send_sem=send_sems.at[k - 1], recv_sem=recv_sems.at[k - 1],
                device_id=(px, py, pc), device_id_type=MESH))
        for cp in copies:
            cp.start()
        for cp in copies:
            cp.wait()
        acc = buf[0]
        for j in range(1, N_DEV):
            acc = acc + buf[j]
        out_ref[...] = acc
        if n_fold:
            l0 = lax.broadcasted_iota(jnp.int32, (W, W), 0) // fold_w
            l1 = lax.broadcasted_iota(jnp.int32, (W, W), 1)
            fold = (l0 == l1).astype(F32)
            out_ref[R - n_fold:, :] = _dot(acc[R - n_fold:, :], fold, NN, HI)

    return _pcall(
        body, name=name, in_specs=[pl.BlockSpec(memory_space=pltpu.VMEM)],
        out_specs=pl.BlockSpec(memory_space=pltpu.VMEM), out_shape=jax.ShapeDtypeStruct((R, W), F32),
        scratch_shapes=[pltpu.VMEM((N_DEV, R, W), F32), pltpu.SemaphoreType.DMA((N_DEV - 1,)),
                        pltpu.SemaphoreType.DMA((N_DEV - 1,))],
        compiler_params=pltpu.CompilerParams(vmem_limit_bytes=VMEM_LIMIT),
    )(x)


def _adam_small(w, g, m, v, name):
    def body(w_ref, g_ref, m_ref, v_ref, d_ref, mo_ref, vo_ref):
        d_ref[...], mo_ref[...], vo_ref[...] = _adam_math(w_ref[...], g_ref[...], m_ref[...], v_ref[...])

    return _pcall(body, name=name, out_shape=(jax.ShapeDtypeStruct(w.shape, F32),) * 3)(w, g, m, v)


def kernel(x, norm_mix_pre, norm_mix_post, norm_ffn_pre, norm_ffn_post, ret_w_in, ret_gn_w, ret_w_out, ssd_w_in, ssd_conv_w, ssd_conv_b, ssd_dt_bias, ssd_a_log, ssd_d, ssd_norm_w, ssd_w_out, mlp_w_up, mlp_w_down, loss_target, m_norm_mix_pre, m_norm_mix_post, m_norm_ffn_pre, m_norm_ffn_post, m_ret_w_in, m_ret_gn_w, m_ret_w_out, m_ssd_w_in, m_ssd_conv_w, m_ssd_conv_b, m_ssd_dt_bias, m_ssd_a_log, m_ssd_d, m_ssd_norm_w, m_ssd_w_out, m_mlp_w_up, m_mlp_w_down, v_norm_mix_pre, v_norm_mix_post, v_norm_ffn_pre, v_norm_ffn_post, v_ret_w_in, v_ret_gn_w, v_ret_w_out, v_ssd_w_in, v_ssd_conv_w, v_ssd_conv_b, v_ssd_dt_bias, v_ssd_a_log, v_ssd_d, v_ssd_norm_w, v_ssd_w_out, v_mlp_w_up, v_mlp_w_down):
    cf = _cfg()
    T, D = cf.T, cf.D
    ax, ay, ac = lax.axis_index("x"), lax.axis_index("y"), lax.axis_index("c")
    my_dev = 4 * ax + 2 * ay + ac
    idx = jnp.stack([ac, 2 * ax + ay]).astype(jnp.int32)

    w_ri = _all_gather(ret_w_in[0].astype(BF16), "ag_ret_w_in")
    chips_of = lambda ws: _rider_join([_r_gather_chips(w.astype(BF16)) for w in ws])
    cores_of = lambda bufs: _rider_join([_r_gather_cores(b) for b in bufs])
    cw, cbw, nww = cf.CD // N_DEV, cf.CD // N_DEV, cf.DI // N_DEV
    small = jnp.concatenate([ssd_conv_w[0], ssd_conv_b, jnp.pad(ssd_norm_w, ((0, 0), (0, cw - nww))),
                             jnp.zeros((2, cw), F32)], axis=0)
    small = _all_gather(small, "ag_ssd_small")
    conv_w = jnp.transpose(small[:, :SSD_CONV_W, :], (1, 0, 2)).reshape(SSD_CONV_W, cf.CD)
    conv_b = small[:, SSD_CONV_W, :].reshape(1, cf.CD)
    ssd_nw = small[:, SSD_CONV_W + 1, :nww].reshape(1, cf.DI)

    half = cf.DK // 2
    inv_freq = ROPE_BASE ** (-jnp.arange(half, dtype=F32) / half)
    ang = jnp.arange(T).astype(F32)[:, None] * inv_freq[None, :]
    cos, sin = jnp.cos(ang), jnp.sin(ang)
    lgam = jnp.log1p(-jnp.exp2(-5.0 - jnp.arange(cf.H, dtype=F32)))
    rep = lambda p: jnp.repeat(p.reshape(1, cf.SH), cf.P, axis=1)
    bias_e, alog_e, dskip_e = rep(ssd_dt_bias), rep(ssd_a_log), rep(ssd_d)

    h0 = x.reshape(T, D)
    tgt = loss_target.reshape(T, D)
    nrm = lambda p, i: p[i:i + 1]

    u0 = _rms_fwd(h0, nrm(norm_mix_pre, 0), "rms_fwd0")
    proj0, part_a = _mm(u0, w_ri, kind="nn", name="mm_ret_in",
                        rider=chips_of([ret_w_out[0], mlp_w_up[0], mlp_w_down[0]]))
    (y0, o0, st0), got = _ret_fwd(cf, proj0, cos, sin, ret_gn_w, lgam, rider=_rider_join(
        [cores_of(part_a), chips_of([ssd_w_in[0], ssd_w_out[0], mlp_w_up[1]])]))
    w_ro, w_up0, w_dn0 = got[0].reshape(cf.V, D), got[1], got[2].reshape(cf.FF, D)
    part_b = got[3:]
    m0 = _mm(y0, w_ro, kind="nn", name="mm_ret_out", tr=cf.V)
    h1, u1 = _resid_fwd(h0, m0, nrm(norm_mix_post, 0), nrm(norm_ffn_pre, 0), "resid_fwd0")
    (sq0, act0), part_c = _mm(u1, w_up0, kind="nn", name="mm_up0", epi="relu2", rider=chips_of([mlp_w_down[1]]))
    f0, got = _mm(sq0, w_dn0, kind="nn", name="mm_down0", tj=1024, rider=cores_of(part_b + part_c))
    w_si = jnp.pad(jnp.transpose(got[0], (1, 0, 2)).reshape(D, cf.SIN), ((0, 0), (0, cf.SINP - cf.SIN)))
    w_so, w_up1, w_dn1 = got[1].reshape(cf.DI, D), got[2], got[3].reshape(cf.FF, D)
    w_up, w_dn = [w_up0, w_up1], [w_dn0, w_dn1]
    h2, u2 = _resid_fwd(h1, f0, nrm(norm_ffn_post, 0), nrm(norm_mix_pre, 1), "resid_fwd1")
    proj1 = _mm(u2, w_si, kind="nn", name="mm_ssd_in", tj=1152)
    xact = _conv_fwd(cf, proj1, conv_w, conv_b)
    y1, yp1, st1 = _ssd_fwd(cf, proj1, xact, bias_e, alog_e, dskip_e, ssd_nw)
    m1 = _mm(y1, w_so, kind="nn", name="mm_ssd_out", tr=cf.DI)
    h3, u3 = _resid_fwd(h2, m1, nrm(norm_mix_post, 1), nrm(norm_ffn_pre, 1), "resid_fwd2")
    sq1, act1 = _mm(u3, w_up[1], kind="nn", name="mm_up1", epi="relu2")
    f1 = _mm(sq1, w_dn[1], kind="nn", name="mm_down1", tj=1024)
    g4, lsum = _final_fwd(h3, f1, nrm(norm_ffn_post, 1), tgt, "final_fwd")
    loss = lax.psum(0.5 * jnp.sum(lsum) / D, ("x", "y", "c"))

    as4 = lambda g: g.reshape(4, 2, g.shape[1], g.shape[2])
    swap_cores = lambda g: _r_swap_cores(as4(g))
    core_sum = lambda g, recv, name: _core_sum(as4(g), recv, idx, name + "_core_sum")

    def mlp_bwd(l, df, u, sq, act, rider=None, then=None):
        dpre = _mm(df, w_dn[l], kind="nt", name=f"mm_dpre{l}", out_dtype=BF16, epi="mul2act", epi_in=act, rider=rider)
        dpre, got = dpre if rider is not None else (dpre, [])
        g_dn = _mm(sq, df, kind="tn", name=f"mm_gdown{l}", tj=1024, tr=4096, rider=then(got) if then else None)
        g_dn, got = g_dn if then else (g_dn, [])
        g_dn = g_dn.reshape(N_DEV, cf.FF // N_DEV, D)
        du, (rc,) = _mm(dpre, w_up[l], kind="nt", name=f"mm_du_mlp{l}", tj=1024, tr=1024, rider=swap_cores(g_dn))
        part, own = core_sum(g_dn, rc, f"rs_mlp_down{l}")
        g_up, (r2,) = _mm(u, dpre, kind="tn", name=f"mm_gup{l}", tj=1024, tr=4096, out_nblk=N_DEV, rider=_r_swap_chips(part))
        return du, g_up, own, r2, got

    df1, g_nfpost1 = _norm_bwd(g4, "norm_bwd4", m=f1, w_post=nrm(norm_ffn_post, 1))
    du3, g_up1, own_dn1, r2_dn1, _ = mlp_bwd(1, df1, u3, sq1, act1, None)
    gh3, dm1, g_nfpre1, g_nmpost1 = _norm_bwd(g4, "norm_bwd3", du=du3, h=h3, w_pre=nrm(norm_ffn_pre, 1),
                                              m=m1, w_post=nrm(norm_mix_post, 1))
    dy1, (rc,) = _mm(dm1, w_so, kind="nt", name="mm_dy_ssd", rider=swap_cores(g_up1))
    part, own_up1 = core_sum(g_up1, rc, "rs_mlp_up1")
    g_so, (r2_up1,) = _mm(y1, dm1, kind="tn", name="mm_g_ssd_out", tj=1024, tr=4096, rider=_r_swap_chips(part))
    g_so = g_so.reshape(N_DEV, cf.DI // N_DEV, D)
    dz, dxs, dbm, dcm, ddt_parts, g_ssd_nw, g_dskip_e, g_alog_e, g_bias_e = _ssd_bwd(
        cf, proj1, xact, bias_e, alog_e, dskip_e, ssd_nw, yp1, st1, dy1)
    dact = _fill_bc(dxs, dbm, dcm, "ssd_dact_fill")
    dpre1, g_conv_w, g_conv_b = _conv_bwd_pre(cf, proj1, conv_w, conv_b, dact)
    assert cf.SINP == cf.DI + cf.CD + LANES
    dproj1 = _conv_bwd_x(cf, dpre1, conv_w, dz)
    dproj1 = _sum_groups(ddt_parts, "ssd_ddt_sum", dproj1, (cf.DI + cf.CD) // LANES)
    du2, (rc,) = _mm(dproj1, w_si, kind="nt", name="mm_du_ssd", tj=1024, tr=3456, rider=swap_cores(g_so))
    part, own_so = core_sum(g_so, rc, "rs_ssd_w_out")
    g_si, (r2_so,) = _mm(u2, dproj1, kind="tn", name="mm_g_ssd_in", ti=512, tj=1152, tr=4096, rider=_r_swap_chips(part))
    g_si = jnp.transpose(g_si[:, :cf.SIN].reshape(D, N_DEV, cf.SIN // N_DEV), (1, 0, 2))
    gh2, df0, g_nmpre1, g_nfpost0 = _norm_bwd(gh3, "norm_bwd2", du=du2, h=h2, w_pre=nrm(norm_mix_pre, 1),
                                              m=f0, w_post=nrm(norm_ffn_post, 0))
    own_si = []

    def si_chips(got):
        part, own = core_sum(g_si, got[0], "rs_ssd_w_in")
        own_si.append(own)
        return _r_swap_chips(part)

    du1, g_up0, own_dn0, r2_dn0, (r2_si,) = mlp_bwd(0, df0, u1, sq0, act0, swap_cores(g_si), si_chips)
    own_si = own_si[0]
    gh1, dm0, g_nfpre0, g_nmpost0 = _norm_bwd(gh2, "norm_bwd1", du=du1, h=h1, w_pre=nrm(norm_ffn_pre, 0),
                                              m=m0, w_post=nrm(norm_mix_post, 0))
    dy0, (rc,) = _mm(dm0, w_ro, kind="nt", name="mm_dy_ret", rider=swap_cores(g_up0))
    part, own_up0 = core_sum(g_up0, rc, "rs_mlp_up0")
    g_ro, (r2_up0,) = _mm(y0, dm0, kind="tn", name="mm_g_ret_out", tj=1024, tr=4096, rider=_r_swap_chips(part))
    g_ro = g_ro.reshape(N_DEV, cf.V // N_DEV, D)
    dq, dk, dv, dg, g_gn = _ret_bwd(cf, proj0, cos, sin, ret_gn_w, lgam, o0, st0, dy0)
    dproj0 = jnp.concatenate([dq, dk, dv, dg], axis=1)
    rin8 = cf.RIN // N_DEV
    g_ri, (rc,) = _mm(u0, dproj0, kind="tn", name="mm_g_ret_in", ti=512, tj=rin8, tr=4096, out_nblk=N_DEV, rider=swap_cores(g_ro))
    part, own_ro = core_sum(g_ro, rc, "rs_ret_w_out")
    du0, (r2_ro, rc) = _mm(dproj0, w_ri, kind="nt", name="mm_du_ret", tj=1024, tr=rin8,
                           rider=_rider_join([_r_swap_chips(part), swap_cores(g_ri)]))
    part, own_ri = core_sum(g_ri, rc, "rs_ret_w_in")
    grad_x, g_nmpre0 = _norm_bwd(gh1, "norm_bwd0", du=du0, h=h0, w_pre=nrm(norm_mix_pre, 0))
    (r2_ri,) = _comm(_r_swap_chips(part), "rs_ret_w_in_swap_chips")

    g_nmpre = jnp.concatenate([g_nmpre0, g_nmpre1], axis=0)
    g_nmpost = jnp.concatenate([g_nmpost0, g_nmpost1], axis=0)
    g_nfpre = jnp.concatenate([g_nfpre0, g_nfpre1], axis=0)
    g_nfpost = jnp.concatenate([g_nfpost0, g_nfpost1], axis=0)
    segs = [g_nmpre, g_nmpost, g_nfpre, g_nfpost, g_gn, g_conv_w, g_conv_b, g_ssd_nw, g_bias_e, g_alog_e, g_dskip_e]
    flat = jnp.concatenate([s.reshape(-1, LANES) for s in segs], axis=0)
    n_fold = 3 * cf.DI // LANES
    red = _all_reduce_small(flat, n_fold, cf.P, "all_reduce_small")
    outs, r0 = [], 0
    for s in segs:
        nr = s.size // LANES
        outs.append(red[r0:r0 + nr])
        r0 += nr
    (g_nmpre, g_nmpost, g_nfpre, g_nfpost) = [o.reshape(DEPTH, D) for o in outs[:4]]
    g_gn = outs[4].reshape(1, cf.V)
    g_conv_w = lax.dynamic_slice_in_dim(outs[5].reshape(SSD_CONV_W, cf.CD), my_dev * cw, cw, axis=1)[None]
    g_conv_b = lax.dynamic_slice_in_dim(outs[6].reshape(1, cf.CD), my_dev * cbw, cbw, axis=1)
    g_ssd_nw = lax.dynamic_slice_in_dim(outs[7].reshape(1, cf.DI), my_dev * nww, nww, axis=1)
    per_row = LANES // cf.P
    g_bias, g_alog, g_dskip = [o[:, :per_row].reshape(1, cf.SH) for o in outs[8:]]

    def rs(own, recv, w, m, v, layer, name, into=None):
        return _chip_sum_adam(own, recv, w, m, v, layer, name + "_adam", into)

    r_ri = rs(own_ri, r2_ri, ret_w_in, m_ret_w_in, v_ret_w_in, 0, "rs_ret_w_in")
    r_ro = rs(own_ro, r2_ro, ret_w_out, m_ret_w_out, v_ret_w_out, 0, "rs_ret_w_out")
    r_si = rs(own_si, r2_si, ssd_w_in, m_ssd_w_in, v_ssd_w_in, 0, "rs_ssd_w_in")
    r_so = rs(own_so, r2_so, ssd_w_out, m_ssd_w_out, v_ssd_w_out, 0, "rs_ssd_w_out")
    r_up = rs(own_up1, r2_up1, mlp_w_up, m_mlp_w_up, v_mlp_w_up, 1, "rs_mlp_up1")
    r_up = rs(own_up0, r2_up0, mlp_w_up, m_mlp_w_up, v_mlp_w_up, 0, "rs_mlp_up0", r_up)
    r_dn = rs(own_dn1, r2_dn1, mlp_w_down, m_mlp_w_down, v_mlp_w_down, 1, "rs_mlp_down1")
    r_dn = rs(own_dn0, r2_dn0, mlp_w_down, m_mlp_w_down, v_mlp_w_down, 0, "rs_mlp_down0", r_dn)
    lead = list

    def small_adam(w, g, m, v, name):
        return [g] + list(_adam_small(w, g, m, v, name))

    results = {
        "norm_mix_pre": small_adam(norm_mix_pre, g_nmpre, m_norm_mix_pre, v_norm_mix_pre, "adam_nmpre"),
        "norm_mix_post": small_adam(norm_mix_post, g_nmpost, m_norm_mix_post, v_norm_mix_post, "adam_nmpost"),
        "norm_ffn_pre": small_adam(norm_ffn_pre, g_nfpre, m_norm_ffn_pre, v_norm_ffn_pre, "adam_nfpre"),
        "norm_ffn_post": small_adam(norm_ffn_post, g_nfpost, m_norm_ffn_post, v_norm_ffn_post, "adam_nfpost"),
        "ret_w_in": lead(r_ri),
        "ret_gn_w": small_adam(ret_gn_w, g_gn, m_ret_gn_w, v_ret_gn_w, "adam_gn"),
        "ret_w_out": lead(r_ro),
        "ssd_w_in": lead(r_si),
        "ssd_conv_w": small_adam(ssd_conv_w, g_conv_w, m_ssd_conv_w, v_ssd_conv_w, "adam_conv_w"),
        "ssd_conv_b": small_adam(ssd_conv_b, g_conv_b, m_ssd_conv_b, v_ssd_conv_b, "adam_conv_b"),
        "ssd_dt_bias": small_adam(ssd_dt_bias, g_bias, m_ssd_dt_bias, v_ssd_dt_bias, "adam_dt_bias"),
        "ssd_a_log": small_adam(ssd_a_log, g_alog, m_ssd_a_log, v_ssd_a_log, "adam_a_log"),
        "ssd_d": small_adam(ssd_d, g_dskip, m_ssd_d, v_ssd_d, "adam_d"),
        "ssd_norm_w": small_adam(ssd_norm_w, g_ssd_nw, m_ssd_norm_w, v_ssd_norm_w, "adam_ssd_nw"),
        "ssd_w_out": lead(r_so),
        "mlp_w_up": r_up,
        "mlp_w_down": r_dn,
    }
    names = list(results)
    out = [loss, grad_x.reshape(1, T, D)]
    for k in range(4):
        out += [results[n][k] for n in names]
    return tuple(out)
```

```python
import functools
import math
import types

import jax
import jax.numpy as jnp
from jax import lax
from jax.experimental import pallas as pl
from jax.experimental.pallas import tpu as pltpu

F32 = jnp.float32
BF16 = jnp.bfloat16
HI = lax.Precision.HIGHEST
NN = (((1,), (0,)), ((), ()))
NT = (((1,), (1,)), ((), ()))
TN = (((0,), (0,)), ((), ()))
MESH = pl.DeviceIdType.MESH

V7X_VMEM_BYTES = 64 * 2**20
VMEM_LIMIT = V7X_VMEM_BYTES - 8 * 2**20
LANES = 128
SUBLANES = 8
N_DEV = 8

D_MODEL = 2048
SEQ = 8192
DEPTH = 2
CHUNK = 64
RMS_EPS = 1e-6
RET_HEAD_DK = 256
ROPE_BASE = 10000.0
GN_EPS = 1e-5
SSD_HEADDIM = 64
SSD_HEADS_PER_GROUP = 8
SSD_STATE = 128
SSD_CONV_W = 4
ADAM_LR = 0.001
ADAM_B1 = 0.9
ADAM_B2 = 0.999
ADAM_EPS = 1e-08
ADAM_WD = 0.01
ADAM_STEP = 10


def _cfg():
    c = types.SimpleNamespace()
    c.D, c.T, c.L = D_MODEL, SEQ, CHUNK
    c.DK = RET_HEAD_DK
    c.H = c.D // c.DK
    c.QK = c.H * c.DK
    c.DV = 2 * c.DK
    c.V = c.H * c.DV
    c.RIN = 2 * c.QK + 2 * c.V
    c.DI = 2 * c.D
    c.P = SSD_HEADDIM
    c.SH = c.DI // c.P
    c.RG = SSD_HEADS_PER_GROUP
    c.G = c.SH // c.RG
    c.GW = c.RG * c.P
    c.N = SSD_STATE
    c.CD = c.DI + 2 * c.G * c.N
    c.SIN = c.DI + c.CD + c.SH
    c.SINP = -(-c.SIN // LANES) * LANES
    c.FF = 4 * c.D
    c.NC = c.T // c.L
    return c


def _pcall(body, **kw):
    return pl.pallas_call(body, **kw)


def _cp(sem=None):
    return pltpu.CompilerParams(dimension_semantics=sem, vmem_limit_bytes=VMEM_LIMIT)


def _tile(n, pref, mult):
    if n <= pref:
        return n
    t = (pref // mult) * mult
    while t >= mult:
        if n % t == 0:
            return t
        t -= mult
    return n


def _dot(a, b, dn=NN, prec=None):
    return lax.dot_general(a, b, dn, precision=prec, preferred_element_type=F32)


ANY = pl.BlockSpec(memory_space=pl.ANY)


def _mesh_pos():
    return lax.axis_index("x"), lax.axis_index("y"), lax.axis_index("c")


def _rider_join(riders):
    j = types.SimpleNamespace(args=[], out_shape=[], aliases={}, sems=[])
    parts = []
    for r in riders:
        a0, o0, s0 = len(j.args), len(j.out_shape), len(j.sems)
        parts.append((r, a0, o0, s0))
        j.aliases.update({a0 + k: o0 + v for k, v in r.aliases.items()})
        j.args += r.args
        j.out_shape += r.out_shape
        j.sems += r.sems

    def make(rins, routs, sems):
        sends, recvs, locs = [], [], []
        for r, a0, o0, s0 in parts:
            s, rc, lc = r.make(rins[a0:a0 + len(r.args)], routs[o0:o0 + len(r.out_shape)], sems[s0:s0 + len(r.sems)])
            sends += s
            recvs += rc
            locs += lc
        return sends, recvs, locs

    j.make = make
    return j


def _rider_start(rider, rins, routs, sems):
    sends, _, locs = rider.make(rins, routs, sems)
    for cp in locs + sends:
        cp.start()


def _rider_wait(rider, rins, routs, sems):
    sends, recvs, locs = rider.make(rins, routs, sems)
    for cp in recvs:
        cp.wait_recv()
    for cp in sends:
        cp.wait_send()
    for cp in locs:
        cp.wait()


def _host_call(body, *, name, grid, in_specs, out_specs, out_shape, scratch, args, sem, rider=None):
    in_specs, out_specs, out_shape, scratch, args = map(list, (in_specs, out_specs, out_shape, scratch, args))
    if rider is None:
        res = _pcall(body, name=name, grid=grid, in_specs=in_specs, out_specs=out_specs, out_shape=out_shape,
                     scratch_shapes=scratch, compiler_params=_cp(sem))(*args)
        return list(res), []
    n_in, n_out, n_scr = len(args), len(out_shape), len(scratch)
    n_ra, n_ro = len(rider.args), len(rider.out_shape)

    def full(*refs):
        ins, rins = refs[:n_in], refs[n_in:n_in + n_ra]
        p = n_in + n_ra
        outs, routs = refs[p:p + n_out], refs[p + n_out:p + n_out + n_ro]
        p += n_out + n_ro
        scr, sems = refs[p:p + n_scr], refs[p + n_scr:]
        first = functools.reduce(jnp.logical_and, [pl.program_id(k) == 0 for k in range(len(grid))])
        last = functools.reduce(jnp.logical_and, [pl.program_id(k) == grid[k] - 1 for k in range(len(grid))])

        @pl.when(first)
        def _():
            _rider_start(rider, rins, routs, sems)

        body(*ins, *outs, *scr)

        @pl.when(last)
        def _():
            _rider_wait(rider, rins, routs, sems)

    res = _pcall(full, name=name, grid=grid, in_specs=in_specs + [ANY] * n_ra, out_specs=out_specs + [ANY] * n_ro,
                 out_shape=out_shape + rider.out_shape, scratch_shapes=scratch + rider.sems,
                 input_output_aliases={n_in + k: n_out + v for k, v in rider.aliases.items()},
                 compiler_params=_cp(("arbitrary",) * len(grid)))(*args, *rider.args)
    return list(res[:n_out]), list(res[n_out:])


def _comm(rider, name):
    n_ra, n_ro = len(rider.args), len(rider.out_shape)

    def body(*refs):
        rins, routs, sems = refs[:n_ra], refs[n_ra:n_ra + n_ro], refs[n_ra + n_ro:]
        _rider_start(rider, rins, routs, sems)
        _rider_wait(rider, rins, routs, sems)

    return list(_pcall(body, name=name, in_specs=[ANY] * n_ra, out_specs=[ANY] * n_ro, out_shape=rider.out_shape,
                       scratch_shapes=rider.sems, input_output_aliases=dict(rider.aliases))(*rider.args))


def _remote(src, dst, send_sems, recv_sems, k, to):
    return pltpu.make_async_remote_copy(src_ref=src, dst_ref=dst, send_sem=send_sems.at[k], recv_sem=recv_sems.at[k],
                                        device_id=to, device_id_type=MESH)


def _r_gather_chips(x):
    def make(rins, routs, sems):
        (x_ref,), (out_ref,), (send_sems, recv_sems, local_sem) = rins, routs, sems
        x_, y_, c_ = _mesh_pos()
        me = 4 * x_ + 2 * y_ + c_
        peers = [(x_, y_, 1 - c_), (1 - x_, y_, c_), (x_, 1 - y_, c_), (1 - x_, 1 - y_, c_)]
        sends = [_remote(x_ref, out_ref.at[me], send_sems, recv_sems, k, to) for k, to in enumerate(peers)]
        recvs = [_remote(x_ref, out_ref.at[4 * px + 2 * py + pc], send_sems, recv_sems, k, (px, py, pc))
                 for k, (px, py, pc) in enumerate(peers)]
        return sends, recvs, [pltpu.make_async_copy(x_ref, out_ref.at[me], local_sem)]

    return types.SimpleNamespace(
        args=[x], out_shape=[jax.ShapeDtypeStruct((N_DEV,) + x.shape, x.dtype)], aliases={},
        sems=[pltpu.SemaphoreType.DMA((4,)), pltpu.SemaphoreType.DMA((4,)), pltpu.SemaphoreType.DMA], make=make)


def _r_gather_cores(buf):
    def make(rins, routs, sems):
        (out_ref,), (send_sems, recv_sems) = routs, sems
        x_, y_, c_ = _mesh_pos()
        chips = [(1 - x_, y_), (x_, 1 - y_), (1 - x_, 1 - y_)]
        sends = [_remote(out_ref.at[4 * cx + 2 * cy + c_], out_ref.at[4 * cx + 2 * cy + c_], send_sems, recv_sems, k,
                         (x_, y_, 1 - c_)) for k, (cx, cy) in enumerate(chips)]
        recvs = [_remote(out_ref.at[4 * cx + 2 * cy + 1 - c_], out_ref.at[4 * cx + 2 * cy + 1 - c_], send_sems,
                         recv_sems, k, (x_, y_, 1 - c_)) for k, (cx, cy) in enumerate(chips)]
        return sends, recvs, []

    return types.SimpleNamespace(
        args=[buf], out_shape=[jax.ShapeDtypeStruct(buf.shape, buf.dtype)], aliases={0: 0},
        sems=[pltpu.SemaphoreType.DMA((3,)), pltpu.SemaphoreType.DMA((3,))], make=make)


def _r_swap_cores(g):
    def make(rins, routs, sems):
        (g_ref,), (out_ref,), (send_sems, recv_sems) = rins, routs, sems
        x_, y_, c_ = _mesh_pos()
        cps = [_remote(g_ref.at[q, 1 - c_], out_ref.at[q], send_sems, recv_sems, q, (x_, y_, 1 - c_)) for q in range(4)]
        return cps, cps, []

    return types.SimpleNamespace(
        args=[g], out_shape=[jax.ShapeDtypeStruct((4,) + g.shape[2:], g.dtype)], aliases={},
        sems=[pltpu.SemaphoreType.DMA((4,)), pltpu.SemaphoreType.DMA((4,))], make=make)


def _r_swap_chips(p):
    def make(rins, routs, sems):
        (p_ref,), (out_ref,), (send_sems, recv_sems) = rins, routs, sems
        x_, y_, c_ = _mesh_pos()
        chips = [(1 - x_, y_), (x_, 1 - y_), (1 - x_, 1 - y_)]
        cps = [_remote(p_ref.at[2 * cx + cy], out_ref.at[k], send_sems, recv_sems, k, (cx, cy, c_))
               for k, (cx, cy) in enumerate(chips)]
        return cps, cps, []

    return types.SimpleNamespace(
        args=[p], out_shape=[jax.ShapeDtypeStruct((3,) + p.shape[1:], p.dtype)], aliases={},
        sems=[pltpu.SemaphoreType.DMA((3,)), pltpu.SemaphoreType.DMA((3,))], make=make)


def _sigmoid(x):
    return 0.5 * (jnp.tanh(0.5 * x) + 1.0)


def _mm(a, b, *, kind, name, out_dtype=F32, ti=1024, tj=512, tr=2048, epi=None, epi_in=None, out_nblk=1, rider=None):
    b_blk = b.ndim == 3
    if kind == "tn":
        R, I = a.shape
    else:
        I, R = a.shape
    if kind == "nn":
        J = b.shape[1] if not b_blk else b.shape[0] * b.shape[2]
        nb_inner = b.shape[2] if b_blk else J
        r_inner = R
    elif kind == "nt":
        J = b.shape[0] if not b_blk else b.shape[1]
        nb_inner = J
        r_inner = b.shape[2] if b_blk else R
    else:
        J = b.shape[1]
        nb_inner = J
        r_inner = R
    out_inner = J // out_nblk
    ti = _tile(I, ti, LANES if kind == "tn" else SUBLANES)
    tj = _tile(min(nb_inner, out_inner), tj, LANES)
    assert nb_inner % tj == 0 and out_inner % tj == 0 and J % tj == 0
    tr = _tile(r_inner, tr, LANES)
    assert R % tr == 0
    ni, nj, nr = I // ti, J // tj, R // tr
    dn = {"nn": NN, "nt": NT, "tn": TN}[kind]

    if kind == "tn":
        a_spec = pl.BlockSpec((tr, ti), lambda i, j, r: (r, i))
    else:
        a_spec = pl.BlockSpec((ti, tr), lambda i, j, r: (i, r))
    if kind == "nn":
        if b_blk:
            per = nb_inner // tj
            b_spec = pl.BlockSpec((None, tr, tj), lambda i, j, r: (j // per, r, j % per))
        else:
            b_spec = pl.BlockSpec((tr, tj), lambda i, j, r: (r, j))
    elif kind == "nt":
        if b_blk:
            per = r_inner // tr
            b_spec = pl.BlockSpec((None, tj, tr), lambda i, j, r: (r // per, j, r % per))
        else:
            b_spec = pl.BlockSpec((tj, tr), lambda i, j, r: (j, r))
    else:
        b_spec = pl.BlockSpec((tr, tj), lambda i, j, r: (r, j))
    if out_nblk > 1:
        pero = out_inner // tj
        o_spec = pl.BlockSpec((None, ti, tj), lambda i, j, r: (j // pero, i, j % pero))
        o_shape = (out_nblk, I, out_inner)
    else:
        o_spec = pl.BlockSpec((ti, tj), lambda i, j, r: (i, j))
        o_shape = (I, J)
    in_specs = [a_spec, b_spec]
    args = [a, b]
    if epi == "mul2act":
        in_specs.append(pl.BlockSpec((ti, tj), lambda i, j, r: (i, j)))
        args.append(epi_in)
    if epi == "relu2":
        out_shape = (jax.ShapeDtypeStruct(o_shape, BF16), jax.ShapeDtypeStruct(o_shape, BF16))
        out_specs = (o_spec, o_spec)
    else:
        out_shape = jax.ShapeDtypeStruct(o_shape, out_dtype)
        out_specs = o_spec
    n_in = len(args)
    n_out = 2 if epi == "relu2" else 1

    def body(*refs):
        a_ref, b_ref = refs[0], refs[1]
        outs = refs[n_in:n_in + n_out]
        acc_ref = refs[n_in + n_out] if nr > 1 else None

        def finish(acc):
            if epi == "relu2":
                act = jnp.maximum(acc, 0.0)
                outs[0][...] = (act * act).astype(BF16)
                outs[1][...] = act.astype(BF16)
            elif epi == "mul2act":
                outs[0][...] = (acc * (2.0 * refs[2][...].astype(F32))).astype(out_dtype)
            else:
                outs[0][...] = acc.astype(out_dtype)

        part = _dot(a_ref[...], b_ref[...], dn)
        if nr == 1:
            finish(part)
        else:
            r = pl.program_id(2)

            @pl.when(r == 0)
            def _():
                acc_ref[...] = part

            @pl.when(r > 0)
            def _():
                acc_ref[...] += part

            @pl.when(r == nr - 1)
            def _():
                finish(acc_ref[...])

    res, rider_res = _host_call(
        body, name=name, grid=(ni, nj, nr), in_specs=in_specs,
        out_specs=out_specs if n_out > 1 else [out_specs], out_shape=out_shape if n_out > 1 else [out_shape],
        scratch=[pltpu.VMEM((ti, tj), F32)] if nr > 1 else [], args=args,
        sem=("parallel", "parallel", "arbitrary"), rider=rider)
    res = tuple(res) if n_out > 1 else res[0]
    return res if rider is None else (res, rider_res)


def _rstd(x):
    return lax.rsqrt(jnp.mean(x * x, axis=-1, keepdims=True) + RMS_EPS)


def _rms_bwd_rows(x, w, dy):
    r = _rstd(x)
    xh = x * r
    dxh = dy * w
    dx = r * (dxh - xh * jnp.mean(dxh * xh, axis=-1, keepdims=True))
    return dx, jnp.sum(dy * xh, axis=0, keepdims=True)


def _row_spec(tb, d):
    return pl.BlockSpec((tb, d), lambda i: (i, 0))


def _vec_spec(d):
    return pl.BlockSpec((1, d), lambda i: (0, 0))


def _rms_fwd(h, w, name):
    T, D = h.shape
    tb = _tile(T, 512, SUBLANES)

    def body(h_ref, w_ref, u_ref):
        x = h_ref[...]
        u_ref[...] = (x * _rstd(x) * w_ref[...]).astype(BF16)

    return _pcall(body, name=name, grid=(T // tb,), in_specs=[_row_spec(tb, D), _vec_spec(D)],
                  out_specs=_row_spec(tb, D), out_shape=jax.ShapeDtypeStruct((T, D), BF16),
                  compiler_params=_cp(("parallel",)))(h, w)


def _resid_fwd(h, m, w_post, w_next, name):
    T, D = h.shape
    tb = _tile(T, 256, SUBLANES)

    def body(h_ref, m_ref, wp_ref, wn_ref, ho_ref, u_ref):
        x = m_ref[...]
        hn = h_ref[...] + x * _rstd(x) * wp_ref[...]
        ho_ref[...] = hn
        u_ref[...] = (hn * _rstd(hn) * wn_ref[...]).astype(BF16)

    return _pcall(body, name=name, grid=(T // tb,),
                  in_specs=[_row_spec(tb, D), _row_spec(tb, D), _vec_spec(D), _vec_spec(D)],
                  out_specs=(_row_spec(tb, D), _row_spec(tb, D)),
                  out_shape=(jax.ShapeDtypeStruct((T, D), F32), jax.ShapeDtypeStruct((T, D), BF16)),
                  compiler_params=_cp(("parallel",)))(h, m, w_post, w_next)


def _final_fwd(h, m, w_post, tgt, name):
    T, D = h.shape
    tb = _tile(T, 256, SUBLANES)

    def body(h_ref, m_ref, wp_ref, t_ref, g_ref, l_ref):
        x = m_ref[...]
        e = h_ref[...] + x * _rstd(x) * wp_ref[...] - t_ref[...]
        g_ref[...] = e * (1.0 / D)
        s = jnp.sum(e * e, axis=0, keepdims=True)

        @pl.when(pl.program_id(0) == 0)
        def _():
            l_ref[...] = s

        @pl.when(pl.program_id(0) > 0)
        def _():
            l_ref[...] += s

    return _pcall(body, name=name, grid=(T // tb,),
                  in_specs=[_row_spec(tb, D), _row_spec(tb, D), _vec_spec(D), _row_spec(tb, D)],
                  out_specs=(_row_spec(tb, D), _vec_spec(D)),
                  out_shape=(jax.ShapeDtypeStruct((T, D), F32), jax.ShapeDtypeStruct((1, D), F32)),
                  compiler_params=_cp(("arbitrary",)))(h, m, w_post, tgt)


def _norm_bwd(g_out, name, du=None, h=None, w_pre=None, m=None, w_post=None):
    T, D = g_out.shape
    tb = _tile(T, 256, SUBLANES)
    has_pre, has_post = du is not None, m is not None
    args, in_specs = [g_out], [_row_spec(tb, D)]
    if has_pre:
        args += [du, h, w_pre]
        in_specs += [_row_spec(tb, D), _row_spec(tb, D), _vec_spec(D)]
    if has_post:
        args += [m, w_post]
        in_specs += [_row_spec(tb, D), _vec_spec(D)]
    out_shape, out_specs = [], []
    if has_pre:
        out_shape.append(jax.ShapeDtypeStruct((T, D), F32))
        out_specs.append(_row_spec(tb, D))
    if has_post:
        out_shape.append(jax.ShapeDtypeStruct((T, D), BF16))
        out_specs.append(_row_spec(tb, D))
    n_w = int(has_pre) + int(has_post)
    out_shape += [jax.ShapeDtypeStruct((1, D), F32)] * n_w
    out_specs += [_vec_spec(D)] * n_w
    n_in = len(args)

    def body(*refs):
        ins, outs = list(refs[:n_in]), list(refs[n_in:])
        g = ins.pop(0)[...]
        sums = []
        if has_pre:
            du_ref, h_ref, w_ref = ins.pop(0), ins.pop(0), ins.pop(0)
            dx, s = _rms_bwd_rows(h_ref[...], w_ref[...], du_ref[...])
            g = g + dx
            outs.pop(0)[...] = g
            sums.append(s)
        if has_post:
            m_ref, w_ref = ins.pop(0), ins.pop(0)
            dx, s = _rms_bwd_rows(m_ref[...], w_ref[...], g)
            outs.pop(0)[...] = dx.astype(BF16)
            sums.append(s)
        first = pl.program_id(0) == 0
        for o_ref, s in zip(outs, sums):
            @pl.when(first)
            def _(o_ref=o_ref, s=s):
                o_ref[...] = s

            @pl.when(jnp.logical_not(first))
            def _(o_ref=o_ref, s=s):
                o_ref[...] += s

    return _pcall(body, name=name, grid=(T // tb,), in_specs=in_specs, out_specs=tuple(out_specs),
                  out_shape=tuple(out_shape), compiler_params=_cp(("arbitrary",)))(*args)


def _ret_consts(lg, L):
    ii = lax.broadcasted_iota(jnp.int32, (L, L), 0).astype(F32)
    jj = lax.broadcasted_iota(jnp.int32, (L, L), 1).astype(F32)
    dmat = jnp.exp(jnp.abs(ii - jj) * lg)
    idx = lax.broadcasted_iota(jnp.int32, (L, 1), 0).astype(F32)
    xi = jnp.exp((idx + 1.0) * lg)
    zeta = jnp.exp((L - 1.0 - idx) * lg)
    cd = jnp.exp(jnp.full((1, 1), L, F32) * lg)
    return dmat, xi, zeta, cd


def _rot(t, cs, sn):
    half = t.shape[-1] // 2
    t1, t2 = t[:, :half], t[:, half:]
    return jnp.concatenate([t1 * cs - t2 * sn, t1 * sn + t2 * cs], axis=-1)


def _rot_bwd(d, cs, sn):
    half = d.shape[-1] // 2
    d1, d2 = d[:, :half], d[:, half:]
    return jnp.concatenate([d1 * cs + d2 * sn, d2 * cs - d1 * sn], axis=-1)


def _ret_specs(cf, tb, rev):
    H, DK, DV = cf.H, cf.DK, cf.DV
    ni = cf.T // tb
    ri = (lambda i: ni - 1 - i) if rev else (lambda i: i)
    q = pl.BlockSpec((tb, DK), lambda h, i: (ri(i), h))
    k = pl.BlockSpec((tb, DK), lambda h, i: (ri(i), H + h))
    v = pl.BlockSpec((tb, DV), lambda h, i: (ri(i), cf.QK * 2 // DV + h))
    g = pl.BlockSpec((tb, DV), lambda h, i: (ri(i), cf.QK * 2 // DV + H + h))
    cs = pl.BlockSpec((tb, DK // 2), lambda h, i: (ri(i), 0))
    gw = pl.BlockSpec((1, DV), lambda h, i: (0, h))
    row_v = pl.BlockSpec((tb, DV), lambda h, i: (ri(i), h))
    row_k = pl.BlockSpec((tb, DK), lambda h, i: (ri(i), h))
    st = pl.BlockSpec((tb // cf.L, None, DK, DV), lambda h, i: (ri(i), h, 0, 0))
    lgs = pl.BlockSpec(memory_space=pltpu.SMEM)
    return q, k, v, g, cs, gw, row_v, row_k, st, lgs


def _ret_fwd(cf, proj, cos, sin, gn_w, lgam, rider=None):
    T, L, H, DK, DV = cf.T, cf.L, cf.H, cf.DK, cf.DV
    tb = _tile(T, 512, L)
    nck = tb // L
    q_s, k_s, v_s, g_s, cs_s, gw_s, row_v, _, st_s, lg_s = _ret_specs(cf, tb, False)
    kscale = DK ** -0.5

    def body(lg_ref, q_ref, k_ref, v_ref, g_ref, cos_ref, sin_ref, gw_ref, y_ref, o_ref, st_ref, state):
        h = pl.program_id(0)

        @pl.when(pl.program_id(1) == 0)
        def _():
            state[...] = jnp.zeros_like(state)

        dmat, xi, zeta, cd = _ret_consts(lg_ref[h], L)
        gw = gw_ref[...]

        def chunk(c, carry):
            rows = pl.ds(pl.multiple_of(c * L, L), L)
            cs, sn = cos_ref[rows, :], sin_ref[rows, :]
            qr = _rot(q_ref[rows, :], cs, sn)
            kr = _rot(k_ref[rows, :], cs, sn) * kscale
            qb, kb = qr.astype(BF16), kr.astype(BF16)
            vb = v_ref[rows, :].astype(BF16)
            st = state[...]
            stb = st.astype(BF16)
            st_ref[c] = stb
            s = _dot(qb, kb, NT) * dmat
            o = _dot(s.astype(BF16), vb) + _dot(qb, stb) * xi
            state[...] = st * cd + _dot((kr * zeta).astype(BF16), vb, TN)
            o_ref[rows, :] = o
            mu = jnp.mean(o, axis=-1, keepdims=True)
            oc = o - mu
            var = jnp.mean(oc * oc, axis=-1, keepdims=True)
            n = oc * lax.rsqrt(var + GN_EPS) * gw
            gt = g_ref[rows, :]
            y_ref[rows, :] = (gt * _sigmoid(gt) * n).astype(BF16)
            return carry

        lax.fori_loop(0, nck, chunk, 0, unroll=True)

    return _host_call(
        body, name="ret_fwd", grid=(H, T // tb),
        in_specs=[lg_s, q_s, k_s, v_s, g_s, cs_s, cs_s, gw_s],
        out_specs=(row_v, row_v, st_s),
        out_shape=(jax.ShapeDtypeStruct((T, cf.V), BF16), jax.ShapeDtypeStruct((T, cf.V), F32),
                   jax.ShapeDtypeStruct((cf.NC, H, DK, DV), BF16)),
        scratch=[pltpu.VMEM((DK, DV), F32)], args=(lgam, proj, proj, proj, proj, cos, sin, gn_w),
        sem=("parallel", "arbitrary"), rider=rider)


def _ret_bwd(cf, proj, cos, sin, gn_w, lgam, o, states, dy):
    T, L, H, DK, DV = cf.T, cf.L, cf.H, cf.DK, cf.DV
    tb = _tile(T, 512, L)
    nck = tb // L
    q_s, k_s, v_s, g_s, cs_s, gw_s, row_v, row_k, st_s, lg_s = _ret_specs(cf, tb, True)
    kscale = DK ** -0.5

    ni = T // tb
    n_steps = H * ni
    col0 = (0, cf.QK, 2 * cf.QK, 2 * cf.QK + cf.V)
    widths = (DK, DK, DV, DV)

    def body(lg_ref, q_ref, k_ref, v_ref, g_ref, cos_ref, sin_ref, gw_ref, o_ref, st_ref, dy_ref,
             dp_ref, dgw_ref, dstate, dq_s, dk_s, dv_s, dg_s, sems):
        h = pl.program_id(0)
        step = h * ni + pl.program_id(1)
        slot = step % 2
        dq_ref, dk_ref, dv_ref, dg_ref = dq_s.at[slot], dk_s.at[slot], dv_s.at[slot], dg_s.at[slot]

        def results_out(s):
            hh, ii = s // ni, s % ni
            rows = pl.ds(pl.multiple_of((ni - 1 - ii) * tb, tb), tb)
            return [pltpu.make_async_copy(
                buf.at[s % 2], dp_ref.at[rows, pl.ds(pl.multiple_of(c0 + hh * w, LANES), w)], sems.at[s % 2, k])
                for k, (buf, c0, w) in enumerate(zip((dq_s, dk_s, dv_s, dg_s), col0, widths))]

        @pl.when(step >= 2)
        def _():
            for cp in results_out(step - 2):
                cp.wait()

        @pl.when(pl.program_id(1) == 0)
        def _():
            dstate[...] = jnp.zeros_like(dstate)
            dgw_ref[...] = jnp.zeros_like(dgw_ref)

        dmat, xi, zeta, cd = _ret_consts(lg_ref[h], L)
        gw = gw_ref[...]

        def chunk(t, carry):
            c = nck - 1 - t
            rows = pl.ds(pl.multiple_of(c * L, L), L)
            cs, sn = cos_ref[rows, :], sin_ref[rows, :]
            qr = _rot(q_ref[rows, :], cs, sn)
            kr = _rot(k_ref[rows, :], cs, sn) * kscale
            qb, kb = qr.astype(BF16), kr.astype(BF16)
            kzb = (kr * zeta).astype(BF16)
            vb = v_ref[rows, :].astype(BF16)
            s = (_dot(qb, kb, NT) * dmat).astype(BF16)
            oo = o_ref[rows, :]
            mu = jnp.mean(oo, axis=-1, keepdims=True)
            oc = oo - mu
            rstd = lax.rsqrt(jnp.mean(oc * oc, axis=-1, keepdims=True) + GN_EPS)
            oh = oc * rstd
            gt = g_ref[rows, :]
            sg = _sigmoid(gt)
            dyv = dy_ref[rows, :]
            dn = dyv * (gt * sg)
            dg_ref[rows, :] = (dyv * (oh * gw) * (sg * (1.0 + gt * (1.0 - sg)))).astype(BF16)
            dgw_ref[...] += jnp.sum(dn * oh, axis=0, keepdims=True)
            doh = dn * gw
            do = rstd * (doh - jnp.mean(doh, axis=-1, keepdims=True) - oh * jnp.mean(doh * oh, axis=-1, keepdims=True))
            dob = do.astype(BF16)
            doxb = (do * xi).astype(BF16)
            dst = dstate[...]
            dstb = dst.astype(BF16)
            stb = st_ref[c]
            dv_ref[rows, :] = (_dot(s, dob, TN) + _dot(kzb, dstb)).astype(BF16)
            ds = (_dot(dob, vb, NT) * dmat).astype(BF16)
            dqr = _dot(ds, kb) + _dot(doxb, stb, NT)
            dkr = _dot(ds, qb, TN) + _dot(vb, dstb, NT) * zeta
            dstate[...] = dst * cd + _dot(qb, doxb, TN)
            dq_ref[rows, :] = _rot_bwd(dqr, cs, sn).astype(BF16)
            dk_ref[rows, :] = _rot_bwd(dkr * kscale, cs, sn).astype(BF16)
            return carry

        lax.fori_loop(0, nck, chunk, 0, unroll=True)
        for cp in results_out(step):
            cp.start()

        @pl.when(step == n_steps - 1)
        def _():
            if n_steps > 1:
                for cp in results_out(step - 1):
                    cp.wait()
            for cp in results_out(step):
                cp.wait()

    return _pcall(
        body, name="ret_bwd", grid=(H, ni),
        in_specs=[lg_s, q_s, k_s, v_s, g_s, cs_s, cs_s, gw_s, row_v, st_s, row_v],
        out_specs=(ANY, gw_s),
        out_shape=(jax.ShapeDtypeStruct((T, cf.RIN), BF16), jax.ShapeDtypeStruct((1, cf.V), F32)),
        scratch_shapes=[pltpu.VMEM((DK, DV), F32), pltpu.VMEM((2, tb, DK), BF16), pltpu.VMEM((2, tb, DK), BF16),
                        pltpu.VMEM((2, tb, DV), BF16), pltpu.VMEM((2, tb, DV), BF16), pltpu.SemaphoreType.DMA((2, 4))],
        compiler_params=_cp(("arbitrary", "arbitrary")),
    )(lgam, proj, proj, proj, proj, cos, sin, gn_w, o, states, dy)


def _conv_pre(x, halo, w, b, first, W):
    tb = x.shape[0]
    ext = jnp.concatenate([jnp.where(first, 0.0, halo), x], axis=0)
    out = b + w[W - 1:W, :] * x
    for tap in range(W - 1):
        out = out + w[tap:tap + 1, :] * pltpu.roll(ext, W - 1 - tap, 0)[SUBLANES:SUBLANES + tb, :]
    return out, ext


def _conv_fwd(cf, proj, conv_w, conv_b):
    T, CD, W = cf.T, cf.CD, SSD_CONV_W
    tb = _tile(T, 512, SUBLANES)
    tc = _tile(CD, 512, LANES)
    off = cf.DI // tc
    nh = tb // SUBLANES

    def body(x_ref, halo_ref, w_ref, b_ref, o_ref):
        pre, _ = _conv_pre(x_ref[...], halo_ref[...], w_ref[...], b_ref[...], pl.program_id(1) == 0, W)
        o_ref[...] = pre * _sigmoid(pre)

    return _pcall(
        body, name="conv_fwd", grid=(CD // tc, T // tb),
        in_specs=[pl.BlockSpec((tb, tc), lambda j, i: (i, off + j)),
                  pl.BlockSpec((SUBLANES, tc), lambda j, i: (jnp.maximum(i * nh - 1, 0), off + j)),
                  pl.BlockSpec((W, tc), lambda j, i: (0, j)), pl.BlockSpec((1, tc), lambda j, i: (0, j))],
        out_specs=pl.BlockSpec((tb, tc), lambda j, i: (i, j)),
        out_shape=jax.ShapeDtypeStruct((T, CD), F32),
        compiler_params=_cp(("parallel", "arbitrary")),
    )(proj, proj, conv_w, conv_b)


def _conv_bwd_pre(cf, proj, conv_w, conv_b, dact):
    T, CD, W = cf.T, cf.CD, SSD_CONV_W
    tb = _tile(T, 512, SUBLANES)
    tc = _tile(CD, 512, LANES)
    off = cf.DI // tc
    nh = tb // SUBLANES

    def body(x_ref, halo_ref, w_ref, b_ref, da_ref, dp_ref, dw_ref, db_ref):
        x = x_ref[...]
        pre, ext = _conv_pre(x, halo_ref[...], w_ref[...], b_ref[...], pl.program_id(1) == 0, W)
        sg = _sigmoid(pre)
        dp = da_ref[...] * (sg * (1.0 + pre * (1.0 - sg)))
        dp_ref[...] = dp
        rows = [jnp.sum(dp * pltpu.roll(ext, W - 1 - tap, 0)[SUBLANES:SUBLANES + tb, :], axis=0, keepdims=True)
                for tap in range(W - 1)]
        rows.append(jnp.sum(dp * x, axis=0, keepdims=True))
        dw = jnp.concatenate(rows, axis=0)
        db = jnp.sum(dp, axis=0, keepdims=True)

        @pl.when(pl.program_id(1) == 0)
        def _():
            dw_ref[...] = dw
            db_ref[...] = db

        @pl.when(pl.program_id(1) > 0)
        def _():
            dw_ref[...] += dw
            db_ref[...] += db

    return _pcall(
        body, name="conv_bwd_pre", grid=(CD // tc, T // tb),
        in_specs=[pl.BlockSpec((tb, tc), lambda j, i: (i, off + j)),
                  pl.BlockSpec((SUBLANES, tc), lambda j, i: (jnp.maximum(i * nh - 1, 0), off + j)),
                  pl.BlockSpec((W, tc), lambda j, i: (0, j)), pl.BlockSpec((1, tc), lambda j, i: (0, j)),
                  pl.BlockSpec((tb, tc), lambda j, i: (i, j))],
        out_specs=(pl.BlockSpec((tb, tc), lambda j, i: (i, j)), pl.BlockSpec((W, tc), lambda j, i: (0, j)),
                   pl.BlockSpec((1, tc), lambda j, i: (0, j))),
        out_shape=(jax.ShapeDtypeStruct((T, CD), F32), jax.ShapeDtypeStruct((W, CD), F32),
                   jax.ShapeDtypeStruct((1, CD), F32)),
        compiler_params=_cp(("parallel", "arbitrary")),
    )(proj, proj, conv_w, conv_b, dact)


def _conv_bwd_x(cf, dpre, conv_w, into):
    T, CD, W = cf.T, cf.CD, SSD_CONV_W
    tb = _tile(T, 512, SUBLANES)
    tc = _tile(CD, 512, LANES)
    off = cf.DI // tc
    nh = tb // SUBLANES
    last_blk = T // SUBLANES - 1
    ni = T // tb

    def body(d_ref, halo_ref, w_ref, into_ref, o_ref):
        d = d_ref[...]
        w = w_ref[...]
        nxt = jnp.where(pl.program_id(1) == ni - 1, 0.0, halo_ref[...])
        ext = jnp.concatenate([d, nxt], axis=0)
        n = tb + SUBLANES
        out = w[W - 1:W, :] * d
        for tap in range(W - 1):
            out = out + w[tap:tap + 1, :] * pltpu.roll(ext, n - (W - 1 - tap), 0)[:tb, :]
        o_ref[...] = out.astype(BF16)

    return _pcall(
        body, name="conv_bwd_x", grid=(CD // tc, ni),
        in_specs=[pl.BlockSpec((tb, tc), lambda j, i: (i, j)),
                  pl.BlockSpec((SUBLANES, tc), lambda j, i: (jnp.minimum((i + 1) * nh, last_blk), j)),
                  pl.BlockSpec((W, tc), lambda j, i: (0, j)), ANY],
        out_specs=pl.BlockSpec((tb, tc), lambda j, i: (i, off + j)),
        out_shape=jax.ShapeDtypeStruct(into.shape, BF16), input_output_aliases={3: 0},
        compiler_params=_cp(("parallel", "arbitrary")),
    )(dpre, dpre, conv_w, into)


def _ssd_masks(cf, g, tb):
    L, GW, P, RG = cf.L, cf.GW, cf.P, cf.RG
    assert L == P and 2 * L == LANES and RG % 2 == 0
    i32 = jnp.int32
    hrow = lax.broadcasted_iota(i32, (LANES, GW), 0)
    hcol = lax.broadcasted_iota(i32, (LANES, GW), 1) // P
    expand = (hrow == g * RG + hcol).astype(BF16)
    ti = lax.broadcasted_iota(i32, (LANES, LANES), 0)
    tj = lax.broadcasted_iota(i32, (LANES, LANES), 1)
    btril = jnp.logical_and(ti // L == tj // L, ti >= tj).astype(BF16)
    r0 = lax.broadcasted_iota(i32, (L, GW), 0)
    c0 = lax.broadcasted_iota(i32, (L, GW), 1) % L
    tile_eye = (r0 == c0).astype(F32)
    lower = r0 >= c0
    p0 = lax.broadcasted_iota(i32, (2 * L, LANES), 0) // L
    p1 = lax.broadcasted_iota(i32, (2 * L, LANES), 1) // P
    pair = (p0 == p1).astype(F32)
    return expand, btril, tile_eye, lower, pair


def _softplus(x):
    return jnp.maximum(x, 0.0) + jnp.log1p(jnp.exp(-jnp.abs(x)))


def _split3(x):
    hi = x.astype(BF16)
    r1 = x - hi.astype(F32)
    mid = r1.astype(BF16)
    return hi, mid, (r1 - mid.astype(F32)).astype(BF16)


def _chunk_sums(btril, x, dn):
    hi, mid, lo = _split3(x)
    outs = []
    for k in range(x.shape[0] // LANES):
        sl = slice(k * LANES, (k + 1) * LANES)
        outs.append((_dot(btril, lo[sl], dn) + _dot(btril, mid[sl], dn)) + _dot(btril, hi[sl], dn))
    return jnp.concatenate(outs, axis=0)


def _expand_heads(x, expand, dn):
    hi, mid, lo = _split3(x)
    return (_dot(lo, expand, dn) + _dot(mid, expand, dn)) + _dot(hi, expand, dn)


def _ssd_chunk(cf, mk, acum, dt, xs, bm, cm):
    _, _, tile_eye, lower, pair = mk
    rowv = jnp.sum(acum * tile_eye, axis=0, keepdims=True)
    lf = jnp.exp(jnp.where(lower, acum - rowv, -1e30))
    xdt = xs * dt
    bb, cb_ = bm.astype(BF16), cm.astype(BF16)
    bb2 = jnp.concatenate([bb, bb], axis=0)
    cb2 = _dot(cb_, bb2, NT)
    ms, bds = [], []
    for j in range(cf.RG // 2):
        ln = slice(j * LANES, (j + 1) * LANES)
        ms.append((cb2 * lf[:, ln]).astype(BF16))
        xp = xdt[:, ln]
        bds.append((jnp.concatenate([xp, xp], axis=0) * pair).astype(BF16))
    return lf, xdt, bb, cb_, bb2, cb2, ms, bds


def _ssd_specs(cf, tb, rev):
    G, GW, N = cf.G, cf.GW, cf.N
    ni = cf.T // tb
    ri = (lambda i: ni - 1 - i) if rev else (lambda i: i)
    z = pl.BlockSpec((tb, GW), lambda g, i: (ri(i), g))
    dt = pl.BlockSpec((tb, LANES), lambda g, i: (ri(i), (cf.DI + cf.CD) // LANES))
    xs = pl.BlockSpec((tb, GW), lambda g, i: (ri(i), g))
    bm = pl.BlockSpec((tb, N), lambda g, i: (ri(i), cf.DI // N + g))
    cm = pl.BlockSpec((tb, N), lambda g, i: (ri(i), cf.DI // N + G + g))
    vec = pl.BlockSpec((1, GW), lambda g, i: (0, g))
    st = pl.BlockSpec((tb // cf.L, None, N, GW), lambda g, i: (ri(i), g, 0, 0))
    return z, dt, xs, bm, cm, vec, st


def _ssd_fwd(cf, proj, xact, bias_e, alog_e, dskip_e, norm_w):
    T, L, G, GW, N = cf.T, cf.L, cf.G, cf.GW, cf.N
    tb = _tile(T, 512, L)
    nck = tb // L
    z_s, dt_s, xs_s, b_s, c_s, vec_s, st_s = _ssd_specs(cf, tb, False)

    def body(z_ref, dt_ref, xs_ref, b_ref, c_ref, bias_ref, alog_ref, dsk_ref, nw_ref, y_ref, yp_ref, st_ref,
             state, dt_s, ac_s):
        @pl.when(pl.program_id(1) == 0)
        def _():
            state[...] = jnp.zeros_like(state)

        mk = _ssd_masks(cf, pl.program_id(0), tb)
        a_e = -jnp.exp(alog_ref[...])
        dt_all = _softplus(_expand_heads(dt_ref[...], mk[0], NN) + bias_ref[...])
        dt_s[...] = dt_all
        ac_s[...] = _chunk_sums(mk[1], dt_all * a_e, NN)

        def chunk(c, carry):
            rows = pl.ds(pl.multiple_of(c * L, L), L)
            acum = ac_s[rows, :]
            lf, xdt, bb, cb_, _, _, ms, bds = _ssd_chunk(cf, mk, acum, dt_s[rows, :], xs_ref[rows, :],
                                                         b_ref[rows, :], c_ref[rows, :])
            st = state[...]
            stb = st.astype(BF16)
            st_ref[c] = stb
            ydiag = jnp.concatenate([_dot(m, bd) for m, bd in zip(ms, bds)], axis=1)
            al = acum[L - 1:L, :]
            state[...] = st * jnp.exp(al) + _dot(bb, (xdt * jnp.exp(al - acum)).astype(BF16), TN)
            yp_ref[rows, :] = ydiag + _dot(cb_, stb) * jnp.exp(acum)
            return carry

        lax.fori_loop(0, nck, chunk, 0, unroll=True)
        z = z_ref[...]
        yg = (yp_ref[...] + dsk_ref[...] * xs_ref[...]) * (z * _sigmoid(z))
        y_ref[...] = (yg * _rstd(yg) * nw_ref[...]).astype(BF16)

    return _pcall(
        body, name="ssd_fwd", grid=(G, T // tb),
        in_specs=[z_s, dt_s, xs_s, b_s, c_s, vec_s, vec_s, vec_s, vec_s],
        out_specs=(z_s, z_s, st_s),
        out_shape=(jax.ShapeDtypeStruct((T, cf.DI), BF16), jax.ShapeDtypeStruct((T, cf.DI), F32),
                   jax.ShapeDtypeStruct((cf.NC, G, N, GW), BF16)),
        scratch_shapes=[pltpu.VMEM((N, GW), F32), pltpu.VMEM((tb, GW), F32), pltpu.VMEM((tb, GW), F32)],
        compiler_params=_cp(("parallel", "arbitrary")),
    )(proj, proj, xact, xact, xact, bias_e, alog_e, dskip_e, norm_w)


def _ssd_bwd(cf, proj, xact, bias_e, alog_e, dskip_e, norm_w, ypre, states, dy):
    T, L, G, GW, N, RG = cf.T, cf.L, cf.G, cf.GW, cf.N, cf.RG
    tb = _tile(T, 512, L)
    nck = tb // L
    ni = T // tb
    z_s, dt_s, xs_s, b_s, c_s, vec_s, st_s = _ssd_specs(cf, tb, True)
    bc_out = pl.BlockSpec((tb, N), lambda g, i: (ni - 1 - i, g))
    ddt_out = pl.BlockSpec((None, tb, LANES), lambda g, i: (g, ni - 1 - i, 0))

    def body(z_ref, dt_ref, xs_ref, b_ref, c_ref, bias_ref, alog_ref, dsk_ref, nw_ref, yp_ref, st_ref, dy_ref,
             dz_ref, dxs_ref, db_ref, dc_ref, ddt_ref, dnw_ref, ddsk_ref, dalog_ref, dbias_ref,
             dstate, dt_s, ac_s, sg_s, dys_s, dxdt_s, dac_s):
        @pl.when(pl.program_id(1) == 0)
        def _():
            dstate[...] = jnp.zeros_like(dstate)
            for r in (dnw_ref, ddsk_ref, dalog_ref, dbias_ref):
                r[...] = jnp.zeros_like(r)

        mk = _ssd_masks(cf, pl.program_id(0), tb)
        expand, btril, tile_eye, lower, pair = mk
        a_e = -jnp.exp(alog_ref[...])
        dsk, nw = dsk_ref[...], nw_ref[...]
        last_row = (lax.broadcasted_iota(jnp.int32, (L, 1), 0) == L - 1).astype(F32)
        raw = _expand_heads(dt_ref[...], expand, NN) + bias_ref[...]
        dt_all = _softplus(raw)
        dt_s[...] = dt_all
        sg_s[...] = _sigmoid(raw)
        ac_s[...] = _chunk_sums(btril, dt_all * a_e, NN)
        z = z_ref[...]
        sz = _sigmoid(z)
        silu = z * sz
        xs_all = xs_ref[...]
        yd = yp_ref[...] + dsk * xs_all
        yg = yd * silu
        rr = _rstd(yg)
        xh = yg * rr
        dout = dy_ref[...]
        dnw_ref[...] += jnp.sum(dout * xh, axis=0, keepdims=True)
        dxh = dout * nw
        dyg = rr * (dxh - xh * jnp.mean(dxh * xh, axis=-1, keepdims=True))
        dz_ref[...] = (dyg * yd * (sz * (1.0 + z * (1.0 - sz)))).astype(BF16)
        dys_all = dyg * silu
        dys_s[...] = dys_all
        ddsk_ref[...] += jnp.sum(dys_all * xs_all, axis=0, keepdims=True)

        def chunk(t, carry):
            c = nck - 1 - t
            rows = pl.ds(pl.multiple_of(c * L, L), L)
            acum = ac_s[rows, :]
            lf, xdt, bb, cb_, bb2, cb2, ms, bds = _ssd_chunk(cf, mk, acum, dt_s[rows, :], xs_ref[rows, :],
                                                             b_ref[rows, :], c_ref[rows, :])
            stb = st_ref[c]
            eac = jnp.exp(acum)
            al = acum[L - 1:L, :]
            eal = jnp.exp(al)
            dte = jnp.exp(al - acum)
            dys = dys_s[rows, :]
            dyb = dys.astype(BF16)
            dms, dxs_, dsegs = [], [], []
            dcb2 = None
            for j in range(RG // 2):
                ln = slice(j * LANES, (j + 1) * LANES)
                dyj = dyb[:, ln]
                dbd = _dot(ms[j], dyj, TN) * pair
                dxs_.append(dbd[:L, :] + dbd[L:, :])
                tj = _dot(dyj, bds[j], NT) * lf[:, ln]
                dcb2 = tj if dcb2 is None else dcb2 + tj
                dsegs.append(tj * cb2)
            dxdt = jnp.concatenate(dxs_, axis=1)
            dseg = jnp.concatenate(dsegs, axis=1)
            dcb2 = dcb2.astype(BF16)
            dcm = _dot(dcb2, bb2)
            dbm2 = _dot(dcb2, cb_, TN)
            dbm = dbm2[:L, :] + dbm2[L:, :]
            dacum = dseg - tile_eye * jnp.sum(dseg, axis=0, keepdims=True)
            dyo = (dys * eac).astype(BF16)
            dcm = dcm + _dot(dyo, stb, NT)
            dacum = dacum + dys * _dot(cb_, stb) * eac
            dst = dstate[...]
            dstb = dst.astype(BF16)
            xd = xdt * dte
            dbm = dbm + _dot(xd.astype(BF16), dstb, NT)
            dxd = _dot(bb, dstb)
            dal = jnp.sum(dst * stb.astype(F32), axis=0, keepdims=True) * eal
            dxdt = dxdt + dxd * dte
            tt = dxd * xd
            dacum = dacum - tt + last_row * (dal + jnp.sum(tt, axis=0, keepdims=True))
            dstate[...] = dst * eal + _dot(cb_, dyo, TN)
            dxdt_s[rows, :] = dxdt
            dac_s[rows, :] = dacum
            db_ref[rows, :] = dbm
            dc_ref[rows, :] = dcm
            return carry

        lax.fori_loop(0, nck, chunk, 0, unroll=True)
        dda = _chunk_sums(btril, dac_s[...], TN)
        dxdt_all = dxdt_s[...]
        dt_all = dt_s[...]
        dxs_ref[...] = dys_s[...] * dsk + dxdt_all * dt_all
        ddt = dxdt_all * xs_ref[...] + dda * a_e
        dalog_ref[...] += jnp.sum(dda * dt_all, axis=0, keepdims=True) * a_e
        draw = ddt * sg_s[...]
        dbias_ref[...] += jnp.sum(draw, axis=0, keepdims=True)
        ddt_ref[...] = _expand_heads(draw, expand, NT)

    GN = G * N
    return _pcall(
        body, name="ssd_bwd", grid=(G, ni),
        in_specs=[z_s, dt_s, xs_s, b_s, c_s, vec_s, vec_s, vec_s, vec_s, z_s, st_s, z_s],
        out_specs=(z_s, z_s, bc_out, bc_out, ddt_out, vec_s, vec_s, vec_s, vec_s),
        out_shape=(jax.ShapeDtypeStruct((T, cf.SINP), BF16), jax.ShapeDtypeStruct((T, cf.CD), F32),
                   jax.ShapeDtypeStruct((T, GN), F32), jax.ShapeDtypeStruct((T, GN), F32),
                   jax.ShapeDtypeStruct((G, T, LANES), F32)) + (jax.ShapeDtypeStruct((1, cf.DI), F32),) * 4,
        scratch_shapes=[pltpu.VMEM((N, GW), F32)] + [pltpu.VMEM((tb, GW), F32)] * 6,
        compiler_params=_cp(("parallel", "arbitrary")),
    )(proj, proj, xact, xact, xact, bias_e, alog_e, dskip_e, norm_w, ypre, states, dy)


def _sum_groups(parts, name, into, col_blk):
    G, T, W = parts.shape
    tb = _tile(T, 512, SUBLANES)

    def body(p_ref, into_ref, o_ref):
        acc = p_ref[0]
        for g in range(1, G):
            acc = acc + p_ref[g]
        o_ref[...] = acc.astype(BF16)

    return _pcall(body, name=name, grid=(T // tb,), in_specs=[pl.BlockSpec((G, tb, W), lambda i: (0, i, 0)), ANY],
                  out_specs=pl.BlockSpec((tb, W), lambda i: (i, col_blk)),
                  out_shape=jax.ShapeDtypeStruct(into.shape, BF16), input_output_aliases={1: 0},
                  compiler_params=_cp(("parallel",)))(parts, into)


def _fill_bc(dact, dbm, dcm, name):
    T, GN = dbm.shape
    tb = _tile(T, 512, SUBLANES)
    blk = (dact.shape[1] - 2 * GN) // (2 * GN)
    assert blk * 2 * GN == dact.shape[1] - 2 * GN

    def body(b_ref, c_ref, into_ref, o_ref):
        o_ref[:, :GN] = b_ref[...]
        o_ref[:, GN:] = c_ref[...]

    row = pl.BlockSpec((tb, GN), lambda i: (i, 0))
    return _pcall(body, name=name, grid=(T // tb,), in_specs=[row, row, ANY],
                  out_specs=pl.BlockSpec((tb, 2 * GN), lambda i: (i, blk)),
                  out_shape=jax.ShapeDtypeStruct(dact.shape, F32), input_output_aliases={2: 0},
                  compiler_params=_cp(("parallel",)))(dbm, dcm, dact)


def _all_gather(x, name):
    def body(x_ref, out_ref, send_sems, recv_sems, local_sem):
        x, y, c = lax.axis_index("x"), lax.axis_index("y"), lax.axis_index("c")
        me, sibling = (x, y, c), (x, y, 1 - c)
        chips = [(1 - x, y), (x, 1 - y), (1 - x, 1 - y)]

        def blk(px, py, pc):
            return out_ref.at[4 * px + 2 * py + pc]

        def copy(k, block, to, src=None):
            return pltpu.make_async_remote_copy(
                src_ref=blk(*block) if src is None else src, dst_ref=blk(*block),
                send_sem=send_sems.at[k], recv_sem=recv_sems.at[k], device_id=to, device_id_type=MESH)

        mine = pltpu.make_async_copy(x_ref, blk(*me), local_sem)
        mine.start()
        first = [copy(0, me, sibling, src=x_ref)]
        first += [copy(1 + j, me, (*chip, c), src=x_ref) for j, chip in enumerate(chips)]
        for cp in first:
            cp.start()
        passed = [copy(4 + j, (*chip, c), sibling) for j, chip in enumerate(chips)]
        for j, chip in enumerate(chips):
            copy(1 + j, (*chip, c), me).wait_recv()
            passed[j].start()
        copy(0, sibling, me).wait_recv()
        for j, chip in enumerate(chips):
            copy(4 + j, (*chip, 1 - c), me).wait_recv()
        for cp in first + passed:
            cp.wait_send()
        mine.wait()

    return _pcall(
        body, name=name, in_specs=[ANY], out_specs=ANY,
        out_shape=jax.ShapeDtypeStruct((N_DEV,) + x.shape, x.dtype),
        scratch_shapes=[pltpu.SemaphoreType.DMA((7,)), pltpu.SemaphoreType.DMA((7,)), pltpu.SemaphoreType.DMA],
    )(x)


def _core_sum(g, recv, idx, name):
    _, _, a, b = g.shape
    tr = _tile(a, max(SUBLANES, (2 * 2**20) // (4 * b) // SUBLANES * SUBLANES), SUBLANES)

    def body(idx_ref, g_ref, r_ref, p_ref, own_ref):
        s = g_ref[...] + r_ref[...]
        p_ref[...] = s.astype(BF16)

        @pl.when(pl.program_id(1) == idx_ref[1])
        def _():
            own_ref[...] = s

    return _pcall(
        body, name=name,
        grid_spec=pltpu.PrefetchScalarGridSpec(
            num_scalar_prefetch=1, grid=(a // tr, 4),
            in_specs=[pl.BlockSpec((None, None, tr, b), lambda i, q, idx: (q, idx[0], i, 0)),
                      pl.BlockSpec((None, tr, b), lambda i, q, idx: (q, i, 0))],
            out_specs=(pl.BlockSpec((None, tr, b), lambda i, q, idx: (q, i, 0)),
                       pl.BlockSpec((tr, b), lambda i, q, idx: (i, 0)))),
        out_shape=(jax.ShapeDtypeStruct((4, a, b), BF16), jax.ShapeDtypeStruct((a, b), F32)),
        compiler_params=_cp(("parallel", "arbitrary")),
    )(idx, g, recv)


def _adam_math(w, g, m, v):
    m = ADAM_B1 * m + (1.0 - ADAM_B1) * g
    v = ADAM_B2 * v + (1.0 - ADAM_B2) * (g * g)
    m_hat = m / (1.0 - ADAM_B1 ** ADAM_STEP)
    v_hat = v / (1.0 - ADAM_B2 ** ADAM_STEP)
    delta = -ADAM_LR * (m_hat / (jnp.sqrt(v_hat) + ADAM_EPS) + ADAM_WD * w)
    return delta, m, v


def _chip_sum_adam(own, recv, w, m, v, layer, name, into=None):
    a, b = own.shape
    n = w.shape[0]
    tr = _tile(a, max(SUBLANES, (2**20) // (4 * b) // SUBLANES * SUBLANES), SUBLANES)
    wspec = pl.BlockSpec((None, tr, b), lambda i: (layer, i, 0))
    ospec = pl.BlockSpec((tr, b), lambda i: (i, 0))
    n_into = 0 if into is None else 4

    def body(own_ref, r_ref, w_ref, m_ref, v_ref, *rest):
        g_ref, d_ref, mo_ref, vo_ref = rest[n_into:]
        g = own_ref[...]
        for k in range(3):
            g = g + r_ref[k].astype(F32)
        g_ref[...] = g
        d_ref[...], mo_ref[...], vo_ref[...] = _adam_math(w_ref[...], g, m_ref[...], v_ref[...])

    return _pcall(
        body, name=name, grid=(a // tr,),
        in_specs=[ospec, pl.BlockSpec((3, tr, b), lambda i: (0, i, 0)), wspec, wspec, wspec] + [ANY] * n_into,
        out_specs=(wspec,) * 4, out_shape=(jax.ShapeDtypeStruct((n, a, b), F32),) * 4,
        input_output_aliases={5 + k: k for k in range(n_into)},
        compiler_params=_cp(("parallel",)),
    )(own, recv, w, m, v, *(into or ()))


def _all_reduce_small(x, n_fold, fold_w, name):
    R, W = x.shape

    def body(x_ref, out_ref, buf, send_sems, recv_sems):
        xx, y, c = lax.axis_index("x"), lax.axis_index("y"), lax.axis_index("c")
        me = 4 * xx + 2 * y + c
        buf[me] = x_ref[...]
        copies = []
        for k in range(1, N_DEV):
            px, py, pc = xx ^ (k >> 2), y ^ ((k >> 1) & 1), c ^ (k & 1)
            copies.append(pltpu.make_async_remote_copy(
                src_ref=x_ref, dst_ref=buf.at[me], send_sem=send_sems.at[k - 1], recv_sem=recv_sems.at[k - 1],
                device_id=(px, py, pc), device_id_type=MESH))
        for cp in copies:
            cp.start()
        for cp in copies:
            cp.wait()
        acc = buf[0]
        for j in range(1, N_DEV):
            acc = acc + buf[j]
        out_ref[...] = acc
        if n_fold:
            l0 = lax.broadcasted_iota(jnp.int32, (W, W), 0) // fold_w
            l1 = lax.broadcasted_iota(jnp.int32, (W, W), 1)
            fold = (l0 == l1).astype(F32)
            out_ref[R - n_fold:, :] = _dot(acc[R - n_fold:, :], fold, NN, HI)

    return _pcall(
        body, name=name, in_specs=[pl.BlockSpec(memory_space=pltpu.VMEM)],
        out_specs=pl.BlockSpec(memory_space=pltpu.VMEM), out_shape=jax.ShapeDtypeStruct((R, W), F32),
        scratch_shapes=[pltpu.VMEM((N_DEV, R, W), F32), pltpu.SemaphoreType.DMA((N_DEV - 1,)),
                        pltpu.SemaphoreType.DMA((N_DEV - 1,))],
        compiler_params=pltpu.CompilerParams(vmem_limit_bytes=VMEM_LIMIT),
    )(x)


def _adam_small(w, g, m, v, name):
    def body(w_ref, g_ref, m_ref, v_ref, d_ref, mo_ref, vo_ref):
        d_ref[...], mo_ref[...], vo_ref[...] = _adam_math(w_ref[...], g_ref[...], m_ref[...], v_ref[...])

    return _pcall(body, name=name, out_shape=(jax.ShapeDtypeStruct(w.shape, F32),) * 3)(w, g, m, v)


def kernel(x, norm_mix_pre, norm_mix_post, norm_ffn_pre, norm_ffn_post, ret_w_in, ret_gn_w, ret_w_out, ssd_w_in, ssd_conv_w, ssd_conv_b, ssd_dt_bias, ssd_a_log, ssd_d, ssd_norm_w, ssd_w_out, mlp_w_up, mlp_w_down, loss_target, m_norm_mix_pre, m_norm_mix_post, m_norm_ffn_pre, m_norm_ffn_post, m_ret_w_in, m_ret_gn_w, m_ret_w_out, m_ssd_w_in, m_ssd_conv_w, m_ssd_conv_b, m_ssd_dt_bias, m_ssd_a_log, m_ssd_d, m_ssd_norm_w, m_ssd_w_out, m_mlp_w_up, m_mlp_w_down, v_norm_mix_pre, v_norm_mix_post, v_norm_ffn_pre, v_norm_ffn_post, v_ret_w_in, v_ret_gn_w, v_ret_w_out, v_ssd_w_in, v_ssd_conv_w, v_ssd_conv_b, v_ssd_dt_bias, v_ssd_a_log, v_ssd_d, v_ssd_norm_w, v_ssd_w_out, v_mlp_w_up, v_mlp_w_down):
    cf = _cfg()
    T, D = cf.T, cf.D
    ax, ay, ac = lax.axis_index("x"), lax.axis_index("y"), lax.axis_index("c")
    my_dev = 4 * ax + 2 * ay + ac
    idx = jnp.stack([ac, 2 * ax + ay]).astype(jnp.int32)

    w_ri = _all_gather(ret_w_in[0].T.astype(BF16), "ag_ret_w_in").reshape(cf.RIN, D)
    chips_of = lambda ws: _rider_join([_r_gather_chips(w.astype(BF16)) for w in ws])
    cores_of = lambda bufs: _rider_join([_r_gather_cores(b) for b in bufs])
    cw, cbw, nww = cf.CD // N_DEV, cf.CD // N_DEV, cf.DI // N_DEV
    small = jnp.concatenate([ssd_conv_w[0], ssd_conv_b, jnp.pad(ssd_norm_w, ((0, 0), (0, cw - nww))),
                             jnp.zeros((2, cw), F32)], axis=0)
    small = _all_gather(small, "ag_ssd_small")
    conv_w = jnp.transpose(small[:, :SSD_CONV_W, :], (1, 0, 2)).reshape(SSD_CONV_W, cf.CD)
    conv_b = small[:, SSD_CONV_W, :].reshape(1, cf.CD)
    ssd_nw = small[:, SSD_CONV_W + 1, :nww].reshape(1, cf.DI)

    half = cf.DK // 2
    inv_freq = ROPE_BASE ** (-jnp.arange(half, dtype=F32) / half)
    ang = jnp.arange(T).astype(F32)[:, None] * inv_freq[None, :]
    cos, sin = jnp.cos(ang), jnp.sin(ang)
    lgam = jnp.log1p(-jnp.exp2(-5.0 - jnp.arange(cf.H, dtype=F32)))
    rep = lambda p: jnp.repeat(p.reshape(1, cf.SH), cf.P, axis=1)
    bias_e, alog_e, dskip_e = rep(ssd_dt_bias), rep(ssd_a_log), rep(ssd_d)

    h0 = x.reshape(T, D)
    tgt = loss_target.reshape(T, D)
    nrm = lambda p, i: p[i:i + 1]

    u0 = _rms_fwd(h0, nrm(norm_mix_pre, 0), "rms_fwd0")
    proj0, part_a = _mm(u0, w_ri, kind="nt", name="mm_ret_in",
                        rider=chips_of([ret_w_out[0], mlp_w_up[0].T, mlp_w_down[0]]))
    (y0, o0, st0), got = _ret_fwd(cf, proj0, cos, sin, ret_gn_w, lgam, rider=_rider_join(
        [cores_of(part_a), chips_of([ssd_w_in[0], ssd_w_out[0], mlp_w_up[1].T])]))
    w_ro, w_up0, w_dn0 = got[0].reshape(cf.V, D), got[1].reshape(cf.FF, D), got[2].reshape(cf.FF, D)
    part_b = got[3:]
    m0 = _mm(y0, w_ro, kind="nn", name="mm_ret_out", tr=cf.V)
    h1, u1 = _resid_fwd(h0, m0, nrm(norm_mix_post, 0), nrm(norm_ffn_pre, 0), "resid_fwd0")
    (sq0, act0), part_c = _mm(u1, w_up0, kind="nt", name="mm_up0", epi="relu2", rider=chips_of([mlp_w_down[1]]))
    f0, got = _mm(sq0, w_dn0, kind="nn", name="mm_down0", tj=1024, rider=cores_of(part_b + part_c))
    w_si = jnp.pad(jnp.transpose(got[0], (1, 0, 2)).reshape(D, cf.SIN), ((0, 0), (0, cf.SINP - cf.SIN)))
    w_so, w_up1, w_dn1 = got[1].reshape(cf.DI, D), got[2].reshape(cf.FF, D), got[3].reshape(cf.FF, D)
    w_up, w_dn = [w_up0, w_up1], [w_dn0, w_dn1]
    h2, u2 = _resid_fwd(h1, f0, nrm(norm_ffn_post, 0), nrm(norm_mix_pre, 1), "resid_fwd1")
    proj1 = _mm(u2, w_si, kind="nn", name="mm_ssd_in", tj=1152)
    xact = _conv_fwd(cf, proj1, conv_w, conv_b)
    y1, yp1, st1 = _ssd_fwd(cf, proj1, xact, bias_e, alog_e, dskip_e, ssd_nw)
    m1 = _mm(y1, w_so, kind="nn", name="mm_ssd_out", tr=cf.DI)
    h3, u3 = _resid_fwd(h2, m1, nrm(norm_mix_post, 1), nrm(norm_ffn_pre, 1), "resid_fwd2")
    sq1, act1 = _mm(u3, w_up[1], kind="nt", name="mm_up1", epi="relu2")
    f1 = _mm(sq1, w_dn[1], kind="nn", name="mm_down1", tj=1024)
    g4, lsum = _final_fwd(h3, f1, nrm(norm_ffn_post, 1), tgt, "final_fwd")
    loss = lax.psum(0.5 * jnp.sum(lsum) / D, ("x", "y", "c"))

    as4 = lambda g: g.reshape(4, 2, g.shape[1], g.shape[2])
    swap_cores = lambda g: _r_swap_cores(as4(g))
    core_sum = lambda g, recv, name: _core_sum(as4(g), recv, idx, name + "_core_sum")

    def mlp_bwd(l, df, u, sq, act, rider=None, then=None):
        dpre = _mm(df, w_dn[l], kind="nt", name=f"mm_dpre{l}", out_dtype=BF16, epi="mul2act", epi_in=act, rider=rider)
        dpre, got = dpre if rider is not None else (dpre, [])
        g_dn = _mm(sq, df, kind="tn", name=f"mm_gdown{l}", tj=1024, tr=4096, rider=then(got) if then else None)
        g_dn, got = g_dn if then else (g_dn, [])
        g_dn = g_dn.reshape(N_DEV, cf.FF // N_DEV, D)
        du, (rc,) = _mm(dpre, w_up[l], kind="nn", name=f"mm_du_mlp{l}", tj=1024, tr=4096, rider=swap_cores(g_dn))
        part, own = core_sum(g_dn, rc, f"rs_mlp_down{l}")
        g_up, (r2,) = _mm(u, dpre, kind="tn", name=f"mm_gup{l}", tj=1024, tr=4096, out_nblk=N_DEV, rider=_r_swap_chips(part))
        return du, g_up, own, r2, got

    df1, g_nfpost1 = _norm_bwd(g4, "norm_bwd4", m=f1, w_post=nrm(norm_ffn_post, 1))
    du3, g_up1, own_dn1, r2_dn1, _ = mlp_bwd(1, df1, u3, sq1, act1, None)
    gh3, dm1, g_nfpre1, g_nmpost1 = _norm_bwd(g4, "norm_bwd3", du=du3, h=h3, w_pre=nrm(norm_ffn_pre, 1),
                                              m=m1, w_post=nrm(norm_mix_post, 1))
    dy1, (rc,) = _mm(dm1, w_so, kind="nt", name="mm_dy_ssd", rider=swap_cores(g_up1))
    part, own_up1 = core_sum(g_up1, rc, "rs_mlp_up1")
    g_so, (r2_up1,) = _mm(y1, dm1, kind="tn", name="mm_g_ssd_out", tj=1024, tr=4096, rider=_r_swap_chips(part))
    g_so = g_so.reshape(N_DEV, cf.DI // N_DEV, D)
    dz, dxs, dbm, dcm, ddt_parts, g_ssd_nw, g_dskip_e, g_alog_e, g_bias_e = _ssd_bwd(
        cf, proj1, xact, bias_e, alog_e, dskip_e, ssd_nw, yp1, st1, dy1)
    dact = _fill_bc(dxs, dbm, dcm, "ssd_dact_fill")
    dpre1, g_conv_w, g_conv_b = _conv_bwd_pre(cf, proj1, conv_w, conv_b, dact)
    assert cf.SINP == cf.DI + cf.CD + LANES
    dproj1 = _conv_bwd_x(cf, dpre1, conv_w, dz)
    dproj1 = _sum_groups(ddt_parts, "ssd_ddt_sum", dproj1, (cf.DI + cf.CD) // LANES)
    du2, (rc,) = _mm(dproj1, w_si, kind="nt", name="mm_du_ssd", tj=1024, tr=3456, rider=swap_cores(g_so))
    part, own_so = core_sum(g_so, rc, "rs_ssd_w_out")
    g_si, (r2_so,) = _mm(u2, dproj1, kind="tn", name="mm_g_ssd_in", ti=512, tj=1152, tr=4096, rider=_r_swap_chips(part))
    g_si = jnp.transpose(g_si[:, :cf.SIN].reshape(D, N_DEV, cf.SIN // N_DEV), (1, 0, 2))
    gh2, df0, g_nmpre1, g_nfpost0 = _norm_bwd(gh3, "norm_bwd2", du=du2, h=h2, w_pre=nrm(norm_mix_pre, 1),
                                              m=f0, w_post=nrm(norm_ffn_post, 0))
    own_si = []

    def si_chips(got):
        part, own = core_sum(g_si, got[0], "rs_ssd_w_in")
        own_si.append(own)
        return _r_swap_chips(part)

    du1, g_up0, own_dn0, r2_dn0, (r2_si,) = mlp_bwd(0, df0, u1, sq0, act0, swap_cores(g_si), si_chips)
    own_si = own_si[0]
    gh1, dm0, g_nfpre0, g_nmpost0 = _norm_bwd(gh2, "norm_bwd1", du=du1, h=h1, w_pre=nrm(norm_ffn_pre, 0),
                                              m=m0, w_post=nrm(norm_mix_post, 0))
    dy0, (rc,) = _mm(dm0, w_ro, kind="nt", name="mm_dy_ret", rider=swap_cores(g_up0))
    part, own_up0 = core_sum(g_up0, rc, "rs_mlp_up0")
    g_ro, (r2_up0,) = _mm(y0, dm0, kind="tn", name="mm_g_ret_out", tj=1024, tr=4096, rider=_r_swap_chips(part))
    g_ro = g_ro.reshape(N_DEV, cf.V // N_DEV, D)
    dproj0, g_gn = _ret_bwd(cf, proj0, cos, sin, ret_gn_w, lgam, o0, st0, dy0)
    rin8 = cf.RIN // N_DEV
    g_ri, (rc,) = _mm(u0, dproj0, kind="tn", name="mm_g_ret_in", ti=512, tj=rin8, tr=4096, out_nblk=N_DEV, rider=swap_cores(g_ro))
    part, own_ro = core_sum(g_ro, rc, "rs_ret_w_out")
    du0, (r2_ro, rc) = _mm(dproj0, w_ri, kind="nn", name="mm_du_ret", tj=1024, tr=4096,
                           rider=_rider_join([_r_swap_chips(part), swap_cores(g_ri)]))
    part, own_ri = core_sum(g_ri, rc, "rs_ret_w_in")
    grad_x, g_nmpre0 = _norm_bwd(gh1, "norm_bwd0", du=du0, h=h0, w_pre=nrm(norm_mix_pre, 0))
    (r2_ri,) = _comm(_r_swap_chips(part), "rs_ret_w_in_swap_chips")

    g_nmpre = jnp.concatenate([g_nmpre0, g_nmpre1], axis=0)
    g_nmpost = jnp.concatenate([g_nmpost0, g_nmpost1], axis=0)
    g_nfpre = jnp.concatenate([g_nfpre0, g_nfpre1], axis=0)
    g_nfpost = jnp.concatenate([g_nfpost0, g_nfpost1], axis=0)
    segs = [g_nmpre, g_nmpost, g_nfpre, g_nfpost, g_gn, g_conv_w, g_conv_b, g_ssd_nw, g_bias_e, g_alog_e, g_dskip_e]
    flat = jnp.concatenate([s.reshape(-1, LANES) for s in segs], axis=0)
    n_fold = 3 * cf.DI // LANES
    red = _all_reduce_small(flat, n_fold, cf.P, "all_reduce_small")
    outs, r0 = [], 0
    for s in segs:
        nr = s.size // LANES
        outs.append(red[r0:r0 + nr])
        r0 += nr
    (g_nmpre, g_nmpost, g_nfpre, g_nfpost) = [o.reshape(DEPTH, D) for o in outs[:4]]
    g_gn = outs[4].reshape(1, cf.V)
    g_conv_w = lax.dynamic_slice_in_dim(outs[5].reshape(SSD_CONV_W, cf.CD), my_dev * cw, cw, axis=1)[None]
    g_conv_b = lax.dynamic_slice_in_dim(outs[6].reshape(1, cf.CD), my_dev * cbw, cbw, axis=1)
    g_ssd_nw = lax.dynamic_slice_in_dim(outs[7].reshape(1, cf.DI), my_dev * nww, nww, axis=1)
    per_row = LANES // cf.P
    g_bias, g_alog, g_dskip = [o[:, :per_row].reshape(1, cf.SH) for o in outs[8:]]

    def rs(own, recv, w, m, v, layer, name, into=None):
        return _chip_sum_adam(own, recv, w, m, v, layer, name + "_adam", into)

    r_ri = rs(own_ri, r2_ri, ret_w_in, m_ret_w_in, v_ret_w_in, 0, "rs_ret_w_in")
    r_ro = rs(own_ro, r2_ro, ret_w_out, m_ret_w_out, v_ret_w_out, 0, "rs_ret_w_out")
    r_si = rs(own_si, r2_si, ssd_w_in, m_ssd_w_in, v_ssd_w_in, 0, "rs_ssd_w_in")
    r_so = rs(own_so, r2_so, ssd_w_out, m_ssd_w_out, v_ssd_w_out, 0, "rs_ssd_w_out")
    r_up = rs(own_up1, r2_up1, mlp_w_up, m_mlp_w_up, v_mlp_w_up, 1, "rs_mlp_up1")
    r_up = rs(own_up0, r2_up0, mlp_w_up, m_mlp_w_up, v_mlp_w_up, 0, "rs_mlp_up0", r_up)
    r_dn = rs(own_dn1, r2_dn1, mlp_w_down, m_mlp_w_down, v_mlp_w_down, 1, "rs_mlp_down1")
    r_dn = rs(own_dn0, r2_dn0, mlp_w_down, m_mlp_w_down, v_mlp_w_down, 0, "rs_mlp_down0", r_dn)
    lead = list

    def small_adam(w, g, m, v, name):
        return [g] + list(_adam_small(w, g, m, v, name))

    results = {
        "norm_mix_pre": small_adam(norm_mix_pre, g_nmpre, m_norm_mix_pre, v_norm_mix_pre, "adam_nmpre"),
        "norm_mix_post": small_adam(norm_mix_post, g_nmpost, m_norm_mix_post, v_norm_mix_post, "adam_nmpost"),
        "norm_ffn_pre": small_adam(norm_ffn_pre, g_nfpre, m_norm_ffn_pre, v_norm_ffn_pre, "adam_nfpre"),
        "norm_ffn_post": small_adam(norm_ffn_post, g_nfpost, m_norm_ffn_post, v_norm_ffn_post, "adam_nfpost"),
        "ret_w_in": lead(r_ri),
        "ret_gn_w": small_adam(ret_gn_w, g_gn, m_ret_gn_w, v_ret_gn_w, "adam_gn"),
        "ret_w_out": lead(r_ro),
        "ssd_w_in": lead(r_si),
        "ssd_conv_w": small_adam(ssd_conv_w, g_conv_w, m_ssd_conv_w, v_ssd_conv_w, "adam_conv_w"),
        "ssd_conv_b": small_adam(ssd_conv_b, g_conv_b, m_ssd_conv_b, v_ssd_conv_b, "adam_conv_b"),
        "ssd_dt_bias": small_adam(ssd_dt_bias, g_bias, m_ssd_dt_bias, v_ssd_dt_bias, "adam_dt_bias"),
        "ssd_a_log": small_adam(ssd_a_log, g_alog, m_ssd_a_log, v_ssd_a_log, "adam_a_log"),
        "ssd_d": small_adam(ssd_d, g_dskip, m_ssd_d, v_ssd_d, "adam_d"),
        "ssd_norm_w": small_adam(ssd_norm_w, g_ssd_nw, m_ssd_norm_w, v_ssd_norm_w, "adam_ssd_nw"),
        "ssd_w_out": lead(r_so),
        "mlp_w_up": r_up,
        "mlp_w_down": r_dn,
    }
    names = list(results)
    out = [loss, grad_x.reshape(1, T, D)]
    for k in range(4):
        out += [results[n][k] for n in names]
    return tuple(out)
```

```python
import functools
import math
import types

import jax
import jax.numpy as jnp
from jax import lax
from jax.experimental import pallas as pl
from jax.experimental.pallas import tpu as pltpu

F32 = jnp.float32
BF16 = jnp.bfloat16
HI = lax.Precision.HIGHEST
NN = (((1,), (0,)), ((), ()))
NT = (((1,), (1,)), ((), ()))
TN = (((0,), (0,)), ((), ()))
MESH = pl.DeviceIdType.MESH

V7X_VMEM_BYTES = 64 * 2**20
VMEM_LIMIT = V7X_VMEM_BYTES - 8 * 2**20
LANES = 128
SUBLANES = 8
N_DEV = 8

D_MODEL = 2048
SEQ = 8192
DEPTH = 2
CHUNK = 64
RMS_EPS = 1e-6
RET_HEAD_DK = 256
ROPE_BASE = 10000.0
GN_EPS = 1e-5
SSD_HEADDIM = 64
SSD_HEADS_PER_GROUP = 8
SSD_STATE = 128
SSD_CONV_W = 4
ADAM_LR = 0.001
ADAM_B1 = 0.9
ADAM_B2 = 0.999
ADAM_EPS = 1e-08
ADAM_WD = 0.01
ADAM_STEP = 10


def _cfg():
    c = types.SimpleNamespace()
    c.D, c.T, c.L = D_MODEL, SEQ, CHUNK
    c.DK = RET_HEAD_DK
    c.H = c.D // c.DK
    c.QK = c.H * c.DK
    c.DV = 2 * c.DK
    c.V = c.H * c.DV
    c.RIN = 2 * c.QK + 2 * c.V
    c.DI = 2 * c.D
    c.P = SSD_HEADDIM
    c.SH = c.DI // c.P
    c.RG = SSD_HEADS_PER_GROUP
    c.G = c.SH // c.RG
    c.GW = c.RG * c.P
    c.N = SSD_STATE
    c.CD = c.DI + 2 * c.G * c.N
    c.SIN = c.DI + c.CD + c.SH
    c.SINP = -(-c.SIN // LANES) * LANES
    c.FF = 4 * c.D
    c.NC = c.T // c.L
    return c


def _pcall(body, **kw):
    return pl.pallas_call(body, **kw)


def _cp(sem=None):
    return pltpu.CompilerParams(dimension_semantics=sem, vmem_limit_bytes=VMEM_LIMIT)


def _tile(n, pref, mult):
    if n <= pref:
        return n
    t = (pref // mult) * mult
    while t >= mult:
        if n % t == 0:
            return t
        t -= mult
    return n


def _dot(a, b, dn=NN, prec=None):
    return lax.dot_general(a, b, dn, precision=prec, preferred_element_type=F32)


ANY = pl.BlockSpec(memory_space=pl.ANY)


def _mesh_pos():
    return lax.axis_index("x"), lax.axis_index("y"), lax.axis_index("c")


def _rider_join(riders):
    j = types.SimpleNamespace(args=[], out_shape=[], aliases={}, sems=[])
    parts = []
    for r in riders:
        a0, o0, s0 = len(j.args), len(j.out_shape), len(j.sems)
        parts.append((r, a0, o0, s0))
        j.aliases.update({a0 + k: o0 + v for k, v in r.aliases.items()})
        j.args += r.args
        j.out_shape += r.out_shape
        j.sems += r.sems

    def make(rins, routs, sems):
        sends, recvs, locs = [], [], []
        for r, a0, o0, s0 in parts:
            s, rc, lc = r.make(rins[a0:a0 + len(r.args)], routs[o0:o0 + len(r.out_shape)], sems[s0:s0 + len(r.sems)])
            sends += s
            recvs += rc
            locs += lc
        return sends, recvs, locs

    j.make = make
    return j


def _rider_start(rider, rins, routs, sems):
    sends, _, locs = rider.make(rins, routs, sems)
    for cp in locs + sends:
        cp.start()


def _rider_wait(rider, rins, routs, sems):
    sends, recvs, locs = rider.make(rins, routs, sems)
    for cp in recvs:
        cp.wait_recv()
    for cp in sends:
        cp.wait_send()
    for cp in locs:
        cp.wait()


def _host_call(body, *, name, grid, in_specs, out_specs, out_shape, scratch, args, sem, rider=None):
    in_specs, out_specs, out_shape, scratch, args = map(list, (in_specs, out_specs, out_shape, scratch, args))
    if rider is None:
        res = _pcall(body, name=name, grid=grid, in_specs=in_specs, out_specs=out_specs, out_shape=out_shape,
                     scratch_shapes=scratch, compiler_params=_cp(sem))(*args)
        return list(res), []
    n_in, n_out, n_scr = len(args), len(out_shape), len(scratch)
    n_ra, n_ro = len(rider.args), len(rider.out_shape)

    def full(*refs):
        ins, rins = refs[:n_in], refs[n_in:n_in + n_ra]
        p = n_in + n_ra
        outs, routs = refs[p:p + n_out], refs[p + n_out:p + n_out + n_ro]
        p += n_out + n_ro
        scr, sems = refs[p:p + n_scr], refs[p + n_scr:]
        first = functools.reduce(jnp.logical_and, [pl.program_id(k) == 0 for k in range(len(grid))])
        last = functools.reduce(jnp.logical_and, [pl.program_id(k) == grid[k] - 1 for k in range(len(grid))])

        @pl.when(first)
        def _():
            _rider_start(rider, rins, routs, sems)

        body(*ins, *outs, *scr)

        @pl.when(last)
        def _():
            _rider_wait(rider, rins, routs, sems)

    res = _pcall(full, name=name, grid=grid, in_specs=in_specs + [ANY] * n_ra, out_specs=out_specs + [ANY] * n_ro,
                 out_shape=out_shape + rider.out_shape, scratch_shapes=scratch + rider.sems,
                 input_output_aliases={n_in + k: n_out + v for k, v in rider.aliases.items()},
                 compiler_params=_cp(("arbitrary",) * len(grid)))(*args, *rider.args)
    return list(res[:n_out]), list(res[n_out:])


def _comm(rider, name):
    n_ra, n_ro = len(rider.args), len(rider.out_shape)

    def body(*refs):
        rins, routs, sems = refs[:n_ra], refs[n_ra:n_ra + n_ro], refs[n_ra + n_ro:]
        _rider_start(rider, rins, routs, sems)
        _rider_wait(rider, rins, routs, sems)

    return list(_pcall(body, name=name, in_specs=[ANY] * n_ra, out_specs=[ANY] * n_ro, out_shape=rider.out_shape,
                       scratch_shapes=rider.sems, input_output_aliases=dict(rider.aliases))(*rider.args))


def _remote(src, dst, send_sems, recv_sems, k, to):
    return pltpu.make_async_remote_copy(src_ref=src, dst_ref=dst, send_sem=send_sems.at[k], recv_sem=recv_sems.at[k],
                                        device_id=to, device_id_type=MESH)


def _r_gather_chips(x):
    def make(rins, routs, sems):
        (x_ref,), (out_ref,), (send_sems, recv_sems, local_sem) = rins, routs, sems
        x_, y_, c_ = _mesh_pos()
        me = 4 * x_ + 2 * y_ + c_
        peers = [(x_, y_, 1 - c_), (1 - x_, y_, c_), (x_, 1 - y_, c_), (1 - x_, 1 - y_, c_)]
        sends = [_remote(x_ref, out_ref.at[me], send_sems, recv_sems, k, to) for k, to in enumerate(peers)]
        recvs = [_remote(x_ref, out_ref.at[4 * px + 2 * py + pc], send_sems, recv_sems, k, (px, py, pc))
                 for k, (px, py, pc) in enumerate(peers)]
        return sends, recvs, [pltpu.make_async_copy(x_ref, out_ref.at[me], local_sem)]

    return types.SimpleNamespace(
        args=[x], out_shape=[jax.ShapeDtypeStruct((N_DEV,) + x.shape, x.dtype)], aliases={},
        sems=[pltpu.SemaphoreType.DMA((4,)), pltpu.SemaphoreType.DMA((4,)), pltpu.SemaphoreType.DMA], make=make)


def _r_gather_cores(buf):
    def make(rins, routs, sems):
        (out_ref,), (send_sems, recv_sems) = routs, sems
        x_, y_, c_ = _mesh_pos()
        chips = [(1 - x_, y_), (x_, 1 - y_), (1 - x_, 1 - y_)]
        sends = [_remote(out_ref.at[4 * cx + 2 * cy + c_], out_ref.at[4 * cx + 2 * cy + c_], send_sems, recv_sems, k,
                         (x_, y_, 1 - c_)) for k, (cx, cy) in enumerate(chips)]
        recvs = [_remote(out_ref.at[4 * cx + 2 * cy + 1 - c_], out_ref.at[4 * cx + 2 * cy + 1 - c_], send_sems,
                         recv_sems, k, (x_, y_, 1 - c_)) for k, (cx, cy) in enumerate(chips)]
        return sends, recvs, []

    return types.SimpleNamespace(
        args=[buf], out_shape=[jax.ShapeDtypeStruct(buf.shape, buf.dtype)], aliases={0: 0},
        sems=[pltpu.SemaphoreType.DMA((3,)), pltpu.SemaphoreType.DMA((3,))], make=make)


def _r_swap_cores(g):
    def make(rins, routs, sems):
        (g_ref,), (out_ref,), (send_sems, recv_sems) = rins, routs, sems
        x_, y_, c_ = _mesh_pos()
        cps = [_remote(g_ref.at[q, 1 - c_], out_ref.at[q], send_sems, recv_sems, q, (x_, y_, 1 - c_)) for q in range(4)]
        return cps, cps, []

    return types.SimpleNamespace(
        args=[g], out_shape=[jax.ShapeDtypeStruct((4,) + g.shape[2:], g.dtype)], aliases={},
        sems=[pltpu.SemaphoreType.DMA((4,)), pltpu.SemaphoreType.DMA((4,))], make=make)


def _r_swap_chips(p):
    def make(rins, routs, sems):
        (p_ref,), (out_ref,), (send_sems, recv_sems) = rins, routs, sems
        x_, y_, c_ = _mesh_pos()
        chips = [(1 - x_, y_), (x_, 1 - y_), (1 - x_, 1 - y_)]
        cps = [_remote(p_ref.at[2 * cx + cy], out_ref.at[k], send_sems, recv_sems, k, (cx, cy, c_))
               for k, (cx, cy) in enumerate(chips)]
        return cps, cps, []

    return types.SimpleNamespace(
        args=[p], out_shape=[jax.ShapeDtypeStruct((3,) + p.shape[1:], p.dtype)], aliases={},
        sems=[pltpu.SemaphoreType.DMA((3,)), pltpu.SemaphoreType.DMA((3,))], make=make)


def _sigmoid(x):
    return 0.5 * (jnp.tanh(0.5 * x) + 1.0)


def _mm(a, b, *, kind, name, out_dtype=F32, ti=1024, tj=512, tr=2048, epi=None, epi_in=None, out_nblk=1, rider=None,
        i_part=(0, 1)):
    b_blk = b.ndim == 3
    if kind == "tn":
        R, I = a.shape
        I //= i_part[1]
    else:
        I, R = a.shape
    if kind == "nn":
        J = b.shape[1] if not b_blk else b.shape[0] * b.shape[2]
        nb_inner = b.shape[2] if b_blk else J
        r_inner = R
    elif kind == "nt":
        J = b.shape[0] if not b_blk else b.shape[1]
        nb_inner = J
        r_inner = b.shape[2] if b_blk else R
    else:
        J = b.shape[1]
        nb_inner = J
        r_inner = R
    out_inner = J // out_nblk
    ti = _tile(I, ti, LANES if kind == "tn" else SUBLANES)
    tj = _tile(min(nb_inner, out_inner), tj, LANES)
    assert nb_inner % tj == 0 and out_inner % tj == 0 and J % tj == 0
    tr = _tile(r_inner, tr, LANES)
    assert R % tr == 0
    ni, nj, nr = I // ti, J // tj, R // tr
    dn = {"nn": NN, "nt": NT, "tn": TN}[kind]

    if kind == "tn":
        i_off = i_part[0] * ni
        a_spec = pl.BlockSpec((tr, ti), lambda i, j, r: (r, i + i_off))
    else:
        a_spec = pl.BlockSpec((ti, tr), lambda i, j, r: (i, r))
    if kind == "nn":
        if b_blk:
            per = nb_inner // tj
            b_spec = pl.BlockSpec((None, tr, tj), lambda i, j, r: (j // per, r, j % per))
        else:
            b_spec = pl.BlockSpec((tr, tj), lambda i, j, r: (r, j))
    elif kind == "nt":
        if b_blk:
            per = r_inner // tr
            b_spec = pl.BlockSpec((None, tj, tr), lambda i, j, r: (r // per, j, r % per))
        else:
            b_spec = pl.BlockSpec((tj, tr), lambda i, j, r: (j, r))
    else:
        b_spec = pl.BlockSpec((tr, tj), lambda i, j, r: (r, j))
    if out_nblk > 1:
        pero = out_inner // tj
        o_spec = pl.BlockSpec((None, ti, tj), lambda i, j, r: (j // pero, i, j % pero))
        o_shape = (out_nblk, I, out_inner)
    else:
        o_spec = pl.BlockSpec((ti, tj), lambda i, j, r: (i, j))
        o_shape = (I, J)
    in_specs = [a_spec, b_spec]
    args = [a, b]
    if epi == "mul2act":
        in_specs.append(pl.BlockSpec((ti, tj), lambda i, j, r: (i, j)))
        args.append(epi_in)
    if epi == "relu2":
        out_shape = (jax.ShapeDtypeStruct(o_shape, BF16), jax.ShapeDtypeStruct(o_shape, BF16))
        out_specs = (o_spec, o_spec)
    else:
        out_shape = jax.ShapeDtypeStruct(o_shape, out_dtype)
        out_specs = o_spec
    n_in = len(args)
    n_out = 2 if epi == "relu2" else 1

    def body(*refs):
        a_ref, b_ref = refs[0], refs[1]
        outs = refs[n_in:n_in + n_out]
        acc_ref = refs[n_in + n_out] if nr > 1 else None

        def finish(acc):
            if epi == "relu2":
                act = jnp.maximum(acc, 0.0)
                outs[0][...] = (act * act).astype(BF16)
                outs[1][...] = act.astype(BF16)
            elif epi == "mul2act":
                outs[0][...] = (acc * (2.0 * refs[2][...].astype(F32))).astype(out_dtype)
            else:
                outs[0][...] = acc.astype(out_dtype)

        part = _dot(a_ref[...], b_ref[...], dn)
        if nr == 1:
            finish(part)
        else:
            r = pl.program_id(2)

            @pl.when(r == 0)
            def _():
                acc_ref[...] = part

            @pl.when(r > 0)
            def _():
                acc_ref[...] += part

            @pl.when(r == nr - 1)
            def _():
                finish(acc_ref[...])

    res, rider_res = _host_call(
        body, name=name, grid=(ni, nj, nr), in_specs=in_specs,
        out_specs=out_specs if n_out > 1 else [out_specs], out_shape=out_shape if n_out > 1 else [out_shape],
        scratch=[pltpu.VMEM((ti, tj), F32)] if nr > 1 else [], args=args,
        sem=("parallel", "parallel", "arbitrary"), rider=rider)
    res = tuple(res) if n_out > 1 else res[0]
    return res if rider is None else (res, rider_res)


def _rstd(x):
    return lax.rsqrt(jnp.mean(x * x, axis=-1, keepdims=True) + RMS_EPS)


def _rms_bwd_rows(x, w, dy):
    r = _rstd(x)
    xh = x * r
    dxh = dy * w
    dx = r * (dxh - xh * jnp.mean(dxh * xh, axis=-1, keepdims=True))
    return dx, jnp.sum(dy * xh, axis=0, keepdims=True)


def _row_spec(tb, d):
    return pl.BlockSpec((tb, d), lambda i: (i, 0))


def _vec_spec(d):
    return pl.BlockSpec((1, d), lambda i: (0, 0))


def _rms_fwd(h, w, name):
    T, D = h.shape
    tb = _tile(T, 512, SUBLANES)

    def body(h_ref, w_ref, u_ref):
        x = h_ref[...]
        u_ref[...] = (x * _rstd(x) * w_ref[...]).astype(BF16)

    return _pcall(body, name=name, grid=(T // tb,), in_specs=[_row_spec(tb, D), _vec_spec(D)],
                  out_specs=_row_spec(tb, D), out_shape=jax.ShapeDtypeStruct((T, D), BF16),
                  compiler_params=_cp(("parallel",)))(h, w)


def _resid_fwd(h, m, w_post, w_next, name):
    T, D = h.shape
    tb = _tile(T, 256, SUBLANES)

    def body(h_ref, m_ref, wp_ref, wn_ref, ho_ref, u_ref):
        x = m_ref[...]
        hn = h_ref[...] + x * _rstd(x) * wp_ref[...]
        ho_ref[...] = hn
        u_ref[...] = (hn * _rstd(hn) * wn_ref[...]).astype(BF16)

    return _pcall(body, name=name, grid=(T // tb,),
                  in_specs=[_row_spec(tb, D), _row_spec(tb, D), _vec_spec(D), _vec_spec(D)],
                  out_specs=(_row_spec(tb, D), _row_spec(tb, D)),
                  out_shape=(jax.ShapeDtypeStruct((T, D), F32), jax.ShapeDtypeStruct((T, D), BF16)),
                  compiler_params=_cp(("parallel",)))(h, m, w_post, w_next)


def _final_fwd(h, m, w_post, tgt, name):
    T, D = h.shape
    tb = _tile(T, 256, SUBLANES)

    def body(h_ref, m_ref, wp_ref, t_ref, g_ref, l_ref):
        x = m_ref[...]
        e = h_ref[...] + x * _rstd(x) * wp_ref[...] - t_ref[...]
        g_ref[...] = e * (1.0 / D)
        s = jnp.sum(e * e, axis=0, keepdims=True)

        @pl.when(pl.program_id(0) == 0)
        def _():
            l_ref[...] = s

        @pl.when(pl.program_id(0) > 0)
        def _():
            l_ref[...] += s

    return _pcall(body, name=name, grid=(T // tb,),
                  in_specs=[_row_spec(tb, D), _row_spec(tb, D), _vec_spec(D), _row_spec(tb, D)],
                  out_specs=(_row_spec(tb, D), _vec_spec(D)),
                  out_shape=(jax.ShapeDtypeStruct((T, D), F32), jax.ShapeDtypeStruct((1, D), F32)),
                  compiler_params=_cp(("arbitrary",)))(h, m, w_post, tgt)


def _norm_bwd(g_out, name, du=None, h=None, w_pre=None, m=None, w_post=None, rider=None):
    T, D = g_out.shape
    tb = _tile(T, 256, SUBLANES)
    has_pre, has_post = du is not None, m is not None
    args, in_specs = [g_out], [_row_spec(tb, D)]
    if has_pre:
        args += [du, h, w_pre]
        in_specs += [_row_spec(tb, D), _row_spec(tb, D), _vec_spec(D)]
    if has_post:
        args += [m, w_post]
        in_specs += [_row_spec(tb, D), _vec_spec(D)]
    out_shape, out_specs = [], []
    if has_pre:
        out_shape.append(jax.ShapeDtypeStruct((T, D), F32))
        out_specs.append(_row_spec(tb, D))
    if has_post:
        out_shape.append(jax.ShapeDtypeStruct((T, D), BF16))
        out_specs.append(_row_spec(tb, D))
    n_w = int(has_pre) + int(has_post)
    out_shape += [jax.ShapeDtypeStruct((1, D), F32)] * n_w
    out_specs += [_vec_spec(D)] * n_w
    n_in = len(args)

    def body(*refs):
        ins, outs = list(refs[:n_in]), list(refs[n_in:])
        g = ins.pop(0)[...]
        sums = []
        if has_pre:
            du_ref, h_ref, w_ref = ins.pop(0), ins.pop(0), ins.pop(0)
            dx, s = _rms_bwd_rows(h_ref[...], w_ref[...], du_ref[...])
            g = g + dx
            outs.pop(0)[...] = g
            sums.append(s)
        if has_post:
            m_ref, w_ref = ins.pop(0), ins.pop(0)
            dx, s = _rms_bwd_rows(m_ref[...], w_ref[...], g)
            outs.pop(0)[...] = dx.astype(BF16)
            sums.append(s)
        first = pl.program_id(0) == 0
        for o_ref, s in zip(outs, sums):
            @pl.when(first)
            def _(o_ref=o_ref, s=s):
                o_ref[...] = s

            @pl.when(jnp.logical_not(first))
            def _(o_ref=o_ref, s=s):
                o_ref[...] += s

    res, rider_res = _host_call(body, name=name, grid=(T // tb,), in_specs=in_specs, out_specs=out_specs,
                                out_shape=out_shape, scratch=[], args=args, sem=("arbitrary",), rider=rider)
    return tuple(res) if rider is None else (tuple(res), rider_res)


def _ret_consts(lg, L):
    ii = lax.broadcasted_iota(jnp.int32, (L, L), 0).astype(F32)
    jj = lax.broadcasted_iota(jnp.int32, (L, L), 1).astype(F32)
    dmat = jnp.exp(jnp.abs(ii - jj) * lg)
    idx = lax.broadcasted_iota(jnp.int32, (L, 1), 0).astype(F32)
    xi = jnp.exp((idx + 1.0) * lg)
    zeta = jnp.exp((L - 1.0 - idx) * lg)
    cd = jnp.exp(jnp.full((1, 1), L, F32) * lg)
    return dmat, xi, zeta, cd


def _rot(t, cs, sn):
    half = t.shape[-1] // 2
    t1, t2 = t[:, :half], t[:, half:]
    return jnp.concatenate([t1 * cs - t2 * sn, t1 * sn + t2 * cs], axis=-1)


def _rot_bwd(d, cs, sn):
    half = d.shape[-1] // 2
    d1, d2 = d[:, :half], d[:, half:]
    return jnp.concatenate([d1 * cs + d2 * sn, d2 * cs - d1 * sn], axis=-1)


def _ret_specs(cf, tb, rev):
    H, DK, DV = cf.H, cf.DK, cf.DV
    ni = cf.T // tb
    ri = (lambda i: ni - 1 - i) if rev else (lambda i: i)
    q = pl.BlockSpec((tb, DK), lambda h, i: (ri(i), h))
    k = pl.BlockSpec((tb, DK), lambda h, i: (ri(i), H + h))
    v = pl.BlockSpec((tb, DV), lambda h, i: (ri(i), cf.QK * 2 // DV + h))
    g = pl.BlockSpec((tb, DV), lambda h, i: (ri(i), cf.QK * 2 // DV + H + h))
    cs = pl.BlockSpec((tb, DK // 2), lambda h, i: (ri(i), 0))
    gw = pl.BlockSpec((1, DV), lambda h, i: (0, h))
    row_v = pl.BlockSpec((tb, DV), lambda h, i: (ri(i), h))
    row_k = pl.BlockSpec((tb, DK), lambda h, i: (ri(i), h))
    st = pl.BlockSpec((tb // cf.L, None, DK, DV), lambda h, i: (ri(i), h, 0, 0))
    lgs = pl.BlockSpec(memory_space=pltpu.SMEM)
    return q, k, v, g, cs, gw, row_v, row_k, st, lgs


def _ret_fwd(cf, proj, cos, sin, gn_w, lgam, rider=None):
    T, L, H, DK, DV = cf.T, cf.L, cf.H, cf.DK, cf.DV
    tb = _tile(T, 512, L)
    nck = tb // L
    q_s, k_s, v_s, g_s, cs_s, gw_s, row_v, _, st_s, lg_s = _ret_specs(cf, tb, False)
    kscale = DK ** -0.5

    def body(lg_ref, q_ref, k_ref, v_ref, g_ref, cos_ref, sin_ref, gw_ref, y_ref, o_ref, st_ref, state):
        h = pl.program_id(0)

        @pl.when(pl.program_id(1) == 0)
        def _():
            state[...] = jnp.zeros_like(state)

        dmat, xi, zeta, cd = _ret_consts(lg_ref[h], L)
        gw = gw_ref[...]

        def chunk(c, carry):
            rows = pl.ds(pl.multiple_of(c * L, L), L)
            cs, sn = cos_ref[rows, :], sin_ref[rows, :]
            qr = _rot(q_ref[rows, :], cs, sn)
            kr = _rot(k_ref[rows, :], cs, sn) * kscale
            qb, kb = qr.astype(BF16), kr.astype(BF16)
            vb = v_ref[rows, :].astype(BF16)
            st = state[...]
            stb = st.astype(BF16)
            st_ref[c] = stb
            s = _dot(qb, kb, NT) * dmat
            o = _dot(s.astype(BF16), vb) + _dot(qb, stb) * xi
            state[...] = st * cd + _dot((kr * zeta).astype(BF16), vb, TN)
            o_ref[rows, :] = o
            mu = jnp.mean(o, axis=-1, keepdims=True)
            oc = o - mu
            var = jnp.mean(oc * oc, axis=-1, keepdims=True)
            n = oc * lax.rsqrt(var + GN_EPS) * gw
            gt = g_ref[rows, :]
            y_ref[rows, :] = (gt * _sigmoid(gt) * n).astype(BF16)
            return carry

        lax.fori_loop(0, nck, chunk, 0, unroll=True)

    return _host_call(
        body, name="ret_fwd", grid=(H, T // tb),
        in_specs=[lg_s, q_s, k_s, v_s, g_s, cs_s, cs_s, gw_s],
        out_specs=(row_v, row_v, st_s),
        out_shape=(jax.ShapeDtypeStruct((T, cf.V), BF16), jax.ShapeDtypeStruct((T, cf.V), F32),
                   jax.ShapeDtypeStruct((cf.NC, H, DK, DV), BF16)),
        scratch=[pltpu.VMEM((DK, DV), F32)], args=(lgam, proj, proj, proj, proj, cos, sin, gn_w),
        sem=("parallel", "arbitrary"), rider=rider)


def _ret_bwd(cf, proj, cos, sin, gn_w, lgam, o, states, dy, rider=None):
    T, L, H, DK, DV = cf.T, cf.L, cf.H, cf.DK, cf.DV
    tb = _tile(T, 512, L)
    nck = tb // L
    q_s, k_s, v_s, g_s, cs_s, gw_s, row_v, row_k, st_s, lg_s = _ret_specs(cf, tb, True)
    kscale = DK ** -0.5

    ni = T // tb
    n_steps = H * ni
    col0 = (0, cf.QK, 2 * cf.QK, 2 * cf.QK + cf.V)
    widths = (DK, DK, DV, DV)

    def body(lg_ref, q_ref, k_ref, v_ref, g_ref, cos_ref, sin_ref, gw_ref, o_ref, st_ref, dy_ref,
             dp_ref, dgw_ref, dstate, dq_s, dk_s, dv_s, dg_s, sems):
        h = pl.program_id(0)
        step = h * ni + pl.program_id(1)
        slot = step % 2
        dq_ref, dk_ref, dv_ref, dg_ref = dq_s.at[slot], dk_s.at[slot], dv_s.at[slot], dg_s.at[slot]

        def results_out(s):
            hh, ii = s // ni, s % ni
            rows = pl.ds(pl.multiple_of((ni - 1 - ii) * tb, tb), tb)
            return [pltpu.make_async_copy(
                buf.at[s % 2], dp_ref.at[rows, pl.ds(pl.multiple_of(c0 + hh * w, LANES), w)], sems.at[s % 2, k])
                for k, (buf, c0, w) in enumerate(zip((dq_s, dk_s, dv_s, dg_s), col0, widths))]

        @pl.when(step >= 2)
        def _():
            for cp in results_out(step - 2):
                cp.wait()

        @pl.when(pl.program_id(1) == 0)
        def _():
            dstate[...] = jnp.zeros_like(dstate)
            dgw_ref[...] = jnp.zeros_like(dgw_ref)

        dmat, xi, zeta, cd = _ret_consts(lg_ref[h], L)
        gw = gw_ref[...]

        def chunk(t, carry):
            c = nck - 1 - t
            rows = pl.ds(pl.multiple_of(c * L, L), L)
            cs, sn = cos_ref[rows, :], sin_ref[rows, :]
            qr = _rot(q_ref[rows, :], cs, sn)
            kr = _rot(k_ref[rows, :], cs, sn) * kscale
            qb, kb = qr.astype(BF16), kr.astype(BF16)
            kzb = (kr * zeta).astype(BF16)
            vb = v_ref[rows, :].astype(BF16)
            s = (_dot(qb, kb, NT) * dmat).astype(BF16)
            oo = o_ref[rows, :]
            mu = jnp.mean(oo, axis=-1, keepdims=True)
            oc = oo - mu
            rstd = lax.rsqrt(jnp.mean(oc * oc, axis=-1, keepdims=True) + GN_EPS)
            oh = oc * rstd
            gt = g_ref[rows, :]
            sg = _sigmoid(gt)
            dyv = dy_ref[rows, :]
            dn = dyv * (gt * sg)
            dg_ref[rows, :] = (dyv * (oh * gw) * (sg * (1.0 + gt * (1.0 - sg)))).astype(BF16)
            dgw_ref[...] += jnp.sum(dn * oh, axis=0, keepdims=True)
            doh = dn * gw
            do = rstd * (doh - jnp.mean(doh, axis=-1, keepdims=True) - oh * jnp.mean(doh * oh, axis=-1, keepdims=True))
            dob = do.astype(BF16)
            doxb = (do * xi).astype(BF16)
            dst = dstate[...]
            dstb = dst.astype(BF16)
            stb = st_ref[c]
            dv_ref[rows, :] = (_dot(s, dob, TN) + _dot(kzb, dstb)).astype(BF16)
            ds = (_dot(dob, vb, NT) * dmat).astype(BF16)
            dqr = _dot(ds, kb) + _dot(doxb, stb, NT)
            dkr = _dot(ds, qb, TN) + _dot(vb, dstb, NT) * zeta
            dstate[...] = dst * cd + _dot(qb, doxb, TN)
            dq_ref[rows, :] = _rot_bwd(dqr, cs, sn).astype(BF16)
            dk_ref[rows, :] = _rot_bwd(dkr * kscale, cs, sn).astype(BF16)
            return carry

        lax.fori_loop(0, nck, chunk, 0, unroll=True)
        for cp in results_out(step):
            cp.start()

        @pl.when(step == n_steps - 1)
        def _():
            if n_steps > 1:
                for cp in results_out(step - 1):
                    cp.wait()
            for cp in results_out(step):
                cp.wait()

    return _host_call(
        body, name="ret_bwd", grid=(H, ni),
        in_specs=[lg_s, q_s, k_s, v_s, g_s, cs_s, cs_s, gw_s, row_v, st_s, row_v],
        out_specs=(ANY, gw_s),
        out_shape=(jax.ShapeDtypeStruct((T, cf.RIN), BF16), jax.ShapeDtypeStruct((1, cf.V), F32)),
        scratch=[pltpu.VMEM((DK, DV), F32), pltpu.VMEM((2, tb, DK), BF16), pltpu.VMEM((2, tb, DK), BF16),
                 pltpu.VMEM((2, tb, DV), BF16), pltpu.VMEM((2, tb, DV), BF16), pltpu.SemaphoreType.DMA((2, 4))],
        args=(lgam, proj, proj, proj, proj, cos, sin, gn_w, o, states, dy),
        sem=("arbitrary", "arbitrary"), rider=rider)


def _conv_pre(x, halo, w, b, first, W):
    tb = x.shape[0]
    ext = jnp.concatenate([jnp.where(first, 0.0, halo), x], axis=0)
    out = b + w[W - 1:W, :] * x
    for tap in range(W - 1):
        out = out + w[tap:tap + 1, :] * pltpu.roll(ext, W - 1 - tap, 0)[SUBLANES:SUBLANES + tb, :]
    return out, ext


def _conv_fwd(cf, proj, conv_w, conv_b):
    T, CD, W = cf.T, cf.CD, SSD_CONV_W
    tb = _tile(T, 512, SUBLANES)
    tc = _tile(CD, 512, LANES)
    off = cf.DI // tc
    nh = tb // SUBLANES

    def body(x_ref, halo_ref, w_ref, b_ref, o_ref):
        pre, _ = _conv_pre(x_ref[...], halo_ref[...], w_ref[...], b_ref[...], pl.program_id(1) == 0, W)
        o_ref[...] = pre * _sigmoid(pre)

    return _pcall(
        body, name="conv_fwd", grid=(CD // tc, T // tb),
        in_specs=[pl.BlockSpec((tb, tc), lambda j, i: (i, off + j)),
                  pl.BlockSpec((SUBLANES, tc), lambda j, i: (jnp.maximum(i * nh - 1, 0), off + j)),
                  pl.BlockSpec((W, tc), lambda j, i: (0, j)), pl.BlockSpec((1, tc), lambda j, i: (0, j))],
        out_specs=pl.BlockSpec((tb, tc), lambda j, i: (i, j)),
        out_shape=jax.ShapeDtypeStruct((T, CD), F32),
        compiler_params=_cp(("parallel", "arbitrary")),
    )(proj, proj, conv_w, conv_b)


def _conv_bwd_pre(cf, proj, conv_w, conv_b, dact):
    T, CD, W = cf.T, cf.CD, SSD_CONV_W
    tb = _tile(T, 512, SUBLANES)
    tc = _tile(CD, 512, LANES)
    off = cf.DI // tc
    nh = tb // SUBLANES

    def body(x_ref, halo_ref, w_ref, b_ref, da_ref, dp_ref, dw_ref, db_ref):
        x = x_ref[...]
        pre, ext = _conv_pre(x, halo_ref[...], w_ref[...], b_ref[...], pl.program_id(1) == 0, W)
        sg = _sigmoid(pre)
        dp = da_ref[...] * (sg * (1.0 + pre * (1.0 - sg)))
        dp_ref[...] = dp
        rows = [jnp.sum(dp * pltpu.roll(ext, W - 1 - tap, 0)[SUBLANES:SUBLANES + tb, :], axis=0, keepdims=True)
                for tap in range(W - 1)]
        rows.append(jnp.sum(dp * x, axis=0, keepdims=True))
        dw = jnp.concatenate(rows, axis=0)
        db = jnp.sum(dp, axis=0, keepdims=True)

        @pl.when(pl.program_id(1) == 0)
        def _():
            dw_ref[...] = dw
            db_ref[...] = db

        @pl.when(pl.program_id(1) > 0)
        def _():
            dw_ref[...] += dw
            db_ref[...] += db

    return _pcall(
        body, name="conv_bwd_pre", grid=(CD // tc, T // tb),
        in_specs=[pl.BlockSpec((tb, tc), lambda j, i: (i, off + j)),
                  pl.BlockSpec((SUBLANES, tc), lambda j, i: (jnp.maximum(i * nh - 1, 0), off + j)),
                  pl.BlockSpec((W, tc), lambda j, i: (0, j)), pl.BlockSpec((1, tc), lambda j, i: (0, j)),
                  pl.BlockSpec((tb, tc), lambda j, i: (i, j))],
        out_specs=(pl.BlockSpec((tb, tc), lambda j, i: (i, j)), pl.BlockSpec((W, tc), lambda j, i: (0, j)),
                   pl.BlockSpec((1, tc), lambda j, i: (0, j))),
        out_shape=(jax.ShapeDtypeStruct((T, CD), F32), jax.ShapeDtypeStruct((W, CD), F32),
                   jax.ShapeDtypeStruct((1, CD), F32)),
        compiler_params=_cp(("parallel", "arbitrary")),
    )(proj, proj, conv_w, conv_b, dact)


def _conv_bwd_x(cf, dpre, conv_w, into):
    T, CD, W = cf.T, cf.CD, SSD_CONV_W
    tb = _tile(T, 512, SUBLANES)
    tc = _tile(CD, 512, LANES)
    off = cf.DI // tc
    nh = tb // SUBLANES
    last_blk = T // SUBLANES - 1
    ni = T // tb

    def body(d_ref, halo_ref, w_ref, into_ref, o_ref):
        d = d_ref[...]
        w = w_ref[...]
        nxt = jnp.where(pl.program_id(1) == ni - 1, 0.0, halo_ref[...])
        ext = jnp.concatenate([d, nxt], axis=0)
        n = tb + SUBLANES
        out = w[W - 1:W, :] * d
        for tap in range(W - 1):
            out = out + w[tap:tap + 1, :] * pltpu.roll(ext, n - (W - 1 - tap), 0)[:tb, :]
        o_ref[...] = out.astype(BF16)

    return _pcall(
        body, name="conv_bwd_x", grid=(CD // tc, ni),
        in_specs=[pl.BlockSpec((tb, tc), lambda j, i: (i, j)),
                  pl.BlockSpec((SUBLANES, tc), lambda j, i: (jnp.minimum((i + 1) * nh, last_blk), j)),
                  pl.BlockSpec((W, tc), lambda j, i: (0, j)), ANY],
        out_specs=pl.BlockSpec((tb, tc), lambda j, i: (i, off + j)),
        out_shape=jax.ShapeDtypeStruct(into.shape, BF16), input_output_aliases={3: 0},
        compiler_params=_cp(("parallel", "arbitrary")),
    )(dpre, dpre, conv_w, into)


def _ssd_masks(cf, g, tb):
    L, GW, P, RG = cf.L, cf.GW, cf.P, cf.RG
    assert L == P and 2 * L == LANES and RG % 2 == 0
    i32 = jnp.int32
    hrow = lax.broadcasted_iota(i32, (LANES, GW), 0)
    hcol = lax.broadcasted_iota(i32, (LANES, GW), 1) // P
    expand = (hrow == g * RG + hcol).astype(BF16)
    ti = lax.broadcasted_iota(i32, (LANES, LANES), 0)
    tj = lax.broadcasted_iota(i32, (LANES, LANES), 1)
    btril = jnp.logical_and(ti // L == tj // L, ti >= tj).astype(BF16)
    r0 = lax.broadcasted_iota(i32, (L, GW), 0)
    c0 = lax.broadcasted_iota(i32, (L, GW), 1) % L
    tile_eye = (r0 == c0).astype(F32)
    lower = r0 >= c0
    p0 = lax.broadcasted_iota(i32, (2 * L, LANES), 0) // L
    p1 = lax.broadcasted_iota(i32, (2 * L, LANES), 1) // P
    pair = (p0 == p1).astype(F32)
    return expand, btril, tile_eye, lower, pair


def _softplus(x):
    return jnp.maximum(x, 0.0) + jnp.log1p(jnp.exp(-jnp.abs(x)))


def _split3(x):
    hi = x.astype(BF16)
    r1 = x - hi.astype(F32)
    mid = r1.astype(BF16)
    return hi, mid, (r1 - mid.astype(F32)).astype(BF16)


def _chunk_sums(btril, x, dn):
    hi, mid, lo = _split3(x)
    outs = []
    for k in range(x.shape[0] // LANES):
        sl = slice(k * LANES, (k + 1) * LANES)
        outs.append((_dot(btril, lo[sl], dn) + _dot(btril, mid[sl], dn)) + _dot(btril, hi[sl], dn))
    return jnp.concatenate(outs, axis=0)


def _expand_heads(x, expand, dn):
    hi, mid, lo = _split3(x)
    return (_dot(lo, expand, dn) + _dot(mid, expand, dn)) + _dot(hi, expand, dn)


def _ssd_chunk(cf, mk, acum, dt, xs, bm, cm):
    _, _, tile_eye, lower, pair = mk
    rowv = jnp.sum(acum * tile_eye, axis=0, keepdims=True)
    lf = jnp.exp(jnp.where(lower, acum - rowv, -1e30))
    xdt = xs * dt
    bb, cb_ = bm.astype(BF16), cm.astype(BF16)
    bb2 = jnp.concatenate([bb, bb], axis=0)
    cb2 = _dot(cb_, bb2, NT)
    ms, bds = [], []
    for j in range(cf.RG // 2):
        ln = slice(j * LANES, (j + 1) * LANES)
        ms.append((cb2 * lf[:, ln]).astype(BF16))
        xp = xdt[:, ln]
        bds.append((jnp.concatenate([xp, xp], axis=0) * pair).astype(BF16))
    return lf, xdt, bb, cb_, bb2, cb2, ms, bds


def _ssd_specs(cf, tb, rev):
    G, GW, N = cf.G, cf.GW, cf.N
    ni = cf.T // tb
    ri = (lambda i: ni - 1 - i) if rev else (lambda i: i)
    z = pl.BlockSpec((tb, GW), lambda g, i: (ri(i), g))
    dt = pl.BlockSpec((tb, LANES), lambda g, i: (ri(i), (cf.DI + cf.CD) // LANES))
    xs = pl.BlockSpec((tb, GW), lambda g, i: (ri(i), g))
    bm = pl.BlockSpec((tb, N), lambda g, i: (ri(i), cf.DI // N + g))
    cm = pl.BlockSpec((tb, N), lambda g, i: (ri(i), cf.DI // N + G + g))
    vec = pl.BlockSpec((1, GW), lambda g, i: (0, g))
    st = pl.BlockSpec((tb // cf.L, None, N, GW), lambda g, i: (ri(i), g, 0, 0))
    return z, dt, xs, bm, cm, vec, st


def _ssd_fwd(cf, proj, xact, bias_e, alog_e, dskip_e, norm_w):
    T, L, G, GW, N = cf.T, cf.L, cf.G, cf.GW, cf.N
    tb = _tile(T, 512, L)
    nck = tb // L
    z_s, dt_s, xs_s, b_s, c_s, vec_s, st_s = _ssd_specs(cf, tb, False)

    def body(z_ref, dt_ref, xs_ref, b_ref, c_ref, bias_ref, alog_ref, dsk_ref, nw_ref, y_ref, yp_ref, st_ref,
             state, dt_s, ac_s):
        @pl.when(pl.program_id(1) == 0)
        def _():
            state[...] = jnp.zeros_like(state)

        mk = _ssd_masks(cf, pl.program_id(0), tb)
        a_e = -jnp.exp(alog_ref[...])
        dt_all = _softplus(_expand_heads(dt_ref[...], mk[0], NN) + bias_ref[...])
        dt_s[...] = dt_all
        ac_s[...] = _chunk_sums(mk[1], dt_all * a_e, NN)

        def chunk(c, carry):
            rows = pl.ds(pl.multiple_of(c * L, L), L)
            acum = ac_s[rows, :]
            lf, xdt, bb, cb_, _, _, ms, bds = _ssd_chunk(cf, mk, acum, dt_s[rows, :], xs_ref[rows, :],
                                                         b_ref[rows, :], c_ref[rows, :])
            st = state[...]
            stb = st.astype(BF16)
            st_ref[c] = stb
            ydiag = jnp.concatenate([_dot(m, bd) for m, bd in zip(ms, bds)], axis=1)
            al = acum[L - 1:L, :]
            state[...] = st * jnp.exp(al) + _dot(bb, (xdt * jnp.exp(al - acum)).astype(BF16), TN)
            yp_ref[rows, :] = ydiag + _dot(cb_, stb) * jnp.exp(acum)
            return carry

        lax.fori_loop(0, nck, chunk, 0, unroll=True)
        z = z_ref[...]
        yg = (yp_ref[...] + dsk_ref[...] * xs_ref[...]) * (z * _sigmoid(z))
        y_ref[...] = (yg * _rstd(yg) * nw_ref[...]).astype(BF16)

    return _pcall(
        body, name="ssd_fwd", grid=(G, T // tb),
        in_specs=[z_s, dt_s, xs_s, b_s, c_s, vec_s, vec_s, vec_s, vec_s],
        out_specs=(z_s, z_s, st_s),
        out_shape=(jax.ShapeDtypeStruct((T, cf.DI), BF16), jax.ShapeDtypeStruct((T, cf.DI), F32),
                   jax.ShapeDtypeStruct((cf.NC, G, N, GW), BF16)),
        scratch_shapes=[pltpu.VMEM((N, GW), F32), pltpu.VMEM((tb, GW), F32), pltpu.VMEM((tb, GW), F32)],
        compiler_params=_cp(("parallel", "arbitrary")),
    )(proj, proj, xact, xact, xact, bias_e, alog_e, dskip_e, norm_w)


def _ssd_bwd(cf, proj, xact, bias_e, alog_e, dskip_e, norm_w, ypre, states, dy, rider=None):
    T, L, G, GW, N, RG = cf.T, cf.L, cf.G, cf.GW, cf.N, cf.RG
    tb = _tile(T, 512, L)
    nck = tb // L
    ni = T // tb
    z_s, dt_s, xs_s, b_s, c_s, vec_s, st_s = _ssd_specs(cf, tb, True)
    bc_out = pl.BlockSpec((tb, N), lambda g, i: (ni - 1 - i, g))
    ddt_out = pl.BlockSpec((None, tb, LANES), lambda g, i: (g, ni - 1 - i, 0))

    def body(z_ref, dt_ref, xs_ref, b_ref, c_ref, bias_ref, alog_ref, dsk_ref, nw_ref, yp_ref, st_ref, dy_ref,
             dz_ref, dxs_ref, db_ref, dc_ref, ddt_ref, dnw_ref, ddsk_ref, dalog_ref, dbias_ref,
             dstate, dt_s, ac_s, sg_s, dys_s, dxdt_s, dac_s):
        @pl.when(pl.program_id(1) == 0)
        def _():
            dstate[...] = jnp.zeros_like(dstate)
            for r in (dnw_ref, ddsk_ref, dalog_ref, dbias_ref):
                r[...] = jnp.zeros_like(r)

        mk = _ssd_masks(cf, pl.program_id(0), tb)
        expand, btril, tile_eye, lower, pair = mk
        a_e = -jnp.exp(alog_ref[...])
        dsk, nw = dsk_ref[...], nw_ref[...]
        last_row = (lax.broadcasted_iota(jnp.int32, (L, 1), 0) == L - 1).astype(F32)
        raw = _expand_heads(dt_ref[...], expand, NN) + bias_ref[...]
        dt_all = _softplus(raw)
        dt_s[...] = dt_all
        sg_s[...] = _sigmoid(raw)
        ac_s[...] = _chunk_sums(btril, dt_all * a_e, NN)
        z = z_ref[...]
        sz = _sigmoid(z)
        silu = z * sz
        xs_all = xs_ref[...]
        yd = yp_ref[...] + dsk * xs_all
        yg = yd * silu
        rr = _rstd(yg)
        xh = yg * rr
        dout = dy_ref[...]
        dnw_ref[...] += jnp.sum(dout * xh, axis=0, keepdims=True)
        dxh = dout * nw
        dyg = rr * (dxh - xh * jnp.mean(dxh * xh, axis=-1, keepdims=True))
        dz_ref[...] = (dyg * yd * (sz * (1.0 + z * (1.0 - sz)))).astype(BF16)
        dys_all = dyg * silu
        dys_s[...] = dys_all
        ddsk_ref[...] += jnp.sum(dys_all * xs_all, axis=0, keepdims=True)

        def chunk(t, carry):
            c = nck - 1 - t
            rows = pl.ds(pl.multiple_of(c * L, L), L)
            acum = ac_s[rows, :]
            lf, xdt, bb, cb_, bb2, cb2, ms, bds = _ssd_chunk(cf, mk, acum, dt_s[rows, :], xs_ref[rows, :],
                                                             b_ref[rows, :], c_ref[rows, :])
            stb = st_ref[c]
            eac = jnp.exp(acum)
            al = acum[L - 1:L, :]
            eal = jnp.exp(al)
            dte = jnp.exp(al - acum)
            dys = dys_s[rows, :]
            dyb = dys.astype(BF16)
            dms, dxs_, dsegs = [], [], []
            dcb2 = None
            for j in range(RG // 2):
                ln = slice(j * LANES, (j + 1) * LANES)
                dyj = dyb[:, ln]
                dbd = _dot(ms[j], dyj, TN) * pair
                dxs_.append(dbd[:L, :] + dbd[L:, :])
                tj = _dot(dyj, bds[j], NT) * lf[:, ln]
                dcb2 = tj if dcb2 is None else dcb2 + tj
                dsegs.append(tj * cb2)
            dxdt = jnp.concatenate(dxs_, axis=1)
            dseg = jnp.concatenate(dsegs, axis=1)
            dcb2 = dcb2.astype(BF16)
            dcm = _dot(dcb2, bb2)
            dbm2 = _dot(dcb2, cb_, TN)
            dbm = dbm2[:L, :] + dbm2[L:, :]
            dacum = dseg - tile_eye * jnp.sum(dseg, axis=0, keepdims=True)
            dyo = (dys * eac).astype(BF16)
            dcm = dcm + _dot(dyo, stb, NT)
            dacum = dacum + dys * _dot(cb_, stb) * eac
            dst = dstate[...]
            dstb = dst.astype(BF16)
            xd = xdt * dte
            dbm = dbm + _dot(xd.astype(BF16), dstb, NT)
            dxd = _dot(bb, dstb)
            dal = jnp.sum(dst * stb.astype(F32), axis=0, keepdims=True) * eal
            dxdt = dxdt + dxd * dte
            tt = dxd * xd
            dacum = dacum - tt + last_row * (dal + jnp.sum(tt, axis=0, keepdims=True))
            dstate[...] = dst * eal + _dot(cb_, dyo, TN)
            dxdt_s[rows, :] = dxdt
            dac_s[rows, :] = dacum
            db_ref[rows, :] = dbm
            dc_ref[rows, :] = dcm
            return carry

        lax.fori_loop(0, nck, chunk, 0, unroll=True)
        dda = _chunk_sums(btril, dac_s[...], TN)
        dxdt_all = dxdt_s[...]
        dt_all = dt_s[...]
        dxs_ref[...] = dys_s[...] * dsk + dxdt_all * dt_all
        ddt = dxdt_all * xs_ref[...] + dda * a_e
        dalog_ref[...] += jnp.sum(dda * dt_all, axis=0, keepdims=True) * a_e
        draw = ddt * sg_s[...]
        dbias_ref[...] += jnp.sum(draw, axis=0, keepdims=True)
        ddt_ref[...] = _expand_heads(draw, expand, NT)

    GN = G * N
    return _host_call(
        body, name="ssd_bwd", grid=(G, ni),
        in_specs=[z_s, dt_s, xs_s, b_s, c_s, vec_s, vec_s, vec_s, vec_s, z_s, st_s, z_s],
        out_specs=(z_s, z_s, bc_out, bc_out, ddt_out, vec_s, vec_s, vec_s, vec_s),
        out_shape=(jax.ShapeDtypeStruct((T, cf.SINP), BF16), jax.ShapeDtypeStruct((T, cf.CD), F32),
                   jax.ShapeDtypeStruct((T, GN), F32), jax.ShapeDtypeStruct((T, GN), F32),
                   jax.ShapeDtypeStruct((G, T, LANES), F32)) + (jax.ShapeDtypeStruct((1, cf.DI), F32),) * 4,
        scratch=[pltpu.VMEM((N, GW), F32)] + [pltpu.VMEM((tb, GW), F32)] * 6,
        args=(proj, proj, xact, xact, xact, bias_e, alog_e, dskip_e, norm_w, ypre, states, dy),
        sem=("parallel", "arbitrary"), rider=rider)


def _sum_groups(parts, name, into, col_blk):
    G, T, W = parts.shape
    tb = _tile(T, 512, SUBLANES)

    def body(p_ref, into_ref, o_ref):
        acc = p_ref[0]
        for g in range(1, G):
            acc = acc + p_ref[g]
        o_ref[...] = acc.astype(BF16)

    return _pcall(body, name=name, grid=(T // tb,), in_specs=[pl.BlockSpec((G, tb, W), lambda i: (0, i, 0)), ANY],
                  out_specs=pl.BlockSpec((tb, W), lambda i: (i, col_blk)),
                  out_shape=jax.ShapeDtypeStruct(into.shape, BF16), input_output_aliases={1: 0},
                  compiler_params=_cp(("parallel",)))(parts, into)


def _fill_bc(dact, dbm, dcm, name):
    T, GN = dbm.shape
    tb = _tile(T, 512, SUBLANES)
    blk = (dact.shape[1] - 2 * GN) // (2 * GN)
    assert blk * 2 * GN == dact.shape[1] - 2 * GN

    def body(b_ref, c_ref, into_ref, o_ref):
        o_ref[:, :GN] = b_ref[...]
        o_ref[:, GN:] = c_ref[...]

    row = pl.BlockSpec((tb, GN), lambda i: (i, 0))
    return _pcall(body, name=name, grid=(T // tb,), in_specs=[row, row, ANY],
                  out_specs=pl.BlockSpec((tb, 2 * GN), lambda i: (i, blk)),
                  out_shape=jax.ShapeDtypeStruct(dact.shape, F32), input_output_aliases={2: 0},
                  compiler_params=_cp(("parallel",)))(dbm, dcm, dact)


def _all_gather(x, name):
    def body(x_ref, out_ref, send_sems, recv_sems, local_sem):
        x, y, c = lax.axis_index("x"), lax.axis_index("y"), lax.axis_index("c")
        me, sibling = (x, y, c), (x, y, 1 - c)
        chips = [(1 - x, y), (x, 1 - y), (1 - x, 1 - y)]

        def blk(px, py, pc):
            return out_ref.at[4 * px + 2 * py + pc]

        def copy(k, block, to, src=None):
            return pltpu.make_async_remote_copy(
                src_ref=blk(*block) if src is None else src, dst_ref=blk(*block),
                send_sem=send_sems.at[k], recv_sem=recv_sems.at[k], device_id=to, device_id_type=MESH)

        mine = pltpu.make_async_copy(x_ref, blk(*me), local_sem)
        mine.start()
        first = [copy(0, me, sibling, src=x_ref)]
        first += [copy(1 + j, me, (*chip, c), src=x_ref) for j, chip in enumerate(chips)]
        for cp in first:
            cp.start()
        passed = [copy(4 + j, (*chip, c), sibling) for j, chip in enumerate(chips)]
        for j, chip in enumerate(chips):
            copy(1 + j, (*chip, c), me).wait_recv()
            passed[j].start()
        copy(0, sibling, me).wait_recv()
        for j, chip in enumerate(chips):
            copy(4 + j, (*chip, 1 - c), me).wait_recv()
        for cp in first + passed:
            cp.wait_send()
        mine.wait()

    return _pcall(
        body, name=name, in_specs=[ANY], out_specs=ANY,
        out_shape=jax.ShapeDtypeStruct((N_DEV,) + x.shape, x.dtype),
        scratch_shapes=[pltpu.SemaphoreType.DMA((7,)), pltpu.SemaphoreType.DMA((7,)), pltpu.SemaphoreType.DMA],
    )(x)


def _core_sum(g, recv, idx, name):
    _, _, a, b = g.shape
    tr = _tile(a, max(SUBLANES, (2 * 2**20) // (4 * b) // SUBLANES * SUBLANES), SUBLANES)

    def body(idx_ref, g_ref, r_ref, p_ref, own_ref):
        s = g_ref[...] + r_ref[...]
        p_ref[...] = s.astype(BF16)

        @pl.when(pl.program_id(1) == idx_ref[1])
        def _():
            own_ref[...] = s

    return _pcall(
        body, name=name,
        grid_spec=pltpu.PrefetchScalarGridSpec(
            num_scalar_prefetch=1, grid=(a // tr, 4),
            in_specs=[pl.BlockSpec((None, None, tr, b), lambda i, q, idx: (q, idx[0], i, 0)),
                      pl.BlockSpec((None, tr, b), lambda i, q, idx: (q, i, 0))],
            out_specs=(pl.BlockSpec((None, tr, b), lambda i, q, idx: (q, i, 0)),
                       pl.BlockSpec((tr, b), lambda i, q, idx: (i, 0)))),
        out_shape=(jax.ShapeDtypeStruct((4, a, b), BF16), jax.ShapeDtypeStruct((a, b), F32)),
        compiler_params=_cp(("parallel", "arbitrary")),
    )(idx, g, recv)


def _adam_math(w, g, m, v):
    m = ADAM_B1 * m + (1.0 - ADAM_B1) * g
    v = ADAM_B2 * v + (1.0 - ADAM_B2) * (g * g)
    m_hat = m / (1.0 - ADAM_B1 ** ADAM_STEP)
    v_hat = v / (1.0 - ADAM_B2 ** ADAM_STEP)
    delta = -ADAM_LR * (m_hat / (jnp.sqrt(v_hat) + ADAM_EPS) + ADAM_WD * w)
    return delta, m, v


def _chip_sum_adam(own, recv, w, m, v, layer, name, into=None, row_part=(0, 1)):
    a, b = own.shape
    n = w.shape[0]
    assert w.shape[1] == a * row_part[1]
    tr = _tile(a, max(SUBLANES, (2**20) // (4 * b) // SUBLANES * SUBLANES), SUBLANES)
    r_off = row_part[0] * (a // tr)
    wspec = pl.BlockSpec((None, tr, b), lambda i: (layer, i + r_off, 0))
    ospec = pl.BlockSpec((tr, b), lambda i: (i, 0))
    n_into = 0 if into is None else 4

    def body(own_ref, r_ref, w_ref, m_ref, v_ref, *rest):
        g_ref, d_ref, mo_ref, vo_ref = rest[n_into:]
        g = own_ref[...]
        for k in range(3):
            g = g + r_ref[k].astype(F32)
        g_ref[...] = g
        d_ref[...], mo_ref[...], vo_ref[...] = _adam_math(w_ref[...], g, m_ref[...], v_ref[...])

    return _pcall(
        body, name=name, grid=(a // tr,),
        in_specs=[ospec, pl.BlockSpec((3, tr, b), lambda i: (0, i, 0)), wspec, wspec, wspec] + [ANY] * n_into,
        out_specs=(wspec,) * 4, out_shape=(jax.ShapeDtypeStruct(w.shape, F32),) * 4,
        input_output_aliases={5 + k: k for k in range(n_into)},
        compiler_params=_cp(("parallel",)),
    )(own, recv, w, m, v, *(into or ()))


def _all_reduce_small(x, n_fold, fold_w, name):
    R, W = x.shape

    def body(x_ref, out_ref, buf, send_sems, recv_sems):
        xx, y, c = lax.axis_index("x"), lax.axis_index("y"), lax.axis_index("c")
        me = 4 * xx + 2 * y + c
        buf[me] = x_ref[...]
        copies = []
        for k in range(1, N_DEV):
            px, py, pc = xx ^ (k >> 2), y ^ ((k >> 1) & 1), c ^ (k & 1)
            copies.append(pltpu.make_async_remote_copy(
                src_ref=x_ref, dst_ref=buf.at[me], send_sem=send_sems.at[k - 1], recv_sem=recv_sems.at[k - 1],
                device_id=(px, py, pc), device_id_type=MESH))
        for cp in copies:
            cp.start()
        for cp in copies:
            cp.wait()
        acc = buf[0]
        for j in range(1, N_DEV):
            acc = acc + buf[j]
        out_ref[...] = acc
        if n_fold:
            l0 = lax.broadcasted_iota(jnp.int32, (W, W), 0) // fold_w
            l1 = lax.broadcasted_iota(jnp.int32, (W, W), 1)
            fold = (l0 == l1).astype(F32)
            out_ref[R - n_fold:, :] = _dot(acc[R - n_fold:, :], fold, NN, HI)

    return _pcall(
        body, name=name, in_specs=[pl.BlockSpec(memory_space=pltpu.VMEM)],
        out_specs=pl.BlockSpec(memory_space=pltpu.VMEM), out_shape=jax.ShapeDtypeStruct((R, W), F32),
        scratch_shapes=[pltpu.VMEM((N_DEV, R, W), F32), pltpu.SemaphoreType.DMA((N_DEV - 1,)),
                        pltpu.SemaphoreType.DMA((N_DEV - 1,))],
        compiler_params=pltpu.CompilerParams(vmem_limit_bytes=VMEM_LIMIT),
    )(x)


def _adam_small(w, g, m, v, name):
    def body(w_ref, g_ref, m_ref, v_ref, d_ref, mo_ref, vo_ref):
        d_ref[...], mo_ref[...], vo_ref[...] = _adam_math(w_ref[...], g_ref[...], m_ref[...], v_ref[...])

    return _pcall(body, name=name, out_shape=(jax.ShapeDtypeStruct(w.shape, F32),) * 3)(w, g, m, v)


def kernel(x, norm_mix_pre, norm_mix_post, norm_ffn_pre, norm_ffn_post, ret_w_in, ret_gn_w, ret_w_out, ssd_w_in, ssd_conv_w, ssd_conv_b, ssd_dt_bias, ssd_a_log, ssd_d, ssd_norm_w, ssd_w_out, mlp_w_up, mlp_w_down, loss_target, m_norm_mix_pre, m_norm_mix_post, m_norm_ffn_pre, m_norm_ffn_post, m_ret_w_in, m_ret_gn_w, m_ret_w_out, m_ssd_w_in, m_ssd_conv_w, m_ssd_conv_b, m_ssd_dt_bias, m_ssd_a_log, m_ssd_d, m_ssd_norm_w, m_ssd_w_out, m_mlp_w_up, m_mlp_w_down, v_norm_mix_pre, v_norm_mix_post, v_norm_ffn_pre, v_norm_ffn_post, v_ret_w_in, v_ret_gn_w, v_ret_w_out, v_ssd_w_in, v_ssd_conv_w, v_ssd_conv_b, v_ssd_dt_bias, v_ssd_a_log, v_ssd_d, v_ssd_norm_w, v_ssd_w_out, v_mlp_w_up, v_mlp_w_down):
    cf = _cfg()
    T, D = cf.T, cf.D
    ax, ay, ac = lax.axis_index("x"), lax.axis_index("y"), lax.axis_index("c")
    my_dev = 4 * ax + 2 * ay + ac
    idx = jnp.stack([ac, 2 * ax + ay]).astype(jnp.int32)

    w_ri = _all_gather(ret_w_in[0].T.astype(BF16), "ag_ret_w_in").reshape(cf.RIN, D)
    chips_of = lambda ws: _rider_join([_r_gather_chips(w.astype(BF16)) for w in ws])
    cores_of = lambda bufs: _rider_join([_r_gather_cores(b) for b in bufs])
    cw, cbw, nww = cf.CD // N_DEV, cf.CD // N_DEV, cf.DI // N_DEV
    small = jnp.concatenate([ssd_conv_w[0], ssd_conv_b, jnp.pad(ssd_norm_w, ((0, 0), (0, cw - nww))),
                             jnp.zeros((2, cw), F32)], axis=0)
    small = _all_gather(small, "ag_ssd_small")
    conv_w = jnp.transpose(small[:, :SSD_CONV_W, :], (1, 0, 2)).reshape(SSD_CONV_W, cf.CD)
    conv_b = small[:, SSD_CONV_W, :].reshape(1, cf.CD)
    ssd_nw = small[:, SSD_CONV_W + 1, :nww].reshape(1, cf.DI)

    half = cf.DK // 2
    inv_freq = ROPE_BASE ** (-jnp.arange(half, dtype=F32) / half)
    ang = jnp.arange(T).astype(F32)[:, None] * inv_freq[None, :]
    cos, sin = jnp.cos(ang), jnp.sin(ang)
    lgam = jnp.log1p(-jnp.exp2(-5.0 - jnp.arange(cf.H, dtype=F32)))
    rep = lambda p: jnp.repeat(p.reshape(1, cf.SH), cf.P, axis=1)
    bias_e, alog_e, dskip_e = rep(ssd_dt_bias), rep(ssd_a_log), rep(ssd_d)

    h0 = x.reshape(T, D)
    tgt = loss_target.reshape(T, D)
    nrm = lambda p, i: p[i:i + 1]

    u0 = _rms_fwd(h0, nrm(norm_mix_pre, 0), "rms_fwd0")
    proj0, part_a = _mm(u0, w_ri, kind="nt", name="mm_ret_in",
                        rider=chips_of([ret_w_out[0], mlp_w_up[0].T, mlp_w_down[0]]))
    (y0, o0, st0), got = _ret_fwd(cf, proj0, cos, sin, ret_gn_w, lgam, rider=_rider_join(
        [cores_of(part_a), chips_of([ssd_w_in[0]])]))
    w_ro, w_up0, w_dn0 = got[0].reshape(cf.V, D), got[1].reshape(cf.FF, D), got[2].reshape(cf.FF, D)
    part_b = got[3:]
    m0, part_c = _mm(y0, w_ro, kind="nn", name="mm_ret_out", tr=cf.V, rider=chips_of([ssd_w_out[0]]))
    h1, u1 = _resid_fwd(h0, m0, nrm(norm_mix_post, 0), nrm(norm_ffn_pre, 0), "resid_fwd0")
    (sq0, act0), part_d = _mm(u1, w_up0, kind="nt", name="mm_up0", epi="relu2", rider=chips_of([mlp_w_up[1].T]))
    f0, got = _mm(sq0, w_dn0, kind="nn", name="mm_down0", tj=1024, rider=_rider_join(
        [cores_of(part_b + part_c + part_d), chips_of([mlp_w_down[1]])]))
    w_si = jnp.pad(jnp.transpose(got[0], (1, 0, 2)).reshape(D, cf.SIN), ((0, 0), (0, cf.SINP - cf.SIN)))
    w_so, w_up1 = got[1].reshape(cf.DI, D), got[2].reshape(cf.FF, D)
    h2, u2 = _resid_fwd(h1, f0, nrm(norm_ffn_post, 0), nrm(norm_mix_pre, 1), "resid_fwd1")
    proj1, (w_dn1,) = _mm(u2, w_si, kind="nn", name="mm_ssd_in", tj=1152, rider=cores_of(got[3:]))
    w_up, w_dn = [w_up0, w_up1], [w_dn0, w_dn1.reshape(cf.FF, D)]
    xact = _conv_fwd(cf, proj1, conv_w, conv_b)
    y1, yp1, st1 = _ssd_fwd(cf, proj1, xact, bias_e, alog_e, dskip_e, ssd_nw)
    m1 = _mm(y1, w_so, kind="nn", name="mm_ssd_out", tr=cf.DI)
    h3, u3 = _resid_fwd(h2, m1, nrm(norm_mix_post, 1), nrm(norm_ffn_pre, 1), "resid_fwd2")
    sq1, act1 = _mm(u3, w_up[1], kind="nt", name="mm_up1", epi="relu2")
    f1 = _mm(sq1, w_dn[1], kind="nn", name="mm_down1", tj=1024)
    g4, lsum = _final_fwd(h3, f1, nrm(norm_ffn_post, 1), tgt, "final_fwd")
    loss = lax.psum(0.5 * jnp.sum(lsum) / D, ("x", "y", "c"))

    as4 = lambda g: g.reshape(4, 2, g.shape[1], g.shape[2])
    swap_cores = lambda g: _r_swap_cores(as4(g))
    core_sum = lambda g, recv, name: _core_sum(as4(g), recv, idx, name + "_core_sum")

    def mlp_bwd(l, df, u, sq, act, rider=None, then=None):
        dpre = _mm(df, w_dn[l], kind="nt", name=f"mm_dpre{l}", out_dtype=BF16, epi="mul2act", epi_in=act, rider=rider)
        dpre, got = dpre if rider is not None else (dpre, [])
        g_dn = _mm(sq, df, kind="tn", name=f"mm_gdown{l}", tj=1024, tr=4096, rider=then(got) if then else None)
        g_dn, got = g_dn if then else (g_dn, [])
        g_dn = g_dn.reshape(N_DEV, cf.FF // N_DEV, D)
        du, (rc,) = _mm(dpre, w_up[l], kind="nn", name=f"mm_du_mlp{l}", tj=1024, tr=4096, rider=swap_cores(g_dn))
        part, own = core_sum(g_dn, rc, f"rs_mlp_down{l}")
        g_up, (r2,) = _mm(u, dpre, kind="tn", name=f"mm_gup{l}", tj=1024, tr=4096, out_nblk=N_DEV, rider=_r_swap_chips(part))
        return du, g_up, own, r2, got

    df1, g_nfpost1 = _norm_bwd(g4, "norm_bwd4", m=f1, w_post=nrm(norm_ffn_post, 1))
    du3, g_up1, own_dn1, r2_dn1, _ = mlp_bwd(1, df1, u3, sq1, act1, None)
    gh3, dm1, g_nfpre1, g_nmpost1 = _norm_bwd(g4, "norm_bwd3", du=du3, h=h3, w_pre=nrm(norm_ffn_pre, 1),
                                              m=m1, w_post=nrm(norm_mix_post, 1))
    dy1, (rc,) = _mm(dm1, w_so, kind="nt", name="mm_dy_ssd", rider=swap_cores(g_up1))
    part, own_up1 = core_sum(g_up1, rc, "rs_mlp_up1")
    g_so = _mm(y1, dm1, kind="tn", name="mm_g_ssd_out", tj=1024, tr=4096).reshape(N_DEV, cf.DI // N_DEV, D)
    (dz, dxs, dbm, dcm, ddt_parts, g_ssd_nw, g_dskip_e, g_alog_e, g_bias_e), (r2_up1, rc) = _ssd_bwd(
        cf, proj1, xact, bias_e, alog_e, dskip_e, ssd_nw, yp1, st1, dy1,
        rider=_rider_join([_r_swap_chips(part), swap_cores(g_so)]))
    part, own_so = core_sum(g_so, rc, "rs_ssd_w_out")
    dact = _fill_bc(dxs, dbm, dcm, "ssd_dact_fill")
    dpre1, g_conv_w, g_conv_b = _conv_bwd_pre(cf, proj1, conv_w, conv_b, dact)
    assert cf.SINP == cf.DI + cf.CD + LANES
    dproj1 = _conv_bwd_x(cf, dpre1, conv_w, dz)
    dproj1 = _sum_groups(ddt_parts, "ssd_ddt_sum", dproj1, (cf.DI + cf.CD) // LANES)
    du2, (r2_so,) = _mm(dproj1, w_si, kind="nt", name="mm_du_ssd", tj=1024, tr=3456, rider=_r_swap_chips(part))
    g_si = _mm(u2, dproj1, kind="tn", name="mm_g_ssd_in", ti=512, tj=1152, tr=4096)
    g_si = jnp.transpose(g_si[:, :cf.SIN].reshape(D, N_DEV, cf.SIN // N_DEV), (1, 0, 2))
    gh2, df0, g_nmpre1, g_nfpost0 = _norm_bwd(gh3, "norm_bwd2", du=du2, h=h2, w_pre=nrm(norm_mix_pre, 1),
                                              m=f0, w_post=nrm(norm_ffn_post, 0))
    own_si = []

    def si_chips(got):
        part, own = core_sum(g_si, got[0], "rs_ssd_w_in")
        own_si.append(own)
        return _r_swap_chips(part)

    du1, g_up0, own_dn0, r2_dn0, (r2_si,) = mlp_bwd(0, df0, u1, sq0, act0, swap_cores(g_si), si_chips)
    own_si = own_si[0]
    gh1, dm0, g_nfpre0, g_nmpost0 = _norm_bwd(gh2, "norm_bwd1", du=du1, h=h1, w_pre=nrm(norm_ffn_pre, 0),
                                              m=m0, w_post=nrm(norm_mix_post, 0))
    dy0, (rc,) = _mm(dm0, w_ro, kind="nt", name="mm_dy_ret", rider=swap_cores(g_up0))
    part, own_up0 = core_sum(g_up0, rc, "rs_mlp_up0")
    g_ro = _mm(y0, dm0, kind="tn", name="mm_g_ret_out", tj=1024, tr=4096).reshape(N_DEV, cf.V // N_DEV, D)
    (dproj0, g_gn), (r2_up0, rc) = _ret_bwd(cf, proj0, cos, sin, ret_gn_w, lgam, o0, st0, dy0,
                                            rider=_rider_join([_r_swap_chips(part), swap_cores(g_ro)]))
    part, own_ro = core_sum(g_ro, rc, "rs_ret_w_out")
    rin8 = cf.RIN // N_DEV
    g_ri0, (r2_ro,) = _mm(u0, dproj0, kind="tn", name="mm_g_ret_in0", ti=512, tj=rin8, tr=4096, out_nblk=N_DEV,
                          i_part=(0, 2), rider=_r_swap_chips(part))
    g_ri1, (rc,) = _mm(u0, dproj0, kind="tn", name="mm_g_ret_in1", ti=512, tj=rin8, tr=4096, out_nblk=N_DEV,
                       i_part=(1, 2), rider=swap_cores(g_ri0))
    part, own_ri0 = core_sum(g_ri0, rc, "rs_ret_w_in0")
    du0, (r2_ri0, rc) = _mm(dproj0, w_ri, kind="nn", name="mm_du_ret", tj=1024, tr=4096,
                            rider=_rider_join([_r_swap_chips(part), swap_cores(g_ri1)]))
    part, own_ri1 = core_sum(g_ri1, rc, "rs_ret_w_in1")
    (grad_x, g_nmpre0), (r2_ri1,) = _norm_bwd(gh1, "norm_bwd0", du=du0, h=h0, w_pre=nrm(norm_mix_pre, 0),
                                              rider=_r_swap_chips(part))

    g_nmpre = jnp.concatenate([g_nmpre0, g_nmpre1], axis=0)
    g_nmpost = jnp.concatenate([g_nmpost0, g_nmpost1], axis=0)
    g_nfpre = jnp.concatenate([g_nfpre0, g_nfpre1], axis=0)
    g_nfpost = jnp.concatenate([g_nfpost0, g_nfpost1], axis=0)
    segs = [g_nmpre, g_nmpost, g_nfpre, g_nfpost, g_gn, g_conv_w, g_conv_b, g_ssd_nw, g_bias_e, g_alog_e, g_dskip_e]
    flat = jnp.concatenate([s.reshape(-1, LANES) for s in segs], axis=0)
    n_fold = 3 * cf.DI // LANES
    red = _all_reduce_small(flat, n_fold, cf.P, "all_reduce_small")
    outs, r0 = [], 0
    for s in segs:
        nr = s.size // LANES
        outs.append(red[r0:r0 + nr])
        r0 += nr
    (g_nmpre, g_nmpost, g_nfpre, g_nfpost) = [o.reshape(DEPTH, D) for o in outs[:4]]
    g_gn = outs[4].reshape(1, cf.V)
    g_conv_w = lax.dynamic_slice_in_dim(outs[5].reshape(SSD_CONV_W, cf.CD), my_dev * cw, cw, axis=1)[None]
    g_conv_b = lax.dynamic_slice_in_dim(outs[6].reshape(1, cf.CD), my_dev * cbw, cbw, axis=1)
    g_ssd_nw = lax.dynamic_slice_in_dim(outs[7].reshape(1, cf.DI), my_dev * nww, nww, axis=1)
    per_row = LANES // cf.P
    g_bias, g_alog, g_dskip = [o[:, :per_row].reshape(1, cf.SH) for o in outs[8:]]

    def rs(own, recv, w, m, v, layer, name, into=None, row_part=(0, 1)):
        return _chip_sum_adam(own, recv, w, m, v, layer, name + "_adam", into, row_part)

    r_ri = rs(own_ri0, r2_ri0, ret_w_in, m_ret_w_in, v_ret_w_in, 0, "rs_ret_w_in0", None, (0, 2))
    r_ri = rs(own_ri1, r2_ri1, ret_w_in, m_ret_w_in, v_ret_w_in, 0, "rs_ret_w_in1", r_ri, (1, 2))
    r_ro = rs(own_ro, r2_ro, ret_w_out, m_ret_w_out, v_ret_w_out, 0, "rs_ret_w_out")
    r_si = rs(own_si, r2_si, ssd_w_in, m_ssd_w_in, v_ssd_w_in, 0, "rs_ssd_w_in")
    r_so = rs(own_so, r2_so, ssd_w_out, m_ssd_w_out, v_ssd_w_out, 0, "rs_ssd_w_out")
    r_up = rs(own_up1, r2_up1, mlp_w_up, m_mlp_w_up, v_mlp_w_up, 1, "rs_mlp_up1")
    r_up = rs(own_up0, r2_up0, mlp_w_up, m_mlp_w_up, v_mlp_w_up, 0, "rs_mlp_up0", r_up)
    r_dn = rs(own_dn1, r2_dn1, mlp_w_down, m_mlp_w_down, v_mlp_w_down, 1, "rs_mlp_down1")
    r_dn = rs(own_dn0, r2_dn0, mlp_w_down, m_mlp_w_down, v_mlp_w_down, 0, "rs_mlp_down0", r_dn)
    lead = list

    def small_adam(w, g, m, v, name):
        return [g] + list(_adam_small(w, g, m, v, name))

    results = {
        "norm_mix_pre": small_adam(norm_mix_pre, g_nmpre, m_norm_mix_pre, v_norm_mix_pre, "adam_nmpre"),
        "norm_mix_post": small_adam(norm_mix_post, g_nmpost, m_norm_mix_post, v_norm_mix_post, "adam_nmpost"),
        "norm_ffn_pre": small_adam(norm_ffn_pre, g_nfpre, m_norm_ffn_pre, v_norm_ffn_pre, "adam_nfpre"),
        "norm_ffn_post": small_adam(norm_ffn_post, g_nfpost, m_norm_ffn_post, v_norm_ffn_post, "adam_nfpost"),
        "ret_w_in": lead(r_ri),
        "ret_gn_w": small_adam(ret_gn_w, g_gn, m_ret_gn_w, v_ret_gn_w, "adam_gn"),
        "ret_w_out": lead(r_ro),
        "ssd_w_in": lead(r_si),
        "ssd_conv_w": small_adam(ssd_conv_w, g_conv_w, m_ssd_conv_w, v_ssd_conv_w, "adam_conv_w"),
        "ssd_conv_b": small_adam(ssd_conv_b, g_conv_b, m_ssd_conv_b, v_ssd_conv_b, "adam_conv_b"),
        "ssd_dt_bias": small_adam(ssd_dt_bias, g_bias, m_ssd_dt_bias, v_ssd_dt_bias, "adam_dt_bias"),
        "ssd_a_log": small_adam(ssd_a_log, g_alog, m_ssd_a_log, v_ssd_a_log, "adam_a_log"),
        "ssd_d": small_adam(ssd_d, g_dskip, m_ssd_d, v_ssd_d, "adam_d"),
        "ssd_norm_w": small_adam(ssd_norm_w, g_ssd_nw, m_ssd_norm_w, v_ssd_norm_w, "adam_ssd_nw"),
        "ssd_w_out": lead(r_so),
        "mlp_w_up": r_up,
        "mlp_w_down": r_dn,
    }
    names = list(results)
    out = [loss, grad_x.reshape(1, T, D)]
    for k in range(4):
        out += [results[n][k] for n in names]
    return tuple(out)
```

```python
import functools
import math
import types

import jax
import jax.numpy as jnp
from jax import lax
from jax.experimental import pallas as pl
from jax.experimental.pallas import tpu as pltpu

F32 = jnp.float32
BF16 = jnp.bfloat16
HI = lax.Precision.HIGHEST
NN = (((1,), (0,)), ((), ()))
NT = (((1,), (1,)), ((), ()))
TN = (((0,), (0,)), ((), ()))
MESH = pl.DeviceIdType.MESH

V7X_VMEM_BYTES = 64 * 2**20
VMEM_LIMIT = V7X_VMEM_BYTES - 8 * 2**20
LANES = 128
SUBLANES = 8
N_DEV = 8

D_MODEL = 2048
SEQ = 8192
DEPTH = 2
CHUNK = 64
RMS_EPS = 1e-6
RET_HEAD_DK = 256
ROPE_BASE = 10000.0
GN_EPS = 1e-5
SSD_HEADDIM = 64
SSD_HEADS_PER_GROUP = 8
SSD_STATE = 128
SSD_CONV_W = 4
ADAM_LR = 0.001
ADAM_B1 = 0.9
ADAM_B2 = 0.999
ADAM_EPS = 1e-08
ADAM_WD = 0.01
ADAM_STEP = 10


def _cfg():
    c = types.SimpleNamespace()
    c.D, c.T, c.L = D_MODEL, SEQ, CHUNK
    c.DK = RET_HEAD_DK
    c.H = c.D // c.DK
    c.QK = c.H * c.DK
    c.DV = 2 * c.DK
    c.V = c.H * c.DV
    c.RIN = 2 * c.QK + 2 * c.V
    c.DI = 2 * c.D
    c.P = SSD_HEADDIM
    c.SH = c.DI // c.P
    c.RG = SSD_HEADS_PER_GROUP
    c.G = c.SH // c.RG
    c.GW = c.RG * c.P
    c.N = SSD_STATE
    c.CD = c.DI + 2 * c.G * c.N
    c.SIN = c.DI + c.CD + c.SH
    c.SINP = -(-c.SIN // LANES) * LANES
    c.FF = 4 * c.D
    c.NC = c.T // c.L
    return c


def _pcall(body, **kw):
    return pl.pallas_call(body, **kw)


def _cp(sem=None):
    return pltpu.CompilerParams(dimension_semantics=sem, vmem_limit_bytes=VMEM_LIMIT)


def _tile(n, pref, mult):
    if n <= pref:
        return n
    t = (pref // mult) * mult
    while t >= mult:
        if n % t == 0:
            return t
        t -= mult
    return n


def _dot(a, b, dn=NN, prec=None):
    return lax.dot_general(a, b, dn, precision=prec, preferred_element_type=F32)


ANY = pl.BlockSpec(memory_space=pl.ANY)


def _mesh_pos():
    return lax.axis_index("x"), lax.axis_index("y"), lax.axis_index("c")


def _rider_join(riders):
    j = types.SimpleNamespace(args=[], out_shape=[], aliases={}, sems=[])
    parts = []
    for r in riders:
        a0, o0, s0 = len(j.args), len(j.out_shape), len(j.sems)
        parts.append((r, a0, o0, s0))
        j.aliases.update({a0 + k: o0 + v for k, v in r.aliases.items()})
        j.args += r.args
        j.out_shape += r.out_shape
        j.sems += r.sems

    def make(rins, routs, sems):
        sends, recvs, locs = [], [], []
        for r, a0, o0, s0 in parts:
            s, rc, lc = r.make(rins[a0:a0 + len(r.args)], routs[o0:o0 + len(r.out_shape)], sems[s0:s0 + len(r.sems)])
            sends += s
            recvs += rc
            locs += lc
        return sends, recvs, locs

    j.make = make
    return j


def _rider_start(rider, rins, routs, sems):
    sends, _, locs = rider.make(rins, routs, sems)
    for cp in locs + sends:
        cp.start()


def _rider_wait(rider, rins, routs, sems):
    sends, recvs, locs = rider.make(rins, routs, sems)
    for cp in recvs:
        cp.wait_recv()
    for cp in sends:
        cp.wait_send()
    for cp in locs:
        cp.wait()


def _host_call(body, *, name, grid, in_specs, out_specs, out_shape, scratch, args, sem, rider=None):
    in_specs, out_specs, out_shape, scratch, args = map(list, (in_specs, out_specs, out_shape, scratch, args))
    if rider is None:
        res = _pcall(body, name=name, grid=grid, in_specs=in_specs, out_specs=out_specs, out_shape=out_shape,
                     scratch_shapes=scratch, compiler_params=_cp(sem))(*args)
        return list(res), []
    n_in, n_out, n_scr = len(args), len(out_shape), len(scratch)
    n_ra, n_ro = len(rider.args), len(rider.out_shape)

    def full(*refs):
        ins, rins = refs[:n_in], refs[n_in:n_in + n_ra]
        p = n_in + n_ra
        outs, routs = refs[p:p + n_out], refs[p + n_out:p + n_out + n_ro]
        p += n_out + n_ro
        scr, sems = refs[p:p + n_scr], refs[p + n_scr:]
        first = functools.reduce(jnp.logical_and, [pl.program_id(k) == 0 for k in range(len(grid))])
        last = functools.reduce(jnp.logical_and, [pl.program_id(k) == grid[k] - 1 for k in range(len(grid))])

        @pl.when(first)
        def _():
            _rider_start(rider, rins, routs, sems)

        body(*ins, *outs, *scr)

        @pl.when(last)
        def _():
            _rider_wait(rider, rins, routs, sems)

    res = _pcall(full, name=name, grid=grid, in_specs=in_specs + [ANY] * n_ra, out_specs=out_specs + [ANY] * n_ro,
                 out_shape=out_shape + rider.out_shape, scratch_shapes=scratch + rider.sems,
                 input_output_aliases={n_in + k: n_out + v for k, v in rider.aliases.items()},
                 compiler_params=_cp(("arbitrary",) * len(grid)))(*args, *rider.args)
    return list(res[:n_out]), list(res[n_out:])


def _comm(rider, name):
    n_ra, n_ro = len(rider.args), len(rider.out_shape)

    def body(*refs):
        rins, routs, sems = refs[:n_ra], refs[n_ra:n_ra + n_ro], refs[n_ra + n_ro:]
        _rider_start(rider, rins, routs, sems)
        _rider_wait(rider, rins, routs, sems)

    return list(_pcall(body, name=name, in_specs=[ANY] * n_ra, out_specs=[ANY] * n_ro, out_shape=rider.out_shape,
                       scratch_shapes=rider.sems, input_output_aliases=dict(rider.aliases))(*rider.args))


def _remote(src, dst, send_sems, recv_sems, k, to):
    return pltpu.make_async_remote_copy(src_ref=src, dst_ref=dst, send_sem=send_sems.at[k], recv_sem=recv_sems.at[k],
                                        device_id=to, device_id_type=MESH)


def _r_gather_chips(x):
    def make(rins, routs, sems):
        (x_ref,), (out_ref,), (send_sems, recv_sems, local_sem) = rins, routs, sems
        x_, y_, c_ = _mesh_pos()
        me = 4 * x_ + 2 * y_ + c_
        peers = [(x_, y_, 1 - c_), (1 - x_, y_, c_), (x_, 1 - y_, c_), (1 - x_, 1 - y_, c_)]
        sends = [_remote(x_ref, out_ref.at[me], send_sems, recv_sems, k, to) for k, to in enumerate(peers)]
        recvs = [_remote(x_ref, out_ref.at[4 * px + 2 * py + pc], send_sems, recv_sems, k, (px, py, pc))
                 for k, (px, py, pc) in enumerate(peers)]
        return sends, recvs, [pltpu.make_async_copy(x_ref, out_ref.at[me], local_sem)]

    return types.SimpleNamespace(
        args=[x], out_shape=[jax.ShapeDtypeStruct((N_DEV,) + x.shape, x.dtype)], aliases={},
        sems=[pltpu.SemaphoreType.DMA((4,)), pltpu.SemaphoreType.DMA((4,)), pltpu.SemaphoreType.DMA], make=make)


def _r_gather_cores(buf):
    def make(rins, routs, sems):
        (out_ref,), (send_sems, recv_sems) = routs, sems
        x_, y_, c_ = _mesh_pos()
        chips = [(1 - x_, y_), (x_, 1 - y_), (1 - x_, 1 - y_)]
        sends = [_remote(out_ref.at[4 * cx + 2 * cy + c_], out_ref.at[4 * cx + 2 * cy + c_], send_sems, recv_sems, k,
                         (x_, y_, 1 - c_)) for k, (cx, cy) in enumerate(chips)]
        recvs = [_remote(out_ref.at[4 * cx + 2 * cy + 1 - c_], out_ref.at[4 * cx + 2 * cy + 1 - c_], send_sems,
                         recv_sems, k, (x_, y_, 1 - c_)) for k, (cx, cy) in enumerate(chips)]
        return sends, recvs, []

    return types.SimpleNamespace(
        args=[buf], out_shape=[jax.ShapeDtypeStruct(buf.shape, buf.dtype)], aliases={0: 0},
        sems=[pltpu.SemaphoreType.DMA((3,)), pltpu.SemaphoreType.DMA((3,))], make=make)


def _r_swap_cores(g):
    def make(rins, routs, sems):
        (g_ref,), (out_ref,), (send_sems, recv_sems) = rins, routs, sems
        x_, y_, c_ = _mesh_pos()
        cps = [_remote(g_ref.at[q, 1 - c_], out_ref.at[q], send_sems, recv_sems, q, (x_, y_, 1 - c_)) for q in range(4)]
        return cps, cps, []

    return types.SimpleNamespace(
        args=[g], out_shape=[jax.ShapeDtypeStruct((4,) + g.shape[2:], g.dtype)], aliases={},
        sems=[pltpu.SemaphoreType.DMA((4,)), pltpu.SemaphoreType.DMA((4,))], make=make)


def _r_swap_chips(p):
    def make(rins, routs, sems):
        (p_ref,), (out_ref,), (send_sems, recv_sems) = rins, routs, sems
        x_, y_, c_ = _mesh_pos()
        chips = [(1 - x_, y_), (x_, 1 - y_), (1 - x_, 1 - y_)]
        cps = [_remote(p_ref.at[2 * cx + cy], out_ref.at[k], send_sems, recv_sems, k, (cx, cy, c_))
               for k, (cx, cy) in enumerate(chips)]
        return cps, cps, []

    return types.SimpleNamespace(
        args=[p], out_shape=[jax.ShapeDtypeStruct((3,) + p.shape[1:], p.dtype)], aliases={},
        sems=[pltpu.SemaphoreType.DMA((3,)), pltpu.SemaphoreType.DMA((3,))], make=make)


def _sigmoid(x):
    return 0.5 * (jnp.tanh(0.5 * x) + 1.0)


def _mm(a, b, *, kind, name, out_dtype=F32, ti=1024, tj=512, tr=2048, epi=None, epi_in=None, out_nblk=1, rider=None,
        i_part=(0, 1)):
    b_blk = b.ndim == 3
    if kind == "tn":
        R, I = a.shape
        I //= i_part[1]
    else:
        I, R = a.shape
    if kind == "nn":
        J = b.shape[1] if not b_blk else b.shape[0] * b.shape[2]
        nb_inner = b.shape[2] if b_blk else J
        r_inner = R
    elif kind == "nt":
        J = b.shape[0] if not b_blk else b.shape[1]
        nb_inner = J
        r_inner = b.shape[2] if b_blk else R
    else:
        J = b.shape[1]
        nb_inner = J
        r_inner = R
    out_inner = J // out_nblk
    ti = _tile(I, ti, LANES if kind == "tn" else SUBLANES)
    tj = _tile(min(nb_inner, out_inner), tj, LANES)
    assert nb_inner % tj == 0 and out_inner % tj == 0 and J % tj == 0
    tr = _tile(r_inner, tr, LANES)
    assert R % tr == 0
    ni, nj, nr = I // ti, J // tj, R // tr
    dn = {"nn": NN, "nt": NT, "tn": TN}[kind]

    if kind == "tn":
        i_off = i_part[0] * ni
        a_spec = pl.BlockSpec((tr, ti), lambda i, j, r: (r, i + i_off))
    else:
        a_spec = pl.BlockSpec((ti, tr), lambda i, j, r: (i, r))
    if kind == "nn":
        if b_blk:
            per = nb_inner // tj
            b_spec = pl.BlockSpec((None, tr, tj), lambda i, j, r: (j // per, r, j % per))
        else:
            b_spec = pl.BlockSpec((tr, tj), lambda i, j, r: (r, j))
    elif kind == "nt":
        if b_blk:
            per = r_inner // tr
            b_spec = pl.BlockSpec((None, tj, tr), lambda i, j, r: (r // per, j, r % per))
        else:
            b_spec = pl.BlockSpec((tj, tr), lambda i, j, r: (j, r))
    else:
        b_spec = pl.BlockSpec((tr, tj), lambda i, j, r: (r, j))
    if out_nblk > 1:
        pero = out_inner // tj
        o_spec = pl.BlockSpec((None, ti, tj), lambda i, j, r: (j // pero, i, j % pero))
        o_shape = (out_nblk, I, out_inner)
    else:
        o_spec = pl.BlockSpec((ti, tj), lambda i, j, r: (i, j))
        o_shape = (I, J)
    in_specs = [a_spec, b_spec]
    args = [a, b]
    if epi == "mul2act":
        in_specs.append(pl.BlockSpec((ti, tj), lambda i, j, r: (i, j)))
        args.append(epi_in)
    if epi == "relu2":
        out_shape = (jax.ShapeDtypeStruct(o_shape, BF16), jax.ShapeDtypeStruct(o_shape, BF16))
        out_specs = (o_spec, o_spec)
    else:
        out_shape = jax.ShapeDtypeStruct(o_shape, out_dtype)
        out_specs = o_spec
    n_in = len(args)
    n_out = 2 if epi == "relu2" else 1

    def body(*refs):
        a_ref, b_ref = refs[0], refs[1]
        outs = refs[n_in:n_in + n_out]
        acc_ref = refs[n_in + n_out] if nr > 1 else None

        def finish(acc):
            if epi == "relu2":
                act = jnp.maximum(acc, 0.0)
                outs[0][...] = (act * act).astype(BF16)
                outs[1][...] = act.astype(BF16)
            elif epi == "mul2act":
                outs[0][...] = (acc * (2.0 * refs[2][...].astype(F32))).astype(out_dtype)
            else:
                outs[0][...] = acc.astype(out_dtype)

        part = _dot(a_ref[...], b_ref[...], dn)
        if nr == 1:
            finish(part)
        else:
            r = pl.program_id(2)

            @pl.when(r == 0)
            def _():
                acc_ref[...] = part

            @pl.when(r > 0)
            def _():
                acc_ref[...] += part

            @pl.when(r == nr - 1)
            def _():
                finish(acc_ref[...])

    res, rider_res = _host_call(
        body, name=name, grid=(ni, nj, nr), in_specs=in_specs,
        out_specs=out_specs if n_out > 1 else [out_specs], out_shape=out_shape if n_out > 1 else [out_shape],
        scratch=[pltpu.VMEM((ti, tj), F32)] if nr > 1 else [], args=args,
        sem=("parallel", "parallel", "arbitrary"), rider=rider)
    res = tuple(res) if n_out > 1 else res[0]
    return res if rider is None else (res, rider_res)


def _rstd(x):
    return lax.rsqrt(jnp.mean(x * x, axis=-1, keepdims=True) + RMS_EPS)


def _rms_bwd_rows(x, w, dy):
    r = _rstd(x)
    xh = x * r
    dxh = dy * w
    dx = r * (dxh - xh * jnp.mean(dxh * xh, axis=-1, keepdims=True))
    return dx, jnp.sum(dy * xh, axis=0, keepdims=True)


def _row_spec(tb, d):
    return pl.BlockSpec((tb, d), lambda i: (i, 0))


def _vec_spec(d):
    return pl.BlockSpec((1, d), lambda i: (0, 0))


def _rms_fwd(h, w, name):
    T, D = h.shape
    tb = _tile(T, 512, SUBLANES)

    def body(h_ref, w_ref, u_ref):
        x = h_ref[...]
        u_ref[...] = (x * _rstd(x) * w_ref[...]).astype(BF16)

    return _pcall(body, name=name, grid=(T // tb,), in_specs=[_row_spec(tb, D), _vec_spec(D)],
                  out_specs=_row_spec(tb, D), out_shape=jax.ShapeDtypeStruct((T, D), BF16),
                  compiler_params=_cp(("parallel",)))(h, w)


def _resid_fwd(h, m, w_post, w_next, name):
    T, D = h.shape
    tb = _tile(T, 256, SUBLANES)

    def body(h_ref, m_ref, wp_ref, wn_ref, ho_ref, u_ref):
        x = m_ref[...]
        hn = h_ref[...] + x * _rstd(x) * wp_ref[...]
        ho_ref[...] = hn
        u_ref[...] = (hn * _rstd(hn) * wn_ref[...]).astype(BF16)

    return _pcall(body, name=name, grid=(T // tb,),
                  in_specs=[_row_spec(tb, D), _row_spec(tb, D), _vec_spec(D), _vec_spec(D)],
                  out_specs=(_row_spec(tb, D), _row_spec(tb, D)),
                  out_shape=(jax.ShapeDtypeStruct((T, D), F32), jax.ShapeDtypeStruct((T, D), BF16)),
                  compiler_params=_cp(("parallel",)))(h, m, w_post, w_next)


def _final_fwd(h, m, w_post, tgt, name):
    T, D = h.shape
    tb = _tile(T, 256, SUBLANES)

    def body(h_ref, m_ref, wp_ref, t_ref, g_ref, l_ref):
        x = m_ref[...]
        e = h_ref[...] + x * _rstd(x) * wp_ref[...] - t_ref[...]
        g_ref[...] = e * (1.0 / D)
        s = jnp.sum(e * e, axis=0, keepdims=True)

        @pl.when(pl.program_id(0) == 0)
        def _():
            l_ref[...] = s

        @pl.when(pl.program_id(0) > 0)
        def _():
            l_ref[...] += s

    return _pcall(body, name=name, grid=(T // tb,),
                  in_specs=[_row_spec(tb, D), _row_spec(tb, D), _vec_spec(D), _row_spec(tb, D)],
                  out_specs=(_row_spec(tb, D), _vec_spec(D)),
                  out_shape=(jax.ShapeDtypeStruct((T, D), F32), jax.ShapeDtypeStruct((1, D), F32)),
                  compiler_params=_cp(("arbitrary",)))(h, m, w_post, tgt)


def _norm_bwd(g_out, name, du=None, h=None, w_pre=None, m=None, w_post=None, rider=None):
    T, D = g_out.shape
    tb = _tile(T, 256, SUBLANES)
    has_pre, has_post = du is not None, m is not None
    args, in_specs = [g_out], [_row_spec(tb, D)]
    if has_pre:
        args += [du, h, w_pre]
        in_specs += [_row_spec(tb, D), _row_spec(tb, D), _vec_spec(D)]
    if has_post:
        args += [m, w_post]
        in_specs += [_row_spec(tb, D), _vec_spec(D)]
    out_shape, out_specs = [], []
    if has_pre:
        out_shape.append(jax.ShapeDtypeStruct((T, D), F32))
        out_specs.append(_row_spec(tb, D))
    if has_post:
        out_shape.append(jax.ShapeDtypeStruct((T, D), BF16))
        out_specs.append(_row_spec(tb, D))
    n_w = int(has_pre) + int(has_post)
    out_shape += [jax.ShapeDtypeStruct((1, D), F32)] * n_w
    out_specs += [_vec_spec(D)] * n_w
    n_in = len(args)

    def body(*refs):
        ins, outs = list(refs[:n_in]), list(refs[n_in:])
        g = ins.pop(0)[...]
        sums = []
        if has_pre:
            du_ref, h_ref, w_ref = ins.pop(0), ins.pop(0), ins.pop(0)
            dx, s = _rms_bwd_rows(h_ref[...], w_ref[...], du_ref[...])
            g = g + dx
            outs.pop(0)[...] = g
            sums.append(s)
        if has_post:
            m_ref, w_ref = ins.pop(0), ins.pop(0)
            dx, s = _rms_bwd_rows(m_ref[...], w_ref[...], g)
            outs.pop(0)[...] = dx.astype(BF16)
            sums.append(s)
        first = pl.program_id(0) == 0
        for o_ref, s in zip(outs, sums):
            @pl.when(first)
            def _(o_ref=o_ref, s=s):
                o_ref[...] = s

            @pl.when(jnp.logical_not(first))
            def _(o_ref=o_ref, s=s):
                o_ref[...] += s

    res, rider_res = _host_call(body, name=name, grid=(T // tb,), in_specs=in_specs, out_specs=out_specs,
                                out_shape=out_shape, scratch=[], args=args, sem=("arbitrary",), rider=rider)
    return tuple(res) if rider is None else (tuple(res), rider_res)


def _ret_consts(lg, L):
    ii = lax.broadcasted_iota(jnp.int32, (L, L), 0).astype(F32)
    jj = lax.broadcasted_iota(jnp.int32, (L, L), 1).astype(F32)
    dmat = jnp.exp(jnp.abs(ii - jj) * lg)
    idx = lax.broadcasted_iota(jnp.int32, (L, 1), 0).astype(F32)
    xi = jnp.exp((idx + 1.0) * lg)
    zeta = jnp.exp((L - 1.0 - idx) * lg)
    cd = jnp.exp(jnp.full((1, 1), L, F32) * lg)
    return dmat, xi, zeta, cd


def _rot(t, cs, sn):
    half = t.shape[-1] // 2
    t1, t2 = t[:, :half], t[:, half:]
    return jnp.concatenate([t1 * cs - t2 * sn, t1 * sn + t2 * cs], axis=-1)


def _rot_bwd(d, cs, sn):
    half = d.shape[-1] // 2
    d1, d2 = d[:, :half], d[:, half:]
    return jnp.concatenate([d1 * cs + d2 * sn, d2 * cs - d1 * sn], axis=-1)


def _ret_specs(cf, tb, rev):
    H, DK, DV = cf.H, cf.DK, cf.DV
    ni = cf.T // tb
    ri = (lambda i: ni - 1 - i) if rev else (lambda i: i)
    q = pl.BlockSpec((tb, DK), lambda h, i: (ri(i), h))
    k = pl.BlockSpec((tb, DK), lambda h, i: (ri(i), H + h))
    v = pl.BlockSpec((tb, DV), lambda h, i: (ri(i), cf.QK * 2 // DV + h))
    g = pl.BlockSpec((tb, DV), lambda h, i: (ri(i), cf.QK * 2 // DV + H + h))
    cs = pl.BlockSpec((tb, DK // 2), lambda h, i: (ri(i), 0))
    gw = pl.BlockSpec((1, DV), lambda h, i: (0, h))
    row_v = pl.BlockSpec((tb, DV), lambda h, i: (ri(i), h))
    row_k = pl.BlockSpec((tb, DK), lambda h, i: (ri(i), h))
    st = pl.BlockSpec((tb // cf.L, None, DK, DV), lambda h, i: (ri(i), h, 0, 0))
    lgs = pl.BlockSpec(memory_space=pltpu.SMEM)
    return q, k, v, g, cs, gw, row_v, row_k, st, lgs


def _ret_fwd(cf, proj, cos, sin, gn_w, lgam, rider=None):
    T, L, H, DK, DV = cf.T, cf.L, cf.H, cf.DK, cf.DV
    tb = _tile(T, 512, L)
    nck = tb // L
    q_s, k_s, v_s, g_s, cs_s, gw_s, row_v, _, st_s, lg_s = _ret_specs(cf, tb, False)
    kscale = DK ** -0.5

    def body(lg_ref, q_ref, k_ref, v_ref, g_ref, cos_ref, sin_ref, gw_ref, y_ref, o_ref, st_ref, state):
        h = pl.program_id(0)

        @pl.when(pl.program_id(1) == 0)
        def _():
            state[...] = jnp.zeros_like(state)

        dmat, xi, zeta, cd = _ret_consts(lg_ref[h], L)
        gw = gw_ref[...]

        def chunk(c, carry):
            rows = pl.ds(pl.multiple_of(c * L, L), L)
            cs, sn = cos_ref[rows, :], sin_ref[rows, :]
            qr = _rot(q_ref[rows, :], cs, sn)
            kr = _rot(k_ref[rows, :], cs, sn) * kscale
            qb, kb = qr.astype(BF16), kr.astype(BF16)
            vb = v_ref[rows, :].astype(BF16)
            st = state[...]
            stb = st.astype(BF16)
            st_ref[c] = stb
            s = _dot(qb, kb, NT) * dmat
            o = _dot(s.astype(BF16), vb) + _dot(qb, stb) * xi
            state[...] = st * cd + _dot((kr * zeta).astype(BF16), vb, TN)
            o_ref[rows, :] = o
            mu = jnp.mean(o, axis=-1, keepdims=True)
            oc = o - mu
            var = jnp.mean(oc * oc, axis=-1, keepdims=True)
            n = oc * lax.rsqrt(var + GN_EPS) * gw
            gt = g_ref[rows, :]
            y_ref[rows, :] = (gt * _sigmoid(gt) * n).astype(BF16)
            return carry

        lax.fori_loop(0, nck, chunk, 0, unroll=True)

    return _host_call(
        body, name="ret_fwd", grid=(H, T // tb),
        in_specs=[lg_s, q_s, k_s, v_s, g_s, cs_s, cs_s, gw_s],
        out_specs=(row_v, row_v, st_s),
        out_shape=(jax.ShapeDtypeStruct((T, cf.V), BF16), jax.ShapeDtypeStruct((T, cf.V), F32),
                   jax.ShapeDtypeStruct((cf.NC, H, DK, DV), BF16)),
        scratch=[pltpu.VMEM((DK, DV), F32)], args=(lgam, proj, proj, proj, proj, cos, sin, gn_w),
        sem=("parallel", "arbitrary"), rider=rider)


def _ret_bwd(cf, proj, cos, sin, gn_w, lgam, o, states, dy, rider=None):
    T, L, H, DK, DV = cf.T, cf.L, cf.H, cf.DK, cf.DV
    tb = _tile(T, 512, L)
    nck = tb // L
    q_s, k_s, v_s, g_s, cs_s, gw_s, row_v, row_k, st_s, lg_s = _ret_specs(cf, tb, True)
    kscale = DK ** -0.5

    ni = T // tb
    n_steps = H * ni
    col0 = (0, cf.QK, 2 * cf.QK, 2 * cf.QK + cf.V)
    widths = (DK, DK, DV, DV)

    def body(lg_ref, q_ref, k_ref, v_ref, g_ref, cos_ref, sin_ref, gw_ref, o_ref, st_ref, dy_ref,
             dp_ref, dgw_ref, dstate, dq_s, dk_s, dv_s, dg_s, sems):
        h = pl.program_id(0)
        step = h * ni + pl.program_id(1)
        slot = step % 2
        dq_ref, dk_ref, dv_ref, dg_ref = dq_s.at[slot], dk_s.at[slot], dv_s.at[slot], dg_s.at[slot]

        def results_out(s):
            hh, ii = s // ni, s % ni
            rows = pl.ds(pl.multiple_of((ni - 1 - ii) * tb, tb), tb)
            return [pltpu.make_async_copy(
                buf.at[s % 2], dp_ref.at[rows, pl.ds(pl.multiple_of(c0 + hh * w, LANES), w)], sems.at[s % 2, k])
                for k, (buf, c0, w) in enumerate(zip((dq_s, dk_s, dv_s, dg_s), col0, widths))]

        @pl.when(step >= 2)
        def _():
            for cp in results_out(step - 2):
                cp.wait()

        @pl.when(pl.program_id(1) == 0)
        def _():
            dstate[...] = jnp.zeros_like(dstate)
            dgw_ref[...] = jnp.zeros_like(dgw_ref)

        dmat, xi, zeta, cd = _ret_consts(lg_ref[h], L)
        gw = gw_ref[...]

        def chunk(t, carry):
            c = nck - 1 - t
            rows = pl.ds(pl.multiple_of(c * L, L), L)
            cs, sn = cos_ref[rows, :], sin_ref[rows, :]
            qr = _rot(q_ref[rows, :], cs, sn)
            kr = _rot(k_ref[rows, :], cs, sn) * kscale
            qb, kb = qr.astype(BF16), kr.astype(BF16)
            kzb = (kr * zeta).astype(BF16)
            vb = v_ref[rows, :].astype(BF16)
            s = (_dot(qb, kb, NT) * dmat).astype(BF16)
            oo = o_ref[rows, :]
            mu = jnp.mean(oo, axis=-1, keepdims=True)
            oc = oo - mu
            rstd = lax.rsqrt(jnp.mean(oc * oc, axis=-1, keepdims=True) + GN_EPS)
            oh = oc * rstd
            gt = g_ref[rows, :]
            sg = _sigmoid(gt)
            dyv = dy_ref[rows, :]
            dn = dyv * (gt * sg)
            dg_ref[rows, :] = (dyv * (oh * gw) * (sg * (1.0 + gt * (1.0 - sg)))).astype(BF16)
            dgw_ref[...] += jnp.sum(dn * oh, axis=0, keepdims=True)
            doh = dn * gw
            do = rstd * (doh - jnp.mean(doh, axis=-1, keepdims=True) - oh * jnp.mean(doh * oh, axis=-1, keepdims=True))
            dob = do.astype(BF16)
            doxb = (do * xi).astype(BF16)
            dst = dstate[...]
            dstb = dst.astype(BF16)
            stb = st_ref[c]
            dv_ref[rows, :] = (_dot(s, dob, TN) + _dot(kzb, dstb)).astype(BF16)
            ds = (_dot(dob, vb, NT) * dmat).astype(BF16)
            dqr = _dot(ds, kb) + _dot(doxb, stb, NT)
            dkr = _dot(ds, qb, TN) + _dot(vb, dstb, NT) * zeta
            dstate[...] = dst * cd + _dot(qb, doxb, TN)
            dq_ref[rows, :] = _rot_bwd(dqr, cs, sn).astype(BF16)
            dk_ref[rows, :] = _rot_bwd(dkr * kscale, cs, sn).astype(BF16)
            return carry

        lax.fori_loop(0, nck, chunk, 0, unroll=True)
        for cp in results_out(step):
            cp.start()

        @pl.when(step == n_steps - 1)
        def _():
            if n_steps > 1:
                for cp in results_out(step - 1):
                    cp.wait()
            for cp in results_out(step):
                cp.wait()

    return _host_call(
        body, name="ret_bwd", grid=(H, ni),
        in_specs=[lg_s, q_s, k_s, v_s, g_s, cs_s, cs_s, gw_s, row_v, st_s, row_v],
        out_specs=(ANY, gw_s),
        out_shape=(jax.ShapeDtypeStruct((T, cf.RIN), BF16), jax.ShapeDtypeStruct((1, cf.V), F32)),
        scratch=[pltpu.VMEM((DK, DV), F32), pltpu.VMEM((2, tb, DK), BF16), pltpu.VMEM((2, tb, DK), BF16),
                 pltpu.VMEM((2, tb, DV), BF16), pltpu.VMEM((2, tb, DV), BF16), pltpu.SemaphoreType.DMA((2, 4))],
        args=(lgam, proj, proj, proj, proj, cos, sin, gn_w, o, states, dy),
        sem=("arbitrary", "arbitrary"), rider=rider)


def _conv_pre(x, halo, w, b, first, W):
    tb = x.shape[0]
    ext = jnp.concatenate([jnp.where(first, 0.0, halo), x], axis=0)
    out = b + w[W - 1:W, :] * x
    for tap in range(W - 1):
        out = out + w[tap:tap + 1, :] * pltpu.roll(ext, W - 1 - tap, 0)[SUBLANES:SUBLANES + tb, :]
    return out, ext


def _conv_fwd(cf, proj, conv_w, conv_b):
    T, CD, W = cf.T, cf.CD, SSD_CONV_W
    tb = _tile(T, 512, SUBLANES)
    tc = _tile(CD, 512, LANES)
    off = cf.DI // tc
    nh = tb // SUBLANES

    def body(x_ref, halo_ref, w_ref, b_ref, o_ref):
        pre, _ = _conv_pre(x_ref[...], halo_ref[...], w_ref[...], b_ref[...], pl.program_id(1) == 0, W)
        o_ref[...] = pre * _sigmoid(pre)

    return _pcall(
        body, name="conv_fwd", grid=(CD // tc, T // tb),
        in_specs=[pl.BlockSpec((tb, tc), lambda j, i: (i, off + j)),
                  pl.BlockSpec((SUBLANES, tc), lambda j, i: (jnp.maximum(i * nh - 1, 0), off + j)),
                  pl.BlockSpec((W, tc), lambda j, i: (0, j)), pl.BlockSpec((1, tc), lambda j, i: (0, j))],
        out_specs=pl.BlockSpec((tb, tc), lambda j, i: (i, j)),
        out_shape=jax.ShapeDtypeStruct((T, CD), F32),
        compiler_params=_cp(("parallel", "arbitrary")),
    )(proj, proj, conv_w, conv_b)


def _conv_bwd_pre(cf, proj, conv_w, conv_b, dact):
    T, CD, W = cf.T, cf.CD, SSD_CONV_W
    tb = _tile(T, 512, SUBLANES)
    tc = _tile(CD, 512, LANES)
    off = cf.DI // tc
    nh = tb // SUBLANES

    def body(x_ref, halo_ref, w_ref, b_ref, da_ref, dp_ref, dw_ref, db_ref):
        x = x_ref[...]
        pre, ext = _conv_pre(x, halo_ref[...], w_ref[...], b_ref[...], pl.program_id(1) == 0, W)
        sg = _sigmoid(pre)
        dp = da_ref[...] * (sg * (1.0 + pre * (1.0 - sg)))
        dp_ref[...] = dp
        rows = [jnp.sum(dp * pltpu.roll(ext, W - 1 - tap, 0)[SUBLANES:SUBLANES + tb, :], axis=0, keepdims=True)
                for tap in range(W - 1)]
        rows.append(jnp.sum(dp * x, axis=0, keepdims=True))
        dw = jnp.concatenate(rows, axis=0)
        db = jnp.sum(dp, axis=0, keepdims=True)

        @pl.when(pl.program_id(1) == 0)
        def _():
            dw_ref[...] = dw
            db_ref[...] = db

        @pl.when(pl.program_id(1) > 0)
        def _():
            dw_ref[...] += dw
            db_ref[...] += db

    return _pcall(
        body, name="conv_bwd_pre", grid=(CD // tc, T // tb),
        in_specs=[pl.BlockSpec((tb, tc), lambda j, i: (i, off + j)),
                  pl.BlockSpec((SUBLANES, tc), lambda j, i: (jnp.maximum(i * nh - 1, 0), off + j)),
                  pl.BlockSpec((W, tc), lambda j, i: (0, j)), pl.BlockSpec((1, tc), lambda j, i: (0, j)),
                  pl.BlockSpec((tb, tc), lambda j, i: (i, j))],
        out_specs=(pl.BlockSpec((tb, tc), lambda j, i: (i, j)), pl.BlockSpec((W, tc), lambda j, i: (0, j)),
                   pl.BlockSpec((1, tc), lambda j, i: (0, j))),
        out_shape=(jax.ShapeDtypeStruct((T, CD), F32), jax.ShapeDtypeStruct((W, CD), F32),
                   jax.ShapeDtypeStruct((1, CD), F32)),
        compiler_params=_cp(("parallel", "arbitrary")),
    )(proj, proj, conv_w, conv_b, dact)


def _conv_bwd_x(cf, dpre, conv_w, into):
    T, CD, W = cf.T, cf.CD, SSD_CONV_W
    tb = _tile(T, 512, SUBLANES)
    tc = _tile(CD, 512, LANES)
    off = cf.DI // tc
    nh = tb // SUBLANES
    last_blk = T // SUBLANES - 1
    ni = T // tb

    def body(d_ref, halo_ref, w_ref, into_ref, o_ref):
        d = d_ref[...]
        w = w_ref[...]
        nxt = jnp.where(pl.program_id(1) == ni - 1, 0.0, halo_ref[...])
        ext = jnp.concatenate([d, nxt], axis=0)
        n = tb + SUBLANES
        out = w[W - 1:W, :] * d
        for tap in range(W - 1):
            out = out + w[tap:tap + 1, :] * pltpu.roll(ext, n - (W - 1 - tap), 0)[:tb, :]
        o_ref[...] = out.astype(BF16)

    return _pcall(
        body, name="conv_bwd_x", grid=(CD // tc, ni),
        in_specs=[pl.BlockSpec((tb, tc), lambda j, i: (i, j)),
                  pl.BlockSpec((SUBLANES, tc), lambda j, i: (jnp.minimum((i + 1) * nh, last_blk), j)),
                  pl.BlockSpec((W, tc), lambda j, i: (0, j)), ANY],
        out_specs=pl.BlockSpec((tb, tc), lambda j, i: (i, off + j)),
        out_shape=jax.ShapeDtypeStruct(into.shape, BF16), input_output_aliases={3: 0},
        compiler_params=_cp(("parallel", "arbitrary")),
    )(dpre, dpre, conv_w, into)


def _ssd_masks(cf, g, tb):
    L, GW, P, RG = cf.L, cf.GW, cf.P, cf.RG
    assert L == P and 2 * L == LANES and RG % 2 == 0
    i32 = jnp.int32
    hrow = lax.broadcasted_iota(i32, (LANES, GW), 0)
    hcol = lax.broadcasted_iota(i32, (LANES, GW), 1) // P
    expand = (hrow == g * RG + hcol).astype(BF16)
    ti = lax.broadcasted_iota(i32, (LANES, LANES), 0)
    tj = lax.broadcasted_iota(i32, (LANES, LANES), 1)
    btril = jnp.logical_and(ti // L == tj // L, ti >= tj).astype(BF16)
    r0 = lax.broadcasted_iota(i32, (L, GW), 0)
    c0 = lax.broadcasted_iota(i32, (L, GW), 1) % L
    tile_eye = (r0 == c0).astype(F32)
    lower = r0 >= c0
    p0 = lax.broadcasted_iota(i32, (2 * L, LANES), 0) // L
    p1 = lax.broadcasted_iota(i32, (2 * L, LANES), 1) // P
    pair = (p0 == p1).astype(F32)
    return expand, btril, tile_eye, lower, pair


def _softplus(x):
    return jnp.maximum(x, 0.0) + jnp.log1p(jnp.exp(-jnp.abs(x)))


def _split3(x):
    hi = x.astype(BF16)
    r1 = x - hi.astype(F32)
    mid = r1.astype(BF16)
    return hi, mid, (r1 - mid.astype(F32)).astype(BF16)


def _chunk_sums(btril, x, dn):
    hi, mid, lo = _split3(x)
    outs = []
    for k in range(x.shape[0] // LANES):
        sl = slice(k * LANES, (k + 1) * LANES)
        outs.append((_dot(btril, lo[sl], dn) + _dot(btril, mid[sl], dn)) + _dot(btril, hi[sl], dn))
    return jnp.concatenate(outs, axis=0)


def _expand_heads(x, expand, dn):
    hi, mid, lo = _split3(x)
    return (_dot(lo, expand, dn) + _dot(mid, expand, dn)) + _dot(hi, expand, dn)


def _ssd_chunk(cf, mk, acum, dt, xs, bm, cm):
    _, _, tile_eye, lower, pair = mk
    rowv = jnp.sum(acum * tile_eye, axis=0, keepdims=True)
    lf = jnp.exp(jnp.where(lower, acum - rowv, -1e30))
    xdt = xs * dt
    bb, cb_ = bm.astype(BF16), cm.astype(BF16)
    bb2 = jnp.concatenate([bb, bb], axis=0)
    cb2 = _dot(cb_, bb2, NT)
    ms, bds = [], []
    for j in range(cf.RG // 2):
        ln = slice(j * LANES, (j + 1) * LANES)
        ms.append((cb2 * lf[:, ln]).astype(BF16))
        xp = xdt[:, ln]
        bds.append((jnp.concatenate([xp, xp], axis=0) * pair).astype(BF16))
    return lf, xdt, bb, cb_, bb2, cb2, ms, bds


def _ssd_specs(cf, tb, rev):
    G, GW, N = cf.G, cf.GW, cf.N
    ni = cf.T // tb
    ri = (lambda i: ni - 1 - i) if rev else (lambda i: i)
    z = pl.BlockSpec((tb, GW), lambda g, i: (ri(i), g))
    dt = pl.BlockSpec((tb, LANES), lambda g, i: (ri(i), (cf.DI + cf.CD) // LANES))
    xs = pl.BlockSpec((tb, GW), lambda g, i: (ri(i), g))
    bm = pl.BlockSpec((tb, N), lambda g, i: (ri(i), cf.DI // N + g))
    cm = pl.BlockSpec((tb, N), lambda g, i: (ri(i), cf.DI // N + G + g))
    vec = pl.BlockSpec((1, GW), lambda g, i: (0, g))
    st = pl.BlockSpec((tb // cf.L, None, N, GW), lambda g, i: (ri(i), g, 0, 0))
    return z, dt, xs, bm, cm, vec, st


def _ssd_fwd(cf, proj, xact, bias_e, alog_e, dskip_e, norm_w):
    T, L, G, GW, N = cf.T, cf.L, cf.G, cf.GW, cf.N
    tb = _tile(T, 512, L)
    nck = tb // L
    z_s, dt_s, xs_s, b_s, c_s, vec_s, st_s = _ssd_specs(cf, tb, False)

    def body(z_ref, dt_ref, xs_ref, b_ref, c_ref, bias_ref, alog_ref, dsk_ref, nw_ref, y_ref, yp_ref, st_ref,
             state, dt_s, ac_s):
        @pl.when(pl.program_id(1) == 0)
        def _():
            state[...] = jnp.zeros_like(state)

        mk = _ssd_masks(cf, pl.program_id(0), tb)
        a_e = -jnp.exp(alog_ref[...])
        dt_all = _softplus(_expand_heads(dt_ref[...], mk[0], NN) + bias_ref[...])
        dt_s[...] = dt_all
        ac_s[...] = _chunk_sums(mk[1], dt_all * a_e, NN)

        def chunk(c, carry):
            rows = pl.ds(pl.multiple_of(c * L, L), L)
            acum = ac_s[rows, :]
            lf, xdt, bb, cb_, _, _, ms, bds = _ssd_chunk(cf, mk, acum, dt_s[rows, :], xs_ref[rows, :],
                                                         b_ref[rows, :], c_ref[rows, :])
            st = state[...]
            stb = st.astype(BF16)
            st_ref[c] = stb
            ydiag = jnp.concatenate([_dot(m, bd) for m, bd in zip(ms, bds)], axis=1)
            al = acum[L - 1:L, :]
            state[...] = st * jnp.exp(al) + _dot(bb, (xdt * jnp.exp(al - acum)).astype(BF16), TN)
            yp_ref[rows, :] = ydiag + _dot(cb_, stb) * jnp.exp(acum)
            return carry

        lax.fori_loop(0, nck, chunk, 0, unroll=True)
        z = z_ref[...]
        yg = (yp_ref[...] + dsk_ref[...] * xs_ref[...]) * (z * _sigmoid(z))
        y_ref[...] = (yg * _rstd(yg) * nw_ref[...]).astype(BF16)

    return _pcall(
        body, name="ssd_fwd", grid=(G, T // tb),
        in_specs=[z_s, dt_s, xs_s, b_s, c_s, vec_s, vec_s, vec_s, vec_s],
        out_specs=(z_s, z_s, st_s),
        out_shape=(jax.ShapeDtypeStruct((T, cf.DI), BF16), jax.ShapeDtypeStruct((T, cf.DI), F32),
                   jax.ShapeDtypeStruct((cf.NC, G, N, GW), BF16)),
        scratch_shapes=[pltpu.VMEM((N, GW), F32), pltpu.VMEM((tb, GW), F32), pltpu.VMEM((tb, GW), F32)],
        compiler_params=_cp(("parallel", "arbitrary")),
    )(proj, proj, xact, xact, xact, bias_e, alog_e, dskip_e, norm_w)


def _ssd_bwd(cf, proj, xact, bias_e, alog_e, dskip_e, norm_w, ypre, states, dy, rider=None):
    T, L, G, GW, N, RG = cf.T, cf.L, cf.G, cf.GW, cf.N, cf.RG
    tb = _tile(T, 512, L)
    nck = tb // L
    ni = T // tb
    z_s, dt_s, xs_s, b_s, c_s, vec_s, st_s = _ssd_specs(cf, tb, True)
    bc_out = pl.BlockSpec((tb, N), lambda g, i: (ni - 1 - i, g))
    ddt_out = pl.BlockSpec((None, tb, LANES), lambda g, i: (g, ni - 1 - i, 0))

    def body(z_ref, dt_ref, xs_ref, b_ref, c_ref, bias_ref, alog_ref, dsk_ref, nw_ref, yp_ref, st_ref, dy_ref,
             dz_ref, dxs_ref, db_ref, dc_ref, ddt_ref, dnw_ref, ddsk_ref, dalog_ref, dbias_ref,
             dstate, dt_s, ac_s, sg_s, dys_s, dxdt_s, dac_s):
        @pl.when(pl.program_id(1) == 0)
        def _():
            dstate[...] = jnp.zeros_like(dstate)
            for r in (dnw_ref, ddsk_ref, dalog_ref, dbias_ref):
                r[...] = jnp.zeros_like(r)

        mk = _ssd_masks(cf, pl.program_id(0), tb)
        expand, btril, tile_eye, lower, pair = mk
        a_e = -jnp.exp(alog_ref[...])
        dsk, nw = dsk_ref[...], nw_ref[...]
        last_row = (lax.broadcasted_iota(jnp.int32, (L, 1), 0) == L - 1).astype(F32)
        raw = _expand_heads(dt_ref[...], expand, NN) + bias_ref[...]
        dt_all = _softplus(raw)
        dt_s[...] = dt_all
        sg_s[...] = _sigmoid(raw)
        ac_s[...] = _chunk_sums(btril, dt_all * a_e, NN)
        z = z_ref[...]
        sz = _sigmoid(z)
        silu = z * sz
        xs_all = xs_ref[...]
        yd = yp_ref[...] + dsk * xs_all
        yg = yd * silu
        rr = _rstd(yg)
        xh = yg * rr
        dout = dy_ref[...]
        dnw_ref[...] += jnp.sum(dout * xh, axis=0, keepdims=True)
        dxh = dout * nw
        dyg = rr * (dxh - xh * jnp.mean(dxh * xh, axis=-1, keepdims=True))
        dz_ref[...] = (dyg * yd * (sz * (1.0 + z * (1.0 - sz)))).astype(BF16)
        dys_all = dyg * silu
        dys_s[...] = dys_all
        ddsk_ref[...] += jnp.sum(dys_all * xs_all, axis=0, keepdims=True)

        def chunk(t, carry):
            c = nck - 1 - t
            rows = pl.ds(pl.multiple_of(c * L, L), L)
            acum = ac_s[rows, :]
            lf, xdt, bb, cb_, bb2, cb2, ms, bds = _ssd_chunk(cf, mk, acum, dt_s[rows, :], xs_ref[rows, :],
                                                             b_ref[rows, :], c_ref[rows, :])
            stb = st_ref[c]
            eac = jnp.exp(acum)
            al = acum[L - 1:L, :]
            eal = jnp.exp(al)
            dte = jnp.exp(al - acum)
            dys = dys_s[rows, :]
            dyb = dys.astype(BF16)
            dms, dxs_, dsegs = [], [], []
            dcb2 = None
            for j in range(RG // 2):
                ln = slice(j * LANES, (j + 1) * LANES)
                dyj = dyb[:, ln]
                dbd = _dot(ms[j], dyj, TN) * pair
                dxs_.append(dbd[:L, :] + dbd[L:, :])
                tj = _dot(dyj, bds[j], NT) * lf[:, ln]
                dcb2 = tj if dcb2 is None else dcb2 + tj
                dsegs.append(tj * cb2)
            dxdt = jnp.concatenate(dxs_, axis=1)
            dseg = jnp.concatenate(dsegs, axis=1)
            dcb2 = dcb2.astype(BF16)
            dcm = _dot(dcb2, bb2)
            dbm2 = _dot(dcb2, cb_, TN)
            dbm = dbm2[:L, :] + dbm2[L:, :]
            dacum = dseg - tile_eye * jnp.sum(dseg, axis=0, keepdims=True)
            dyo = (dys * eac).astype(BF16)
            dcm = dcm + _dot(dyo, stb, NT)
            dacum = dacum + dys * _dot(cb_, stb) * eac
            dst = dstate[...]
            dstb = dst.astype(BF16)
            xd = xdt * dte
            dbm = dbm + _dot(xd.astype(BF16), dstb, NT)
            dxd = _dot(bb, dstb)
            dal = jnp.sum(dst * stb.astype(F32), axis=0, keepdims=True) * eal
            dxdt = dxdt + dxd * dte
            tt = dxd * xd
            dacum = dacum - tt + last_row * (dal + jnp.sum(tt, axis=0, keepdims=True))
            dstate[...] = dst * eal + _dot(cb_, dyo, TN)
            dxdt_s[rows, :] = dxdt
            dac_s[rows, :] = dacum
            db_ref[rows, :] = dbm
            dc_ref[rows, :] = dcm
            return carry

        lax.fori_loop(0, nck, chunk, 0, unroll=True)
        dda = _chunk_sums(btril, dac_s[...], TN)
        dxdt_all = dxdt_s[...]
        dt_all = dt_s[...]
        dxs_ref[...] = dys_s[...] * dsk + dxdt_all * dt_all
        ddt = dxdt_all * xs_ref[...] + dda * a_e
        dalog_ref[...] += jnp.sum(dda * dt_all, axis=0, keepdims=True) * a_e
        draw = ddt * sg_s[...]
        dbias_ref[...] += jnp.sum(draw, axis=0, keepdims=True)
        ddt_ref[...] = _expand_heads(draw, expand, NT)

    GN = G * N
    return _host_call(
        body, name="ssd_bwd", grid=(G, ni),
        in_specs=[z_s, dt_s, xs_s, b_s, c_s, vec_s, vec_s, vec_s, vec_s, z_s, st_s, z_s],
        out_specs=(z_s, z_s, bc_out, bc_out, ddt_out, vec_s, vec_s, vec_s, vec_s),
        out_shape=(jax.ShapeDtypeStruct((T, cf.SINP), BF16), jax.ShapeDtypeStruct((T, cf.CD), F32),
                   jax.ShapeDtypeStruct((T, GN), F32), jax.ShapeDtypeStruct((T, GN), F32),
                   jax.ShapeDtypeStruct((G, T, LANES), F32)) + (jax.ShapeDtypeStruct((1, cf.DI), F32),) * 4,
        scratch=[pltpu.VMEM((N, GW), F32)] + [pltpu.VMEM((tb, GW), F32)] * 6,
        args=(proj, proj, xact, xact, xact, bias_e, alog_e, dskip_e, norm_w, ypre, states, dy),
        sem=("parallel", "arbitrary"), rider=rider)


def _sum_groups(parts, name, into, col_blk):
    G, T, W = parts.shape
    tb = _tile(T, 512, SUBLANES)

    def body(p_ref, into_ref, o_ref):
        acc = p_ref[0]
        for g in range(1, G):
            acc = acc + p_ref[g]
        o_ref[...] = acc.astype(BF16)

    return _pcall(body, name=name, grid=(T // tb,), in_specs=[pl.BlockSpec((G, tb, W), lambda i: (0, i, 0)), ANY],
                  out_specs=pl.BlockSpec((tb, W), lambda i: (i, col_blk)),
                  out_shape=jax.ShapeDtypeStruct(into.shape, BF16), input_output_aliases={1: 0},
                  compiler_params=_cp(("parallel",)))(parts, into)


def _fill_bc(dact, dbm, dcm, name):
    T, GN = dbm.shape
    tb = _tile(T, 512, SUBLANES)
    blk = (dact.shape[1] - 2 * GN) // (2 * GN)
    assert blk * 2 * GN == dact.shape[1] - 2 * GN

    def body(b_ref, c_ref, into_ref, o_ref):
        o_ref[:, :GN] = b_ref[...]
        o_ref[:, GN:] = c_ref[...]

    row = pl.BlockSpec((tb, GN), lambda i: (i, 0))
    return _pcall(body, name=name, grid=(T // tb,), in_specs=[row, row, ANY],
                  out_specs=pl.BlockSpec((tb, 2 * GN), lambda i: (i, blk)),
                  out_shape=jax.ShapeDtypeStruct(dact.shape, F32), input_output_aliases={2: 0},
                  compiler_params=_cp(("parallel",)))(dbm, dcm, dact)


def _all_gather(x, name):
    def body(x_ref, out_ref, send_sems, recv_sems, local_sem):
        x, y, c = lax.axis_index("x"), lax.axis_index("y"), lax.axis_index("c")
        me, sibling = (x, y, c), (x, y, 1 - c)
        chips = [(1 - x, y), (x, 1 - y), (1 - x, 1 - y)]

        def blk(px, py, pc):
            return out_ref.at[4 * px + 2 * py + pc]

        def copy(k, block, to, src=None):
            return pltpu.make_async_remote_copy(
                src_ref=blk(*block) if src is None else src, dst_ref=blk(*block),
                send_sem=send_sems.at[k], recv_sem=recv_sems.at[k], device_id=to, device_id_type=MESH)

        mine = pltpu.make_async_copy(x_ref, blk(*me), local_sem)
        mine.start()
        first = [copy(0, me, sibling, src=x_ref)]
        first += [copy(1 + j, me, (*chip, c), src=x_ref) for j, chip in enumerate(chips)]
        for cp in first:
            cp.start()
        passed = [copy(4 + j, (*chip, c), sibling) for j, chip in enumerate(chips)]
        for j, chip in enumerate(chips):
            copy(1 + j, (*chip, c), me).wait_recv()
            passed[j].start()
        copy(0, sibling, me).wait_recv()
        for j, chip in enumerate(chips):
            copy(4 + j, (*chip, 1 - c), me).wait_recv()
        for cp in first + passed:
            cp.wait_send()
        mine.wait()

    return _pcall(
        body, name=name, in_specs=[ANY], out_specs=ANY,
        out_shape=jax.ShapeDtypeStruct((N_DEV,) + x.shape, x.dtype),
        scratch_shapes=[pltpu.SemaphoreType.DMA((7,)), pltpu.SemaphoreType.DMA((7,)), pltpu.SemaphoreType.DMA],
    )(x)


def _core_sum(g, recv, idx, name):
    _, _, a, b = g.shape
    n, blk, at = _plane_tiles(a, b, 2 * 2**20)

    def body(idx_ref, g_ref, r_ref, p_ref, own_ref):
        s = g_ref[...] + r_ref[...]
        p_ref[...] = s.astype(BF16)

        @pl.when(pl.program_id(1) == idx_ref[1])
        def _():
            own_ref[...] = s

    return _pcall(
        body, name=name,
        grid_spec=pltpu.PrefetchScalarGridSpec(
            num_scalar_prefetch=1, grid=(n, 4),
            in_specs=[pl.BlockSpec((None, None) + blk, lambda i, q, idx: (q, idx[0]) + at(i)),
                      pl.BlockSpec((None,) + blk, lambda i, q, idx: (q,) + at(i))],
            out_specs=(pl.BlockSpec((None,) + blk, lambda i, q, idx: (q,) + at(i)),
                       pl.BlockSpec(blk, lambda i, q, idx: at(i)))),
        out_shape=(jax.ShapeDtypeStruct((4, a, b), BF16), jax.ShapeDtypeStruct((a, b), F32)),
        compiler_params=_cp(("parallel", "arbitrary")),
    )(idx, g, recv)


def _plane_tiles(a, b, f32_bytes, row_off=0):
    if a % (2 * SUBLANES) == 0:
        tr = _tile(a, max(2 * SUBLANES, f32_bytes // (4 * b) // (2 * SUBLANES) * (2 * SUBLANES)), 2 * SUBLANES)
        return a // tr, (tr, b), lambda i: (i + row_off * (a // tr), 0)
    assert row_off == 0
    tc = _tile(b, max(LANES, f32_bytes // (4 * a) // LANES * LANES), LANES)
    return b // tc, (a, tc), lambda i: (0, i)


def _adam_math(w, g, m, v):
    m = ADAM_B1 * m + (1.0 - ADAM_B1) * g
    v = ADAM_B2 * v + (1.0 - ADAM_B2) * (g * g)
    m_hat = m / (1.0 - ADAM_B1 ** ADAM_STEP)
    v_hat = v / (1.0 - ADAM_B2 ** ADAM_STEP)
    delta = -ADAM_LR * (m_hat / (jnp.sqrt(v_hat) + ADAM_EPS) + ADAM_WD * w)
    return delta, m, v


def _chip_sum_adam(own, recv, w, m, v, layer, name, into=None, row_part=(0, 1)):
    a, b = own.shape
    n = w.shape[0]
    assert w.shape[1] == a * row_part[1]
    nt, blk, at = _plane_tiles(a, b, 2**20)
    _, _, at_w = _plane_tiles(a, b, 2**20, row_part[0])
    wspec = pl.BlockSpec((None,) + blk, lambda i: (layer,) + at_w(i))
    ospec = pl.BlockSpec(blk, at)
    n_into = 0 if into is None else 4

    def body(own_ref, r_ref, w_ref, m_ref, v_ref, *rest):
        g_ref, d_ref, mo_ref, vo_ref = rest[n_into:]
        g = own_ref[...]
        for k in range(3):
            g = g + r_ref[k].astype(F32)
        g_ref[...] = g
        d_ref[...], mo_ref[...], vo_ref[...] = _adam_math(w_ref[...], g, m_ref[...], v_ref[...])

    return _pcall(
        body, name=name, grid=(nt,),
        in_specs=[ospec, pl.BlockSpec((3,) + blk, lambda i: (0,) + at(i)), wspec, wspec, wspec] + [ANY] * n_into,
        out_specs=(wspec,) * 4, out_shape=(jax.ShapeDtypeStruct(w.shape, F32),) * 4,
        input_output_aliases={5 + k: k for k in range(n_into)},
        compiler_params=_cp(("parallel",)),
    )(own, recv, w, m, v, *(into or ()))


def _all_reduce_small(x, n_fold, fold_w, name):
    R, W = x.shape

    def body(x_ref, out_ref, buf, send_sems, recv_sems):
        xx, y, c = lax.axis_index("x"), lax.axis_index("y"), lax.axis_index("c")
        me = 4 * xx + 2 * y + c
        buf[me] = x_ref[...]
        copies = []
        for k in range(1, N_DEV):
            px, py, pc = xx ^ (k >> 2), y ^ ((k >> 1) & 1), c ^ (k & 1)
            copies.append(pltpu.make_async_remote_copy(
                src_ref=x_ref, dst_ref=buf.at[me], send_sem=send_sems.at[k - 1], recv_sem=recv_sems.at[k - 1],
                device_id=(px, py, pc), device_id_type=MESH))
        for cp in copies:
            cp.start()
        for cp in copies:
            cp.wait()
        acc = buf[0]
        for j in range(1, N_DEV):
            acc = acc + buf[j]
        out_ref[...] = acc
        if n_fold:
            l0 = lax.broadcasted_iota(jnp.int32, (W, W), 0) // fold_w
            l1 = lax.broadcasted_iota(jnp.int32, (W, W), 1)
            fold = (l0 == l1).astype(F32)
            out_ref[R - n_fold:, :] = _dot(acc[R - n_fold:, :], fold, NN, HI)

    return _pcall(
        body, name=name, in_specs=[pl.BlockSpec(memory_space=pltpu.VMEM)],
        out_specs=pl.BlockSpec(memory_space=pltpu.VMEM), out_shape=jax.ShapeDtypeStruct((R, W), F32),
        scratch_shapes=[pltpu.VMEM((N_DEV, R, W), F32), pltpu.SemaphoreType.DMA((N_DEV - 1,)),
                        pltpu.SemaphoreType.DMA((N_DEV - 1,))],
        compiler_params=pltpu.CompilerParams(vmem_limit_bytes=VMEM_LIMIT),
    )(x)


def _adam_small(w, g, m, v, name):
    def body(w_ref, g_ref, m_ref, v_ref, d_ref, mo_ref, vo_ref):
        d_ref[...], mo_ref[...], vo_ref[...] = _adam_math(w_ref[...], g_ref[...], m_ref[...], v_ref[...])

    return _pcall(body, name=name, out_shape=(jax.ShapeDtypeStruct(w.shape, F32),) * 3)(w, g, m, v)


def kernel(x, norm_mix_pre, norm_mix_post, norm_ffn_pre, norm_ffn_post, ret_w_in, ret_gn_w, ret_w_out, ssd_w_in, ssd_conv_w, ssd_conv_b, ssd_dt_bias, ssd_a_log, ssd_d, ssd_norm_w, ssd_w_out, mlp_w_up, mlp_w_down, loss_target, m_norm_mix_pre, m_norm_mix_post, m_norm_ffn_pre, m_norm_ffn_post, m_ret_w_in, m_ret_gn_w, m_ret_w_out, m_ssd_w_in, m_ssd_conv_w, m_ssd_conv_b, m_ssd_dt_bias, m_ssd_a_log, m_ssd_d, m_ssd_norm_w, m_ssd_w_out, m_mlp_w_up, m_mlp_w_down, v_norm_mix_pre, v_norm_mix_post, v_norm_ffn_pre, v_norm_ffn_post, v_ret_w_in, v_ret_gn_w, v_ret_w_out, v_ssd_w_in, v_ssd_conv_w, v_ssd_conv_b, v_ssd_dt_bias, v_ssd_a_log, v_ssd_d, v_ssd_norm_w, v_ssd_w_out, v_mlp_w_up, v_mlp_w_down):
    cf = _cfg()
    T, D = cf.T, cf.D
    ax, ay, ac = lax.axis_index("x"), lax.axis_index("y"), lax.axis_index("c")
    my_dev = 4 * ax + 2 * ay + ac
    idx = jnp.stack([ac, 2 * ax + ay]).astype(jnp.int32)

    w_ri = _all_gather(ret_w_in[0].T.astype(BF16), "ag_ret_w_in").reshape(cf.RIN, D)
    tr12 = lambda t: jnp.swapaxes(t, 1, 2)
    chips_of = lambda ws: _rider_join([_r_gather_chips(w.astype(BF16)) for w in ws])
    cores_of = lambda bufs: _rider_join([_r_gather_cores(b) for b in bufs])
    cw, cbw, nww = cf.CD // N_DEV, cf.CD // N_DEV, cf.DI // N_DEV
    small = jnp.concatenate([ssd_conv_w[0], ssd_conv_b, jnp.pad(ssd_norm_w, ((0, 0), (0, cw - nww))),
                             jnp.zeros((2, cw), F32)], axis=0)
    small = _all_gather(small, "ag_ssd_small")
    conv_w = jnp.transpose(small[:, :SSD_CONV_W, :], (1, 0, 2)).reshape(SSD_CONV_W, cf.CD)
    conv_b = small[:, SSD_CONV_W, :].reshape(1, cf.CD)
    ssd_nw = small[:, SSD_CONV_W + 1, :nww].reshape(1, cf.DI)

    half = cf.DK // 2
    inv_freq = ROPE_BASE ** (-jnp.arange(half, dtype=F32) / half)
    ang = jnp.arange(T).astype(F32)[:, None] * inv_freq[None, :]
    cos, sin = jnp.cos(ang), jnp.sin(ang)
    lgam = jnp.log1p(-jnp.exp2(-5.0 - jnp.arange(cf.H, dtype=F32)))
    rep = lambda p: jnp.repeat(p.reshape(1, cf.SH), cf.P, axis=1)
    bias_e, alog_e, dskip_e = rep(ssd_dt_bias), rep(ssd_a_log), rep(ssd_d)

    h0 = x.reshape(T, D)
    tgt = loss_target.reshape(T, D)
    nrm = lambda p, i: p[i:i + 1]

    u0 = _rms_fwd(h0, nrm(norm_mix_pre, 0), "rms_fwd0")
    proj0, part_a = _mm(u0, w_ri, kind="nt", name="mm_ret_in",
                        rider=chips_of([ret_w_out[0], mlp_w_up[0].T, mlp_w_down[0]]))
    (y0, o0, st0), got = _ret_fwd(cf, proj0, cos, sin, ret_gn_w, lgam, rider=_rider_join(
        [cores_of(part_a), chips_of([tr12(ssd_w_in)[0]])]))
    w_ro, w_up0, w_dn0 = got[0].reshape(cf.V, D), got[1].reshape(cf.FF, D), got[2].reshape(cf.FF, D)
    part_b = got[3:]
    m0, part_c = _mm(y0, w_ro, kind="nn", name="mm_ret_out", tr=cf.V, rider=chips_of([ssd_w_out[0]]))
    h1, u1 = _resid_fwd(h0, m0, nrm(norm_mix_post, 0), nrm(norm_ffn_pre, 0), "resid_fwd0")
    (sq0, act0), part_d = _mm(u1, w_up0, kind="nt", name="mm_up0", epi="relu2", rider=chips_of([mlp_w_up[1].T]))
    f0, got = _mm(sq0, w_dn0, kind="nn", name="mm_down0", tj=1024, rider=_rider_join(
        [cores_of(part_b + part_c + part_d), chips_of([mlp_w_down[1]])]))
    w_si = jnp.pad(got[0].reshape(cf.SIN, D), ((0, cf.SINP - cf.SIN), (0, 0)))
    w_so, w_up1 = got[1].reshape(cf.DI, D), got[2].reshape(cf.FF, D)
    h2, u2 = _resid_fwd(h1, f0, nrm(norm_ffn_post, 0), nrm(norm_mix_pre, 1), "resid_fwd1")
    proj1, (w_dn1,) = _mm(u2, w_si, kind="nt", name="mm_ssd_in", tj=1152, rider=cores_of(got[3:]))
    w_up, w_dn = [w_up0, w_up1], [w_dn0, w_dn1.reshape(cf.FF, D)]
    xact = _conv_fwd(cf, proj1, conv_w, conv_b)
    y1, yp1, st1 = _ssd_fwd(cf, proj1, xact, bias_e, alog_e, dskip_e, ssd_nw)
    m1 = _mm(y1, w_so, kind="nn", name="mm_ssd_out", tr=cf.DI)
    h3, u3 = _resid_fwd(h2, m1, nrm(norm_mix_post, 1), nrm(norm_ffn_pre, 1), "resid_fwd2")
    sq1, act1 = _mm(u3, w_up[1], kind="nt", name="mm_up1", epi="relu2")
    f1 = _mm(sq1, w_dn[1], kind="nn", name="mm_down1", tj=1024)
    g4, lsum = _final_fwd(h3, f1, nrm(norm_ffn_post, 1), tgt, "final_fwd")
    loss = lax.psum(0.5 * jnp.sum(lsum) / D, ("x", "y", "c"))

    as4 = lambda g: g.reshape(4, 2, g.shape[1], g.shape[2])
    swap_cores = lambda g: _r_swap_cores(as4(g))
    core_sum = lambda g, recv, name: _core_sum(as4(g), recv, idx, name + "_core_sum")

    def mlp_bwd(l, df, u, sq, act, rider=None, then=None):
        dpre = _mm(df, w_dn[l], kind="nt", name=f"mm_dpre{l}", out_dtype=BF16, epi="mul2act", epi_in=act, rider=rider)
        dpre, got = dpre if rider is not None else (dpre, [])
        g_dn = _mm(sq, df, kind="tn", name=f"mm_gdown{l}", tj=1024, tr=4096, rider=then(got) if then else None)
        g_dn, got = g_dn if then else (g_dn, [])
        g_dn = g_dn.reshape(N_DEV, cf.FF // N_DEV, D)
        du, (rc,) = _mm(dpre, w_up[l], kind="nn", name=f"mm_du_mlp{l}", tj=1024, tr=4096, rider=swap_cores(g_dn))
        part, own = core_sum(g_dn, rc, f"rs_mlp_down{l}")
        g_up, (r2,) = _mm(u, dpre, kind="tn", name=f"mm_gup{l}", tj=1024, tr=4096, out_nblk=N_DEV, rider=_r_swap_chips(part))
        return du, g_up, own, r2, got

    df1, g_nfpost1 = _norm_bwd(g4, "norm_bwd4", m=f1, w_post=nrm(norm_ffn_post, 1))
    du3, g_up1, own_dn1, r2_dn1, _ = mlp_bwd(1, df1, u3, sq1, act1, None)
    gh3, dm1, g_nfpre1, g_nmpost1 = _norm_bwd(g4, "norm_bwd3", du=du3, h=h3, w_pre=nrm(norm_ffn_pre, 1),
                                              m=m1, w_post=nrm(norm_mix_post, 1))
    dy1, (rc,) = _mm(dm1, w_so, kind="nt", name="mm_dy_ssd", rider=swap_cores(g_up1))
    part, own_up1 = core_sum(g_up1, rc, "rs_mlp_up1")
    g_so = _mm(y1, dm1, kind="tn", name="mm_g_ssd_out", tj=1024, tr=4096).reshape(N_DEV, cf.DI // N_DEV, D)
    (dz, dxs, dbm, dcm, ddt_parts, g_ssd_nw, g_dskip_e, g_alog_e, g_bias_e), (r2_up1, rc) = _ssd_bwd(
        cf, proj1, xact, bias_e, alog_e, dskip_e, ssd_nw, yp1, st1, dy1,
        rider=_rider_join([_r_swap_chips(part), swap_cores(g_so)]))
    part, own_so = core_sum(g_so, rc, "rs_ssd_w_out")
    dact = _fill_bc(dxs, dbm, dcm, "ssd_dact_fill")
    dpre1, g_conv_w, g_conv_b = _conv_bwd_pre(cf, proj1, conv_w, conv_b, dact)
    assert cf.SINP == cf.DI + cf.CD + LANES
    dproj1 = _conv_bwd_x(cf, dpre1, conv_w, dz)
    dproj1 = _sum_groups(ddt_parts, "ssd_ddt_sum", dproj1, (cf.DI + cf.CD) // LANES)
    du2, (r2_so,) = _mm(dproj1, w_si, kind="nn", name="mm_du_ssd", tj=1024, tr=3456, rider=_r_swap_chips(part))
    g_si = _mm(dproj1, u2, kind="tn", name="mm_g_ssd_in", ti=1152, tj=1024, tr=2048)
    g_si = g_si[:cf.SIN].reshape(N_DEV, cf.SIN // N_DEV, D)
    gh2, df0, g_nmpre1, g_nfpost0 = _norm_bwd(gh3, "norm_bwd2", du=du2, h=h2, w_pre=nrm(norm_mix_pre, 1),
                                              m=f0, w_post=nrm(norm_ffn_post, 0))
    own_si = []

    def si_chips(got):
        part, own = core_sum(g_si, got[0], "rs_ssd_w_in")
        own_si.append(own)
        return _r_swap_chips(part)

    du1, g_up0, own_dn0, r2_dn0, (r2_si,) = mlp_bwd(0, df0, u1, sq0, act0, swap_cores(g_si), si_chips)
    own_si = own_si[0]
    gh1, dm0, g_nfpre0, g_nmpost0 = _norm_bwd(gh2, "norm_bwd1", du=du1, h=h1, w_pre=nrm(norm_ffn_pre, 0),
                                              m=m0, w_post=nrm(norm_mix_post, 0))
    dy0, (rc,) = _mm(dm0, w_ro, kind="nt", name="mm_dy_ret", rider=swap_cores(g_up0))
    part, own_up0 = core_sum(g_up0, rc, "rs_mlp_up0")
    g_ro = _mm(y0, dm0, kind="tn", name="mm_g_ret_out", tj=1024, tr=4096).reshape(N_DEV, cf.V // N_DEV, D)
    (dproj0, g_gn), (r2_up0, rc) = _ret_bwd(cf, proj0, cos, sin, ret_gn_w, lgam, o0, st0, dy0,
                                            rider=_rider_join([_r_swap_chips(part), swap_cores(g_ro)]))
    part, own_ro = core_sum(g_ro, rc, "rs_ret_w_out")
    rin8 = cf.RIN // N_DEV
    g_ri0, (r2_ro,) = _mm(u0, dproj0, kind="tn", name="mm_g_ret_in0", ti=512, tj=rin8, tr=4096, out_nblk=N_DEV,
                          i_part=(0, 2), rider=_r_swap_chips(part))
    g_ri1, (rc,) = _mm(u0, dproj0, kind="tn", name="mm_g_ret_in1", ti=512, tj=rin8, tr=4096, out_nblk=N_DEV,
                       i_part=(1, 2), rider=swap_cores(g_ri0))
    part, own_ri0 = core_sum(g_ri0, rc, "rs_ret_w_in0")
    du0, (r2_ri0, rc) = _mm(dproj0, w_ri, kind="nn", name="mm_du_ret", tj=1024, tr=4096,
                            rider=_rider_join([_r_swap_chips(part), swap_cores(g_ri1)]))
    part, own_ri1 = core_sum(g_ri1, rc, "rs_ret_w_in1")
    (grad_x, g_nmpre0), (r2_ri1,) = _norm_bwd(gh1, "norm_bwd0", du=du0, h=h0, w_pre=nrm(norm_mix_pre, 0),
                                              rider=_r_swap_chips(part))

    g_nmpre = jnp.concatenate([g_nmpre0, g_nmpre1], axis=0)
    g_nmpost = jnp.concatenate([g_nmpost0, g_nmpost1], axis=0)
    g_nfpre = jnp.concatenate([g_nfpre0, g_nfpre1], axis=0)
    g_nfpost = jnp.concatenate([g_nfpost0, g_nfpost1], axis=0)
    segs = [g_nmpre, g_nmpost, g_nfpre, g_nfpost, g_gn, g_conv_w, g_conv_b, g_ssd_nw, g_bias_e, g_alog_e, g_dskip_e]
    flat = jnp.concatenate([s.reshape(-1, LANES) for s in segs], axis=0)
    n_fold = 3 * cf.DI // LANES
    red = _all_reduce_small(flat, n_fold, cf.P, "all_reduce_small")
    outs, r0 = [], 0
    for s in segs:
        nr = s.size // LANES
        outs.append(red[r0:r0 + nr])
        r0 += nr
    (g_nmpre, g_nmpost, g_nfpre, g_nfpost) = [o.reshape(DEPTH, D) for o in outs[:4]]
    g_gn = outs[4].reshape(1, cf.V)
    g_conv_w = lax.dynamic_slice_in_dim(outs[5].reshape(SSD_CONV_W, cf.CD), my_dev * cw, cw, axis=1)[None]
    g_conv_b = lax.dynamic_slice_in_dim(outs[6].reshape(1, cf.CD), my_dev * cbw, cbw, axis=1)
    g_ssd_nw = lax.dynamic_slice_in_dim(outs[7].reshape(1, cf.DI), my_dev * nww, nww, axis=1)
    per_row = LANES // cf.P
    g_bias, g_alog, g_dskip = [o[:, :per_row].reshape(1, cf.SH) for o in outs[8:]]

    def rs(own, recv, w, m, v, layer, name, into=None, row_part=(0, 1)):
        return _chip_sum_adam(own, recv, w, m, v, layer, name + "_adam", into, row_part)

    r_ri = rs(own_ri0, r2_ri0, ret_w_in, m_ret_w_in, v_ret_w_in, 0, "rs_ret_w_in0", None, (0, 2))
    r_ri = rs(own_ri1, r2_ri1, ret_w_in, m_ret_w_in, v_ret_w_in, 0, "rs_ret_w_in1", r_ri, (1, 2))
    r_ro = rs(own_ro, r2_ro, ret_w_out, m_ret_w_out, v_ret_w_out, 0, "rs_ret_w_out")
    r_si = [tr12(r) for r in rs(own_si, r2_si, tr12(ssd_w_in), tr12(m_ssd_w_in), tr12(v_ssd_w_in), 0, "rs_ssd_w_in")]
    r_so = rs(own_so, r2_so, ssd_w_out, m_ssd_w_out, v_ssd_w_out, 0, "rs_ssd_w_out")
    r_up = rs(own_up1, r2_up1, mlp_w_up, m_mlp_w_up, v_mlp_w_up, 1, "rs_mlp_up1")
    r_up = rs(own_up0, r2_up0, mlp_w_up, m_mlp_w_up, v_mlp_w_up, 0, "rs_mlp_up0", r_up)
    r_dn = rs(own_dn1, r2_dn1, mlp_w_down, m_mlp_w_down, v_mlp_w_down, 1, "rs_mlp_down1")
    r_dn = rs(own_dn0, r2_dn0, mlp_w_down, m_mlp_w_down, v_mlp_w_down, 0, "rs_mlp_down0", r_dn)
    lead = list

    def small_adam(w, g, m, v, name):
        return [g] + list(_adam_small(w, g, m, v, name))

    results = {
        "norm_mix_pre": small_adam(norm_mix_pre, g_nmpre, m_norm_mix_pre, v_norm_mix_pre, "adam_nmpre"),
        "norm_mix_post": small_adam(norm_mix_post, g_nmpost, m_norm_mix_post, v_norm_mix_post, "adam_nmpost"),
        "norm_ffn_pre": small_adam(norm_ffn_pre, g_nfpre, m_norm_ffn_pre, v_norm_ffn_pre, "adam_nfpre"),
        "norm_ffn_post": small_adam(norm_ffn_post, g_nfpost, m_norm_ffn_post, v_norm_ffn_post, "adam_nfpost"),
        "ret_w_in": lead(r_ri),
        "ret_gn_w": small_adam(ret_gn_w, g_gn, m_ret_gn_w, v_ret_gn_w, "adam_gn"),
        "ret_w_out": lead(r_ro),
        "ssd_w_in": lead(r_si),
        "ssd_conv_w": small_adam(ssd_conv_w, g_conv_w, m_ssd_conv_w, v_ssd_conv_w, "adam_conv_w"),
        "ssd_conv_b": small_adam(ssd_conv_b, g_conv_b, m_ssd_conv_b, v_ssd_conv_b, "adam_conv_b"),
        "ssd_dt_bias": small_adam(ssd_dt_bias, g_bias, m_ssd_dt_bias, v_ssd_dt_bias, "adam_dt_bias"),
        "ssd_a_log": small_adam(ssd_a_log, g_alog, m_ssd_a_log, v_ssd_a_log, "adam_a_log"),
        "ssd_d": small_adam(ssd_d, g_dskip, m_ssd_d, v_ssd_d, "adam_d"),
        "ssd_norm_w": small_adam(ssd_norm_w, g_ssd_nw, m_ssd_norm_w, v_ssd_norm_w, "adam_ssd_nw"),
        "ssd_w_out": lead(r_so),
        "mlp_w_up": r_up,
        "mlp_w_down": r_dn,
    }
    names = list(results)
    out = [loss, grad_x.reshape(1, T, D)]
    for k in range(4):
        out += [results[n][k] for n in names]
    return tuple(out)
```

```python
import functools
import math
import types

import jax
import jax.numpy as jnp
from jax import lax
from jax.experimental import pallas as pl
from jax.experimental.pallas import tpu as pltpu

F32 = jnp.float32
BF16 = jnp.bfloat16
HI = lax.Precision.HIGHEST
NN = (((1,), (0,)), ((), ()))
NT = (((1,), (1,)), ((), ()))
TN = (((0,), (0,)), ((), ()))
MESH = pl.DeviceIdType.MESH

V7X_VMEM_BYTES = 64 * 2**20
VMEM_LIMIT = V7X_VMEM_BYTES - 8 * 2**20
LANES = 128
SUBLANES = 8
N_DEV = 8

D_MODEL = 2048
SEQ = 8192
DEPTH = 2
CHUNK = 64
RMS_EPS = 1e-6
RET_HEAD_DK = 256
ROPE_BASE = 10000.0
GN_EPS = 1e-5
SSD_HEADDIM = 64
SSD_HEADS_PER_GROUP = 8
SSD_STATE = 128
SSD_CONV_W = 4
ADAM_LR = 0.001
ADAM_B1 = 0.9
ADAM_B2 = 0.999
ADAM_EPS = 1e-08
ADAM_WD = 0.01
ADAM_STEP = 10


def _cfg():
    c = types.SimpleNamespace()
    c.D, c.T, c.L = D_MODEL, SEQ, CHUNK
    c.DK = RET_HEAD_DK
    c.H = c.D // c.DK
    c.QK = c.H * c.DK
    c.DV = 2 * c.DK
    c.V = c.H * c.DV
    c.RIN = 2 * c.QK + 2 * c.V
    c.DI = 2 * c.D
    c.P = SSD_HEADDIM
    c.SH = c.DI // c.P
    c.RG = SSD_HEADS_PER_GROUP
    c.G = c.SH // c.RG
    c.GW = c.RG * c.P
    c.N = SSD_STATE
    c.CD = c.DI + 2 * c.G * c.N
    c.SIN = c.DI + c.CD + c.SH
    c.SINP = -(-c.SIN // LANES) * LANES
    c.FF = 4 * c.D
    c.NC = c.T // c.L
    return c


def _pcall(body, **kw):
    return pl.pallas_call(body, **kw)


def _cp(sem=None):
    return pltpu.CompilerParams(dimension_semantics=sem, vmem_limit_bytes=VMEM_LIMIT)


def _tile(n, pref, mult):
    if n <= pref:
        return n
    t = (pref // mult) * mult
    while t >= mult:
        if n % t == 0:
            return t
        t -= mult
    return n


def _dot(a, b, dn=NN, prec=None):
    return lax.dot_general(a, b, dn, precision=prec, preferred_element_type=F32)


ANY = pl.BlockSpec(memory_space=pl.ANY)


def _mesh_pos():
    return lax.axis_index("x"), lax.axis_index("y"), lax.axis_index("c")


def _rider_join(riders):
    j = types.SimpleNamespace(args=[], out_shape=[], aliases={}, sems=[])
    parts = []
    for r in riders:
        a0, o0, s0 = len(j.args), len(j.out_shape), len(j.sems)
        parts.append((r, a0, o0, s0))
        j.aliases.update({a0 + k: o0 + v for k, v in r.aliases.items()})
        j.args += r.args
        j.out_shape += r.out_shape
        j.sems += r.sems

    def make(rins, routs, sems):
        sends, recvs, locs = [], [], []
        for r, a0, o0, s0 in parts:
            s, rc, lc = r.make(rins[a0:a0 + len(r.args)], routs[o0:o0 + len(r.out_shape)], sems[s0:s0 + len(r.sems)])
            sends += s
            recvs += rc
            locs += lc
        return sends, recvs, locs

    j.make = make
    return j


def _rider_start(rider, rins, routs, sems):
    sends, _, locs = rider.make(rins, routs, sems)
    for cp in locs + sends:
        cp.start()


def _rider_wait(rider, rins, routs, sems):
    sends, recvs, locs = rider.make(rins, routs, sems)
    for cp in recvs:
        cp.wait_recv()
    for cp in sends:
        cp.wait_send()
    for cp in locs:
        cp.wait()


def _host_call(body, *, name, grid, in_specs, out_specs, out_shape, scratch, args, sem, rider=None):
    in_specs, out_specs, out_shape, scratch, args = map(list, (in_specs, out_specs, out_shape, scratch, args))
    if rider is None:
        res = _pcall(body, name=name, grid=grid, in_specs=in_specs, out_specs=out_specs, out_shape=out_shape,
                     scratch_shapes=scratch, compiler_params=_cp(sem))(*args)
        return list(res), []
    n_in, n_out, n_scr = len(args), len(out_shape), len(scratch)
    n_ra, n_ro = len(rider.args), len(rider.out_shape)

    def full(*refs):
        ins, rins = refs[:n_in], refs[n_in:n_in + n_ra]
        p = n_in + n_ra
        outs, routs = refs[p:p + n_out], refs[p + n_out:p + n_out + n_ro]
        p += n_out + n_ro
        scr, sems = refs[p:p + n_scr], refs[p + n_scr:]
        first = functools.reduce(jnp.logical_and, [pl.program_id(k) == 0 for k in range(len(grid))])
        last = functools.reduce(jnp.logical_and, [pl.program_id(k) == grid[k] - 1 for k in range(len(grid))])

        @pl.when(first)
        def _():
            _rider_start(rider, rins, routs, sems)

        body(*ins, *outs, *scr)

        @pl.when(last)
        def _():
            _rider_wait(rider, rins, routs, sems)

    res = _pcall(full, name=name, grid=grid, in_specs=in_specs + [ANY] * n_ra, out_specs=out_specs + [ANY] * n_ro,
                 out_shape=out_shape + rider.out_shape, scratch_shapes=scratch + rider.sems,
                 input_output_aliases={n_in + k: n_out + v for k, v in rider.aliases.items()},
                 compiler_params=_cp(("arbitrary",) * len(grid)))(*args, *rider.args)
    return list(res[:n_out]), list(res[n_out:])


def _comm(rider, name):
    n_ra, n_ro = len(rider.args), len(rider.out_shape)

    def body(*refs):
        rins, routs, sems = refs[:n_ra], refs[n_ra:n_ra + n_ro], refs[n_ra + n_ro:]
        _rider_start(rider, rins, routs, sems)
        _rider_wait(rider, rins, routs, sems)

    return list(_pcall(body, name=name, in_specs=[ANY] * n_ra, out_specs=[ANY] * n_ro, out_shape=rider.out_shape,
                       scratch_shapes=rider.sems, input_output_aliases=dict(rider.aliases))(*rider.args))


def _remote(src, dst, send_sems, recv_sems, k, to):
    return pltpu.make_async_remote_copy(src_ref=src, dst_ref=dst, send_sem=send_sems.at[k], recv_sem=recv_sems.at[k],
                                        device_id=to, device_id_type=MESH)


def _r_gather_chips(x):
    def make(rins, routs, sems):
        (x_ref,), (out_ref,), (send_sems, recv_sems, local_sem) = rins, routs, sems
        x_, y_, c_ = _mesh_pos()
        me = 4 * x_ + 2 * y_ + c_
        peers = [(x_, y_, 1 - c_), (1 - x_, y_, c_), (x_, 1 - y_, c_), (1 - x_, 1 - y_, c_)]
        sends = [_remote(x_ref, out_ref.at[me], send_sems, recv_sems, k, to) for k, to in enumerate(peers)]
        recvs = [_remote(x_ref, out_ref.at[4 * px + 2 * py + pc], send_sems, recv_sems, k, (px, py, pc))
                 for k, (px, py, pc) in enumerate(peers)]
        return sends, recvs, [pltpu.make_async_copy(x_ref, out_ref.at[me], local_sem)]

    return types.SimpleNamespace(
        args=[x], out_shape=[jax.ShapeDtypeStruct((N_DEV,) + x.shape, x.dtype)], aliases={},
        sems=[pltpu.SemaphoreType.DMA((4,)), pltpu.SemaphoreType.DMA((4,)), pltpu.SemaphoreType.DMA], make=make)


def _r_gather_cores(buf):
    def make(rins, routs, sems):
        (out_ref,), (send_sems, recv_sems) = routs, sems
        x_, y_, c_ = _mesh_pos()
        chips = [(1 - x_, y_), (x_, 1 - y_), (1 - x_, 1 - y_)]
        sends = [_remote(out_ref.at[4 * cx + 2 * cy + c_], out_ref.at[4 * cx + 2 * cy + c_], send_sems, recv_sems, k,
                         (x_, y_, 1 - c_)) for k, (cx, cy) in enumerate(chips)]
        recvs = [_remote(out_ref.at[4 * cx + 2 * cy + 1 - c_], out_ref.at[4 * cx + 2 * cy + 1 - c_], send_sems,
                         recv_sems, k, (x_, y_, 1 - c_)) for k, (cx, cy) in enumerate(chips)]
        return sends, recvs, []

    return types.SimpleNamespace(
        args=[buf], out_shape=[jax.ShapeDtypeStruct(buf.shape, buf.dtype)], aliases={0: 0},
        sems=[pltpu.SemaphoreType.DMA((3,)), pltpu.SemaphoreType.DMA((3,))], make=make)


def _r_swap_cores(g):
    def make(rins, routs, sems):
        (g_ref,), (out_ref,), (send_sems, recv_sems) = rins, routs, sems
        x_, y_, c_ = _mesh_pos()
        cps = [_remote(g_ref.at[q, 1 - c_], out_ref.at[q], send_sems, recv_sems, q, (x_, y_, 1 - c_)) for q in range(4)]
        return cps, cps, []

    return types.SimpleNamespace(
        args=[g], out_shape=[jax.ShapeDtypeStruct((4,) + g.shape[2:], g.dtype)], aliases={},
        sems=[pltpu.SemaphoreType.DMA((4,)), pltpu.SemaphoreType.DMA((4,))], make=make)


def _r_swap_chips(p):
    def make(rins, routs, sems):
        (p_ref,), (out_ref,), (send_sems, recv_sems) = rins, routs, sems
        x_, y_, c_ = _mesh_pos()
        chips = [(1 - x_, y_), (x_, 1 - y_), (1 - x_, 1 - y_)]
        cps = [_remote(p_ref.at[2 * cx + cy], out_ref.at[k], send_sems, recv_sems, k, (cx, cy, c_))
               for k, (cx, cy) in enumerate(chips)]
        return cps, cps, []

    return types.SimpleNamespace(
        args=[p], out_shape=[jax.ShapeDtypeStruct((3,) + p.shape[1:], p.dtype)], aliases={},
        sems=[pltpu.SemaphoreType.DMA((3,)), pltpu.SemaphoreType.DMA((3,))], make=make)


def _sigmoid(x):
    return 0.5 * (jnp.tanh(0.5 * x) + 1.0)


def _mm(a, b, *, kind, name, out_dtype=F32, ti=1024, tj=512, tr=2048, epi=None, epi_in=None, out_nblk=1, rider=None,
        i_part=(0, 1)):
    b_blk = b.ndim == 3
    if kind == "tn":
        R, I = a.shape
        I //= i_part[1]
    else:
        I, R = a.shape
    if kind == "nn":
        J = b.shape[1] if not b_blk else b.shape[0] * b.shape[2]
        nb_inner = b.shape[2] if b_blk else J
        r_inner = R
    elif kind == "nt":
        J = b.shape[0] if not b_blk else b.shape[1]
        nb_inner = J
        r_inner = b.shape[2] if b_blk else R
    else:
        J = b.shape[1]
        nb_inner = J
        r_inner = R
    out_inner = J // out_nblk
    ti = _tile(I, ti, LANES if kind == "tn" else SUBLANES)
    tj = _tile(min(nb_inner, out_inner), tj, LANES)
    assert nb_inner % tj == 0 and out_inner % tj == 0 and J % tj == 0
    tr = _tile(r_inner, tr, LANES)
    assert R % tr == 0
    ni, nj, nr = I // ti, J // tj, R // tr
    dn = {"nn": NN, "nt": NT, "tn": TN}[kind]

    if kind == "tn":
        i_off = i_part[0] * ni
        a_spec = pl.BlockSpec((tr, ti), lambda i, j, r: (r, i + i_off))
    else:
        a_spec = pl.BlockSpec((ti, tr), lambda i, j, r: (i, r))
    if kind == "nn":
        if b_blk:
            per = nb_inner // tj
            b_spec = pl.BlockSpec((None, tr, tj), lambda i, j, r: (j // per, r, j % per))
        else:
            b_spec = pl.BlockSpec((tr, tj), lambda i, j, r: (r, j))
    elif kind == "nt":
        if b_blk:
            per = r_inner // tr
            b_spec = pl.BlockSpec((None, tj, tr), lambda i, j, r: (r // per, j, r % per))
        else:
            b_spec = pl.BlockSpec((tj, tr), lambda i, j, r: (j, r))
    else:
        b_spec = pl.BlockSpec((tr, tj), lambda i, j, r: (r, j))
    if out_nblk > 1:
        pero = out_inner // tj
        o_spec = pl.BlockSpec((None, ti, tj), lambda i, j, r: (j // pero, i, j % pero))
        o_shape = (out_nblk, I, out_inner)
    else:
        o_spec = pl.BlockSpec((ti, tj), lambda i, j, r: (i, j))
        o_shape = (I, J)
    in_specs = [a_spec, b_spec]
    args = [a, b]
    if epi == "mul2act":
        in_specs.append(pl.BlockSpec((ti, tj), lambda i, j, r: (i, j)))
        args.append(epi_in)
    if epi == "relu2":
        out_shape = (jax.ShapeDtypeStruct(o_shape, BF16), jax.ShapeDtypeStruct(o_shape, BF16))
        out_specs = (o_spec, o_spec)
    else:
        out_shape = jax.ShapeDtypeStruct(o_shape, out_dtype)
        out_specs = o_spec
    n_in = len(args)
    n_out = 2 if epi == "relu2" else 1

    def body(*refs):
        a_ref, b_ref = refs[0], refs[1]
        outs = refs[n_in:n_in + n_out]
        acc_ref = refs[n_in + n_out] if nr > 1 else None

        def finish(acc):
            if epi == "relu2":
                act = jnp.maximum(acc, 0.0)
                outs[0][...] = (act * act).astype(BF16)
                outs[1][...] = act.astype(BF16)
            elif epi == "mul2act":
                outs[0][...] = (acc * (2.0 * refs[2][...].astype(F32))).astype(out_dtype)
            else:
                outs[0][...] = acc.astype(out_dtype)

        part = _dot(a_ref[...], b_ref[...], dn)
        if nr == 1:
            finish(part)
        else:
            r = pl.program_id(2)

            @pl.when(r == 0)
            def _():
                acc_ref[...] = part

            @pl.when(r > 0)
            def _():
                acc_ref[...] += part

            @pl.when(r == nr - 1)
            def _():
                finish(acc_ref[...])

    res, rider_res = _host_call(
        body, name=name, grid=(ni, nj, nr), in_specs=in_specs,
        out_specs=out_specs if n_out > 1 else [out_specs], out_shape=out_shape if n_out > 1 else [out_shape],
        scratch=[pltpu.VMEM((ti, tj), F32)] if nr > 1 else [], args=args,
        sem=("parallel", "parallel", "arbitrary"), rider=rider)
    res = tuple(res) if n_out > 1 else res[0]
    return res if rider is None else (res, rider_res)


def _rstd(x):
    return lax.rsqrt(jnp.mean(x * x, axis=-1, keepdims=True) + RMS_EPS)


def _rms_bwd_rows(x, w, dy):
    r = _rstd(x)
    xh = x * r
    dxh = dy * w
    dx = r * (dxh - xh * jnp.mean(dxh * xh, axis=-1, keepdims=True))
    return dx, jnp.sum(dy * xh, axis=0, keepdims=True)


def _row_spec(tb, d):
    return pl.BlockSpec((tb, d), lambda i: (i, 0))


def _vec_spec(d):
    return pl.BlockSpec((1, d), lambda i: (0, 0))


def _rms_fwd(h, w, name):
    T, D = h.shape
    tb = _tile(T, 512, SUBLANES)

    def body(h_ref, w_ref, u_ref):
        x = h_ref[...]
        u_ref[...] = (x * _rstd(x) * w_ref[...]).astype(BF16)

    return _pcall(body, name=name, grid=(T // tb,), in_specs=[_row_spec(tb, D), _vec_spec(D)],
                  out_specs=_row_spec(tb, D), out_shape=jax.ShapeDtypeStruct((T, D), BF16),
                  compiler_params=_cp(("parallel",)))(h, w)


def _resid_fwd(h, m, w_post, w_next, name):
    T, D = h.shape
    tb = _tile(T, 256, SUBLANES)

    def body(h_ref, m_ref, wp_ref, wn_ref, ho_ref, u_ref):
        x = m_ref[...]
        hn = h_ref[...] + x * _rstd(x) * wp_ref[...]
        ho_ref[...] = hn
        u_ref[...] = (hn * _rstd(hn) * wn_ref[...]).astype(BF16)

    return _pcall(body, name=name, grid=(T // tb,),
                  in_specs=[_row_spec(tb, D), _row_spec(tb, D), _vec_spec(D), _vec_spec(D)],
                  out_specs=(_row_spec(tb, D), _row_spec(tb, D)),
                  out_shape=(jax.ShapeDtypeStruct((T, D), F32), jax.ShapeDtypeStruct((T, D), BF16)),
                  compiler_params=_cp(("parallel",)))(h, m, w_post, w_next)


def _final_fwd(h, m, w_post, tgt, name):
    T, D = h.shape
    tb = _tile(T, 256, SUBLANES)

    def body(h_ref, m_ref, wp_ref, t_ref, g_ref, l_ref):
        x = m_ref[...]
        e = h_ref[...] + x * _rstd(x) * wp_ref[...] - t_ref[...]
        g_ref[...] = e * (1.0 / D)
        s = jnp.sum(e * e, axis=0, keepdims=True)

        @pl.when(pl.program_id(0) == 0)
        def _():
            l_ref[...] = s

        @pl.when(pl.program_id(0) > 0)
        def _():
            l_ref[...] += s

    return _pcall(body, name=name, grid=(T // tb,),
                  in_specs=[_row_spec(tb, D), _row_spec(tb, D), _vec_spec(D), _row_spec(tb, D)],
                  out_specs=(_row_spec(tb, D), _vec_spec(D)),
                  out_shape=(jax.ShapeDtypeStruct((T, D), F32), jax.ShapeDtypeStruct((1, D), F32)),
                  compiler_params=_cp(("arbitrary",)))(h, m, w_post, tgt)


def _norm_bwd(g_out, name, du=None, h=None, w_pre=None, m=None, w_post=None, rider=None):
    T, D = g_out.shape
    tb = _tile(T, 256, SUBLANES)
    has_pre, has_post = du is not None, m is not None
    args, in_specs = [g_out], [_row_spec(tb, D)]
    if has_pre:
        args += [du, h, w_pre]
        in_specs += [_row_spec(tb, D), _row_spec(tb, D), _vec_spec(D)]
    if has_post:
        args += [m, w_post]
        in_specs += [_row_spec(tb, D), _vec_spec(D)]
    out_shape, out_specs = [], []
    if has_pre:
        out_shape.append(jax.ShapeDtypeStruct((T, D), F32))
        out_specs.append(_row_spec(tb, D))
    if has_post:
        out_shape.append(jax.ShapeDtypeStruct((T, D), BF16))
        out_specs.append(_row_spec(tb, D))
    n_w = int(has_pre) + int(has_post)
    out_shape += [jax.ShapeDtypeStruct((1, D), F32)] * n_w
    out_specs += [_vec_spec(D)] * n_w
    n_in = len(args)

    def body(*refs):
        ins, outs = list(refs[:n_in]), list(refs[n_in:])
        g = ins.pop(0)[...]
        sums = []
        if has_pre:
            du_ref, h_ref, w_ref = ins.pop(0), ins.pop(0), ins.pop(0)
            dx, s = _rms_bwd_rows(h_ref[...], w_ref[...], du_ref[...])
            g = g + dx
            outs.pop(0)[...] = g
            sums.append(s)
        if has_post:
            m_ref, w_ref = ins.pop(0), ins.pop(0)
            dx, s = _rms_bwd_rows(m_ref[...], w_ref[...], g)
            outs.pop(0)[...] = dx.astype(BF16)
            sums.append(s)
        first = pl.program_id(0) == 0
        for o_ref, s in zip(outs, sums):
            @pl.when(first)
            def _(o_ref=o_ref, s=s):
                o_ref[...] = s

            @pl.when(jnp.logical_not(first))
            def _(o_ref=o_ref, s=s):
                o_ref[...] += s

    res, rider_res = _host_call(body, name=name, grid=(T // tb,), in_specs=in_specs, out_specs=out_specs,
                                out_shape=out_shape, scratch=[], args=args, sem=("arbitrary",), rider=rider)
    return tuple(res) if rider is None else (tuple(res), rider_res)


def _ret_consts(lg, L):
    ii = lax.broadcasted_iota(jnp.int32, (L, L), 0).astype(F32)
    jj = lax.broadcasted_iota(jnp.int32, (L, L), 1).astype(F32)
    dmat = jnp.exp(jnp.abs(ii - jj) * lg)
    idx = lax.broadcasted_iota(jnp.int32, (L, 1), 0).astype(F32)
    xi = jnp.exp((idx + 1.0) * lg)
    zeta = jnp.exp((L - 1.0 - idx) * lg)
    cd = jnp.exp(jnp.full((1, 1), L, F32) * lg)
    return dmat, xi, zeta, cd


def _rot(t, cs, sn):
    half = t.shape[-1] // 2
    t1, t2 = t[:, :half], t[:, half:]
    return jnp.concatenate([t1 * cs - t2 * sn, t1 * sn + t2 * cs], axis=-1)


def _rot_bwd(d, cs, sn):
    half = d.shape[-1] // 2
    d1, d2 = d[:, :half], d[:, half:]
    return jnp.concatenate([d1 * cs + d2 * sn, d2 * cs - d1 * sn], axis=-1)


def _ret_specs(cf, tb, rev):
    H, DK, DV = cf.H, cf.DK, cf.DV
    ni = cf.T // tb
    ri = (lambda i: ni - 1 - i) if rev else (lambda i: i)
    q = pl.BlockSpec((tb, DK), lambda h, i: (ri(i), h))
    k = pl.BlockSpec((tb, DK), lambda h, i: (ri(i), H + h))
    v = pl.BlockSpec((tb, DV), lambda h, i: (ri(i), cf.QK * 2 // DV + h))
    g = pl.BlockSpec((tb, DV), lambda h, i: (ri(i), cf.QK * 2 // DV + H + h))
    cs = pl.BlockSpec((tb, DK // 2), lambda h, i: (ri(i), 0))
    gw = pl.BlockSpec((1, DV), lambda h, i: (0, h))
    row_v = pl.BlockSpec((tb, DV), lambda h, i: (ri(i), h))
    row_k = pl.BlockSpec((tb, DK), lambda h, i: (ri(i), h))
    st = pl.BlockSpec((tb // cf.L, None, DK, DV), lambda h, i: (ri(i), h, 0, 0))
    lgs = pl.BlockSpec(memory_space=pltpu.SMEM)
    return q, k, v, g, cs, gw, row_v, row_k, st, lgs


def _ret_fwd(cf, proj, cos, sin, gn_w, lgam, rider=None):
    T, L, H, DK, DV = cf.T, cf.L, cf.H, cf.DK, cf.DV
    tb = _tile(T, 512, L)
    nck = tb // L
    q_s, k_s, v_s, g_s, cs_s, gw_s, row_v, _, st_s, lg_s = _ret_specs(cf, tb, False)
    kscale = DK ** -0.5

    def body(lg_ref, q_ref, k_ref, v_ref, g_ref, cos_ref, sin_ref, gw_ref, y_ref, o_ref, st_ref, state):
        h = pl.program_id(0)

        @pl.when(pl.program_id(1) == 0)
        def _():
            state[...] = jnp.zeros_like(state)

        dmat, xi, zeta, cd = _ret_consts(lg_ref[h], L)
        gw = gw_ref[...]

        def chunk(c, carry):
            rows = pl.ds(pl.multiple_of(c * L, L), L)
            cs, sn = cos_ref[rows, :], sin_ref[rows, :]
            qr = _rot(q_ref[rows, :], cs, sn)
            kr = _rot(k_ref[rows, :], cs, sn) * kscale
            qb, kb = qr.astype(BF16), kr.astype(BF16)
            vb = v_ref[rows, :].astype(BF16)
            st = state[...]
            stb = st.astype(BF16)
            st_ref[c] = stb
            s = _dot(qb, kb, NT) * dmat
            o = _dot(s.astype(BF16), vb) + _dot(qb, stb) * xi
            state[...] = st * cd + _dot((kr * zeta).astype(BF16), vb, TN)
            o_ref[rows, :] = o
            mu = jnp.mean(o, axis=-1, keepdims=True)
            oc = o - mu
            var = jnp.mean(oc * oc, axis=-1, keepdims=True)
            n = oc * lax.rsqrt(var + GN_EPS) * gw
            gt = g_ref[rows, :]
            y_ref[rows, :] = (gt * _sigmoid(gt) * n).astype(BF16)
            return carry

        lax.fori_loop(0, nck, chunk, 0, unroll=True)

    return _host_call(
        body, name="ret_fwd", grid=(H, T // tb),
        in_specs=[lg_s, q_s, k_s, v_s, g_s, cs_s, cs_s, gw_s],
        out_specs=(row_v, row_v, st_s),
        out_shape=(jax.ShapeDtypeStruct((T, cf.V), BF16), jax.ShapeDtypeStruct((T, cf.V), F32),
                   jax.ShapeDtypeStruct((cf.NC, H, DK, DV), BF16)),
        scratch=[pltpu.VMEM((DK, DV), F32)], args=(lgam, proj, proj, proj, proj, cos, sin, gn_w),
        sem=("parallel", "arbitrary"), rider=rider)


def _ret_bwd(cf, proj, cos, sin, gn_w, lgam, o, states, dy, rider=None):
    T, L, H, DK, DV = cf.T, cf.L, cf.H, cf.DK, cf.DV
    tb = _tile(T, 512, L)
    nck = tb // L
    q_s, k_s, v_s, g_s, cs_s, gw_s, row_v, row_k, st_s, lg_s = _ret_specs(cf, tb, True)
    kscale = DK ** -0.5

    ni = T // tb
    n_steps = H * ni
    col0 = (0, cf.QK, 2 * cf.QK, 2 * cf.QK + cf.V)
    widths = (DK, DK, DV, DV)

    def body(lg_ref, q_ref, k_ref, v_ref, g_ref, cos_ref, sin_ref, gw_ref, o_ref, st_ref, dy_ref,
             dp_ref, dgw_ref, dstate, dq_s, dk_s, dv_s, dg_s, sems):
        h = pl.program_id(0)
        step = h * ni + pl.program_id(1)
        slot = step % 2
        dq_ref, dk_ref, dv_ref, dg_ref = dq_s.at[slot], dk_s.at[slot], dv_s.at[slot], dg_s.at[slot]

        def results_out(s):
            hh, ii = s // ni, s % ni
            rows = pl.ds(pl.multiple_of((ni - 1 - ii) * tb, tb), tb)
            return [pltpu.make_async_copy(
                buf.at[s % 2], dp_ref.at[rows, pl.ds(pl.multiple_of(c0 + hh * w, LANES), w)], sems.at[s % 2, k])
                for k, (buf, c0, w) in enumerate(zip((dq_s, dk_s, dv_s, dg_s), col0, widths))]

        @pl.when(step >= 2)
        def _():
            for cp in results_out(step - 2):
                cp.wait()

        @pl.when(pl.program_id(1) == 0)
        def _():
            dstate[...] = jnp.zeros_like(dstate)
            dgw_ref[...] = jnp.zeros_like(dgw_ref)

        dmat, xi, zeta, cd = _ret_consts(lg_ref[h], L)
        gw = gw_ref[...]

        def chunk(t, carry):
            c = nck - 1 - t
            rows = pl.ds(pl.multiple_of(c * L, L), L)
            cs, sn = cos_ref[rows, :], sin_ref[rows, :]
            qr = _rot(q_ref[rows, :], cs, sn)
            kr = _rot(k_ref[rows, :], cs, sn) * kscale
            qb, kb = qr.astype(BF16), kr.astype(BF16)
            kzb = (kr * zeta).astype(BF16)
            vb = v_ref[rows, :].astype(BF16)
            s = (_dot(qb, kb, NT) * dmat).astype(BF16)
            oo = o_ref[rows, :]
            mu = jnp.mean(oo, axis=-1, keepdims=True)
            oc = oo - mu
            rstd = lax.rsqrt(jnp.mean(oc * oc, axis=-1, keepdims=True) + GN_EPS)
            oh = oc * rstd
            gt = g_ref[rows, :]
            sg = _sigmoid(gt)
            dyv = dy_ref[rows, :]
            dn = dyv * (gt * sg)
            dg_ref[rows, :] = (dyv * (oh * gw) * (sg * (1.0 + gt * (1.0 - sg)))).astype(BF16)
            dgw_ref[...] += jnp.sum(dn * oh, axis=0, keepdims=True)
            doh = dn * gw
            do = rstd * (doh - jnp.mean(doh, axis=-1, keepdims=True) - oh * jnp.mean(doh * oh, axis=-1, keepdims=True))
            dob = do.astype(BF16)
            doxb = (do * xi).astype(BF16)
            dst = dstate[...]
            dstb = dst.astype(BF16)
            stb = st_ref[c]
            dv_ref[rows, :] = (_dot(s, dob, TN) + _dot(kzb, dstb)).astype(BF16)
            ds = (_dot(dob, vb, NT) * dmat).astype(BF16)
            dqr = _dot(ds, kb) + _dot(doxb, stb, NT)
            dkr = _dot(ds, qb, TN) + _dot(vb, dstb, NT) * zeta
            dstate[...] = dst * cd + _dot(qb, doxb, TN)
            dq_ref[rows, :] = _rot_bwd(dqr, cs, sn).astype(BF16)
            dk_ref[rows, :] = _rot_bwd(dkr * kscale, cs, sn).astype(BF16)
            return carry

        lax.fori_loop(0, nck, chunk, 0, unroll=True)
        for cp in results_out(step):
            cp.start()

        @pl.when(step == n_steps - 1)
        def _():
            if n_steps > 1:
                for cp in results_out(step - 1):
                    cp.wait()
            for cp in results_out(step):
                cp.wait()

    return _host_call(
        body, name="ret_bwd", grid=(H, ni),
        in_specs=[lg_s, q_s, k_s, v_s, g_s, cs_s, cs_s, gw_s, row_v, st_s, row_v],
        out_specs=(ANY, gw_s),
        out_shape=(jax.ShapeDtypeStruct((T, cf.RIN), BF16), jax.ShapeDtypeStruct((1, cf.V), F32)),
        scratch=[pltpu.VMEM((DK, DV), F32), pltpu.VMEM((2, tb, DK), BF16), pltpu.VMEM((2, tb, DK), BF16),
                 pltpu.VMEM((2, tb, DV), BF16), pltpu.VMEM((2, tb, DV), BF16), pltpu.SemaphoreType.DMA((2, 4))],
        args=(lgam, proj, proj, proj, proj, cos, sin, gn_w, o, states, dy),
        sem=("arbitrary", "arbitrary"), rider=rider)


def _conv_pre(x, halo, w, b, first, W):
    tb = x.shape[0]
    ext = jnp.concatenate([jnp.where(first, 0.0, halo), x], axis=0)
    out = b + w[W - 1:W, :] * x
    for tap in range(W - 1):
        out = out + w[tap:tap + 1, :] * pltpu.roll(ext, W - 1 - tap, 0)[SUBLANES:SUBLANES + tb, :]
    return out, ext


def _conv_fwd(cf, proj, conv_w, conv_b):
    T, CD, W = cf.T, cf.CD, SSD_CONV_W
    tb = _tile(T, 512, SUBLANES)
    tc = _tile(CD, 512, LANES)
    off = cf.DI // tc
    nh = tb // SUBLANES

    def body(x_ref, halo_ref, w_ref, b_ref, o_ref):
        pre, _ = _conv_pre(x_ref[...], halo_ref[...], w_ref[...], b_ref[...], pl.program_id(1) == 0, W)
        o_ref[...] = pre * _sigmoid(pre)

    return _pcall(
        body, name="conv_fwd", grid=(CD // tc, T // tb),
        in_specs=[pl.BlockSpec((tb, tc), lambda j, i: (i, off + j)),
                  pl.BlockSpec((SUBLANES, tc), lambda j, i: (jnp.maximum(i * nh - 1, 0), off + j)),
                  pl.BlockSpec((W, tc), lambda j, i: (0, j)), pl.BlockSpec((1, tc), lambda j, i: (0, j))],
        out_specs=pl.BlockSpec((tb, tc), lambda j, i: (i, j)),
        out_shape=jax.ShapeDtypeStruct((T, CD), F32),
        compiler_params=_cp(("parallel", "arbitrary")),
    )(proj, proj, conv_w, conv_b)


def _conv_bwd_pre(cf, proj, conv_w, conv_b, dact):
    T, CD, W = cf.T, cf.CD, SSD_CONV_W
    tb = _tile(T, 512, SUBLANES)
    tc = _tile(CD, 512, LANES)
    off = cf.DI // tc
    nh = tb // SUBLANES

    def body(x_ref, halo_ref, w_ref, b_ref, da_ref, dp_ref, dw_ref, db_ref):
        x = x_ref[...]
        pre, ext = _conv_pre(x, halo_ref[...], w_ref[...], b_ref[...], pl.program_id(1) == 0, W)
        sg = _sigmoid(pre)
        dp = da_ref[...] * (sg * (1.0 + pre * (1.0 - sg)))
        dp_ref[...] = dp
        rows = [jnp.sum(dp * pltpu.roll(ext, W - 1 - tap, 0)[SUBLANES:SUBLANES + tb, :], axis=0, keepdims=True)
                for tap in range(W - 1)]
        rows.append(jnp.sum(dp * x, axis=0, keepdims=True))
        dw = jnp.concatenate(rows, axis=0)
        db = jnp.sum(dp, axis=0, keepdims=True)

        @pl.when(pl.program_id(1) == 0)
        def _():
            dw_ref[...] = dw
            db_ref[...] = db

        @pl.when(pl.program_id(1) > 0)
        def _():
            dw_ref[...] += dw
            db_ref[...] += db

    return _pcall(
        body, name="conv_bwd_pre", grid=(CD // tc, T // tb),
        in_specs=[pl.BlockSpec((tb, tc), lambda j, i: (i, off + j)),
                  pl.BlockSpec((SUBLANES, tc), lambda j, i: (jnp.maximum(i * nh - 1, 0), off + j)),
                  pl.BlockSpec((W, tc), lambda j, i: (0, j)), pl.BlockSpec((1, tc), lambda j, i: (0, j)),
                  pl.BlockSpec((tb, tc), lambda j, i: (i, j))],
        out_specs=(pl.BlockSpec((tb, tc), lambda j, i: (i, j)), pl.BlockSpec((W, tc), lambda j, i: (0, j)),
                   pl.BlockSpec((1, tc), lambda j, i: (0, j))),
        out_shape=(jax.ShapeDtypeStruct((T, CD), F32), jax.ShapeDtypeStruct((W, CD), F32),
                   jax.ShapeDtypeStruct((1, CD), F32)),
        compiler_params=_cp(("parallel", "arbitrary")),
    )(proj, proj, conv_w, conv_b, dact)


def _conv_bwd_x(cf, dpre, conv_w, into):
    T, CD, W = cf.T, cf.CD, SSD_CONV_W
    tb = _tile(T, 512, SUBLANES)
    tc = _tile(CD, 512, LANES)
    off = cf.DI // tc
    nh = tb // SUBLANES
    last_blk = T // SUBLANES - 1
    ni = T // tb

    def body(d_ref, halo_ref, w_ref, into_ref, o_ref):
        d = d_ref[...]
        w = w_ref[...]
        nxt = jnp.where(pl.program_id(1) == ni - 1, 0.0, halo_ref[...])
        ext = jnp.concatenate([d, nxt], axis=0)
        n = tb + SUBLANES
        out = w[W - 1:W, :] * d
        for tap in range(W - 1):
            out = out + w[tap:tap + 1, :] * pltpu.roll(ext, n - (W - 1 - tap), 0)[:tb, :]
        o_ref[...] = out.astype(BF16)

    return _pcall(
        body, name="conv_bwd_x", grid=(CD // tc, ni),
        in_specs=[pl.BlockSpec((tb, tc), lambda j, i: (i, j)),
                  pl.BlockSpec((SUBLANES, tc), lambda j, i: (jnp.minimum((i + 1) * nh, last_blk), j)),
                  pl.BlockSpec((W, tc), lambda j, i: (0, j)), ANY],
        out_specs=pl.BlockSpec((tb, tc), lambda j, i: (i, off + j)),
        out_shape=jax.ShapeDtypeStruct(into.shape, BF16), input_output_aliases={3: 0},
        compiler_params=_cp(("parallel", "arbitrary")),
    )(dpre, dpre, conv_w, into)


def _ssd_masks(cf, g, tb):
    L, GW, P, RG = cf.L, cf.GW, cf.P, cf.RG
    assert L == P and 2 * L == LANES and RG % 2 == 0
    i32 = jnp.int32
    hrow = lax.broadcasted_iota(i32, (LANES, GW), 0)
    hcol = lax.broadcasted_iota(i32, (LANES, GW), 1) // P
    expand = (hrow == g * RG + hcol).astype(BF16)
    ti = lax.broadcasted_iota(i32, (LANES, LANES), 0)
    tj = lax.broadcasted_iota(i32, (LANES, LANES), 1)
    btril = jnp.logical_and(ti // L == tj // L, ti >= tj).astype(BF16)
    r0 = lax.broadcasted_iota(i32, (L, GW), 0)
    c0 = lax.broadcasted_iota(i32, (L, GW), 1) % L
    tile_eye = (r0 == c0).astype(F32)
    lower = r0 >= c0
    p0 = lax.broadcasted_iota(i32, (2 * L, LANES), 0) // L
    p1 = lax.broadcasted_iota(i32, (2 * L, LANES), 1) // P
    pair = (p0 == p1).astype(F32)
    return expand, btril, tile_eye, lower, pair


def _softplus(x):
    return jnp.maximum(x, 0.0) + jnp.log1p(jnp.exp(-jnp.abs(x)))


def _split3(x):
    hi = x.astype(BF16)
    r1 = x - hi.astype(F32)
    mid = r1.astype(BF16)
    return hi, mid, (r1 - mid.astype(F32)).astype(BF16)


def _chunk_sums(btril, x, dn):
    hi, mid, lo = _split3(x)
    outs = []
    for k in range(x.shape[0] // LANES):
        sl = slice(k * LANES, (k + 1) * LANES)
        outs.append((_dot(btril, lo[sl], dn) + _dot(btril, mid[sl], dn)) + _dot(btril, hi[sl], dn))
    return jnp.concatenate(outs, axis=0)


def _expand_heads(x, expand, dn):
    hi, mid, lo = _split3(x)
    return (_dot(lo, expand, dn) + _dot(mid, expand, dn)) + _dot(hi, expand, dn)


def _ssd_chunk(cf, mk, acum, dt, xs, bm, cm):
    _, _, tile_eye, lower, pair = mk
    rowv = jnp.sum(acum * tile_eye, axis=0, keepdims=True)
    lf = jnp.exp(jnp.where(lower, acum - rowv, -1e30))
    xdt = xs * dt
    bb, cb_ = bm.astype(BF16), cm.astype(BF16)
    bb2 = jnp.concatenate([bb, bb], axis=0)
    cb2 = _dot(cb_, bb2, NT)
    ms, bds = [], []
    for j in range(cf.RG // 2):
        ln = slice(j * LANES, (j + 1) * LANES)
        ms.append((cb2 * lf[:, ln]).astype(BF16))
        xp = xdt[:, ln]
        bds.append((jnp.concatenate([xp, xp], axis=0) * pair).astype(BF16))
    return lf, xdt, bb, cb_, bb2, cb2, ms, bds


def _ssd_specs(cf, tb, rev):
    G, GW, N = cf.G, cf.GW, cf.N
    ni = cf.T // tb
    ri = (lambda i: ni - 1 - i) if rev else (lambda i: i)
    z = pl.BlockSpec((tb, GW), lambda g, i: (ri(i), g))
    dt = pl.BlockSpec((tb, LANES), lambda g, i: (ri(i), (cf.DI + cf.CD) // LANES))
    xs = pl.BlockSpec((tb, GW), lambda g, i: (ri(i), g))
    bm = pl.BlockSpec((tb, N), lambda g, i: (ri(i), cf.DI // N + g))
    cm = pl.BlockSpec((tb, N), lambda g, i: (ri(i), cf.DI // N + G + g))
    vec = pl.BlockSpec((1, GW), lambda g, i: (0, g))
    st = pl.BlockSpec((tb // cf.L, None, N, GW), lambda g, i: (ri(i), g, 0, 0))
    return z, dt, xs, bm, cm, vec, st


def _ssd_fwd(cf, proj, xact, bias_e, alog_e, dskip_e, norm_w):
    T, L, G, GW, N = cf.T, cf.L, cf.G, cf.GW, cf.N
    tb = _tile(T, 512, L)
    nck = tb // L
    z_s, dt_s, xs_s, b_s, c_s, vec_s, st_s = _ssd_specs(cf, tb, False)

    def body(z_ref, dt_ref, xs_ref, b_ref, c_ref, bias_ref, alog_ref, dsk_ref, nw_ref, y_ref, yp_ref, st_ref,
             state, dt_s, ac_s):
        @pl.when(pl.program_id(1) == 0)
        def _():
            state[...] = jnp.zeros_like(state)

        mk = _ssd_masks(cf, pl.program_id(0), tb)
        a_e = -jnp.exp(alog_ref[...])
        dt_all = _softplus(_expand_heads(dt_ref[...], mk[0], NN) + bias_ref[...])
        dt_s[...] = dt_all
        ac_s[...] = _chunk_sums(mk[1], dt_all * a_e, NN)

        def chunk(c, carry):
            rows = pl.ds(pl.multiple_of(c * L, L), L)
            acum = ac_s[rows, :]
            lf, xdt, bb, cb_, _, _, ms, bds = _ssd_chunk(cf, mk, acum, dt_s[rows, :], xs_ref[rows, :],
                                                         b_ref[rows, :], c_ref[rows, :])
            st = state[...]
            stb = st.astype(BF16)
            st_ref[c] = stb
            ydiag = jnp.concatenate([_dot(m, bd) for m, bd in zip(ms, bds)], axis=1)
            al = acum[L - 1:L, :]
            state[...] = st * jnp.exp(al) + _dot(bb, (xdt * jnp.exp(al - acum)).astype(BF16), TN)
            yp_ref[rows, :] = ydiag + _dot(cb_, stb) * jnp.exp(acum)
            return carry

        lax.fori_loop(0, nck, chunk, 0, unroll=True)
        z = z_ref[...]
        yg = (yp_ref[...] + dsk_ref[...] * xs_ref[...]) * (z * _sigmoid(z))
        y_ref[...] = (yg * _rstd(yg) * nw_ref[...]).astype(BF16)

    return _pcall(
        body, name="ssd_fwd", grid=(G, T // tb),
        in_specs=[z_s, dt_s, xs_s, b_s, c_s, vec_s, vec_s, vec_s, vec_s],
        out_specs=(z_s, z_s, st_s),
        out_shape=(jax.ShapeDtypeStruct((T, cf.DI), BF16), jax.ShapeDtypeStruct((T, cf.DI), F32),
                   jax.ShapeDtypeStruct((cf.NC, G, N, GW), BF16)),
        scratch_shapes=[pltpu.VMEM((N, GW), F32), pltpu.VMEM((tb, GW), F32), pltpu.VMEM((tb, GW), F32)],
        compiler_params=_cp(("parallel", "arbitrary")),
    )(proj, proj, xact, xact, xact, bias_e, alog_e, dskip_e, norm_w)


def _ssd_bwd(cf, proj, xact, bias_e, alog_e, dskip_e, norm_w, ypre, states, dy, rider=None):
    T, L, G, GW, N, RG = cf.T, cf.L, cf.G, cf.GW, cf.N, cf.RG
    tb = _tile(T, 512, L)
    nck = tb // L
    ni = T // tb
    z_s, dt_s, xs_s, b_s, c_s, vec_s, st_s = _ssd_specs(cf, tb, True)
    bc_out = pl.BlockSpec((tb, N), lambda g, i: (ni - 1 - i, g))
    ddt_out = pl.BlockSpec((None, tb, LANES), lambda g, i: (g, ni - 1 - i, 0))

    def body(z_ref, dt_ref, xs_ref, b_ref, c_ref, bias_ref, alog_ref, dsk_ref, nw_ref, yp_ref, st_ref, dy_ref,
             dz_ref, dxs_ref, db_ref, dc_ref, ddt_ref, dnw_ref, ddsk_ref, dalog_ref, dbias_ref,
             dstate, dt_s, ac_s, sg_s, dys_s, dxdt_s, dac_s):
        @pl.when(pl.program_id(1) == 0)
        def _():
            dstate[...] = jnp.zeros_like(dstate)
            for r in (dnw_ref, ddsk_ref, dalog_ref, dbias_ref):
                r[...] = jnp.zeros_like(r)

        mk = _ssd_masks(cf, pl.program_id(0), tb)
        expand, btril, tile_eye, lower, pair = mk
        a_e = -jnp.exp(alog_ref[...])
        dsk, nw = dsk_ref[...], nw_ref[...]
        last_row = (lax.broadcasted_iota(jnp.int32, (L, 1), 0) == L - 1).astype(F32)
        raw = _expand_heads(dt_ref[...], expand, NN) + bias_ref[...]
        dt_all = _softplus(raw)
        dt_s[...] = dt_all
        sg_s[...] = _sigmoid(raw)
        ac_s[...] = _chunk_sums(btril, dt_all * a_e, NN)
        z = z_ref[...]
        sz = _sigmoid(z)
        silu = z * sz
        xs_all = xs_ref[...]
        yd = yp_ref[...] + dsk * xs_all
        yg = yd * silu
        rr = _rstd(yg)
        xh = yg * rr
        dout = dy_ref[...]
        dnw_ref[...] += jnp.sum(dout * xh, axis=0, keepdims=True)
        dxh = dout * nw
        dyg = rr * (dxh - xh * jnp.mean(dxh * xh, axis=-1, keepdims=True))
        dz_ref[...] = (dyg * yd * (sz * (1.0 + z * (1.0 - sz)))).astype(BF16)
        dys_all = dyg * silu
        dys_s[...] = dys_all
        ddsk_ref[...] += jnp.sum(dys_all * xs_all, axis=0, keepdims=True)

        def chunk(t, carry):
            c = nck - 1 - t
            rows = pl.ds(pl.multiple_of(c * L, L), L)
            acum = ac_s[rows, :]
            lf, xdt, bb, cb_, bb2, cb2, ms, bds = _ssd_chunk(cf, mk, acum, dt_s[rows, :], xs_ref[rows, :],
                                                             b_ref[rows, :], c_ref[rows, :])
            stb = st_ref[c]
            eac = jnp.exp(acum)
            al = acum[L - 1:L, :]
            eal = jnp.exp(al)
            dte = jnp.exp(al - acum)
            dys = dys_s[rows, :]
            dyb = dys.astype(BF16)
            dms, dxs_, dsegs = [], [], []
            dcb2 = None
            for j in range(RG // 2):
                ln = slice(j * LANES, (j + 1) * LANES)
                dyj = dyb[:, ln]
                dbd = _dot(ms[j], dyj, TN) * pair
                dxs_.append(dbd[:L, :] + dbd[L:, :])
                tj = _dot(dyj, bds[j], NT) * lf[:, ln]
                dcb2 = tj if dcb2 is None else dcb2 + tj
                dsegs.append(tj * cb2)
            dxdt = jnp.concatenate(dxs_, axis=1)
            dseg = jnp.concatenate(dsegs, axis=1)
            dcb2 = dcb2.astype(BF16)
            dcm = _dot(dcb2, bb2)
            dbm2 = _dot(dcb2, cb_, TN)
            dbm = dbm2[:L, :] + dbm2[L:, :]
            dacum = dseg - tile_eye * jnp.sum(dseg, axis=0, keepdims=True)
            dyo = (dys * eac).astype(BF16)
            dcm = dcm + _dot(dyo, stb, NT)
            dacum = dacum + dys * _dot(cb_, stb) * eac
            dst = dstate[...]
            dstb = dst.astype(BF16)
            xd = xdt * dte
            dbm = dbm + _dot(xd.astype(BF16), dstb, NT)
            dxd = _dot(bb, dstb)
            dal = jnp.sum(dst * stb.astype(F32), axis=0, keepdims=True) * eal
            dxdt = dxdt + dxd * dte
            tt = dxd * xd
            dacum = dacum - tt + last_row * (dal + jnp.sum(tt, axis=0, keepdims=True))
            dstate[...] = dst * eal + _dot(cb_, dyo, TN)
            dxdt_s[rows, :] = dxdt
            dac_s[rows, :] = dacum
            db_ref[rows, :] = dbm
            dc_ref[rows, :] = dcm
            return carry

        lax.fori_loop(0, nck, chunk, 0, unroll=True)
        dda = _chunk_sums(btril, dac_s[...], TN)
        dxdt_all = dxdt_s[...]
        dt_all = dt_s[...]
        dxs_ref[...] = dys_s[...] * dsk + dxdt_all * dt_all
        ddt = dxdt_all * xs_ref[...] + dda * a_e
        dalog_ref[...] += jnp.sum(dda * dt_all, axis=0, keepdims=True) * a_e
        draw = ddt * sg_s[...]
        dbias_ref[...] += jnp.sum(draw, axis=0, keepdims=True)
        ddt_ref[...] = _expand_heads(draw, expand, NT)

    GN = G * N
    return _host_call(
        body, name="ssd_bwd", grid=(G, ni),
        in_specs=[z_s, dt_s, xs_s, b_s, c_s, vec_s, vec_s, vec_s, vec_s, z_s, st_s, z_s],
        out_specs=(z_s, z_s, bc_out, bc_out, ddt_out, vec_s, vec_s, vec_s, vec_s),
        out_shape=(jax.ShapeDtypeStruct((T, cf.SINP), BF16), jax.ShapeDtypeStruct((T, cf.CD), F32),
                   jax.ShapeDtypeStruct((T, GN), F32), jax.ShapeDtypeStruct((T, GN), F32),
                   jax.ShapeDtypeStruct((G, T, LANES), F32)) + (jax.ShapeDtypeStruct((1, cf.DI), F32),) * 4,
        scratch=[pltpu.VMEM((N, GW), F32)] + [pltpu.VMEM((tb, GW), F32)] * 6,
        args=(proj, proj, xact, xact, xact, bias_e, alog_e, dskip_e, norm_w, ypre, states, dy),
        sem=("parallel", "arbitrary"), rider=rider)


def _sum_groups(parts, name, into, col_blk):
    G, T, W = parts.shape
    tb = _tile(T, 512, SUBLANES)

    def body(p_ref, into_ref, o_ref):
        acc = p_ref[0]
        for g in range(1, G):
            acc = acc + p_ref[g]
        o_ref[...] = acc.astype(BF16)

    return _pcall(body, name=name, grid=(T // tb,), in_specs=[pl.BlockSpec((G, tb, W), lambda i: (0, i, 0)), ANY],
                  out_specs=pl.BlockSpec((tb, W), lambda i: (i, col_blk)),
                  out_shape=jax.ShapeDtypeStruct(into.shape, BF16), input_output_aliases={1: 0},
                  compiler_params=_cp(("parallel",)))(parts, into)


def _fill_bc(dact, dbm, dcm, name):
    T, GN = dbm.shape
    tb = _tile(T, 512, SUBLANES)
    blk = (dact.shape[1] - 2 * GN) // (2 * GN)
    assert blk * 2 * GN == dact.shape[1] - 2 * GN

    def body(b_ref, c_ref, into_ref, o_ref):
        o_ref[:, :GN] = b_ref[...]
        o_ref[:, GN:] = c_ref[...]

    row = pl.BlockSpec((tb, GN), lambda i: (i, 0))
    return _pcall(body, name=name, grid=(T // tb,), in_specs=[row, row, ANY],
                  out_specs=pl.BlockSpec((tb, 2 * GN), lambda i: (i, blk)),
                  out_shape=jax.ShapeDtypeStruct(dact.shape, F32), input_output_aliases={2: 0},
                  compiler_params=_cp(("parallel",)))(dbm, dcm, dact)


def _mm_gather_nt(a, w_shard, order, name, ti=1024, tj=512, late_rider=None):
    I, R = a.shape
    n = w_shard.shape[0]
    ti = _tile(I, ti, SUBLANES)
    tj = _tile(n, tj, LANES)
    per, ni = n // tj, I // ti
    n_tiles = N_DEV * per
    lr = late_rider if late_rider is not None else _rider_join([])
    n_ra, n_ro = len(lr.args), len(lr.out_shape)

    def body(ord_ref, a_ref, w_ref, *refs):
        rins, (out_ref, gath_ref), routs = refs[:n_ra], refs[n_ra:n_ra + 2], refs[n_ra + 2:n_ra + 2 + n_ro]
        bbuf, tile_sems, send_sems, recv_sems, local_sem = refs[n_ra + 2 + n_ro:n_ra + 7 + n_ro]
        rsems = refs[n_ra + 7 + n_ro:]
        k, t, i = pl.program_id(0), pl.program_id(1), pl.program_id(2)
        x_, y_, c_ = _mesh_pos()
        me = 4 * x_ + 2 * y_ + c_
        sib = (x_, y_, 1 - c_)
        chips = [(1 - x_, y_), (x_, 1 - y_), (1 - x_, 1 - y_)]
        near = [sib] + [(cx, cy, c_) for cx, cy in chips]

        def send_mine(q):
            return _remote(w_ref, gath_ref.at[me], send_sems, recv_sems, q, near[q])

        def from_near(q):
            px, py, pc = near[q]
            return _remote(w_ref, gath_ref.at[4 * px + 2 * py + pc], send_sems, recv_sems, q, near[q])

        def pass_on(j, core):
            rows = gath_ref.at[4 * chips[j][0] + 2 * chips[j][1] + core]
            return _remote(rows, rows, send_sems, recv_sems, 4 + j, sib)

        def tile_in(s):
            rows = pl.ds(pl.multiple_of((s % per) * tj, tj), tj)
            return pltpu.make_async_copy(gath_ref.at[ord_ref[s // per], rows, :], bbuf.at[s % 2], tile_sems.at[s % 2])

        mine = pltpu.make_async_copy(w_ref, gath_ref.at[me], local_sem)
        s = k * per + t
        first_row = i == 0

        @pl.when(jnp.logical_and(s == 0, first_row))
        def _():
            mine.start()
            for q in range(4):
                send_mine(q).start()
            mine.wait()
            tile_in(0).start()

        @pl.when(first_row)
        def _():
            tile_in(s).wait()

            @pl.when(s + 1 < n_tiles)
            def _():
                new_block = (s + 1) % per == 0
                kk = (s + 1) // per

                @pl.when(jnp.logical_and(new_block, kk == 1))
                def _():
                    from_near(0).wait_recv()

                for j in range(3):
                    @pl.when(jnp.logical_and(new_block, kk == 2 + j))
                    def _(j=j):
                        from_near(1 + j).wait_recv()
                        pass_on(j, c_).start()

                    @pl.when(jnp.logical_and(new_block, kk == 5 + j))
                    def _(j=j):
                        pass_on(j, 1 - c_).wait_recv()

                tile_in(s + 1).start()

        if n_ra:
            @pl.when(jnp.logical_and(s == 5 * per, first_row))
            def _():
                _rider_start(lr, rins, routs, rsems)

        out_ref[...] = _dot(a_ref[...], bbuf[s % 2], NT)

        @pl.when(jnp.logical_and(s == n_tiles - 1, i == ni - 1))
        def _():
            for q in range(4):
                send_mine(q).wait_send()
            for j in range(3):
                pass_on(j, c_).wait_send()
            if n_ra:
                _rider_wait(lr, rins, routs, rsems)

    res = _pcall(
        body, name=name,
        grid_spec=pltpu.PrefetchScalarGridSpec(
            num_scalar_prefetch=1, grid=(N_DEV, per, ni),
            in_specs=[pl.BlockSpec((ti, R), lambda k, t, i, o: (i, 0)), ANY] + [ANY] * n_ra,
            out_specs=[pl.BlockSpec((ti, tj), lambda k, t, i, o: (i, o[k] * per + t)), ANY] + [ANY] * n_ro,
            scratch_shapes=[pltpu.VMEM((2, tj, R), BF16), pltpu.SemaphoreType.DMA((2,)), pltpu.SemaphoreType.DMA((7,)),
                            pltpu.SemaphoreType.DMA((7,)), pltpu.SemaphoreType.DMA] + lr.sems),
        out_shape=[jax.ShapeDtypeStruct((I, N_DEV * n), F32), jax.ShapeDtypeStruct((N_DEV, n, R), BF16)] + lr.out_shape,
        input_output_aliases={3 + q: 2 + v for q, v in lr.aliases.items()},
        compiler_params=_cp(("arbitrary",) * 3),
    )(order, a, w_shard, *lr.args)
    return res[0], res[1], list(res[2:])


def _all_gather(x, name):
    def body(x_ref, out_ref, send_sems, recv_sems, local_sem):
        x, y, c = lax.axis_index("x"), lax.axis_index("y"), lax.axis_index("c")
        me, sibling = (x, y, c), (x, y, 1 - c)
        chips = [(1 - x, y), (x, 1 - y), (1 - x, 1 - y)]

        def blk(px, py, pc):
            return out_ref.at[4 * px + 2 * py + pc]

        def copy(k, block, to, src=None):
            return pltpu.make_async_remote_copy(
                src_ref=blk(*block) if src is None else src, dst_ref=blk(*block),
                send_sem=send_sems.at[k], recv_sem=recv_sems.at[k], device_id=to, device_id_type=MESH)

        mine = pltpu.make_async_copy(x_ref, blk(*me), local_sem)
        mine.start()
        first = [copy(0, me, sibling, src=x_ref)]
        first += [copy(1 + j, me, (*chip, c), src=x_ref) for j, chip in enumerate(chips)]
        for cp in first:
            cp.start()
        passed = [copy(4 + j, (*chip, c), sibling) for j, chip in enumerate(chips)]
        for j, chip in enumerate(chips):
            copy(1 + j, (*chip, c), me).wait_recv()
            passed[j].start()
        copy(0, sibling, me).wait_recv()
        for j, chip in enumerate(chips):
            copy(4 + j, (*chip, 1 - c), me).wait_recv()
        for cp in first + passed:
            cp.wait_send()
        mine.wait()

    return _pcall(
        body, name=name, in_specs=[ANY], out_specs=ANY,
        out_shape=jax.ShapeDtypeStruct((N_DEV,) + x.shape, x.dtype),
        scratch_shapes=[pltpu.SemaphoreType.DMA((7,)), pltpu.SemaphoreType.DMA((7,)), pltpu.SemaphoreType.DMA],
    )(x)


def _core_sum(g, recv, idx, name):
    _, _, a, b = g.shape
    n, blk, at = _plane_tiles(a, b, 2 * 2**20)

    def body(idx_ref, g_ref, r_ref, p_ref, own_ref):
        s = g_ref[...] + r_ref[...]
        p_ref[...] = s.astype(BF16)

        @pl.when(pl.program_id(1) == idx_ref[1])
        def _():
            own_ref[...] = s

    return _pcall(
        body, name=name,
        grid_spec=pltpu.PrefetchScalarGridSpec(
            num_scalar_prefetch=1, grid=(n, 4),
            in_specs=[pl.BlockSpec((None, None) + blk, lambda i, q, idx: (q, idx[0]) + at(i)),
                      pl.BlockSpec((None,) + blk, lambda i, q, idx: (q,) + at(i))],
            out_specs=(pl.BlockSpec((None,) + blk, lambda i, q, idx: (q,) + at(i)),
                       pl.BlockSpec(blk, lambda i, q, idx: at(i)))),
        out_shape=(jax.ShapeDtypeStruct((4, a, b), BF16), jax.ShapeDtypeStruct((a, b), F32)),
        compiler_params=_cp(("parallel", "arbitrary")),
    )(idx, g, recv)


def _plane_tiles(a, b, f32_bytes, row_off=0):
    if a % (2 * SUBLANES) == 0:
        tr = _tile(a, max(2 * SUBLANES, f32_bytes // (4 * b) // (2 * SUBLANES) * (2 * SUBLANES)), 2 * SUBLANES)
        return a // tr, (tr, b), lambda i: (i + row_off * (a // tr), 0)
    assert row_off == 0
    tc = _tile(b, max(LANES, f32_bytes // (4 * a) // LANES * LANES), LANES)
    return b // tc, (a, tc), lambda i: (0, i)


def _adam_math(w, g, m, v):
    m = ADAM_B1 * m + (1.0 - ADAM_B1) * g
    v = ADAM_B2 * v + (1.0 - ADAM_B2) * (g * g)
    m_hat = m / (1.0 - ADAM_B1 ** ADAM_STEP)
    v_hat = v / (1.0 - ADAM_B2 ** ADAM_STEP)
    delta = -ADAM_LR * (m_hat / (jnp.sqrt(v_hat) + ADAM_EPS) + ADAM_WD * w)
    return delta, m, v


def _chip_sum_adam(own, recv, w, m, v, layer, name, into=None, row_part=(0, 1)):
    a, b = own.shape
    n = w.shape[0]
    assert w.shape[1] == a * row_part[1]
    nt, blk, at = _plane_tiles(a, b, 2**20)
    _, _, at_w = _plane_tiles(a, b, 2**20, row_part[0])
    wspec = pl.BlockSpec((None,) + blk, lambda i: (layer,) + at_w(i))
    ospec = pl.BlockSpec(blk, at)
    n_into = 0 if into is None else 4

    def body(own_ref, r_ref, w_ref, m_ref, v_ref, *rest):
        g_ref, d_ref, mo_ref, vo_ref = rest[n_into:]
        g = own_ref[...]
        for k in range(3):
            g = g + r_ref[k].astype(F32)
        g_ref[...] = g
        d_ref[...], mo_ref[...], vo_ref[...] = _adam_math(w_ref[...], g, m_ref[...], v_ref[...])

    return _pcall(
        body, name=name, grid=(nt,),
        in_specs=[ospec, pl.BlockSpec((3,) + blk, lambda i: (0,) + at(i)), wspec, wspec, wspec] + [ANY] * n_into,
        out_specs=(wspec,) * 4, out_shape=(jax.ShapeDtypeStruct(w.shape, F32),) * 4,
        input_output_aliases={5 + k: k for k in range(n_into)},
        compiler_params=_cp(("parallel",)),
    )(own, recv, w, m, v, *(into or ()))


def _all_reduce_small(x, n_fold, fold_w, name):
    R, W = x.shape

    def body(x_ref, out_ref, buf, send_sems, recv_sems):
        xx, y, c = lax.axis_index("x"), lax.axis_index("y"), lax.axis_index("c")
        me = 4 * xx + 2 * y + c
        buf[me] = x_ref[...]
        copies = []
        for k in range(1, N_DEV):
            px, py, pc = xx ^ (k >> 2), y ^ ((k >> 1) & 1), c ^ (k & 1)
            copies.append(pltpu.make_async_remote_copy(
                src_ref=x_ref, dst_ref=buf.at[me], send_sem=send_sems.at[k - 1], recv_sem=recv_sems.at[k - 1],
                device_id=(px, py, pc), device_id_type=MESH))
        for cp in copies:
            cp.start()
        for cp in copies:
            cp.wait()
        acc = buf[0]
        for j in range(1, N_DEV):
            acc = acc + buf[j]
        out_ref[...] = acc
        if n_fold:
            l0 = lax.broadcasted_iota(jnp.int32, (W, W), 0) // fold_w
            l1 = lax.broadcasted_iota(jnp.int32, (W, W), 1)
            fold = (l0 == l1).astype(F32)
            out_ref[R - n_fold:, :] = _dot(acc[R - n_fold:, :], fold, NN, HI)

    return _pcall(
        body, name=name, in_specs=[pl.BlockSpec(memory_space=pltpu.VMEM)],
        out_specs=pl.BlockSpec(memory_space=pltpu.VMEM), out_shape=jax.ShapeDtypeStruct((R, W), F32),
        scratch_shapes=[pltpu.VMEM((N_DEV, R, W), F32), pltpu.SemaphoreType.DMA((N_DEV - 1,)),
                        pltpu.SemaphoreType.DMA((N_DEV - 1,))],
        compiler_params=pltpu.CompilerParams(vmem_limit_bytes=VMEM_LIMIT),
    )(x)


def _adam_small(w, g, m, v, name):
    def body(w_ref, g_ref, m_ref, v_ref, d_ref, mo_ref, vo_ref):
        d_ref[...], mo_ref[...], vo_ref[...] = _adam_math(w_ref[...], g_ref[...], m_ref[...], v_ref[...])

    return _pcall(body, name=name, out_shape=(jax.ShapeDtypeStruct(w.shape, F32),) * 3)(w, g, m, v)


def kernel(x, norm_mix_pre, norm_mix_post, norm_ffn_pre, norm_ffn_post, ret_w_in, ret_gn_w, ret_w_out, ssd_w_in, ssd_conv_w, ssd_conv_b, ssd_dt_bias, ssd_a_log, ssd_d, ssd_norm_w, ssd_w_out, mlp_w_up, mlp_w_down, loss_target, m_norm_mix_pre, m_norm_mix_post, m_norm_ffn_pre, m_norm_ffn_post, m_ret_w_in, m_ret_gn_w, m_ret_w_out, m_ssd_w_in, m_ssd_conv_w, m_ssd_conv_b, m_ssd_dt_bias, m_ssd_a_log, m_ssd_d, m_ssd_norm_w, m_ssd_w_out, m_mlp_w_up, m_mlp_w_down, v_norm_mix_pre, v_norm_mix_post, v_norm_ffn_pre, v_norm_ffn_post, v_ret_w_in, v_ret_gn_w, v_ret_w_out, v_ssd_w_in, v_ssd_conv_w, v_ssd_conv_b, v_ssd_dt_bias, v_ssd_a_log, v_ssd_d, v_ssd_norm_w, v_ssd_w_out, v_mlp_w_up, v_mlp_w_down):
    cf = _cfg()
    T, D = cf.T, cf.D
    ax, ay, ac = lax.axis_index("x"), lax.axis_index("y"), lax.axis_index("c")
    my_dev = 4 * ax + 2 * ay + ac
    idx = jnp.stack([ac, 2 * ax + ay]).astype(jnp.int32)

    tr12 = lambda t: jnp.swapaxes(t, 1, 2)
    chips_of = lambda ws: _rider_join([_r_gather_chips(w.astype(BF16)) for w in ws])
    cores_of = lambda bufs: _rider_join([_r_gather_cores(b) for b in bufs])
    cw, cbw, nww = cf.CD // N_DEV, cf.CD // N_DEV, cf.DI // N_DEV
    small = jnp.concatenate([ssd_conv_w[0], ssd_conv_b, jnp.pad(ssd_norm_w, ((0, 0), (0, cw - nww))),
                             jnp.zeros((2, cw), F32)], axis=0)
    small = _all_gather(small, "ag_ssd_small")
    conv_w = jnp.transpose(small[:, :SSD_CONV_W, :], (1, 0, 2)).reshape(SSD_CONV_W, cf.CD)
    conv_b = small[:, SSD_CONV_W, :].reshape(1, cf.CD)
    ssd_nw = small[:, SSD_CONV_W + 1, :nww].reshape(1, cf.DI)

    half = cf.DK // 2
    inv_freq = ROPE_BASE ** (-jnp.arange(half, dtype=F32) / half)
    ang = jnp.arange(T).astype(F32)[:, None] * inv_freq[None, :]
    cos, sin = jnp.cos(ang), jnp.sin(ang)
    lgam = jnp.log1p(-jnp.exp2(-5.0 - jnp.arange(cf.H, dtype=F32)))
    rep = lambda p: jnp.repeat(p.reshape(1, cf.SH), cf.P, axis=1)
    bias_e, alog_e, dskip_e = rep(ssd_dt_bias), rep(ssd_a_log), rep(ssd_d)

    h0 = x.reshape(T, D)
    tgt = loss_target.reshape(T, D)
    nrm = lambda p, i: p[i:i + 1]

    u0 = _rms_fwd(h0, nrm(norm_mix_pre, 0), "rms_fwd0")
    order = jnp.stack([my_dev ^ mask for mask in (0, 1, 4, 2, 6, 5, 3, 7)]).astype(jnp.int32)
    proj0, w_ri, part_a = _mm_gather_nt(u0, ret_w_in[0].T.astype(BF16), order, "mm_ret_in",
                                        late_rider=chips_of([ret_w_out[0]]))
    w_ri = w_ri.reshape(cf.RIN, D)
    (y0, o0, st0), got = _ret_fwd(cf, proj0, cos, sin, ret_gn_w, lgam, rider=_rider_join(
        [cores_of(part_a), chips_of([mlp_w_up[0].T, mlp_w_down[0]])]))
    w_ro = got[0].reshape(cf.V, D)
    m0, got = _mm(y0, w_ro, kind="nn", name="mm_ret_out", tr=cf.V, rider=cores_of(got[1:]))
    w_up0, w_dn0 = got[0].reshape(cf.FF, D), got[1].reshape(cf.FF, D)
    h1, u1 = _resid_fwd(h0, m0, nrm(norm_mix_post, 0), nrm(norm_ffn_pre, 0), "resid_fwd0")
    (sq0, act0), part_b = _mm(u1, w_up0, kind="nt", name="mm_up0", epi="relu2", rider=chips_of([tr12(ssd_w_in)[0]]))
    f0, got = _mm(sq0, w_dn0, kind="nn", name="mm_down0", tj=1024, rider=_rider_join(
        [cores_of(part_b), chips_of([ssd_w_out[0], mlp_w_up[1].T])]))
    w_si = jnp.pad(got[0].reshape(cf.SIN, D), ((0, cf.SINP - cf.SIN), (0, 0)))
    h2, u2 = _resid_fwd(h1, f0, nrm(norm_ffn_post, 0), nrm(norm_mix_pre, 1), "resid_fwd1")
    proj1, got = _mm(u2, w_si, kind="nt", name="mm_ssd_in", tj=1152, rider=_rider_join(
        [cores_of(got[1:]), chips_of([mlp_w_down[1]])]))
    w_so, w_up1 = got[0].reshape(cf.DI, D), got[1].reshape(cf.FF, D)
    xact = _conv_fwd(cf, proj1, conv_w, conv_b)
    y1, yp1, st1 = _ssd_fwd(cf, proj1, xact, bias_e, alog_e, dskip_e, ssd_nw)
    m1, (w_dn1,) = _mm(y1, w_so, kind="nn", name="mm_ssd_out", tr=cf.DI, rider=cores_of(got[2:]))
    w_up, w_dn = [w_up0, w_up1], [w_dn0, w_dn1.reshape(cf.FF, D)]
    h3, u3 = _resid_fwd(h2, m1, nrm(norm_mix_post, 1), nrm(norm_ffn_pre, 1), "resid_fwd2")
    sq1, act1 = _mm(u3, w_up[1], kind="nt", name="mm_up1", epi="relu2")
    f1 = _mm(sq1, w_dn[1], kind="nn", name="mm_down1", tj=1024)
    g4, lsum = _final_fwd(h3, f1, nrm(norm_ffn_post, 1), tgt, "final_fwd")
    loss = lax.psum(0.5 * jnp.sum(lsum) / D, ("x", "y", "c"))

    as4 = lambda g: g.reshape(4, 2, g.shape[1], g.shape[2])
    swap_cores = lambda g: _r_swap_cores(as4(g))
    core_sum = lambda g, recv, name: _core_sum(as4(g), recv, idx, name + "_core_sum")

    def mlp_bwd(l, df, u, sq, act, rider=None, then=None):
        dpre = _mm(df, w_dn[l], kind="nt", name=f"mm_dpre{l}", out_dtype=BF16, epi="mul2act", epi_in=act, rider=rider)
        dpre, got = dpre if rider is not None else (dpre, [])
        g_dn = _mm(sq, df, kind="tn", name=f"mm_gdown{l}", tj=1024, tr=4096, rider=then(got) if then else None)
        g_dn, got = g_dn if then else (g_dn, [])
        g_dn = g_dn.reshape(N_DEV, cf.FF // N_DEV, D)
        du, (rc,) = _mm(dpre, w_up[l], kind="nn", name=f"mm_du_mlp{l}", tj=1024, tr=4096, rider=swap_cores(g_dn))
        part, own = core_sum(g_dn, rc, f"rs_mlp_down{l}")
        g_up, (r2,) = _mm(u, dpre, kind="tn", name=f"mm_gup{l}", tj=1024, tr=4096, out_nblk=N_DEV, rider=_r_swap_chips(part))
        return du, g_up, own, r2, got

    df1, g_nfpost1 = _norm_bwd(g4, "norm_bwd4", m=f1, w_post=nrm(norm_ffn_post, 1))
    du3, g_up1, own_dn1, r2_dn1, _ = mlp_bwd(1, df1, u3, sq1, act1, None)
    gh3, dm1, g_nfpre1, g_nmpost1 = _norm_bwd(g4, "norm_bwd3", du=du3, h=h3, w_pre=nrm(norm_ffn_pre, 1),
                                              m=m1, w_post=nrm(norm_mix_post, 1))
    dy1, (rc,) = _mm(dm1, w_so, kind="nt", name="mm_dy_ssd", rider=swap_cores(g_up1))
    part, own_up1 = core_sum(g_up1, rc, "rs_mlp_up1")
    g_so = _mm(y1, dm1, kind="tn", name="mm_g_ssd_out", tj=1024, tr=4096).reshape(N_DEV, cf.DI // N_DEV, D)
    (dz, dxs, dbm, dcm, ddt_parts, g_ssd_nw, g_dskip_e, g_alog_e, g_bias_e), (r2_up1, rc) = _ssd_bwd(
        cf, proj1, xact, bias_e, alog_e, dskip_e, ssd_nw, yp1, st1, dy1,
        rider=_rider_join([_r_swap_chips(part), swap_cores(g_so)]))
    part, own_so = core_sum(g_so, rc, "rs_ssd_w_out")
    dact = _fill_bc(dxs, dbm, dcm, "ssd_dact_fill")
    dpre1, g_conv_w, g_conv_b = _conv_bwd_pre(cf, proj1, conv_w, conv_b, dact)
    assert cf.SINP == cf.DI + cf.CD + LANES
    dproj1 = _conv_bwd_x(cf, dpre1, conv_w, dz)
    dproj1 = _sum_groups(ddt_parts, "ssd_ddt_sum", dproj1, (cf.DI + cf.CD) // LANES)
    du2, (r2_so,) = _mm(dproj1, w_si, kind="nn", name="mm_du_ssd", tj=1024, tr=3456, rider=_r_swap_chips(part))
    g_si = _mm(dproj1, u2, kind="tn", name="mm_g_ssd_in", ti=1152, tj=1024, tr=2048)
    g_si = g_si[:cf.SIN].reshape(N_DEV, cf.SIN // N_DEV, D)
    gh2, df0, g_nmpre1, g_nfpost0 = _norm_bwd(gh3, "norm_bwd2", du=du2, h=h2, w_pre=nrm(norm_mix_pre, 1),
                                              m=f0, w_post=nrm(norm_ffn_post, 0))
    own_si = []

    def si_chips(got):
        part, own = core_sum(g_si, got[0], "rs_ssd_w_in")
        own_si.append(own)
        return _r_swap_chips(part)

    du1, g_up0, own_dn0, r2_dn0, (r2_si,) = mlp_bwd(0, df0, u1, sq0, act0, swap_cores(g_si), si_chips)
    own_si = own_si[0]
    gh1, dm0, g_nfpre0, g_nmpost0 = _norm_bwd(gh2, "norm_bwd1", du=du1, h=h1, w_pre=nrm(norm_ffn_pre, 0),
                                              m=m0, w_post=nrm(norm_mix_post, 0))
    dy0, (rc,) = _mm(dm0, w_ro, kind="nt", name="mm_dy_ret", rider=swap_cores(g_up0))
    part, own_up0 = core_sum(g_up0, rc, "rs_mlp_up0")
    g_ro = _mm(y0, dm0, kind="tn", name="mm_g_ret_out", tj=1024, tr=4096).reshape(N_DEV, cf.V // N_DEV, D)
    (dproj0, g_gn), (r2_up0, rc) = _ret_bwd(cf, proj0, cos, sin, ret_gn_w, lgam, o0, st0, dy0,
                                            rider=_rider_join([_r_swap_chips(part), swap_cores(g_ro)]))
    part, own_ro = core_sum(g_ro, rc, "rs_ret_w_out")
    rin8 = cf.RIN // N_DEV
    g_ri0, (r2_ro,) = _mm(u0, dproj0, kind="tn", name="mm_g_ret_in0", ti=512, tj=rin8, tr=4096, out_nblk=N_DEV,
                          i_part=(0, 2), rider=_r_swap_chips(part))
    g_ri1, (rc,) = _mm(u0, dproj0, kind="tn", name="mm_g_ret_in1", ti=512, tj=rin8, tr=4096, out_nblk=N_DEV,
                       i_part=(1, 2), rider=swap_cores(g_ri0))
    part, own_ri0 = core_sum(g_ri0, rc, "rs_ret_w_in0")
    du0, (r2_ri0, rc) = _mm(dproj0, w_ri, kind="nn", name="mm_du_ret", tj=1024, tr=4096,
                            rider=_rider_join([_r_swap_chips(part), swap_cores(g_ri1)]))
    part, own_ri1 = core_sum(g_ri1, rc, "rs_ret_w_in1")
    (grad_x, g_nmpre0), (r2_ri1,) = _norm_bwd(gh1, "norm_bwd0", du=du0, h=h0, w_pre=nrm(norm_mix_pre, 0),
                                              rider=_r_swap_chips(part))

    g_nmpre = jnp.concatenate([g_nmpre0, g_nmpre1], axis=0)
    g_nmpost = jnp.concatenate([g_nmpost0, g_nmpost1], axis=0)
    g_nfpre = jnp.concatenate([g_nfpre0, g_nfpre1], axis=0)
    g_nfpost = jnp.concatenate([g_nfpost0, g_nfpost1], axis=0)
    segs = [g_nmpre, g_nmpost, g_nfpre, g_nfpost, g_gn, g_conv_w, g_conv_b, g_ssd_nw, g_bias_e, g_alog_e, g_dskip_e]
    flat = jnp.concatenate([s.reshape(-1, LANES) for s in segs], axis=0)
    n_fold = 3 * cf.DI // LANES
    red = _all_reduce_small(flat, n_fold, cf.P, "all_reduce_small")
    outs, r0 = [], 0
    for s in segs:
        nr = s.size // LANES
        outs.append(red[r0:r0 + nr])
        r0 += nr
    (g_nmpre, g_nmpost, g_nfpre, g_nfpost) = [o.reshape(DEPTH, D) for o in outs[:4]]
    g_gn = outs[4].reshape(1, cf.V)
    g_conv_w = lax.dynamic_slice_in_dim(outs[5].reshape(SSD_CONV_W, cf.CD), my_dev * cw, cw, axis=1)[None]
    g_conv_b = lax.dynamic_slice_in_dim(outs[6].reshape(1, cf.CD), my_dev * cbw, cbw, axis=1)
    g_ssd_nw = lax.dynamic_slice_in_dim(outs[7].reshape(1, cf.DI), my_dev * nww, nww, axis=1)
    per_row = LANES // cf.P
    g_bias, g_alog, g_dskip = [o[:, :per_row].reshape(1, cf.SH) for o in outs[8:]]

    def rs(own, recv, w, m, v, layer, name, into=None, row_part=(0, 1)):
        return _chip_sum_adam(own, recv, w, m, v, layer, name + "_adam", into, row_part)

    r_ri = rs(own_ri0, r2_ri0, ret_w_in, m_ret_w_in, v_ret_w_in, 0, "rs_ret_w_in0", None, (0, 2))
    r_ri = rs(own_ri1, r2_ri1, ret_w_in, m_ret_w_in, v_ret_w_in, 0, "rs_ret_w_in1", r_ri, (1, 2))
    r_ro = rs(own_ro, r2_ro, ret_w_out, m_ret_w_out, v_ret_w_out, 0, "rs_ret_w_out")
    r_si = [tr12(r) for r in rs(own_si, r2_si, tr12(ssd_w_in), tr12(m_ssd_w_in), tr12(v_ssd_w_in), 0, "rs_ssd_w_in")]
    r_so = rs(own_so, r2_so, ssd_w_out, m_ssd_w_out, v_ssd_w_out, 0, "rs_ssd_w_out")
    r_up = rs(own_up1, r2_up1, mlp_w_up, m_mlp_w_up, v_mlp_w_up, 1, "rs_mlp_up1")
    r_up = rs(own_up0, r2_up0, mlp_w_up, m_mlp_w_up, v_mlp_w_up, 0, "rs_mlp_up0", r_up)
    r_dn = rs(own_dn1, r2_dn1, mlp_w_down, m_mlp_w_down, v_mlp_w_down, 1, "rs_mlp_down1")
    r_dn = rs(own_dn0, r2_dn0, mlp_w_down, m_mlp_w_down, v_mlp_w_down, 0, "rs_mlp_down0", r_dn)
    lead = list

    def small_adam(w, g, m, v, name):
        return [g] + list(_adam_small(w, g, m, v, name))

    results = {
        "norm_mix_pre": small_adam(norm_mix_pre, g_nmpre, m_norm_mix_pre, v_norm_mix_pre, "adam_nmpre"),
        "norm_mix_post": small_adam(norm_mix_post, g_nmpost, m_norm_mix_post, v_norm_mix_post, "adam_nmpost"),
        "norm_ffn_pre": small_adam(norm_ffn_pre, g_nfpre, m_norm_ffn_pre, v_norm_ffn_pre, "adam_nfpre"),
        "norm_ffn_post": small_adam(norm_ffn_post, g_nfpost, m_norm_ffn_post, v_norm_ffn_post, "adam_nfpost"),
        "ret_w_in": lead(r_ri),
        "ret_gn_w": small_adam(ret_gn_w, g_gn, m_ret_gn_w, v_ret_gn_w, "adam_gn"),
        "ret_w_out": lead(r_ro),
        "ssd_w_in": lead(r_si),
        "ssd_conv_w": small_adam(ssd_conv_w, g_conv_w, m_ssd_conv_w, v_ssd_conv_w, "adam_conv_w"),
        "ssd_conv_b": small_adam(ssd_conv_b, g_conv_b, m_ssd_conv_b, v_ssd_conv_b, "adam_conv_b"),
        "ssd_dt_bias": small_adam(ssd_dt_bias, g_bias, m_ssd_dt_bias, v_ssd_dt_bias, "adam_dt_bias"),
        "ssd_a_log": small_adam(ssd_a_log, g_alog, m_ssd_a_log, v_ssd_a_log, "adam_a_log"),
        "ssd_d": small_adam(ssd_d, g_dskip, m_ssd_d, v_ssd_d, "adam_d"),
        "ssd_norm_w": small_adam(ssd_norm_w, g_ssd_nw, m_ssd_norm_w, v_ssd_norm_w, "adam_ssd_nw"),
        "ssd_w_out": lead(r_so),
        "mlp_w_up": r_up,
        "mlp_w_down": r_dn,
    }
    names = list(results)
    out = [loss, grad_x.reshape(1, T, D)]
    for k in range(4):
        out += [results[n][k] for n in names]
    return tuple(out)
```

```python
import functools
import math
import types

import jax
import jax.numpy as jnp
from jax import lax
from jax.experimental import pallas as pl
from jax.experimental.pallas import tpu as pltpu

F32 = jnp.float32
BF16 = jnp.bfloat16
HI = lax.Precision.HIGHEST
NN = (((1,), (0,)), ((), ()))
NT = (((1,), (1,)), ((), ()))
TN = (((0,), (0,)), ((), ()))
MESH = pl.DeviceIdType.MESH

V7X_VMEM_BYTES = 64 * 2**20
VMEM_LIMIT = V7X_VMEM_BYTES - 8 * 2**20
LANES = 128
SUBLANES = 8
N_DEV = 8

D_MODEL = 2048
SEQ = 8192
DEPTH = 2
CHUNK = 64
RMS_EPS = 1e-6
RET_HEAD_DK = 256
ROPE_BASE = 10000.0
GN_EPS = 1e-5
SSD_HEADDIM = 64
SSD_HEADS_PER_GROUP = 8
SSD_STATE = 128
SSD_CONV_W = 4
ADAM_LR = 0.001
ADAM_B1 = 0.9
ADAM_B2 = 0.999
ADAM_EPS = 1e-08
ADAM_WD = 0.01
ADAM_STEP = 10


def _cfg():
    c = types.SimpleNamespace()
    c.D, c.T, c.L = D_MODEL, SEQ, CHUNK
    c.DK = RET_HEAD_DK
    c.H = c.D // c.DK
    c.QK = c.H * c.DK
    c.DV = 2 * c.DK
    c.V = c.H * c.DV
    c.RIN = 2 * c.QK + 2 * c.V
    c.DI = 2 * c.D
    c.P = SSD_HEADDIM
    c.SH = c.DI // c.P
    c.RG = SSD_HEADS_PER_GROUP
    c.G = c.SH // c.RG
    c.GW = c.RG * c.P
    c.N = SSD_STATE
    c.CD = c.DI + 2 * c.G * c.N
    c.SIN = c.DI + c.CD + c.SH
    c.SINP = -(-c.SIN // LANES) * LANES
    c.FF = 4 * c.D
    c.NC = c.T // c.L
    return c


def _pcall(body, **kw):
    return pl.pallas_call(body, **kw)


def _cp(sem=None):
    return pltpu.CompilerParams(dimension_semantics=sem, vmem_limit_bytes=VMEM_LIMIT)


def _tile(n, pref, mult):
    if n <= pref:
        return n
    t = (pref // mult) * mult
    while t >= mult:
        if n % t == 0:
            return t
        t -= mult
    return n


def _dot(a, b, dn=NN, prec=None):
    return lax.dot_general(a, b, dn, precision=prec, preferred_element_type=F32)


ANY = pl.BlockSpec(memory_space=pl.ANY)


def _mesh_pos():
    return lax.axis_index("x"), lax.axis_index("y"), lax.axis_index("c")


def _rider_join(riders):
    j = types.SimpleNamespace(args=[], out_shape=[], aliases={}, sems=[])
    parts = []
    for r in riders:
        a0, o0, s0 = len(j.args), len(j.out_shape), len(j.sems)
        parts.append((r, a0, o0, s0))
        j.aliases.update({a0 + k: o0 + v for k, v in r.aliases.items()})
        j.args += r.args
        j.out_shape += r.out_shape
        j.sems += r.sems

    def make(rins, routs, sems):
        sends, recvs, locs = [], [], []
        for r, a0, o0, s0 in parts:
            s, rc, lc = r.make(rins[a0:a0 + len(r.args)], routs[o0:o0 + len(r.out_shape)], sems[s0:s0 + len(r.sems)])
            sends += s
            recvs += rc
            locs += lc
        return sends, recvs, locs

    j.make = make
    return j


def _rider_start(rider, rins, routs, sems):
    sends, _, locs = rider.make(rins, routs, sems)
    for cp in locs + sends:
        cp.start()


def _rider_wait(rider, rins, routs, sems):
    sends, recvs, locs = rider.make(rins, routs, sems)
    for cp in recvs:
        cp.wait_recv()
    for cp in sends:
        cp.wait_send()
    for cp in locs:
        cp.wait()


def _host_call(body, *, name, grid, in_specs, out_specs, out_shape, scratch, args, sem, rider=None):
    in_specs, out_specs, out_shape, scratch, args = map(list, (in_specs, out_specs, out_shape, scratch, args))
    if rider is None:
        res = _pcall(body, name=name, grid=grid, in_specs=in_specs, out_specs=out_specs, out_shape=out_shape,
                     scratch_shapes=scratch, compiler_params=_cp(sem))(*args)
        return list(res), []
    n_in, n_out, n_scr = len(args), len(out_shape), len(scratch)
    n_ra, n_ro = len(rider.args), len(rider.out_shape)

    def full(*refs):
        ins, rins = refs[:n_in], refs[n_in:n_in + n_ra]
        p = n_in + n_ra
        outs, routs = refs[p:p + n_out], refs[p + n_out:p + n_out + n_ro]
        p += n_out + n_ro
        scr, sems = refs[p:p + n_scr], refs[p + n_scr:]
        first = functools.reduce(jnp.logical_and, [pl.program_id(k) == 0 for k in range(len(grid))])
        last = functools.reduce(jnp.logical_and, [pl.program_id(k) == grid[k] - 1 for k in range(len(grid))])

        @pl.when(first)
        def _():
            _rider_start(rider, rins, routs, sems)

        body(*ins, *outs, *scr)

        @pl.when(last)
        def _():
            _rider_wait(rider, rins, routs, sems)

    res = _pcall(full, name=name, grid=grid, in_specs=in_specs + [ANY] * n_ra, out_specs=out_specs + [ANY] * n_ro,
                 out_shape=out_shape + rider.out_shape, scratch_shapes=scratch + rider.sems,
                 input_output_aliases={n_in + k: n_out + v for k, v in rider.aliases.items()},
                 compiler_params=_cp(("arbitrary",) * len(grid)))(*args, *rider.args)
    return list(res[:n_out]), list(res[n_out:])


def _comm(rider, name):
    n_ra, n_ro = len(rider.args), len(rider.out_shape)

    def body(*refs):
        rins, routs, sems = refs[:n_ra], refs[n_ra:n_ra + n_ro], refs[n_ra + n_ro:]
        _rider_start(rider, rins, routs, sems)
        _rider_wait(rider, rins, routs, sems)

    return list(_pcall(body, name=name, in_specs=[ANY] * n_ra, out_specs=[ANY] * n_ro, out_shape=rider.out_shape,
                       scratch_shapes=rider.sems, input_output_aliases=dict(rider.aliases))(*rider.args))


def _remote(src, dst, send_sems, recv_sems, k, to):
    return pltpu.make_async_remote_copy(src_ref=src, dst_ref=dst, send_sem=send_sems.at[k], recv_sem=recv_sems.at[k],
                                        device_id=to, device_id_type=MESH)


def _r_gather_chips(x):
    def make(rins, routs, sems):
        (x_ref,), (out_ref,), (send_sems, recv_sems, local_sem) = rins, routs, sems
        x_, y_, c_ = _mesh_pos()
        me = 4 * x_ + 2 * y_ + c_
        peers = [(x_, y_, 1 - c_), (1 - x_, y_, c_), (x_, 1 - y_, c_), (1 - x_, 1 - y_, c_)]
        sends = [_remote(x_ref, out_ref.at[me], send_sems, recv_sems, k, to) for k, to in enumerate(peers)]
        recvs = [_remote(x_ref, out_ref.at[4 * px + 2 * py + pc], send_sems, recv_sems, k, (px, py, pc))
                 for k, (px, py, pc) in enumerate(peers)]
        return sends, recvs, [pltpu.make_async_copy(x_ref, out_ref.at[me], local_sem)]

    return types.SimpleNamespace(
        args=[x], out_shape=[jax.ShapeDtypeStruct((N_DEV,) + x.shape, x.dtype)], aliases={},
        sems=[pltpu.SemaphoreType.DMA((4,)), pltpu.SemaphoreType.DMA((4,)), pltpu.SemaphoreType.DMA], make=make)


def _r_gather_cores(buf):
    def make(rins, routs, sems):
        (out_ref,), (send_sems, recv_sems) = routs, sems
        x_, y_, c_ = _mesh_pos()
        chips = [(1 - x_, y_), (x_, 1 - y_), (1 - x_, 1 - y_)]
        sends = [_remote(out_ref.at[4 * cx + 2 * cy + c_], out_ref.at[4 * cx + 2 * cy + c_], send_sems, recv_sems, k,
                         (x_, y_, 1 - c_)) for k, (cx, cy) in enumerate(chips)]
        recvs = [_remote(out_ref.at[4 * cx + 2 * cy + 1 - c_], out_ref.at[4 * cx + 2 * cy + 1 - c_], send_sems,
                         recv_sems, k, (x_, y_, 1 - c_)) for k, (cx, cy) in enumerate(chips)]
        return sends, recvs, []

    return types.SimpleNamespace(
        args=[buf], out_shape=[jax.ShapeDtypeStruct(buf.shape, buf.dtype)], aliases={0: 0},
        sems=[pltpu.SemaphoreType.DMA((3,)), pltpu.SemaphoreType.DMA((3,))], make=make)


def _r_swap_cores(g):
    def make(rins, routs, sems):
        (g_ref,), (out_ref,), (send_sems, recv_sems) = rins, routs, sems
        x_, y_, c_ = _mesh_pos()
        cps = [_remote(g_ref.at[q, 1 - c_], out_ref.at[q], send_sems, recv_sems, q, (x_, y_, 1 - c_)) for q in range(4)]
        return cps, cps, []

    return types.SimpleNamespace(
        args=[g], out_shape=[jax.ShapeDtypeStruct((4,) + g.shape[2:], g.dtype)], aliases={},
        sems=[pltpu.SemaphoreType.DMA((4,)), pltpu.SemaphoreType.DMA((4,))], make=make)


def _r_swap_chips(p):
    def make(rins, routs, sems):
        (p_ref,), (out_ref,), (send_sems, recv_sems) = rins, routs, sems
        x_, y_, c_ = _mesh_pos()
        chips = [(1 - x_, y_), (x_, 1 - y_), (1 - x_, 1 - y_)]
        cps = [_remote(p_ref.at[2 * cx + cy], out_ref.at[k], send_sems, recv_sems, k, (cx, cy, c_))
               for k, (cx, cy) in enumerate(chips)]
        return cps, cps, []

    return types.SimpleNamespace(
        args=[p], out_shape=[jax.ShapeDtypeStruct((3,) + p.shape[1:], p.dtype)], aliases={},
        sems=[pltpu.SemaphoreType.DMA((3,)), pltpu.SemaphoreType.DMA((3,))], make=make)


def _sigmoid(x):
    return 0.5 * (jnp.tanh(0.5 * x) + 1.0)


def _mm(a, b, *, kind, name, out_dtype=F32, ti=1024, tj=512, tr=2048, epi=None, epi_in=None, out_nblk=1, rider=None,
        i_part=(0, 1)):
    b_blk = b.ndim == 3
    if kind == "tn":
        R, I = a.shape
        I //= i_part[1]
    else:
        I, R = a.shape
    if kind == "nn":
        J = b.shape[1] if not b_blk else b.shape[0] * b.shape[2]
        nb_inner = b.shape[2] if b_blk else J
        r_inner = R
    elif kind == "nt":
        J = b.shape[0] if not b_blk else b.shape[1]
        nb_inner = J
        r_inner = b.shape[2] if b_blk else R
    else:
        J = b.shape[1]
        nb_inner = J
        r_inner = R
    out_inner = J // out_nblk
    ti = _tile(I, ti, LANES if kind == "tn" else SUBLANES)
    tj = _tile(min(nb_inner, out_inner), tj, LANES)
    assert nb_inner % tj == 0 and out_inner % tj == 0 and J % tj == 0
    tr = _tile(r_inner, tr, LANES)
    assert R % tr == 0
    ni, nj, nr = I // ti, J // tj, R // tr
    dn = {"nn": NN, "nt": NT, "tn": TN}[kind]

    if kind == "tn":
        i_off = i_part[0] * ni
        a_spec = pl.BlockSpec((tr, ti), lambda i, j, r: (r, i + i_off))
    else:
        a_spec = pl.BlockSpec((ti, tr), lambda i, j, r: (i, r))
    if kind == "nn":
        if b_blk:
            per = nb_inner // tj
            b_spec = pl.BlockSpec((None, tr, tj), lambda i, j, r: (j // per, r, j % per))
        else:
            b_spec = pl.BlockSpec((tr, tj), lambda i, j, r: (r, j))
    elif kind == "nt":
        if b_blk:
            per = r_inner // tr
            b_spec = pl.BlockSpec((None, tj, tr), lambda i, j, r: (r // per, j, r % per))
        else:
            b_spec = pl.BlockSpec((tj, tr), lambda i, j, r: (j, r))
    else:
        b_spec = pl.BlockSpec((tr, tj), lambda i, j, r: (r, j))
    if out_nblk > 1:
        pero = out_inner // tj
        o_spec = pl.BlockSpec((None, ti, tj), lambda i, j, r: (j // pero, i, j % pero))
        o_shape = (out_nblk, I, out_inner)
    else:
        o_spec = pl.BlockSpec((ti, tj), lambda i, j, r: (i, j))
        o_shape = (I, J)
    in_specs = [a_spec, b_spec]
    args = [a, b]
    if epi == "mul2act":
        in_specs.append(pl.BlockSpec((ti, tj), lambda i, j, r: (i, j)))
        args.append(epi_in)
    if epi == "relu2":
        out_shape = (jax.ShapeDtypeStruct(o_shape, BF16), jax.ShapeDtypeStruct(o_shape, BF16))
        out_specs = (o_spec, o_spec)
    else:
        out_shape = jax.ShapeDtypeStruct(o_shape, out_dtype)
        out_specs = o_spec
    n_in = len(args)
    n_out = 2 if epi == "relu2" else 1

    def body(*refs):
        a_ref, b_ref = refs[0], refs[1]
        outs = refs[n_in:n_in + n_out]
        acc_ref = refs[n_in + n_out] if nr > 1 else None

        def finish(acc):
            if epi == "relu2":
                act = jnp.maximum(acc, 0.0)
                outs[0][...] = (act * act).astype(BF16)
                outs[1][...] = act.astype(BF16)
            elif epi == "mul2act":
                outs[0][...] = (acc * (2.0 * refs[2][...].astype(F32))).astype(out_dtype)
            else:
                outs[0][...] = acc.astype(out_dtype)

        part = _dot(a_ref[...], b_ref[...], dn)
        if nr == 1:
            finish(part)
        else:
            r = pl.program_id(2)

            @pl.when(r == 0)
            def _():
                acc_ref[...] = part

            @pl.when(r > 0)
            def _():
                acc_ref[...] += part

            @pl.when(r == nr - 1)
            def _():
                finish(acc_ref[...])

    res, rider_res = _host_call(
        body, name=name, grid=(ni, nj, nr), in_specs=in_specs,
        out_specs=out_specs if n_out > 1 else [out_specs], out_shape=out_shape if n_out > 1 else [out_shape],
        scratch=[pltpu.VMEM((ti, tj), F32)] if nr > 1 else [], args=args,
        sem=("parallel", "parallel", "arbitrary"), rider=rider)
    res = tuple(res) if n_out > 1 else res[0]
    return res if rider is None else (res, rider_res)


def _rstd(x):
    return lax.rsqrt(jnp.mean(x * x, axis=-1, keepdims=True) + RMS_EPS)


def _rms_bwd_rows(x, w, dy):
    r = _rstd(x)
    xh = x * r
    dxh = dy * w
    dx = r * (dxh - xh * jnp.mean(dxh * xh, axis=-1, keepdims=True))
    return dx, jnp.sum(dy * xh, axis=0, keepdims=True)


def _row_spec(tb, d):
    return pl.BlockSpec((tb, d), lambda i: (i, 0))


def _vec_spec(d):
    return pl.BlockSpec((1, d), lambda i: (0, 0))


def _rms_fwd(h, w, name):
    T, D = h.shape
    tb = _tile(T, 512, SUBLANES)

    def body(h_ref, w_ref, u_ref):
        x = h_ref[...]
        u_ref[...] = (x * _rstd(x) * w_ref[...]).astype(BF16)

    return _pcall(body, name=name, grid=(T // tb,), in_specs=[_row_spec(tb, D), _vec_spec(D)],
                  out_specs=_row_spec(tb, D), out_shape=jax.ShapeDtypeStruct((T, D), BF16),
                  compiler_params=_cp(("parallel",)))(h, w)


def _resid_fwd(h, m, w_post, w_next, name):
    T, D = h.shape
    tb = _tile(T, 256, SUBLANES)

    def body(h_ref, m_ref, wp_ref, wn_ref, ho_ref, u_ref):
        x = m_ref[...]
        hn = h_ref[...] + x * _rstd(x) * wp_ref[...]
        ho_ref[...] = hn
        u_ref[...] = (hn * _rstd(hn) * wn_ref[...]).astype(BF16)

    return _pcall(body, name=name, grid=(T // tb,),
                  in_specs=[_row_spec(tb, D), _row_spec(tb, D), _vec_spec(D), _vec_spec(D)],
                  out_specs=(_row_spec(tb, D), _row_spec(tb, D)),
                  out_shape=(jax.ShapeDtypeStruct((T, D), F32), jax.ShapeDtypeStruct((T, D), BF16)),
                  compiler_params=_cp(("parallel",)))(h, m, w_post, w_next)


def _final_fwd(h, m, w_post, tgt, name):
    T, D = h.shape
    tb = _tile(T, 256, SUBLANES)

    def body(h_ref, m_ref, wp_ref, t_ref, g_ref, l_ref):
        x = m_ref[...]
        e = h_ref[...] + x * _rstd(x) * wp_ref[...] - t_ref[...]
        g_ref[...] = e * (1.0 / D)
        s = jnp.sum(e * e, axis=0, keepdims=True)

        @pl.when(pl.program_id(0) == 0)
        def _():
            l_ref[...] = s

        @pl.when(pl.program_id(0) > 0)
        def _():
            l_ref[...] += s

    return _pcall(body, name=name, grid=(T // tb,),
                  in_specs=[_row_spec(tb, D), _row_spec(tb, D), _vec_spec(D), _row_spec(tb, D)],
                  out_specs=(_row_spec(tb, D), _vec_spec(D)),
                  out_shape=(jax.ShapeDtypeStruct((T, D), F32), jax.ShapeDtypeStruct((1, D), F32)),
                  compiler_params=_cp(("arbitrary",)))(h, m, w_post, tgt)


def _norm_bwd(g_out, name, du=None, h=None, w_pre=None, m=None, w_post=None, rider=None):
    T, D = g_out.shape
    tb = _tile(T, 256, SUBLANES)
    has_pre, has_post = du is not None, m is not None
    args, in_specs = [g_out], [_row_spec(tb, D)]
    if has_pre:
        args += [du, h, w_pre]
        in_specs += [_row_spec(tb, D), _row_spec(tb, D), _vec_spec(D)]
    if has_post:
        args += [m, w_post]
        in_specs += [_row_spec(tb, D), _vec_spec(D)]
    out_shape, out_specs = [], []
    if has_pre:
        out_shape.append(jax.ShapeDtypeStruct((T, D), F32))
        out_specs.append(_row_spec(tb, D))
    if has_post:
        out_shape.append(jax.ShapeDtypeStruct((T, D), BF16))
        out_specs.append(_row_spec(tb, D))
    n_w = int(has_pre) + int(has_post)
    out_shape += [jax.ShapeDtypeStruct((1, D), F32)] * n_w
    out_specs += [_vec_spec(D)] * n_w
    n_in = len(args)

    def body(*refs):
        ins, outs = list(refs[:n_in]), list(refs[n_in:])
        g = ins.pop(0)[...]
        sums = []
        if has_pre:
            du_ref, h_ref, w_ref = ins.pop(0), ins.pop(0), ins.pop(0)
            dx, s = _rms_bwd_rows(h_ref[...], w_ref[...], du_ref[...])
            g = g + dx
            outs.pop(0)[...] = g
            sums.append(s)
        if has_post:
            m_ref, w_ref = ins.pop(0), ins.pop(0)
            dx, s = _rms_bwd_rows(m_ref[...], w_ref[...], g)
            outs.pop(0)[...] = dx.astype(BF16)
            sums.append(s)
        first = pl.program_id(0) == 0
        for o_ref, s in zip(outs, sums):
            @pl.when(first)
            def _(o_ref=o_ref, s=s):
                o_ref[...] = s

            @pl.when(jnp.logical_not(first))
            def _(o_ref=o_ref, s=s):
                o_ref[...] += s

    res, rider_res = _host_call(body, name=name, grid=(T // tb,), in_specs=in_specs, out_specs=out_specs,
                                out_shape=out_shape, scratch=[], args=args, sem=("arbitrary",), rider=rider)
    return tuple(res) if rider is None else (tuple(res), rider_res)


def _ret_consts(lg, L):
    ii = lax.broadcasted_iota(jnp.int32, (L, L), 0).astype(F32)
    jj = lax.broadcasted_iota(jnp.int32, (L, L), 1).astype(F32)
    dmat = jnp.exp(jnp.abs(ii - jj) * lg)
    idx = lax.broadcasted_iota(jnp.int32, (L, 1), 0).astype(F32)
    xi = jnp.exp((idx + 1.0) * lg)
    zeta = jnp.exp((L - 1.0 - idx) * lg)
    cd = jnp.exp(jnp.full((1, 1), L, F32) * lg)
    return dmat, xi, zeta, cd


def _rot(t, cs, sn):
    half = t.shape[-1] // 2
    t1, t2 = t[:, :half], t[:, half:]
    return jnp.concatenate([t1 * cs - t2 * sn, t1 * sn + t2 * cs], axis=-1)


def _rot_bwd(d, cs, sn):
    half = d.shape[-1] // 2
    d1, d2 = d[:, :half], d[:, half:]
    return jnp.concatenate([d1 * cs + d2 * sn, d2 * cs - d1 * sn], axis=-1)


def _ret_specs(cf, tb, rev):
    H, DK, DV = cf.H, cf.DK, cf.DV
    ni = cf.T // tb
    ri = (lambda i: ni - 1 - i) if rev else (lambda i: i)
    q = pl.BlockSpec((tb, DK), lambda h, i: (ri(i), h))
    k = pl.BlockSpec((tb, DK), lambda h, i: (ri(i), H + h))
    v = pl.BlockSpec((tb, DV), lambda h, i: (ri(i), cf.QK * 2 // DV + h))
    g = pl.BlockSpec((tb, DV), lambda h, i: (ri(i), cf.QK * 2 // DV + H + h))
    cs = pl.BlockSpec((tb, DK // 2), lambda h, i: (ri(i), 0))
    gw = pl.BlockSpec((1, DV), lambda h, i: (0, h))
    row_v = pl.BlockSpec((tb, DV), lambda h, i: (ri(i), h))
    row_k = pl.BlockSpec((tb, DK), lambda h, i: (ri(i), h))
    st = pl.BlockSpec((tb // cf.L, None, DK, DV), lambda h, i: (ri(i), h, 0, 0))
    lgs = pl.BlockSpec(memory_space=pltpu.SMEM)
    return q, k, v, g, cs, gw, row_v, row_k, st, lgs


def _ret_fwd(cf, proj, cos, sin, gn_w, lgam, rider=None):
    T, L, H, DK, DV = cf.T, cf.L, cf.H, cf.DK, cf.DV
    tb = _tile(T, 512, L)
    nck = tb // L
    q_s, k_s, v_s, g_s, cs_s, gw_s, row_v, _, st_s, lg_s = _ret_specs(cf, tb, False)
    kscale = DK ** -0.5

    def body(lg_ref, q_ref, k_ref, v_ref, g_ref, cos_ref, sin_ref, gw_ref, y_ref, o_ref, st_ref, state):
        h = pl.program_id(0)

        @pl.when(pl.program_id(1) == 0)
        def _():
            state[...] = jnp.zeros_like(state)

        dmat, xi, zeta, cd = _ret_consts(lg_ref[h], L)
        gw = gw_ref[...]

        def chunk(c, carry):
            rows = pl.ds(pl.multiple_of(c * L, L), L)
            cs, sn = cos_ref[rows, :], sin_ref[rows, :]
            qr = _rot(q_ref[rows, :], cs, sn)
            kr = _rot(k_ref[rows, :], cs, sn) * kscale
            qb, kb = qr.astype(BF16), kr.astype(BF16)
            vb = v_ref[rows, :].astype(BF16)
            st = state[...]
            stb = st.astype(BF16)
            st_ref[c] = stb
            s = _dot(qb, kb, NT) * dmat
            o = _dot(s.astype(BF16), vb) + _dot(qb, stb) * xi
            state[...] = st * cd + _dot((kr * zeta).astype(BF16), vb, TN)
            o_ref[rows, :] = o
            mu = jnp.mean(o, axis=-1, keepdims=True)
            oc = o - mu
            var = jnp.mean(oc * oc, axis=-1, keepdims=True)
            n = oc * lax.rsqrt(var + GN_EPS) * gw
            gt = g_ref[rows, :]
            y_ref[rows, :] = (gt * _sigmoid(gt) * n).astype(BF16)
            return carry

        lax.fori_loop(0, nck, chunk, 0, unroll=True)

    return _host_call(
        body, name="ret_fwd", grid=(H, T // tb),
        in_specs=[lg_s, q_s, k_s, v_s, g_s, cs_s, cs_s, gw_s],
        out_specs=(row_v, row_v, st_s),
        out_shape=(jax.ShapeDtypeStruct((T, cf.V), BF16), jax.ShapeDtypeStruct((T, cf.V), F32),
                   jax.ShapeDtypeStruct((cf.NC, H, DK, DV), BF16)),
        scratch=[pltpu.VMEM((DK, DV), F32)], args=(lgam, proj, proj, proj, proj, cos, sin, gn_w),
        sem=("parallel", "arbitrary"), rider=rider)


def _ret_bwd(cf, proj, cos, sin, gn_w, lgam, o, states, dy, rider=None):
    T, L, H, DK, DV = cf.T, cf.L, cf.H, cf.DK, cf.DV
    tb = _tile(T, 512, L)
    nck = tb // L
    q_s, k_s, v_s, g_s, cs_s, gw_s, row_v, row_k, st_s, lg_s = _ret_specs(cf, tb, True)
    kscale = DK ** -0.5

    ni = T // tb
    n_steps = H * ni
    col0 = (0, cf.QK, 2 * cf.QK, 2 * cf.QK + cf.V)
    widths = (DK, DK, DV, DV)

    def body(lg_ref, q_ref, k_ref, v_ref, g_ref, cos_ref, sin_ref, gw_ref, o_ref, st_ref, dy_ref,
             dp_ref, dgw_ref, dstate, dq_s, dk_s, dv_s, dg_s, sems):
        h = pl.program_id(0)
        step = h * ni + pl.program_id(1)
        slot = step % 2
        dq_ref, dk_ref, dv_ref, dg_ref = dq_s.at[slot], dk_s.at[slot], dv_s.at[slot], dg_s.at[slot]

        def results_out(s):
            hh, ii = s // ni, s % ni
            rows = pl.ds(pl.multiple_of((ni - 1 - ii) * tb, tb), tb)
            return [pltpu.make_async_copy(
                buf.at[s % 2], dp_ref.at[rows, pl.ds(pl.multiple_of(c0 + hh * w, LANES), w)], sems.at[s % 2, k])
                for k, (buf, c0, w) in enumerate(zip((dq_s, dk_s, dv_s, dg_s), col0, widths))]

        @pl.when(step >= 2)
        def _():
            for cp in results_out(step - 2):
                cp.wait()

        @pl.when(pl.program_id(1) == 0)
        def _():
            dstate[...] = jnp.zeros_like(dstate)
            dgw_ref[...] = jnp.zeros_like(dgw_ref)

        dmat, xi, zeta, cd = _ret_consts(lg_ref[h], L)
        gw = gw_ref[...]

        def chunk(t, carry):
            c = nck - 1 - t
            rows = pl.ds(pl.multiple_of(c * L, L), L)
            cs, sn = cos_ref[rows, :], sin_ref[rows, :]
            qr = _rot(q_ref[rows, :], cs, sn)
            kr = _rot(k_ref[rows, :], cs, sn) * kscale
            qb, kb = qr.astype(BF16), kr.astype(BF16)
            kzb = (kr * zeta).astype(BF16)
            vb = v_ref[rows, :].astype(BF16)
            s = (_dot(qb, kb, NT) * dmat).astype(BF16)
            oo = o_ref[rows, :]
            mu = jnp.mean(oo, axis=-1, keepdims=True)
            oc = oo - mu
            rstd = lax.rsqrt(jnp.mean(oc * oc, axis=-1, keepdims=True) + GN_EPS)
            oh = oc * rstd
            gt = g_ref[rows, :]
            sg = _sigmoid(gt)
            dyv = dy_ref[rows, :]
            dn = dyv * (gt * sg)
            dg_ref[rows, :] = (dyv * (oh * gw) * (sg * (1.0 + gt * (1.0 - sg)))).astype(BF16)
            dgw_ref[...] += jnp.sum(dn * oh, axis=0, keepdims=True)
            doh = dn * gw
            do = rstd * (doh - jnp.mean(doh, axis=-1, keepdims=True) - oh * jnp.mean(doh * oh, axis=-1, keepdims=True))
            dob = do.astype(BF16)
            doxb = (do * xi).astype(BF16)
            dst = dstate[...]
            dstb = dst.astype(BF16)
            stb = st_ref[c]
            dv_ref[rows, :] = (_dot(s, dob, TN) + _dot(kzb, dstb)).astype(BF16)
            ds = (_dot(dob, vb, NT) * dmat).astype(BF16)
            dqr = _dot(ds, kb) + _dot(doxb, stb, NT)
            dkr = _dot(ds, qb, TN) + _dot(vb, dstb, NT) * zeta
            dstate[...] = dst * cd + _dot(qb, doxb, TN)
            dq_ref[rows, :] = _rot_bwd(dqr, cs, sn).astype(BF16)
            dk_ref[rows, :] = _rot_bwd(dkr * kscale, cs, sn).astype(BF16)
            return carry

        lax.fori_loop(0, nck, chunk, 0, unroll=True)
        for cp in results_out(step):
            cp.start()

        @pl.when(step == n_steps - 1)
        def _():
            if n_steps > 1:
                for cp in results_out(step - 1):
                    cp.wait()
            for cp in results_out(step):
                cp.wait()

    return _host_call(
        body, name="ret_bwd", grid=(H, ni),
        in_specs=[lg_s, q_s, k_s, v_s, g_s, cs_s, cs_s, gw_s, row_v, st_s, row_v],
        out_specs=(ANY, gw_s),
        out_shape=(jax.ShapeDtypeStruct((T, cf.RIN), BF16), jax.ShapeDtypeStruct((1, cf.V), F32)),
        scratch=[pltpu.VMEM((DK, DV), F32), pltpu.VMEM((2, tb, DK), BF16), pltpu.VMEM((2, tb, DK), BF16),
                 pltpu.VMEM((2, tb, DV), BF16), pltpu.VMEM((2, tb, DV), BF16), pltpu.SemaphoreType.DMA((2, 4))],
        args=(lgam, proj, proj, proj, proj, cos, sin, gn_w, o, states, dy),
        sem=("arbitrary", "arbitrary"), rider=rider)


def _conv_pre(x, halo, w, b, first, W):
    tb = x.shape[0]
    ext = jnp.concatenate([jnp.where(first, 0.0, halo), x], axis=0)
    out = b + w[W - 1:W, :] * x
    for tap in range(W - 1):
        out = out + w[tap:tap + 1, :] * pltpu.roll(ext, W - 1 - tap, 0)[SUBLANES:SUBLANES + tb, :]
    return out, ext


def _conv_fwd(cf, proj, conv_w, conv_b):
    T, CD, W = cf.T, cf.CD, SSD_CONV_W
    tb = _tile(T, 512, SUBLANES)
    tc = _tile(CD, 512, LANES)
    off = cf.DI // tc
    nh = tb // SUBLANES

    def body(x_ref, halo_ref, w_ref, b_ref, o_ref):
        pre, _ = _conv_pre(x_ref[...], halo_ref[...], w_ref[...], b_ref[...], pl.program_id(1) == 0, W)
        o_ref[...] = pre * _sigmoid(pre)

    return _pcall(
        body, name="conv_fwd", grid=(CD // tc, T // tb),
        in_specs=[pl.BlockSpec((tb, tc), lambda j, i: (i, off + j)),
                  pl.BlockSpec((SUBLANES, tc), lambda j, i: (jnp.maximum(i * nh - 1, 0), off + j)),
                  pl.BlockSpec((W, tc), lambda j, i: (0, j)), pl.BlockSpec((1, tc), lambda j, i: (0, j))],
        out_specs=pl.BlockSpec((tb, tc), lambda j, i: (i, j)),
        out_shape=jax.ShapeDtypeStruct((T, CD), F32),
        compiler_params=_cp(("parallel", "arbitrary")),
    )(proj, proj, conv_w, conv_b)


def _conv_bwd_pre(cf, proj, conv_w, conv_b, dact):
    T, CD, W = cf.T, cf.CD, SSD_CONV_W
    tb = _tile(T, 512, SUBLANES)
    tc = _tile(CD, 512, LANES)
    off = cf.DI // tc
    nh = tb // SUBLANES

    def body(x_ref, halo_ref, w_ref, b_ref, da_ref, dp_ref, dw_ref, db_ref):
        x = x_ref[...]
        pre, ext = _conv_pre(x, halo_ref[...], w_ref[...], b_ref[...], pl.program_id(1) == 0, W)
        sg = _sigmoid(pre)
        dp = da_ref[...] * (sg * (1.0 + pre * (1.0 - sg)))
        dp_ref[...] = dp
        rows = [jnp.sum(dp * pltpu.roll(ext, W - 1 - tap, 0)[SUBLANES:SUBLANES + tb, :], axis=0, keepdims=True)
                for tap in range(W - 1)]
        rows.append(jnp.sum(dp * x, axis=0, keepdims=True))
        dw = jnp.concatenate(rows, axis=0)
        db = jnp.sum(dp, axis=0, keepdims=True)

        @pl.when(pl.program_id(1) == 0)
        def _():
            dw_ref[...] = dw
            db_ref[...] = db

        @pl.when(pl.program_id(1) > 0)
        def _():
            dw_ref[...] += dw
            db_ref[...] += db

    return _pcall(
        body, name="conv_bwd_pre", grid=(CD // tc, T // tb),
        in_specs=[pl.BlockSpec((tb, tc), lambda j, i: (i, off + j)),
                  pl.BlockSpec((SUBLANES, tc), lambda j, i: (jnp.maximum(i * nh - 1, 0), off + j)),
                  pl.BlockSpec((W, tc), lambda j, i: (0, j)), pl.BlockSpec((1, tc), lambda j, i: (0, j)),
                  pl.BlockSpec((tb, tc), lambda j, i: (i, j))],
        out_specs=(pl.BlockSpec((tb, tc), lambda j, i: (i, j)), pl.BlockSpec((W, tc), lambda j, i: (0, j)),
                   pl.BlockSpec((1, tc), lambda j, i: (0, j))),
        out_shape=(jax.ShapeDtypeStruct((T, CD), F32), jax.ShapeDtypeStruct((W, CD), F32),
                   jax.ShapeDtypeStruct((1, CD), F32)),
        compiler_params=_cp(("parallel", "arbitrary")),
    )(proj, proj, conv_w, conv_b, dact)


def _conv_bwd_x(cf, dpre, conv_w, into):
    T, CD, W = cf.T, cf.CD, SSD_CONV_W
    tb = _tile(T, 512, SUBLANES)
    tc = _tile(CD, 512, LANES)
    off = cf.DI // tc
    nh = tb // SUBLANES
    last_blk = T // SUBLANES - 1
    ni = T // tb

    def body(d_ref, halo_ref, w_ref, into_ref, o_ref):
        d = d_ref[...]
        w = w_ref[...]
        nxt = jnp.where(pl.program_id(1) == ni - 1, 0.0, halo_ref[...])
        ext = jnp.concatenate([d, nxt], axis=0)
        n = tb + SUBLANES
        out = w[W - 1:W, :] * d
        for tap in range(W - 1):
            out = out + w[tap:tap + 1, :] * pltpu.roll(ext, n - (W - 1 - tap), 0)[:tb, :]
        o_ref[...] = out.astype(BF16)

    return _pcall(
        body, name="conv_bwd_x", grid=(CD // tc, ni),
        in_specs=[pl.BlockSpec((tb, tc), lambda j, i: (i, j)),
                  pl.BlockSpec((SUBLANES, tc), lambda j, i: (jnp.minimum((i + 1) * nh, last_blk), j)),
                  pl.BlockSpec((W, tc), lambda j, i: (0, j)), ANY],
        out_specs=pl.BlockSpec((tb, tc), lambda j, i: (i, off + j)),
        out_shape=jax.ShapeDtypeStruct(into.shape, BF16), input_output_aliases={3: 0},
        compiler_params=_cp(("parallel", "arbitrary")),
    )(dpre, dpre, conv_w, into)


def _ssd_masks(cf, g, tb):
    L, GW, P, RG = cf.L, cf.GW, cf.P, cf.RG
    assert L == P and 2 * L == LANES and RG % 2 == 0
    i32 = jnp.int32
    hrow = lax.broadcasted_iota(i32, (LANES, GW), 0)
    hcol = lax.broadcasted_iota(i32, (LANES, GW), 1) // P
    expand = (hrow == g * RG + hcol).astype(BF16)
    ti = lax.broadcasted_iota(i32, (LANES, LANES), 0)
    tj = lax.broadcasted_iota(i32, (LANES, LANES), 1)
    btril = jnp.logical_and(ti // L == tj // L, ti >= tj).astype(BF16)
    r0 = lax.broadcasted_iota(i32, (L, GW), 0)
    c0 = lax.broadcasted_iota(i32, (L, GW), 1) % L
    tile_eye = (r0 == c0).astype(F32)
    lower = r0 >= c0
    p0 = lax.broadcasted_iota(i32, (2 * L, LANES), 0) // L
    p1 = lax.broadcasted_iota(i32, (2 * L, LANES), 1) // P
    pair = (p0 == p1).astype(F32)
    return expand, btril, tile_eye, lower, pair


def _softplus(x):
    return jnp.maximum(x, 0.0) + jnp.log1p(jnp.exp(-jnp.abs(x)))


def _split3(x):
    hi = x.astype(BF16)
    r1 = x - hi.astype(F32)
    mid = r1.astype(BF16)
    return hi, mid, (r1 - mid.astype(F32)).astype(BF16)


def _chunk_sums(btril, x, dn):
    hi, mid, lo = _split3(x)
    outs = []
    for k in range(x.shape[0] // LANES):
        sl = slice(k * LANES, (k + 1) * LANES)
        outs.append((_dot(btril, lo[sl], dn) + _dot(btril, mid[sl], dn)) + _dot(btril, hi[sl], dn))
    return jnp.concatenate(outs, axis=0)


def _expand_heads(x, expand, dn):
    hi, mid, lo = _split3(x)
    return (_dot(lo, expand, dn) + _dot(mid, expand, dn)) + _dot(hi, expand, dn)


def _ssd_chunk(cf, mk, acum, dt, xs, bm, cm):
    _, _, tile_eye, lower, pair = mk
    rowv = jnp.sum(acum * tile_eye, axis=0, keepdims=True)
    lf = jnp.exp(jnp.where(lower, acum - rowv, -1e30))
    xdt = xs * dt
    bb, cb_ = bm.astype(BF16), cm.astype(BF16)
    bb2 = jnp.concatenate([bb, bb], axis=0)
    cb2 = _dot(cb_, bb2, NT)
    ms, bds = [], []
    for j in range(cf.RG // 2):
        ln = slice(j * LANES, (j + 1) * LANES)
        ms.append((cb2 * lf[:, ln]).astype(BF16))
        xp = xdt[:, ln]
        bds.append((jnp.concatenate([xp, xp], axis=0) * pair).astype(BF16))
    return lf, xdt, bb, cb_, bb2, cb2, ms, bds


def _ssd_specs(cf, tb, rev):
    G, GW, N = cf.G, cf.GW, cf.N
    ni = cf.T // tb
    ri = (lambda i: ni - 1 - i) if rev else (lambda i: i)
    z = pl.BlockSpec((tb, GW), lambda g, i: (ri(i), g))
    dt = pl.BlockSpec((tb, LANES), lambda g, i: (ri(i), (cf.DI + cf.CD) // LANES))
    xs = pl.BlockSpec((tb, GW), lambda g, i: (ri(i), g))
    bm = pl.BlockSpec((tb, N), lambda g, i: (ri(i), cf.DI // N + g))
    cm = pl.BlockSpec((tb, N), lambda g, i: (ri(i), cf.DI // N + G + g))
    vec = pl.BlockSpec((1, GW), lambda g, i: (0, g))
    st = pl.BlockSpec((tb // cf.L, None, N, GW), lambda g, i: (ri(i), g, 0, 0))
    return z, dt, xs, bm, cm, vec, st


def _ssd_fwd(cf, proj, xact, bias_e, alog_e, dskip_e, norm_w):
    T, L, G, GW, N = cf.T, cf.L, cf.G, cf.GW, cf.N
    tb = _tile(T, 512, L)
    nck = tb // L
    z_s, dt_s, xs_s, b_s, c_s, vec_s, st_s = _ssd_specs(cf, tb, False)

    def body(z_ref, dt_ref, xs_ref, b_ref, c_ref, bias_ref, alog_ref, dsk_ref, nw_ref, y_ref, yp_ref, st_ref,
             state, dt_s, ac_s):
        @pl.when(pl.program_id(1) == 0)
        def _():
            state[...] = jnp.zeros_like(state)

        mk = _ssd_masks(cf, pl.program_id(0), tb)
        a_e = -jnp.exp(alog_ref[...])
        dt_all = _softplus(_expand_heads(dt_ref[...], mk[0], NN) + bias_ref[...])
        dt_s[...] = dt_all
        ac_s[...] = _chunk_sums(mk[1], dt_all * a_e, NN)

        def chunk(c, carry):
            rows = pl.ds(pl.multiple_of(c * L, L), L)
            acum = ac_s[rows, :]
            lf, xdt, bb, cb_, _, _, ms, bds = _ssd_chunk(cf, mk, acum, dt_s[rows, :], xs_ref[rows, :],
                                                         b_ref[rows, :], c_ref[rows, :])
            st = state[...]
            stb = st.astype(BF16)
            st_ref[c] = stb
            ydiag = jnp.concatenate([_dot(m, bd) for m, bd in zip(ms, bds)], axis=1)
            al = acum[L - 1:L, :]
            state[...] = st * jnp.exp(al) + _dot(bb, (xdt * jnp.exp(al - acum)).astype(BF16), TN)
            yp_ref[rows, :] = ydiag + _dot(cb_, stb) * jnp.exp(acum)
            return carry

        lax.fori_loop(0, nck, chunk, 0, unroll=True)
        z = z_ref[...]
        yg = (yp_ref[...] + dsk_ref[...] * xs_ref[...]) * (z * _sigmoid(z))
        y_ref[...] = (yg * _rstd(yg) * nw_ref[...]).astype(BF16)

    return _pcall(
        body, name="ssd_fwd", grid=(G, T // tb),
        in_specs=[z_s, dt_s, xs_s, b_s, c_s, vec_s, vec_s, vec_s, vec_s],
        out_specs=(z_s, z_s, st_s),
        out_shape=(jax.ShapeDtypeStruct((T, cf.DI), BF16), jax.ShapeDtypeStruct((T, cf.DI), F32),
                   jax.ShapeDtypeStruct((cf.NC, G, N, GW), BF16)),
        scratch_shapes=[pltpu.VMEM((N, GW), F32), pltpu.VMEM((tb, GW), F32), pltpu.VMEM((tb, GW), F32)],
        compiler_params=_cp(("parallel", "arbitrary")),
    )(proj, proj, xact, xact, xact, bias_e, alog_e, dskip_e, norm_w)


def _ssd_bwd(cf, proj, xact, bias_e, alog_e, dskip_e, norm_w, ypre, states, dy, rider=None):
    T, L, G, GW, N, RG = cf.T, cf.L, cf.G, cf.GW, cf.N, cf.RG
    tb = _tile(T, 512, L)
    nck = tb // L
    ni = T // tb
    z_s, dt_s, xs_s, b_s, c_s, vec_s, st_s = _ssd_specs(cf, tb, True)
    bc_out = pl.BlockSpec((tb, N), lambda g, i: (ni - 1 - i, g))
    ddt_out = pl.BlockSpec((None, tb, LANES), lambda g, i: (g, ni - 1 - i, 0))

    def body(z_ref, dt_ref, xs_ref, b_ref, c_ref, bias_ref, alog_ref, dsk_ref, nw_ref, yp_ref, st_ref, dy_ref,
             dz_ref, dxs_ref, db_ref, dc_ref, ddt_ref, dnw_ref, ddsk_ref, dalog_ref, dbias_ref,
             dstate, dt_s, ac_s, sg_s, dys_s, dxdt_s, dac_s):
        @pl.when(pl.program_id(1) == 0)
        def _():
            dstate[...] = jnp.zeros_like(dstate)
            for r in (dnw_ref, ddsk_ref, dalog_ref, dbias_ref):
                r[...] = jnp.zeros_like(r)

        mk = _ssd_masks(cf, pl.program_id(0), tb)
        expand, btril, tile_eye, lower, pair = mk
        a_e = -jnp.exp(alog_ref[...])
        dsk, nw = dsk_ref[...], nw_ref[...]
        last_row = (lax.broadcasted_iota(jnp.int32, (L, 1), 0) == L - 1).astype(F32)
        raw = _expand_heads(dt_ref[...], expand, NN) + bias_ref[...]
        dt_all = _softplus(raw)
        dt_s[...] = dt_all
        sg_s[...] = _sigmoid(raw)
        ac_s[...] = _chunk_sums(btril, dt_all * a_e, NN)
        z = z_ref[...]
        sz = _sigmoid(z)
        silu = z * sz
        xs_all = xs_ref[...]
        yd = yp_ref[...] + dsk * xs_all
        yg = yd * silu
        rr = _rstd(yg)
        xh = yg * rr
        dout = dy_ref[...]
        dnw_ref[...] += jnp.sum(dout * xh, axis=0, keepdims=True)
        dxh = dout * nw
        dyg = rr * (dxh - xh * jnp.mean(dxh * xh, axis=-1, keepdims=True))
        dz_ref[...] = (dyg * yd * (sz * (1.0 + z * (1.0 - sz)))).astype(BF16)
        dys_all = dyg * silu
        dys_s[...] = dys_all
        ddsk_ref[...] += jnp.sum(dys_all * xs_all, axis=0, keepdims=True)

        def chunk(t, carry):
            c = nck - 1 - t
            rows = pl.ds(pl.multiple_of(c * L, L), L)
            acum = ac_s[rows, :]
            lf, xdt, bb, cb_, bb2, cb2, ms, bds = _ssd_chunk(cf, mk, acum, dt_s[rows, :], xs_ref[rows, :],
                                                             b_ref[rows, :], c_ref[rows, :])
            stb = st_ref[c]
            eac = jnp.exp(acum)
            al = acum[L - 1:L, :]
            eal = jnp.exp(al)
            dte = jnp.exp(al - acum)
            dys = dys_s[rows, :]
            dyb = dys.astype(BF16)
            dms, dxs_, dsegs = [], [], []
            dcb2 = None
            for j in range(RG // 2):
                ln = slice(j * LANES, (j + 1) * LANES)
                dyj = dyb[:, ln]
                dbd = _dot(ms[j], dyj, TN) * pair
                dxs_.append(dbd[:L, :] + dbd[L:, :])
                tj = _dot(dyj, bds[j], NT) * lf[:, ln]
                dcb2 = tj if dcb2 is None else dcb2 + tj
                dsegs.append(tj * cb2)
            dxdt = jnp.concatenate(dxs_, axis=1)
            dseg = jnp.concatenate(dsegs, axis=1)
            dcb2 = dcb2.astype(BF16)
            dcm = _dot(dcb2, bb2)
            dbm2 = _dot(dcb2, cb_, TN)
            dbm = dbm2[:L, :] + dbm2[L:, :]
            dacum = dseg - tile_eye * jnp.sum(dseg, axis=0, keepdims=True)
            dyo = (dys * eac).astype(BF16)
            dcm = dcm + _dot(dyo, stb, NT)
            dacum = dacum + dys * _dot(cb_, stb) * eac
            dst = dstate[...]
            dstb = dst.astype(BF16)
            xd = xdt * dte
            dbm = dbm + _dot(xd.astype(BF16), dstb, NT)
            dxd = _dot(bb, dstb)
            dal = jnp.sum(dst * stb.astype(F32), axis=0, keepdims=True) * eal
            dxdt = dxdt + dxd * dte
            tt = dxd * xd
            dacum = dacum - tt + last_row * (dal + jnp.sum(tt, axis=0, keepdims=True))
            dstate[...] = dst * eal + _dot(cb_, dyo, TN)
            dxdt_s[rows, :] = dxdt
            dac_s[rows, :] = dacum
            db_ref[rows, :] = dbm
            dc_ref[rows, :] = dcm
            return carry

        lax.fori_loop(0, nck, chunk, 0, unroll=True)
        dda = _chunk_sums(btril, dac_s[...], TN)
        dxdt_all = dxdt_s[...]
        dt_all = dt_s[...]
        dxs_ref[...] = dys_s[...] * dsk + dxdt_all * dt_all
        ddt = dxdt_all * xs_ref[...] + dda * a_e
        dalog_ref[...] += jnp.sum(dda * dt_all, axis=0, keepdims=True) * a_e
        draw = ddt * sg_s[...]
        dbias_ref[...] += jnp.sum(draw, axis=0, keepdims=True)
        ddt_ref[...] = _expand_heads(draw, expand, NT)

    GN = G * N
    return _host_call(
        body, name="ssd_bwd", grid=(G, ni),
        in_specs=[z_s, dt_s, xs_s, b_s, c_s, vec_s, vec_s, vec_s, vec_s, z_s, st_s, z_s],
        out_specs=(z_s, z_s, bc_out, bc_out, ddt_out, vec_s, vec_s, vec_s, vec_s),
        out_shape=(jax.ShapeDtypeStruct((T, cf.SINP), BF16), jax.ShapeDtypeStruct((T, cf.CD), F32),
                   jax.ShapeDtypeStruct((T, GN), F32), jax.ShapeDtypeStruct((T, GN), F32),
                   jax.ShapeDtypeStruct((G, T, LANES), F32)) + (jax.ShapeDtypeStruct((1, cf.DI), F32),) * 4,
        scratch=[pltpu.VMEM((N, GW), F32)] + [pltpu.VMEM((tb, GW), F32)] * 6,
        args=(proj, proj, xact, xact, xact, bias_e, alog_e, dskip_e, norm_w, ypre, states, dy),
        sem=("parallel", "arbitrary"), rider=rider)


def _sum_groups(parts, name, into, col_blk):
    G, T, W = parts.shape
    tb = _tile(T, 512, SUBLANES)

    def body(p_ref, into_ref, o_ref):
        acc = p_ref[0]
        for g in range(1, G):
            acc = acc + p_ref[g]
        o_ref[...] = acc.astype(BF16)

    return _pcall(body, name=name, grid=(T // tb,), in_specs=[pl.BlockSpec((G, tb, W), lambda i: (0, i, 0)), ANY],
                  out_specs=pl.BlockSpec((tb, W), lambda i: (i, col_blk)),
                  out_shape=jax.ShapeDtypeStruct(into.shape, BF16), input_output_aliases={1: 0},
                  compiler_params=_cp(("parallel",)))(parts, into)


def _fill_bc(dact, dbm, dcm, name):
    T, GN = dbm.shape
    tb = _tile(T, 512, SUBLANES)
    blk = (dact.shape[1] - 2 * GN) // (2 * GN)
    assert blk * 2 * GN == dact.shape[1] - 2 * GN

    def body(b_ref, c_ref, into_ref, o_ref):
        o_ref[:, :GN] = b_ref[...]
        o_ref[:, GN:] = c_ref[...]

    row = pl.BlockSpec((tb, GN), lambda i: (i, 0))
    return _pcall(body, name=name, grid=(T // tb,), in_specs=[row, row, ANY],
                  out_specs=pl.BlockSpec((tb, 2 * GN), lambda i: (i, blk)),
                  out_shape=jax.ShapeDtypeStruct(dact.shape, F32), input_output_aliases={2: 0},
                  compiler_params=_cp(("parallel",)))(dbm, dcm, dact)


def _mm_gather_nt(a, w_shard, order, name, ti=1024, tj=512, late_rider=None):
    I, R = a.shape
    n = w_shard.shape[0]
    ti = _tile(I, ti, SUBLANES)
    tj = _tile(n, tj, LANES)
    per, ni = n // tj, I // ti
    n_tiles = N_DEV * per
    lr = late_rider if late_rider is not None else _rider_join([])
    n_ra, n_ro = len(lr.args), len(lr.out_shape)

    def body(ord_ref, a_ref, w_ref, *refs):
        rins, (out_ref, gath_ref), routs = refs[:n_ra], refs[n_ra:n_ra + 2], refs[n_ra + 2:n_ra + 2 + n_ro]
        bbuf, tile_sems, send_sems, recv_sems, local_sem = refs[n_ra + 2 + n_ro:n_ra + 7 + n_ro]
        rsems = refs[n_ra + 7 + n_ro:]
        k, t, i = pl.program_id(0), pl.program_id(1), pl.program_id(2)
        x_, y_, c_ = _mesh_pos()
        me = 4 * x_ + 2 * y_ + c_
        sib = (x_, y_, 1 - c_)
        chips = [(1 - x_, y_), (x_, 1 - y_), (1 - x_, 1 - y_)]
        near = [sib] + [(cx, cy, c_) for cx, cy in chips]

        def send_mine(q):
            return _remote(w_ref, gath_ref.at[me], send_sems, recv_sems, q, near[q])

        def from_near(q):
            px, py, pc = near[q]
            return _remote(w_ref, gath_ref.at[4 * px + 2 * py + pc], send_sems, recv_sems, q, near[q])

        def pass_on(j, core):
            rows = gath_ref.at[4 * chips[j][0] + 2 * chips[j][1] + core]
            return _remote(rows, rows, send_sems, recv_sems, 4 + j, sib)

        def tile_in(s):
            rows = pl.ds(pl.multiple_of((s % per) * tj, tj), tj)
            return pltpu.make_async_copy(gath_ref.at[ord_ref[s // per], rows, :], bbuf.at[s % 2], tile_sems.at[s % 2])

        mine = pltpu.make_async_copy(w_ref, gath_ref.at[me], local_sem)
        s = k * per + t
        first_row = i == 0

        @pl.when(jnp.logical_and(s == 0, first_row))
        def _():
            mine.start()
            for q in range(3):
                send_mine(q).start()
            mine.wait()
            tile_in(0).start()

        @pl.when(first_row)
        def _():
            tile_in(s).wait()

            @pl.when(s + 1 < n_tiles)
            def _():
                new_block = (s + 1) % per == 0
                kk = (s + 1) // per

                @pl.when(jnp.logical_and(new_block, kk == 1))
                def _():
                    from_near(0).wait_recv()

                for j in range(3):
                    @pl.when(jnp.logical_and(new_block, kk == 2 + j))
                    def _(j=j):
                        from_near(1 + j).wait_recv()
                        pass_on(j, c_).start()
                        if j == 0:
                            send_mine(3).start()

                    @pl.when(jnp.logical_and(new_block, kk == 5 + j))
                    def _(j=j):
                        pass_on(j, 1 - c_).wait_recv()

                tile_in(s + 1).start()

        if n_ra:
            @pl.when(jnp.logical_and(s == 5 * per, first_row))
            def _():
                _rider_start(lr, rins, routs, rsems)

        out_ref[...] = _dot(a_ref[...], bbuf[s % 2], NT)

        @pl.when(jnp.logical_and(s == n_tiles - 1, i == ni - 1))
        def _():
            for q in range(4):
                send_mine(q).wait_send()
            for j in range(3):
                pass_on(j, c_).wait_send()
            if n_ra:
                _rider_wait(lr, rins, routs, rsems)

    res = _pcall(
        body, name=name,
        grid_spec=pltpu.PrefetchScalarGridSpec(
            num_scalar_prefetch=1, grid=(N_DEV, per, ni),
            in_specs=[pl.BlockSpec((ti, R), lambda k, t, i, o: (i, 0)), ANY] + [ANY] * n_ra,
            out_specs=[pl.BlockSpec((ti, tj), lambda k, t, i, o: (i, o[k] * per + t)), ANY] + [ANY] * n_ro,
            scratch_shapes=[pltpu.VMEM((2, tj, R), BF16), pltpu.SemaphoreType.DMA((2,)), pltpu.SemaphoreType.DMA((7,)),
                            pltpu.SemaphoreType.DMA((7,)), pltpu.SemaphoreType.DMA] + lr.sems),
        out_shape=[jax.ShapeDtypeStruct((I, N_DEV * n), F32), jax.ShapeDtypeStruct((N_DEV, n, R), BF16)] + lr.out_shape,
        input_output_aliases={3 + q: 2 + v for q, v in lr.aliases.items()},
        compiler_params=_cp(("arbitrary",) * 3),
    )(order, a, w_shard, *lr.args)
    return res[0], res[1], list(res[2:])


def _all_gather(x, name):
    def body(x_ref, out_ref, send_sems, recv_sems, local_sem):
        x, y, c = lax.axis_index("x"), lax.axis_index("y"), lax.axis_index("c")
        me, sibling = (x, y, c), (x, y, 1 - c)
        chips = [(1 - x, y), (x, 1 - y), (1 - x, 1 - y)]

        def blk(px, py, pc):
            return out_ref.at[4 * px + 2 * py + pc]

        def copy(k, block, to, src=None):
            return pltpu.make_async_remote_copy(
                src_ref=blk(*block) if src is None else src, dst_ref=blk(*block),
                send_sem=send_sems.at[k], recv_sem=recv_sems.at[k], device_id=to, device_id_type=MESH)

        mine = pltpu.make_async_copy(x_ref, blk(*me), local_sem)
        mine.start()
        first = [copy(0, me, sibling, src=x_ref)]
        first += [copy(1 + j, me, (*chip, c), src=x_ref) for j, chip in enumerate(chips)]
        for cp in first:
            cp.start()
        passed = [copy(4 + j, (*chip, c), sibling) for j, chip in enumerate(chips)]
        for j, chip in enumerate(chips):
            copy(1 + j, (*chip, c), me).wait_recv()
            passed[j].start()
        copy(0, sibling, me).wait_recv()
        for j, chip in enumerate(chips):
            copy(4 + j, (*chip, 1 - c), me).wait_recv()
        for cp in first + passed:
            cp.wait_send()
        mine.wait()

    return _pcall(
        body, name=name, in_specs=[ANY], out_specs=ANY,
        out_shape=jax.ShapeDtypeStruct((N_DEV,) + x.shape, x.dtype),
        scratch_shapes=[pltpu.SemaphoreType.DMA((7,)), pltpu.SemaphoreType.DMA((7,)), pltpu.SemaphoreType.DMA],
    )(x)


def _core_sum(g, recv, idx, name):
    _, _, a, b = g.shape
    n, blk, at = _plane_tiles(a, b, 2 * 2**20)

    def body(idx_ref, g_ref, r_ref, p_ref, own_ref):
        s = g_ref[...] + r_ref[...]
        p_ref[...] = s.astype(BF16)

        @pl.when(pl.program_id(1) == idx_ref[1])
        def _():
            own_ref[...] = s

    return _pcall(
        body, name=name,
        grid_spec=pltpu.PrefetchScalarGridSpec(
            num_scalar_prefetch=1, grid=(n, 4),
            in_specs=[pl.BlockSpec((None, None) + blk, lambda i, q, idx: (q, idx[0]) + at(i)),
                      pl.BlockSpec((None,) + blk, lambda i, q, idx: (q,) + at(i))],
            out_specs=(pl.BlockSpec((None,) + blk, lambda i, q, idx: (q,) + at(i)),
                       pl.BlockSpec(blk, lambda i, q, idx: at(i)))),
        out_shape=(jax.ShapeDtypeStruct((4, a, b), BF16), jax.ShapeDtypeStruct((a, b), F32)),
        compiler_params=_cp(("parallel", "arbitrary")),
    )(idx, g, recv)


def _plane_tiles(a, b, f32_bytes, row_off=0):
    if a % (2 * SUBLANES) == 0:
        tr = _tile(a, max(2 * SUBLANES, f32_bytes // (4 * b) // (2 * SUBLANES) * (2 * SUBLANES)), 2 * SUBLANES)
        return a // tr, (tr, b), lambda i: (i + row_off * (a // tr), 0)
    assert row_off == 0
    tc = _tile(b, max(LANES, f32_bytes // (4 * a) // LANES * LANES), LANES)
    return b // tc, (a, tc), lambda i: (0, i)


def _adam_math(w, g, m, v):
    m = ADAM_B1 * m + (1.0 - ADAM_B1) * g
    v = ADAM_B2 * v + (1.0 - ADAM_B2) * (g * g)
    m_hat = m / (1.0 - ADAM_B1 ** ADAM_STEP)
    v_hat = v / (1.0 - ADAM_B2 ** ADAM_STEP)
    delta = -ADAM_LR * (m_hat / (jnp.sqrt(v_hat) + ADAM_EPS) + ADAM_WD * w)
    return delta, m, v


def _chip_sum_adam(own, recv, w, m, v, layer, name, into=None, row_part=(0, 1)):
    a, b = own.shape
    n = w.shape[0]
    assert w.shape[1] == a * row_part[1]
    nt, blk, at = _plane_tiles(a, b, 2**20)
    _, _, at_w = _plane_tiles(a, b, 2**20, row_part[0])
    wspec = pl.BlockSpec((None,) + blk, lambda i: (layer,) + at_w(i))
    ospec = pl.BlockSpec(blk, at)
    n_into = 0 if into is None else 4

    def body(own_ref, r_ref, w_ref, m_ref, v_ref, *rest):
        g_ref, d_ref, mo_ref, vo_ref = rest[n_into:]
        g = own_ref[...]
        for k in range(3):
            g = g + r_ref[k].astype(F32)
        g_ref[...] = g
        d_ref[...], mo_ref[...], vo_ref[...] = _adam_math(w_ref[...], g, m_ref[...], v_ref[...])

    return _pcall(
        body, name=name, grid=(nt,),
        in_specs=[ospec, pl.BlockSpec((3,) + blk, lambda i: (0,) + at(i)), wspec, wspec, wspec] + [ANY] * n_into,
        out_specs=(wspec,) * 4, out_shape=(jax.ShapeDtypeStruct(w.shape, F32),) * 4,
        input_output_aliases={5 + k: k for k in range(n_into)},
        compiler_params=_cp(("parallel",)),
    )(own, recv, w, m, v, *(into or ()))


def _all_reduce_small(x, n_fold, fold_w, name):
    R, W = x.shape

    def body(x_ref, out_ref, buf, send_sems, recv_sems):
        xx, y, c = lax.axis_index("x"), lax.axis_index("y"), lax.axis_index("c")
        me = 4 * xx + 2 * y + c
        buf[me] = x_ref[...]
        copies = []
        for k in range(1, N_DEV):
            px, py, pc = xx ^ (k >> 2), y ^ ((k >> 1) & 1), c ^ (k & 1)
            copies.append(pltpu.make_async_remote_copy(
                src_ref=x_ref, dst_ref=buf.at[me], send_sem=send_sems.at[k - 1], recv_sem=recv_sems.at[k - 1],
                device_id=(px, py, pc), device_id_type=MESH))
        for cp in copies:
            cp.start()
        for cp in copies:
            cp.wait()
        acc = buf[0]
        for j in range(1, N_DEV):
            acc = acc + buf[j]
        out_ref[...] = acc
        if n_fold:
            l0 = lax.broadcasted_iota(jnp.int32, (W, W), 0) // fold_w
            l1 = lax.broadcasted_iota(jnp.int32, (W, W), 1)
            fold = (l0 == l1).astype(F32)
            out_ref[R - n_fold:, :] = _dot(acc[R - n_fold:, :], fold, NN, HI)

    return _pcall(
        body, name=name, in_specs=[pl.BlockSpec(memory_space=pltpu.VMEM)],
        out_specs=pl.BlockSpec(memory_space=pltpu.VMEM), out_shape=jax.ShapeDtypeStruct((R, W), F32),
        scratch_shapes=[pltpu.VMEM((N_DEV, R, W), F32), pltpu.SemaphoreType.DMA((N_DEV - 1,)),
                        pltpu.SemaphoreType.DMA((N_DEV - 1,))],
        compiler_params=pltpu.CompilerParams(vmem_limit_bytes=VMEM_LIMIT),
    )(x)


def _adam_small(w, g, m, v, name):
    def body(w_ref, g_ref, m_ref, v_ref, d_ref, mo_ref, vo_ref):
        d_ref[...], mo_ref[...], vo_ref[...] = _adam_math(w_ref[...], g_ref[...], m_ref[...], v_ref[...])

    return _pcall(body, name=name, out_shape=(jax.ShapeDtypeStruct(w.shape, F32),) * 3)(w, g, m, v)


def kernel(x, norm_mix_pre, norm_mix_post, norm_ffn_pre, norm_ffn_post, ret_w_in, ret_gn_w, ret_w_out, ssd_w_in, ssd_conv_w, ssd_conv_b, ssd_dt_bias, ssd_a_log, ssd_d, ssd_norm_w, ssd_w_out, mlp_w_up, mlp_w_down, loss_target, m_norm_mix_pre, m_norm_mix_post, m_norm_ffn_pre, m_norm_ffn_post, m_ret_w_in, m_ret_gn_w, m_ret_w_out, m_ssd_w_in, m_ssd_conv_w, m_ssd_conv_b, m_ssd_dt_bias, m_ssd_a_log, m_ssd_d, m_ssd_norm_w, m_ssd_w_out, m_mlp_w_up, m_mlp_w_down, v_norm_mix_pre, v_norm_mix_post, v_norm_ffn_pre, v_norm_ffn_post, v_ret_w_in, v_ret_gn_w, v_ret_w_out, v_ssd_w_in, v_ssd_conv_w, v_ssd_conv_b, v_ssd_dt_bias, v_ssd_a_log, v_ssd_d, v_ssd_norm_w, v_ssd_w_out, v_mlp_w_up, v_mlp_w_down):
    cf = _cfg()
    T, D = cf.T, cf.D
    ax, ay, ac = lax.axis_index("x"), lax.axis_index("y"), lax.axis_index("c")
    my_dev = 4 * ax + 2 * ay + ac
    idx = jnp.stack([ac, 2 * ax + ay]).astype(jnp.int32)

    tr12 = lambda t: jnp.swapaxes(t, 1, 2)
    chips_of = lambda ws: _rider_join([_r_gather_chips(w.astype(BF16)) for w in ws])
    cores_of = lambda bufs: _rider_join([_r_gather_cores(b) for b in bufs])
    cw, cbw, nww = cf.CD // N_DEV, cf.CD // N_DEV, cf.DI // N_DEV
    small = jnp.concatenate([ssd_conv_w[0], ssd_conv_b, jnp.pad(ssd_norm_w, ((0, 0), (0, cw - nww))),
                             jnp.zeros((2, cw), F32)], axis=0)
    small = _all_gather(small, "ag_ssd_small")
    conv_w = jnp.transpose(small[:, :SSD_CONV_W, :], (1, 0, 2)).reshape(SSD_CONV_W, cf.CD)
    conv_b = small[:, SSD_CONV_W, :].reshape(1, cf.CD)
    ssd_nw = small[:, SSD_CONV_W + 1, :nww].reshape(1, cf.DI)

    half = cf.DK // 2
    inv_freq = ROPE_BASE ** (-jnp.arange(half, dtype=F32) / half)
    ang = jnp.arange(T).astype(F32)[:, None] * inv_freq[None, :]
    cos, sin = jnp.cos(ang), jnp.sin(ang)
    lgam = jnp.log1p(-jnp.exp2(-5.0 - jnp.arange(cf.H, dtype=F32)))
    rep = lambda p: jnp.repeat(p.reshape(1, cf.SH), cf.P, axis=1)
    bias_e, alog_e, dskip_e = rep(ssd_dt_bias), rep(ssd_a_log), rep(ssd_d)

    h0 = x.reshape(T, D)
    tgt = loss_target.reshape(T, D)
    nrm = lambda p, i: p[i:i + 1]

    u0 = _rms_fwd(h0, nrm(norm_mix_pre, 0), "rms_fwd0")
    order = jnp.stack([my_dev ^ mask for mask in (0, 1, 4, 2, 6, 5, 3, 7)]).astype(jnp.int32)
    proj0, w_ri, part_a = _mm_gather_nt(u0, ret_w_in[0].T.astype(BF16), order, "mm_ret_in",
                                        late_rider=chips_of([ret_w_out[0]]))
    w_ri = w_ri.reshape(cf.RIN, D)
    (y0, o0, st0), got = _ret_fwd(cf, proj0, cos, sin, ret_gn_w, lgam, rider=_rider_join(
        [cores_of(part_a), chips_of([mlp_w_up[0].T, mlp_w_down[0]])]))
    w_ro = got[0].reshape(cf.V, D)
    m0, got = _mm(y0, w_ro, kind="nn", name="mm_ret_out", tr=cf.V, rider=cores_of(got[1:]))
    w_up0, w_dn0 = got[0].reshape(cf.FF, D), got[1].reshape(cf.FF, D)
    h1, u1 = _resid_fwd(h0, m0, nrm(norm_mix_post, 0), nrm(norm_ffn_pre, 0), "resid_fwd0")
    (sq0, act0), part_b = _mm(u1, w_up0, kind="nt", name="mm_up0", epi="relu2", rider=chips_of([tr12(ssd_w_in)[0]]))
    f0, got = _mm(sq0, w_dn0, kind="nn", name="mm_down0", tj=1024, rider=_rider_join(
        [cores_of(part_b), chips_of([ssd_w_out[0], mlp_w_up[1].T])]))
    w_si = jnp.pad(got[0].reshape(cf.SIN, D), ((0, cf.SINP - cf.SIN), (0, 0)))
    h2, u2 = _resid_fwd(h1, f0, nrm(norm_ffn_post, 0), nrm(norm_mix_pre, 1), "resid_fwd1")
    proj1, got = _mm(u2, w_si, kind="nt", name="mm_ssd_in", tj=1152, rider=_rider_join(
        [cores_of(got[1:]), chips_of([mlp_w_down[1]])]))
    w_so, w_up1 = got[0].reshape(cf.DI, D), got[1].reshape(cf.FF, D)
    xact = _conv_fwd(cf, proj1, conv_w, conv_b)
    y1, yp1, st1 = _ssd_fwd(cf, proj1, xact, bias_e, alog_e, dskip_e, ssd_nw)
    m1, (w_dn1,) = _mm(y1, w_so, kind="nn", name="mm_ssd_out", tr=cf.DI, rider=cores_of(got[2:]))
    w_up, w_dn = [w_up0, w_up1], [w_dn0, w_dn1.reshape(cf.FF, D)]
    h3, u3 = _resid_fwd(h2, m1, nrm(norm_mix_post, 1), nrm(norm_ffn_pre, 1), "resid_fwd2")
    sq1, act1 = _mm(u3, w_up[1], kind="nt", name="mm_up1", epi="relu2")
    f1 = _mm(sq1, w_dn[1], kind="nn", name="mm_down1", tj=1024)
    g4, lsum = _final_fwd(h3, f1, nrm(norm_ffn_post, 1), tgt, "final_fwd")
    loss = lax.psum(0.5 * jnp.sum(lsum) / D, ("x", "y", "c"))

    as4 = lambda g: g.reshape(4, 2, g.shape[1], g.shape[2])
    swap_cores = lambda g: _r_swap_cores(as4(g))
    core_sum = lambda g, recv, name: _core_sum(as4(g), recv, idx, name + "_core_sum")

    def mlp_bwd(l, df, u, sq, act, rider=None, then=None):
        dpre = _mm(df, w_dn[l], kind="nt", name=f"mm_dpre{l}", out_dtype=BF16, epi="mul2act", epi_in=act, rider=rider)
        dpre, got = dpre if rider is not None else (dpre, [])
        g_dn = _mm(sq, df, kind="tn", name=f"mm_gdown{l}", tj=1024, tr=4096, rider=then(got) if then else None)
        g_dn, got = g_dn if then else (g_dn, [])
        g_dn = g_dn.reshape(N_DEV, cf.FF // N_DEV, D)
        du, (rc,) = _mm(dpre, w_up[l], kind="nn", name=f"mm_du_mlp{l}", tj=1024, tr=4096, rider=swap_cores(g_dn))
        part, own = core_sum(g_dn, rc, f"rs_mlp_down{l}")
        g_up, (r2,) = _mm(u, dpre, kind="tn", name=f"mm_gup{l}", tj=1024, tr=4096, out_nblk=N_DEV, rider=_r_swap_chips(part))
        return du, g_up, own, r2, got

    df1, g_nfpost1 = _norm_bwd(g4, "norm_bwd4", m=f1, w_post=nrm(norm_ffn_post, 1))
    du3, g_up1, own_dn1, r2_dn1, _ = mlp_bwd(1, df1, u3, sq1, act1, None)
    gh3, dm1, g_nfpre1, g_nmpost1 = _norm_bwd(g4, "norm_bwd3", du=du3, h=h3, w_pre=nrm(norm_ffn_pre, 1),
                                              m=m1, w_post=nrm(norm_mix_post, 1))
    dy1, (rc,) = _mm(dm1, w_so, kind="nt", name="mm_dy_ssd", rider=swap_cores(g_up1))
    part, own_up1 = core_sum(g_up1, rc, "rs_mlp_up1")
    g_so = _mm(y1, dm1, kind="tn", name="mm_g_ssd_out", tj=1024, tr=4096).reshape(N_DEV, cf.DI // N_DEV, D)
    (dz, dxs, dbm, dcm, ddt_parts, g_ssd_nw, g_dskip_e, g_alog_e, g_bias_e), (r2_up1, rc) = _ssd_bwd(
        cf, proj1, xact, bias_e, alog_e, dskip_e, ssd_nw, yp1, st1, dy1,
        rider=_rider_join([_r_swap_chips(part), swap_cores(g_so)]))
    part, own_so = core_sum(g_so, rc, "rs_ssd_w_out")
    dact = _fill_bc(dxs, dbm, dcm, "ssd_dact_fill")
    dpre1, g_conv_w, g_conv_b = _conv_bwd_pre(cf, proj1, conv_w, conv_b, dact)
    assert cf.SINP == cf.DI + cf.CD + LANES
    dproj1 = _conv_bwd_x(cf, dpre1, conv_w, dz)
    dproj1 = _sum_groups(ddt_parts, "ssd_ddt_sum", dproj1, (cf.DI + cf.CD) // LANES)
    du2, (r2_so,) = _mm(dproj1, w_si, kind="nn", name="mm_du_ssd", tj=1024, tr=3456, rider=_r_swap_chips(part))
    g_si = _mm(dproj1, u2, kind="tn", name="mm_g_ssd_in", ti=1152, tj=1024, tr=2048)
    g_si = g_si[:cf.SIN].reshape(N_DEV, cf.SIN // N_DEV, D)
    gh2, df0, g_nmpre1, g_nfpost0 = _norm_bwd(gh3, "norm_bwd2", du=du2, h=h2, w_pre=nrm(norm_mix_pre, 1),
                                              m=f0, w_post=nrm(norm_ffn_post, 0))
    own_si = []

    def si_chips(got):
        part, own = core_sum(g_si, got[0], "rs_ssd_w_in")
        own_si.append(own)
        return _r_swap_chips(part)

    du1, g_up0, own_dn0, r2_dn0, (r2_si,) = mlp_bwd(0, df0, u1, sq0, act0, swap_cores(g_si), si_chips)
    own_si = own_si[0]
    gh1, dm0, g_nfpre0, g_nmpost0 = _norm_bwd(gh2, "norm_bwd1", du=du1, h=h1, w_pre=nrm(norm_ffn_pre, 0),
                                              m=m0, w_post=nrm(norm_mix_post, 0))
    dy0, (rc,) = _mm(dm0, w_ro, kind="nt", name="mm_dy_ret", rider=swap_cores(g_up0))
    part, own_up0 = core_sum(g_up0, rc, "rs_mlp_up0")
    g_ro = _mm(y0, dm0, kind="tn", name="mm_g_ret_out", tj=1024, tr=4096).reshape(N_DEV, cf.V // N_DEV, D)
    (dproj0, g_gn), (r2_up0, rc) = _ret_bwd(cf, proj0, cos, sin, ret_gn_w, lgam, o0, st0, dy0,
                                            rider=_rider_join([_r_swap_chips(part), swap_cores(g_ro)]))
    part, own_ro = core_sum(g_ro, rc, "rs_ret_w_out")
    rin8 = cf.RIN // N_DEV
    g_ri0, (r2_ro,) = _mm(u0, dproj0, kind="tn", name="mm_g_ret_in0", ti=512, tj=rin8, tr=4096, out_nblk=N_DEV,
                          i_part=(0, 2), rider=_r_swap_chips(part))
    g_ri1, (rc,) = _mm(u0, dproj0, kind="tn", name="mm_g_ret_in1", ti=512, tj=rin8, tr=4096, out_nblk=N_DEV,
                       i_part=(1, 2), rider=swap_cores(g_ri0))
    part, own_ri0 = core_sum(g_ri0, rc, "rs_ret_w_in0")
    du0, (r2_ri0, rc) = _mm(dproj0, w_ri, kind="nn", name="mm_du_ret", tj=1024, tr=4096,
                            rider=_rider_join([_r_swap_chips(part), swap_cores(g_ri1)]))
    part, own_ri1 = core_sum(g_ri1, rc, "rs_ret_w_in1")
    (grad_x, g_nmpre0), (r2_ri1,) = _norm_bwd(gh1, "norm_bwd0", du=du0, h=h0, w_pre=nrm(norm_mix_pre, 0),
                                              rider=_r_swap_chips(part))

    g_nmpre = jnp.concatenate([g_nmpre0, g_nmpre1], axis=0)
    g_nmpost = jnp.concatenate([g_nmpost0, g_nmpost1], axis=0)
    g_nfpre = jnp.concatenate([g_nfpre0, g_nfpre1], axis=0)
    g_nfpost = jnp.concatenate([g_nfpost0, g_nfpost1], axis=0)
    segs = [g_nmpre, g_nmpost, g_nfpre, g_nfpost, g_gn, g_conv_w, g_conv_b, g_ssd_nw, g_bias_e, g_alog_e, g_dskip_e]
    flat = jnp.concatenate([s.reshape(-1, LANES) for s in segs], axis=0)
    n_fold = 3 * cf.DI // LANES
    red = _all_reduce_small(flat, n_fold, cf.P, "all_reduce_small")
    outs, r0 = [], 0
    for s in segs:
        nr = s.size // LANES
        outs.append(red[r0:r0 + nr])
        r0 += nr
    (g_nmpre, g_nmpost, g_nfpre, g_nfpost) = [o.reshape(DEPTH, D) for o in outs[:4]]
    g_gn = outs[4].reshape(1, cf.V)
    g_conv_w = lax.dynamic_slice_in_dim(outs[5].reshape(SSD_CONV_W, cf.CD), my_dev * cw, cw, axis=1)[None]
    g_conv_b = lax.dynamic_slice_in_dim(outs[6].reshape(1, cf.CD), my_dev * cbw, cbw, axis=1)
    g_ssd_nw = lax.dynamic_slice_in_dim(outs[7].reshape(1, cf.DI), my_dev * nww, nww, axis=1)
    per_row = LANES // cf.P
    g_bias, g_alog, g_dskip = [o[:, :per_row].reshape(1, cf.SH) for o in outs[8:]]

    def rs(own, recv, w, m, v, layer, name, into=None, row_part=(0, 1)):
        return _chip_sum_adam(own, recv, w, m, v, layer, name + "_adam", into, row_part)

    r_ri = rs(own_ri0, r2_ri0, ret_w_in, m_ret_w_in, v_ret_w_in, 0, "rs_ret_w_in0", None, (0, 2))
    r_ri = rs(own_ri1, r2_ri1, ret_w_in, m_ret_w_in, v_ret_w_in, 0, "rs_ret_w_in1", r_ri, (1, 2))
    r_ro = rs(own_ro, r2_ro, ret_w_out, m_ret_w_out, v_ret_w_out, 0, "rs_ret_w_out")
    r_si = [tr12(r) for r in rs(own_si, r2_si, tr12(ssd_w_in), tr12(m_ssd_w_in), tr12(v_ssd_w_in), 0, "rs_ssd_w_in")]
    r_so = rs(own_so, r2_so, ssd_w_out, m_ssd_w_out, v_ssd_w_out, 0, "rs_ssd_w_out")
    r_up = rs(own_up1, r2_up1, mlp_w_up, m_mlp_w_up, v_mlp_w_up, 1, "rs_mlp_up1")
    r_up = rs(own_up0, r2_up0, mlp_w_up, m_mlp_w_up, v_mlp_w_up, 0, "rs_mlp_up0", r_up)
    r_dn = rs(own_dn1, r2_dn1, mlp_w_down, m_mlp_w_down, v_mlp_w_down, 1, "rs_mlp_down1")
    r_dn = rs(own_dn0, r2_dn0, mlp_w_down, m_mlp_w_down, v_mlp_w_down, 0, "rs_mlp_down0", r_dn)
    lead = list

    def small_adam(w, g, m, v, name):
        return [g] + list(_adam_small(w, g, m, v, name))

    results = {
        "norm_mix_pre": small_adam(norm_mix_pre, g_nmpre, m_norm_mix_pre, v_norm_mix_pre, "adam_nmpre"),
        "norm_mix_post": small_adam(norm_mix_post, g_nmpost, m_norm_mix_post, v_norm_mix_post, "adam_nmpost"),
        "norm_ffn_pre": small_adam(norm_ffn_pre, g_nfpre, m_norm_ffn_pre, v_norm_ffn_pre, "adam_nfpre"),
        "norm_ffn_post": small_adam(norm_ffn_post, g_nfpost, m_norm_ffn_post, v_norm_ffn_post, "adam_nfpost"),
        "ret_w_in": lead(r_ri),
        "ret_gn_w": small_adam(ret_gn_w, g_gn, m_ret_gn_w, v_ret_gn_w, "adam_gn"),
        "ret_w_out": lead(r_ro),
        "ssd_w_in": lead(r_si),
        "ssd_conv_w": small_adam(ssd_conv_w, g_conv_w, m_ssd_conv_w, v_ssd_conv_w, "adam_conv_w"),
        "ssd_conv_b": small_adam(ssd_conv_b, g_conv_b, m_ssd_conv_b, v_ssd_conv_b, "adam_conv_b"),
        "ssd_dt_bias": small_adam(ssd_dt_bias, g_bias, m_ssd_dt_bias, v_ssd_dt_bias, "adam_dt_bias"),
        "ssd_a_log": small_adam(ssd_a_log, g_alog, m_ssd_a_log, v_ssd_a_log, "adam_a_log"),
        "ssd_d": small_adam(ssd_d, g_dskip, m_ssd_d, v_ssd_d, "adam_d"),
        "ssd_norm_w": small_adam(ssd_norm_w, g_ssd_nw, m_ssd_norm_w, v_ssd_norm_w, "adam_ssd_nw"),
        "ssd_w_out": lead(r_so),
        "mlp_w_up": r_up,
        "mlp_w_down": r_dn,
    }
    names = list(results)
    out = [loss, grad_x.reshape(1, T, D)]
    for k in range(4):
        out += [results[n][k] for n in names]
    return tuple(out)
```

```python
import functools
import math
import types

import jax
import jax.numpy as jnp
from jax import lax
from jax.experimental import pallas as pl
from jax.experimental.pallas import tpu as pltpu

F32 = jnp.float32
BF16 = jnp.bfloat16
HI = lax.Precision.HIGHEST
NN = (((1,), (0,)), ((), ()))
NT = (((1,), (1,)), ((), ()))
TN = (((0,), (0,)), ((), ()))
MESH = pl.DeviceIdType.MESH

V7X_VMEM_BYTES = 64 * 2**20
VMEM_LIMIT = V7X_VMEM_BYTES - 8 * 2**20
LANES = 128
SUBLANES = 8
N_DEV = 8

D_MODEL = 2048
SEQ = 8192
DEPTH = 2
CHUNK = 64
RMS_EPS = 1e-6
RET_HEAD_DK = 256
ROPE_BASE = 10000.0
GN_EPS = 1e-5
SSD_HEADDIM = 64
SSD_HEADS_PER_GROUP = 8
SSD_STATE = 128
SSD_CONV_W = 4
ADAM_LR = 0.001
ADAM_B1 = 0.9
ADAM_B2 = 0.999
ADAM_EPS = 1e-08
ADAM_WD = 0.01
ADAM_STEP = 10


def _cfg():
    c = types.SimpleNamespace()
    c.D, c.T, c.L = D_MODEL, SEQ, CHUNK
    c.DK = RET_HEAD_DK
    c.H = c.D // c.DK
    c.QK = c.H * c.DK
    c.DV = 2 * c.DK
    c.V = c.H * c.DV
    c.RIN = 2 * c.QK + 2 * c.V
    c.DI = 2 * c.D
    c.P = SSD_HEADDIM
    c.SH = c.DI // c.P
    c.RG = SSD_HEADS_PER_GROUP
    c.G = c.SH // c.RG
    c.GW = c.RG * c.P
    c.N = SSD_STATE
    c.CD = c.DI + 2 * c.G * c.N
    c.SIN = c.DI + c.CD + c.SH
    c.SINP = -(-c.SIN // LANES) * LANES
    c.FF = 4 * c.D
    c.NC = c.T // c.L
    return c


def _pcall(body, **kw):
    return pl.pallas_call(body, **kw)


def _cp(sem=None):
    return pltpu.CompilerParams(dimension_semantics=sem, vmem_limit_bytes=VMEM_LIMIT)


def _tile(n, pref, mult):
    if n <= pref:
        return n
    t = (pref // mult) * mult
    while t >= mult:
        if n % t == 0:
            return t
        t -= mult
    return n


def _dot(a, b, dn=NN, prec=None):
    return lax.dot_general(a, b, dn, precision=prec, preferred_element_type=F32)


ANY = pl.BlockSpec(memory_space=pl.ANY)


def _mesh_pos():
    return lax.axis_index("x"), lax.axis_index("y"), lax.axis_index("c")


def _rider_join(riders):
    j = types.SimpleNamespace(args=[], out_shape=[], aliases={}, sems=[])
    parts = []
    for r in riders:
        a0, o0, s0 = len(j.args), len(j.out_shape), len(j.sems)
        parts.append((r, a0, o0, s0))
        j.aliases.update({a0 + k: o0 + v for k, v in r.aliases.items()})
        j.args += r.args
        j.out_shape += r.out_shape
        j.sems += r.sems

    def make(rins, routs, sems):
        sends, recvs, locs = [], [], []
        for r, a0, o0, s0 in parts:
            s, rc, lc = r.make(rins[a0:a0 + len(r.args)], routs[o0:o0 + len(r.out_shape)], sems[s0:s0 + len(r.sems)])
            sends += s
            recvs += rc
            locs += lc
        return sends, recvs, locs

    j.make = make
    return j


def _rider_start(rider, rins, routs, sems):
    sends, _, locs = rider.make(rins, routs, sems)
    for cp in locs + sends:
        cp.start()


def _rider_wait(rider, rins, routs, sems):
    sends, recvs, locs = rider.make(rins, routs, sems)
    for cp in recvs:
        cp.wait_recv()
    for cp in sends:
        cp.wait_send()
    for cp in locs:
        cp.wait()


def _host_call(body, *, name, grid, in_specs, out_specs, out_shape, scratch, args, sem, rider=None):
    in_specs, out_specs, out_shape, scratch, args = map(list, (in_specs, out_specs, out_shape, scratch, args))
    if rider is None:
        res = _pcall(body, name=name, grid=grid, in_specs=in_specs, out_specs=out_specs, out_shape=out_shape,
                     scratch_shapes=scratch, compiler_params=_cp(sem))(*args)
        return list(res), []
    n_in, n_out, n_scr = len(args), len(out_shape), len(scratch)
    n_ra, n_ro = len(rider.args), len(rider.out_shape)

    def full(*refs):
        ins, rins = refs[:n_in], refs[n_in:n_in + n_ra]
        p = n_in + n_ra
        outs, routs = refs[p:p + n_out], refs[p + n_out:p + n_out + n_ro]
        p += n_out + n_ro
        scr, sems = refs[p:p + n_scr], refs[p + n_scr:]
        first = functools.reduce(jnp.logical_and, [pl.program_id(k) == 0 for k in range(len(grid))])
        last = functools.reduce(jnp.logical_and, [pl.program_id(k) == grid[k] - 1 for k in range(len(grid))])

        @pl.when(first)
        def _():
            _rider_start(rider, rins, routs, sems)

        body(*ins, *outs, *scr)

        @pl.when(last)
        def _():
            _rider_wait(rider, rins, routs, sems)

    res = _pcall(full, name=name, grid=grid, in_specs=in_specs + [ANY] * n_ra, out_specs=out_specs + [ANY] * n_ro,
                 out_shape=out_shape + rider.out_shape, scratch_shapes=scratch + rider.sems,
                 input_output_aliases={n_in + k: n_out + v for k, v in rider.aliases.items()},
                 compiler_params=_cp(("arbitrary",) * len(grid)))(*args, *rider.args)
    return list(res[:n_out]), list(res[n_out:])


def _comm(rider, name):
    n_ra, n_ro = len(rider.args), len(rider.out_shape)

    def body(*refs):
        rins, routs, sems = refs[:n_ra], refs[n_ra:n_ra + n_ro], refs[n_ra + n_ro:]
        _rider_start(rider, rins, routs, sems)
        _rider_wait(rider, rins, routs, sems)

    return list(_pcall(body, name=name, in_specs=[ANY] * n_ra, out_specs=[ANY] * n_ro, out_shape=rider.out_shape,
                       scratch_shapes=rider.sems, input_output_aliases=dict(rider.aliases))(*rider.args))


def _remote(src, dst, send_sems, recv_sems, k, to):
    return pltpu.make_async_remote_copy(src_ref=src, dst_ref=dst, send_sem=send_sems.at[k], recv_sem=recv_sems.at[k],
                                        device_id=to, device_id_type=MESH)


def _r_gather_chips(x):
    def make(rins, routs, sems):
        (x_ref,), (out_ref,), (send_sems, recv_sems, local_sem) = rins, routs, sems
        x_, y_, c_ = _mesh_pos()
        me = 4 * x_ + 2 * y_ + c_
        peers = [(x_, y_, 1 - c_), (1 - x_, y_, c_), (x_, 1 - y_, c_), (1 - x_, 1 - y_, c_)]
        sends = [_remote(x_ref, out_ref.at[me], send_sems, recv_sems, k, to) for k, to in enumerate(peers)]
        recvs = [_remote(x_ref, out_ref.at[4 * px + 2 * py + pc], send_sems, recv_sems, k, (px, py, pc))
                 for k, (px, py, pc) in enumerate(peers)]
        return sends, recvs, [pltpu.make_async_copy(x_ref, out_ref.at[me], local_sem)]

    return types.SimpleNamespace(
        args=[x], out_shape=[jax.ShapeDtypeStruct((N_DEV,) + x.shape, x.dtype)], aliases={},
        sems=[pltpu.SemaphoreType.DMA((4,)), pltpu.SemaphoreType.DMA((4,)), pltpu.SemaphoreType.DMA], make=make)


def _r_gather_cores(buf):
    def make(rins, routs, sems):
        (out_ref,), (send_sems, recv_sems) = routs, sems
        x_, y_, c_ = _mesh_pos()
        chips = [(1 - x_, y_), (x_, 1 - y_), (1 - x_, 1 - y_)]
        sends = [_remote(out_ref.at[4 * cx + 2 * cy + c_], out_ref.at[4 * cx + 2 * cy + c_], send_sems, recv_sems, k,
                         (x_, y_, 1 - c_)) for k, (cx, cy) in enumerate(chips)]
        recvs = [_remote(out_ref.at[4 * cx + 2 * cy + 1 - c_], out_ref.at[4 * cx + 2 * cy + 1 - c_], send_sems,
                         recv_sems, k, (x_, y_, 1 - c_)) for k, (cx, cy) in enumerate(chips)]
        return sends, recvs, []

    return types.SimpleNamespace(
        args=[buf], out_shape=[jax.ShapeDtypeStruct(buf.shape, buf.dtype)], aliases={0: 0},
        sems=[pltpu.SemaphoreType.DMA((3,)), pltpu.SemaphoreType.DMA((3,))], make=make)


def _r_swap_cores(g):
    def make(rins, routs, sems):
        (g_ref,), (out_ref,), (send_sems, recv_sems) = rins, routs, sems
        x_, y_, c_ = _mesh_pos()
        cps = [_remote(g_ref.at[q, 1 - c_], out_ref.at[q], send_sems, recv_sems, q, (x_, y_, 1 - c_)) for q in range(4)]
        return cps, cps, []

    return types.SimpleNamespace(
        args=[g], out_shape=[jax.ShapeDtypeStruct((4,) + g.shape[2:], g.dtype)], aliases={},
        sems=[pltpu.SemaphoreType.DMA((4,)), pltpu.SemaphoreType.DMA((4,))], make=make)


def _r_swap_chips(p):
    def make(rins, routs, sems):
        (p_ref,), (out_ref,), (send_sems, recv_sems) = rins, routs, sems
        x_, y_, c_ = _mesh_pos()
        chips = [(1 - x_, y_), (x_, 1 - y_), (1 - x_, 1 - y_)]
        cps = [_remote(p_ref.at[2 * cx + cy], out_ref.at[k], send_sems, recv_sems, k, (cx, cy, c_))
               for k, (cx, cy) in enumerate(chips)]
        return cps, cps, []

    return types.SimpleNamespace(
        args=[p], out_shape=[jax.ShapeDtypeStruct((3,) + p.shape[1:], p.dtype)], aliases={},
        sems=[pltpu.SemaphoreType.DMA((3,)), pltpu.SemaphoreType.DMA((3,))], make=make)


def _sigmoid(x):
    return 0.5 * (jnp.tanh(0.5 * x) + 1.0)


def _mm(a, b, *, kind, name, out_dtype=F32, ti=1024, tj=512, tr=2048, epi=None, epi_in=None, out_nblk=1, rider=None,
        i_part=(0, 1)):
    b_blk = b.ndim == 3
    if kind == "tn":
        R, I = a.shape
        I //= i_part[1]
    else:
        I, R = a.shape
    if kind == "nn":
        J = b.shape[1] if not b_blk else b.shape[0] * b.shape[2]
        nb_inner = b.shape[2] if b_blk else J
        r_inner = R
    elif kind == "nt":
        J = b.shape[0] if not b_blk else b.shape[1]
        nb_inner = J
        r_inner = b.shape[2] if b_blk else R
    else:
        J = b.shape[1]
        nb_inner = J
        r_inner = R
    out_inner = J // out_nblk
    ti = _tile(I, ti, LANES if kind == "tn" else SUBLANES)
    tj = _tile(min(nb_inner, out_inner), tj, LANES)
    assert nb_inner % tj == 0 and out_inner % tj == 0 and J % tj == 0
    tr = _tile(r_inner, tr, LANES)
    assert R % tr == 0
    ni, nj, nr = I // ti, J // tj, R // tr
    dn = {"nn": NN, "nt": NT, "tn": TN}[kind]

    if kind == "tn":
        i_off = i_part[0] * ni
        a_spec = pl.BlockSpec((tr, ti), lambda i, j, r: (r, i + i_off))
    else:
        a_spec = pl.BlockSpec((ti, tr), lambda i, j, r: (i, r))
    if kind == "nn":
        if b_blk:
            per = nb_inner // tj
            b_spec = pl.BlockSpec((None, tr, tj), lambda i, j, r: (j // per, r, j % per))
        else:
            b_spec = pl.BlockSpec((tr, tj), lambda i, j, r: (r, j))
    elif kind == "nt":
        if b_blk:
            per = r_inner // tr
            b_spec = pl.BlockSpec((None, tj, tr), lambda i, j, r: (r // per, j, r % per))
        else:
            b_spec = pl.BlockSpec((tj, tr), lambda i, j, r: (j, r))
    else:
        b_spec = pl.BlockSpec((tr, tj), lambda i, j, r: (r, j))
    if out_nblk > 1:
        pero = out_inner // tj
        o_spec = pl.BlockSpec((None, ti, tj), lambda i, j, r: (j // pero, i, j % pero))
        o_shape = (out_nblk, I, out_inner)
    else:
        o_spec = pl.BlockSpec((ti, tj), lambda i, j, r: (i, j))
        o_shape = (I, J)
    in_specs = [a_spec, b_spec]
    args = [a, b]
    if epi == "mul2act":
        in_specs.append(pl.BlockSpec((ti, tj), lambda i, j, r: (i, j)))
        args.append(epi_in)
    if epi == "relu2":
        out_shape = (jax.ShapeDtypeStruct(o_shape, BF16), jax.ShapeDtypeStruct(o_shape, BF16))
        out_specs = (o_spec, o_spec)
    else:
        out_shape = jax.ShapeDtypeStruct(o_shape, out_dtype)
        out_specs = o_spec
    n_in = len(args)
    n_out = 2 if epi == "relu2" else 1

    def body(*refs):
        a_ref, b_ref = refs[0], refs[1]
        outs = refs[n_in:n_in + n_out]
        acc_ref = refs[n_in + n_out] if nr > 1 else None

        def finish(acc):
            if epi == "relu2":
                act = jnp.maximum(acc, 0.0)
                outs[0][...] = (act * act).astype(BF16)
                outs[1][...] = act.astype(BF16)
            elif epi == "mul2act":
                outs[0][...] = (acc * (2.0 * refs[2][...].astype(F32))).astype(out_dtype)
            else:
                outs[0][...] = acc.astype(out_dtype)

        part = _dot(a_ref[...], b_ref[...], dn)
        if nr == 1:
            finish(part)
        else:
            r = pl.program_id(2)

            @pl.when(r == 0)
            def _():
                acc_ref[...] = part

            @pl.when(r > 0)
            def _():
                acc_ref[...] += part

            @pl.when(r == nr - 1)
            def _():
                finish(acc_ref[...])

    res, rider_res = _host_call(
        body, name=name, grid=(ni, nj, nr), in_specs=in_specs,
        out_specs=out_specs if n_out > 1 else [out_specs], out_shape=out_shape if n_out > 1 else [out_shape],
        scratch=[pltpu.VMEM((ti, tj), F32)] if nr > 1 else [], args=args,
        sem=("parallel", "parallel", "arbitrary"), rider=rider)
    res = tuple(res) if n_out > 1 else res[0]
    return res if rider is None else (res, rider_res)


def _rstd(x):
    return lax.rsqrt(jnp.mean(x * x, axis=-1, keepdims=True) + RMS_EPS)


def _rms_bwd_rows(x, w, dy):
    r = _rstd(x)
    xh = x * r
    dxh = dy * w
    dx = r * (dxh - xh * jnp.mean(dxh * xh, axis=-1, keepdims=True))
    return dx, jnp.sum(dy * xh, axis=0, keepdims=True)


def _row_spec(tb, d):
    return pl.BlockSpec((tb, d), lambda i: (i, 0))


def _vec_spec(d):
    return pl.BlockSpec((1, d), lambda i: (0, 0))


def _rms_fwd(h, w, name):
    T, D = h.shape
    tb = _tile(T, 512, SUBLANES)

    def body(h_ref, w_ref, u_ref):
        x = h_ref[...]
        u_ref[...] = (x * _rstd(x) * w_ref[...]).astype(BF16)

    return _pcall(body, name=name, grid=(T // tb,), in_specs=[_row_spec(tb, D), _vec_spec(D)],
                  out_specs=_row_spec(tb, D), out_shape=jax.ShapeDtypeStruct((T, D), BF16),
                  compiler_params=_cp(("parallel",)))(h, w)


def _resid_fwd(h, m, w_post, w_next, name):
    T, D = h.shape
    tb = _tile(T, 256, SUBLANES)

    def body(h_ref, m_ref, wp_ref, wn_ref, ho_ref, u_ref):
        x = m_ref[...]
        hn = h_ref[...] + x * _rstd(x) * wp_ref[...]
        ho_ref[...] = hn
        u_ref[...] = (hn * _rstd(hn) * wn_ref[...]).astype(BF16)

    return _pcall(body, name=name, grid=(T // tb,),
                  in_specs=[_row_spec(tb, D), _row_spec(tb, D), _vec_spec(D), _vec_spec(D)],
                  out_specs=(_row_spec(tb, D), _row_spec(tb, D)),
                  out_shape=(jax.ShapeDtypeStruct((T, D), F32), jax.ShapeDtypeStruct((T, D), BF16)),
                  compiler_params=_cp(("parallel",)))(h, m, w_post, w_next)


def _final_fwd(h, m, w_post, tgt, name):
    T, D = h.shape
    tb = _tile(T, 256, SUBLANES)

    def body(h_ref, m_ref, wp_ref, t_ref, g_ref, l_ref):
        x = m_ref[...]
        e = h_ref[...] + x * _rstd(x) * wp_ref[...] - t_ref[...]
        g_ref[...] = e * (1.0 / D)
        s = jnp.sum(e * e, axis=0, keepdims=True)

        @pl.when(pl.program_id(0) == 0)
        def _():
            l_ref[...] = s

        @pl.when(pl.program_id(0) > 0)
        def _():
            l_ref[...] += s

    return _pcall(body, name=name, grid=(T // tb,),
                  in_specs=[_row_spec(tb, D), _row_spec(tb, D), _vec_spec(D), _row_spec(tb, D)],
                  out_specs=(_row_spec(tb, D), _vec_spec(D)),
                  out_shape=(jax.ShapeDtypeStruct((T, D), F32), jax.ShapeDtypeStruct((1, D), F32)),
                  compiler_params=_cp(("arbitrary",)))(h, m, w_post, tgt)


def _norm_bwd(g_out, name, du=None, h=None, w_pre=None, m=None, w_post=None, rider=None):
    T, D = g_out.shape
    tb = _tile(T, 256, SUBLANES)
    has_pre, has_post = du is not None, m is not None
    args, in_specs = [g_out], [_row_spec(tb, D)]
    if has_pre:
        args += [du, h, w_pre]
        in_specs += [_row_spec(tb, D), _row_spec(tb, D), _vec_spec(D)]
    if has_post:
        args += [m, w_post]
        in_specs += [_row_spec(tb, D), _vec_spec(D)]
    out_shape, out_specs = [], []
    if has_pre:
        out_shape.append(jax.ShapeDtypeStruct((T, D), F32))
        out_specs.append(_row_spec(tb, D))
    if has_post:
        out_shape.append(jax.ShapeDtypeStruct((T, D), BF16))
        out_specs.append(_row_spec(tb, D))
    n_w = int(has_pre) + int(has_post)
    out_shape += [jax.ShapeDtypeStruct((1, D), F32)] * n_w
    out_specs += [_vec_spec(D)] * n_w
    n_in = len(args)

    def body(*refs):
        ins, outs = list(refs[:n_in]), list(refs[n_in:])
        g = ins.pop(0)[...]
        sums = []
        if has_pre:
            du_ref, h_ref, w_ref = ins.pop(0), ins.pop(0), ins.pop(0)
            dx, s = _rms_bwd_rows(h_ref[...], w_ref[...], du_ref[...])
            g = g + dx
            outs.pop(0)[...] = g
            sums.append(s)
        if has_post:
            m_ref, w_ref = ins.pop(0), ins.pop(0)
            dx, s = _rms_bwd_rows(m_ref[...], w_ref[...], g)
            outs.pop(0)[...] = dx.astype(BF16)
            sums.append(s)
        first = pl.program_id(0) == 0
        for o_ref, s in zip(outs, sums):
            @pl.when(first)
            def _(o_ref=o_ref, s=s):
                o_ref[...] = s

            @pl.when(jnp.logical_not(first))
            def _(o_ref=o_ref, s=s):
                o_ref[...] += s

    res, rider_res = _host_call(body, name=name, grid=(T // tb,), in_specs=in_specs, out_specs=out_specs,
                                out_shape=out_shape, scratch=[], args=args, sem=("arbitrary",), rider=rider)
    return tuple(res) if rider is None else (tuple(res), rider_res)


def _ret_consts(lg, L):
    ii = lax.broadcasted_iota(jnp.int32, (L, L), 0).astype(F32)
    jj = lax.broadcasted_iota(jnp.int32, (L, L), 1).astype(F32)
    dmat = jnp.exp(jnp.abs(ii - jj) * lg)
    idx = lax.broadcasted_iota(jnp.int32, (L, 1), 0).astype(F32)
    xi = jnp.exp((idx + 1.0) * lg)
    zeta = jnp.exp((L - 1.0 - idx) * lg)
    cd = jnp.exp(jnp.full((1, 1), L, F32) * lg)
    return dmat, xi, zeta, cd


def _rot(t, cs, sn):
    half = t.shape[-1] // 2
    t1, t2 = t[:, :half], t[:, half:]
    return jnp.concatenate([t1 * cs - t2 * sn, t1 * sn + t2 * cs], axis=-1)


def _rot_bwd(d, cs, sn):
    half = d.shape[-1] // 2
    d1, d2 = d[:, :half], d[:, half:]
    return jnp.concatenate([d1 * cs + d2 * sn, d2 * cs - d1 * sn], axis=-1)


def _ret_specs(cf, tb, rev):
    H, DK, DV = cf.H, cf.DK, cf.DV
    ni = cf.T // tb
    ri = (lambda i: ni - 1 - i) if rev else (lambda i: i)
    q = pl.BlockSpec((tb, DK), lambda h, i: (ri(i), h))
    k = pl.BlockSpec((tb, DK), lambda h, i: (ri(i), H + h))
    v = pl.BlockSpec((tb, DV), lambda h, i: (ri(i), cf.QK * 2 // DV + h))
    g = pl.BlockSpec((tb, DV), lambda h, i: (ri(i), cf.QK * 2 // DV + H + h))
    cs = pl.BlockSpec((tb, DK // 2), lambda h, i: (ri(i), 0))
    gw = pl.BlockSpec((1, DV), lambda h, i: (0, h))
    row_v = pl.BlockSpec((tb, DV), lambda h, i: (ri(i), h))
    row_k = pl.BlockSpec((tb, DK), lambda h, i: (ri(i), h))
    st = pl.BlockSpec((tb // cf.L, None, DK, DV), lambda h, i: (ri(i), h, 0, 0))
    lgs = pl.BlockSpec(memory_space=pltpu.SMEM)
    return q, k, v, g, cs, gw, row_v, row_k, st, lgs


def _ret_fwd(cf, proj, cos, sin, gn_w, lgam, rider=None):
    T, L, H, DK, DV = cf.T, cf.L, cf.H, cf.DK, cf.DV
    tb = _tile(T, 512, L)
    nck = tb // L
    q_s, k_s, v_s, g_s, cs_s, gw_s, row_v, _, st_s, lg_s = _ret_specs(cf, tb, False)
    kscale = DK ** -0.5

    def body(lg_ref, q_ref, k_ref, v_ref, g_ref, cos_ref, sin_ref, gw_ref, y_ref, o_ref, st_ref, state):
        h = pl.program_id(0)

        @pl.when(pl.program_id(1) == 0)
        def _():
            state[...] = jnp.zeros_like(state)

        dmat, xi, zeta, cd = _ret_consts(lg_ref[h], L)
        gw = gw_ref[...]

        def chunk(c, carry):
            rows = pl.ds(pl.multiple_of(c * L, L), L)
            cs, sn = cos_ref[rows, :], sin_ref[rows, :]
            qr = _rot(q_ref[rows, :], cs, sn)
            kr = _rot(k_ref[rows, :], cs, sn) * kscale
            qb, kb = qr.astype(BF16), kr.astype(BF16)
            vb = v_ref[rows, :].astype(BF16)
            st = state[...]
            stb = st.astype(BF16)
            st_ref[c] = stb
            s = _dot(qb, kb, NT) * dmat
            o = _dot(s.astype(BF16), vb) + _dot(qb, stb) * xi
            state[...] = st * cd + _dot((kr * zeta).astype(BF16), vb, TN)
            o_ref[rows, :] = o
            mu = jnp.mean(o, axis=-1, keepdims=True)
            oc = o - mu
            var = jnp.mean(oc * oc, axis=-1, keepdims=True)
            n = oc * lax.rsqrt(var + GN_EPS) * gw
            gt = g_ref[rows, :]
            y_ref[rows, :] = (gt * _sigmoid(gt) * n).astype(BF16)
            return carry

        lax.fori_loop(0, nck, chunk, 0, unroll=True)

    return _host_call(
        body, name="ret_fwd", grid=(H, T // tb),
        in_specs=[lg_s, q_s, k_s, v_s, g_s, cs_s, cs_s, gw_s],
        out_specs=(row_v, row_v, st_s),
        out_shape=(jax.ShapeDtypeStruct((T, cf.V), BF16), jax.ShapeDtypeStruct((T, cf.V), F32),
                   jax.ShapeDtypeStruct((cf.NC, H, DK, DV), BF16)),
        scratch=[pltpu.VMEM((DK, DV), F32)], args=(lgam, proj, proj, proj, proj, cos, sin, gn_w),
        sem=("parallel", "arbitrary"), rider=rider)


def _ret_bwd(cf, proj, cos, sin, gn_w, lgam, o, states, dy, rider=None):
    T, L, H, DK, DV = cf.T, cf.L, cf.H, cf.DK, cf.DV
    tb = _tile(T, 512, L)
    nck = tb // L
    q_s, k_s, v_s, g_s, cs_s, gw_s, row_v, row_k, st_s, lg_s = _ret_specs(cf, tb, True)
    kscale = DK ** -0.5

    ni = T // tb
    n_steps = H * ni
    col0 = (0, cf.QK, 2 * cf.QK, 2 * cf.QK + cf.V)
    widths = (DK, DK, DV, DV)

    def body(lg_ref, q_ref, k_ref, v_ref, g_ref, cos_ref, sin_ref, gw_ref, o_ref, st_ref, dy_ref,
             dp_ref, dgw_ref, dstate, dq_s, dk_s, dv_s, dg_s, sems):
        h = pl.program_id(0)
        step = h * ni + pl.program_id(1)
        slot = step % 2
        dq_ref, dk_ref, dv_ref, dg_ref = dq_s.at[slot], dk_s.at[slot], dv_s.at[slot], dg_s.at[slot]

        def results_out(s):
            hh, ii = s // ni, s % ni
            rows = pl.ds(pl.multiple_of((ni - 1 - ii) * tb, tb), tb)
            return [pltpu.make_async_copy(
                buf.at[s % 2], dp_ref.at[rows, pl.ds(pl.multiple_of(c0 + hh * w, LANES), w)], sems.at[s % 2, k])
                for k, (buf, c0, w) in enumerate(zip((dq_s, dk_s, dv_s, dg_s), col0, widths))]

        @pl.when(step >= 2)
        def _():
            for cp in results_out(step - 2):
                cp.wait()

        @pl.when(pl.program_id(1) == 0)
        def _():
            dstate[...] = jnp.zeros_like(dstate)
            dgw_ref[...] = jnp.zeros_like(dgw_ref)

        dmat, xi, zeta, cd = _ret_consts(lg_ref[h], L)
        gw = gw_ref[...]

        def chunk(t, carry):
            c = nck - 1 - t
            rows = pl.ds(pl.multiple_of(c * L, L), L)
            cs, sn = cos_ref[rows, :], sin_ref[rows, :]
            qr = _rot(q_ref[rows, :], cs, sn)
            kr = _rot(k_ref[rows, :], cs, sn) * kscale
            qb, kb = qr.astype(BF16), kr.astype(BF16)
            kzb = (kr * zeta).astype(BF16)
            vb = v_ref[rows, :].astype(BF16)
            s = (_dot(qb, kb, NT) * dmat).astype(BF16)
            oo = o_ref[rows, :]
            mu = jnp.mean(oo, axis=-1, keepdims=True)
            oc = oo - mu
            rstd = lax.rsqrt(jnp.mean(oc * oc, axis=-1, keepdims=True) + GN_EPS)
            oh = oc * rstd
            gt = g_ref[rows, :]
            sg = _sigmoid(gt)
            dyv = dy_ref[rows, :]
            dn = dyv * (gt * sg)
            dg_ref[rows, :] = (dyv * (oh * gw) * (sg * (1.0 + gt * (1.0 - sg)))).astype(BF16)
            dgw_ref[...] += jnp.sum(dn * oh, axis=0, keepdims=True)
            doh = dn * gw
            do = rstd * (doh - jnp.mean(doh, axis=-1, keepdims=True) - oh * jnp.mean(doh * oh, axis=-1, keepdims=True))
            dob = do.astype(BF16)
            doxb = (do * xi).astype(BF16)
            dst = dstate[...]
            dstb = dst.astype(BF16)
            stb = st_ref[c]
            dv_ref[rows, :] = (_dot(s, dob, TN) + _dot(kzb, dstb)).astype(BF16)
            ds = (_dot(dob, vb, NT) * dmat).astype(BF16)
            dqr = _dot(ds, kb) + _dot(doxb, stb, NT)
            dkr = _dot(ds, qb, TN) + _dot(vb, dstb, NT) * zeta
            dstate[...] = dst * cd + _dot(qb, doxb, TN)
            dq_ref[rows, :] = _rot_bwd(dqr, cs, sn).astype(BF16)
            dk_ref[rows, :] = _rot_bwd(dkr * kscale, cs, sn).astype(BF16)
            return carry

        lax.fori_loop(0, nck, chunk, 0, unroll=True)
        for cp in results_out(step):
            cp.start()

        @pl.when(step == n_steps - 1)
        def _():
            if n_steps > 1:
                for cp in results_out(step - 1):
                    cp.wait()
            for cp in results_out(step):
                cp.wait()

    return _host_call(
        body, name="ret_bwd", grid=(H, ni),
        in_specs=[lg_s, q_s, k_s, v_s, g_s, cs_s, cs_s, gw_s, row_v, st_s, row_v],
        out_specs=(ANY, gw_s),
        out_shape=(jax.ShapeDtypeStruct((T, cf.RIN), BF16), jax.ShapeDtypeStruct((1, cf.V), F32)),
        scratch=[pltpu.VMEM((DK, DV), F32), pltpu.VMEM((2, tb, DK), BF16), pltpu.VMEM((2, tb, DK), BF16),
                 pltpu.VMEM((2, tb, DV), BF16), pltpu.VMEM((2, tb, DV), BF16), pltpu.SemaphoreType.DMA((2, 4))],
        args=(lgam, proj, proj, proj, proj, cos, sin, gn_w, o, states, dy),
        sem=("arbitrary", "arbitrary"), rider=rider)


def _conv_pre(x, halo, w, b, first, W):
    tb = x.shape[0]
    ext = jnp.concatenate([jnp.where(first, 0.0, halo), x], axis=0)
    out = b + w[W - 1:W, :] * x
    for tap in range(W - 1):
        out = out + w[tap:tap + 1, :] * pltpu.roll(ext, W - 1 - tap, 0)[SUBLANES:SUBLANES + tb, :]
    return out, ext


def _conv_fwd(cf, proj, conv_w, conv_b):
    T, CD, W = cf.T, cf.CD, SSD_CONV_W
    tb = _tile(T, 512, SUBLANES)
    tc = _tile(CD, 512, LANES)
    off = cf.DI // tc
    nh = tb // SUBLANES

    def body(x_ref, halo_ref, w_ref, b_ref, o_ref):
        pre, _ = _conv_pre(x_ref[...], halo_ref[...], w_ref[...], b_ref[...], pl.program_id(1) == 0, W)
        o_ref[...] = pre * _sigmoid(pre)

    return _pcall(
        body, name="conv_fwd", grid=(CD // tc, T // tb),
        in_specs=[pl.BlockSpec((tb, tc), lambda j, i: (i, off + j)),
                  pl.BlockSpec((SUBLANES, tc), lambda j, i: (jnp.maximum(i * nh - 1, 0), off + j)),
                  pl.BlockSpec((W, tc), lambda j, i: (0, j)), pl.BlockSpec((1, tc), lambda j, i: (0, j))],
        out_specs=pl.BlockSpec((tb, tc), lambda j, i: (i, j)),
        out_shape=jax.ShapeDtypeStruct((T, CD), F32),
        compiler_params=_cp(("parallel", "arbitrary")),
    )(proj, proj, conv_w, conv_b)


def _conv_bwd_pre(cf, proj, conv_w, conv_b, dact):
    T, CD, W = cf.T, cf.CD, SSD_CONV_W
    tb = _tile(T, 512, SUBLANES)
    tc = _tile(CD, 512, LANES)
    off = cf.DI // tc
    nh = tb // SUBLANES

    def body(x_ref, halo_ref, w_ref, b_ref, da_ref, dp_ref, dw_ref, db_ref):
        x = x_ref[...]
        pre, ext = _conv_pre(x, halo_ref[...], w_ref[...], b_ref[...], pl.program_id(1) == 0, W)
        sg = _sigmoid(pre)
        dp = da_ref[...] * (sg * (1.0 + pre * (1.0 - sg)))
        dp_ref[...] = dp
        rows = [jnp.sum(dp * pltpu.roll(ext, W - 1 - tap, 0)[SUBLANES:SUBLANES + tb, :], axis=0, keepdims=True)
                for tap in range(W - 1)]
        rows.append(jnp.sum(dp * x, axis=0, keepdims=True))
        dw = jnp.concatenate(rows, axis=0)
        db = jnp.sum(dp, axis=0, keepdims=True)

        @pl.when(pl.program_id(1) == 0)
        def _():
            dw_ref[...] = dw
            db_ref[...] = db

        @pl.when(pl.program_id(1) > 0)
        def _():
            dw_ref[...] += dw
            db_ref[...] += db

    return _pcall(
        body, name="conv_bwd_pre", grid=(CD // tc, T // tb),
        in_specs=[pl.BlockSpec((tb, tc), lambda j, i: (i, off + j)),
                  pl.BlockSpec((SUBLANES, tc), lambda j, i: (jnp.maximum(i * nh - 1, 0), off + j)),
                  pl.BlockSpec((W, tc), lambda j, i: (0, j)), pl.BlockSpec((1, tc), lambda j, i: (0, j)),
                  pl.BlockSpec((tb, tc), lambda j, i: (i, j))],
        out_specs=(pl.BlockSpec((tb, tc), lambda j, i: (i, j)), pl.BlockSpec((W, tc), lambda j, i: (0, j)),
                   pl.BlockSpec((1, tc), lambda j, i: (0, j))),
        out_shape=(jax.ShapeDtypeStruct((T, CD), F32), jax.ShapeDtypeStruct((W, CD), F32),
                   jax.ShapeDtypeStruct((1, CD), F32)),
        compiler_params=_cp(("parallel", "arbitrary")),
    )(proj, proj, conv_w, conv_b, dact)


def _conv_bwd_x(cf, dpre, conv_w, into):
    T, CD, W = cf.T, cf.CD, SSD_CONV_W
    tb = _tile(T, 512, SUBLANES)
    tc = _tile(CD, 512, LANES)
    off = cf.DI // tc
    nh = tb // SUBLANES
    last_blk = T // SUBLANES - 1
    ni = T // tb

    def body(d_ref, halo_ref, w_ref, into_ref, o_ref):
        d = d_ref[...]
        w = w_ref[...]
        nxt = jnp.where(pl.program_id(1) == ni - 1, 0.0, halo_ref[...])
        ext = jnp.concatenate([d, nxt], axis=0)
        n = tb + SUBLANES
        out = w[W - 1:W, :] * d
        for tap in range(W - 1):
            out = out + w[tap:tap + 1, :] * pltpu.roll(ext, n - (W - 1 - tap), 0)[:tb, :]
        o_ref[...] = out.astype(BF16)

    return _pcall(
        body, name="conv_bwd_x", grid=(CD // tc, ni),
        in_specs=[pl.BlockSpec((tb, tc), lambda j, i: (i, j)),
                  pl.BlockSpec((SUBLANES, tc), lambda j, i: (jnp.minimum((i + 1) * nh, last_blk), j)),
                  pl.BlockSpec((W, tc), lambda j, i: (0, j)), ANY],
        out_specs=pl.BlockSpec((tb, tc), lambda j, i: (i, off + j)),
        out_shape=jax.ShapeDtypeStruct(into.shape, BF16), input_output_aliases={3: 0},
        compiler_params=_cp(("parallel", "arbitrary")),
    )(dpre, dpre, conv_w, into)


def _ssd_masks(cf, g, tb):
    L, GW, P, RG = cf.L, cf.GW, cf.P, cf.RG
    assert L == P and 2 * L == LANES and RG % 2 == 0
    i32 = jnp.int32
    hrow = lax.broadcasted_iota(i32, (LANES, GW), 0)
    hcol = lax.broadcasted_iota(i32, (LANES, GW), 1) // P
    expand = (hrow == g * RG + hcol).astype(BF16)
    ti = lax.broadcasted_iota(i32, (LANES, LANES), 0)
    tj = lax.broadcasted_iota(i32, (LANES, LANES), 1)
    btril = jnp.logical_and(ti // L == tj // L, ti >= tj).astype(BF16)
    r0 = lax.broadcasted_iota(i32, (L, GW), 0)
    c0 = lax.broadcasted_iota(i32, (L, GW), 1) % L
    tile_eye = (r0 == c0).astype(F32)
    lower = r0 >= c0
    p0 = lax.broadcasted_iota(i32, (2 * L, LANES), 0) // L
    p1 = lax.broadcasted_iota(i32, (2 * L, LANES), 1) // P
    pair = (p0 == p1).astype(F32)
    return expand, btril, tile_eye, lower, pair


def _softplus(x):
    return jnp.maximum(x, 0.0) + jnp.log1p(jnp.exp(-jnp.abs(x)))


def _split3(x):
    hi = x.astype(BF16)
    r1 = x - hi.astype(F32)
    mid = r1.astype(BF16)
    return hi, mid, (r1 - mid.astype(F32)).astype(BF16)


def _chunk_sums(btril, x, dn):
    hi, mid, lo = _split3(x)
    outs = []
    for k in range(x.shape[0] // LANES):
        sl = slice(k * LANES, (k + 1) * LANES)
        outs.append((_dot(btril, lo[sl], dn) + _dot(btril, mid[sl], dn)) + _dot(btril, hi[sl], dn))
    return jnp.concatenate(outs, axis=0)


def _expand_heads(x, expand, dn):
    hi, mid, lo = _split3(x)
    return (_dot(lo, expand, dn) + _dot(mid, expand, dn)) + _dot(hi, expand, dn)


def _ssd_chunk(cf, mk, acum, dt, xs, bm, cm):
    _, _, tile_eye, lower, pair = mk
    rowv = jnp.sum(acum * tile_eye, axis=0, keepdims=True)
    lf = jnp.exp(jnp.where(lower, acum - rowv, -1e30))
    xdt = xs * dt
    bb, cb_ = bm.astype(BF16), cm.astype(BF16)
    bb2 = jnp.concatenate([bb, bb], axis=0)
    cb2 = _dot(cb_, bb2, NT)
    ms, bds = [], []
    for j in range(cf.RG // 2):
        ln = slice(j * LANES, (j + 1) * LANES)
        ms.append((cb2 * lf[:, ln]).astype(BF16))
        xp = xdt[:, ln]
        bds.append((jnp.concatenate([xp, xp], axis=0) * pair).astype(BF16))
    return lf, xdt, bb, cb_, bb2, cb2, ms, bds


def _ssd_specs(cf, tb, rev):
    G, GW, N = cf.G, cf.GW, cf.N
    ni = cf.T // tb
    ri = (lambda i: ni - 1 - i) if rev else (lambda i: i)
    z = pl.BlockSpec((tb, GW), lambda g, i: (ri(i), g))
    dt = pl.BlockSpec((tb, LANES), lambda g, i: (ri(i), (cf.DI + cf.CD) // LANES))
    xs = pl.BlockSpec((tb, GW), lambda g, i: (ri(i), g))
    bm = pl.BlockSpec((tb, N), lambda g, i: (ri(i), cf.DI // N + g))
    cm = pl.BlockSpec((tb, N), lambda g, i: (ri(i), cf.DI // N + G + g))
    vec = pl.BlockSpec((1, GW), lambda g, i: (0, g))
    st = pl.BlockSpec((tb // cf.L, None, N, GW), lambda g, i: (ri(i), g, 0, 0))
    return z, dt, xs, bm, cm, vec, st


def _ssd_fwd(cf, proj, xact, bias_e, alog_e, dskip_e, norm_w):
    T, L, G, GW, N = cf.T, cf.L, cf.G, cf.GW, cf.N
    tb = _tile(T, 512, L)
    nck = tb // L
    z_s, dt_s, xs_s, b_s, c_s, vec_s, st_s = _ssd_specs(cf, tb, False)

    def body(z_ref, dt_ref, xs_ref, b_ref, c_ref, bias_ref, alog_ref, dsk_ref, nw_ref, y_ref, yp_ref, st_ref,
             state, dt_s, ac_s):
        @pl.when(pl.program_id(1) == 0)
        def _():
            state[...] = jnp.zeros_like(state)

        mk = _ssd_masks(cf, pl.program_id(0), tb)
        a_e = -jnp.exp(alog_ref[...])
        dt_all = _softplus(_expand_heads(dt_ref[...], mk[0], NN) + bias_ref[...])
        dt_s[...] = dt_all
        ac_s[...] = _chunk_sums(mk[1], dt_all * a_e, NN)

        def chunk(c, carry):
            rows = pl.ds(pl.multiple_of(c * L, L), L)
            acum = ac_s[rows, :]
            lf, xdt, bb, cb_, _, _, ms, bds = _ssd_chunk(cf, mk, acum, dt_s[rows, :], xs_ref[rows, :],
                                                         b_ref[rows, :], c_ref[rows, :])
            st = state[...]
            stb = st.astype(BF16)
            st_ref[c] = stb
            ydiag = jnp.concatenate([_dot(m, bd) for m, bd in zip(ms, bds)], axis=1)
            al = acum[L - 1:L, :]
            state[...] = st * jnp.exp(al) + _dot(bb, (xdt * jnp.exp(al - acum)).astype(BF16), TN)
            yp_ref[rows, :] = ydiag + _dot(cb_, stb) * jnp.exp(acum)
            return carry

        lax.fori_loop(0, nck, chunk, 0, unroll=True)
        z = z_ref[...]
        yg = (yp_ref[...] + dsk_ref[...] * xs_ref[...]) * (z * _sigmoid(z))
        y_ref[...] = (yg * _rstd(yg) * nw_ref[...]).astype(BF16)

    return _pcall(
        body, name="ssd_fwd", grid=(G, T // tb),
        in_specs=[z_s, dt_s, xs_s, b_s, c_s, vec_s, vec_s, vec_s, vec_s],
        out_specs=(z_s, z_s, st_s),
        out_shape=(jax.ShapeDtypeStruct((T, cf.DI), BF16), jax.ShapeDtypeStruct((T, cf.DI), F32),
                   jax.ShapeDtypeStruct((cf.NC, G, N, GW), BF16)),
        scratch_shapes=[pltpu.VMEM((N, GW), F32), pltpu.VMEM((tb, GW), F32), pltpu.VMEM((tb, GW), F32)],
        compiler_params=_cp(("parallel", "arbitrary")),
    )(proj, proj, xact, xact, xact, bias_e, alog_e, dskip_e, norm_w)


def _ssd_bwd(cf, proj, xact, bias_e, alog_e, dskip_e, norm_w, ypre, states, dy, rider=None):
    T, L, G, GW, N, RG = cf.T, cf.L, cf.G, cf.GW, cf.N, cf.RG
    tb = _tile(T, 512, L)
    nck = tb // L
    ni = T // tb
    z_s, dt_s, xs_s, b_s, c_s, vec_s, st_s = _ssd_specs(cf, tb, True)
    bc_out = pl.BlockSpec((tb, N), lambda g, i: (ni - 1 - i, g))
    ddt_out = pl.BlockSpec((None, tb, LANES), lambda g, i: (g, ni - 1 - i, 0))

    def body(z_ref, dt_ref, xs_ref, b_ref, c_ref, bias_ref, alog_ref, dsk_ref, nw_ref, yp_ref, st_ref, dy_ref,
             dz_ref, dxs_ref, db_ref, dc_ref, ddt_ref, dnw_ref, ddsk_ref, dalog_ref, dbias_ref,
             dstate, dt_s, ac_s, sg_s, dys_s, dxdt_s, dac_s):
        @pl.when(pl.program_id(1) == 0)
        def _():
            dstate[...] = jnp.zeros_like(dstate)
            for r in (dnw_ref, ddsk_ref, dalog_ref, dbias_ref):
                r[...] = jnp.zeros_like(r)

        mk = _ssd_masks(cf, pl.program_id(0), tb)
        expand, btril, tile_eye, lower, pair = mk
        a_e = -jnp.exp(alog_ref[...])
        dsk, nw = dsk_ref[...], nw_ref[...]
        last_row = (lax.broadcasted_iota(jnp.int32, (L, 1), 0) == L - 1).astype(F32)
        raw = _expand_heads(dt_ref[...], expand, NN) + bias_ref[...]
        dt_all = _softplus(raw)
        dt_s[...] = dt_all
        sg_s[...] = _sigmoid(raw)
        ac_s[...] = _chunk_sums(btril, dt_all * a_e, NN)
        z = z_ref[...]
        sz = _sigmoid(z)
        silu = z * sz
        xs_all = xs_ref[...]
        yd = yp_ref[...] + dsk * xs_all
        yg = yd * silu
        rr = _rstd(yg)
        xh = yg * rr
        dout = dy_ref[...]
        dnw_ref[...] += jnp.sum(dout * xh, axis=0, keepdims=True)
        dxh = dout * nw
        dyg = rr * (dxh - xh * jnp.mean(dxh * xh, axis=-1, keepdims=True))
        dz_ref[...] = (dyg * yd * (sz * (1.0 + z * (1.0 - sz)))).astype(BF16)
        dys_all = dyg * silu
        dys_s[...] = dys_all
        ddsk_ref[...] += jnp.sum(dys_all * xs_all, axis=0, keepdims=True)

        def chunk(t, carry):
            c = nck - 1 - t
            rows = pl.ds(pl.multiple_of(c * L, L), L)
            acum = ac_s[rows, :]
            lf, xdt, bb, cb_, bb2, cb2, ms, bds = _ssd_chunk(cf, mk, acum, dt_s[rows, :], xs_ref[rows, :],
                                                             b_ref[rows, :], c_ref[rows, :])
            stb = st_ref[c]
            eac = jnp.exp(acum)
            al = acum[L - 1:L, :]
            eal = jnp.exp(al)
            dte = jnp.exp(al - acum)
            dys = dys_s[rows, :]
            dyb = dys.astype(BF16)
            dms, dxs_, dsegs = [], [], []
            dcb2 = None
            for j in range(RG // 2):
                ln = slice(j * LANES, (j + 1) * LANES)
                dyj = dyb[:, ln]
                dbd = _dot(ms[j], dyj, TN) * pair
                dxs_.append(dbd[:L, :] + dbd[L:, :])
                tj = _dot(dyj, bds[j], NT) * lf[:, ln]
                dcb2 = tj if dcb2 is None else dcb2 + tj
                dsegs.append(tj * cb2)
            dxdt = jnp.concatenate(dxs_, axis=1)
            dseg = jnp.concatenate(dsegs, axis=1)
            dcb2 = dcb2.astype(BF16)
            dcm = _dot(dcb2, bb2)
            dbm2 = _dot(dcb2, cb_, TN)
            dbm = dbm2[:L, :] + dbm2[L:, :]
            dacum = dseg - tile_eye * jnp.sum(dseg, axis=0, keepdims=True)
            dyo = (dys * eac).astype(BF16)
            dcm = dcm + _dot(dyo, stb, NT)
            dacum = dacum + dys * _dot(cb_, stb) * eac
            dst = dstate[...]
            dstb = dst.astype(BF16)
            xd = xdt * dte
            dbm = dbm + _dot(xd.astype(BF16), dstb, NT)
            dxd = _dot(bb, dstb)
            dal = jnp.sum(dst * stb.astype(F32), axis=0, keepdims=True) * eal
            dxdt = dxdt + dxd * dte
            tt = dxd * xd
            dacum = dacum - tt + last_row * (dal + jnp.sum(tt, axis=0, keepdims=True))
            dstate[...] = dst * eal + _dot(cb_, dyo, TN)
            dxdt_s[rows, :] = dxdt
            dac_s[rows, :] = dacum
            db_ref[rows, :] = dbm
            dc_ref[rows, :] = dcm
            return carry

        lax.fori_loop(0, nck, chunk, 0, unroll=True)
        dda = _chunk_sums(btril, dac_s[...], TN)
        dxdt_all = dxdt_s[...]
        dt_all = dt_s[...]
        dxs_ref[...] = dys_s[...] * dsk + dxdt_all * dt_all
        ddt = dxdt_all * xs_ref[...] + dda * a_e
        dalog_ref[...] += jnp.sum(dda * dt_all, axis=0, keepdims=True) * a_e
        draw = ddt * sg_s[...]
        dbias_ref[...] += jnp.sum(draw, axis=0, keepdims=True)
        ddt_ref[...] = _expand_heads(draw, expand, NT)

    GN = G * N
    return _host_call(
        body, name="ssd_bwd", grid=(G, ni),
        in_specs=[z_s, dt_s, xs_s, b_s, c_s, vec_s, vec_s, vec_s, vec_s, z_s, st_s, z_s],
        out_specs=(z_s, z_s, bc_out, bc_out, ddt_out, vec_s, vec_s, vec_s, vec_s),
        out_shape=(jax.ShapeDtypeStruct((T, cf.SINP), BF16), jax.ShapeDtypeStruct((T, cf.CD), F32),
                   jax.ShapeDtypeStruct((T, GN), F32), jax.ShapeDtypeStruct((T, GN), F32),
                   jax.ShapeDtypeStruct((G, T, LANES), F32)) + (jax.ShapeDtypeStruct((1, cf.DI), F32),) * 4,
        scratch=[pltpu.VMEM((N, GW), F32)] + [pltpu.VMEM((tb, GW), F32)] * 6,
        args=(proj, proj, xact, xact, xact, bias_e, alog_e, dskip_e, norm_w, ypre, states, dy),
        sem=("parallel", "arbitrary"), rider=rider)


def _sum_groups(parts, name, into, col_blk):
    G, T, W = parts.shape
    tb = _tile(T, 512, SUBLANES)

    def body(p_ref, into_ref, o_ref):
        acc = p_ref[0]
        for g in range(1, G):
            acc = acc + p_ref[g]
        o_ref[...] = acc.astype(BF16)

    return _pcall(body, name=name, grid=(T // tb,), in_specs=[pl.BlockSpec((G, tb, W), lambda i: (0, i, 0)), ANY],
                  out_specs=pl.BlockSpec((tb, W), lambda i: (i, col_blk)),
                  out_shape=jax.ShapeDtypeStruct(into.shape, BF16), input_output_aliases={1: 0},
                  compiler_params=_cp(("parallel",)))(parts, into)


def _fill_bc(dact, dbm, dcm, name):
    T, GN = dbm.shape
    tb = _tile(T, 512, SUBLANES)
    blk = (dact.shape[1] - 2 * GN) // (2 * GN)
    assert blk * 2 * GN == dact.shape[1] - 2 * GN

    def body(b_ref, c_ref, into_ref, o_ref):
        o_ref[:, :GN] = b_ref[...]
        o_ref[:, GN:] = c_ref[...]

    row = pl.BlockSpec((tb, GN), lambda i: (i, 0))
    return _pcall(body, name=name, grid=(T // tb,), in_specs=[row, row, ANY],
                  out_specs=pl.BlockSpec((tb, 2 * GN), lambda i: (i, blk)),
                  out_shape=jax.ShapeDtypeStruct(dact.shape, F32), input_output_aliases={2: 0},
                  compiler_params=_cp(("parallel",)))(dbm, dcm, dact)


def _mm_gather_nt(a, w_shard, order, name, ti=1024, tj=512, late_rider=None):
    I, R = a.shape
    n = w_shard.shape[0]
    ti = _tile(I, ti, SUBLANES)
    tj = _tile(n, tj, LANES)
    per, ni = n // tj, I // ti
    lr = late_rider if late_rider is not None else _rider_join([])
    n_ra, n_ro = len(lr.args), len(lr.out_shape)

    def body(ord_ref, a_ref, w_ref, *refs):
        rins, (out_ref, gath_ref), routs = refs[:n_ra], refs[n_ra:n_ra + 2], refs[n_ra + 2:n_ra + 2 + n_ro]
        bbuf, tile_sems, send_sems, recv_sems, local_sem = refs[n_ra + 2 + n_ro:n_ra + 7 + n_ro]
        rsems = refs[n_ra + 7 + n_ro:]
        k, i, t = pl.program_id(0), pl.program_id(1), pl.program_id(2)
        x_, y_, c_ = _mesh_pos()
        me = 4 * x_ + 2 * y_ + c_
        sib = (x_, y_, 1 - c_)
        chips = [(1 - x_, y_), (x_, 1 - y_), (1 - x_, 1 - y_)]
        near = [sib] + [(cx, cy, c_) for cx, cy in chips]

        def send_mine(q):
            return _remote(w_ref, gath_ref.at[me], send_sems, recv_sems, q, near[q])

        def from_near(q):
            px, py, pc = near[q]
            return _remote(w_ref, gath_ref.at[4 * px + 2 * py + pc], send_sems, recv_sems, q, near[q])

        def pass_on(j, core):
            rows = gath_ref.at[4 * chips[j][0] + 2 * chips[j][1] + core]
            return _remote(rows, rows, send_sems, recv_sems, 4 + j, sib)

        def block_in(kk):
            return pltpu.make_async_copy(gath_ref.at[ord_ref[kk]], bbuf.at[kk % 2], tile_sems.at[kk % 2])

        mine = pltpu.make_async_copy(w_ref, gath_ref.at[me], local_sem)
        first_tile = t == 0

        @pl.when(jnp.logical_and(jnp.logical_and(k == 0, i == 0), first_tile))
        def _():
            mine.start()
            for q in range(4):
                send_mine(q).start()
            mine.wait()
            block_in(0).start()

        @pl.when(jnp.logical_and(i == 0, first_tile))
        def _():
            block_in(k).wait()

        @pl.when(jnp.logical_and(jnp.logical_and(i == ni - 1, first_tile), k + 1 < N_DEV))
        def _():
            kk = k + 1

            @pl.when(kk == 1)
            def _():
                from_near(0).wait_recv()

            for j in range(3):
                @pl.when(kk == 2 + j)
                def _(j=j):
                    from_near(1 + j).wait_recv()
                    pass_on(j, c_).start()

                @pl.when(kk == 5 + j)
                def _(j=j):
                    pass_on(j, 1 - c_).wait_recv()

            block_in(kk).start()

        if n_ra:
            @pl.when(jnp.logical_and(jnp.logical_and(k == 5, i == 0), first_tile))
            def _():
                _rider_start(lr, rins, routs, rsems)

        cols = pl.ds(pl.multiple_of(t * tj, tj), tj)
        out_ref[...] = _dot(a_ref[...], bbuf[k % 2, cols, :], NT)

        @pl.when(jnp.logical_and(jnp.logical_and(k == N_DEV - 1, i == ni - 1), t == per - 1))
        def _():
            for q in range(4):
                send_mine(q).wait_send()
            for j in range(3):
                pass_on(j, c_).wait_send()
            if n_ra:
                _rider_wait(lr, rins, routs, rsems)

    res = _pcall(
        body, name=name,
        grid_spec=pltpu.PrefetchScalarGridSpec(
            num_scalar_prefetch=1, grid=(N_DEV, ni, per),
            in_specs=[pl.BlockSpec((ti, R), lambda k, i, t, o: (i, 0)), ANY] + [ANY] * n_ra,
            out_specs=[pl.BlockSpec((ti, tj), lambda k, i, t, o: (i, o[k] * per + t)), ANY] + [ANY] * n_ro,
            scratch_shapes=[pltpu.VMEM((2, n, R), BF16), pltpu.SemaphoreType.DMA((2,)), pltpu.SemaphoreType.DMA((7,)),
                            pltpu.SemaphoreType.DMA((7,)), pltpu.SemaphoreType.DMA] + lr.sems),
        out_shape=[jax.ShapeDtypeStruct((I, N_DEV * n), F32), jax.ShapeDtypeStruct((N_DEV, n, R), BF16)] + lr.out_shape,
        input_output_aliases={3 + q: 2 + v for q, v in lr.aliases.items()},
        compiler_params=_cp(("arbitrary",) * 3),
    )(order, a, w_shard, *lr.args)
    return res[0], res[1], list(res[2:])


def _all_gather(x, name):
    def body(x_ref, out_ref, send_sems, recv_sems, local_sem):
        x, y, c = lax.axis_index("x"), lax.axis_index("y"), lax.axis_index("c")
        me, sibling = (x, y, c), (x, y, 1 - c)
        chips = [(1 - x, y), (x, 1 - y), (1 - x, 1 - y)]

        def blk(px, py, pc):
            return out_ref.at[4 * px + 2 * py + pc]

        def copy(k, block, to, src=None):
            return pltpu.make_async_remote_copy(
                src_ref=blk(*block) if src is None else src, dst_ref=blk(*block),
                send_sem=send_sems.at[k], recv_sem=recv_sems.at[k], device_id=to, device_id_type=MESH)

        mine = pltpu.make_async_copy(x_ref, blk(*me), local_sem)
        mine.start()
        first = [copy(0, me, sibling, src=x_ref)]
        first += [copy(1 + j, me, (*chip, c), src=x_ref) for j, chip in enumerate(chips)]
        for cp in first:
            cp.start()
        passed = [copy(4 + j, (*chip, c), sibling) for j, chip in enumerate(chips)]
        for j, chip in enumerate(chips):
            copy(1 + j, (*chip, c), me).wait_recv()
            passed[j].start()
        copy(0, sibling, me).wait_recv()
        for j, chip in enumerate(chips):
            copy(4 + j, (*chip, 1 - c), me).wait_recv()
        for cp in first + passed:
            cp.wait_send()
        mine.wait()

    return _pcall(
        body, name=name, in_specs=[ANY], out_specs=ANY,
        out_shape=jax.ShapeDtypeStruct((N_DEV,) + x.shape, x.dtype),
        scratch_shapes=[pltpu.SemaphoreType.DMA((7,)), pltpu.SemaphoreType.DMA((7,)), pltpu.SemaphoreType.DMA],
    )(x)


def _core_sum(g, recv, idx, name):
    _, _, a, b = g.shape
    n, blk, at = _plane_tiles(a, b, 2 * 2**20)

    def body(idx_ref, g_ref, r_ref, p_ref, own_ref):
        s = g_ref[...] + r_ref[...]
        p_ref[...] = s.astype(BF16)

        @pl.when(pl.program_id(1) == idx_ref[1])
        def _():
            own_ref[...] = s

    return _pcall(
        body, name=name,
        grid_spec=pltpu.PrefetchScalarGridSpec(
            num_scalar_prefetch=1, grid=(n, 4),
            in_specs=[pl.BlockSpec((None, None) + blk, lambda i, q, idx: (q, idx[0]) + at(i)),
                      pl.BlockSpec((None,) + blk, lambda i, q, idx: (q,) + at(i))],
            out_specs=(pl.BlockSpec((None,) + blk, lambda i, q, idx: (q,) + at(i)),
                       pl.BlockSpec(blk, lambda i, q, idx: at(i)))),
        out_shape=(jax.ShapeDtypeStruct((4, a, b), BF16), jax.ShapeDtypeStruct((a, b), F32)),
        compiler_params=_cp(("parallel", "arbitrary")),
    )(idx, g, recv)


def _plane_tiles(a, b, f32_bytes, row_off=0):
    if a % (2 * SUBLANES) == 0:
        tr = _tile(a, max(2 * SUBLANES, f32_bytes // (4 * b) // (2 * SUBLANES) * (2 * SUBLANES)), 2 * SUBLANES)
        return a // tr, (tr, b), lambda i: (i + row_off * (a // tr), 0)
    assert row_off == 0
    tc = _tile(b, max(LANES, f32_bytes // (4 * a) // LANES * LANES), LANES)
    return b // tc, (a, tc), lambda i: (0, i)


def _adam_math(w, g, m, v):
    m = ADAM_B1 * m + (1.0 - ADAM_B1) * g
    v = ADAM_B2 * v + (1.0 - ADAM_B2) * (g * g)
    m_hat = m / (1.0 - ADAM_B1 ** ADAM_STEP)
    v_hat = v / (1.0 - ADAM_B2 ** ADAM_STEP)
    delta = -ADAM_LR * (m_hat / (jnp.sqrt(v_hat) + ADAM_EPS) + ADAM_WD * w)
    return delta, m, v


def _chip_sum_adam(own, recv, w, m, v, layer, name, into=None, row_part=(0, 1)):
    a, b = own.shape
    n = w.shape[0]
    assert w.shape[1] == a * row_part[1]
    nt, blk, at = _plane_tiles(a, b, 2**20)
    _, _, at_w = _plane_tiles(a, b, 2**20, row_part[0])
    wspec = pl.BlockSpec((None,) + blk, lambda i: (layer,) + at_w(i))
    ospec = pl.BlockSpec(blk, at)
    n_into = 0 if into is None else 4

    def body(own_ref, r_ref, w_ref, m_ref, v_ref, *rest):
        g_ref, d_ref, mo_ref, vo_ref = rest[n_into:]
        g = own_ref[...]
        for k in range(3):
            g = g + r_ref[k].astype(F32)
        g_ref[...] = g
        d_ref[...], mo_ref[...], vo_ref[...] = _adam_math(w_ref[...], g, m_ref[...], v_ref[...])

    return _pcall(
        body, name=name, grid=(nt,),
        in_specs=[ospec, pl.BlockSpec((3,) + blk, lambda i: (0,) + at(i)), wspec, wspec, wspec] + [ANY] * n_into,
        out_specs=(wspec,) * 4, out_shape=(jax.ShapeDtypeStruct(w.shape, F32),) * 4,
        input_output_aliases={5 + k: k for k in range(n_into)},
        compiler_params=_cp(("parallel",)),
    )(own, recv, w, m, v, *(into or ()))


def _all_reduce_small(x, n_fold, fold_w, name):
    R, W = x.shape

    def body(x_ref, out_ref, buf, send_sems, recv_sems):
        xx, y, c = lax.axis_index("x"), lax.axis_index("y"), lax.axis_index("c")
        me = 4 * xx + 2 * y + c
        buf[me] = x_ref[...]
        copies = []
        for k in range(1, N_DEV):
            px, py, pc = xx ^ (k >> 2), y ^ ((k >> 1) & 1), c ^ (k & 1)
            copies.append(pltpu.make_async_remote_copy(
                src_ref=x_ref, dst_ref=buf.at[me], send_sem=send_sems.at[k - 1], recv_sem=recv_sems.at[k - 1],
                device_id=(px, py, pc), device_id_type=MESH))
        for cp in copies:
            cp.start()
        for cp in copies:
            cp.wait()
        acc = buf[0]
        for j in range(1, N_DEV):
            acc = acc + buf[j]
        out_ref[...] = acc
        if n_fold:
            l0 = lax.broadcasted_iota(jnp.int32, (W, W), 0) // fold_w
            l1 = lax.broadcasted_iota(jnp.int32, (W, W), 1)
            fold = (l0 == l1).astype(F32)
            out_ref[R - n_fold:, :] = _dot(acc[R - n_fold:, :], fold, NN, HI)

    return _pcall(
        body, name=name, in_specs=[pl.BlockSpec(memory_space=pltpu.VMEM)],
        out_specs=pl.BlockSpec(memory_space=pltpu.VMEM), out_shape=jax.ShapeDtypeStruct((R, W), F32),
        scratch_shapes=[pltpu.VMEM((N_DEV, R, W), F32), pltpu.SemaphoreType.DMA((N_DEV - 1,)),
                        pltpu.SemaphoreType.DMA((N_DEV - 1,))],
        compiler_params=pltpu.CompilerParams(vmem_limit_bytes=VMEM_LIMIT),
    )(x)


def _adam_small(w, g, m, v, name):
    def body(w_ref, g_ref, m_ref, v_ref, d_ref, mo_ref, vo_ref):
        d_ref[...], mo_ref[...], vo_ref[...] = _adam_math(w_ref[...], g_ref[...], m_ref[...], v_ref[...])

    return _pcall(body, name=name, out_shape=(jax.ShapeDtypeStruct(w.shape, F32),) * 3)(w, g, m, v)


def kernel(x, norm_mix_pre, norm_mix_post, norm_ffn_pre, norm_ffn_post, ret_w_in, ret_gn_w, ret_w_out, ssd_w_in, ssd_conv_w, ssd_conv_b, ssd_dt_bias, ssd_a_log, ssd_d, ssd_norm_w, ssd_w_out, mlp_w_up, mlp_w_down, loss_target, m_norm_mix_pre, m_norm_mix_post, m_norm_ffn_pre, m_norm_ffn_post, m_ret_w_in, m_ret_gn_w, m_ret_w_out, m_ssd_w_in, m_ssd_conv_w, m_ssd_conv_b, m_ssd_dt_bias, m_ssd_a_log, m_ssd_d, m_ssd_norm_w, m_ssd_w_out, m_mlp_w_up, m_mlp_w_down, v_norm_mix_pre, v_norm_mix_post, v_norm_ffn_pre, v_norm_ffn_post, v_ret_w_in, v_ret_gn_w, v_ret_w_out, v_ssd_w_in, v_ssd_conv_w, v_ssd_conv_b, v_ssd_dt_bias, v_ssd_a_log, v_ssd_d, v_ssd_norm_w, v_ssd_w_out, v_mlp_w_up, v_mlp_w_down):
    cf = _cfg()
    T, D = cf.T, cf.D
    ax, ay, ac = lax.axis_index("x"), lax.axis_index("y"), lax.axis_index("c")
    my_dev = 4 * ax + 2 * ay + ac
    idx = jnp.stack([ac, 2 * ax + ay]).astype(jnp.int32)

    tr12 = lambda t: jnp.swapaxes(t, 1, 2)
    chips_of = lambda ws: _rider_join([_r_gather_chips(w.astype(BF16)) for w in ws])
    cores_of = lambda bufs: _rider_join([_r_gather_cores(b) for b in bufs])
    cw, cbw, nww = cf.CD // N_DEV, cf.CD // N_DEV, cf.DI // N_DEV
    small = jnp.concatenate([ssd_conv_w[0], ssd_conv_b, jnp.pad(ssd_norm_w, ((0, 0), (0, cw - nww))),
                             jnp.zeros((2, cw), F32)], axis=0)
    small = _all_gather(small, "ag_ssd_small")
    conv_w = jnp.transpose(small[:, :SSD_CONV_W, :], (1, 0, 2)).reshape(SSD_CONV_W, cf.CD)
    conv_b = small[:, SSD_CONV_W, :].reshape(1, cf.CD)
    ssd_nw = small[:, SSD_CONV_W + 1, :nww].reshape(1, cf.DI)

    half = cf.DK // 2
    inv_freq = ROPE_BASE ** (-jnp.arange(half, dtype=F32) / half)
    ang = jnp.arange(T).astype(F32)[:, None] * inv_freq[None, :]
    cos, sin = jnp.cos(ang), jnp.sin(ang)
    lgam = jnp.log1p(-jnp.exp2(-5.0 - jnp.arange(cf.H, dtype=F32)))
    rep = lambda p: jnp.repeat(p.reshape(1, cf.SH), cf.P, axis=1)
    bias_e, alog_e, dskip_e = rep(ssd_dt_bias), rep(ssd_a_log), rep(ssd_d)

    h0 = x.reshape(T, D)
    tgt = loss_target.reshape(T, D)
    nrm = lambda p, i: p[i:i + 1]

    u0 = _rms_fwd(h0, nrm(norm_mix_pre, 0), "rms_fwd0")
    order = jnp.stack([my_dev ^ mask for mask in (0, 1, 4, 2, 6, 5, 3, 7)]).astype(jnp.int32)
    proj0, w_ri, part_a = _mm_gather_nt(u0, ret_w_in[0].T.astype(BF16), order, "mm_ret_in",
                                        late_rider=chips_of([ret_w_out[0]]))
    w_ri = w_ri.reshape(cf.RIN, D)
    (y0, o0, st0), got = _ret_fwd(cf, proj0, cos, sin, ret_gn_w, lgam, rider=_rider_join(
        [cores_of(part_a), chips_of([mlp_w_up[0].T, mlp_w_down[0]])]))
    w_ro = got[0].reshape(cf.V, D)
    m0, got = _mm(y0, w_ro, kind="nn", name="mm_ret_out", tr=cf.V, rider=cores_of(got[1:]))
    w_up0, w_dn0 = got[0].reshape(cf.FF, D), got[1].reshape(cf.FF, D)
    h1, u1 = _resid_fwd(h0, m0, nrm(norm_mix_post, 0), nrm(norm_ffn_pre, 0), "resid_fwd0")
    (sq0, act0), part_b = _mm(u1, w_up0, kind="nt", name="mm_up0", epi="relu2", rider=chips_of([tr12(ssd_w_in)[0]]))
    f0, got = _mm(sq0, w_dn0, kind="nn", name="mm_down0", tj=1024, rider=_rider_join(
        [cores_of(part_b), chips_of([ssd_w_out[0], mlp_w_up[1].T])]))
    w_si = jnp.pad(got[0].reshape(cf.SIN, D), ((0, cf.SINP - cf.SIN), (0, 0)))
    h2, u2 = _resid_fwd(h1, f0, nrm(norm_ffn_post, 0), nrm(norm_mix_pre, 1), "resid_fwd1")
    proj1, got = _mm(u2, w_si, kind="nt", name="mm_ssd_in", tj=1152, rider=_rider_join(
        [cores_of(got[1:]), chips_of([mlp_w_down[1]])]))
    w_so, w_up1 = got[0].reshape(cf.DI, D), got[1].reshape(cf.FF, D)
    xact = _conv_fwd(cf, proj1, conv_w, conv_b)
    y1, yp1, st1 = _ssd_fwd(cf, proj1, xact, bias_e, alog_e, dskip_e, ssd_nw)
    m1, (w_dn1,) = _mm(y1, w_so, kind="nn", name="mm_ssd_out", tr=cf.DI, rider=cores_of(got[2:]))
    w_up, w_dn = [w_up0, w_up1], [w_dn0, w_dn1.reshape(cf.FF, D)]
    h3, u3 = _resid_fwd(h2, m1, nrm(norm_mix_post, 1), nrm(norm_ffn_pre, 1), "resid_fwd2")
    sq1, act1 = _mm(u3, w_up[1], kind="nt", name="mm_up1", epi="relu2")
    f1 = _mm(sq1, w_dn[1], kind="nn", name="mm_down1", tj=1024)
    g4, lsum = _final_fwd(h3, f1, nrm(norm_ffn_post, 1), tgt, "final_fwd")
    loss = lax.psum(0.5 * jnp.sum(lsum) / D, ("x", "y", "c"))

    as4 = lambda g: g.reshape(4, 2, g.shape[1], g.shape[2])
    swap_cores = lambda g: _r_swap_cores(as4(g))
    core_sum = lambda g, recv, name: _core_sum(as4(g), recv, idx, name + "_core_sum")

    def mlp_bwd(l, df, u, sq, act, rider=None, then=None):
        dpre = _mm(df, w_dn[l], kind="nt", name=f"mm_dpre{l}", out_dtype=BF16, epi="mul2act", epi_in=act, rider=rider)
        dpre, got = dpre if rider is not None else (dpre, [])
        g_dn = _mm(sq, df, kind="tn", name=f"mm_gdown{l}", tj=1024, tr=4096, rider=then(got) if then else None)
        g_dn, got = g_dn if then else (g_dn, [])
        g_dn = g_dn.reshape(N_DEV, cf.FF // N_DEV, D)
        du, (rc,) = _mm(dpre, w_up[l], kind="nn", name=f"mm_du_mlp{l}", tj=1024, tr=4096, rider=swap_cores(g_dn))
        part, own = core_sum(g_dn, rc, f"rs_mlp_down{l}")
        g_up, (r2,) = _mm(u, dpre, kind="tn", name=f"mm_gup{l}", tj=1024, tr=4096, out_nblk=N_DEV, rider=_r_swap_chips(part))
        return du, g_up, own, r2, got

    df1, g_nfpost1 = _norm_bwd(g4, "norm_bwd4", m=f1, w_post=nrm(norm_ffn_post, 1))
    du3, g_up1, own_dn1, r2_dn1, _ = mlp_bwd(1, df1, u3, sq1, act1, None)
    gh3, dm1, g_nfpre1, g_nmpost1 = _norm_bwd(g4, "norm_bwd3", du=du3, h=h3, w_pre=nrm(norm_ffn_pre, 1),
                                              m=m1, w_post=nrm(norm_mix_post, 1))
    dy1, (rc,) = _mm(dm1, w_so, kind="nt", name="mm_dy_ssd", rider=swap_cores(g_up1))
    part, own_up1 = core_sum(g_up1, rc, "rs_mlp_up1")
    g_so = _mm(y1, dm1, kind="tn", name="mm_g_ssd_out", tj=1024, tr=4096).reshape(N_DEV, cf.DI // N_DEV, D)
    (dz, dxs, dbm, dcm, ddt_parts, g_ssd_nw, g_dskip_e, g_alog_e, g_bias_e), (r2_up1, rc) = _ssd_bwd(
        cf, proj1, xact, bias_e, alog_e, dskip_e, ssd_nw, yp1, st1, dy1,
        rider=_rider_join([_r_swap_chips(part), swap_cores(g_so)]))
    part, own_so = core_sum(g_so, rc, "rs_ssd_w_out")
    dact = _fill_bc(dxs, dbm, dcm, "ssd_dact_fill")
    dpre1, g_conv_w, g_conv_b = _conv_bwd_pre(cf, proj1, conv_w, conv_b, dact)
    assert cf.SINP == cf.DI + cf.CD + LANES
    dproj1 = _conv_bwd_x(cf, dpre1, conv_w, dz)
    dproj1 = _sum_groups(ddt_parts, "ssd_ddt_sum", dproj1, (cf.DI + cf.CD) // LANES)
    du2, (r2_so,) = _mm(dproj1, w_si, kind="nn", name="mm_du_ssd", tj=1024, tr=3456, rider=_r_swap_chips(part))
    g_si = _mm(dproj1, u2, kind="tn", name="mm_g_ssd_in", ti=1152, tj=1024, tr=2048)
    g_si = g_si[:cf.SIN].reshape(N_DEV, cf.SIN // N_DEV, D)
    gh2, df0, g_nmpre1, g_nfpost0 = _norm_bwd(gh3, "norm_bwd2", du=du2, h=h2, w_pre=nrm(norm_mix_pre, 1),
                                              m=f0, w_post=nrm(norm_ffn_post, 0))
    own_si = []

    def si_chips(got):
        part, own = core_sum(g_si, got[0], "rs_ssd_w_in")
        own_si.append(own)
        return _r_swap_chips(part)

    du1, g_up0, own_dn0, r2_dn0, (r2_si,) = mlp_bwd(0, df0, u1, sq0, act0, swap_cores(g_si), si_chips)
    own_si = own_si[0]
    gh1, dm0, g_nfpre0, g_nmpost0 = _norm_bwd(gh2, "norm_bwd1", du=du1, h=h1, w_pre=nrm(norm_ffn_pre, 0),
                                              m=m0, w_post=nrm(norm_mix_post, 0))
    dy0, (rc,) = _mm(dm0, w_ro, kind="nt", name="mm_dy_ret", rider=swap_cores(g_up0))
    part, own_up0 = core_sum(g_up0, rc, "rs_mlp_up0")
    g_ro = _mm(y0, dm0, kind="tn", name="mm_g_ret_out", tj=1024, tr=4096).reshape(N_DEV, cf.V // N_DEV, D)
    (dproj0, g_gn), (r2_up0, rc) = _ret_bwd(cf, proj0, cos, sin, ret_gn_w, lgam, o0, st0, dy0,
                                            rider=_rider_join([_r_swap_chips(part), swap_cores(g_ro)]))
    part, own_ro = core_sum(g_ro, rc, "rs_ret_w_out")
    rin8 = cf.RIN // N_DEV
    g_ri0, (r2_ro,) = _mm(u0, dproj0, kind="tn", name="mm_g_ret_in0", ti=512, tj=rin8, tr=4096, out_nblk=N_DEV,
                          i_part=(0, 2), rider=_r_swap_chips(part))
    g_ri1, (rc,) = _mm(u0, dproj0, kind="tn", name="mm_g_ret_in1", ti=512, tj=rin8, tr=4096, out_nblk=N_DEV,
                       i_part=(1, 2), rider=swap_cores(g_ri0))
    part, own_ri0 = core_sum(g_ri0, rc, "rs_ret_w_in0")
    du0, (r2_ri0, rc) = _mm(dproj0, w_ri, kind="nn", name="mm_du_ret", tj=1024, tr=4096,
                            rider=_rider_join([_r_swap_chips(part), swap_cores(g_ri1)]))
    part, own_ri1 = core_sum(g_ri1, rc, "rs_ret_w_in1")
    (grad_x, g_nmpre0), (r2_ri1,) = _norm_bwd(gh1, "norm_bwd0", du=du0, h=h0, w_pre=nrm(norm_mix_pre, 0),
                                              rider=_r_swap_chips(part))

    g_nmpre = jnp.concatenate([g_nmpre0, g_nmpre1], axis=0)
    g_nmpost = jnp.concatenate([g_nmpost0, g_nmpost1], axis=0)
    g_nfpre = jnp.concatenate([g_nfpre0, g_nfpre1], axis=0)
    g_nfpost = jnp.concatenate([g_nfpost0, g_nfpost1], axis=0)
    segs = [g_nmpre, g_nmpost, g_nfpre, g_nfpost, g_gn, g_conv_w, g_conv_b, g_ssd_nw, g_bias_e, g_alog_e, g_dskip_e]
    flat = jnp.concatenate([s.reshape(-1, LANES) for s in segs], axis=0)
    n_fold = 3 * cf.DI // LANES
    red = _all_reduce_small(flat, n_fold, cf.P, "all_reduce_small")
    outs, r0 = [], 0
    for s in segs:
        nr = s.size // LANES
        outs.append(red[r0:r0 + nr])
        r0 += nr
    (g_nmpre, g_nmpost, g_nfpre, g_nfpost) = [o.reshape(DEPTH, D) for o in outs[:4]]
    g_gn = outs[4].reshape(1, cf.V)
    g_conv_w = lax.dynamic_slice_in_dim(outs[5].reshape(SSD_CONV_W, cf.CD), my_dev * cw, cw, axis=1)[None]
    g_conv_b = lax.dynamic_slice_in_dim(outs[6].reshape(1, cf.CD), my_dev * cbw, cbw, axis=1)
    g_ssd_nw = lax.dynamic_slice_in_dim(outs[7].reshape(1, cf.DI), my_dev * nww, nww, axis=1)
    per_row = LANES // cf.P
    g_bias, g_alog, g_dskip = [o[:, :per_row].reshape(1, cf.SH) for o in outs[8:]]

    def rs(own, recv, w, m, v, layer, name, into=None, row_part=(0, 1)):
        return _chip_sum_adam(own, recv, w, m, v, layer, name + "_adam", into, row_part)

    r_ri = rs(own_ri0, r2_ri0, ret_w_in, m_ret_w_in, v_ret_w_in, 0, "rs_ret_w_in0", None, (0, 2))
    r_ri = rs(own_ri1, r2_ri1, ret_w_in, m_ret_w_in, v_ret_w_in, 0, "rs_ret_w_in1", r_ri, (1, 2))
    r_ro = rs(own_ro, r2_ro, ret_w_out, m_ret_w_out, v_ret_w_out, 0, "rs_ret_w_out")
    r_si = [tr12(r) for r in rs(own_si, r2_si, tr12(ssd_w_in), tr12(m_ssd_w_in), tr12(v_ssd_w_in), 0, "rs_ssd_w_in")]
    r_so = rs(own_so, r2_so, ssd_w_out, m_ssd_w_out, v_ssd_w_out, 0, "rs_ssd_w_out")
    r_up = rs(own_up1, r2_up1, mlp_w_up, m_mlp_w_up, v_mlp_w_up, 1, "rs_mlp_up1")
    r_up = rs(own_up0, r2_up0, mlp_w_up, m_mlp_w_up, v_mlp_w_up, 0, "rs_mlp_up0", r_up)
    r_dn = rs(own_dn1, r2_dn1, mlp_w_down, m_mlp_w_down, v_mlp_w_down, 1, "rs_mlp_down1")
    r_dn = rs(own_dn0, r2_dn0, mlp_w_down, m_mlp_w_down, v_mlp_w_down, 0, "rs_mlp_down0", r_dn)
    lead = list

    def small_adam(w, g, m, v, name):
        return [g] + list(_adam_small(w, g, m, v, name))

    results = {
        "norm_mix_pre": small_adam(norm_mix_pre, g_nmpre, m_norm_mix_pre, v_norm_mix_pre, "adam_nmpre"),
        "norm_mix_post": small_adam(norm_mix_post, g_nmpost, m_norm_mix_post, v_norm_mix_post, "adam_nmpost"),
        "norm_ffn_pre": small_adam(norm_ffn_pre, g_nfpre, m_norm_ffn_pre, v_norm_ffn_pre, "adam_nfpre"),
        "norm_ffn_post": small_adam(norm_ffn_post, g_nfpost, m_norm_ffn_post, v_norm_ffn_post, "adam_nfpost"),
        "ret_w_in": lead(r_ri),
        "ret_gn_w": small_adam(ret_gn_w, g_gn, m_ret_gn_w, v_ret_gn_w, "adam_gn"),
        "ret_w_out": lead(r_ro),
        "ssd_w_in": lead(r_si),
        "ssd_conv_w": small_adam(ssd_conv_w, g_conv_w, m_ssd_conv_w, v_ssd_conv_w, "adam_conv_w"),
        "ssd_conv_b": small_adam(ssd_conv_b, g_conv_b, m_ssd_conv_b, v_ssd_conv_b, "adam_conv_b"),
        "ssd_dt_bias": small_adam(ssd_dt_bias, g_bias, m_ssd_dt_bias, v_ssd_dt_bias, "adam_dt_bias"),
        "ssd_a_log": small_adam(ssd_a_log, g_alog, m_ssd_a_log, v_ssd_a_log, "adam_a_log"),
        "ssd_d": small_adam(ssd_d, g_dskip, m_ssd_d, v_ssd_d, "adam_d"),
        "ssd_norm_w": small_adam(ssd_norm_w, g_ssd_nw, m_ssd_norm_w, v_ssd_norm_w, "adam_ssd_nw"),
        "ssd_w_out": lead(r_so),
        "mlp_w_up": r_up,
        "mlp_w_down": r_dn,
    }
    names = list(results)
    out = [loss, grad_x.reshape(1, T, D)]
    for k in range(4):
        out += [results[n][k] for n in names]
    return tuple(out)
```

```python
import functools
import math
import types

import jax
import jax.numpy as jnp
from jax import lax
from jax.experimental import pallas as pl
from jax.experimental.pallas import tpu as pltpu

F32 = jnp.float32
BF16 = jnp.bfloat16
HI = lax.Precision.HIGHEST
NN = (((1,), (0,)), ((), ()))
NT = (((1,), (1,)), ((), ()))
TN = (((0,), (0,)), ((), ()))
MESH = pl.DeviceIdType.MESH

V7X_VMEM_BYTES = 64 * 2**20
VMEM_LIMIT = V7X_VMEM_BYTES - 8 * 2**20
LANES = 128
SUBLANES = 8
N_DEV = 8

D_MODEL = 2048
SEQ = 8192
DEPTH = 2
CHUNK = 64
RMS_EPS = 1e-6
RET_HEAD_DK = 256
ROPE_BASE = 10000.0
GN_EPS = 1e-5
SSD_HEADDIM = 64
SSD_HEADS_PER_GROUP = 8
SSD_STATE = 128
SSD_CONV_W = 4
ADAM_LR = 0.001
ADAM_B1 = 0.9
ADAM_B2 = 0.999
ADAM_EPS = 1e-08
ADAM_WD = 0.01
ADAM_STEP = 10


def _cfg():
    c = types.SimpleNamespace()
    c.D, c.T, c.L = D_MODEL, SEQ, CHUNK
    c.DK = RET_HEAD_DK
    c.H = c.D // c.DK
    c.QK = c.H * c.DK
    c.DV = 2 * c.DK
    c.V = c.H * c.DV
    c.RIN = 2 * c.QK + 2 * c.V
    c.DI = 2 * c.D
    c.P = SSD_HEADDIM
    c.SH = c.DI // c.P
    c.RG = SSD_HEADS_PER_GROUP
    c.G = c.SH // c.RG
    c.GW = c.RG * c.P
    c.N = SSD_STATE
    c.CD = c.DI + 2 * c.G * c.N
    c.SIN = c.DI + c.CD + c.SH
    c.SINP = -(-c.SIN // LANES) * LANES
    c.FF = 4 * c.D
    c.NC = c.T // c.L
    return c


def _pcall(body, **kw):
    return pl.pallas_call(body, **kw)


def _cp(sem=None):
    return pltpu.CompilerParams(dimension_semantics=sem, vmem_limit_bytes=VMEM_LIMIT)


def _tile(n, pref, mult):
    if n <= pref:
        return n
    t = (pref // mult) * mult
    while t >= mult:
        if n % t == 0:
            return t
        t -= mult
    return n


def _dot(a, b, dn=NN, prec=None):
    return lax.dot_general(a, b, dn, precision=prec, preferred_element_type=F32)


ANY = pl.BlockSpec(memory_space=pl.ANY)


def _mesh_pos():
    return lax.axis_index("x"), lax.axis_index("y"), lax.axis_index("c")


def _rider_join(riders):
    j = types.SimpleNamespace(args=[], out_shape=[], aliases={}, sems=[])
    parts = []
    for r in riders:
        a0, o0, s0 = len(j.args), len(j.out_shape), len(j.sems)
        parts.append((r, a0, o0, s0))
        j.aliases.update({a0 + k: o0 + v for k, v in r.aliases.items()})
        j.args += r.args
        j.out_shape += r.out_shape
        j.sems += r.sems

    def make(rins, routs, sems):
        sends, recvs, locs = [], [], []
        for r, a0, o0, s0 in parts:
            s, rc, lc = r.make(rins[a0:a0 + len(r.args)], routs[o0:o0 + len(r.out_shape)], sems[s0:s0 + len(r.sems)])
            sends += s
            recvs += rc
            locs += lc
        return sends, recvs, locs

    j.make = make
    return j


def _rider_start(rider, rins, routs, sems):
    sends, _, locs = rider.make(rins, routs, sems)
    for cp in locs + sends:
        cp.start()


def _rider_wait(rider, rins, routs, sems):
    sends, recvs, locs = rider.make(rins, routs, sems)
    for cp in recvs:
        cp.wait_recv()
    for cp in sends:
        cp.wait_send()
    for cp in locs:
        cp.wait()


def _host_call(body, *, name, grid, in_specs, out_specs, out_shape, scratch, args, sem, rider=None):
    in_specs, out_specs, out_shape, scratch, args = map(list, (in_specs, out_specs, out_shape, scratch, args))
    if rider is None:
        res = _pcall(body, name=name, grid=grid, in_specs=in_specs, out_specs=out_specs, out_shape=out_shape,
                     scratch_shapes=scratch, compiler_params=_cp(sem))(*args)
        return list(res), []
    n_in, n_out, n_scr = len(args), len(out_shape), len(scratch)
    n_ra, n_ro = len(rider.args), len(rider.out_shape)

    def full(*refs):
        ins, rins = refs[:n_in], refs[n_in:n_in + n_ra]
        p = n_in + n_ra
        outs, routs = refs[p:p + n_out], refs[p + n_out:p + n_out + n_ro]
        p += n_out + n_ro
        scr, sems = refs[p:p + n_scr], refs[p + n_scr:]
        first = functools.reduce(jnp.logical_and, [pl.program_id(k) == 0 for k in range(len(grid))])
        last = functools.reduce(jnp.logical_and, [pl.program_id(k) == grid[k] - 1 for k in range(len(grid))])

        @pl.when(first)
        def _():
            _rider_start(rider, rins, routs, sems)

        body(*ins, *outs, *scr)

        @pl.when(last)
        def _():
            _rider_wait(rider, rins, routs, sems)

    res = _pcall(full, name=name, grid=grid, in_specs=in_specs + [ANY] * n_ra, out_specs=out_specs + [ANY] * n_ro,
                 out_shape=out_shape + rider.out_shape, scratch_shapes=scratch + rider.sems,
                 input_output_aliases={n_in + k: n_out + v for k, v in rider.aliases.items()},
                 compiler_params=_cp(("arbitrary",) * len(grid)))(*args, *rider.args)
    return list(res[:n_out]), list(res[n_out:])


def _comm(rider, name):
    n_ra, n_ro = len(rider.args), len(rider.out_shape)

    def body(*refs):
        rins, routs, sems = refs[:n_ra], refs[n_ra:n_ra + n_ro], refs[n_ra + n_ro:]
        _rider_start(rider, rins, routs, sems)
        _rider_wait(rider, rins, routs, sems)

    return list(_pcall(body, name=name, in_specs=[ANY] * n_ra, out_specs=[ANY] * n_ro, out_shape=rider.out_shape,
                       scratch_shapes=rider.sems, input_output_aliases=dict(rider.aliases))(*rider.args))


def _remote(src, dst, send_sems, recv_sems, k, to):
    return pltpu.make_async_remote_copy(src_ref=src, dst_ref=dst, send_sem=send_sems.at[k], recv_sem=recv_sems.at[k],
                                        device_id=to, device_id_type=MESH)


def _r_gather_chips(x):
    def make(rins, routs, sems):
        (x_ref,), (out_ref,), (send_sems, recv_sems, local_sem) = rins, routs, sems
        x_, y_, c_ = _mesh_pos()
        me = 4 * x_ + 2 * y_ + c_
        peers = [(x_, y_, 1 - c_), (1 - x_, y_, c_), (x_, 1 - y_, c_), (1 - x_, 1 - y_, c_)]
        sends = [_remote(x_ref, out_ref.at[me], send_sems, recv_sems, k, to) for k, to in enumerate(peers)]
        recvs = [_remote(x_ref, out_ref.at[4 * px + 2 * py + pc], send_sems, recv_sems, k, (px, py, pc))
                 for k, (px, py, pc) in enumerate(peers)]
        return sends, recvs, [pltpu.make_async_copy(x_ref, out_ref.at[me], local_sem)]

    return types.SimpleNamespace(
        args=[x], out_shape=[jax.ShapeDtypeStruct((N_DEV,) + x.shape, x.dtype)], aliases={},
        sems=[pltpu.SemaphoreType.DMA((4,)), pltpu.SemaphoreType.DMA((4,)), pltpu.SemaphoreType.DMA], make=make)


def _r_gather_cores(buf):
    def make(rins, routs, sems):
        (out_ref,), (send_sems, recv_sems) = routs, sems
        x_, y_, c_ = _mesh_pos()
        chips = [(1 - x_, y_), (x_, 1 - y_), (1 - x_, 1 - y_)]
        sends = [_remote(out_ref.at[4 * cx + 2 * cy + c_], out_ref.at[4 * cx + 2 * cy + c_], send_sems, recv_sems, k,
                         (x_, y_, 1 - c_)) for k, (cx, cy) in enumerate(chips)]
        recvs = [_remote(out_ref.at[4 * cx + 2 * cy + 1 - c_], out_ref.at[4 * cx + 2 * cy + 1 - c_], send_sems,
                         recv_sems, k, (x_, y_, 1 - c_)) for k, (cx, cy) in enumerate(chips)]
        return sends, recvs, []

    return types.SimpleNamespace(
        args=[buf], out_shape=[jax.ShapeDtypeStruct(buf.shape, buf.dtype)], aliases={0: 0},
        sems=[pltpu.SemaphoreType.DMA((3,)), pltpu.SemaphoreType.DMA((3,))], make=make)


def _r_swap_cores(g):
    def make(rins, routs, sems):
        (g_ref,), (out_ref,), (send_sems, recv_sems) = rins, routs, sems
        x_, y_, c_ = _mesh_pos()
        cps = [_remote(g_ref.at[q, 1 - c_], out_ref.at[q], send_sems, recv_sems, q, (x_, y_, 1 - c_)) for q in range(4)]
        return cps, cps, []

    return types.SimpleNamespace(
        args=[g], out_shape=[jax.ShapeDtypeStruct((4,) + g.shape[2:], g.dtype)], aliases={},
        sems=[pltpu.SemaphoreType.DMA((4,)), pltpu.SemaphoreType.DMA((4,))], make=make)


def _r_swap_chips(p):
    def make(rins, routs, sems):
        (p_ref,), (out_ref,), (send_sems, recv_sems) = rins, routs, sems
        x_, y_, c_ = _mesh_pos()
        chips = [(1 - x_, y_), (x_, 1 - y_), (1 - x_, 1 - y_)]
        cps = [_remote(p_ref.at[2 * cx + cy], out_ref.at[k], send_sems, recv_sems, k, (cx, cy, c_))
               for k, (cx, cy) in enumerate(chips)]
        return cps, cps, []

    return types.SimpleNamespace(
        args=[p], out_shape=[jax.ShapeDtypeStruct((3,) + p.shape[1:], p.dtype)], aliases={},
        sems=[pltpu.SemaphoreType.DMA((3,)), pltpu.SemaphoreType.DMA((3,))], make=make)


def _sigmoid(x):
    return 0.5 * (jnp.tanh(0.5 * x) + 1.0)


def _mm(a, b, *, kind, name, out_dtype=F32, ti=1024, tj=512, tr=2048, epi=None, epi_in=None, out_nblk=1, rider=None,
        i_part=(0, 1)):
    b_blk = b.ndim == 3
    if kind == "tn":
        R, I = a.shape
        I //= i_part[1]
    else:
        I, R = a.shape
    if kind == "nn":
        J = b.shape[1] if not b_blk else b.shape[0] * b.shape[2]
        nb_inner = b.shape[2] if b_blk else J
        r_inner = R
    elif kind == "nt":
        J = b.shape[0] if not b_blk else b.shape[1]
        nb_inner = J
        r_inner = b.shape[2] if b_blk else R
    else:
        J = b.shape[1]
        nb_inner = J
        r_inner = R
    out_inner = J // out_nblk
    ti = _tile(I, ti, LANES if kind == "tn" else SUBLANES)
    tj = _tile(min(nb_inner, out_inner), tj, LANES)
    assert nb_inner % tj == 0 and out_inner % tj == 0 and J % tj == 0
    tr = _tile(r_inner, tr, LANES)
    assert R % tr == 0
    ni, nj, nr = I // ti, J // tj, R // tr
    dn = {"nn": NN, "nt": NT, "tn": TN}[kind]

    if kind == "tn":
        i_off = i_part[0] * ni
        a_spec = pl.BlockSpec((tr, ti), lambda i, j, r: (r, i + i_off))
    else:
        a_spec = pl.BlockSpec((ti, tr), lambda i, j, r: (i, r))
    if kind == "nn":
        if b_blk:
            per = nb_inner // tj
            b_spec = pl.BlockSpec((None, tr, tj), lambda i, j, r: (j // per, r, j % per))
        else:
            b_spec = pl.BlockSpec((tr, tj), lambda i, j, r: (r, j))
    elif kind == "nt":
        if b_blk:
            per = r_inner // tr
            b_spec = pl.BlockSpec((None, tj, tr), lambda i, j, r: (r // per, j, r % per))
        else:
            b_spec = pl.BlockSpec((tj, tr), lambda i, j, r: (j, r))
    else:
        b_spec = pl.BlockSpec((tr, tj), lambda i, j, r: (r, j))
    if out_nblk > 1:
        pero = out_inner // tj
        o_spec = pl.BlockSpec((None, ti, tj), lambda i, j, r: (j // pero, i, j % pero))
        o_shape = (out_nblk, I, out_inner)
    else:
        o_spec = pl.BlockSpec((ti, tj), lambda i, j, r: (i, j))
        o_shape = (I, J)
    in_specs = [a_spec, b_spec]
    args = [a, b]
    if epi == "mul2act":
        in_specs.append(pl.BlockSpec((ti, tj), lambda i, j, r: (i, j)))
        args.append(epi_in)
    if epi == "relu2":
        out_shape = (jax.ShapeDtypeStruct(o_shape, BF16), jax.ShapeDtypeStruct(o_shape, BF16))
        out_specs = (o_spec, o_spec)
    else:
        out_shape = jax.ShapeDtypeStruct(o_shape, out_dtype)
        out_specs = o_spec
    n_in = len(args)
    n_out = 2 if epi == "relu2" else 1

    def body(*refs):
        a_ref, b_ref = refs[0], refs[1]
        outs = refs[n_in:n_in + n_out]
        acc_ref = refs[n_in + n_out] if nr > 1 else None

        def finish(acc):
            if epi == "relu2":
                act = jnp.maximum(acc, 0.0)
                outs[0][...] = (act * act).astype(BF16)
                outs[1][...] = act.astype(BF16)
            elif epi == "mul2act":
                outs[0][...] = (acc * (2.0 * refs[2][...].astype(F32))).astype(out_dtype)
            else:
                outs[0][...] = acc.astype(out_dtype)

        part = _dot(a_ref[...], b_ref[...], dn)
        if nr == 1:
            finish(part)
        else:
            r = pl.program_id(2)

            @pl.when(r == 0)
            def _():
                acc_ref[...] = part

            @pl.when(r > 0)
            def _():
                acc_ref[...] += part

            @pl.when(r == nr - 1)
            def _():
                finish(acc_ref[...])

    res, rider_res = _host_call(
        body, name=name, grid=(ni, nj, nr), in_specs=in_specs,
        out_specs=out_specs if n_out > 1 else [out_specs], out_shape=out_shape if n_out > 1 else [out_shape],
        scratch=[pltpu.VMEM((ti, tj), F32)] if nr > 1 else [], args=args,
        sem=("parallel", "parallel", "arbitrary"), rider=rider)
    res = tuple(res) if n_out > 1 else res[0]
    return res if rider is None else (res, rider_res)


def _rstd(x):
    return lax.rsqrt(jnp.mean(x * x, axis=-1, keepdims=True) + RMS_EPS)


def _rms_bwd_rows(x, w, dy):
    r = _rstd(x)
    xh = x * r
    dxh = dy * w
    dx = r * (dxh - xh * jnp.mean(dxh * xh, axis=-1, keepdims=True))
    return dx, jnp.sum(dy * xh, axis=0, keepdims=True)


def _row_spec(tb, d):
    return pl.BlockSpec((tb, d), lambda i: (i, 0))


def _vec_spec(d):
    return pl.BlockSpec((1, d), lambda i: (0, 0))


def _rms_fwd(h, w, name):
    T, D = h.shape
    tb = _tile(T, 512, SUBLANES)

    def body(h_ref, w_ref, u_ref):
        x = h_ref[...]
        u_ref[...] = (x * _rstd(x) * w_ref[...]).astype(BF16)

    return _pcall(body, name=name, grid=(T // tb,), in_specs=[_row_spec(tb, D), _vec_spec(D)],
                  out_specs=_row_spec(tb, D), out_shape=jax.ShapeDtypeStruct((T, D), BF16),
                  compiler_params=_cp(("parallel",)))(h, w)


def _resid_fwd(h, m, w_post, w_next, name):
    T, D = h.shape
    tb = _tile(T, 256, SUBLANES)

    def body(h_ref, m_ref, wp_ref, wn_ref, ho_ref, u_ref):
        x = m_ref[...]
        hn = h_ref[...] + x * _rstd(x) * wp_ref[...]
        ho_ref[...] = hn
        u_ref[...] = (hn * _rstd(hn) * wn_ref[...]).astype(BF16)

    return _pcall(body, name=name, grid=(T // tb,),
                  in_specs=[_row_spec(tb, D), _row_spec(tb, D), _vec_spec(D), _vec_spec(D)],
                  out_specs=(_row_spec(tb, D), _row_spec(tb, D)),
                  out_shape=(jax.ShapeDtypeStruct((T, D), F32), jax.ShapeDtypeStruct((T, D), BF16)),
                  compiler_params=_cp(("parallel",)))(h, m, w_post, w_next)


def _final_fwd(h, m, w_post, tgt, name):
    T, D = h.shape
    tb = _tile(T, 256, SUBLANES)

    def body(h_ref, m_ref, wp_ref, t_ref, g_ref, l_ref):
        x = m_ref[...]
        e = h_ref[...] + x * _rstd(x) * wp_ref[...] - t_ref[...]
        g_ref[...] = e * (1.0 / D)
        s = jnp.sum(e * e, axis=0, keepdims=True)

        @pl.when(pl.program_id(0) == 0)
        def _():
            l_ref[...] = s

        @pl.when(pl.program_id(0) > 0)
        def _():
            l_ref[...] += s

    return _pcall(body, name=name, grid=(T // tb,),
                  in_specs=[_row_spec(tb, D), _row_spec(tb, D), _vec_spec(D), _row_spec(tb, D)],
                  out_specs=(_row_spec(tb, D), _vec_spec(D)),
                  out_shape=(jax.ShapeDtypeStruct((T, D), F32), jax.ShapeDtypeStruct((1, D), F32)),
                  compiler_params=_cp(("arbitrary",)))(h, m, w_post, tgt)


def _norm_bwd(g_out, name, du=None, h=None, w_pre=None, m=None, w_post=None, rider=None):
    T, D = g_out.shape
    tb = _tile(T, 256, SUBLANES)
    has_pre, has_post = du is not None, m is not None
    args, in_specs = [g_out], [_row_spec(tb, D)]
    if has_pre:
        args += [du, h, w_pre]
        in_specs += [_row_spec(tb, D), _row_spec(tb, D), _vec_spec(D)]
    if has_post:
        args += [m, w_post]
        in_specs += [_row_spec(tb, D), _vec_spec(D)]
    out_shape, out_specs = [], []
    if has_pre:
        out_shape.append(jax.ShapeDtypeStruct((T, D), F32))
        out_specs.append(_row_spec(tb, D))
    if has_post:
        out_shape.append(jax.ShapeDtypeStruct((T, D), BF16))
        out_specs.append(_row_spec(tb, D))
    n_w = int(has_pre) + int(has_post)
    out_shape += [jax.ShapeDtypeStruct((1, D), F32)] * n_w
    out_specs += [_vec_spec(D)] * n_w
    n_in = len(args)

    def body(*refs):
        ins, outs = list(refs[:n_in]), list(refs[n_in:])
        g = ins.pop(0)[...]
        sums = []
        if has_pre:
            du_ref, h_ref, w_ref = ins.pop(0), ins.pop(0), ins.pop(0)
            dx, s = _rms_bwd_rows(h_ref[...], w_ref[...], du_ref[...])
            g = g + dx
            outs.pop(0)[...] = g
            sums.append(s)
        if has_post:
            m_ref, w_ref = ins.pop(0), ins.pop(0)
            dx, s = _rms_bwd_rows(m_ref[...], w_ref[...], g)
            outs.pop(0)[...] = dx.astype(BF16)
            sums.append(s)
        first = pl.program_id(0) == 0
        for o_ref, s in zip(outs, sums):
            @pl.when(first)
            def _(o_ref=o_ref, s=s):
                o_ref[...] = s

            @pl.when(jnp.logical_not(first))
            def _(o_ref=o_ref, s=s):
                o_ref[...] += s

    res, rider_res = _host_call(body, name=name, grid=(T // tb,), in_specs=in_specs, out_specs=out_specs,
                                out_shape=out_shape, scratch=[], args=args, sem=("arbitrary",), rider=rider)
    return tuple(res) if rider is None else (tuple(res), rider_res)


def _ret_consts(lg, L):
    ii = lax.broadcasted_iota(jnp.int32, (L, L), 0).astype(F32)
    jj = lax.broadcasted_iota(jnp.int32, (L, L), 1).astype(F32)
    dmat = jnp.exp(jnp.abs(ii - jj) * lg)
    idx = lax.broadcasted_iota(jnp.int32, (L, 1), 0).astype(F32)
    xi = jnp.exp((idx + 1.0) * lg)
    zeta = jnp.exp((L - 1.0 - idx) * lg)
    cd = jnp.exp(jnp.full((1, 1), L, F32) * lg)
    return dmat, xi, zeta, cd


def _rot(t, cs, sn):
    half = t.shape[-1] // 2
    t1, t2 = t[:, :half], t[:, half:]
    return jnp.concatenate([t1 * cs - t2 * sn, t1 * sn + t2 * cs], axis=-1)


def _rot_bwd(d, cs, sn):
    half = d.shape[-1] // 2
    d1, d2 = d[:, :half], d[:, half:]
    return jnp.concatenate([d1 * cs + d2 * sn, d2 * cs - d1 * sn], axis=-1)


def _ret_specs(cf, tb, rev):
    H, DK, DV = cf.H, cf.DK, cf.DV
    ni = cf.T // tb
    ri = (lambda i: ni - 1 - i) if rev else (lambda i: i)
    q = pl.BlockSpec((tb, DK), lambda h, i: (ri(i), h))
    k = pl.BlockSpec((tb, DK), lambda h, i: (ri(i), H + h))
    v = pl.BlockSpec((tb, DV), lambda h, i: (ri(i), cf.QK * 2 // DV + h))
    g = pl.BlockSpec((tb, DV), lambda h, i: (ri(i), cf.QK * 2 // DV + H + h))
    cs = pl.BlockSpec((tb, DK // 2), lambda h, i: (ri(i), 0))
    gw = pl.BlockSpec((1, DV), lambda h, i: (0, h))
    row_v = pl.BlockSpec((tb, DV), lambda h, i: (ri(i), h))
    row_k = pl.BlockSpec((tb, DK), lambda h, i: (ri(i), h))
    st = pl.BlockSpec((tb // cf.L, None, DK, DV), lambda h, i: (ri(i), h, 0, 0))
    lgs = pl.BlockSpec(memory_space=pltpu.SMEM)
    return q, k, v, g, cs, gw, row_v, row_k, st, lgs


def _ret_fwd(cf, proj, cos, sin, gn_w, lgam, rider=None):
    T, L, H, DK, DV = cf.T, cf.L, cf.H, cf.DK, cf.DV
    tb = _tile(T, 512, L)
    nck = tb // L
    q_s, k_s, v_s, g_s, cs_s, gw_s, row_v, _, st_s, lg_s = _ret_specs(cf, tb, False)
    kscale = DK ** -0.5

    def body(lg_ref, q_ref, k_ref, v_ref, g_ref, cos_ref, sin_ref, gw_ref, y_ref, o_ref, st_ref, state):
        h = pl.program_id(0)

        @pl.when(pl.program_id(1) == 0)
        def _():
            state[...] = jnp.zeros_like(state)

        dmat, xi, zeta, cd = _ret_consts(lg_ref[h], L)
        gw = gw_ref[...]

        def chunk(c, carry):
            rows = pl.ds(pl.multiple_of(c * L, L), L)
            cs, sn = cos_ref[rows, :], sin_ref[rows, :]
            qr = _rot(q_ref[rows, :], cs, sn)
            kr = _rot(k_ref[rows, :], cs, sn) * kscale
            qb, kb = qr.astype(BF16), kr.astype(BF16)
            vb = v_ref[rows, :].astype(BF16)
            st = state[...]
            stb = st.astype(BF16)
            st_ref[c] = stb
            s = _dot(qb, kb, NT) * dmat
            o = _dot(s.astype(BF16), vb) + _dot(qb, stb) * xi
            state[...] = st * cd + _dot((kr * zeta).astype(BF16), vb, TN)
            o_ref[rows, :] = o
            mu = jnp.mean(o, axis=-1, keepdims=True)
            oc = o - mu
            var = jnp.mean(oc * oc, axis=-1, keepdims=True)
            n = oc * lax.rsqrt(var + GN_EPS) * gw
            gt = g_ref[rows, :]
            y_ref[rows, :] = (gt * _sigmoid(gt) * n).astype(BF16)
            return carry

        lax.fori_loop(0, nck, chunk, 0, unroll=True)

    return _host_call(
        body, name="ret_fwd", grid=(H, T // tb),
        in_specs=[lg_s, q_s, k_s, v_s, g_s, cs_s, cs_s, gw_s],
        out_specs=(row_v, row_v, st_s),
        out_shape=(jax.ShapeDtypeStruct((T, cf.V), BF16), jax.ShapeDtypeStruct((T, cf.V), F32),
                   jax.ShapeDtypeStruct((cf.NC, H, DK, DV), BF16)),
        scratch=[pltpu.VMEM((DK, DV), F32)], args=(lgam, proj, proj, proj, proj, cos, sin, gn_w),
        sem=("parallel", "arbitrary"), rider=rider)


def _ret_bwd(cf, proj, cos, sin, gn_w, lgam, o, states, dy, rider=None):
    T, L, H, DK, DV = cf.T, cf.L, cf.H, cf.DK, cf.DV
    tb = _tile(T, 512, L)
    nck = tb // L
    q_s, k_s, v_s, g_s, cs_s, gw_s, row_v, row_k, st_s, lg_s = _ret_specs(cf, tb, True)
    kscale = DK ** -0.5

    ni = T // tb
    n_steps = H * ni
    col0 = (0, cf.QK, 2 * cf.QK, 2 * cf.QK + cf.V)
    widths = (DK, DK, DV, DV)

    def body(lg_ref, q_ref, k_ref, v_ref, g_ref, cos_ref, sin_ref, gw_ref, o_ref, st_ref, dy_ref,
             dp_ref, dgw_ref, dstate, dq_s, dk_s, dv_s, dg_s, sems):
        h = pl.program_id(0)
        step = h * ni + pl.program_id(1)
        slot = step % 2
        dq_ref, dk_ref, dv_ref, dg_ref = dq_s.at[slot], dk_s.at[slot], dv_s.at[slot], dg_s.at[slot]

        def results_out(s):
            hh, ii = s // ni, s % ni
            rows = pl.ds(pl.multiple_of((ni - 1 - ii) * tb, tb), tb)
            return [pltpu.make_async_copy(
                buf.at[s % 2], dp_ref.at[rows, pl.ds(pl.multiple_of(c0 + hh * w, LANES), w)], sems.at[s % 2, k])
                for k, (buf, c0, w) in enumerate(zip((dq_s, dk_s, dv_s, dg_s), col0, widths))]

        @pl.when(step >= 2)
        def _():
            for cp in results_out(step - 2):
                cp.wait()

        @pl.when(pl.program_id(1) == 0)
        def _():
            dstate[...] = jnp.zeros_like(dstate)
            dgw_ref[...] = jnp.zeros_like(dgw_ref)

        dmat, xi, zeta, cd = _ret_consts(lg_ref[h], L)
        gw = gw_ref[...]

        def chunk(t, carry):
            c = nck - 1 - t
            rows = pl.ds(pl.multiple_of(c * L, L), L)
            cs, sn = cos_ref[rows, :], sin_ref[rows, :]
            qr = _rot(q_ref[rows, :], cs, sn)
            kr = _rot(k_ref[rows, :], cs, sn) * kscale
            qb, kb = qr.astype(BF16), kr.astype(BF16)
            kzb = (kr * zeta).astype(BF16)
            vb = v_ref[rows, :].astype(BF16)
            s = (_dot(qb, kb, NT) * dmat).astype(BF16)
            oo = o_ref[rows, :]
            mu = jnp.mean(oo, axis=-1, keepdims=True)
            oc = oo - mu
            rstd = lax.rsqrt(jnp.mean(oc * oc, axis=-1, keepdims=True) + GN_EPS)
            oh = oc * rstd
            gt = g_ref[rows, :]
            sg = _sigmoid(gt)
            dyv = dy_ref[rows, :]
            dn = dyv * (gt * sg)
            dg_ref[rows, :] = (dyv * (oh * gw) * (sg * (1.0 + gt * (1.0 - sg)))).astype(BF16)
            dgw_ref[...] += jnp.sum(dn * oh, axis=0, keepdims=True)
            doh = dn * gw
            do = rstd * (doh - jnp.mean(doh, axis=-1, keepdims=True) - oh * jnp.mean(doh * oh, axis=-1, keepdims=True))
            dob = do.astype(BF16)
            doxb = (do * xi).astype(BF16)
            dst = dstate[...]
            dstb = dst.astype(BF16)
            stb = st_ref[c]
            dv_ref[rows, :] = (_dot(s, dob, TN) + _dot(kzb, dstb)).astype(BF16)
            ds = (_dot(dob, vb, NT) * dmat).astype(BF16)
            dqr = _dot(ds, kb) + _dot(doxb, stb, NT)
            dkr = _dot(ds, qb, TN) + _dot(vb, dstb, NT) * zeta
            dstate[...] = dst * cd + _dot(qb, doxb, TN)
            dq_ref[rows, :] = _rot_bwd(dqr, cs, sn).astype(BF16)
            dk_ref[rows, :] = _rot_bwd(dkr * kscale, cs, sn).astype(BF16)
            return carry

        lax.fori_loop(0, nck, chunk, 0, unroll=True)
        for cp in results_out(step):
            cp.start()

        @pl.when(step == n_steps - 1)
        def _():
            if n_steps > 1:
                for cp in results_out(step - 1):
                    cp.wait()
            for cp in results_out(step):
                cp.wait()

    return _host_call(
        body, name="ret_bwd", grid=(H, ni),
        in_specs=[lg_s, q_s, k_s, v_s, g_s, cs_s, cs_s, gw_s, row_v, st_s, row_v],
        out_specs=(ANY, gw_s),
        out_shape=(jax.ShapeDtypeStruct((T, cf.RIN), BF16), jax.ShapeDtypeStruct((1, cf.V), F32)),
        scratch=[pltpu.VMEM((DK, DV), F32), pltpu.VMEM((2, tb, DK), BF16), pltpu.VMEM((2, tb, DK), BF16),
                 pltpu.VMEM((2, tb, DV), BF16), pltpu.VMEM((2, tb, DV), BF16), pltpu.SemaphoreType.DMA((2, 4))],
        args=(lgam, proj, proj, proj, proj, cos, sin, gn_w, o, states, dy),
        sem=("arbitrary", "arbitrary"), rider=rider)


def _conv_pre(x, halo, w, b, first, W):
    tb = x.shape[0]
    ext = jnp.concatenate([jnp.where(first, 0.0, halo), x], axis=0)
    out = b + w[W - 1:W, :] * x
    for tap in range(W - 1):
        out = out + w[tap:tap + 1, :] * pltpu.roll(ext, W - 1 - tap, 0)[SUBLANES:SUBLANES + tb, :]
    return out, ext


def _conv_fwd(cf, proj, conv_w, conv_b):
    T, CD, W = cf.T, cf.CD, SSD_CONV_W
    tb = _tile(T, 512, SUBLANES)
    tc = _tile(CD, 512, LANES)
    off = cf.DI // tc
    nh = tb // SUBLANES

    def body(x_ref, halo_ref, w_ref, b_ref, o_ref, pre_ref):
        pre, _ = _conv_pre(x_ref[...], halo_ref[...], w_ref[...], b_ref[...], pl.program_id(1) == 0, W)
        pre_ref[...] = pre
        o_ref[...] = pre * _sigmoid(pre)

    tile = pl.BlockSpec((tb, tc), lambda j, i: (i, j))
    return _pcall(
        body, name="conv_fwd", grid=(CD // tc, T // tb),
        in_specs=[pl.BlockSpec((tb, tc), lambda j, i: (i, off + j)),
                  pl.BlockSpec((SUBLANES, tc), lambda j, i: (jnp.maximum(i * nh - 1, 0), off + j)),
                  pl.BlockSpec((W, tc), lambda j, i: (0, j)), pl.BlockSpec((1, tc), lambda j, i: (0, j))],
        out_specs=(tile, tile),
        out_shape=(jax.ShapeDtypeStruct((T, CD), F32), jax.ShapeDtypeStruct((T, CD), F32)),
        compiler_params=_cp(("parallel", "arbitrary")),
    )(proj, proj, conv_w, conv_b)


def _conv_bwd_pre(cf, proj, pre, dact):
    T, CD, W = cf.T, cf.CD, SSD_CONV_W
    tb = _tile(T, 512, SUBLANES)
    tc = _tile(CD, 512, LANES)
    off = cf.DI // tc
    nh = tb // SUBLANES

    def body(x_ref, halo_ref, pre_ref, da_ref, dp_ref, dw_ref, db_ref):
        x = x_ref[...]
        ext = jnp.concatenate([jnp.where(pl.program_id(1) == 0, 0.0, halo_ref[...]), x], axis=0)
        pre = pre_ref[...]
        sg = _sigmoid(pre)
        dp = da_ref[...] * (sg * (1.0 + pre * (1.0 - sg)))
        dp_ref[...] = dp
        rows = [jnp.sum(dp * pltpu.roll(ext, W - 1 - tap, 0)[SUBLANES:SUBLANES + tb, :], axis=0, keepdims=True)
                for tap in range(W - 1)]
        rows.append(jnp.sum(dp * x, axis=0, keepdims=True))
        dw = jnp.concatenate(rows, axis=0)
        db = jnp.sum(dp, axis=0, keepdims=True)

        @pl.when(pl.program_id(1) == 0)
        def _():
            dw_ref[...] = dw
            db_ref[...] = db

        @pl.when(pl.program_id(1) > 0)
        def _():
            dw_ref[...] += dw
            db_ref[...] += db

    return _pcall(
        body, name="conv_bwd_pre", grid=(CD // tc, T // tb),
        in_specs=[pl.BlockSpec((tb, tc), lambda j, i: (i, off + j)),
                  pl.BlockSpec((SUBLANES, tc), lambda j, i: (jnp.maximum(i * nh - 1, 0), off + j)),
                  pl.BlockSpec((tb, tc), lambda j, i: (i, j)), pl.BlockSpec((tb, tc), lambda j, i: (i, j))],
        out_specs=(pl.BlockSpec((tb, tc), lambda j, i: (i, j)), pl.BlockSpec((W, tc), lambda j, i: (0, j)),
                   pl.BlockSpec((1, tc), lambda j, i: (0, j))),
        out_shape=(jax.ShapeDtypeStruct((T, CD), F32), jax.ShapeDtypeStruct((W, CD), F32),
                   jax.ShapeDtypeStruct((1, CD), F32)),
        compiler_params=_cp(("parallel", "arbitrary")),
    )(proj, proj, pre, dact)


def _conv_bwd_x(cf, dpre, conv_w, into):
    T, CD, W = cf.T, cf.CD, SSD_CONV_W
    tb = _tile(T, 512, SUBLANES)
    tc = _tile(CD, 512, LANES)
    off = cf.DI // tc
    nh = tb // SUBLANES
    last_blk = T // SUBLANES - 1
    ni = T // tb

    def body(d_ref, halo_ref, w_ref, into_ref, o_ref):
        d = d_ref[...]
        w = w_ref[...]
        nxt = jnp.where(pl.program_id(1) == ni - 1, 0.0, halo_ref[...])
        ext = jnp.concatenate([d, nxt], axis=0)
        n = tb + SUBLANES
        out = w[W - 1:W, :] * d
        for tap in range(W - 1):
            out = out + w[tap:tap + 1, :] * pltpu.roll(ext, n - (W - 1 - tap), 0)[:tb, :]
        o_ref[...] = out.astype(BF16)

    return _pcall(
        body, name="conv_bwd_x", grid=(CD // tc, ni),
        in_specs=[pl.BlockSpec((tb, tc), lambda j, i: (i, j)),
                  pl.BlockSpec((SUBLANES, tc), lambda j, i: (jnp.minimum((i + 1) * nh, last_blk), j)),
                  pl.BlockSpec((W, tc), lambda j, i: (0, j)), ANY],
        out_specs=pl.BlockSpec((tb, tc), lambda j, i: (i, off + j)),
        out_shape=jax.ShapeDtypeStruct(into.shape, BF16), input_output_aliases={3: 0},
        compiler_params=_cp(("parallel", "arbitrary")),
    )(dpre, dpre, conv_w, into)


def _ssd_masks(cf, g, tb):
    L, GW, P, RG = cf.L, cf.GW, cf.P, cf.RG
    assert L == P and 2 * L == LANES and RG % 2 == 0
    i32 = jnp.int32
    hrow = lax.broadcasted_iota(i32, (LANES, GW), 0)
    hcol = lax.broadcasted_iota(i32, (LANES, GW), 1) // P
    expand = (hrow == g * RG + hcol).astype(BF16)
    ti = lax.broadcasted_iota(i32, (LANES, LANES), 0)
    tj = lax.broadcasted_iota(i32, (LANES, LANES), 1)
    btril = jnp.logical_and(ti // L == tj // L, ti >= tj).astype(BF16)
    r0 = lax.broadcasted_iota(i32, (L, GW), 0)
    c0 = lax.broadcasted_iota(i32, (L, GW), 1) % L
    tile_eye = (r0 == c0).astype(F32)
    lower = r0 >= c0
    p0 = lax.broadcasted_iota(i32, (2 * L, LANES), 0) // L
    p1 = lax.broadcasted_iota(i32, (2 * L, LANES), 1) // P
    pair = (p0 == p1).astype(F32)
    return expand, btril, tile_eye, lower, pair


def _softplus(x):
    return jnp.maximum(x, 0.0) + jnp.log1p(jnp.exp(-jnp.abs(x)))


def _split3(x):
    hi = x.astype(BF16)
    r1 = x - hi.astype(F32)
    mid = r1.astype(BF16)
    return hi, mid, (r1 - mid.astype(F32)).astype(BF16)


def _chunk_sums(btril, x, dn):
    hi, mid, lo = _split3(x)
    outs = []
    for k in range(x.shape[0] // LANES):
        sl = slice(k * LANES, (k + 1) * LANES)
        outs.append((_dot(btril, lo[sl], dn) + _dot(btril, mid[sl], dn)) + _dot(btril, hi[sl], dn))
    return jnp.concatenate(outs, axis=0)


def _expand_heads(x, expand, dn):
    hi, mid, lo = _split3(x)
    return (_dot(lo, expand, dn) + _dot(mid, expand, dn)) + _dot(hi, expand, dn)


def _ssd_chunk(cf, mk, acum, dt, xs, bm, cm):
    _, _, tile_eye, lower, pair = mk
    rowv = jnp.sum(acum * tile_eye, axis=0, keepdims=True)
    lf = jnp.exp(jnp.where(lower, acum - rowv, -1e30))
    xdt = xs * dt
    bb, cb_ = bm.astype(BF16), cm.astype(BF16)
    bb2 = jnp.concatenate([bb, bb], axis=0)
    cb2 = _dot(cb_, bb2, NT)
    ms, bds = [], []
    for j in range(cf.RG // 2):
        ln = slice(j * LANES, (j + 1) * LANES)
        ms.append((cb2 * lf[:, ln]).astype(BF16))
        xp = xdt[:, ln]
        bds.append((jnp.concatenate([xp, xp], axis=0) * pair).astype(BF16))
    return lf, xdt, bb, cb_, bb2, cb2, ms, bds


def _ssd_specs(cf, tb, rev):
    G, GW, N = cf.G, cf.GW, cf.N
    ni = cf.T // tb
    ri = (lambda i: ni - 1 - i) if rev else (lambda i: i)
    z = pl.BlockSpec((tb, GW), lambda g, i: (ri(i), g))
    dt = pl.BlockSpec((tb, LANES), lambda g, i: (ri(i), (cf.DI + cf.CD) // LANES))
    xs = pl.BlockSpec((tb, GW), lambda g, i: (ri(i), g))
    bm = pl.BlockSpec((tb, N), lambda g, i: (ri(i), cf.DI // N + g))
    cm = pl.BlockSpec((tb, N), lambda g, i: (ri(i), cf.DI // N + G + g))
    vec = pl.BlockSpec((1, GW), lambda g, i: (0, g))
    st = pl.BlockSpec((tb // cf.L, None, N, GW), lambda g, i: (ri(i), g, 0, 0))
    return z, dt, xs, bm, cm, vec, st


def _ssd_fwd(cf, proj, xact, bias_e, alog_e, dskip_e, norm_w):
    T, L, G, GW, N = cf.T, cf.L, cf.G, cf.GW, cf.N
    tb = _tile(T, 512, L)
    nck = tb // L
    z_s, dt_s, xs_s, b_s, c_s, vec_s, st_s = _ssd_specs(cf, tb, False)

    def body(z_ref, dt_ref, xs_ref, b_ref, c_ref, bias_ref, alog_ref, dsk_ref, nw_ref, y_ref, yp_ref, st_ref,
             state, dt_s, ac_s):
        @pl.when(pl.program_id(1) == 0)
        def _():
            state[...] = jnp.zeros_like(state)

        mk = _ssd_masks(cf, pl.program_id(0), tb)
        a_e = -jnp.exp(alog_ref[...])
        dt_all = _softplus(_expand_heads(dt_ref[...], mk[0], NN) + bias_ref[...])
        dt_s[...] = dt_all
        ac_s[...] = _chunk_sums(mk[1], dt_all * a_e, NN)

        def chunk(c, carry):
            rows = pl.ds(pl.multiple_of(c * L, L), L)
            acum = ac_s[rows, :]
            lf, xdt, bb, cb_, _, _, ms, bds = _ssd_chunk(cf, mk, acum, dt_s[rows, :], xs_ref[rows, :],
                                                         b_ref[rows, :], c_ref[rows, :])
            st = state[...]
            stb = st.astype(BF16)
            st_ref[c] = stb
            ydiag = jnp.concatenate([_dot(m, bd) for m, bd in zip(ms, bds)], axis=1)
            al = acum[L - 1:L, :]
            state[...] = st * jnp.exp(al) + _dot(bb, (xdt * jnp.exp(al - acum)).astype(BF16), TN)
            yp_ref[rows, :] = ydiag + _dot(cb_, stb) * jnp.exp(acum)
            return carry

        lax.fori_loop(0, nck, chunk, 0, unroll=True)
        z = z_ref[...]
        yg = (yp_ref[...] + dsk_ref[...] * xs_ref[...]) * (z * _sigmoid(z))
        y_ref[...] = (yg * _rstd(yg) * nw_ref[...]).astype(BF16)

    return _pcall(
        body, name="ssd_fwd", grid=(G, T // tb),
        in_specs=[z_s, dt_s, xs_s, b_s, c_s, vec_s, vec_s, vec_s, vec_s],
        out_specs=(z_s, z_s, st_s),
        out_shape=(jax.ShapeDtypeStruct((T, cf.DI), BF16), jax.ShapeDtypeStruct((T, cf.DI), F32),
                   jax.ShapeDtypeStruct((cf.NC, G, N, GW), BF16)),
        scratch_shapes=[pltpu.VMEM((N, GW), F32), pltpu.VMEM((tb, GW), F32), pltpu.VMEM((tb, GW), F32)],
        compiler_params=_cp(("parallel", "arbitrary")),
    )(proj, proj, xact, xact, xact, bias_e, alog_e, dskip_e, norm_w)


def _ssd_bwd(cf, proj, xact, bias_e, alog_e, dskip_e, norm_w, ypre, states, dy, rider=None):
    T, L, G, GW, N, RG = cf.T, cf.L, cf.G, cf.GW, cf.N, cf.RG
    tb = _tile(T, 512, L)
    nck = tb // L
    ni = T // tb
    z_s, dt_s, xs_s, b_s, c_s, vec_s, st_s = _ssd_specs(cf, tb, True)
    bc_out = pl.BlockSpec((tb, N), lambda g, i: (ni - 1 - i, g))
    ddt_out = pl.BlockSpec((None, tb, LANES), lambda g, i: (g, ni - 1 - i, 0))

    def body(z_ref, dt_ref, xs_ref, b_ref, c_ref, bias_ref, alog_ref, dsk_ref, nw_ref, yp_ref, st_ref, dy_ref,
             dz_ref, dxs_ref, db_ref, dc_ref, ddt_ref, dnw_ref, ddsk_ref, dalog_ref, dbias_ref,
             dstate, dt_s, ac_s, sg_s, dys_s, dxdt_s, dac_s):
        @pl.when(pl.program_id(1) == 0)
        def _():
            dstate[...] = jnp.zeros_like(dstate)
            for r in (dnw_ref, ddsk_ref, dalog_ref, dbias_ref):
                r[...] = jnp.zeros_like(r)

        mk = _ssd_masks(cf, pl.program_id(0), tb)
        expand, btril, tile_eye, lower, pair = mk
        a_e = -jnp.exp(alog_ref[...])
        dsk, nw = dsk_ref[...], nw_ref[...]
        last_row = (lax.broadcasted_iota(jnp.int32, (L, 1), 0) == L - 1).astype(F32)
        raw = _expand_heads(dt_ref[...], expand, NN) + bias_ref[...]
        dt_all = _softplus(raw)
        dt_s[...] = dt_all
        sg_s[...] = _sigmoid(raw)
        ac_s[...] = _chunk_sums(btril, dt_all * a_e, NN)
        z = z_ref[...]
        sz = _sigmoid(z)
        silu = z * sz
        xs_all = xs_ref[...]
        yd = yp_ref[...] + dsk * xs_all
        yg = yd * silu
        rr = _rstd(yg)
        xh = yg * rr
        dout = dy_ref[...]
        dnw_ref[...] += jnp.sum(dout * xh, axis=0, keepdims=True)
        dxh = dout * nw
        dyg = rr * (dxh - xh * jnp.mean(dxh * xh, axis=-1, keepdims=True))
        dz_ref[...] = (dyg * yd * (sz * (1.0 + z * (1.0 - sz)))).astype(BF16)
        dys_all = dyg * silu
        dys_s[...] = dys_all
        ddsk_ref[...] += jnp.sum(dys_all * xs_all, axis=0, keepdims=True)

        def chunk(t, carry):
            c = nck - 1 - t
            rows = pl.ds(pl.multiple_of(c * L, L), L)
            acum = ac_s[rows, :]
            lf, xdt, bb, cb_, bb2, cb2, ms, bds = _ssd_chunk(cf, mk, acum, dt_s[rows, :], xs_ref[rows, :],
                                                             b_ref[rows, :], c_ref[rows, :])
            stb = st_ref[c]
            eac = jnp.exp(acum)
            al = acum[L - 1:L, :]
            eal = jnp.exp(al)
            dte = jnp.exp(al - acum)
            dys = dys_s[rows, :]
            dyb = dys.astype(BF16)
            dms, dxs_, dsegs = [], [], []
            dcb2 = None
            for j in range(RG // 2):
                ln = slice(j * LANES, (j + 1) * LANES)
                dyj = dyb[:, ln]
                dbd = _dot(ms[j], dyj, TN) * pair
                dxs_.append(dbd[:L, :] + dbd[L:, :])
                tj = _dot(dyj, bds[j], NT) * lf[:, ln]
                dcb2 = tj if dcb2 is None else dcb2 + tj
                dsegs.append(tj * cb2)
            dxdt = jnp.concatenate(dxs_, axis=1)
            dseg = jnp.concatenate(dsegs, axis=1)
            dcb2 = dcb2.astype(BF16)
            dcm = _dot(dcb2, bb2)
            dbm2 = _dot(dcb2, cb_, TN)
            dbm = dbm2[:L, :] + dbm2[L:, :]
            dacum = dseg - tile_eye * jnp.sum(dseg, axis=0, keepdims=True)
            dyo = (dys * eac).astype(BF16)
            dcm = dcm + _dot(dyo, stb, NT)
            dacum = dacum + dys * _dot(cb_, stb) * eac
            dst = dstate[...]
            dstb = dst.astype(BF16)
            xd = xdt * dte
            dbm = dbm + _dot(xd.astype(BF16), dstb, NT)
            dxd = _dot(bb, dstb)
            dal = jnp.sum(dst * stb.astype(F32), axis=0, keepdims=True) * eal
            dxdt = dxdt + dxd * dte
            tt = dxd * xd
            dacum = dacum - tt + last_row * (dal + jnp.sum(tt, axis=0, keepdims=True))
            dstate[...] = dst * eal + _dot(cb_, dyo, TN)
            dxdt_s[rows, :] = dxdt
            dac_s[rows, :] = dacum
            db_ref[rows, :] = dbm
            dc_ref[rows, :] = dcm
            return carry

        lax.fori_loop(0, nck, chunk, 0, unroll=True)
        dda = _chunk_sums(btril, dac_s[...], TN)
        dxdt_all = dxdt_s[...]
        dt_all = dt_s[...]
        dxs_ref[...] = dys_s[...] * dsk + dxdt_all * dt_all
        ddt = dxdt_all * xs_ref[...] + dda * a_e
        dalog_ref[...] += jnp.sum(dda * dt_all, axis=0, keepdims=True) * a_e
        draw = ddt * sg_s[...]
        dbias_ref[...] += jnp.sum(draw, axis=0, keepdims=True)
        ddt_ref[...] = _expand_heads(draw, expand, NT)

    GN = G * N
    return _host_call(
        body, name="ssd_bwd", grid=(G, ni),
        in_specs=[z_s, dt_s, xs_s, b_s, c_s, vec_s, vec_s, vec_s, vec_s, z_s, st_s, z_s],
        out_specs=(z_s, z_s, bc_out, bc_out, ddt_out, vec_s, vec_s, vec_s, vec_s),
        out_shape=(jax.ShapeDtypeStruct((T, cf.SINP), BF16), jax.ShapeDtypeStruct((T, cf.CD), F32),
                   jax.ShapeDtypeStruct((T, GN), F32), jax.ShapeDtypeStruct((T, GN), F32),
                   jax.ShapeDtypeStruct((G, T, LANES), F32)) + (jax.ShapeDtypeStruct((1, cf.DI), F32),) * 4,
        scratch=[pltpu.VMEM((N, GW), F32)] + [pltpu.VMEM((tb, GW), F32)] * 6,
        args=(proj, proj, xact, xact, xact, bias_e, alog_e, dskip_e, norm_w, ypre, states, dy),
        sem=("parallel", "arbitrary"), rider=rider)


def _sum_groups(parts, name, into, col_blk):
    G, T, W = parts.shape
    tb = _tile(T, 512, SUBLANES)

    def body(p_ref, into_ref, o_ref):
        acc = p_ref[0]
        for g in range(1, G):
            acc = acc + p_ref[g]
        o_ref[...] = acc.astype(BF16)

    return _pcall(body, name=name, grid=(T // tb,), in_specs=[pl.BlockSpec((G, tb, W), lambda i: (0, i, 0)), ANY],
                  out_specs=pl.BlockSpec((tb, W), lambda i: (i, col_blk)),
                  out_shape=jax.ShapeDtypeStruct(into.shape, BF16), input_output_aliases={1: 0},
                  compiler_params=_cp(("parallel",)))(parts, into)


def _fill_bc(dact, dbm, dcm, name):
    T, GN = dbm.shape
    tb = _tile(T, 512, SUBLANES)
    blk = (dact.shape[1] - 2 * GN) // (2 * GN)
    assert blk * 2 * GN == dact.shape[1] - 2 * GN

    def body(b_ref, c_ref, into_ref, o_ref):
        o_ref[:, :GN] = b_ref[...]
        o_ref[:, GN:] = c_ref[...]

    row = pl.BlockSpec((tb, GN), lambda i: (i, 0))
    return _pcall(body, name=name, grid=(T // tb,), in_specs=[row, row, ANY],
                  out_specs=pl.BlockSpec((tb, 2 * GN), lambda i: (i, blk)),
                  out_shape=jax.ShapeDtypeStruct(dact.shape, F32), input_output_aliases={2: 0},
                  compiler_params=_cp(("parallel",)))(dbm, dcm, dact)


def _mm_gather_nt(a, w_shard, order, name, ti=1024, tj=512, late_rider=None):
    I, R = a.shape
    n = w_shard.shape[0]
    ti = _tile(I, ti, SUBLANES)
    tj = _tile(n, tj, LANES)
    per, ni = n // tj, I // ti
    lr = late_rider if late_rider is not None else _rider_join([])
    n_ra, n_ro = len(lr.args), len(lr.out_shape)

    def body(ord_ref, a_ref, w_ref, *refs):
        rins, (out_ref, gath_ref), routs = refs[:n_ra], refs[n_ra:n_ra + 2], refs[n_ra + 2:n_ra + 2 + n_ro]
        bbuf, tile_sems, send_sems, recv_sems, local_sem = refs[n_ra + 2 + n_ro:n_ra + 7 + n_ro]
        rsems = refs[n_ra + 7 + n_ro:]
        k, i, t = pl.program_id(0), pl.program_id(1), pl.program_id(2)
        x_, y_, c_ = _mesh_pos()
        me = 4 * x_ + 2 * y_ + c_
        sib = (x_, y_, 1 - c_)
        chips = [(1 - x_, y_), (x_, 1 - y_), (1 - x_, 1 - y_)]
        near = [sib] + [(cx, cy, c_) for cx, cy in chips]

        def send_mine(q):
            return _remote(w_ref, gath_ref.at[me], send_sems, recv_sems, q, near[q])

        def from_near(q):
            px, py, pc = near[q]
            return _remote(w_ref, gath_ref.at[4 * px + 2 * py + pc], send_sems, recv_sems, q, near[q])

        def pass_on(j, core):
            rows = gath_ref.at[4 * chips[j][0] + 2 * chips[j][1] + core]
            return _remote(rows, rows, send_sems, recv_sems, 4 + j, sib)

        def block_in(kk):
            return pltpu.make_async_copy(gath_ref.at[ord_ref[kk]], bbuf.at[kk % 2], tile_sems.at[kk % 2])

        mine = pltpu.make_async_copy(w_ref, gath_ref.at[me], local_sem)
        first_tile = t == 0

        @pl.when(jnp.logical_and(jnp.logical_and(k == 0, i == 0), first_tile))
        def _():
            mine.start()
            for q in range(4):
                send_mine(q).start()
            mine.wait()
            block_in(0).start()

        @pl.when(jnp.logical_and(i == 0, first_tile))
        def _():
            block_in(k).wait()

        @pl.when(jnp.logical_and(jnp.logical_and(i == ni - 1, first_tile), k + 1 < N_DEV))
        def _():
            kk = k + 1

            @pl.when(kk == 1)
            def _():
                from_near(0).wait_recv()

            for j in range(3):
                @pl.when(kk == 2 + j)
                def _(j=j):
                    from_near(1 + j).wait_recv()
                    pass_on(j, c_).start()

                @pl.when(kk == 5 + j)
                def _(j=j):
                    pass_on(j, 1 - c_).wait_recv()

            block_in(kk).start()

        if n_ra:
            @pl.when(jnp.logical_and(jnp.logical_and(k == 5, i == 0), first_tile))
            def _():
                _rider_start(lr, rins, routs, rsems)

        cols = pl.ds(pl.multiple_of(t * tj, tj), tj)
        out_ref[...] = _dot(a_ref[...], bbuf[k % 2, cols, :], NT)

        @pl.when(jnp.logical_and(jnp.logical_and(k == N_DEV - 1, i == ni - 1), t == per - 1))
        def _():
            for q in range(4):
                send_mine(q).wait_send()
            for j in range(3):
                pass_on(j, c_).wait_send()
            if n_ra:
                _rider_wait(lr, rins, routs, rsems)

    res = _pcall(
        body, name=name,
        grid_spec=pltpu.PrefetchScalarGridSpec(
            num_scalar_prefetch=1, grid=(N_DEV, ni, per),
            in_specs=[pl.BlockSpec((ti, R), lambda k, i, t, o: (i, 0)), ANY] + [ANY] * n_ra,
            out_specs=[pl.BlockSpec((ti, tj), lambda k, i, t, o: (i, o[k] * per + t)), ANY] + [ANY] * n_ro,
            scratch_shapes=[pltpu.VMEM((2, n, R), BF16), pltpu.SemaphoreType.DMA((2,)), pltpu.SemaphoreType.DMA((7,)),
                            pltpu.SemaphoreType.DMA((7,)), pltpu.SemaphoreType.DMA] + lr.sems),
        out_shape=[jax.ShapeDtypeStruct((I, N_DEV * n), F32), jax.ShapeDtypeStruct((N_DEV, n, R), BF16)] + lr.out_shape,
        input_output_aliases={3 + q: 2 + v for q, v in lr.aliases.items()},
        compiler_params=_cp(("arbitrary",) * 3),
    )(order, a, w_shard, *lr.args)
    return res[0], res[1], list(res[2:])


def _all_gather(x, name):
    def body(x_ref, out_ref, send_sems, recv_sems, local_sem):
        x, y, c = lax.axis_index("x"), lax.axis_index("y"), lax.axis_index("c")
        me, sibling = (x, y, c), (x, y, 1 - c)
        chips = [(1 - x, y), (x, 1 - y), (1 - x, 1 - y)]

        def blk(px, py, pc):
            return out_ref.at[4 * px + 2 * py + pc]

        def copy(k, block, to, src=None):
            return pltpu.make_async_remote_copy(
                src_ref=blk(*block) if src is None else src, dst_ref=blk(*block),
                send_sem=send_sems.at[k], recv_sem=recv_sems.at[k], device_id=to, device_id_type=MESH)

        mine = pltpu.make_async_copy(x_ref, blk(*me), local_sem)
        mine.start()
        first = [copy(0, me, sibling, src=x_ref)]
        first += [copy(1 + j, me, (*chip, c), src=x_ref) for j, chip in enumerate(chips)]
        for cp in first:
            cp.start()
        passed = [copy(4 + j, (*chip, c), sibling) for j, chip in enumerate(chips)]
        for j, chip in enumerate(chips):
            copy(1 + j, (*chip, c), me).wait_recv()
            passed[j].start()
        copy(0, sibling, me).wait_recv()
        for j, chip in enumerate(chips):
            copy(4 + j, (*chip, 1 - c), me).wait_recv()
        for cp in first + passed:
            cp.wait_send()
        mine.wait()

    return _pcall(
        body, name=name, in_specs=[ANY], out_specs=ANY,
        out_shape=jax.ShapeDtypeStruct((N_DEV,) + x.shape, x.dtype),
        scratch_shapes=[pltpu.SemaphoreType.DMA((7,)), pltpu.SemaphoreType.DMA((7,)), pltpu.SemaphoreType.DMA],
    )(x)


def _core_sum(g, recv, idx, name):
    _, _, a, b = g.shape
    n, blk, at = _plane_tiles(a, b, 2 * 2**20)

    def body(idx_ref, g_ref, r_ref, p_ref, own_ref):
        s = g_ref[...] + r_ref[...]
        p_ref[...] = s.astype(BF16)

        @pl.when(pl.program_id(1) == idx_ref[1])
        def _():
            own_ref[...] = s

    return _pcall(
        body, name=name,
        grid_spec=pltpu.PrefetchScalarGridSpec(
            num_scalar_prefetch=1, grid=(n, 4),
            in_specs=[pl.BlockSpec((None, None) + blk, lambda i, q, idx: (q, idx[0]) + at(i)),
                      pl.BlockSpec((None,) + blk, lambda i, q, idx: (q,) + at(i))],
            out_specs=(pl.BlockSpec((None,) + blk, lambda i, q, idx: (q,) + at(i)),
                       pl.BlockSpec(blk, lambda i, q, idx: at(i)))),
        out_shape=(jax.ShapeDtypeStruct((4, a, b), BF16), jax.ShapeDtypeStruct((a, b), F32)),
        compiler_params=_cp(("parallel", "arbitrary")),
    )(idx, g, recv)


def _plane_tiles(a, b, f32_bytes, row_off=0):
    if a % (2 * SUBLANES) == 0:
        tr = _tile(a, max(2 * SUBLANES, f32_bytes // (4 * b) // (2 * SUBLANES) * (2 * SUBLANES)), 2 * SUBLANES)
        return a // tr, (tr, b), lambda i: (i + row_off * (a // tr), 0)
    assert row_off == 0
    tc = _tile(b, max(LANES, f32_bytes // (4 * a) // LANES * LANES), LANES)
    return b // tc, (a, tc), lambda i: (0, i)


def _adam_math(w, g, m, v):
    m = ADAM_B1 * m + (1.0 - ADAM_B1) * g
    v = ADAM_B2 * v + (1.0 - ADAM_B2) * (g * g)
    m_hat = m / (1.0 - ADAM_B1 ** ADAM_STEP)
    v_hat = v / (1.0 - ADAM_B2 ** ADAM_STEP)
    delta = -ADAM_LR * (m_hat / (jnp.sqrt(v_hat) + ADAM_EPS) + ADAM_WD * w)
    return delta, m, v


def _chip_sum_adam(own, recv, w, m, v, layer, name, into=None, row_part=(0, 1)):
    a, b = own.shape
    n = w.shape[0]
    assert w.shape[1] == a * row_part[1]
    nt, blk, at = _plane_tiles(a, b, 2**20)
    _, _, at_w = _plane_tiles(a, b, 2**20, row_part[0])
    wspec = pl.BlockSpec((None,) + blk, lambda i: (layer,) + at_w(i))
    ospec = pl.BlockSpec(blk, at)
    n_into = 0 if into is None else 4

    def body(own_ref, r_ref, w_ref, m_ref, v_ref, *rest):
        g_ref, d_ref, mo_ref, vo_ref = rest[n_into:]
        g = own_ref[...]
        for k in range(3):
            g = g + r_ref[k].astype(F32)
        g_ref[...] = g
        d_ref[...], mo_ref[...], vo_ref[...] = _adam_math(w_ref[...], g, m_ref[...], v_ref[...])

    return _pcall(
        body, name=name, grid=(nt,),
        in_specs=[ospec, pl.BlockSpec((3,) + blk, lambda i: (0,) + at(i)), wspec, wspec, wspec] + [ANY] * n_into,
        out_specs=(wspec,) * 4, out_shape=(jax.ShapeDtypeStruct(w.shape, F32),) * 4,
        input_output_aliases={5 + k: k for k in range(n_into)},
        compiler_params=_cp(("parallel",)),
    )(own, recv, w, m, v, *(into or ()))


def _all_reduce_small(x, n_fold, fold_w, name):
    R, W = x.shape

    def body(x_ref, out_ref, buf, send_sems, recv_sems):
        xx, y, c = lax.axis_index("x"), lax.axis_index("y"), lax.axis_index("c")
        me = 4 * xx + 2 * y + c
        buf[me] = x_ref[...]
        copies = []
        for k in range(1, N_DEV):
            px, py, pc = xx ^ (k >> 2), y ^ ((k >> 1) & 1), c ^ (k & 1)
            copies.append(pltpu.make_async_remote_copy(
                src_ref=x_ref, dst_ref=buf.at[me], send_sem=send_sems.at[k - 1], recv_sem=recv_sems.at[k - 1],
                device_id=(px, py, pc), device_id_type=MESH))
        for cp in copies:
            cp.start()
        for cp in copies:
            cp.wait()
        acc = buf[0]
        for j in range(1, N_DEV):
            acc = acc + buf[j]
        out_ref[...] = acc
        if n_fold:
            l0 = lax.broadcasted_iota(jnp.int32, (W, W), 0) // fold_w
            l1 = lax.broadcasted_iota(jnp.int32, (W, W), 1)
            fold = (l0 == l1).astype(F32)
            out_ref[R - n_fold:, :] = _dot(acc[R - n_fold:, :], fold, NN, HI)

    return _pcall(
        body, name=name, in_specs=[pl.BlockSpec(memory_space=pltpu.VMEM)],
        out_specs=pl.BlockSpec(memory_space=pltpu.VMEM), out_shape=jax.ShapeDtypeStruct((R, W), F32),
        scratch_shapes=[pltpu.VMEM((N_DEV, R, W), F32), pltpu.SemaphoreType.DMA((N_DEV - 1,)),
                        pltpu.SemaphoreType.DMA((N_DEV - 1,))],
        compiler_params=pltpu.CompilerParams(vmem_limit_bytes=VMEM_LIMIT),
    )(x)


def _adam_small(w, g, m, v, name):
    def body(w_ref, g_ref, m_ref, v_ref, d_ref, mo_ref, vo_ref):
        d_ref[...], mo_ref[...], vo_ref[...] = _adam_math(w_ref[...], g_ref[...], m_ref[...], v_ref[...])

    return _pcall(body, name=name, out_shape=(jax.ShapeDtypeStruct(w.shape, F32),) * 3)(w, g, m, v)


def kernel(x, norm_mix_pre, norm_mix_post, norm_ffn_pre, norm_ffn_post, ret_w_in, ret_gn_w, ret_w_out, ssd_w_in, ssd_conv_w, ssd_conv_b, ssd_dt_bias, ssd_a_log, ssd_d, ssd_norm_w, ssd_w_out, mlp_w_up, mlp_w_down, loss_target, m_norm_mix_pre, m_norm_mix_post, m_norm_ffn_pre, m_norm_ffn_post, m_ret_w_in, m_ret_gn_w, m_ret_w_out, m_ssd_w_in, m_ssd_conv_w, m_ssd_conv_b, m_ssd_dt_bias, m_ssd_a_log, m_ssd_d, m_ssd_norm_w, m_ssd_w_out, m_mlp_w_up, m_mlp_w_down, v_norm_mix_pre, v_norm_mix_post, v_norm_ffn_pre, v_norm_ffn_post, v_ret_w_in, v_ret_gn_w, v_ret_w_out, v_ssd_w_in, v_ssd_conv_w, v_ssd_conv_b, v_ssd_dt_bias, v_ssd_a_log, v_ssd_d, v_ssd_norm_w, v_ssd_w_out, v_mlp_w_up, v_mlp_w_down):
    cf = _cfg()
    T, D = cf.T, cf.D
    ax, ay, ac = lax.axis_index("x"), lax.axis_index("y"), lax.axis_index("c")
    my_dev = 4 * ax + 2 * ay + ac
    idx = jnp.stack([ac, 2 * ax + ay]).astype(jnp.int32)

    tr12 = lambda t: jnp.swapaxes(t, 1, 2)
    chips_of = lambda ws: _rider_join([_r_gather_chips(w.astype(BF16)) for w in ws])
    cores_of = lambda bufs: _rider_join([_r_gather_cores(b) for b in bufs])
    cw, cbw, nww = cf.CD // N_DEV, cf.CD // N_DEV, cf.DI // N_DEV
    small = jnp.concatenate([ssd_conv_w[0], ssd_conv_b, jnp.pad(ssd_norm_w, ((0, 0), (0, cw - nww))),
                             jnp.zeros((2, cw), F32)], axis=0)
    small = _all_gather(small, "ag_ssd_small")
    conv_w = jnp.transpose(small[:, :SSD_CONV_W, :], (1, 0, 2)).reshape(SSD_CONV_W, cf.CD)
    conv_b = small[:, SSD_CONV_W, :].reshape(1, cf.CD)
    ssd_nw = small[:, SSD_CONV_W + 1, :nww].reshape(1, cf.DI)

    half = cf.DK // 2
    inv_freq = ROPE_BASE ** (-jnp.arange(half, dtype=F32) / half)
    ang = jnp.arange(T).astype(F32)[:, None] * inv_freq[None, :]
    cos, sin = jnp.cos(ang), jnp.sin(ang)
    lgam = jnp.log1p(-jnp.exp2(-5.0 - jnp.arange(cf.H, dtype=F32)))
    rep = lambda p: jnp.repeat(p.reshape(1, cf.SH), cf.P, axis=1)
    bias_e, alog_e, dskip_e = rep(ssd_dt_bias), rep(ssd_a_log), rep(ssd_d)

    h0 = x.reshape(T, D)
    tgt = loss_target.reshape(T, D)
    nrm = lambda p, i: p[i:i + 1]

    u0 = _rms_fwd(h0, nrm(norm_mix_pre, 0), "rms_fwd0")
    order = jnp.stack([my_dev ^ mask for mask in (0, 1, 4, 2, 6, 5, 3, 7)]).astype(jnp.int32)
    proj0, w_ri, part_a = _mm_gather_nt(u0, ret_w_in[0].T.astype(BF16), order, "mm_ret_in",
                                        late_rider=chips_of([ret_w_out[0]]))
    w_ri = w_ri.reshape(cf.RIN, D)
    (y0, o0, st0), got = _ret_fwd(cf, proj0, cos, sin, ret_gn_w, lgam, rider=_rider_join(
        [cores_of(part_a), chips_of([mlp_w_up[0].T, mlp_w_down[0]])]))
    w_ro = got[0].reshape(cf.V, D)
    m0, got = _mm(y0, w_ro, kind="nn", name="mm_ret_out", tr=cf.V, rider=cores_of(got[1:]))
    w_up0, w_dn0 = got[0].reshape(cf.FF, D), got[1].reshape(cf.FF, D)
    h1, u1 = _resid_fwd(h0, m0, nrm(norm_mix_post, 0), nrm(norm_ffn_pre, 0), "resid_fwd0")
    (sq0, act0), part_b = _mm(u1, w_up0, kind="nt", name="mm_up0", epi="relu2", rider=chips_of([tr12(ssd_w_in)[0]]))
    f0, got = _mm(sq0, w_dn0, kind="nn", name="mm_down0", tj=1024, rider=_rider_join(
        [cores_of(part_b), chips_of([ssd_w_out[0], mlp_w_up[1].T])]))
    w_si = jnp.pad(got[0].reshape(cf.SIN, D), ((0, cf.SINP - cf.SIN), (0, 0)))
    h2, u2 = _resid_fwd(h1, f0, nrm(norm_ffn_post, 0), nrm(norm_mix_pre, 1), "resid_fwd1")
    proj1, got = _mm(u2, w_si, kind="nt", name="mm_ssd_in", tj=1152, rider=_rider_join(
        [cores_of(got[1:]), chips_of([mlp_w_down[1]])]))
    w_so, w_up1 = got[0].reshape(cf.DI, D), got[1].reshape(cf.FF, D)
    xact, xpre = _conv_fwd(cf, proj1, conv_w, conv_b)
    y1, yp1, st1 = _ssd_fwd(cf, proj1, xact, bias_e, alog_e, dskip_e, ssd_nw)
    m1, (w_dn1,) = _mm(y1, w_so, kind="nn", name="mm_ssd_out", tr=cf.DI, rider=cores_of(got[2:]))
    w_up, w_dn = [w_up0, w_up1], [w_dn0, w_dn1.reshape(cf.FF, D)]
    h3, u3 = _resid_fwd(h2, m1, nrm(norm_mix_post, 1), nrm(norm_ffn_pre, 1), "resid_fwd2")
    sq1, act1 = _mm(u3, w_up[1], kind="nt", name="mm_up1", epi="relu2")
    f1 = _mm(sq1, w_dn[1], kind="nn", name="mm_down1", tj=1024)
    g4, lsum = _final_fwd(h3, f1, nrm(norm_ffn_post, 1), tgt, "final_fwd")
    loss = lax.psum(0.5 * jnp.sum(lsum) / D, ("x", "y", "c"))

    as4 = lambda g: g.reshape(4, 2, g.shape[1], g.shape[2])
    swap_cores = lambda g: _r_swap_cores(as4(g))
    core_sum = lambda g, recv, name: _core_sum(as4(g), recv, idx, name + "_core_sum")

    def mlp_bwd(l, df, u, sq, act, rider=None, then=None):
        dpre = _mm(df, w_dn[l], kind="nt", name=f"mm_dpre{l}", out_dtype=BF16, epi="mul2act", epi_in=act, rider=rider)
        dpre, got = dpre if rider is not None else (dpre, [])
        g_dn = _mm(sq, df, kind="tn", name=f"mm_gdown{l}", tj=1024, tr=4096, rider=then(got) if then else None)
        g_dn, got = g_dn if then else (g_dn, [])
        g_dn = g_dn.reshape(N_DEV, cf.FF // N_DEV, D)
        du, (rc,) = _mm(dpre, w_up[l], kind="nn", name=f"mm_du_mlp{l}", tj=1024, tr=4096, rider=swap_cores(g_dn))
        part, own = core_sum(g_dn, rc, f"rs_mlp_down{l}")
        g_up, (r2,) = _mm(u, dpre, kind="tn", name=f"mm_gup{l}", tj=1024, tr=4096, out_nblk=N_DEV, rider=_r_swap_chips(part))
        return du, g_up, own, r2, got

    df1, g_nfpost1 = _norm_bwd(g4, "norm_bwd4", m=f1, w_post=nrm(norm_ffn_post, 1))
    du3, g_up1, own_dn1, r2_dn1, _ = mlp_bwd(1, df1, u3, sq1, act1, None)
    gh3, dm1, g_nfpre1, g_nmpost1 = _norm_bwd(g4, "norm_bwd3", du=du3, h=h3, w_pre=nrm(norm_ffn_pre, 1),
                                              m=m1, w_post=nrm(norm_mix_post, 1))
    dy1, (rc,) = _mm(dm1, w_so, kind="nt", name="mm_dy_ssd", rider=swap_cores(g_up1))
    part, own_up1 = core_sum(g_up1, rc, "rs_mlp_up1")
    g_so = _mm(y1, dm1, kind="tn", name="mm_g_ssd_out", tj=1024, tr=4096).reshape(N_DEV, cf.DI // N_DEV, D)
    (dz, dxs, dbm, dcm, ddt_parts, g_ssd_nw, g_dskip_e, g_alog_e, g_bias_e), (r2_up1, rc) = _ssd_bwd(
        cf, proj1, xact, bias_e, alog_e, dskip_e, ssd_nw, yp1, st1, dy1,
        rider=_rider_join([_r_swap_chips(part), swap_cores(g_so)]))
    part, own_so = core_sum(g_so, rc, "rs_ssd_w_out")
    dact = _fill_bc(dxs, dbm, dcm, "ssd_dact_fill")
    dpre1, g_conv_w, g_conv_b = _conv_bwd_pre(cf, proj1, xpre, dact)
    assert cf.SINP == cf.DI + cf.CD + LANES
    dproj1 = _conv_bwd_x(cf, dpre1, conv_w, dz)
    dproj1 = _sum_groups(ddt_parts, "ssd_ddt_sum", dproj1, (cf.DI + cf.CD) // LANES)
    du2, (r2_so,) = _mm(dproj1, w_si, kind="nn", name="mm_du_ssd", tj=1024, tr=3456, rider=_r_swap_chips(part))
    g_si = _mm(dproj1, u2, kind="tn", name="mm_g_ssd_in", ti=1152, tj=1024, tr=2048)
    g_si = g_si[:cf.SIN].reshape(N_DEV, cf.SIN // N_DEV, D)
    gh2, df0, g_nmpre1, g_nfpost0 = _norm_bwd(gh3, "norm_bwd2", du=du2, h=h2, w_pre=nrm(norm_mix_pre, 1),
                                              m=f0, w_post=nrm(norm_ffn_post, 0))
    own_si = []

    def si_chips(got):
        part, own = core_sum(g_si, got[0], "rs_ssd_w_in")
        own_si.append(own)
        return _r_swap_chips(part)

    du1, g_up0, own_dn0, r2_dn0, (r2_si,) = mlp_bwd(0, df0, u1, sq0, act0, swap_cores(g_si), si_chips)
    own_si = own_si[0]
    gh1, dm0, g_nfpre0, g_nmpost0 = _norm_bwd(gh2, "norm_bwd1", du=du1, h=h1, w_pre=nrm(norm_ffn_pre, 0),
                                              m=m0, w_post=nrm(norm_mix_post, 0))
    dy0, (rc,) = _mm(dm0, w_ro, kind="nt", name="mm_dy_ret", rider=swap_cores(g_up0))
    part, own_up0 = core_sum(g_up0, rc, "rs_mlp_up0")
    g_ro = _mm(y0, dm0, kind="tn", name="mm_g_ret_out", tj=1024, tr=4096).reshape(N_DEV, cf.V // N_DEV, D)
    (dproj0, g_gn), (r2_up0, rc) = _ret_bwd(cf, proj0, cos, sin, ret_gn_w, lgam, o0, st0, dy0,
                                            rider=_rider_join([_r_swap_chips(part), swap_cores(g_ro)]))
    part, own_ro = core_sum(g_ro, rc, "rs_ret_w_out")
    rin8 = cf.RIN // N_DEV
    g_ri0, (r2_ro,) = _mm(u0, dproj0, kind="tn", name="mm_g_ret_in0", ti=512, tj=rin8, tr=4096, out_nblk=N_DEV,
                          i_part=(0, 2), rider=_r_swap_chips(part))
    g_ri1, (rc,) = _mm(u0, dproj0, kind="tn", name="mm_g_ret_in1", ti=512, tj=rin8, tr=4096, out_nblk=N_DEV,
                       i_part=(1, 2), rider=swap_cores(g_ri0))
    part, own_ri0 = core_sum(g_ri0, rc, "rs_ret_w_in0")
    du0, (r2_ri0, rc) = _mm(dproj0, w_ri, kind="nn", name="mm_du_ret", tj=1024, tr=4096,
                            rider=_rider_join([_r_swap_chips(part), swap_cores(g_ri1)]))
    part, own_ri1 = core_sum(g_ri1, rc, "rs_ret_w_in1")
    (grad_x, g_nmpre0), (r2_ri1,) = _norm_bwd(gh1, "norm_bwd0", du=du0, h=h0, w_pre=nrm(norm_mix_pre, 0),
                                              rider=_r_swap_chips(part))

    g_nmpre = jnp.concatenate([g_nmpre0, g_nmpre1], axis=0)
    g_nmpost = jnp.concatenate([g_nmpost0, g_nmpost1], axis=0)
    g_nfpre = jnp.concatenate([g_nfpre0, g_nfpre1], axis=0)
    g_nfpost = jnp.concatenate([g_nfpost0, g_nfpost1], axis=0)
    segs = [g_nmpre, g_nmpost, g_nfpre, g_nfpost, g_gn, g_conv_w, g_conv_b, g_ssd_nw, g_bias_e, g_alog_e, g_dskip_e]
    flat = jnp.concatenate([s.reshape(-1, LANES) for s in segs], axis=0)
    n_fold = 3 * cf.DI // LANES
    red = _all_reduce_small(flat, n_fold, cf.P, "all_reduce_small")
    outs, r0 = [], 0
    for s in segs:
        nr = s.size // LANES
        outs.append(red[r0:r0 + nr])
        r0 += nr
    (g_nmpre, g_nmpost, g_nfpre, g_nfpost) = [o.reshape(DEPTH, D) for o in outs[:4]]
    g_gn = outs[4].reshape(1, cf.V)
    g_conv_w = lax.dynamic_slice_in_dim(outs[5].reshape(SSD_CONV_W, cf.CD), my_dev * cw, cw, axis=1)[None]
    g_conv_b = lax.dynamic_slice_in_dim(outs[6].reshape(1, cf.CD), my_dev * cbw, cbw, axis=1)
    g_ssd_nw = lax.dynamic_slice_in_dim(outs[7].reshape(1, cf.DI), my_dev * nww, nww, axis=1)
    per_row = LANES // cf.P
    g_bias, g_alog, g_dskip = [o[:, :per_row].reshape(1, cf.SH) for o in outs[8:]]

    def rs(own, recv, w, m, v, layer, name, into=None, row_part=(0, 1)):
        return _chip_sum_adam(own, recv, w, m, v, layer, name + "_adam", into, row_part)

    r_ri = rs(own_ri0, r2_ri0, ret_w_in, m_ret_w_in, v_ret_w_in, 0, "rs_ret_w_in0", None, (0, 2))
    r_ri = rs(own_ri1, r2_ri1, ret_w_in, m_ret_w_in, v_ret_w_in, 0, "rs_ret_w_in1", r_ri, (1, 2))
    r_ro = rs(own_ro, r2_ro, ret_w_out, m_ret_w_out, v_ret_w_out, 0, "rs_ret_w_out")
    r_si = [tr12(r) for r in rs(own_si, r2_si, tr12(ssd_w_in), tr12(m_ssd_w_in), tr12(v_ssd_w_in), 0, "rs_ssd_w_in")]
    r_so = rs(own_so, r2_so, ssd_w_out, m_ssd_w_out, v_ssd_w_out, 0, "rs_ssd_w_out")
    r_up = rs(own_up1, r2_up1, mlp_w_up, m_mlp_w_up, v_mlp_w_up, 1, "rs_mlp_up1")
    r_up = rs(own_up0, r2_up0, mlp_w_up, m_mlp_w_up, v_mlp_w_up, 0, "rs_mlp_up0", r_up)
    r_dn = rs(own_dn1, r2_dn1, mlp_w_down, m_mlp_w_down, v_mlp_w_down, 1, "rs_mlp_down1")
    r_dn = rs(own_dn0, r2_dn0, mlp_w_down, m_mlp_w_down, v_mlp_w_down, 0, "rs_mlp_down0", r_dn)
    lead = list

    def small_adam(w, g, m, v, name):
        return [g] + list(_adam_small(w, g, m, v, name))

    results = {
        "norm_mix_pre": small_adam(norm_mix_pre, g_nmpre, m_norm_mix_pre, v_norm_mix_pre, "adam_nmpre"),
        "norm_mix_post": small_adam(norm_mix_post, g_nmpost, m_norm_mix_post, v_norm_mix_post, "adam_nmpost"),
        "norm_ffn_pre": small_adam(norm_ffn_pre, g_nfpre, m_norm_ffn_pre, v_norm_ffn_pre, "adam_nfpre"),
        "norm_ffn_post": small_adam(norm_ffn_post, g_nfpost, m_norm_ffn_post, v_norm_ffn_post, "adam_nfpost"),
        "ret_w_in": lead(r_ri),
        "ret_gn_w": small_adam(ret_gn_w, g_gn, m_ret_gn_w, v_ret_gn_w, "adam_gn"),
        "ret_w_out": lead(r_ro),
        "ssd_w_in": lead(r_si),
        "ssd_conv_w": small_adam(ssd_conv_w, g_conv_w, m_ssd_conv_w, v_ssd_conv_w, "adam_conv_w"),
        "ssd_conv_b": small_adam(ssd_conv_b, g_conv_b, m_ssd_conv_b, v_ssd_conv_b, "adam_conv_b"),
        "ssd_dt_bias": small_adam(ssd_dt_bias, g_bias, m_ssd_dt_bias, v_ssd_dt_bias, "adam_dt_bias"),
        "ssd_a_log": small_adam(ssd_a_log, g_alog, m_ssd_a_log, v_ssd_a_log, "adam_a_log"),
        "ssd_d": small_adam(ssd_d, g_dskip, m_ssd_d, v_ssd_d, "adam_d"),
        "ssd_norm_w": small_adam(ssd_norm_w, g_ssd_nw, m_ssd_norm_w, v_ssd_norm_w, "adam_ssd_nw"),
        "ssd_w_out": lead(r_so),
        "mlp_w_up": r_up,
        "mlp_w_down": r_dn,
    }
    names = list(results)
    out = [loss, grad_x.reshape(1, T, D)]
    for k in range(4):
        out += [results[n][k] for n in names]
    return tuple(out)
```

```python
import functools
import math
import types

import jax
import jax.numpy as jnp
from jax import lax
from jax.experimental import pallas as pl
from jax.experimental.pallas import tpu as pltpu

F32 = jnp.float32
BF16 = jnp.bfloat16
HI = lax.Precision.HIGHEST
NN = (((1,), (0,)), ((), ()))
NT = (((1,), (1,)), ((), ()))
TN = (((0,), (0,)), ((), ()))
MESH = pl.DeviceIdType.MESH

V7X_VMEM_BYTES = 64 * 2**20
VMEM_LIMIT = V7X_VMEM_BYTES - 8 * 2**20
LANES = 128
SUBLANES = 8
N_DEV = 8

D_MODEL = 2048
SEQ = 8192
DEPTH = 2
CHUNK = 64
RMS_EPS = 1e-6
RET_HEAD_DK = 256
ROPE_BASE = 10000.0
GN_EPS = 1e-5
SSD_HEADDIM = 64
SSD_HEADS_PER_GROUP = 8
SSD_STATE = 128
SSD_CONV_W = 4
ADAM_LR = 0.001
ADAM_B1 = 0.9
ADAM_B2 = 0.999
ADAM_EPS = 1e-08
ADAM_WD = 0.01
ADAM_STEP = 10


def _cfg():
    c = types.SimpleNamespace()
    c.D, c.T, c.L = D_MODEL, SEQ, CHUNK
    c.DK = RET_HEAD_DK
    c.H = c.D // c.DK
    c.QK = c.H * c.DK
    c.DV = 2 * c.DK
    c.V = c.H * c.DV
    c.RIN = 2 * c.QK + 2 * c.V
    c.DI = 2 * c.D
    c.P = SSD_HEADDIM
    c.SH = c.DI // c.P
    c.RG = SSD_HEADS_PER_GROUP
    c.G = c.SH // c.RG
    c.GW = c.RG * c.P
    c.N = SSD_STATE
    c.CD = c.DI + 2 * c.G * c.N
    c.SIN = c.DI + c.CD + c.SH
    c.SINP = -(-c.SIN // LANES) * LANES
    c.FF = 4 * c.D
    c.NC = c.T // c.L
    return c


def _pcall(body, **kw):
    return pl.pallas_call(body, **kw)


def _cp(sem=None):
    return pltpu.CompilerParams(dimension_semantics=sem, vmem_limit_bytes=VMEM_LIMIT)


def _tile(n, pref, mult):
    if n <= pref:
        return n
    t = (pref // mult) * mult
    while t >= mult:
        if n % t == 0:
            return t
        t -= mult
    return n


def _dot(a, b, dn=NN, prec=None):
    return lax.dot_general(a, b, dn, precision=prec, preferred_element_type=F32)


ANY = pl.BlockSpec(memory_space=pl.ANY)


def _mesh_pos():
    return lax.axis_index("x"), lax.axis_index("y"), lax.axis_index("c")


def _rider_join(riders):
    j = types.SimpleNamespace(args=[], out_shape=[], aliases={}, sems=[])
    parts = []
    for r in riders:
        a0, o0, s0 = len(j.args), len(j.out_shape), len(j.sems)
        parts.append((r, a0, o0, s0))
        j.aliases.update({a0 + k: o0 + v for k, v in r.aliases.items()})
        j.args += r.args
        j.out_shape += r.out_shape
        j.sems += r.sems

    def make(rins, routs, sems):
        sends, recvs, locs = [], [], []
        for r, a0, o0, s0 in parts:
            s, rc, lc = r.make(rins[a0:a0 + len(r.args)], routs[o0:o0 + len(r.out_shape)], sems[s0:s0 + len(r.sems)])
            sends += s
            recvs += rc
            locs += lc
        return sends, recvs, locs

    j.make = make
    return j


def _rider_start(rider, rins, routs, sems):
    sends, _, locs = rider.make(rins, routs, sems)
    for cp in locs + sends:
        cp.start()


def _rider_wait(rider, rins, routs, sems):
    sends, recvs, locs = rider.make(rins, routs, sems)
    for cp in recvs:
        cp.wait_recv()
    for cp in sends:
        cp.wait_send()
    for cp in locs:
        cp.wait()


def _host_call(body, *, name, grid, in_specs, out_specs, out_shape, scratch, args, sem, rider=None):
    in_specs, out_specs, out_shape, scratch, args = map(list, (in_specs, out_specs, out_shape, scratch, args))
    if rider is None:
        res = _pcall(body, name=name, grid=grid, in_specs=in_specs, out_specs=out_specs, out_shape=out_shape,
                     scratch_shapes=scratch, compiler_params=_cp(sem))(*args)
        return list(res), []
    n_in, n_out, n_scr = len(args), len(out_shape), len(scratch)
    n_ra, n_ro = len(rider.args), len(rider.out_shape)

    def full(*refs):
        ins, rins = refs[:n_in], refs[n_in:n_in + n_ra]
        p = n_in + n_ra
        outs, routs = refs[p:p + n_out], refs[p + n_out:p + n_out + n_ro]
        p += n_out + n_ro
        scr, sems = refs[p:p + n_scr], refs[p + n_scr:]
        first = functools.reduce(jnp.logical_and, [pl.program_id(k) == 0 for k in range(len(grid))])
        last = functools.reduce(jnp.logical_and, [pl.program_id(k) == grid[k] - 1 for k in range(len(grid))])

        @pl.when(first)
        def _():
            _rider_start(rider, rins, routs, sems)

        body(*ins, *outs, *scr)

        @pl.when(last)
        def _():
            _rider_wait(rider, rins, routs, sems)

    res = _pcall(full, name=name, grid=grid, in_specs=in_specs + [ANY] * n_ra, out_specs=out_specs + [ANY] * n_ro,
                 out_shape=out_shape + rider.out_shape, scratch_shapes=scratch + rider.sems,
                 input_output_aliases={n_in + k: n_out + v for k, v in rider.aliases.items()},
                 compiler_params=_cp(("arbitrary",) * len(grid)))(*args, *rider.args)
    return list(res[:n_out]), list(res[n_out:])


def _comm(rider, name):
    n_ra, n_ro = len(rider.args), len(rider.out_shape)

    def body(*refs):
        rins, routs, sems = refs[:n_ra], refs[n_ra:n_ra + n_ro], refs[n_ra + n_ro:]
        _rider_start(rider, rins, routs, sems)
        _rider_wait(rider, rins, routs, sems)

    return list(_pcall(body, name=name, in_specs=[ANY] * n_ra, out_specs=[ANY] * n_ro, out_shape=rider.out_shape,
                       scratch_shapes=rider.sems, input_output_aliases=dict(rider.aliases))(*rider.args))


def _remote(src, dst, send_sems, recv_sems, k, to):
    return pltpu.make_async_remote_copy(src_ref=src, dst_ref=dst, send_sem=send_sems.at[k], recv_sem=recv_sems.at[k],
                                        device_id=to, device_id_type=MESH)


def _r_gather_chips(x):
    def make(rins, routs, sems):
        (x_ref,), (out_ref,), (send_sems, recv_sems, local_sem) = rins, routs, sems
        x_, y_, c_ = _mesh_pos()
        me = 4 * x_ + 2 * y_ + c_
        peers = [(x_, y_, 1 - c_), (1 - x_, y_, c_), (x_, 1 - y_, c_), (1 - x_, 1 - y_, c_)]
        sends = [_remote(x_ref, out_ref.at[me], send_sems, recv_sems, k, to) for k, to in enumerate(peers)]
        recvs = [_remote(x_ref, out_ref.at[4 * px + 2 * py + pc], send_sems, recv_sems, k, (px, py, pc))
                 for k, (px, py, pc) in enumerate(peers)]
        return sends, recvs, [pltpu.make_async_copy(x_ref, out_ref.at[me], local_sem)]

    return types.SimpleNamespace(
        args=[x], out_shape=[jax.ShapeDtypeStruct((N_DEV,) + x.shape, x.dtype)], aliases={},
        sems=[pltpu.SemaphoreType.DMA((4,)), pltpu.SemaphoreType.DMA((4,)), pltpu.SemaphoreType.DMA], make=make)


def _r_gather_cores(buf):
    def make(rins, routs, sems):
        (out_ref,), (send_sems, recv_sems) = routs, sems
        x_, y_, c_ = _mesh_pos()
        chips = [(1 - x_, y_), (x_, 1 - y_), (1 - x_, 1 - y_)]
        sends = [_remote(out_ref.at[4 * cx + 2 * cy + c_], out_ref.at[4 * cx + 2 * cy + c_], send_sems, recv_sems, k,
                         (x_, y_, 1 - c_)) for k, (cx, cy) in enumerate(chips)]
        recvs = [_remote(out_ref.at[4 * cx + 2 * cy + 1 - c_], out_ref.at[4 * cx + 2 * cy + 1 - c_], send_sems,
                         recv_sems, k, (x_, y_, 1 - c_)) for k, (cx, cy) in enumerate(chips)]
        return sends, recvs, []

    return types.SimpleNamespace(
        args=[buf], out_shape=[jax.ShapeDtypeStruct(buf.shape, buf.dtype)], aliases={0: 0},
        sems=[pltpu.SemaphoreType.DMA((3,)), pltpu.SemaphoreType.DMA((3,))], make=make)


def _r_swap_cores(g, rows=None):
    blk_shape = g.shape[2:] if rows is None else (rows, g.shape[1])

    def make(rins, routs, sems):
        (g_ref,), (out_ref,), (send_sems, recv_sems) = rins, routs, sems
        x_, y_, c_ = _mesh_pos()

        def blk(q):
            if rows is None:
                return g_ref.at[q, 1 - c_]
            return g_ref.at[pl.ds(pl.multiple_of((2 * q + 1 - c_) * rows, SUBLANES), rows), :]

        cps = [_remote(blk(q), out_ref.at[q], send_sems, recv_sems, q, (x_, y_, 1 - c_)) for q in range(4)]
        return cps, cps, []

    return types.SimpleNamespace(
        args=[g], out_shape=[jax.ShapeDtypeStruct((4,) + blk_shape, g.dtype)], aliases={},
        sems=[pltpu.SemaphoreType.DMA((4,)), pltpu.SemaphoreType.DMA((4,))], make=make)


def _r_swap_chips(p):
    def make(rins, routs, sems):
        (p_ref,), (out_ref,), (send_sems, recv_sems) = rins, routs, sems
        x_, y_, c_ = _mesh_pos()
        chips = [(1 - x_, y_), (x_, 1 - y_), (1 - x_, 1 - y_)]
        cps = [_remote(p_ref.at[2 * cx + cy], out_ref.at[k], send_sems, recv_sems, k, (cx, cy, c_))
               for k, (cx, cy) in enumerate(chips)]
        return cps, cps, []

    return types.SimpleNamespace(
        args=[p], out_shape=[jax.ShapeDtypeStruct((3,) + p.shape[1:], p.dtype)], aliases={},
        sems=[pltpu.SemaphoreType.DMA((3,)), pltpu.SemaphoreType.DMA((3,))], make=make)


def _sigmoid(x):
    return 0.5 * (jnp.tanh(0.5 * x) + 1.0)


def _mm(a, b, *, kind, name, out_dtype=F32, ti=1024, tj=512, tr=2048, epi=None, epi_in=None, out_nblk=1, rider=None,
        i_part=(0, 1)):
    b_blk = b.ndim == 3
    if kind == "tn":
        R, I = a.shape
        I //= i_part[1]
    else:
        I, R = a.shape
    if kind == "nn":
        J = b.shape[1] if not b_blk else b.shape[0] * b.shape[2]
        nb_inner = b.shape[2] if b_blk else J
        r_inner = R
    elif kind == "nt":
        J = b.shape[0] if not b_blk else b.shape[1]
        nb_inner = J
        r_inner = b.shape[2] if b_blk else R
    else:
        J = b.shape[1]
        nb_inner = J
        r_inner = R
    out_inner = J // out_nblk
    ti = _tile(I, ti, LANES if kind == "tn" else SUBLANES)
    tj = _tile(min(nb_inner, out_inner), tj, LANES)
    assert nb_inner % tj == 0 and out_inner % tj == 0 and J % tj == 0
    tr = _tile(r_inner, tr, LANES)
    assert R % tr == 0
    ni, nj, nr = I // ti, J // tj, R // tr
    dn = {"nn": NN, "nt": NT, "tn": TN}[kind]

    if kind == "tn":
        i_off = i_part[0] * ni
        a_spec = pl.BlockSpec((tr, ti), lambda i, j, r: (r, i + i_off))
    else:
        a_spec = pl.BlockSpec((ti, tr), lambda i, j, r: (i, r))
    if kind == "nn":
        if b_blk:
            per = nb_inner // tj
            b_spec = pl.BlockSpec((None, tr, tj), lambda i, j, r: (j // per, r, j % per))
        else:
            b_spec = pl.BlockSpec((tr, tj), lambda i, j, r: (r, j))
    elif kind == "nt":
        if b_blk:
            per = r_inner // tr
            b_spec = pl.BlockSpec((None, tj, tr), lambda i, j, r: (r // per, j, r % per))
        else:
            b_spec = pl.BlockSpec((tj, tr), lambda i, j, r: (j, r))
    else:
        b_spec = pl.BlockSpec((tr, tj), lambda i, j, r: (r, j))
    if out_nblk > 1:
        pero = out_inner // tj
        o_spec = pl.BlockSpec((None, ti, tj), lambda i, j, r: (j // pero, i, j % pero))
        o_shape = (out_nblk, I, out_inner)
    else:
        o_spec = pl.BlockSpec((ti, tj), lambda i, j, r: (i, j))
        o_shape = (I, J)
    in_specs = [a_spec, b_spec]
    args = [a, b]
    if epi == "mul2act":
        in_specs.append(pl.BlockSpec((ti, tj), lambda i, j, r: (i, j)))
        args.append(epi_in)
    if epi == "relu2":
        out_shape = (jax.ShapeDtypeStruct(o_shape, BF16), jax.ShapeDtypeStruct(o_shape, BF16))
        out_specs = (o_spec, o_spec)
    else:
        out_shape = jax.ShapeDtypeStruct(o_shape, out_dtype)
        out_specs = o_spec
    n_in = len(args)
    n_out = 2 if epi == "relu2" else 1

    def body(*refs):
        a_ref, b_ref = refs[0], refs[1]
        outs = refs[n_in:n_in + n_out]
        acc_ref = refs[n_in + n_out] if nr > 1 else None

        def finish(acc):
            if epi == "relu2":
                act = jnp.maximum(acc, 0.0)
                outs[0][...] = (act * act).astype(BF16)
                outs[1][...] = act.astype(BF16)
            elif epi == "mul2act":
                outs[0][...] = (acc * (2.0 * refs[2][...].astype(F32))).astype(out_dtype)
            else:
                outs[0][...] = acc.astype(out_dtype)

        part = _dot(a_ref[...], b_ref[...], dn)
        if nr == 1:
            finish(part)
        else:
            r = pl.program_id(2)

            @pl.when(r == 0)
            def _():
                acc_ref[...] = part

            @pl.when(r > 0)
            def _():
                acc_ref[...] += part

            @pl.when(r == nr - 1)
            def _():
                finish(acc_ref[...])

    res, rider_res = _host_call(
        body, name=name, grid=(ni, nj, nr), in_specs=in_specs,
        out_specs=out_specs if n_out > 1 else [out_specs], out_shape=out_shape if n_out > 1 else [out_shape],
        scratch=[pltpu.VMEM((ti, tj), F32)] if nr > 1 else [], args=args,
        sem=("parallel", "parallel", "arbitrary"), rider=rider)
    res = tuple(res) if n_out > 1 else res[0]
    return res if rider is None else (res, rider_res)


def _rstd(x):
    return lax.rsqrt(jnp.mean(x * x, axis=-1, keepdims=True) + RMS_EPS)


def _rms_bwd_rows(x, w, dy):
    r = _rstd(x)
    xh = x * r
    dxh = dy * w
    dx = r * (dxh - xh * jnp.mean(dxh * xh, axis=-1, keepdims=True))
    return dx, jnp.sum(dy * xh, axis=0, keepdims=True)


def _row_spec(tb, d):
    return pl.BlockSpec((tb, d), lambda i: (i, 0))


def _vec_spec(d):
    return pl.BlockSpec((1, d), lambda i: (0, 0))


def _rms_fwd(h, w, name):
    T, D = h.shape
    tb = _tile(T, 512, SUBLANES)

    def body(h_ref, w_ref, u_ref):
        x = h_ref[...]
        u_ref[...] = (x * _rstd(x) * w_ref[...]).astype(BF16)

    return _pcall(body, name=name, grid=(T // tb,), in_specs=[_row_spec(tb, D), _vec_spec(D)],
                  out_specs=_row_spec(tb, D), out_shape=jax.ShapeDtypeStruct((T, D), BF16),
                  compiler_params=_cp(("parallel",)))(h, w)


def _resid_fwd(h, m, w_post, w_next, name):
    T, D = h.shape
    tb = _tile(T, 256, SUBLANES)

    def body(h_ref, m_ref, wp_ref, wn_ref, ho_ref, u_ref):
        x = m_ref[...]
        hn = h_ref[...] + x * _rstd(x) * wp_ref[...]
        ho_ref[...] = hn
        u_ref[...] = (hn * _rstd(hn) * wn_ref[...]).astype(BF16)

    return _pcall(body, name=name, grid=(T // tb,),
                  in_specs=[_row_spec(tb, D), _row_spec(tb, D), _vec_spec(D), _vec_spec(D)],
                  out_specs=(_row_spec(tb, D), _row_spec(tb, D)),
                  out_shape=(jax.ShapeDtypeStruct((T, D), F32), jax.ShapeDtypeStruct((T, D), BF16)),
                  compiler_params=_cp(("parallel",)))(h, m, w_post, w_next)


def _final_fwd(h, m, w_post, tgt, name):
    T, D = h.shape
    tb = _tile(T, 256, SUBLANES)

    def body(h_ref, m_ref, wp_ref, t_ref, g_ref, l_ref):
        x = m_ref[...]
        e = h_ref[...] + x * _rstd(x) * wp_ref[...] - t_ref[...]
        g_ref[...] = e * (1.0 / D)
        s = jnp.sum(e * e, axis=0, keepdims=True)

        @pl.when(pl.program_id(0) == 0)
        def _():
            l_ref[...] = s

        @pl.when(pl.program_id(0) > 0)
        def _():
            l_ref[...] += s

    return _pcall(body, name=name, grid=(T // tb,),
                  in_specs=[_row_spec(tb, D), _row_spec(tb, D), _vec_spec(D), _row_spec(tb, D)],
                  out_specs=(_row_spec(tb, D), _vec_spec(D)),
                  out_shape=(jax.ShapeDtypeStruct((T, D), F32), jax.ShapeDtypeStruct((1, D), F32)),
                  compiler_params=_cp(("arbitrary",)))(h, m, w_post, tgt)


def _norm_bwd(g_out, name, du=None, h=None, w_pre=None, m=None, w_post=None, rider=None):
    T, D = g_out.shape
    tb = _tile(T, 256, SUBLANES)
    has_pre, has_post = du is not None, m is not None
    args, in_specs = [g_out], [_row_spec(tb, D)]
    if has_pre:
        args += [du, h, w_pre]
        in_specs += [_row_spec(tb, D), _row_spec(tb, D), _vec_spec(D)]
    if has_post:
        args += [m, w_post]
        in_specs += [_row_spec(tb, D), _vec_spec(D)]
    out_shape, out_specs = [], []
    if has_pre:
        out_shape.append(jax.ShapeDtypeStruct((T, D), F32))
        out_specs.append(_row_spec(tb, D))
    if has_post:
        out_shape.append(jax.ShapeDtypeStruct((T, D), BF16))
        out_specs.append(_row_spec(tb, D))
    n_w = int(has_pre) + int(has_post)
    out_shape += [jax.ShapeDtypeStruct((1, D), F32)] * n_w
    out_specs += [_vec_spec(D)] * n_w
    n_in = len(args)

    def body(*refs):
        ins, outs = list(refs[:n_in]), list(refs[n_in:])
        g = ins.pop(0)[...]
        sums = []
        if has_pre:
            du_ref, h_ref, w_ref = ins.pop(0), ins.pop(0), ins.pop(0)
            dx, s = _rms_bwd_rows(h_ref[...], w_ref[...], du_ref[...])
            g = g + dx
            outs.pop(0)[...] = g
            sums.append(s)
        if has_post:
            m_ref, w_ref = ins.pop(0), ins.pop(0)
            dx, s = _rms_bwd_rows(m_ref[...], w_ref[...], g)
            outs.pop(0)[...] = dx.astype(BF16)
            sums.append(s)
        first = pl.program_id(0) == 0
        for o_ref, s in zip(outs, sums):
            @pl.when(first)
            def _(o_ref=o_ref, s=s):
                o_ref[...] = s

            @pl.when(jnp.logical_not(first))
            def _(o_ref=o_ref, s=s):
                o_ref[...] += s

    res, rider_res = _host_call(body, name=name, grid=(T // tb,), in_specs=in_specs, out_specs=out_specs,
                                out_shape=out_shape, scratch=[], args=args, sem=("arbitrary",), rider=rider)
    return tuple(res) if rider is None else (tuple(res), rider_res)


def _ret_consts(lg, L):
    ii = lax.broadcasted_iota(jnp.int32, (L, L), 0).astype(F32)
    jj = lax.broadcasted_iota(jnp.int32, (L, L), 1).astype(F32)
    dmat = jnp.exp(jnp.abs(ii - jj) * lg)
    idx = lax.broadcasted_iota(jnp.int32, (L, 1), 0).astype(F32)
    xi = jnp.exp((idx + 1.0) * lg)
    zeta = jnp.exp((L - 1.0 - idx) * lg)
    cd = jnp.exp(jnp.full((1, 1), L, F32) * lg)
    return dmat, xi, zeta, cd


def _rot(t, cs, sn):
    half = t.shape[-1] // 2
    t1, t2 = t[:, :half], t[:, half:]
    return jnp.concatenate([t1 * cs - t2 * sn, t1 * sn + t2 * cs], axis=-1)


def _rot_bwd(d, cs, sn):
    half = d.shape[-1] // 2
    d1, d2 = d[:, :half], d[:, half:]
    return jnp.concatenate([d1 * cs + d2 * sn, d2 * cs - d1 * sn], axis=-1)


def _ret_specs(cf, tb, rev):
    H, DK, DV = cf.H, cf.DK, cf.DV
    ni = cf.T // tb
    ri = (lambda i: ni - 1 - i) if rev else (lambda i: i)
    q = pl.BlockSpec((tb, DK), lambda h, i: (ri(i), h))
    k = pl.BlockSpec((tb, DK), lambda h, i: (ri(i), H + h))
    v = pl.BlockSpec((tb, DV), lambda h, i: (ri(i), cf.QK * 2 // DV + h))
    g = pl.BlockSpec((tb, DV), lambda h, i: (ri(i), cf.QK * 2 // DV + H + h))
    cs = pl.BlockSpec((tb, DK // 2), lambda h, i: (ri(i), 0))
    gw = pl.BlockSpec((1, DV), lambda h, i: (0, h))
    row_v = pl.BlockSpec((tb, DV), lambda h, i: (ri(i), h))
    row_k = pl.BlockSpec((tb, DK), lambda h, i: (ri(i), h))
    st = pl.BlockSpec((tb // cf.L, None, DK, DV), lambda h, i: (ri(i), h, 0, 0))
    lgs = pl.BlockSpec(memory_space=pltpu.SMEM)
    return q, k, v, g, cs, gw, row_v, row_k, st, lgs


def _ret_fwd(cf, proj, cos, sin, gn_w, lgam, rider=None):
    T, L, H, DK, DV = cf.T, cf.L, cf.H, cf.DK, cf.DV
    tb = _tile(T, 512, L)
    nck = tb // L
    q_s, k_s, v_s, g_s, cs_s, gw_s, row_v, _, st_s, lg_s = _ret_specs(cf, tb, False)
    kscale = DK ** -0.5

    def body(lg_ref, q_ref, k_ref, v_ref, g_ref, cos_ref, sin_ref, gw_ref, y_ref, o_ref, st_ref, state):
        h = pl.program_id(0)

        @pl.when(pl.program_id(1) == 0)
        def _():
            state[...] = jnp.zeros_like(state)

        dmat, xi, zeta, cd = _ret_consts(lg_ref[h], L)
        gw = gw_ref[...]

        def chunk(c, carry):
            rows = pl.ds(pl.multiple_of(c * L, L), L)
            cs, sn = cos_ref[rows, :], sin_ref[rows, :]
            qr = _rot(q_ref[rows, :], cs, sn)
            kr = _rot(k_ref[rows, :], cs, sn) * kscale
            qb, kb = qr.astype(BF16), kr.astype(BF16)
            vb = v_ref[rows, :].astype(BF16)
            st = state[...]
            stb = st.astype(BF16)
            st_ref[c] = stb
            s = _dot(qb, kb, NT) * dmat
            o = _dot(s.astype(BF16), vb) + _dot(qb, stb) * xi
            state[...] = st * cd + _dot((kr * zeta).astype(BF16), vb, TN)
            o_ref[rows, :] = o
            mu = jnp.mean(o, axis=-1, keepdims=True)
            oc = o - mu
            var = jnp.mean(oc * oc, axis=-1, keepdims=True)
            n = oc * lax.rsqrt(var + GN_EPS) * gw
            gt = g_ref[rows, :]
            y_ref[rows, :] = (gt * _sigmoid(gt) * n).astype(BF16)
            return carry

        lax.fori_loop(0, nck, chunk, 0, unroll=True)

    return _host_call(
        body, name="ret_fwd", grid=(H, T // tb),
        in_specs=[lg_s, q_s, k_s, v_s, g_s, cs_s, cs_s, gw_s],
        out_specs=(row_v, row_v, st_s),
        out_shape=(jax.ShapeDtypeStruct((T, cf.V), BF16), jax.ShapeDtypeStruct((T, cf.V), F32),
                   jax.ShapeDtypeStruct((cf.NC, H, DK, DV), BF16)),
        scratch=[pltpu.VMEM((DK, DV), F32)], args=(lgam, proj, proj, proj, proj, cos, sin, gn_w),
        sem=("parallel", "arbitrary"), rider=rider)


def _ret_bwd(cf, proj, cos, sin, gn_w, lgam, o, states, dy, rider=None):
    T, L, H, DK, DV = cf.T, cf.L, cf.H, cf.DK, cf.DV
    tb = _tile(T, 512, L)
    nck = tb // L
    q_s, k_s, v_s, g_s, cs_s, gw_s, row_v, row_k, st_s, lg_s = _ret_specs(cf, tb, True)
    kscale = DK ** -0.5

    ni = T // tb
    n_steps = H * ni
    col0 = (0, cf.QK, 2 * cf.QK, 2 * cf.QK + cf.V)
    widths = (DK, DK, DV, DV)

    def body(lg_ref, q_ref, k_ref, v_ref, g_ref, cos_ref, sin_ref, gw_ref, o_ref, st_ref, dy_ref,
             dp_ref, dgw_ref, dstate, dq_s, dk_s, dv_s, dg_s, sems):
        h = pl.program_id(0)
        step = h * ni + pl.program_id(1)
        slot = step % 2
        dq_ref, dk_ref, dv_ref, dg_ref = dq_s.at[slot], dk_s.at[slot], dv_s.at[slot], dg_s.at[slot]

        def results_out(s):
            hh, ii = s // ni, s % ni
            rows = pl.ds(pl.multiple_of((ni - 1 - ii) * tb, tb), tb)
            return [pltpu.make_async_copy(
                buf.at[s % 2], dp_ref.at[rows, pl.ds(pl.multiple_of(c0 + hh * w, LANES), w)], sems.at[s % 2, k])
                for k, (buf, c0, w) in enumerate(zip((dq_s, dk_s, dv_s, dg_s), col0, widths))]

        @pl.when(step >= 2)
        def _():
            for cp in results_out(step - 2):
                cp.wait()

        @pl.when(pl.program_id(1) == 0)
        def _():
            dstate[...] = jnp.zeros_like(dstate)
            dgw_ref[...] = jnp.zeros_like(dgw_ref)

        dmat, xi, zeta, cd = _ret_consts(lg_ref[h], L)
        gw = gw_ref[...]

        def chunk(t, carry):
            c = nck - 1 - t
            rows = pl.ds(pl.multiple_of(c * L, L), L)
            cs, sn = cos_ref[rows, :], sin_ref[rows, :]
            qr = _rot(q_ref[rows, :], cs, sn)
            kr = _rot(k_ref[rows, :], cs, sn) * kscale
            qb, kb = qr.astype(BF16), kr.astype(BF16)
            kzb = (kr * zeta).astype(BF16)
            vb = v_ref[rows, :].astype(BF16)
            s = (_dot(qb, kb, NT) * dmat).astype(BF16)
            oo = o_ref[rows, :]
            mu = jnp.mean(oo, axis=-1, keepdims=True)
            oc = oo - mu
            rstd = lax.rsqrt(jnp.mean(oc * oc, axis=-1, keepdims=True) + GN_EPS)
            oh = oc * rstd
            gt = g_ref[rows, :]
            sg = _sigmoid(gt)
            dyv = dy_ref[rows, :]
            dn = dyv * (gt * sg)
            dg_ref[rows, :] = (dyv * (oh * gw) * (sg * (1.0 + gt * (1.0 - sg)))).astype(BF16)
            dgw_ref[...] += jnp.sum(dn * oh, axis=0, keepdims=True)
            doh = dn * gw
            do = rstd * (doh - jnp.mean(doh, axis=-1, keepdims=True) - oh * jnp.mean(doh * oh, axis=-1, keepdims=True))
            dob = do.astype(BF16)
            doxb = (do * xi).astype(BF16)
            dst = dstate[...]
            dstb = dst.astype(BF16)
            stb = st_ref[c]
            dv_ref[rows, :] = (_dot(s, dob, TN) + _dot(kzb, dstb)).astype(BF16)
            ds = (_dot(dob, vb, NT) * dmat).astype(BF16)
            dqr = _dot(ds, kb) + _dot(doxb, stb, NT)
            dkr = _dot(ds, qb, TN) + _dot(vb, dstb, NT) * zeta
            dstate[...] = dst * cd + _dot(qb, doxb, TN)
            dq_ref[rows, :] = _rot_bwd(dqr, cs, sn).astype(BF16)
            dk_ref[rows, :] = _rot_bwd(dkr * kscale, cs, sn).astype(BF16)
            return carry

        lax.fori_loop(0, nck, chunk, 0, unroll=True)
        for cp in results_out(step):
            cp.start()

        @pl.when(step == n_steps - 1)
        def _():
            if n_steps > 1:
                for cp in results_out(step - 1):
                    cp.wait()
            for cp in results_out(step):
                cp.wait()

    return _host_call(
        body, name="ret_bwd", grid=(H, ni),
        in_specs=[lg_s, q_s, k_s, v_s, g_s, cs_s, cs_s, gw_s, row_v, st_s, row_v],
        out_specs=(ANY, gw_s),
        out_shape=(jax.ShapeDtypeStruct((T, cf.RIN), BF16), jax.ShapeDtypeStruct((1, cf.V), F32)),
        scratch=[pltpu.VMEM((DK, DV), F32), pltpu.VMEM((2, tb, DK), BF16), pltpu.VMEM((2, tb, DK), BF16),
                 pltpu.VMEM((2, tb, DV), BF16), pltpu.VMEM((2, tb, DV), BF16), pltpu.SemaphoreType.DMA((2, 4))],
        args=(lgam, proj, proj, proj, proj, cos, sin, gn_w, o, states, dy),
        sem=("arbitrary", "arbitrary"), rider=rider)


def _conv_pre(x, halo, w, b, first, W):
    tb = x.shape[0]
    ext = jnp.concatenate([jnp.where(first, 0.0, halo), x], axis=0)
    out = b + w[W - 1:W, :] * x
    for tap in range(W - 1):
        out = out + w[tap:tap + 1, :] * pltpu.roll(ext, W - 1 - tap, 0)[SUBLANES:SUBLANES + tb, :]
    return out, ext


def _conv_fwd(cf, proj, conv_w, conv_b):
    T, CD, W = cf.T, cf.CD, SSD_CONV_W
    tb = _tile(T, 512, SUBLANES)
    tc = _tile(CD, 512, LANES)
    off = cf.DI // tc
    nh = tb // SUBLANES

    def body(x_ref, halo_ref, w_ref, b_ref, o_ref):
        pre, _ = _conv_pre(x_ref[...], halo_ref[...], w_ref[...], b_ref[...], pl.program_id(1) == 0, W)
        o_ref[...] = pre * _sigmoid(pre)

    return _pcall(
        body, name="conv_fwd", grid=(CD // tc, T // tb),
        in_specs=[pl.BlockSpec((tb, tc), lambda j, i: (i, off + j)),
                  pl.BlockSpec((SUBLANES, tc), lambda j, i: (jnp.maximum(i * nh - 1, 0), off + j)),
                  pl.BlockSpec((W, tc), lambda j, i: (0, j)), pl.BlockSpec((1, tc), lambda j, i: (0, j))],
        out_specs=pl.BlockSpec((tb, tc), lambda j, i: (i, j)),
        out_shape=jax.ShapeDtypeStruct((T, CD), F32),
        compiler_params=_cp(("parallel", "arbitrary")),
    )(proj, proj, conv_w, conv_b)


def _conv_bwd_pre(cf, proj, conv_w, conv_b, dact):
    T, CD, W = cf.T, cf.CD, SSD_CONV_W
    tb = _tile(T, 512, SUBLANES)
    tc = _tile(CD, 512, LANES)
    off = cf.DI // tc
    nh = tb // SUBLANES

    def body(x_ref, halo_ref, w_ref, b_ref, da_ref, dp_ref, dw_ref, db_ref):
        x = x_ref[...]
        pre, ext = _conv_pre(x, halo_ref[...], w_ref[...], b_ref[...], pl.program_id(1) == 0, W)
        sg = _sigmoid(pre)
        dp = da_ref[...] * (sg * (1.0 + pre * (1.0 - sg)))
        dp_ref[...] = dp
        rows = [jnp.sum(dp * pltpu.roll(ext, W - 1 - tap, 0)[SUBLANES:SUBLANES + tb, :], axis=0, keepdims=True)
                for tap in range(W - 1)]
        rows.append(jnp.sum(dp * x, axis=0, keepdims=True))
        dw = jnp.concatenate(rows, axis=0)
        db = jnp.sum(dp, axis=0, keepdims=True)

        @pl.when(pl.program_id(1) == 0)
        def _():
            dw_ref[...] = dw
            db_ref[...] = db

        @pl.when(pl.program_id(1) > 0)
        def _():
            dw_ref[...] += dw
            db_ref[...] += db

    return _pcall(
        body, name="conv_bwd_pre", grid=(CD // tc, T // tb),
        in_specs=[pl.BlockSpec((tb, tc), lambda j, i: (i, off + j)),
                  pl.BlockSpec((SUBLANES, tc), lambda j, i: (jnp.maximum(i * nh - 1, 0), off + j)),
                  pl.BlockSpec((W, tc), lambda j, i: (0, j)), pl.BlockSpec((1, tc), lambda j, i: (0, j)),
                  pl.BlockSpec((tb, tc), lambda j, i: (i, j))],
        out_specs=(pl.BlockSpec((tb, tc), lambda j, i: (i, j)), pl.BlockSpec((W, tc), lambda j, i: (0, j)),
                   pl.BlockSpec((1, tc), lambda j, i: (0, j))),
        out_shape=(jax.ShapeDtypeStruct((T, CD), F32), jax.ShapeDtypeStruct((W, CD), F32),
                   jax.ShapeDtypeStruct((1, CD), F32)),
        compiler_params=_cp(("parallel", "arbitrary")),
    )(proj, proj, conv_w, conv_b, dact)


def _conv_bwd_x(cf, dpre, conv_w, into):
    T, CD, W = cf.T, cf.CD, SSD_CONV_W
    tb = _tile(T, 512, SUBLANES)
    tc = _tile(CD, 512, LANES)
    off = cf.DI // tc
    nh = tb // SUBLANES
    last_blk = T // SUBLANES - 1
    ni = T // tb

    def body(d_ref, halo_ref, w_ref, into_ref, o_ref):
        d = d_ref[...]
        w = w_ref[...]
        nxt = jnp.where(pl.program_id(1) == ni - 1, 0.0, halo_ref[...])
        ext = jnp.concatenate([d, nxt], axis=0)
        n = tb + SUBLANES
        out = w[W - 1:W, :] * d
        for tap in range(W - 1):
            out = out + w[tap:tap + 1, :] * pltpu.roll(ext, n - (W - 1 - tap), 0)[:tb, :]
        o_ref[...] = out.astype(BF16)

    return _pcall(
        body, name="conv_bwd_x", grid=(CD // tc, ni),
        in_specs=[pl.BlockSpec((tb, tc), lambda j, i: (i, j)),
                  pl.BlockSpec((SUBLANES, tc), lambda j, i: (jnp.minimum((i + 1) * nh, last_blk), j)),
                  pl.BlockSpec((W, tc), lambda j, i: (0, j)), ANY],
        out_specs=pl.BlockSpec((tb, tc), lambda j, i: (i, off + j)),
        out_shape=jax.ShapeDtypeStruct(into.shape, BF16), input_output_aliases={3: 0},
        compiler_params=_cp(("parallel", "arbitrary")),
    )(dpre, dpre, conv_w, into)


def _ssd_masks(cf, g, tb):
    L, GW, P, RG = cf.L, cf.GW, cf.P, cf.RG
    assert L == P and 2 * L == LANES and RG % 2 == 0
    i32 = jnp.int32
    hrow = lax.broadcasted_iota(i32, (LANES, GW), 0)
    hcol = lax.broadcasted_iota(i32, (LANES, GW), 1) // P
    expand = (hrow == g * RG + hcol).astype(BF16)
    ti = lax.broadcasted_iota(i32, (LANES, LANES), 0)
    tj = lax.broadcasted_iota(i32, (LANES, LANES), 1)
    btril = jnp.logical_and(ti // L == tj // L, ti >= tj).astype(BF16)
    r0 = lax.broadcasted_iota(i32, (L, GW), 0)
    c0 = lax.broadcasted_iota(i32, (L, GW), 1) % L
    tile_eye = (r0 == c0).astype(F32)
    lower = r0 >= c0
    p0 = lax.broadcasted_iota(i32, (2 * L, LANES), 0) // L
    p1 = lax.broadcasted_iota(i32, (2 * L, LANES), 1) // P
    pair = (p0 == p1).astype(F32)
    return expand, btril, tile_eye, lower, pair


def _softplus(x):
    return jnp.maximum(x, 0.0) + jnp.log1p(jnp.exp(-jnp.abs(x)))


def _split3(x):
    hi = x.astype(BF16)
    r1 = x - hi.astype(F32)
    mid = r1.astype(BF16)
    return hi, mid, (r1 - mid.astype(F32)).astype(BF16)


def _chunk_sums(btril, x, dn):
    hi, mid, lo = _split3(x)
    outs = []
    for k in range(x.shape[0] // LANES):
        sl = slice(k * LANES, (k + 1) * LANES)
        outs.append((_dot(btril, lo[sl], dn) + _dot(btril, mid[sl], dn)) + _dot(btril, hi[sl], dn))
    return jnp.concatenate(outs, axis=0)


def _expand_heads(x, expand, dn):
    hi, mid, lo = _split3(x)
    return (_dot(lo, expand, dn) + _dot(mid, expand, dn)) + _dot(hi, expand, dn)


def _ssd_chunk(cf, mk, acum, dt, xs, bm, cm):
    _, _, tile_eye, lower, pair = mk
    rowv = jnp.sum(acum * tile_eye, axis=0, keepdims=True)
    lf = jnp.exp(jnp.where(lower, acum - rowv, -1e30))
    xdt = xs * dt
    bb, cb_ = bm.astype(BF16), cm.astype(BF16)
    bb2 = jnp.concatenate([bb, bb], axis=0)
    cb2 = _dot(cb_, bb2, NT)
    ms, bds = [], []
    for j in range(cf.RG // 2):
        ln = slice(j * LANES, (j + 1) * LANES)
        ms.append((cb2 * lf[:, ln]).astype(BF16))
        xp = xdt[:, ln]
        bds.append((jnp.concatenate([xp, xp], axis=0) * pair).astype(BF16))
    return lf, xdt, bb, cb_, bb2, cb2, ms, bds


def _ssd_specs(cf, tb, rev):
    G, GW, N = cf.G, cf.GW, cf.N
    ni = cf.T // tb
    ri = (lambda i: ni - 1 - i) if rev else (lambda i: i)
    z = pl.BlockSpec((tb, GW), lambda g, i: (ri(i), g))
    dt = pl.BlockSpec((tb, LANES), lambda g, i: (ri(i), (cf.DI + cf.CD) // LANES))
    xs = pl.BlockSpec((tb, GW), lambda g, i: (ri(i), g))
    bm = pl.BlockSpec((tb, N), lambda g, i: (ri(i), cf.DI // N + g))
    cm = pl.BlockSpec((tb, N), lambda g, i: (ri(i), cf.DI // N + G + g))
    vec = pl.BlockSpec((1, GW), lambda g, i: (0, g))
    st = pl.BlockSpec((tb // cf.L, None, N, GW), lambda g, i: (ri(i), g, 0, 0))
    return z, dt, xs, bm, cm, vec, st


def _ssd_fwd(cf, proj, xact, bias_e, alog_e, dskip_e, norm_w):
    T, L, G, GW, N = cf.T, cf.L, cf.G, cf.GW, cf.N
    tb = _tile(T, 512, L)
    nck = tb // L
    z_s, dt_s, xs_s, b_s, c_s, vec_s, st_s = _ssd_specs(cf, tb, False)

    def body(z_ref, dt_ref, xs_ref, b_ref, c_ref, bias_ref, alog_ref, dsk_ref, nw_ref, y_ref, yp_ref, st_ref,
             state, dt_s, ac_s):
        @pl.when(pl.program_id(1) == 0)
        def _():
            state[...] = jnp.zeros_like(state)

        mk = _ssd_masks(cf, pl.program_id(0), tb)
        a_e = -jnp.exp(alog_ref[...])
        dt_all = _softplus(_expand_heads(dt_ref[...], mk[0], NN) + bias_ref[...])
        dt_s[...] = dt_all
        ac_s[...] = _chunk_sums(mk[1], dt_all * a_e, NN)

        def chunk(c, carry):
            rows = pl.ds(pl.multiple_of(c * L, L), L)
            acum = ac_s[rows, :]
            lf, xdt, bb, cb_, _, _, ms, bds = _ssd_chunk(cf, mk, acum, dt_s[rows, :], xs_ref[rows, :],
                                                         b_ref[rows, :], c_ref[rows, :])
            st = state[...]
            stb = st.astype(BF16)
            st_ref[c] = stb
            ydiag = jnp.concatenate([_dot(m, bd) for m, bd in zip(ms, bds)], axis=1)
            al = acum[L - 1:L, :]
            state[...] = st * jnp.exp(al) + _dot(bb, (xdt * jnp.exp(al - acum)).astype(BF16), TN)
            yp_ref[rows, :] = ydiag + _dot(cb_, stb) * jnp.exp(acum)
            return carry

        lax.fori_loop(0, nck, chunk, 0, unroll=True)
        z = z_ref[...]
        yg = (yp_ref[...] + dsk_ref[...] * xs_ref[...]) * (z * _sigmoid(z))
        y_ref[...] = (yg * _rstd(yg) * nw_ref[...]).astype(BF16)

    return _pcall(
        body, name="ssd_fwd", grid=(G, T // tb),
        in_specs=[z_s, dt_s, xs_s, b_s, c_s, vec_s, vec_s, vec_s, vec_s],
        out_specs=(z_s, z_s, st_s),
        out_shape=(jax.ShapeDtypeStruct((T, cf.DI), BF16), jax.ShapeDtypeStruct((T, cf.DI), F32),
                   jax.ShapeDtypeStruct((cf.NC, G, N, GW), BF16)),
        scratch_shapes=[pltpu.VMEM((N, GW), F32), pltpu.VMEM((tb, GW), F32), pltpu.VMEM((tb, GW), F32)],
        compiler_params=_cp(("parallel", "arbitrary")),
    )(proj, proj, xact, xact, xact, bias_e, alog_e, dskip_e, norm_w)


def _ssd_bwd(cf, proj, xact, bias_e, alog_e, dskip_e, norm_w, ypre, states, dy, rider=None):
    T, L, G, GW, N, RG = cf.T, cf.L, cf.G, cf.GW, cf.N, cf.RG
    tb = _tile(T, 512, L)
    nck = tb // L
    ni = T // tb
    z_s, dt_s, xs_s, b_s, c_s, vec_s, st_s = _ssd_specs(cf, tb, True)
    bc_out = pl.BlockSpec((tb, N), lambda g, i: (ni - 1 - i, g))
    ddt_out = pl.BlockSpec((None, tb, LANES), lambda g, i: (g, ni - 1 - i, 0))

    def body(z_ref, dt_ref, xs_ref, b_ref, c_ref, bias_ref, alog_ref, dsk_ref, nw_ref, yp_ref, st_ref, dy_ref,
             dz_ref, dxs_ref, db_ref, dc_ref, ddt_ref, dnw_ref, ddsk_ref, dalog_ref, dbias_ref,
             dstate, dt_s, ac_s, sg_s, dys_s, dxdt_s, dac_s):
        @pl.when(pl.program_id(1) == 0)
        def _():
            dstate[...] = jnp.zeros_like(dstate)
            for r in (dnw_ref, ddsk_ref, dalog_ref, dbias_ref):
                r[...] = jnp.zeros_like(r)

        mk = _ssd_masks(cf, pl.program_id(0), tb)
        expand, btril, tile_eye, lower, pair = mk
        a_e = -jnp.exp(alog_ref[...])
        dsk, nw = dsk_ref[...], nw_ref[...]
        last_row = (lax.broadcasted_iota(jnp.int32, (L, 1), 0) == L - 1).astype(F32)
        raw = _expand_heads(dt_ref[...], expand, NN) + bias_ref[...]
        dt_all = _softplus(raw)
        dt_s[...] = dt_all
        sg_s[...] = _sigmoid(raw)
        ac_s[...] = _chunk_sums(btril, dt_all * a_e, NN)
        z = z_ref[...]
        sz = _sigmoid(z)
        silu = z * sz
        xs_all = xs_ref[...]
        yd = yp_ref[...] + dsk * xs_all
        yg = yd * silu
        rr = _rstd(yg)
        xh = yg * rr
        dout = dy_ref[...]
        dnw_ref[...] += jnp.sum(dout * xh, axis=0, keepdims=True)
        dxh = dout * nw
        dyg = rr * (dxh - xh * jnp.mean(dxh * xh, axis=-1, keepdims=True))
        dz_ref[...] = (dyg * yd * (sz * (1.0 + z * (1.0 - sz)))).astype(BF16)
        dys_all = dyg * silu
        dys_s[...] = dys_all
        ddsk_ref[...] += jnp.sum(dys_all * xs_all, axis=0, keepdims=True)

        def chunk(t, carry):
            c = nck - 1 - t
            rows = pl.ds(pl.multiple_of(c * L, L), L)
            acum = ac_s[rows, :]
            lf, xdt, bb, cb_, bb2, cb2, ms, bds = _ssd_chunk(cf, mk, acum, dt_s[rows, :], xs_ref[rows, :],
                                                             b_ref[rows, :], c_ref[rows, :])
            stb = st_ref[c]
            eac = jnp.exp(acum)
            al = acum[L - 1:L, :]
            eal = jnp.exp(al)
            dte = jnp.exp(al - acum)
            dys = dys_s[rows, :]
            dyb = dys.astype(BF16)
            dms, dxs_, dsegs = [], [], []
            dcb2 = None
            for j in range(RG // 2):
                ln = slice(j * LANES, (j + 1) * LANES)
                dyj = dyb[:, ln]
                dbd = _dot(ms[j], dyj, TN) * pair
                dxs_.append(dbd[:L, :] + dbd[L:, :])
                tj = _dot(dyj, bds[j], NT) * lf[:, ln]
                dcb2 = tj if dcb2 is None else dcb2 + tj
                dsegs.append(tj * cb2)
            dxdt = jnp.concatenate(dxs_, axis=1)
            dseg = jnp.concatenate(dsegs, axis=1)
            dcb2 = dcb2.astype(BF16)
            dcm = _dot(dcb2, bb2)
            dbm2 = _dot(dcb2, cb_, TN)
            dbm = dbm2[:L, :] + dbm2[L:, :]
            dacum = dseg - tile_eye * jnp.sum(dseg, axis=0, keepdims=True)
            dyo = (dys * eac).astype(BF16)
            dcm = dcm + _dot(dyo, stb, NT)
            dacum = dacum + dys * _dot(cb_, stb) * eac
            dst = dstate[...]
            dstb = dst.astype(BF16)
            xd = xdt * dte
            dbm = dbm + _dot(xd.astype(BF16), dstb, NT)
            dxd = _dot(bb, dstb)
            dal = jnp.sum(dst * stb.astype(F32), axis=0, keepdims=True) * eal
            dxdt = dxdt + dxd * dte
            tt = dxd * xd
            dacum = dacum - tt + last_row * (dal + jnp.sum(tt, axis=0, keepdims=True))
            dstate[...] = dst * eal + _dot(cb_, dyo, TN)
            dxdt_s[rows, :] = dxdt
            dac_s[rows, :] = dacum
            db_ref[rows, :] = dbm
            dc_ref[rows, :] = dcm
            return carry

        lax.fori_loop(0, nck, chunk, 0, unroll=True)
        dda = _chunk_sums(btril, dac_s[...], TN)
        dxdt_all = dxdt_s[...]
        dt_all = dt_s[...]
        dxs_ref[...] = dys_s[...] * dsk + dxdt_all * dt_all
        ddt = dxdt_all * xs_ref[...] + dda * a_e
        dalog_ref[...] += jnp.sum(dda * dt_all, axis=0, keepdims=True) * a_e
        draw = ddt * sg_s[...]
        dbias_ref[...] += jnp.sum(draw, axis=0, keepdims=True)
        ddt_ref[...] = _expand_heads(draw, expand, NT)

    GN = G * N
    return _host_call(
        body, name="ssd_bwd", grid=(G, ni),
        in_specs=[z_s, dt_s, xs_s, b_s, c_s, vec_s, vec_s, vec_s, vec_s, z_s, st_s, z_s],
        out_specs=(z_s, z_s, bc_out, bc_out, ddt_out, vec_s, vec_s, vec_s, vec_s),
        out_shape=(jax.ShapeDtypeStruct((T, cf.SINP), BF16), jax.ShapeDtypeStruct((T, cf.CD), F32),
                   jax.ShapeDtypeStruct((T, GN), F32), jax.ShapeDtypeStruct((T, GN), F32),
                   jax.ShapeDtypeStruct((G, T, LANES), F32)) + (jax.ShapeDtypeStruct((1, cf.DI), F32),) * 4,
        scratch=[pltpu.VMEM((N, GW), F32)] + [pltpu.VMEM((tb, GW), F32)] * 6,
        args=(proj, proj, xact, xact, xact, bias_e, alog_e, dskip_e, norm_w, ypre, states, dy),
        sem=("parallel", "arbitrary"), rider=rider)


def _sum_groups(parts, name, into, col_blk):
    G, T, W = parts.shape
    tb = _tile(T, 512, SUBLANES)

    def body(p_ref, into_ref, o_ref):
        acc = p_ref[0]
        for g in range(1, G):
            acc = acc + p_ref[g]
        o_ref[...] = acc.astype(BF16)

    return _pcall(body, name=name, grid=(T // tb,), in_specs=[pl.BlockSpec((G, tb, W), lambda i: (0, i, 0)), ANY],
                  out_specs=pl.BlockSpec((tb, W), lambda i: (i, col_blk)),
                  out_shape=jax.ShapeDtypeStruct(into.shape, BF16), input_output_aliases={1: 0},
                  compiler_params=_cp(("parallel",)))(parts, into)


def _fill_bc(dact, dbm, dcm, name):
    T, GN = dbm.shape
    tb = _tile(T, 512, SUBLANES)
    blk = (dact.shape[1] - 2 * GN) // (2 * GN)
    assert blk * 2 * GN == dact.shape[1] - 2 * GN

    def body(b_ref, c_ref, into_ref, o_ref):
        o_ref[:, :GN] = b_ref[...]
        o_ref[:, GN:] = c_ref[...]

    row = pl.BlockSpec((tb, GN), lambda i: (i, 0))
    return _pcall(body, name=name, grid=(T // tb,), in_specs=[row, row, ANY],
                  out_specs=pl.BlockSpec((tb, 2 * GN), lambda i: (i, blk)),
                  out_shape=jax.ShapeDtypeStruct(dact.shape, F32), input_output_aliases={2: 0},
                  compiler_params=_cp(("parallel",)))(dbm, dcm, dact)


def _mm_gather_nt(a, w_shard, order, name, ti=1024, tj=512, late_rider=None):
    I, R = a.shape
    n = w_shard.shape[0]
    ti = _tile(I, ti, SUBLANES)
    tj = _tile(n, tj, LANES)
    per, ni = n // tj, I // ti
    lr = late_rider if late_rider is not None else _rider_join([])
    n_ra, n_ro = len(lr.args), len(lr.out_shape)

    def body(ord_ref, a_ref, w_ref, *refs):
        rins, (out_ref, gath_ref), routs = refs[:n_ra], refs[n_ra:n_ra + 2], refs[n_ra + 2:n_ra + 2 + n_ro]
        bbuf, tile_sems, send_sems, recv_sems, local_sem = refs[n_ra + 2 + n_ro:n_ra + 7 + n_ro]
        rsems = refs[n_ra + 7 + n_ro:]
        k, i, t = pl.program_id(0), pl.program_id(1), pl.program_id(2)
        x_, y_, c_ = _mesh_pos()
        me = 4 * x_ + 2 * y_ + c_
        sib = (x_, y_, 1 - c_)
        chips = [(1 - x_, y_), (x_, 1 - y_), (1 - x_, 1 - y_)]
        near = [sib] + [(cx, cy, c_) for cx, cy in chips]

        def send_mine(q):
            return _remote(w_ref, gath_ref.at[me], send_sems, recv_sems, q, near[q])

        def from_near(q):
            px, py, pc = near[q]
            return _remote(w_ref, gath_ref.at[4 * px + 2 * py + pc], send_sems, recv_sems, q, near[q])

        def pass_on(j, core):
            rows = gath_ref.at[4 * chips[j][0] + 2 * chips[j][1] + core]
            return _remote(rows, rows, send_sems, recv_sems, 4 + j, sib)

        def block_in(kk):
            return pltpu.make_async_copy(gath_ref.at[ord_ref[kk]], bbuf.at[kk % 2], tile_sems.at[kk % 2])

        mine = pltpu.make_async_copy(w_ref, gath_ref.at[me], local_sem)
        first_tile = t == 0

        @pl.when(jnp.logical_and(jnp.logical_and(k == 0, i == 0), first_tile))
        def _():
            mine.start()
            for q in range(4):
                send_mine(q).start()
            mine.wait()
            block_in(0).start()

        @pl.when(jnp.logical_and(i == 0, first_tile))
        def _():
            block_in(k).wait()

        @pl.when(jnp.logical_and(jnp.logical_and(i == ni - 1, first_tile), k + 1 < N_DEV))
        def _():
            kk = k + 1

            @pl.when(kk == 1)
            def _():
                from_near(0).wait_recv()

            for j in range(3):
                @pl.when(kk == 2 + j)
                def _(j=j):
                    from_near(1 + j).wait_recv()
                    pass_on(j, c_).start()

                @pl.when(kk == 5 + j)
                def _(j=j):
                    pass_on(j, 1 - c_).wait_recv()

            block_in(kk).start()

        if n_ra:
            @pl.when(jnp.logical_and(jnp.logical_and(k == 5, i == 0), first_tile))
            def _():
                _rider_start(lr, rins, routs, rsems)

        cols = pl.ds(pl.multiple_of(t * tj, tj), tj)
        out_ref[...] = _dot(a_ref[...], bbuf[k % 2, cols, :], NT)

        @pl.when(jnp.logical_and(jnp.logical_and(k == N_DEV - 1, i == ni - 1), t == per - 1))
        def _():
            for q in range(4):
                send_mine(q).wait_send()
            for j in range(3):
                pass_on(j, c_).wait_send()
            if n_ra:
                _rider_wait(lr, rins, routs, rsems)

    res = _pcall(
        body, name=name,
        grid_spec=pltpu.PrefetchScalarGridSpec(
            num_scalar_prefetch=1, grid=(N_DEV, ni, per),
            in_specs=[pl.BlockSpec((ti, R), lambda k, i, t, o: (i, 0)), ANY] + [ANY] * n_ra,
            out_specs=[pl.BlockSpec((ti, tj), lambda k, i, t, o: (i, o[k] * per + t)), ANY] + [ANY] * n_ro,
            scratch_shapes=[pltpu.VMEM((2, n, R), BF16), pltpu.SemaphoreType.DMA((2,)), pltpu.SemaphoreType.DMA((7,)),
                            pltpu.SemaphoreType.DMA((7,)), pltpu.SemaphoreType.DMA] + lr.sems),
        out_shape=[jax.ShapeDtypeStruct((I, N_DEV * n), F32), jax.ShapeDtypeStruct((N_DEV, n, R), BF16)] + lr.out_shape,
        input_output_aliases={3 + q: 2 + v for q, v in lr.aliases.items()},
        compiler_params=_cp(("arbitrary",) * 3),
    )(order, a, w_shard, *lr.args)
    return res[0], res[1], list(res[2:])


def _all_gather(x, name):
    def body(x_ref, out_ref, send_sems, recv_sems, local_sem):
        x, y, c = lax.axis_index("x"), lax.axis_index("y"), lax.axis_index("c")
        me, sibling = (x, y, c), (x, y, 1 - c)
        chips = [(1 - x, y), (x, 1 - y), (1 - x, 1 - y)]

        def blk(px, py, pc):
            return out_ref.at[4 * px + 2 * py + pc]

        def copy(k, block, to, src=None):
            return pltpu.make_async_remote_copy(
                src_ref=blk(*block) if src is None else src, dst_ref=blk(*block),
                send_sem=send_sems.at[k], recv_sem=recv_sems.at[k], device_id=to, device_id_type=MESH)

        mine = pltpu.make_async_copy(x_ref, blk(*me), local_sem)
        mine.start()
        first = [copy(0, me, sibling, src=x_ref)]
        first += [copy(1 + j, me, (*chip, c), src=x_ref) for j, chip in enumerate(chips)]
        for cp in first:
            cp.start()
        passed = [copy(4 + j, (*chip, c), sibling) for j, chip in enumerate(chips)]
        for j, chip in enumerate(chips):
            copy(1 + j, (*chip, c), me).wait_recv()
            passed[j].start()
        copy(0, sibling, me).wait_recv()
        for j, chip in enumerate(chips):
            copy(4 + j, (*chip, 1 - c), me).wait_recv()
        for cp in first + passed:
            cp.wait_send()
        mine.wait()

    return _pcall(
        body, name=name, in_specs=[ANY], out_specs=ANY,
        out_shape=jax.ShapeDtypeStruct((N_DEV,) + x.shape, x.dtype),
        scratch_shapes=[pltpu.SemaphoreType.DMA((7,)), pltpu.SemaphoreType.DMA((7,)), pltpu.SemaphoreType.DMA],
    )(x)


def _core_sum(g, recv, idx, name, rows=None):
    if rows is None:
        _, _, a, b = g.shape
    else:
        a, b = rows, g.shape[1]
    n, blk, at = _plane_tiles(a, b, 2 * 2**20)
    if rows is None:
        g_spec = pl.BlockSpec((None, None) + blk, lambda i, q, idx: (q, idx[0]) + at(i))
    else:
        per_blk = (a // blk[0], 0) if blk[1] == b else (1, 0)
        g_spec = pl.BlockSpec(blk, lambda i, q, idx: (at(i)[0] + (2 * q + idx[0]) * per_blk[0], at(i)[1]))

    def body(idx_ref, g_ref, r_ref, p_ref, own_ref):
        s = g_ref[...] + r_ref[...]
        p_ref[...] = s.astype(BF16)

        @pl.when(pl.program_id(1) == idx_ref[1])
        def _():
            own_ref[...] = s

    return _pcall(
        body, name=name,
        grid_spec=pltpu.PrefetchScalarGridSpec(
            num_scalar_prefetch=1, grid=(n, 4),
            in_specs=[g_spec, pl.BlockSpec((None,) + blk, lambda i, q, idx: (q,) + at(i))],
            out_specs=(pl.BlockSpec((None,) + blk, lambda i, q, idx: (q,) + at(i)),
                       pl.BlockSpec(blk, lambda i, q, idx: at(i)))),
        out_shape=(jax.ShapeDtypeStruct((4, a, b), BF16), jax.ShapeDtypeStruct((a, b), F32)),
        compiler_params=_cp(("parallel", "arbitrary")),
    )(idx, g, recv)


def _plane_tiles(a, b, f32_bytes, row_off=0):
    if a % (2 * SUBLANES) == 0:
        tr = _tile(a, max(2 * SUBLANES, f32_bytes // (4 * b) // (2 * SUBLANES) * (2 * SUBLANES)), 2 * SUBLANES)
        return a // tr, (tr, b), lambda i: (i + row_off * (a // tr), 0)
    assert row_off == 0
    tc = _tile(b, max(LANES, f32_bytes // (4 * a) // LANES * LANES), LANES)
    return b // tc, (a, tc), lambda i: (0, i)


def _adam_math(w, g, m, v):
    m = ADAM_B1 * m + (1.0 - ADAM_B1) * g
    v = ADAM_B2 * v + (1.0 - ADAM_B2) * (g * g)
    m_hat = m / (1.0 - ADAM_B1 ** ADAM_STEP)
    v_hat = v / (1.0 - ADAM_B2 ** ADAM_STEP)
    delta = -ADAM_LR * (m_hat / (jnp.sqrt(v_hat) + ADAM_EPS) + ADAM_WD * w)
    return delta, m, v


def _chip_sum_adam(own, recv, w, m, v, layer, name, into=None, row_part=(0, 1)):
    a, b = own.shape
    n = w.shape[0]
    assert w.shape[1] == a * row_part[1]
    nt, blk, at = _plane_tiles(a, b, 2**20)
    _, _, at_w = _plane_tiles(a, b, 2**20, row_part[0])
    wspec = pl.BlockSpec((None,) + blk, lambda i: (layer,) + at_w(i))
    ospec = pl.BlockSpec(blk, at)
    n_into = 0 if into is None else 4

    def body(own_ref, r_ref, w_ref, m_ref, v_ref, *rest):
        g_ref, d_ref, mo_ref, vo_ref = rest[n_into:]
        g = own_ref[...]
        for k in range(3):
            g = g + r_ref[k].astype(F32)
        g_ref[...] = g
        d_ref[...], mo_ref[...], vo_ref[...] = _adam_math(w_ref[...], g, m_ref[...], v_ref[...])

    return _pcall(
        body, name=name, grid=(nt,),
        in_specs=[ospec, pl.BlockSpec((3,) + blk, lambda i: (0,) + at(i)), wspec, wspec, wspec] + [ANY] * n_into,
        out_specs=(wspec,) * 4, out_shape=(jax.ShapeDtypeStruct(w.shape, F32),) * 4,
        input_output_aliases={5 + k: k for k in range(n_into)},
        compiler_params=_cp(("parallel",)),
    )(own, recv, w, m, v, *(into or ()))


def _all_reduce_small(x, n_fold, fold_w, name):
    R, W = x.shape

    def body(x_ref, out_ref, buf, send_sems, recv_sems):
        xx, y, c = lax.axis_index("x"), lax.axis_index("y"), lax.axis_index("c")
        me = 4 * xx + 2 * y + c
        buf[me] = x_ref[...]
        copies = []
        for k in range(1, N_DEV):
            px, py, pc = xx ^ (k >> 2), y ^ ((k >> 1) & 1), c ^ (k & 1)
            copies.append(pltpu.make_async_remote_copy(
                src_ref=x_ref, dst_ref=buf.at[me], send_sem=send_sems.at[k - 1], recv_sem=recv_sems.at[k - 1],
                device_id=(px, py, pc), device_id_type=MESH))
        for cp in copies:
            cp.start()
        for cp in copies:
            cp.wait()
        acc = buf[0]
        for j in range(1, N_DEV):
            acc = acc + buf[j]
        out_ref[...] = acc
        if n_fold:
            l0 = lax.broadcasted_iota(jnp.int32, (W, W), 0) // fold_w
            l1 = lax.broadcasted_iota(jnp.int32, (W, W), 1)
            fold = (l0 == l1).astype(F32)
            out_ref[R - n_fold:, :] = _dot(acc[R - n_fold:, :], fold, NN, HI)

    return _pcall(
        body, name=name, in_specs=[pl.BlockSpec(memory_space=pltpu.VMEM)],
        out_specs=pl.BlockSpec(memory_space=pltpu.VMEM), out_shape=jax.ShapeDtypeStruct((R, W), F32),
        scratch_shapes=[pltpu.VMEM((N_DEV, R, W), F32), pltpu.SemaphoreType.DMA((N_DEV - 1,)),
                        pltpu.SemaphoreType.DMA((N_DEV - 1,))],
        compiler_params=pltpu.CompilerParams(vmem_limit_bytes=VMEM_LIMIT),
    )(x)


def _adam_small(w, g, m, v, name):
    def body(w_ref, g_ref, m_ref, v_ref, d_ref, mo_ref, vo_ref):
        d_ref[...], mo_ref[...], vo_ref[...] = _adam_math(w_ref[...], g_ref[...], m_ref[...], v_ref[...])

    return _pcall(body, name=name, out_shape=(jax.ShapeDtypeStruct(w.shape, F32),) * 3)(w, g, m, v)


def kernel(x, norm_mix_pre, norm_mix_post, norm_ffn_pre, norm_ffn_post, ret_w_in, ret_gn_w, ret_w_out, ssd_w_in, ssd_conv_w, ssd_conv_b, ssd_dt_bias, ssd_a_log, ssd_d, ssd_norm_w, ssd_w_out, mlp_w_up, mlp_w_down, loss_target, m_norm_mix_pre, m_norm_mix_post, m_norm_ffn_pre, m_norm_ffn_post, m_ret_w_in, m_ret_gn_w, m_ret_w_out, m_ssd_w_in, m_ssd_conv_w, m_ssd_conv_b, m_ssd_dt_bias, m_ssd_a_log, m_ssd_d, m_ssd_norm_w, m_ssd_w_out, m_mlp_w_up, m_mlp_w_down, v_norm_mix_pre, v_norm_mix_post, v_norm_ffn_pre, v_norm_ffn_post, v_ret_w_in, v_ret_gn_w, v_ret_w_out, v_ssd_w_in, v_ssd_conv_w, v_ssd_conv_b, v_ssd_dt_bias, v_ssd_a_log, v_ssd_d, v_ssd_norm_w, v_ssd_w_out, v_mlp_w_up, v_mlp_w_down):
    cf = _cfg()
    T, D = cf.T, cf.D
    ax, ay, ac = lax.axis_index("x"), lax.axis_index("y"), lax.axis_index("c")
    my_dev = 4 * ax + 2 * ay + ac
    idx = jnp.stack([ac, 2 * ax + ay]).astype(jnp.int32)

    tr12 = lambda t: jnp.swapaxes(t, 1, 2)
    chips_of = lambda ws: _rider_join([_r_gather_chips(w.astype(BF16)) for w in ws])
    cores_of = lambda bufs: _rider_join([_r_gather_cores(b) for b in bufs])
    cw, cbw, nww = cf.CD // N_DEV, cf.CD // N_DEV, cf.DI // N_DEV
    small = jnp.concatenate([ssd_conv_w[0], ssd_conv_b, jnp.pad(ssd_norm_w, ((0, 0), (0, cw - nww))),
                             jnp.zeros((2, cw), F32)], axis=0)
    small = _all_gather(small, "ag_ssd_small")
    conv_w = jnp.transpose(small[:, :SSD_CONV_W, :], (1, 0, 2)).reshape(SSD_CONV_W, cf.CD)
    conv_b = small[:, SSD_CONV_W, :].reshape(1, cf.CD)
    ssd_nw = small[:, SSD_CONV_W + 1, :nww].reshape(1, cf.DI)

    half = cf.DK // 2
    inv_freq = ROPE_BASE ** (-jnp.arange(half, dtype=F32) / half)
    ang = jnp.arange(T).astype(F32)[:, None] * inv_freq[None, :]
    cos, sin = jnp.cos(ang), jnp.sin(ang)
    lgam = jnp.log1p(-jnp.exp2(-5.0 - jnp.arange(cf.H, dtype=F32)))
    rep = lambda p: jnp.repeat(p.reshape(1, cf.SH), cf.P, axis=1)
    bias_e, alog_e, dskip_e = rep(ssd_dt_bias), rep(ssd_a_log), rep(ssd_d)

    h0 = x.reshape(T, D)
    tgt = loss_target.reshape(T, D)
    nrm = lambda p, i: p[i:i + 1]

    u0 = _rms_fwd(h0, nrm(norm_mix_pre, 0), "rms_fwd0")
    order = jnp.stack([my_dev ^ mask for mask in (0, 1, 4, 2, 6, 5, 3, 7)]).astype(jnp.int32)
    proj0, w_ri, part_a = _mm_gather_nt(u0, ret_w_in[0].T.astype(BF16), order, "mm_ret_in",
                                        late_rider=chips_of([ret_w_out[0]]))
    w_ri = w_ri.reshape(cf.RIN, D)
    (y0, o0, st0), got = _ret_fwd(cf, proj0, cos, sin, ret_gn_w, lgam, rider=_rider_join(
        [cores_of(part_a), chips_of([mlp_w_up[0].T, mlp_w_down[0]])]))
    w_ro = got[0].reshape(cf.V, D)
    m0, got = _mm(y0, w_ro, kind="nn", name="mm_ret_out", tr=cf.V, rider=cores_of(got[1:]))
    w_up0, w_dn0 = got[0].reshape(cf.FF, D), got[1].reshape(cf.FF, D)
    h1, u1 = _resid_fwd(h0, m0, nrm(norm_mix_post, 0), nrm(norm_ffn_pre, 0), "resid_fwd0")
    (sq0, act0), part_b = _mm(u1, w_up0, kind="nt", name="mm_up0", epi="relu2", rider=chips_of([tr12(ssd_w_in)[0]]))
    f0, got = _mm(sq0, w_dn0, kind="nn", name="mm_down0", tj=1024, rider=_rider_join(
        [cores_of(part_b), chips_of([ssd_w_out[0], mlp_w_up[1].T])]))
    w_si = jnp.pad(got[0].reshape(cf.SIN, D), ((0, cf.SINP - cf.SIN), (0, 0)))
    h2, u2 = _resid_fwd(h1, f0, nrm(norm_ffn_post, 0), nrm(norm_mix_pre, 1), "resid_fwd1")
    proj1, got = _mm(u2, w_si, kind="nt", name="mm_ssd_in", tj=1152, rider=_rider_join(
        [cores_of(got[1:]), chips_of([mlp_w_down[1]])]))
    w_so, w_up1 = got[0].reshape(cf.DI, D), got[1].reshape(cf.FF, D)
    xact = _conv_fwd(cf, proj1, conv_w, conv_b)
    y1, yp1, st1 = _ssd_fwd(cf, proj1, xact, bias_e, alog_e, dskip_e, ssd_nw)
    m1, (w_dn1,) = _mm(y1, w_so, kind="nn", name="mm_ssd_out", tr=cf.DI, rider=cores_of(got[2:]))
    w_up, w_dn = [w_up0, w_up1], [w_dn0, w_dn1.reshape(cf.FF, D)]
    h3, u3 = _resid_fwd(h2, m1, nrm(norm_mix_post, 1), nrm(norm_ffn_pre, 1), "resid_fwd2")
    sq1, act1 = _mm(u3, w_up[1], kind="nt", name="mm_up1", epi="relu2")
    f1 = _mm(sq1, w_dn[1], kind="nn", name="mm_down1", tj=1024)
    g4, lsum = _final_fwd(h3, f1, nrm(norm_ffn_post, 1), tgt, "final_fwd")
    loss = lax.psum(0.5 * jnp.sum(lsum) / D, ("x", "y", "c"))

    as4 = lambda g: g.reshape(4, 2, g.shape[1], g.shape[2])
    swap_cores = lambda g: _r_swap_cores(as4(g))
    core_sum = lambda g, recv, name: _core_sum(as4(g), recv, idx, name + "_core_sum")

    def mlp_bwd(l, df, u, sq, act, rider=None, then=None):
        dpre = _mm(df, w_dn[l], kind="nt", name=f"mm_dpre{l}", out_dtype=BF16, epi="mul2act", epi_in=act, rider=rider)
        dpre, got = dpre if rider is not None else (dpre, [])
        g_dn = _mm(sq, df, kind="tn", name=f"mm_gdown{l}", tj=1024, tr=4096, rider=then(got) if then else None)
        g_dn, got = g_dn if then else (g_dn, [])
        g_dn = g_dn.reshape(N_DEV, cf.FF // N_DEV, D)
        du, (rc,) = _mm(dpre, w_up[l], kind="nn", name=f"mm_du_mlp{l}", tj=1024, tr=4096, rider=swap_cores(g_dn))
        part, own = core_sum(g_dn, rc, f"rs_mlp_down{l}")
        g_up, (r2,) = _mm(u, dpre, kind="tn", name=f"mm_gup{l}", tj=1024, tr=4096, out_nblk=N_DEV, rider=_r_swap_chips(part))
        return du, g_up, own, r2, got

    df1, g_nfpost1 = _norm_bwd(g4, "norm_bwd4", m=f1, w_post=nrm(norm_ffn_post, 1))
    du3, g_up1, own_dn1, r2_dn1, _ = mlp_bwd(1, df1, u3, sq1, act1, None)
    gh3, dm1, g_nfpre1, g_nmpost1 = _norm_bwd(g4, "norm_bwd3", du=du3, h=h3, w_pre=nrm(norm_ffn_pre, 1),
                                              m=m1, w_post=nrm(norm_mix_post, 1))
    dy1, (rc,) = _mm(dm1, w_so, kind="nt", name="mm_dy_ssd", rider=swap_cores(g_up1))
    part, own_up1 = core_sum(g_up1, rc, "rs_mlp_up1")
    g_so = _mm(y1, dm1, kind="tn", name="mm_g_ssd_out", tj=1024, tr=4096).reshape(N_DEV, cf.DI // N_DEV, D)
    (dz, dxs, dbm, dcm, ddt_parts, g_ssd_nw, g_dskip_e, g_alog_e, g_bias_e), (r2_up1, rc) = _ssd_bwd(
        cf, proj1, xact, bias_e, alog_e, dskip_e, ssd_nw, yp1, st1, dy1,
        rider=_rider_join([_r_swap_chips(part), swap_cores(g_so)]))
    part, own_so = core_sum(g_so, rc, "rs_ssd_w_out")
    dact = _fill_bc(dxs, dbm, dcm, "ssd_dact_fill")
    dpre1, g_conv_w, g_conv_b = _conv_bwd_pre(cf, proj1, conv_w, conv_b, dact)
    assert cf.SINP == cf.DI + cf.CD + LANES
    dproj1 = _conv_bwd_x(cf, dpre1, conv_w, dz)
    dproj1 = _sum_groups(ddt_parts, "ssd_ddt_sum", dproj1, (cf.DI + cf.CD) // LANES)
    du2, (r2_so,) = _mm(dproj1, w_si, kind="nn", name="mm_du_ssd", tj=1024, tr=3456, rider=_r_swap_chips(part))
    g_si = _mm(dproj1, u2, kind="tn", name="mm_g_ssd_in", ti=1152, tj=1024, tr=2048)
    sin8 = cf.SIN // N_DEV
    gh2, df0, g_nmpre1, g_nfpost0 = _norm_bwd(gh3, "norm_bwd2", du=du2, h=h2, w_pre=nrm(norm_mix_pre, 1),
                                              m=f0, w_post=nrm(norm_ffn_post, 0))
    own_si = []

    def si_chips(got):
        part, own = _core_sum(g_si, got[0], idx, "rs_ssd_w_in_core_sum", rows=sin8)
        own_si.append(own)
        return _r_swap_chips(part)

    du1, g_up0, own_dn0, r2_dn0, (r2_si,) = mlp_bwd(0, df0, u1, sq0, act0, _r_swap_cores(g_si, rows=sin8), si_chips)
    own_si = own_si[0]
    gh1, dm0, g_nfpre0, g_nmpost0 = _norm_bwd(gh2, "norm_bwd1", du=du1, h=h1, w_pre=nrm(norm_ffn_pre, 0),
                                              m=m0, w_post=nrm(norm_mix_post, 0))
    dy0, (rc,) = _mm(dm0, w_ro, kind="nt", name="mm_dy_ret", rider=swap_cores(g_up0))
    part, own_up0 = core_sum(g_up0, rc, "rs_mlp_up0")
    g_ro = _mm(y0, dm0, kind="tn", name="mm_g_ret_out", tj=1024, tr=4096).reshape(N_DEV, cf.V // N_DEV, D)
    (dproj0, g_gn), (r2_up0, rc) = _ret_bwd(cf, proj0, cos, sin, ret_gn_w, lgam, o0, st0, dy0,
                                            rider=_rider_join([_r_swap_chips(part), swap_cores(g_ro)]))
    part, own_ro = core_sum(g_ro, rc, "rs_ret_w_out")
    rin8 = cf.RIN // N_DEV
    g_ri0, (r2_ro,) = _mm(u0, dproj0, kind="tn", name="mm_g_ret_in0", ti=512, tj=rin8, tr=4096, out_nblk=N_DEV,
                          i_part=(0, 2), rider=_r_swap_chips(part))
    g_ri1, (rc,) = _mm(u0, dproj0, kind="tn", name="mm_g_ret_in1", ti=512, tj=rin8, tr=4096, out_nblk=N_DEV,
                       i_part=(1, 2), rider=swap_cores(g_ri0))
    part, own_ri0 = core_sum(g_ri0, rc, "rs_ret_w_in0")
    du0, (r2_ri0, rc) = _mm(dproj0, w_ri, kind="nn", name="mm_du_ret", tj=1024, tr=4096,
                            rider=_rider_join([_r_swap_chips(part), swap_cores(g_ri1)]))
    part, own_ri1 = core_sum(g_ri1, rc, "rs_ret_w_in1")
    (grad_x, g_nmpre0), (r2_ri1,) = _norm_bwd(gh1, "norm_bwd0", du=du0, h=h0, w_pre=nrm(norm_mix_pre, 0),
                                              rider=_r_swap_chips(part))

    g_nmpre = jnp.concatenate([g_nmpre0, g_nmpre1], axis=0)
    g_nmpost = jnp.concatenate([g_nmpost0, g_nmpost1], axis=0)
    g_nfpre = jnp.concatenate([g_nfpre0, g_nfpre1], axis=0)
    g_nfpost = jnp.concatenate([g_nfpost0, g_nfpost1], axis=0)
    segs = [g_nmpre, g_nmpost, g_nfpre, g_nfpost, g_gn, g_conv_w, g_conv_b, g_ssd_nw, g_bias_e, g_alog_e, g_dskip_e]
    flat = jnp.concatenate([s.reshape(-1, LANES) for s in segs], axis=0)
    n_fold = 3 * cf.DI // LANES
    red = _all_reduce_small(flat, n_fold, cf.P, "all_reduce_small")
    outs, r0 = [], 0
    for s in segs:
        nr = s.size // LANES
        outs.append(red[r0:r0 + nr])
        r0 += nr
    (g_nmpre, g_nmpost, g_nfpre, g_nfpost) = [o.reshape(DEPTH, D) for o in outs[:4]]
    g_gn = outs[4].reshape(1, cf.V)
    g_conv_w = lax.dynamic_slice_in_dim(outs[5].reshape(SSD_CONV_W, cf.CD), my_dev * cw, cw, axis=1)[None]
    g_conv_b = lax.dynamic_slice_in_dim(outs[6].reshape(1, cf.CD), my_dev * cbw, cbw, axis=1)
    g_ssd_nw = lax.dynamic_slice_in_dim(outs[7].reshape(1, cf.DI), my_dev * nww, nww, axis=1)
    per_row = LANES // cf.P
    g_bias, g_alog, g_dskip = [o[:, :per_row].reshape(1, cf.SH) for o in outs[8:]]

    def rs(own, recv, w, m, v, layer, name, into=None, row_part=(0, 1)):
        return _chip_sum_adam(own, recv, w, m, v, layer, name + "_adam", into, row_part)

    r_ri = rs(own_ri0, r2_ri0, ret_w_in, m_ret_w_in, v_ret_w_in, 0, "rs_ret_w_in0", None, (0, 2))
    r_ri = rs(own_ri1, r2_ri1, ret_w_in, m_ret_w_in, v_ret_w_in, 0, "rs_ret_w_in1", r_ri, (1, 2))
    r_ro = rs(own_ro, r2_ro, ret_w_out, m_ret_w_out, v_ret_w_out, 0, "rs_ret_w_out")
    r_si = [tr12(r) for r in rs(own_si, r2_si, tr12(ssd_w_in), tr12(m_ssd_w_in), tr12(v_ssd_w_in), 0, "rs_ssd_w_in")]
    r_so = rs(own_so, r2_so, ssd_w_out, m_ssd_w_out, v_ssd_w_out, 0, "rs_ssd_w_out")
    r_up = rs(own_up1, r2_up1, mlp_w_up, m_mlp_w_up, v_mlp_w_up, 1, "rs_mlp_up1")
    r_up = rs(own_up0, r2_up0, mlp_w_up, m_mlp_w_up, v_mlp_w_up, 0, "rs_mlp_up0", r_up)
    r_dn = rs(own_dn1, r2_dn1, mlp_w_down, m_mlp_w_down, v_mlp_w_down, 1, "rs_mlp_down1")
    r_dn = rs(own_dn0, r2_dn0, mlp_w_down, m_mlp_w_down, v_mlp_w_down, 0, "rs_mlp_down0", r_dn)
    lead = list

    def small_adam(w, g, m, v, name):
        return [g] + list(_adam_small(w, g, m, v, name))

    results = {
        "norm_mix_pre": small_adam(norm_mix_pre, g_nmpre, m_norm_mix_pre, v_norm_mix_pre, "adam_nmpre"),
        "norm_mix_post": small_adam(norm_mix_post, g_nmpost, m_norm_mix_post, v_norm_mix_post, "adam_nmpost"),
        "norm_ffn_pre": small_adam(norm_ffn_pre, g_nfpre, m_norm_ffn_pre, v_norm_ffn_pre, "adam_nfpre"),
        "norm_ffn_post": small_adam(norm_ffn_post, g_nfpost, m_norm_ffn_post, v_norm_ffn_post, "adam_nfpost"),
        "ret_w_in": lead(r_ri),
        "ret_gn_w": small_adam(ret_gn_w, g_gn, m_ret_gn_w, v_ret_gn_w, "adam_gn"),
        "ret_w_out": lead(r_ro),
        "ssd_w_in": lead(r_si),
        "ssd_conv_w": small_adam(ssd_conv_w, g_conv_w, m_ssd_conv_w, v_ssd_conv_w, "adam_conv_w"),
        "ssd_conv_b": small_adam(ssd_conv_b, g_conv_b, m_ssd_conv_b, v_ssd_conv_b, "adam_conv_b"),
        "ssd_dt_bias": small_adam(ssd_dt_bias, g_bias, m_ssd_dt_bias, v_ssd_dt_bias, "adam_dt_bias"),
        "ssd_a_log": small_adam(ssd_a_log, g_alog, m_ssd_a_log, v_ssd_a_log, "adam_a_log"),
        "ssd_d": small_adam(ssd_d, g_dskip, m_ssd_d, v_ssd_d, "adam_d"),
        "ssd_norm_w": small_adam(ssd_norm_w, g_ssd_nw, m_ssd_norm_w, v_ssd_norm_w, "adam_ssd_nw"),
        "ssd_w_out": lead(r_so),
        "mlp_w_up": r_up,
        "mlp_w_down": r_dn,
    }
    names = list(results)
    out = [loss, grad_x.reshape(1, T, D)]
    for k in range(4):
        out += [results[n][k] for n in names]
    return tuple(out)
```
